```python
import jax, jax.numpy as jnp
from jax import lax
import numpy as np

D_MODEL = 1024
BATCH = 32
SEQ = 256
DEPTH = 1
DEC_BATCH = 8
DEC_SEQ = 4096
PAST_LEN = 256

F32 = jnp.float32
GRID_W = 64
EPS = 1e-6
NEG = -1e30
M_HEADS = 4
M_WIDTH = D_MODEL // 2
M_DH = M_WIDTH // M_HEADS
M_CHUNK = 64
A_HEADS = 8
A_KV = 2
A_REP = A_HEADS // A_KV
A_DH = (D_MODEL // 2) // A_HEADS
WINDOW = 128
BLOCK = 128
ROPE_BASE = 10000.0
IN_SIZES = (M_WIDTH, M_WIDTH, M_WIDTH, M_WIDTH, 4 * M_HEADS, A_HEADS * A_DH, A_KV * A_DH, A_KV * A_DH)
IN_COLS = 4 * M_WIDTH + 4 * M_HEADS + (A_HEADS + 2 * A_KV) * A_DH
N_KEYS = 128
N_EXPERTS = N_KEYS * N_KEYS
P_HEADS = 8
P_DKEY = 256
P_HALF = P_DKEY // 2
P_TOPK = 16
P_TOKBLK = 128

kernel_name = 'hybrid_mlstm_swa_peer_diffusion_step'


def rmsnorm(x, g):
    xf = x.astype(F32)
    y = xf * lax.rsqrt(jnp.mean(xf * xf, axis=-1, keepdims=True) + EPS)
    return (y * g.astype(F32)).astype(x.dtype)


def modulate(h, shift, scale):
    return h * (1 + scale[:, None, :]) + shift[:, None, :]


def split_cols(z):
    outs, off = [], 0
    for s in IN_SIZES:
        outs.append(z[..., off:off + s])
        off += s
    return outs


def rope_2d(x):
    T = x.shape[1]
    rows = T // GRID_W
    row = jnp.repeat(jnp.arange(rows, dtype=F32), GRID_W)
    col = jnp.tile(jnp.arange(GRID_W, dtype=F32), rows)
    half = A_DH // 2
    quarter = half // 2
    inv = ROPE_BASE ** (-jnp.arange(quarter, dtype=F32) / quarter)
    xf = x.astype(F32)

    def rot(xa, pos):
        ang = pos[:, None] * inv[None, :]
        cos = jnp.cos(ang)[None, :, None, :]
        sin = jnp.sin(ang)[None, :, None, :]
        x1, x2 = xa[..., :quarter], xa[..., quarter:]
        return jnp.concatenate([x1 * cos - x2 * sin, x1 * sin + x2 * cos], axis=-1)

    return jnp.concatenate([rot(xf[..., :half], row), rot(xf[..., half:], col)], axis=-1).astype(x.dtype)


def mlstm_chunkwise(q, k, v, li, lf, C0, n0, m0):
    B, T, H, dh = q.shape
    nc = T // M_CHUNK

    def chunks(a):
        return jnp.moveaxis(a.reshape((B, nc, M_CHUNK) + a.shape[2:]), 1, 0)

    lower = jnp.tril(jnp.ones((M_CHUNK, M_CHUNK), dtype=bool))

    def step(carry, inp):
        C, n, m = carry
        qc, kc, vc, lic, lfc = inp
        b = jnp.swapaxes(jnp.cumsum(lfc, axis=1), 1, 2)
        ig = jnp.swapaxes(lic, 1, 2)
        a_inter = b + m[..., None]
        d = jnp.where(lower, b[..., :, None] - b[..., None, :] + ig[..., None, :], -jnp.inf)
        m_t = jnp.maximum(a_inter, jnp.max(d, axis=-1))
        w_inter = jnp.exp(a_inter - m_t)
        s = jnp.einsum('blhd,bshd->bhls', qc, kc) * jnp.exp(d - m_t[..., None])
        num = (jnp.einsum('bhls,bshv->blhv', s, vc)
               + jnp.swapaxes(w_inter, 1, 2)[..., None] * jnp.einsum('blhk,bhkv->blhv', qc, C))
        den = jnp.sum(s, axis=-1) + w_inter * jnp.einsum('blhk,bhk->bhl', qc, n)
        den = jnp.maximum(jnp.abs(den), jnp.exp(-m_t))
        h = num / jnp.swapaxes(den, 1, 2)[..., None]
        m_new = m_t[..., -1]
        g = jnp.exp(b[..., -1:] - b + ig - m_new[..., None])
        decay = jnp.exp(b[..., -1] + m - m_new)
        C_new = decay[..., None, None] * C + jnp.einsum('bhs,bshk,bshv->bhkv', g, kc, vc)
        n_new = decay[..., None] * n + jnp.einsum('bhs,bshk->bhk', g, kc)
        return (C_new, n_new, m_new), h

    (C, n, m), hs = lax.scan(step, (C0, n0, m0), tuple(chunks(a) for a in (q, k, v, li, lf)))
    return jnp.moveaxis(hs, 0, 1).reshape(B, T, H, dh), (C, n, m)


def mlstm_mixer(mq, mk, mv, mo, gt, b_gates, mh_g, st_f, st_b):
    B, T, _ = mq.shape

    def heads(a):
        return a.astype(F32).reshape(B, T, M_HEADS, M_DH)

    q = heads(mq) * (M_DH ** -0.5)
    k = heads(mk)
    v = heads(mv)
    g = (gt.astype(F32) + b_gates.astype(F32)).reshape(B, T, 4, M_HEADS)
    li_f, lf_f = g[:, :, 0], jax.nn.log_sigmoid(g[:, :, 1])
    li_b, lf_b = g[:, :, 2], jax.nn.log_sigmoid(g[:, :, 3])
    h_f, fin_f = mlstm_chunkwise(q, k, v, li_f, lf_f, *st_f)

    def rev(a):
        return jnp.flip(a, axis=1)

    h_b, fin_b = mlstm_chunkwise(rev(q), rev(k), rev(v), rev(li_b), rev(lf_b), *st_b)
    h = h_f + rev(h_b)
    h = h * lax.rsqrt(jnp.mean(h * h, axis=-1, keepdims=True) + EPS) * mh_g.astype(F32).reshape(M_HEADS, M_DH)
    out = jax.nn.sigmoid(mo.astype(F32)) * h.reshape(B, T, M_WIDTH)
    return out.astype(mq.dtype), fin_f, fin_b


def sink_attend(s, sink, vals):
    sk = sink.astype(F32).reshape(1, A_KV, A_REP, 1, 1)
    mx = jnp.maximum(jnp.max(s, axis=-1, keepdims=True), sk)
    p = jnp.exp(s - mx)
    den = jnp.sum(p, axis=-1, keepdims=True) + jnp.exp(sk - mx)
    return jnp.einsum('bgrqk,bkgd->bqgrd', p / den, vals.astype(F32))


def attn_context(q, k, v, sink):
    B, S = q.shape[:2]
    nb = S // BLOCK
    scale = A_DH ** -0.5
    qb = jnp.moveaxis(q.astype(F32).reshape(B, nb, BLOCK, A_KV, A_REP, A_DH), 1, 0)
    kf = k.astype(F32)

    def one(qblk):
        s = jnp.einsum('bqgrd,bkgd->bgrqk', qblk, kf) * scale
        return sink_attend(s, sink, v)

    out = lax.map(one, qb)
    return jnp.moveaxis(out, 0, 1).reshape(B, S, A_HEADS * A_DH)


def attn_latent(q, k, v, kc, vc, sink):
    B, T = q.shape[:2]
    nb = T // BLOCK
    scale = A_DH ** -0.5
    qb = jnp.moveaxis(q.astype(F32).reshape(B, nb, BLOCK, A_KV, A_REP, A_DH), 1, 0)
    pad = ((0, 0), (BLOCK, BLOCK), (0, 0), (0, 0))

    def windows(a):
        ap = jnp.pad(a.astype(F32), pad).reshape(B, nb + 2, BLOCK, A_KV, A_DH)
        w = jnp.concatenate([ap[:, :-2], ap[:, 1:-1], ap[:, 2:]], axis=2)
        return jnp.moveaxis(w, 1, 0)

    kw, vw = windows(k), windows(v)
    qpos = jnp.arange(nb)[:, None] * BLOCK + jnp.arange(BLOCK)[None, :]
    kpos = jnp.arange(nb)[:, None] * BLOCK - BLOCK + jnp.arange(3 * BLOCK)[None, :]
    mask = ((kpos[:, None, :] >= 0) & (kpos[:, None, :] < T)
            & (jnp.abs(kpos[:, None, :] - qpos[:, :, None]) <= WINDOW))
    kcf, vcf = kc.astype(F32), vc.astype(F32)

    def one(args):
        qblk, kblk, vblk, mblk = args
        s_c = jnp.einsum('bqgrd,bkgd->bgrqk', qblk, kcf) * scale
        s_w = jnp.where(mblk, jnp.einsum('bqgrd,bkgd->bgrqk', qblk, kblk) * scale, NEG)
        s = jnp.concatenate([s_c, s_w], axis=-1)
        vals = jnp.concatenate([vcf, vblk], axis=1)
        return sink_attend(s, sink, vals)

    out = lax.map(one, (qb, kw, vw, mask))
    return jnp.moveaxis(out, 0, 1).reshape(B, T, A_HEADS * A_DH)


def peer(h, w_q, sub_a, sub_b, u_tab, v_tab):
    B, T, D = h.shape
    N = B * T
    x = h.reshape(N, D)
    q = (x @ w_q).astype(F32).reshape(N, P_HEADS, P_DKEY)
    sa = jnp.einsum('nhd,kd->nhk', q[..., :P_HALF], sub_a.astype(F32))
    sb = jnp.einsum('nhd,kd->nhk', q[..., P_HALF:], sub_b.astype(F32))
    va, ia = lax.top_k(sa, P_TOPK)
    vb, ib = lax.top_k(sb, P_TOPK)
    cand = (va[..., :, None] + vb[..., None, :]).reshape(N, P_HEADS, P_TOPK * P_TOPK)
    cidx = (ia[..., :, None] * N_KEYS + ib[..., None, :]).reshape(N, P_HEADS, P_TOPK * P_TOPK)
    top, pos = lax.top_k(cand, P_TOPK)
    eidx = jnp.take_along_axis(cidx, pos, axis=-1)
    gates = jax.nn.softmax(top, axis=-1).astype(h.dtype)
    nblk = N // P_TOKBLK

    def blk(args):
        xb, eb, gb = args
        a = jax.nn.gelu(jnp.einsum('pd,phkd->phk', xb, u_tab[eb]))
        return jnp.einsum('phk,phkd->pd', gb * a, v_tab[eb])

    out = lax.map(blk, (x.reshape(nblk, P_TOKBLK, D),
                        eidx.reshape(nblk, P_TOKBLK, P_HEADS, P_TOPK),
                        gates.reshape(nblk, P_TOKBLK, P_HEADS, P_TOPK)))
    return out.reshape(B, T, D).astype(h.dtype)


def trunk_layer(x, cvec, prm, ctx_cache):
    (w_ada, b_ada, n1, n2, w_in, b_g, mh_g, qg, kg, sink, w_out, pwq, psa, psb, pu, pv) = prm
    B, T, _ = x.shape
    sh1, sc1, g1, sh2, sc2, g2 = jnp.split(jax.nn.silu(cvec) @ w_ada + b_ada, 6, axis=-1)
    h = modulate(rmsnorm(x, n1), sh1, sc1)
    mq, mk, mv, mo, gt, aq, ak, av = split_cols(h @ w_in)
    q = rmsnorm(aq.reshape(B, T, A_HEADS, A_DH), qg)
    k = rmsnorm(ak.reshape(B, T, A_KV, A_DH), kg)
    v = av.reshape(B, T, A_KV, A_DH)
    if ctx_cache is None:
        zero = (jnp.zeros((B, M_HEADS, M_DH, M_DH), F32), jnp.zeros((B, M_HEADS, M_DH), F32),
                jnp.full((B, M_HEADS), NEG, F32))
        m_out, fin_f, fin_b = mlstm_mixer(mq, mk, mv, mo, gt, b_g, mh_g, zero, zero)
        a_out = attn_context(q, k, v, sink)
        new = (k, v, jnp.stack([fin_f[0], fin_b[0]], axis=1), jnp.stack([fin_f[1], fin_b[1]], axis=1),
               jnp.stack([fin_f[2], fin_b[2]], axis=1))
    else:
        kc, vc, C0, n0, m0 = ctx_cache
        C0, n0, m0 = C0.astype(F32), n0.astype(F32), m0.astype(F32)
        m_out, _, _ = mlstm_mixer(mq, mk, mv, mo, gt, b_g, mh_g,
                                  (C0[:, 0], n0[:, 0], m0[:, 0]), (C0[:, 1], n0[:, 1], m0[:, 1]))
        a_out = attn_latent(rope_2d(q), rope_2d(k), v, kc, vc, sink)
        new = None
    mix = jnp.concatenate([m_out, a_out.astype(x.dtype)], axis=-1) @ w_out
    x = x + g1[:, None, :] * mix
    h2 = modulate(rmsnorm(x, n2), sh2, sc2)
    x = x + g2[:, None, :] * peer(h2, pwq, psa, psb, pu, pv)
    return x, new


def setup_inputs(seed: int = 0) -> dict:
    key = jax.random.key(seed)
    ks = jax.random.split(key, 28)

    def nrm(k, shape, s=1.0):
        return (s * jax.random.normal(k, shape)).astype(F32)

    gate_base = jnp.tile(jnp.repeat(jnp.array([0.0, 3.0], F32), M_HEADS), 2)
    return {
        'x_prompt': nrm(ks[0], (BATCH, SEQ, D_MODEL)),
        'x_sample': nrm(ks[1], (DEC_BATCH, DEC_SEQ, D_MODEL)),
        'c': nrm(ks[2], (DEC_BATCH, D_MODEL)),
        'cache_attn_k': nrm(ks[3], (DEC_BATCH, DEPTH, PAST_LEN, A_KV, A_DH)),
        'cache_attn_v': nrm(ks[4], (DEC_BATCH, DEPTH, PAST_LEN, A_KV, A_DH)),
        'state_mlstm_C': nrm(ks[5], (DEC_BATCH, DEPTH, 2, M_HEADS, M_DH, M_DH)),
        'state_mlstm_n': nrm(ks[6], (DEC_BATCH, DEPTH, 2, M_HEADS, M_DH)),
        'state_mlstm_m': nrm(ks[7], (DEC_BATCH, DEPTH, 2, M_HEADS)),
        'c_ctx': nrm(ks[8], (D_MODEL,)),
        'w_ada': nrm(ks[9], (DEPTH, D_MODEL, 6 * D_MODEL), D_MODEL ** -0.5),
        'b_ada': nrm(ks[10], (DEPTH, 6 * D_MODEL), 0.01),
        'norm1_g': 1.0 + nrm(ks[11], (DEPTH, D_MODEL), 0.05),
        'norm2_g': 1.0 + nrm(ks[12], (DEPTH, D_MODEL), 0.05),
        'w_in': nrm(ks[13], (DEPTH, D_MODEL, IN_COLS), D_MODEL ** -0.5),
        'b_gates': gate_base + nrm(ks[14], (DEPTH, 4 * M_HEADS), 0.5),
        'mh_norm_g': 1.0 + nrm(ks[15], (DEPTH, M_WIDTH), 0.05),
        'q_norm_g': 1.0 + nrm(ks[16], (DEPTH, A_DH), 0.05),
        'k_norm_g': 1.0 + nrm(ks[17], (DEPTH, A_DH), 0.05),
        'sink_logits': nrm(ks[18], (DEPTH, A_HEADS)),
        'w_out': nrm(ks[19], (DEPTH, D_MODEL, D_MODEL), D_MODEL ** -0.5),
        'peer_w_q': nrm(ks[20], (DEPTH, D_MODEL, P_HEADS * P_DKEY), D_MODEL ** -0.5),
        'peer_sub_a': nrm(ks[21], (DEPTH, N_KEYS, P_HALF), P_HALF ** -0.5),
        'peer_sub_b': nrm(ks[22], (DEPTH, N_KEYS, P_HALF), P_HALF ** -0.5),
        'peer_u': nrm(ks[23], (DEPTH, N_EXPERTS, D_MODEL), D_MODEL ** -0.5),
        'peer_v': nrm(ks[24], (DEPTH, N_EXPERTS, D_MODEL), P_HEADS ** -0.5),
    }


def reference(x_prompt, x_sample, c, cache_attn_k, cache_attn_v, state_mlstm_C, state_mlstm_n,
              state_mlstm_m, c_ctx, w_ada, b_ada, norm1_g, norm2_g, w_in, b_gates, mh_norm_g,
              q_norm_g, k_norm_g, sink_logits, w_out, peer_w_q, peer_sub_a, peer_sub_b, peer_u, peer_v):
    def layer_params(l):
        return (w_ada[l], b_ada[l], norm1_g[l], norm2_g[l], w_in[l], b_gates[l], mh_norm_g[l],
                q_norm_g[l], k_norm_g[l], sink_logits[l], w_out[l], peer_w_q[l], peer_sub_a[l],
                peer_sub_b[l], peer_u[l], peer_v[l])

    xp = x_prompt
    cp = jnp.broadcast_to(c_ctx, (xp.shape[0], D_MODEL))
    ks_, vs_, Cs, ns, ms = [], [], [], [], []
    for l in range(DEPTH):
        xp, (k_l, v_l, C_l, n_l, m_l) = trunk_layer(xp, cp, layer_params(l), None)
        ks_.append(k_l)
        vs_.append(v_l)
        Cs.append(C_l)
        ns.append(n_l)
        ms.append(m_l)

    xs = x_sample
    for l in range(DEPTH):
        xs, _ = trunk_layer(xs, c, layer_params(l),
                            (cache_attn_k[:, l], cache_attn_v[:, l], state_mlstm_C[:, l],
                             state_mlstm_n[:, l], state_mlstm_m[:, l]))

    return (xp, xs, jnp.stack(ks_, axis=1), jnp.stack(vs_, axis=1), jnp.stack(Cs, axis=1),
            jnp.stack(ns, axis=1), jnp.stack(ms, axis=1))
```

```python
import functools

import jax
import jax.numpy as jnp
from jax import lax
from jax.experimental import pallas as pl
from jax.experimental.pallas import tpu as pltpu

F32 = jnp.float32
BF16 = jnp.bfloat16
I32 = jnp.int32
HI = lax.Precision.HIGHEST

D_MODEL = 1024
EPS = 1e-6
NEG = -1e30
GRID_W = 64
M_HEADS = 4
M_WIDTH = 512
M_DH = 128
A_HEADS = 8
A_KV = 2
A_REP = 4
A_DH = 64
A_WIDTH = 512
KV_WIDTH = A_KV * A_DH
BLOCK = 128
ROPE_BASE = 10000.0
N_KEYS = 128
P_HEADS = 8
P_DKEY = 256
P_HALF = 128
P_TOPK = 16
N_SEL = P_HEADS * P_TOPK
N_GATES = 4 * M_HEADS
MAIN_COLS = 4 * M_WIDTH + A_WIDTH + 2 * KV_WIDTH
MOD_ROWS = 16

TOKEN_TILE = 256
MLSTM_CHUNK = 128
ADA_COL_TILE = 768


def _sigmoid(x):
    return 1.0 / (1.0 + jnp.exp(-x))


def _log_sigmoid(x):
    return jnp.minimum(x, 0.0) - jnp.log1p(jnp.exp(-jnp.abs(x)))


def _dot_t(a, b, precision=None):
    return lax.dot_general(a, b, (((1,), (1,)), ((), ())), precision=precision,
                           preferred_element_type=F32)


def _ada_kernel(c_ref, w_ref, b_ref, o_ref):
    c = c_ref[...]
    s = c * _sigmoid(c)
    o_ref[...] = jnp.dot(s, w_ref[...], precision=HI, preferred_element_type=F32) + b_ref[...]


def _ada_call(cond, w_ada, b_ada):
    n_out = w_ada.shape[1]
    return pl.pallas_call(
        _ada_kernel,
        grid=(n_out // ADA_COL_TILE,),
        in_specs=[pl.BlockSpec((MOD_ROWS, D_MODEL), lambda j: (0, 0)),
                  pl.BlockSpec((D_MODEL, ADA_COL_TILE), lambda j: (0, j)),
                  pl.BlockSpec((1, ADA_COL_TILE), lambda j: (0, j))],
        out_specs=pl.BlockSpec((MOD_ROWS, ADA_COL_TILE), lambda j: (0, j)),
        out_shape=jax.ShapeDtypeStruct((MOD_ROWS, n_out), F32),
        name="ada",
    )(cond, w_ada, b_ada.reshape(1, n_out))


def _swap16(x):
    n = x.shape[-1]
    lane = lax.broadcasted_iota(I32, x.shape, x.ndim - 1)
    return jnp.where((lane & 16) == 0, pltpu.roll(x, n - 16, x.ndim - 1), pltpu.roll(x, 16, x.ndim - 1))


def _inproj_kernel(x_ref, mod_ref, n1_ref, w_ref, wg_ref, wgt_ref, bg_ref, bgt_ref, qg_ref, kg_ref,
                   bd_ref, cos_ref, sin_ref,
                   mq_ref, mk_ref, mv_ref, mo_ref, gc_ref, gr_ref, aq_ref, ak_ref, av_ref):
    x = x_ref[...]
    h = x * lax.rsqrt(jnp.mean(x * x, axis=-1, keepdims=True) + EPS) * n1_ref[...]
    h = h * (1.0 + mod_ref[0, 1:2, :]) + mod_ref[0, 0:1, :]
    z = jnp.dot(h.astype(BF16), w_ref[...], preferred_element_type=F32)

    mq_ref[...] = (z[:, 0:M_WIDTH] * (M_DH ** -0.5)).astype(BF16)
    mk_ref[...] = z[:, M_WIDTH:2 * M_WIDTH].astype(BF16)
    mv_ref[...] = z[:, 2 * M_WIDTH:3 * M_WIDTH].astype(BF16)
    mo_ref[...] = z[:, 3 * M_WIDTH:4 * M_WIDTH]

    g = jnp.dot(h, wg_ref[...], precision=HI, preferred_element_type=F32) + bg_ref[...]
    kind = lax.broadcasted_iota(I32, g.shape, 1) // M_HEADS
    gc_ref[...] = jnp.where((kind & 1) == 1, _log_sigmoid(g), g)
    gt = _dot_t(wgt_ref[...], h, precision=HI) + bgt_ref[...]
    kind_t = lax.broadcasted_iota(I32, gt.shape, 0) // M_HEADS
    gr_ref[...] = jnp.where((kind_t & 1) == 1, _log_sigmoid(gt), gt)

    o = 4 * M_WIDTH
    aq = z[:, o:o + A_WIDTH]
    ak = z[:, o + A_WIDTH:o + A_WIDTH + KV_WIDTH]
    av_ref[...] = z[:, o + A_WIDTH + KV_WIDTH:o + A_WIDTH + 2 * KV_WIDTH]
    bd = bd_ref[...]
    cos = cos_ref[...]
    sin = sin_ref[...]
    aq = aq * lax.rsqrt(jnp.dot(aq * aq, bd, precision=HI, preferred_element_type=F32) + EPS) * qg_ref[...]
    cos4 = jnp.concatenate([cos] * (A_WIDTH // KV_WIDTH), axis=1)
    sin4 = jnp.concatenate([sin] * (A_WIDTH // KV_WIDTH), axis=1)
    aq = (aq * cos4 + _swap16(aq) * sin4) * (A_DH ** -0.5)
    ak = ak * lax.rsqrt(jnp.dot(ak * ak, bd[0:KV_WIDTH, 0:KV_WIDTH], precision=HI,
                                preferred_element_type=F32) + EPS) * kg_ref[...]
    ak_ref[...] = ak * cos + _swap16(ak) * sin

    lane = lax.broadcasted_iota(I32, (aq.shape[0], KV_WIDTH), 1)
    for hd in range(A_HEADS):
        grp = hd // A_REP
        blk = aq[:, (hd // 2) * KV_WIDTH:(hd // 2 + 1) * KV_WIDTH]
        if hd % 2 != grp:
            blk = pltpu.roll(blk, A_DH, 1)
        keep = (lane >= grp * A_DH) & (lane < (grp + 1) * A_DH)
        aq_ref[hd] = jnp.where(keep, blk, 0.0).astype(BF16)


def _inproj_call(x2d, seq, mod3, mod_row0, n1, w_main, w_g, w_gt, b_g, b_gt, qg_t, kg_t, bd, cos, sin):
    n = x2d.shape[0]
    tm = TOKEN_TILE
    per_seq = seq // tm

    def tok(i):
        return (i, 0)

    def const2(i):
        return (0, 0)

    in_specs = [
        pl.BlockSpec((tm, D_MODEL), tok),
        pl.BlockSpec((1, 6, D_MODEL), lambda i: (mod_row0 + (i // per_seq if mod_row0 else 0), 0, 0)),
        pl.BlockSpec((1, D_MODEL), const2),
        pl.BlockSpec((D_MODEL, MAIN_COLS), const2),
        pl.BlockSpec((D_MODEL, N_GATES), const2),
        pl.BlockSpec((N_GATES, D_MODEL), const2),
        pl.BlockSpec((1, N_GATES), const2),
        pl.BlockSpec((N_GATES, 1), const2),
        pl.BlockSpec((1, A_WIDTH), const2),
        pl.BlockSpec((1, KV_WIDTH), const2),
        pl.BlockSpec((A_WIDTH, A_WIDTH), const2),
        pl.BlockSpec((tm, KV_WIDTH), lambda i: (i % per_seq, 0)),
        pl.BlockSpec((tm, KV_WIDTH), lambda i: (i % per_seq, 0)),
    ]
    out_specs = [
        pl.BlockSpec((tm, M_WIDTH), tok),
        pl.BlockSpec((tm, M_WIDTH), tok),
        pl.BlockSpec((tm, M_WIDTH), tok),
        pl.BlockSpec((tm, M_WIDTH), tok),
        pl.BlockSpec((tm, N_GATES), tok),
        pl.BlockSpec((N_GATES, tm), lambda i: (0, i)),
        pl.BlockSpec((A_HEADS, tm, KV_WIDTH), lambda i: (0, i, 0)),
        pl.BlockSpec((tm, KV_WIDTH), tok),
        pl.BlockSpec((tm, KV_WIDTH), tok),
    ]
    out_shape = [
        jax.ShapeDtypeStruct((n, M_WIDTH), BF16),
        jax.ShapeDtypeStruct((n, M_WIDTH), BF16),
        jax.ShapeDtypeStruct((n, M_WIDTH), BF16),
        jax.ShapeDtypeStruct((n, M_WIDTH), F32),
        jax.ShapeDtypeStruct((n, N_GATES), F32),
        jax.ShapeDtypeStruct((N_GATES, n), F32),
        jax.ShapeDtypeStruct((A_HEADS, n, KV_WIDTH), BF16),
        jax.ShapeDtypeStruct((n, KV_WIDTH), F32),
        jax.ShapeDtypeStruct((n, KV_WIDTH), F32),
    ]
    return pl.pallas_call(
        _inproj_kernel, grid=(n // tm,), in_specs=in_specs, out_specs=out_specs, out_shape=out_shape,
        compiler_params=pltpu.CompilerParams(dimension_semantics=("parallel",)),
        name="inproj",
    )(x2d, mod3, n1, w_main, w_g, w_gt, b_g, b_gt, qg_t, kg_t, bd, cos, sin)


def _mlstm_chain(q, k, v, li_c, lf_c, li_r, lf_r, caug, m, tri, tri_t, mask, reverse):
    L = q.shape[0]
    last = 0 if reverse else L - 1
    b_c = jnp.dot(tri, jnp.broadcast_to(lf_c, (L, L)), precision=HI, preferred_element_type=F32)
    b_r = jnp.dot(jnp.broadcast_to(lf_r, (8, L)), tri_t, precision=HI, preferred_element_type=F32)[0:1, :]
    a_inter = b_c[:, 0:1] + m
    d = jnp.where(mask, b_c - b_r + li_r, -jnp.inf)
    m_t = jnp.maximum(a_inter, jnp.max(d, axis=1, keepdims=True))
    w_inter = jnp.exp(a_inter - m_t)
    s = _dot_t(q, k) * jnp.exp(d - m_t)
    qc = jnp.dot(q, caug.astype(BF16), preferred_element_type=F32)
    num = jnp.dot(s.astype(BF16), v, preferred_element_type=F32) + w_inter * qc[:, 0:M_DH]
    den = jnp.sum(s, axis=1, keepdims=True) + w_inter * qc[:, M_DH:M_DH + 1]
    den = jnp.maximum(jnp.abs(den), jnp.exp(-m_t))
    h = num / den
    m_new = m_t[last:last + 1, :]
    b_last = b_c[last:last + 1, 0:1]
    g_c = jnp.exp(b_last - b_c[:, 0:1] + li_c - m_new)
    decay = jnp.exp(b_last + m - m_new)
    kw = (k.astype(F32) * g_c).astype(BF16)
    vaug = jnp.concatenate([v, jnp.ones_like(v)], axis=1)
    upd = lax.dot_general(kw, vaug, (((0,), (0,)), ((), ())), preferred_element_type=F32)
    return h, decay * caug + upd, m_new


def _mlstm_kernel(qf_ref, kf_ref, vf_ref, gcf_ref, grf_ref, qb_ref, kb_ref, vb_ref, gcb_ref, grb_ref,
                  c0_ref, m0_ref, hf_ref, hb_ref, cfin_ref, mfin_ref, c_scr, m_scr):
    c = pl.program_id(1)
    nc = pl.num_programs(1)
    L = qf_ref.shape[0]

    @pl.when(c == 0)
    def _():
        c_scr[...] = c0_ref[0]
        m_scr[...] = m0_ref[0]

    row = lax.broadcasted_iota(I32, (L, L), 0)
    col = lax.broadcasted_iota(I32, (L, L), 1)
    lower = row >= col
    upper = row <= col
    lower_f = lower.astype(F32)
    upper_f = upper.astype(F32)

    for direction in range(2):
        reverse = direction == 1
        q_ref, k_ref, v_ref, gc_ref, gr_ref, h_ref = (
            (qb_ref, kb_ref, vb_ref, gcb_ref, grb_ref, hb_ref) if reverse
            else (qf_ref, kf_ref, vf_ref, gcf_ref, grf_ref, hf_ref))
        tri, tri_t, mask = (upper_f, lower_f, upper) if reverse else (lower_f, upper_f, lower)
        gc = gc_ref[...]
        gr = gr_ref[...]
        for hd in range(M_HEADS):
            ch = direction * M_HEADS + hd
            sl = slice(hd * M_DH, (hd + 1) * M_DH)
            ci = 2 * direction * M_HEADS + hd
            cf = ci + M_HEADS
            h, caug, m_new = _mlstm_chain(
                q_ref[:, sl], k_ref[:, sl], v_ref[:, sl],
                gc[:, ci:ci + 1], gc[:, cf:cf + 1], gr[ci:ci + 1, :], gr[cf:cf + 1, :],
                c_scr[ch], m_scr[ch][0:1, 0:1], tri, tri_t, mask, reverse)
            h_ref[:, sl] = h
            c_scr[ch] = caug
            m_scr[ch] = jnp.broadcast_to(m_new, m_scr.shape[1:])

    @pl.when(c == nc - 1)
    def _():
        cfin_ref[0] = c_scr[...]
        mfin_ref[0] = m_scr[...]


def _mlstm_call(mq, mk, mv, gcol, grow, c0, m0, batch, seq):
    n = mq.shape[0]
    L = MLSTM_CHUNK
    nc = seq // L
    n_ch = 2 * M_HEADS

    def fwd(b, c):
        return (b * nc + c, 0)

    def bwd(b, c):
        return (b * nc + nc - 1 - c, 0)

    def fwd_t(b, c):
        return (0, b * nc + c)

    def bwd_t(b, c):
        return (0, b * nc + nc - 1 - c)

    tok = pl.BlockSpec((L, M_WIDTH), fwd)
    tok_b = pl.BlockSpec((L, M_WIDTH), bwd)
    in_specs = [tok, tok, tok, pl.BlockSpec((L, N_GATES), fwd), pl.BlockSpec((N_GATES, L), fwd_t),
                tok_b, tok_b, tok_b, pl.BlockSpec((L, N_GATES), bwd), pl.BlockSpec((N_GATES, L), bwd_t),
                pl.BlockSpec((1, n_ch, M_DH, 2 * M_DH), lambda b, c: (b, 0, 0, 0)),
                pl.BlockSpec((1, n_ch, 8, M_DH), lambda b, c: (b, 0, 0, 0))]
    out_specs = [tok, tok_b,
                 pl.BlockSpec((1, n_ch, M_DH, 2 * M_DH), lambda b, c: (b, 0, 0, 0)),
                 pl.BlockSpec((1, n_ch, 8, M_DH), lambda b, c: (b, 0, 0, 0))]
    out_shape = [jax.ShapeDtypeStruct((n, M_WIDTH), F32), jax.ShapeDtypeStruct((n, M_WIDTH), F32),
                 jax.ShapeDtypeStruct((batch, n_ch, M_DH, 2 * M_DH), F32),
                 jax.ShapeDtypeStruct((batch, n_ch, 8, M_DH), F32)]
    return pl.pallas_call(
        _mlstm_kernel, grid=(batch, nc), in_specs=in_specs, out_specs=out_specs, out_shape=out_shape,
        scratch_shapes=[pltpu.VMEM((n_ch, M_DH, 2 * M_DH), F32), pltpu.VMEM((n_ch, 8, M_DH), F32)],
        compiler_params=pltpu.CompilerParams(dimension_semantics=("parallel", "arbitrary")),
        name="mlstm",
    )(mq, mk, mv, gcol, grow, mq, mk, mv, gcol, grow, c0, m0)


def _sink_column(sink_ref, grp, rows_per_head):
    return jnp.concatenate(
        [jnp.full((rows_per_head, 1), sink_ref[grp * A_REP + r], F32) for r in range(A_REP)], axis=0)


def _store_heads(out_ref, o, grp, rows_per_head):
    for r in range(A_REP):
        hd = grp * A_REP + r
        out_ref[:, hd * A_DH:(hd + 1) * A_DH] = o[r * rows_per_head:(r + 1) * rows_per_head,
                                                  grp * A_DH:(grp + 1) * A_DH].astype(out_ref.dtype)


def _attn_ctx_kernel(sink_ref, q_ref, k_ref, v_ref, out_ref):
    s_len = k_ref.shape[0]
    k = k_ref[...].astype(BF16)
    v = v_ref[...].astype(BF16)
    for grp in range(A_KV):
        q = q_ref[grp * A_REP:(grp + 1) * A_REP].reshape(A_REP * s_len, KV_WIDTH)
        s = _dot_t(q, k)
        sk = _sink_column(sink_ref, grp, s_len)
        mx = jnp.maximum(jnp.max(s, axis=1, keepdims=True), sk)
        p = jnp.exp(s - mx)
        den = jnp.sum(p, axis=1, keepdims=True) + jnp.exp(sk - mx)
        o = jnp.dot(p.astype(BF16), v, preferred_element_type=F32) / den
        _store_heads(out_ref, o, grp, s_len)


def _attn_ctx_call(sink, aq, ak, av, batch, seq):
    n = ak.shape[0]
    return pl.pallas_call(
        _attn_ctx_kernel, grid=(batch,),
        in_specs=[pl.BlockSpec(memory_space=pltpu.SMEM),
                  pl.BlockSpec((A_HEADS, seq, KV_WIDTH), lambda b: (0, b, 0)),
                  pl.BlockSpec((seq, KV_WIDTH), lambda b: (b, 0)),
                  pl.BlockSpec((seq, KV_WIDTH), lambda b: (b, 0))],
        out_specs=pl.BlockSpec((seq, A_WIDTH), lambda b: (b, 0)),
        out_shape=jax.ShapeDtypeStruct((n, A_WIDTH), BF16),
        compiler_params=pltpu.CompilerParams(dimension_semantics=("parallel",)),
        name="attn_ctx",
    )(sink, aq, ak, av)


def _attn_lat_kernel(sink_ref, q_ref, kc_ref, vc_ref, kp_ref, kq_ref, kn_ref, vp_ref, vq_ref, vn_ref, out_ref):
    i = pl.program_id(1)
    nb = pl.num_programs(1)
    kc = kc_ref[0].astype(BF16)
    vc = vc_ref[0].astype(BF16)
    kp, kq, kn = kp_ref[...].astype(BF16), kq_ref[...].astype(BF16), kn_ref[...].astype(BF16)
    vp, vq, vn = vp_ref[...].astype(BF16), vq_ref[...].astype(BF16), vn_ref[...].astype(BF16)
    rows = A_REP * BLOCK
    qpos = lax.broadcasted_iota(I32, (rows, BLOCK), 0) % BLOCK
    kpos = lax.broadcasted_iota(I32, (rows, BLOCK), 1)
    mask_p = (kpos >= qpos) & (i > 0)
    mask_n = (kpos <= qpos) & (i < nb - 1)
    for grp in range(A_KV):
        q = q_ref[grp * A_REP:(grp + 1) * A_REP].reshape(rows, KV_WIDTH)
        s_c = _dot_t(q, kc)
        s_p = jnp.where(mask_p, _dot_t(q, kp), NEG)
        s_q = _dot_t(q, kq)
        s_n = jnp.where(mask_n, _dot_t(q, kn), NEG)
        sk = _sink_column(sink_ref, grp, BLOCK)
        mx = jnp.maximum(jnp.maximum(jnp.max(s_c, axis=1, keepdims=True), jnp.max(s_p, axis=1, keepdims=True)),
                         jnp.maximum(jnp.max(s_q, axis=1, keepdims=True), jnp.max(s_n, axis=1, keepdims=True)))
        mx = jnp.maximum(mx, sk)
        p_c, p_p, p_q, p_n = jnp.exp(s_c - mx), jnp.exp(s_p - mx), jnp.exp(s_q - mx), jnp.exp(s_n - mx)
        den = (jnp.sum(p_c, axis=1, keepdims=True) + jnp.sum(p_p, axis=1, keepdims=True)
               + jnp.sum(p_q, axis=1, keepdims=True) + jnp.sum(p_n, axis=1, keepdims=True) + jnp.exp(sk - mx))
        o = (jnp.dot(p_c.astype(BF16), vc, preferred_element_type=F32)
             + jnp.dot(p_p.astype(BF16), vp, preferred_element_type=F32)
             + jnp.dot(p_q.astype(BF16), vq, preferred_element_type=F32)
             + jnp.dot(p_n.astype(BF16), vn, preferred_element_type=F32)) / den
        _store_heads(out_ref, o, grp, BLOCK)


def _attn_lat_call(sink, aq, ak, av, kc, vc, batch, seq):
    n = ak.shape[0]
    nb = seq // BLOCK
    past = kc.shape[1]

    def cur(b, i):
        return (b * nb + i, 0)

    def prev(b, i):
        return (b * nb + jnp.maximum(i - 1, 0), 0)

    def nxt(b, i):
        return (b * nb + jnp.minimum(i + 1, nb - 1), 0)

    blk = functools.partial(pl.BlockSpec, (BLOCK, KV_WIDTH))
    cache = pl.BlockSpec((1, past, KV_WIDTH), lambda b, i: (b, 0, 0))
    return pl.pallas_call(
        _attn_lat_kernel, grid=(batch, nb),
        in_specs=[pl.BlockSpec(memory_space=pltpu.SMEM),
                  pl.BlockSpec((A_HEADS, BLOCK, KV_WIDTH), lambda b, i: (0, b * nb + i, 0)),
                  cache, cache, blk(prev), blk(cur), blk(nxt), blk(prev), blk(cur), blk(nxt)],
        out_specs=pl.BlockSpec((BLOCK, A_WIDTH), cur),
        out_shape=jax.ShapeDtypeStruct((n, A_WIDTH), BF16),
        compiler_params=pltpu.CompilerParams(dimension_semantics=("parallel", "parallel")),
        name="attn_lat",
    )(sink, aq, kc, vc, ak, ak, ak, av, av, av)


def _top16_rows(s, payload=None):
    n_rows = s.shape[0]
    rows = lax.broadcasted_iota(I32, s.shape, 0).astype(F32)
    vals, idxs, pays = [], [], []
    for _ in range(P_TOPK):
        mx = jnp.max(s, axis=0, keepdims=True)
        ix = jnp.min(jnp.where(s == mx, rows, float(n_rows)), axis=0, keepdims=True)
        hit = rows == ix
        vals.append(mx)
        idxs.append(ix)
        if payload is not None:
            pays.append(jnp.sum(jnp.where(hit, payload, 0.0), axis=0, keepdims=True))
        s = jnp.where(hit, -jnp.inf, s)
    out = (jnp.concatenate(vals, axis=0), jnp.concatenate(idxs, axis=0))
    if payload is not None:
        out += (jnp.concatenate(pays, axis=0),)
    return out


def _mix_kernel(x_ref, hf_ref, hb_ref, mo_ref, ao_ref, mod_ref, mhg_ref, n2_ref, wm_ref, wa_ref, wq_ref,
                sa_ref, sb_ref, x1_ref, h2_ref, eidx_ref, gate_ref, qp_scr, e_scr, g_scr):
    tm = x_ref.shape[0]
    hs = hf_ref[...] + hb_ref[...]
    parts = []
    for hd in range(M_HEADS):
        blk = hs[:, hd * M_DH:(hd + 1) * M_DH]
        parts.append(blk * lax.rsqrt(jnp.mean(blk * blk, axis=-1, keepdims=True) + EPS))
    m_out = _sigmoid(mo_ref[...]) * (jnp.concatenate(parts, axis=1) * mhg_ref[...])
    mix = (jnp.dot(m_out.astype(BF16), wm_ref[...], preferred_element_type=F32)
           + jnp.dot(ao_ref[...], wa_ref[...], preferred_element_type=F32))
    x1 = x_ref[...] + mod_ref[0, 2:3, :] * mix
    x1_ref[...] = x1
    h2 = x1 * lax.rsqrt(jnp.mean(x1 * x1, axis=-1, keepdims=True) + EPS) * n2_ref[...]
    h2 = h2 * (1.0 + mod_ref[0, 4:5, :]) + mod_ref[0, 3:4, :]
    h2_ref[...] = h2
    qp = jnp.dot(h2.astype(BF16), wq_ref[...], preferred_element_type=F32)
    for p in range(P_HEADS):
        qp_scr[p] = qp[:, p * P_DKEY:(p + 1) * P_DKEY].astype(BF16)
    sub_a = sa_ref[...]
    sub_b = sb_ref[...]

    def head_body(p, carry):
        for half in range(tm // N_KEYS):
            cols = slice(half * N_KEYS, (half + 1) * N_KEYS)
            qh = qp_scr[p, pl.ds(half * N_KEYS, N_KEYS), :]
            s_a = _dot_t(sub_a, qh[:, 0:P_HALF])
            s_b = _dot_t(sub_b, qh[:, P_HALF:P_DKEY])
            va, ia = _top16_rows(s_a)
            vb, ib = _top16_rows(s_b)
            cand = jnp.concatenate([va[i:i + 1, :] + vb for i in range(P_TOPK)], axis=0)
            cidx = jnp.concatenate([ia[i:i + 1, :] * float(N_KEYS) + ib for i in range(P_TOPK)], axis=0)
            top, _, eidx = _top16_rows(cand, cidx)
            ex = jnp.exp(top - jnp.max(top, axis=0, keepdims=True))
            gates = ex / jnp.sum(ex, axis=0, keepdims=True)
            r0 = pl.multiple_of(p * P_TOPK, P_TOPK)
            e_scr[pl.ds(r0, P_TOPK), cols] = eidx
            g_scr[pl.ds(r0, P_TOPK), cols] = gates
        return carry

    lax.fori_loop(0, P_HEADS, head_body, 0)
    for half in range(tm // N_KEYS):
        cols = slice(half * N_KEYS, (half + 1) * N_KEYS)
        eidx_ref[cols, :] = e_scr[:, cols].T.astype(I32)
        gate_ref[cols, :] = g_scr[:, cols].T


def _mix_call(x2d, seq, h_f, h_b, mo, a_out, mod3, mod_row0, mhg, n2, w_m, w_a, w_q, sub_a, sub_b):
    n = x2d.shape[0]
    tm = TOKEN_TILE
    per_seq = seq // tm

    def tok(i):
        return (i, 0)

    def const2(i):
        return (0, 0)

    in_specs = [
        pl.BlockSpec((tm, D_MODEL), tok),
        pl.BlockSpec((tm, M_WIDTH), tok), pl.BlockSpec((tm, M_WIDTH), tok), pl.BlockSpec((tm, M_WIDTH), tok),
        pl.BlockSpec((tm, A_WIDTH), tok),
        pl.BlockSpec((1, 6, D_MODEL), lambda i: (mod_row0 + (i // per_seq if mod_row0 else 0), 0, 0)),
        pl.BlockSpec((1, M_WIDTH), const2),
        pl.BlockSpec((1, D_MODEL), const2),
        pl.BlockSpec((M_WIDTH, D_MODEL), const2),
        pl.BlockSpec((A_WIDTH, D_MODEL), const2),
        pl.BlockSpec((D_MODEL, P_HEADS * P_DKEY), const2),
        pl.BlockSpec((N_KEYS, P_HALF), const2),
        pl.BlockSpec((N_KEYS, P_HALF), const2),
    ]
    out_specs = [pl.BlockSpec((tm, D_MODEL), tok), pl.BlockSpec((tm, D_MODEL), tok),
                 pl.BlockSpec((tm, N_SEL), tok), pl.BlockSpec((tm, N_SEL), tok)]
    out_shape = [jax.ShapeDtypeStruct((n, D_MODEL), F32), jax.ShapeDtypeStruct((n, D_MODEL), F32),
                 jax.ShapeDtypeStruct((n, N_SEL), I32), jax.ShapeDtypeStruct((n, N_SEL), F32)]
    return pl.pallas_call(
        _mix_kernel, grid=(n // tm,), in_specs=in_specs, out_specs=out_specs, out_shape=out_shape,
        scratch_shapes=[pltpu.VMEM((P_HEADS, tm, P_DKEY), BF16), pltpu.VMEM((N_SEL, tm), F32),
                        pltpu.VMEM((N_SEL, tm), F32)],
        compiler_params=pltpu.CompilerParams(dimension_semantics=("parallel",)),
        name="mix",
    )(x2d, h_f, h_b, mo, a_out, mod3, mhg, n2, w_m, w_a, w_q, sub_a, sub_b)


def _peer_experts(h2, eidx, gates, u_tab, v_tab):
    def blk(args):
        xb, eb, gb = args
        a = jax.nn.gelu(jnp.einsum('pd,pkd->pk', xb, u_tab[eb]))
        return jnp.einsum('pk,pkd->pd', gb * a, v_tab[eb])
    n = h2.shape[0]
    nb = n // 128
    out = lax.map(blk, (h2.reshape(nb, 128, D_MODEL), eidx.reshape(nb, 128, N_SEL), gates.reshape(nb, 128, N_SEL)))
    return out.reshape(n, D_MODEL)


def _resid_kernel(x1_ref, p_ref, mod_ref, o_ref):
    o_ref[...] = x1_ref[...] + mod_ref[0, 5:6, :] * p_ref[...]


def _resid_call(x1, peer_out, seq, mod3, mod_row0):
    n = x1.shape[0]
    tm = TOKEN_TILE
    per_seq = seq // tm
    tok = pl.BlockSpec((tm, D_MODEL), lambda i: (i, 0))
    return pl.pallas_call(
        _resid_kernel, grid=(n // tm,),
        in_specs=[tok, tok,
                  pl.BlockSpec((1, 6, D_MODEL), lambda i: (mod_row0 + (i // per_seq if mod_row0 else 0), 0, 0))],
        out_specs=tok, out_shape=jax.ShapeDtypeStruct((n, D_MODEL), F32),
        compiler_params=pltpu.CompilerParams(dimension_semantics=("parallel",)),
        name="resid",
    )(x1, peer_out, mod3)


def _rope_tables(seq, rotate):
    if not rotate:
        return jnp.ones((seq, KV_WIDTH), F32), jnp.zeros((seq, KV_WIDTH), F32)
    quarter = A_DH // 4
    t = jnp.arange(seq)
    row = (t // GRID_W).astype(F32)
    col = (t % GRID_W).astype(F32)
    inv = ROPE_BASE ** (-jnp.arange(quarter, dtype=F32) / quarter)
    d = jnp.arange(A_DH)
    pos = jnp.where(d[None, :] < A_DH // 2, row[:, None], col[:, None])
    ang = pos * inv[d % quarter][None, :]
    sign = jnp.where((d % (A_DH // 2)) < quarter, -1.0, 1.0).astype(F32)
    cos = jnp.cos(ang)
    sin = jnp.sin(ang) * sign[None, :]
    return jnp.tile(cos, (1, KV_WIDTH // A_DH)), jnp.tile(sin, (1, KV_WIDTH // A_DH))


def _layer(x, mod3, mod_row0, prm, cache, rotate):
    (n1, n2, w_main, w_g, w_gt, b_g, b_gt, mhg, qg_t, kg_t, bd, sink, w_m, w_a, w_q, sub_a, sub_b, u_tab, v_tab) = prm
    batch, seq, _ = x.shape
    n = batch * seq
    x2d = x.reshape(n, D_MODEL)
    cos, sin = _rope_tables(seq, rotate)
    mq, mk, mv, mo, gcol, grow, aq, ak, av = _inproj_call(
        x2d, seq, mod3, mod_row0, n1, w_main, w_g, w_gt, b_g, b_gt, qg_t, kg_t, bd, cos, sin)
    kc, vc, c0, m0 = cache
    h_f, h_b, c_fin, m_fin = _mlstm_call(mq, mk, mv, gcol, grow, c0, m0, batch, seq)
    if kc is None:
        a_out = _attn_ctx_call(sink, aq, ak, av, batch, seq)
    else:
        a_out = _attn_lat_call(sink, aq, ak, av, kc, vc, batch, seq)
    x1, h2, eidx, gates = _mix_call(x2d, seq, h_f, h_b, mo, a_out, mod3, mod_row0, mhg, n2, w_m, w_a, w_q,
                                    sub_a, sub_b)
    peer_out = _peer_experts(h2, eidx, gates, u_tab, v_tab)
    out = _resid_call(x1, peer_out, seq, mod3, mod_row0)
    return out.reshape(batch, seq, D_MODEL), ak, av, c_fin, m_fin


def _pack_state(C, n_vec, m):
    b = C.shape[0]
    caug = jnp.concatenate([C, jnp.broadcast_to(n_vec[..., None], C.shape)], axis=-1)
    caug = caug.reshape(b, 2 * M_HEADS, M_DH, 2 * M_DH)
    m_rep = jnp.broadcast_to(m.reshape(b, 2 * M_HEADS, 1, 1), (b, 2 * M_HEADS, 8, M_DH))
    return caug.astype(F32), m_rep.astype(F32)


def kernel(x_prompt, x_sample, c, cache_attn_k, cache_attn_v, state_mlstm_C, state_mlstm_n, state_mlstm_m,
           c_ctx, w_ada, b_ada, norm1_g, norm2_g, w_in, b_gates, mh_norm_g, q_norm_g, k_norm_g, sink_logits,
           w_out, peer_w_q, peer_sub_a, peer_sub_b, peer_u, peer_v):
    depth = w_ada.shape[0]
    assert depth == 1
    batch, seq, _ = x_prompt.shape
    dec_batch, dec_seq, _ = x_sample.shape
    l = 0

    cond = jnp.concatenate([c_ctx[None, :], c, jnp.zeros((MOD_ROWS - 1 - dec_batch, D_MODEL), F32)], axis=0)
    mod3 = _ada_call(cond, w_ada[l], b_ada[l]).reshape(MOD_ROWS, 6, D_MODEL)

    wi = w_in[l]
    g0 = 4 * M_WIDTH
    w_main = jnp.concatenate([wi[:, :g0], wi[:, g0 + N_GATES:]], axis=1).astype(BF16)
    w_g = wi[:, g0:g0 + N_GATES]
    seg = jnp.arange(A_WIDTH) // A_DH
    bd = jnp.where(seg[:, None] == seg[None, :], 1.0 / A_DH, 0.0).astype(F32)
    prm = (norm1_g[l][None, :], norm2_g[l][None, :], w_main, w_g, w_g.T, b_gates[l][None, :], b_gates[l][:, None],
           mh_norm_g[l][None, :], jnp.tile(q_norm_g[l], A_HEADS)[None, :], jnp.tile(k_norm_g[l], A_KV)[None, :], bd,
           sink_logits[l], w_out[l][:M_WIDTH].astype(BF16), w_out[l][M_WIDTH:].astype(BF16),
           peer_w_q[l].astype(BF16), peer_sub_a[l].astype(BF16), peer_sub_b[l].astype(BF16), peer_u[l], peer_v[l])

    zeros_c = jnp.zeros((batch, 2, M_HEADS, M_DH, M_DH), F32)
    c0, m0 = _pack_state(zeros_c, zeros_c[..., 0], jnp.full((batch, 2, M_HEADS), NEG, F32))
    y_p, k_new, v_new, c_fin, m_fin = _layer(x_prompt, mod3, 0, prm, (None, None, c0, m0), False)

    c0s, m0s = _pack_state(state_mlstm_C[:, l], state_mlstm_n[:, l], state_mlstm_m[:, l])
    past = cache_attn_k.shape[2]
    kc = cache_attn_k[:, l].reshape(dec_batch, past, KV_WIDTH)
    vc = cache_attn_v[:, l].reshape(dec_batch, past, KV_WIDTH)
    y_s, _, _, _, _ = _layer(x_sample, mod3, 1, prm, (kc, vc, c0s, m0s), True)

    c_fin = c_fin.reshape(batch, 2, M_HEADS, M_DH, 2 * M_DH)
    new_c = c_fin[..., :M_DH][:, None]
    new_n = c_fin[..., M_DH][:, None]
    new_m = m_fin[:, :, 0, 0].reshape(batch, 2, M_HEADS)[:, None]
    new_k = k_new.reshape(batch, 1, seq, A_KV, A_DH)
    new_v = v_new.reshape(batch, 1, seq, A_KV, A_DH)
    return y_p, y_s, new_k, new_v, new_c, new_n, new_m
```

```python
import functools

import jax
import jax.numpy as jnp
from jax import lax
from jax.experimental import pallas as pl
from jax.experimental.pallas import tpu as pltpu
from jax.experimental.pallas import tpu_sc as plsc

F32 = jnp.float32
BF16 = jnp.bfloat16
I32 = jnp.int32
HI = lax.Precision.HIGHEST

D_MODEL = 1024
EPS = 1e-6
NEG = -1e30
GRID_W = 64
M_HEADS = 4
M_WIDTH = 512
M_DH = 128
A_HEADS = 8
A_KV = 2
A_REP = 4
A_DH = 64
A_WIDTH = 512
KV_WIDTH = A_KV * A_DH
BLOCK = 128
ROPE_BASE = 10000.0
N_KEYS = 128
P_HEADS = 8
P_DKEY = 256
P_HALF = 128
P_TOPK = 16
N_SEL = P_HEADS * P_TOPK
N_GATES = 4 * M_HEADS
MAIN_COLS = 4 * M_WIDTH + A_WIDTH + 2 * KV_WIDTH
MOD_ROWS = 16

TOKEN_TILE = 256
MLSTM_CHUNK = 128
ADA_COL_TILE = 768


def _sigmoid(x):
    return 1.0 / (1.0 + jnp.exp(-x))


def _log_sigmoid(x):
    return jnp.minimum(x, 0.0) - jnp.log1p(jnp.exp(-jnp.abs(x)))


def _dot_t(a, b, precision=None):
    return lax.dot_general(a, b, (((1,), (1,)), ((), ())), precision=precision,
                           preferred_element_type=F32)


def _ada_kernel(c_ref, w_ref, b_ref, o_ref):
    c = c_ref[...]
    s = c * _sigmoid(c)
    o_ref[...] = jnp.dot(s, w_ref[...], precision=HI, preferred_element_type=F32) + b_ref[...]


def _ada_call(cond, w_ada, b_ada):
    n_out = w_ada.shape[1]
    return pl.pallas_call(
        _ada_kernel,
        grid=(n_out // ADA_COL_TILE,),
        in_specs=[pl.BlockSpec((MOD_ROWS, D_MODEL), lambda j: (0, 0)),
                  pl.BlockSpec((D_MODEL, ADA_COL_TILE), lambda j: (0, j)),
                  pl.BlockSpec((1, ADA_COL_TILE), lambda j: (0, j))],
        out_specs=pl.BlockSpec((MOD_ROWS, ADA_COL_TILE), lambda j: (0, j)),
        out_shape=jax.ShapeDtypeStruct((MOD_ROWS, n_out), F32),
        name="ada",
    )(cond, w_ada, b_ada.reshape(1, n_out))


def _swap16(x):
    n = x.shape[-1]
    lane = lax.broadcasted_iota(I32, x.shape, x.ndim - 1)
    return jnp.where((lane & 16) == 0, pltpu.roll(x, n - 16, x.ndim - 1), pltpu.roll(x, 16, x.ndim - 1))


def _inproj_kernel(x_ref, mod_ref, n1_ref, w_ref, wg_ref, wgt_ref, bg_ref, bgt_ref, qg_ref, kg_ref,
                   bd_ref, cos_ref, sin_ref,
                   mq_ref, mk_ref, mv_ref, mo_ref, gc_ref, gr_ref, aq_ref, ak_ref, av_ref):
    x = x_ref[...]
    h = x * lax.rsqrt(jnp.mean(x * x, axis=-1, keepdims=True) + EPS) * n1_ref[...]
    h = h * (1.0 + mod_ref[0, 1:2, :]) + mod_ref[0, 0:1, :]
    z = jnp.dot(h.astype(BF16), w_ref[...], preferred_element_type=F32)

    mq_ref[...] = (z[:, 0:M_WIDTH] * (M_DH ** -0.5)).astype(BF16)
    mk_ref[...] = z[:, M_WIDTH:2 * M_WIDTH].astype(BF16)
    mv_ref[...] = z[:, 2 * M_WIDTH:3 * M_WIDTH].astype(BF16)
    mo_ref[...] = z[:, 3 * M_WIDTH:4 * M_WIDTH]

    g = jnp.dot(h, wg_ref[...], precision=HI, preferred_element_type=F32) + bg_ref[...]
    kind = lax.broadcasted_iota(I32, g.shape, 1) // M_HEADS
    gc_ref[...] = jnp.where((kind & 1) == 1, _log_sigmoid(g), g)
    gt = _dot_t(wgt_ref[...], h, precision=HI) + bgt_ref[...]
    kind_t = lax.broadcasted_iota(I32, gt.shape, 0) // M_HEADS
    gr_ref[...] = jnp.where((kind_t & 1) == 1, _log_sigmoid(gt), gt)

    o = 4 * M_WIDTH
    aq = z[:, o:o + A_WIDTH]
    ak = z[:, o + A_WIDTH:o + A_WIDTH + KV_WIDTH]
    av_ref[...] = z[:, o + A_WIDTH + KV_WIDTH:o + A_WIDTH + 2 * KV_WIDTH]
    bd = bd_ref[...]
    cos = cos_ref[...]
    sin = sin_ref[...]
    aq = aq * lax.rsqrt(jnp.dot(aq * aq, bd, precision=HI, preferred_element_type=F32) + EPS) * qg_ref[...]
    cos4 = jnp.concatenate([cos] * (A_WIDTH // KV_WIDTH), axis=1)
    sin4 = jnp.concatenate([sin] * (A_WIDTH // KV_WIDTH), axis=1)
    aq = (aq * cos4 + _swap16(aq) * sin4) * (A_DH ** -0.5)
    ak = ak * lax.rsqrt(jnp.dot(ak * ak, bd[0:KV_WIDTH, 0:KV_WIDTH], precision=HI,
                                preferred_element_type=F32) + EPS) * kg_ref[...]
    ak_ref[...] = ak * cos + _swap16(ak) * sin

    lane = lax.broadcasted_iota(I32, (aq.shape[0], KV_WIDTH), 1)
    for hd in range(A_HEADS):
        grp = hd // A_REP
        blk = aq[:, (hd // 2) * KV_WIDTH:(hd // 2 + 1) * KV_WIDTH]
        if hd % 2 != grp:
            blk = pltpu.roll(blk, A_DH, 1)
        keep = (lane >= grp * A_DH) & (lane < (grp + 1) * A_DH)
        aq_ref[hd] = jnp.where(keep, blk, 0.0).astype(BF16)


def _inproj_call(x2d, seq, mod3, mod_row0, n1, w_main, w_g, w_gt, b_g, b_gt, qg_t, kg_t, bd, cos, sin):
    n = x2d.shape[0]
    tm = TOKEN_TILE
    per_seq = seq // tm

    def tok(i):
        return (i, 0)

    def const2(i):
        return (0, 0)

    in_specs = [
        pl.BlockSpec((tm, D_MODEL), tok),
        pl.BlockSpec((1, 6, D_MODEL), lambda i: (mod_row0 + (i // per_seq if mod_row0 else 0), 0, 0)),
        pl.BlockSpec((1, D_MODEL), const2),
        pl.BlockSpec((D_MODEL, MAIN_COLS), const2),
        pl.BlockSpec((D_MODEL, N_GATES), const2),
        pl.BlockSpec((N_GATES, D_MODEL), const2),
        pl.BlockSpec((1, N_GATES), const2),
        pl.BlockSpec((N_GATES, 1), const2),
        pl.BlockSpec((1, A_WIDTH), const2),
        pl.BlockSpec((1, KV_WIDTH), const2),
        pl.BlockSpec((A_WIDTH, A_WIDTH), const2),
        pl.BlockSpec((tm, KV_WIDTH), lambda i: (i % per_seq, 0)),
        pl.BlockSpec((tm, KV_WIDTH), lambda i: (i % per_seq, 0)),
    ]
    out_specs = [
        pl.BlockSpec((tm, M_WIDTH), tok),
        pl.BlockSpec((tm, M_WIDTH), tok),
        pl.BlockSpec((tm, M_WIDTH), tok),
        pl.BlockSpec((tm, M_WIDTH), tok),
        pl.BlockSpec((tm, N_GATES), tok),
        pl.BlockSpec((N_GATES, tm), lambda i: (0, i)),
        pl.BlockSpec((A_HEADS, tm, KV_WIDTH), lambda i: (0, i, 0)),
        pl.BlockSpec((tm, KV_WIDTH), tok),
        pl.BlockSpec((tm, KV_WIDTH), tok),
    ]
    out_shape = [
        jax.ShapeDtypeStruct((n, M_WIDTH), BF16),
        jax.ShapeDtypeStruct((n, M_WIDTH), BF16),
        jax.ShapeDtypeStruct((n, M_WIDTH), BF16),
        jax.ShapeDtypeStruct((n, M_WIDTH), F32),
        jax.ShapeDtypeStruct((n, N_GATES), F32),
        jax.ShapeDtypeStruct((N_GATES, n), F32),
        jax.ShapeDtypeStruct((A_HEADS, n, KV_WIDTH), BF16),
        jax.ShapeDtypeStruct((n, KV_WIDTH), F32),
        jax.ShapeDtypeStruct((n, KV_WIDTH), F32),
    ]
    return pl.pallas_call(
        _inproj_kernel, grid=(n // tm,), in_specs=in_specs, out_specs=out_specs, out_shape=out_shape,
        compiler_params=pltpu.CompilerParams(dimension_semantics=("parallel",)),
        name="inproj",
    )(x2d, mod3, n1, w_main, w_g, w_gt, b_g, b_gt, qg_t, kg_t, bd, cos, sin)


def _mlstm_chain(q, k, v, li_c, lf_c, li_r, lf_r, caug, m, tri, tri_t, mask, reverse):
    L = q.shape[0]
    last = 0 if reverse else L - 1
    b_c = jnp.dot(tri, jnp.broadcast_to(lf_c, (L, L)), precision=HI, preferred_element_type=F32)
    b_r = jnp.dot(jnp.broadcast_to(lf_r, (8, L)), tri_t, precision=HI, preferred_element_type=F32)[0:1, :]
    a_inter = b_c[:, 0:1] + m
    d = jnp.where(mask, b_c - b_r + li_r, -jnp.inf)
    m_t = jnp.maximum(a_inter, jnp.max(d, axis=1, keepdims=True))
    w_inter = jnp.exp(a_inter - m_t)
    s = _dot_t(q, k) * jnp.exp(d - m_t)
    qc = jnp.dot(q, caug.astype(BF16), preferred_element_type=F32)
    num = jnp.dot(s.astype(BF16), v, preferred_element_type=F32) + w_inter * qc[:, 0:M_DH]
    den = jnp.sum(s, axis=1, keepdims=True) + w_inter * qc[:, M_DH:M_DH + 1]
    den = jnp.maximum(jnp.abs(den), jnp.exp(-m_t))
    h = num / den
    m_new = m_t[last:last + 1, :]
    b_last = b_c[last:last + 1, 0:1]
    g_c = jnp.exp(b_last - b_c[:, 0:1] + li_c - m_new)
    decay = jnp.exp(b_last + m - m_new)
    kw = (k.astype(F32) * g_c).astype(BF16)
    vaug = jnp.concatenate([v, jnp.ones_like(v)], axis=1)
    upd = lax.dot_general(kw, vaug, (((0,), (0,)), ((), ())), preferred_element_type=F32)
    return h, decay * caug + upd, m_new


def _mlstm_kernel(qf_ref, kf_ref, vf_ref, gcf_ref, grf_ref, qb_ref, kb_ref, vb_ref, gcb_ref, grb_ref,
                  c0_ref, m0_ref, hf_ref, hb_ref, cfin_ref, mfin_ref, c_scr, m_scr):
    c = pl.program_id(1)
    nc = pl.num_programs(1)
    L = qf_ref.shape[0]

    @pl.when(c == 0)
    def _():
        c_scr[...] = c0_ref[0]
        m_scr[...] = m0_ref[0]

    row = lax.broadcasted_iota(I32, (L, L), 0)
    col = lax.broadcasted_iota(I32, (L, L), 1)
    lower = row >= col
    upper = row <= col
    lower_f = lower.astype(F32)
    upper_f = upper.astype(F32)

    for direction in range(2):
        reverse = direction == 1
        q_ref, k_ref, v_ref, gc_ref, gr_ref, h_ref = (
            (qb_ref, kb_ref, vb_ref, gcb_ref, grb_ref, hb_ref) if reverse
            else (qf_ref, kf_ref, vf_ref, gcf_ref, grf_ref, hf_ref))
        tri, tri_t, mask = (upper_f, lower_f, upper) if reverse else (lower_f, upper_f, lower)
        gc = gc_ref[...]
        gr = gr_ref[...]
        for hd in range(M_HEADS):
            ch = direction * M_HEADS + hd
            sl = slice(hd * M_DH, (hd + 1) * M_DH)
            ci = 2 * direction * M_HEADS + hd
            cf = ci + M_HEADS
            h, caug, m_new = _mlstm_chain(
                q_ref[:, sl], k_ref[:, sl], v_ref[:, sl],
                gc[:, ci:ci + 1], gc[:, cf:cf + 1], gr[ci:ci + 1, :], gr[cf:cf + 1, :],
                c_scr[ch], m_scr[ch][0:1, 0:1], tri, tri_t, mask, reverse)
            h_ref[:, sl] = h
            c_scr[ch] = caug
            m_scr[ch] = jnp.broadcast_to(m_new, m_scr.shape[1:])

    @pl.when(c == nc - 1)
    def _():
        cfin_ref[0] = c_scr[...]
        mfin_ref[0] = m_scr[...]


def _mlstm_call(mq, mk, mv, gcol, grow, c0, m0, batch, seq):
    n = mq.shape[0]
    L = MLSTM_CHUNK
    nc = seq // L
    n_ch = 2 * M_HEADS

    def fwd(b, c):
        return (b * nc + c, 0)

    def bwd(b, c):
        return (b * nc + nc - 1 - c, 0)

    def fwd_t(b, c):
        return (0, b * nc + c)

    def bwd_t(b, c):
        return (0, b * nc + nc - 1 - c)

    tok = pl.BlockSpec((L, M_WIDTH), fwd)
    tok_b = pl.BlockSpec((L, M_WIDTH), bwd)
    in_specs = [tok, tok, tok, pl.BlockSpec((L, N_GATES), fwd), pl.BlockSpec((N_GATES, L), fwd_t),
                tok_b, tok_b, tok_b, pl.BlockSpec((L, N_GATES), bwd), pl.BlockSpec((N_GATES, L), bwd_t),
                pl.BlockSpec((1, n_ch, M_DH, 2 * M_DH), lambda b, c: (b, 0, 0, 0)),
                pl.BlockSpec((1, n_ch, 8, M_DH), lambda b, c: (b, 0, 0, 0))]
    out_specs = [tok, tok_b,
                 pl.BlockSpec((1, n_ch, M_DH, 2 * M_DH), lambda b, c: (b, 0, 0, 0)),
                 pl.BlockSpec((1, n_ch, 8, M_DH), lambda b, c: (b, 0, 0, 0))]
    out_shape = [jax.ShapeDtypeStruct((n, M_WIDTH), F32), jax.ShapeDtypeStruct((n, M_WIDTH), F32),
                 jax.ShapeDtypeStruct((batch, n_ch, M_DH, 2 * M_DH), F32),
                 jax.ShapeDtypeStruct((batch, n_ch, 8, M_DH), F32)]
    return pl.pallas_call(
        _mlstm_kernel, grid=(batch, nc), in_specs=in_specs, out_specs=out_specs, out_shape=out_shape,
        scratch_shapes=[pltpu.VMEM((n_ch, M_DH, 2 * M_DH), F32), pltpu.VMEM((n_ch, 8, M_DH), F32)],
        compiler_params=pltpu.CompilerParams(dimension_semantics=("parallel", "arbitrary")),
        name="mlstm",
    )(mq, mk, mv, gcol, grow, mq, mk, mv, gcol, grow, c0, m0)


def _sink_column(sink_ref, grp, rows_per_head):
    return jnp.concatenate(
        [jnp.full((rows_per_head, 1), sink_ref[grp * A_REP + r], F32) for r in range(A_REP)], axis=0)


def _store_heads(out_ref, o, grp, rows_per_head):
    for r in range(A_REP):
        hd = grp * A_REP + r
        out_ref[:, hd * A_DH:(hd + 1) * A_DH] = o[r * rows_per_head:(r + 1) * rows_per_head,
                                                  grp * A_DH:(grp + 1) * A_DH].astype(out_ref.dtype)


def _attn_ctx_kernel(sink_ref, q_ref, k_ref, v_ref, out_ref):
    s_len = k_ref.shape[0]
    k = k_ref[...].astype(BF16)
    v = v_ref[...].astype(BF16)
    for grp in range(A_KV):
        q = q_ref[grp * A_REP:(grp + 1) * A_REP].reshape(A_REP * s_len, KV_WIDTH)
        s = _dot_t(q, k)
        sk = _sink_column(sink_ref, grp, s_len)
        mx = jnp.maximum(jnp.max(s, axis=1, keepdims=True), sk)
        p = jnp.exp(s - mx)
        den = jnp.sum(p, axis=1, keepdims=True) + jnp.exp(sk - mx)
        o = jnp.dot(p.astype(BF16), v, preferred_element_type=F32) / den
        _store_heads(out_ref, o, grp, s_len)


def _attn_ctx_call(sink, aq, ak, av, batch, seq):
    n = ak.shape[0]
    return pl.pallas_call(
        _attn_ctx_kernel, grid=(batch,),
        in_specs=[pl.BlockSpec(memory_space=pltpu.SMEM),
                  pl.BlockSpec((A_HEADS, seq, KV_WIDTH), lambda b: (0, b, 0)),
                  pl.BlockSpec((seq, KV_WIDTH), lambda b: (b, 0)),
                  pl.BlockSpec((seq, KV_WIDTH), lambda b: (b, 0))],
        out_specs=pl.BlockSpec((seq, A_WIDTH), lambda b: (b, 0)),
        out_shape=jax.ShapeDtypeStruct((n, A_WIDTH), BF16),
        compiler_params=pltpu.CompilerParams(dimension_semantics=("parallel",)),
        name="attn_ctx",
    )(sink, aq, ak, av)


def _attn_lat_kernel(sink_ref, q_ref, kc_ref, vc_ref, kp_ref, kq_ref, kn_ref, vp_ref, vq_ref, vn_ref, out_ref):
    i = pl.program_id(1)
    nb = pl.num_programs(1)
    kc = kc_ref[0].astype(BF16)
    vc = vc_ref[0].astype(BF16)
    kp, kq, kn = kp_ref[...].astype(BF16), kq_ref[...].astype(BF16), kn_ref[...].astype(BF16)
    vp, vq, vn = vp_ref[...].astype(BF16), vq_ref[...].astype(BF16), vn_ref[...].astype(BF16)
    rows = A_REP * BLOCK
    qpos = lax.broadcasted_iota(I32, (rows, BLOCK), 0) % BLOCK
    kpos = lax.broadcasted_iota(I32, (rows, BLOCK), 1)
    mask_p = (kpos >= qpos) & (i > 0)
    mask_n = (kpos <= qpos) & (i < nb - 1)
    for grp in range(A_KV):
        q = q_ref[grp * A_REP:(grp + 1) * A_REP].reshape(rows, KV_WIDTH)
        s_c = _dot_t(q, kc)
        s_p = jnp.where(mask_p, _dot_t(q, kp), NEG)
        s_q = _dot_t(q, kq)
        s_n = jnp.where(mask_n, _dot_t(q, kn), NEG)
        sk = _sink_column(sink_ref, grp, BLOCK)
        mx = jnp.maximum(jnp.maximum(jnp.max(s_c, axis=1, keepdims=True), jnp.max(s_p, axis=1, keepdims=True)),
                         jnp.maximum(jnp.max(s_q, axis=1, keepdims=True), jnp.max(s_n, axis=1, keepdims=True)))
        mx = jnp.maximum(mx, sk)
        p_c, p_p, p_q, p_n = jnp.exp(s_c - mx), jnp.exp(s_p - mx), jnp.exp(s_q - mx), jnp.exp(s_n - mx)
        den = (jnp.sum(p_c, axis=1, keepdims=True) + jnp.sum(p_p, axis=1, keepdims=True)
               + jnp.sum(p_q, axis=1, keepdims=True) + jnp.sum(p_n, axis=1, keepdims=True) + jnp.exp(sk - mx))
        o = (jnp.dot(p_c.astype(BF16), vc, preferred_element_type=F32)
             + jnp.dot(p_p.astype(BF16), vp, preferred_element_type=F32)
             + jnp.dot(p_q.astype(BF16), vq, preferred_element_type=F32)
             + jnp.dot(p_n.astype(BF16), vn, preferred_element_type=F32)) / den
        _store_heads(out_ref, o, grp, BLOCK)


def _attn_lat_call(sink, aq, ak, av, kc, vc, batch, seq):
    n = ak.shape[0]
    nb = seq // BLOCK
    past = kc.shape[1]

    def cur(b, i):
        return (b * nb + i, 0)

    def prev(b, i):
        return (b * nb + jnp.maximum(i - 1, 0), 0)

    def nxt(b, i):
        return (b * nb + jnp.minimum(i + 1, nb - 1), 0)

    blk = functools.partial(pl.BlockSpec, (BLOCK, KV_WIDTH))
    cache = pl.BlockSpec((1, past, KV_WIDTH), lambda b, i: (b, 0, 0))
    return pl.pallas_call(
        _attn_lat_kernel, grid=(batch, nb),
        in_specs=[pl.BlockSpec(memory_space=pltpu.SMEM),
                  pl.BlockSpec((A_HEADS, BLOCK, KV_WIDTH), lambda b, i: (0, b * nb + i, 0)),
                  cache, cache, blk(prev), blk(cur), blk(nxt), blk(prev), blk(cur), blk(nxt)],
        out_specs=pl.BlockSpec((BLOCK, A_WIDTH), cur),
        out_shape=jax.ShapeDtypeStruct((n, A_WIDTH), BF16),
        compiler_params=pltpu.CompilerParams(dimension_semantics=("parallel", "parallel")),
        name="attn_lat",
    )(sink, aq, kc, vc, ak, ak, ak, av, av, av)


def _top16_rows(s, payload=None):
    n_rows = s.shape[0]
    rows = lax.broadcasted_iota(I32, s.shape, 0).astype(F32)
    vals, idxs, pays = [], [], []
    for _ in range(P_TOPK):
        mx = jnp.max(s, axis=0, keepdims=True)
        ix = jnp.min(jnp.where(s == mx, rows, float(n_rows)), axis=0, keepdims=True)
        hit = rows == ix
        vals.append(mx)
        idxs.append(ix)
        if payload is not None:
            pays.append(jnp.sum(jnp.where(hit, payload, 0.0), axis=0, keepdims=True))
        s = jnp.where(hit, -jnp.inf, s)
    out = (jnp.concatenate(vals, axis=0), jnp.concatenate(idxs, axis=0))
    if payload is not None:
        out += (jnp.concatenate(pays, axis=0),)
    return out


def _mix_kernel(x_ref, hf_ref, hb_ref, mo_ref, ao_ref, mod_ref, mhg_ref, n2_ref, wm_ref, wa_ref, wq_ref,
                sa_ref, sb_ref, x1_ref, h2_ref, eidx_ref, gate_ref, qp_scr, e_scr, g_scr):
    tm = x_ref.shape[0]
    hs = hf_ref[...] + hb_ref[...]
    parts = []
    for hd in range(M_HEADS):
        blk = hs[:, hd * M_DH:(hd + 1) * M_DH]
        parts.append(blk * lax.rsqrt(jnp.mean(blk * blk, axis=-1, keepdims=True) + EPS))
    m_out = _sigmoid(mo_ref[...]) * (jnp.concatenate(parts, axis=1) * mhg_ref[...])
    mix = (jnp.dot(m_out.astype(BF16), wm_ref[...], preferred_element_type=F32)
           + jnp.dot(ao_ref[...], wa_ref[...], preferred_element_type=F32))
    x1 = x_ref[...] + mod_ref[0, 2:3, :] * mix
    x1_ref[...] = x1
    h2 = x1 * lax.rsqrt(jnp.mean(x1 * x1, axis=-1, keepdims=True) + EPS) * n2_ref[...]
    h2 = h2 * (1.0 + mod_ref[0, 4:5, :]) + mod_ref[0, 3:4, :]
    h2_ref[...] = h2
    qp = jnp.dot(h2.astype(BF16), wq_ref[...], preferred_element_type=F32)
    for p in range(P_HEADS):
        qp_scr[p] = qp[:, p * P_DKEY:(p + 1) * P_DKEY].astype(BF16)
    sub_a = sa_ref[...]
    sub_b = sb_ref[...]

    def head_body(p, carry):
        for half in range(tm // N_KEYS):
            cols = slice(half * N_KEYS, (half + 1) * N_KEYS)
            qh = qp_scr[p, pl.ds(half * N_KEYS, N_KEYS), :]
            s_a = _dot_t(sub_a, qh[:, 0:P_HALF])
            s_b = _dot_t(sub_b, qh[:, P_HALF:P_DKEY])
            va, ia = _top16_rows(s_a)
            vb, ib = _top16_rows(s_b)
            cand = jnp.concatenate([va[i:i + 1, :] + vb for i in range(P_TOPK)], axis=0)
            cidx = jnp.concatenate([ia[i:i + 1, :] * float(N_KEYS) + ib for i in range(P_TOPK)], axis=0)
            top, _, eidx = _top16_rows(cand, cidx)
            ex = jnp.exp(top - jnp.max(top, axis=0, keepdims=True))
            gates = ex / jnp.sum(ex, axis=0, keepdims=True)
            r0 = pl.multiple_of(p * P_TOPK, P_TOPK)
            e_scr[pl.ds(r0, P_TOPK), cols] = eidx
            g_scr[pl.ds(r0, P_TOPK), cols] = gates
        return carry

    lax.fori_loop(0, P_HEADS, head_body, 0)
    for half in range(tm // N_KEYS):
        cols = slice(half * N_KEYS, (half + 1) * N_KEYS)
        eidx_ref[cols, :] = e_scr[:, cols].T.astype(I32)
        gate_ref[cols, :] = g_scr[:, cols].T


def _mix_call(x2d, seq, h_f, h_b, mo, a_out, mod3, mod_row0, mhg, n2, w_m, w_a, w_q, sub_a, sub_b):
    n = x2d.shape[0]
    tm = TOKEN_TILE
    per_seq = seq // tm

    def tok(i):
        return (i, 0)

    def const2(i):
        return (0, 0)

    in_specs = [
        pl.BlockSpec((tm, D_MODEL), tok),
        pl.BlockSpec((tm, M_WIDTH), tok), pl.BlockSpec((tm, M_WIDTH), tok), pl.BlockSpec((tm, M_WIDTH), tok),
        pl.BlockSpec((tm, A_WIDTH), tok),
        pl.BlockSpec((1, 6, D_MODEL), lambda i: (mod_row0 + (i // per_seq if mod_row0 else 0), 0, 0)),
        pl.BlockSpec((1, M_WIDTH), const2),
        pl.BlockSpec((1, D_MODEL), const2),
        pl.BlockSpec((M_WIDTH, D_MODEL), const2),
        pl.BlockSpec((A_WIDTH, D_MODEL), const2),
        pl.BlockSpec((D_MODEL, P_HEADS * P_DKEY), const2),
        pl.BlockSpec((N_KEYS, P_HALF), const2),
        pl.BlockSpec((N_KEYS, P_HALF), const2),
    ]
    out_specs = [pl.BlockSpec((tm, D_MODEL), tok), pl.BlockSpec((tm, D_MODEL), tok),
                 pl.BlockSpec((tm, N_SEL), tok), pl.BlockSpec((tm, N_SEL), tok)]
    out_shape = [jax.ShapeDtypeStruct((n, D_MODEL), F32), jax.ShapeDtypeStruct((n, D_MODEL), F32),
                 jax.ShapeDtypeStruct((n, N_SEL), I32), jax.ShapeDtypeStruct((n, N_SEL), F32)]
    return pl.pallas_call(
        _mix_kernel, grid=(n // tm,), in_specs=in_specs, out_specs=out_specs, out_shape=out_shape,
        scratch_shapes=[pltpu.VMEM((P_HEADS, tm, P_DKEY), BF16), pltpu.VMEM((N_SEL, tm), F32),
                        pltpu.VMEM((N_SEL, tm), F32)],
        compiler_params=pltpu.CompilerParams(dimension_semantics=("parallel",)),
        name="mix",
    )(x2d, h_f, h_b, mo, a_out, mod3, mhg, n2, w_m, w_a, w_q, sub_a, sub_b)


SC_LANES = 16
SC_CORES = 2
SC_SUBCORES = 16
SC_WORKERS = SC_CORES * SC_SUBCORES
SC_TOKENS = 8
SC_GROUP = SC_LANES
SC_NGROUPS = N_SEL // SC_GROUP
SC_XCHUNK = 8 * SC_LANES
SC_OCHUNK = 32 * SC_LANES
GELU_C0 = 0.7978845608028654
GELU_C1 = 0.044715


def _sc_gelu(a):
    z = GELU_C0 * (a + GELU_C1 * (a * a * a))
    tanh = 1.0 - 2.0 / (jnp.exp(2.0 * z) + 1.0)
    return 0.5 * a * (1.0 + tanh)


def _peer_sc_kernel(h2_hbm, eidx_hbm, gate_hbm, u_hbm, v_hbm, out_hbm,
                    xbuf, ibuf, gbuf, obuf, ubuf, vbuf, mbuf, wbuf, sem_u0, sem_u1, sem_v0, sem_v1):
    n = h2_hbm.shape[0]
    per_worker = n // SC_WORKERS
    wid = lax.axis_index("c") * SC_SUBCORES + lax.axis_index("s")
    sems_u = (sem_u0, sem_u1)
    sems_v = (sem_v0, sem_v1)
    lane = lax.iota(I32, SC_LANES)

    def start_gather(t, g, slot):
        idx = ibuf[t, pl.ds(g * SC_GROUP, SC_GROUP)]
        cu = pltpu.async_copy(u_hbm.at[idx], ubuf.at[slot], sems_u[slot])
        cv = pltpu.async_copy(v_hbm.at[idx], vbuf.at[slot], sems_v[slot])
        return cu, cv

    def dots(t, slot):
        def sweep(jc, accs):
            base = jc * SC_XCHUNK
            xs = [xbuf[t, pl.ds(base + k * SC_LANES, SC_LANES)] for k in range(SC_XCHUNK // SC_LANES)]
            out = []
            for r in range(SC_GROUP):
                a = accs[r]
                for k in range(SC_XCHUNK // SC_LANES):
                    a = a + xs[k] * ubuf[slot, r, pl.ds(base + k * SC_LANES, SC_LANES)]
                out.append(a)
            return tuple(out)

        zero = jnp.zeros((SC_LANES,), F32)
        accs = lax.fori_loop(0, D_MODEL // SC_XCHUNK, sweep, tuple(zero for _ in range(SC_GROUP)))
        for r in range(SC_GROUP):
            mbuf[r, :] = accs[r]
        tot = zero
        for c in range(SC_LANES):
            tot = tot + plsc.load_gather(mbuf, [lane, jnp.full((SC_LANES,), c, I32)])
        return tot

    def accumulate(t, slot):
        for oc in range(D_MODEL // SC_OCHUNK):
            nv = SC_OCHUNK // SC_LANES
            accs = tuple(obuf[t, pl.ds(oc * SC_OCHUNK + j * SC_LANES, SC_LANES)] for j in range(nv))

            def row(r, accs):
                wb = plsc.load_gather(wbuf, [jnp.full((SC_LANES,), r, I32)])
                return tuple(accs[j] + wb * vbuf[slot, r, pl.ds(oc * SC_OCHUNK + j * SC_LANES, SC_LANES)]
                             for j in range(nv))

            accs = lax.fori_loop(0, SC_GROUP, row, accs)
            for j in range(nv):
                obuf[t, pl.ds(oc * SC_OCHUNK + j * SC_LANES, SC_LANES)] = accs[j]

    @pl.loop(0, per_worker // SC_TOKENS)
    def _(blk):
        tok0 = pl.multiple_of(wid * per_worker + blk * SC_TOKENS, SC_TOKENS)
        pltpu.sync_copy(h2_hbm.at[pl.ds(tok0, SC_TOKENS)], xbuf)
        pltpu.sync_copy(eidx_hbm.at[pl.ds(tok0, SC_TOKENS)], ibuf)
        pltpu.sync_copy(gate_hbm.at[pl.ds(tok0, SC_TOKENS)], gbuf)

        @pl.loop(0, SC_TOKENS)
        def _(t):
            zero = jnp.zeros((SC_LANES,), F32)
            for j in range(D_MODEL // SC_LANES):
                obuf[t, pl.ds(j * SC_LANES, SC_LANES)] = zero
            pending = start_gather(t, 0, 0)
            for g in range(SC_NGROUPS):
                slot = g % 2
                nxt = start_gather(t, g + 1, 1 - slot) if g + 1 < SC_NGROUPS else None
                pending[0].wait()
                pending[1].wait()
                a = dots(t, slot)
                wbuf[...] = gbuf[t, pl.ds(g * SC_GROUP, SC_GROUP)] * _sc_gelu(a)
                accumulate(t, slot)
                pending = nxt

        pltpu.sync_copy(obuf, out_hbm.at[pl.ds(tok0, SC_TOKENS)])


def _peer_experts(h2, eidx, gates, u_tab, v_tab):
    n = h2.shape[0]
    mesh = plsc.VectorSubcoreMesh(core_axis_name="c", subcore_axis_name="s")
    fn = pl.kernel(
        _peer_sc_kernel,
        out_type=jax.ShapeDtypeStruct((n, D_MODEL), F32),
        mesh=mesh,
        scratch_types=[
            pltpu.VMEM((SC_TOKENS, D_MODEL), F32),
            pltpu.VMEM((SC_TOKENS, N_SEL), I32),
            pltpu.VMEM((SC_TOKENS, N_SEL), F32),
            pltpu.VMEM((SC_TOKENS, D_MODEL), F32),
            pltpu.VMEM((2, SC_GROUP, D_MODEL), F32),
            pltpu.VMEM((2, SC_GROUP, D_MODEL), F32),
            pltpu.VMEM((SC_GROUP, SC_LANES), F32),
            pltpu.VMEM((SC_LANES,), F32),
            pltpu.SemaphoreType.DMA, pltpu.SemaphoreType.DMA, pltpu.SemaphoreType.DMA, pltpu.SemaphoreType.DMA,
        ],
        compiler_params=pltpu.CompilerParams(needs_layout_passes=False),
        name="peer_experts",
    )
    return fn(h2, eidx, gates, u_tab, v_tab)


def _resid_kernel(x1_ref, p_ref, mod_ref, o_ref):
    o_ref[...] = x1_ref[...] + mod_ref[0, 5:6, :] * p_ref[...]


def _resid_call(x1, peer_out, seq, mod3, mod_row0):
    n = x1.shape[0]
    tm = TOKEN_TILE
    per_seq = seq // tm
    tok = pl.BlockSpec((tm, D_MODEL), lambda i: (i, 0))
    return pl.pallas_call(
        _resid_kernel, grid=(n // tm,),
        in_specs=[tok, tok,
                  pl.BlockSpec((1, 6, D_MODEL), lambda i: (mod_row0 + (i // per_seq if mod_row0 else 0), 0, 0))],
        out_specs=tok, out_shape=jax.ShapeDtypeStruct((n, D_MODEL), F32),
        compiler_params=pltpu.CompilerParams(dimension_semantics=("parallel",)),
        name="resid",
    )(x1, peer_out, mod3)


def _rope_tables(seq, rotate):
    if not rotate:
        return jnp.ones((seq, KV_WIDTH), F32), jnp.zeros((seq, KV_WIDTH), F32)
    quarter = A_DH // 4
    t = jnp.arange(seq)
    row = (t // GRID_W).astype(F32)
    col = (t % GRID_W).astype(F32)
    inv = ROPE_BASE ** (-jnp.arange(quarter, dtype=F32) / quarter)
    d = jnp.arange(A_DH)
    pos = jnp.where(d[None, :] < A_DH // 2, row[:, None], col[:, None])
    ang = pos * inv[d % quarter][None, :]
    sign = jnp.where((d % (A_DH // 2)) < quarter, -1.0, 1.0).astype(F32)
    cos = jnp.cos(ang)
    sin = jnp.sin(ang) * sign[None, :]
    return jnp.tile(cos, (1, KV_WIDTH // A_DH)), jnp.tile(sin, (1, KV_WIDTH // A_DH))


def _layer(x, mod3, mod_row0, prm, cache, rotate):
    (n1, n2, w_main, w_g, w_gt, b_g, b_gt, mhg, qg_t, kg_t, bd, sink, w_m, w_a, w_q, sub_a, sub_b, u_tab, v_tab) = prm
    batch, seq, _ = x.shape
    n = batch * seq
    x2d = x.reshape(n, D_MODEL)
    cos, sin = _rope_tables(seq, rotate)
    mq, mk, mv, mo, gcol, grow, aq, ak, av = _inproj_call(
        x2d, seq, mod3, mod_row0, n1, w_main, w_g, w_gt, b_g, b_gt, qg_t, kg_t, bd, cos, sin)
    kc, vc, c0, m0 = cache
    h_f, h_b, c_fin, m_fin = _mlstm_call(mq, mk, mv, gcol, grow, c0, m0, batch, seq)
    if kc is None:
        a_out = _attn_ctx_call(sink, aq, ak, av, batch, seq)
    else:
        a_out = _attn_lat_call(sink, aq, ak, av, kc, vc, batch, seq)
    x1, h2, eidx, gates = _mix_call(x2d, seq, h_f, h_b, mo, a_out, mod3, mod_row0, mhg, n2, w_m, w_a, w_q,
                                    sub_a, sub_b)
    peer_out = _peer_experts(h2, eidx, gates, u_tab, v_tab)
    out = _resid_call(x1, peer_out, seq, mod3, mod_row0)
    return out.reshape(batch, seq, D_MODEL), ak, av, c_fin, m_fin


def _pack_state(C, n_vec, m):
    b = C.shape[0]
    caug = jnp.concatenate([C, jnp.broadcast_to(n_vec[..., None], C.shape)], axis=-1)
    caug = caug.reshape(b, 2 * M_HEADS, M_DH, 2 * M_DH)
    m_rep = jnp.broadcast_to(m.reshape(b, 2 * M_HEADS, 1, 1), (b, 2 * M_HEADS, 8, M_DH))
    return caug.astype(F32), m_rep.astype(F32)


def kernel(x_prompt, x_sample, c, cache_attn_k, cache_attn_v, state_mlstm_C, state_mlstm_n, state_mlstm_m,
           c_ctx, w_ada, b_ada, norm1_g, norm2_g, w_in, b_gates, mh_norm_g, q_norm_g, k_norm_g, sink_logits,
           w_out, peer_w_q, peer_sub_a, peer_sub_b, peer_u, peer_v):
    depth = w_ada.shape[0]
    assert depth == 1
    batch, seq, _ = x_prompt.shape
    dec_batch, dec_seq, _ = x_sample.shape
    l = 0

    cond = jnp.concatenate([c_ctx[None, :], c, jnp.zeros((MOD_ROWS - 1 - dec_batch, D_MODEL), F32)], axis=0)
    mod3 = _ada_call(cond, w_ada[l], b_ada[l]).reshape(MOD_ROWS, 6, D_MODEL)

    wi = w_in[l]
    g0 = 4 * M_WIDTH
    w_main = jnp.concatenate([wi[:, :g0], wi[:, g0 + N_GATES:]], axis=1).astype(BF16)
    w_g = wi[:, g0:g0 + N_GATES]
    seg = jnp.arange(A_WIDTH) // A_DH
    bd = jnp.where(seg[:, None] == seg[None, :], 1.0 / A_DH, 0.0).astype(F32)
    prm = (norm1_g[l][None, :], norm2_g[l][None, :], w_main, w_g, w_g.T, b_gates[l][None, :], b_gates[l][:, None],
           mh_norm_g[l][None, :], jnp.tile(q_norm_g[l], A_HEADS)[None, :], jnp.tile(k_norm_g[l], A_KV)[None, :], bd,
           sink_logits[l], w_out[l][:M_WIDTH].astype(BF16), w_out[l][M_WIDTH:].astype(BF16),
           peer_w_q[l].astype(BF16), peer_sub_a[l].astype(BF16), peer_sub_b[l].astype(BF16), peer_u[l], peer_v[l])

    zeros_c = jnp.zeros((batch, 2, M_HEADS, M_DH, M_DH), F32)
    c0, m0 = _pack_state(zeros_c, zeros_c[..., 0], jnp.full((batch, 2, M_HEADS), NEG, F32))
    y_p, k_new, v_new, c_fin, m_fin = _layer(x_prompt, mod3, 0, prm, (None, None, c0, m0), False)

    c0s, m0s = _pack_state(state_mlstm_C[:, l], state_mlstm_n[:, l], state_mlstm_m[:, l])
    past = cache_attn_k.shape[2]
    kc = cache_attn_k[:, l].reshape(dec_batch, past, KV_WIDTH)
    vc = cache_attn_v[:, l].reshape(dec_batch, past, KV_WIDTH)
    y_s, _, _, _, _ = _layer(x_sample, mod3, 1, prm, (kc, vc, c0s, m0s), True)

    c_fin = c_fin.reshape(batch, 2, M_HEADS, M_DH, 2 * M_DH)
    new_c = c_fin[..., :M_DH][:, None]
    new_n = c_fin[..., M_DH][:, None]
    new_m = m_fin[:, :, 0, 0].reshape(batch, 2, M_HEADS)[:, None]
    new_k = k_new.reshape(batch, 1, seq, A_KV, A_DH)
    new_v = v_new.reshape(batch, 1, seq, A_KV, A_DH)
    return y_p, y_s, new_k, new_v, new_c, new_n, new_m
```

```python
import functools

import jax
import jax.numpy as jnp
from jax import lax
from jax.experimental import pallas as pl
from jax.experimental.pallas import tpu as pltpu
from jax.experimental.pallas import tpu_sc as plsc

F32 = jnp.float32
BF16 = jnp.bfloat16
I32 = jnp.int32
HI = lax.Precision.HIGHEST

D_MODEL = 1024
EPS = 1e-6
NEG = -1e30
GRID_W = 64
M_HEADS = 4
M_WIDTH = 512
M_DH = 128
A_HEADS = 8
A_KV = 2
A_REP = 4
A_DH = 64
A_WIDTH = 512
KV_WIDTH = A_KV * A_DH
BLOCK = 128
ROPE_BASE = 10000.0
N_KEYS = 128
P_HEADS = 8
P_DKEY = 256
P_HALF = 128
P_TOPK = 16
N_SEL = P_HEADS * P_TOPK
N_GATES = 4 * M_HEADS
MAIN_COLS = 4 * M_WIDTH + A_WIDTH + 2 * KV_WIDTH
MOD_ROWS = 16

TOKEN_TILE = 256
MLSTM_CHUNK = 128
ADA_COL_TILE = 768


def _sigmoid(x):
    return 1.0 / (1.0 + jnp.exp(-x))


def _log_sigmoid(x):
    return jnp.minimum(x, 0.0) - jnp.log1p(jnp.exp(-jnp.abs(x)))


def _dot_t(a, b, precision=None):
    return lax.dot_general(a, b, (((1,), (1,)), ((), ())), precision=precision,
                           preferred_element_type=F32)


def _ada_kernel(c_ref, w_ref, b_ref, o_ref):
    c = c_ref[...]
    s = c * _sigmoid(c)
    o_ref[...] = jnp.dot(s, w_ref[...], precision=HI, preferred_element_type=F32) + b_ref[...]


def _ada_call(cond, w_ada, b_ada):
    n_out = w_ada.shape[1]
    return pl.pallas_call(
        _ada_kernel,
        grid=(n_out // ADA_COL_TILE,),
        in_specs=[pl.BlockSpec((MOD_ROWS, D_MODEL), lambda j: (0, 0)),
                  pl.BlockSpec((D_MODEL, ADA_COL_TILE), lambda j: (0, j)),
                  pl.BlockSpec((1, ADA_COL_TILE), lambda j: (0, j))],
        out_specs=pl.BlockSpec((MOD_ROWS, ADA_COL_TILE), lambda j: (0, j)),
        out_shape=jax.ShapeDtypeStruct((MOD_ROWS, n_out), F32),
        name="ada",
    )(cond, w_ada, b_ada.reshape(1, n_out))


def _swap16(x):
    n = x.shape[-1]
    lane = lax.broadcasted_iota(I32, x.shape, x.ndim - 1)
    return jnp.where((lane & 16) == 0, pltpu.roll(x, n - 16, x.ndim - 1), pltpu.roll(x, 16, x.ndim - 1))


def _inproj_kernel(x_ref, mod_ref, n1_ref, w_ref, wg_ref, wgt_ref, bg_ref, bgt_ref, qg_ref, kg_ref,
                   bd_ref, cos_ref, sin_ref,
                   mq_ref, mk_ref, mv_ref, mo_ref, gc_ref, gr_ref, aq_ref, ak_ref, av_ref):
    x = x_ref[...]
    h = x * lax.rsqrt(jnp.mean(x * x, axis=-1, keepdims=True) + EPS) * n1_ref[...]
    h = h * (1.0 + mod_ref[0, 1:2, :]) + mod_ref[0, 0:1, :]
    z = jnp.dot(h.astype(BF16), w_ref[...], preferred_element_type=F32)

    mq_ref[...] = (z[:, 0:M_WIDTH] * (M_DH ** -0.5)).astype(BF16)
    mk_ref[...] = z[:, M_WIDTH:2 * M_WIDTH].astype(BF16)
    mv_ref[...] = z[:, 2 * M_WIDTH:3 * M_WIDTH].astype(BF16)
    mo_ref[...] = z[:, 3 * M_WIDTH:4 * M_WIDTH]

    g = jnp.dot(h, wg_ref[...], precision=HI, preferred_element_type=F32) + bg_ref[...]
    kind = lax.broadcasted_iota(I32, g.shape, 1) // M_HEADS
    gc_ref[...] = jnp.where((kind & 1) == 1, _log_sigmoid(g), g)
    gt = _dot_t(wgt_ref[...], h, precision=HI) + bgt_ref[...]
    kind_t = lax.broadcasted_iota(I32, gt.shape, 0) // M_HEADS
    gr_ref[...] = jnp.where((kind_t & 1) == 1, _log_sigmoid(gt), gt)

    o = 4 * M_WIDTH
    aq = z[:, o:o + A_WIDTH]
    ak = z[:, o + A_WIDTH:o + A_WIDTH + KV_WIDTH]
    av_ref[...] = z[:, o + A_WIDTH + KV_WIDTH:o + A_WIDTH + 2 * KV_WIDTH]
    bd = bd_ref[...]
    cos = cos_ref[...]
    sin = sin_ref[...]
    aq = aq * lax.rsqrt(jnp.dot(aq * aq, bd, precision=HI, preferred_element_type=F32) + EPS) * qg_ref[...]
    cos4 = jnp.concatenate([cos] * (A_WIDTH // KV_WIDTH), axis=1)
    sin4 = jnp.concatenate([sin] * (A_WIDTH // KV_WIDTH), axis=1)
    aq = (aq * cos4 + _swap16(aq) * sin4) * (A_DH ** -0.5)
    ak = ak * lax.rsqrt(jnp.dot(ak * ak, bd[0:KV_WIDTH, 0:KV_WIDTH], precision=HI,
                                preferred_element_type=F32) + EPS) * kg_ref[...]
    ak_ref[...] = ak * cos + _swap16(ak) * sin

    lane = lax.broadcasted_iota(I32, (aq.shape[0], KV_WIDTH), 1)
    for hd in range(A_HEADS):
        grp = hd // A_REP
        blk = aq[:, (hd // 2) * KV_WIDTH:(hd // 2 + 1) * KV_WIDTH]
        if hd % 2 != grp:
            blk = pltpu.roll(blk, A_DH, 1)
        keep = (lane >= grp * A_DH) & (lane < (grp + 1) * A_DH)
        aq_ref[hd] = jnp.where(keep, blk, 0.0).astype(BF16)


def _inproj_call(x2d, seq, mod3, mod_row0, n1, w_main, w_g, w_gt, b_g, b_gt, qg_t, kg_t, bd, cos, sin):
    n = x2d.shape[0]
    tm = TOKEN_TILE
    per_seq = seq // tm

    def tok(i):
        return (i, 0)

    def const2(i):
        return (0, 0)

    in_specs = [
        pl.BlockSpec((tm, D_MODEL), tok),
        pl.BlockSpec((1, 6, D_MODEL), lambda i: (mod_row0 + (i // per_seq if mod_row0 else 0), 0, 0)),
        pl.BlockSpec((1, D_MODEL), const2),
        pl.BlockSpec((D_MODEL, MAIN_COLS), const2),
        pl.BlockSpec((D_MODEL, N_GATES), const2),
        pl.BlockSpec((N_GATES, D_MODEL), const2),
        pl.BlockSpec((1, N_GATES), const2),
        pl.BlockSpec((N_GATES, 1), const2),
        pl.BlockSpec((1, A_WIDTH), const2),
        pl.BlockSpec((1, KV_WIDTH), const2),
        pl.BlockSpec((A_WIDTH, A_WIDTH), const2),
        pl.BlockSpec((tm, KV_WIDTH), lambda i: (i % per_seq, 0)),
        pl.BlockSpec((tm, KV_WIDTH), lambda i: (i % per_seq, 0)),
    ]
    out_specs = [
        pl.BlockSpec((tm, M_WIDTH), tok),
        pl.BlockSpec((tm, M_WIDTH), tok),
        pl.BlockSpec((tm, M_WIDTH), tok),
        pl.BlockSpec((tm, M_WIDTH), tok),
        pl.BlockSpec((tm, N_GATES), tok),
        pl.BlockSpec((N_GATES, tm), lambda i: (0, i)),
        pl.BlockSpec((A_HEADS, tm, KV_WIDTH), lambda i: (0, i, 0)),
        pl.BlockSpec((tm, KV_WIDTH), tok),
        pl.BlockSpec((tm, KV_WIDTH), tok),
    ]
    out_shape = [
        jax.ShapeDtypeStruct((n, M_WIDTH), BF16),
        jax.ShapeDtypeStruct((n, M_WIDTH), BF16),
        jax.ShapeDtypeStruct((n, M_WIDTH), BF16),
        jax.ShapeDtypeStruct((n, M_WIDTH), F32),
        jax.ShapeDtypeStruct((n, N_GATES), F32),
        jax.ShapeDtypeStruct((N_GATES, n), F32),
        jax.ShapeDtypeStruct((A_HEADS, n, KV_WIDTH), BF16),
        jax.ShapeDtypeStruct((n, KV_WIDTH), F32),
        jax.ShapeDtypeStruct((n, KV_WIDTH), F32),
    ]
    return pl.pallas_call(
        _inproj_kernel, grid=(n // tm,), in_specs=in_specs, out_specs=out_specs, out_shape=out_shape,
        compiler_params=pltpu.CompilerParams(dimension_semantics=("parallel",)),
        name="inproj",
    )(x2d, mod3, n1, w_main, w_g, w_gt, b_g, b_gt, qg_t, kg_t, bd, cos, sin)


def _mlstm_chain(q, k, v, li_c, lf_c, li_r, lf_r, caug, m, tri, tri_t, mask, reverse):
    L = q.shape[0]
    last = 0 if reverse else L - 1
    b_c = jnp.dot(tri, jnp.broadcast_to(lf_c, (L, L)), precision=HI, preferred_element_type=F32)
    b_r = jnp.dot(jnp.broadcast_to(lf_r, (8, L)), tri_t, precision=HI, preferred_element_type=F32)[0:1, :]
    a_inter = b_c[:, 0:1] + m
    d = jnp.where(mask, b_c - b_r + li_r, -jnp.inf)
    m_t = jnp.maximum(a_inter, jnp.max(d, axis=1, keepdims=True))
    w_inter = jnp.exp(a_inter - m_t)
    s = _dot_t(q, k) * jnp.exp(d - m_t)
    qc = jnp.dot(q, caug.astype(BF16), preferred_element_type=F32)
    num = jnp.dot(s.astype(BF16), v, preferred_element_type=F32) + w_inter * qc[:, 0:M_DH]
    den = jnp.sum(s, axis=1, keepdims=True) + w_inter * qc[:, M_DH:M_DH + 1]
    den = jnp.maximum(jnp.abs(den), jnp.exp(-m_t))
    h = num / den
    m_new = m_t[last:last + 1, :]
    b_last = b_c[last:last + 1, 0:1]
    g_c = jnp.exp(b_last - b_c[:, 0:1] + li_c - m_new)
    decay = jnp.exp(b_last + m - m_new)
    kw = (k.astype(F32) * g_c).astype(BF16)
    vaug = jnp.concatenate([v, jnp.ones_like(v)], axis=1)
    upd = lax.dot_general(kw, vaug, (((0,), (0,)), ((), ())), preferred_element_type=F32)
    return h, decay * caug + upd, m_new


def _mlstm_kernel(qf_ref, kf_ref, vf_ref, gcf_ref, grf_ref, qb_ref, kb_ref, vb_ref, gcb_ref, grb_ref,
                  c0_ref, m0_ref, hf_ref, hb_ref, cfin_ref, mfin_ref, c_scr, m_scr):
    c = pl.program_id(1)
    nc = pl.num_programs(1)
    L = qf_ref.shape[0]

    @pl.when(c == 0)
    def _():
        c_scr[...] = c0_ref[0]
        m_scr[...] = m0_ref[0]

    row = lax.broadcasted_iota(I32, (L, L), 0)
    col = lax.broadcasted_iota(I32, (L, L), 1)
    lower = row >= col
    upper = row <= col
    lower_f = lower.astype(F32)
    upper_f = upper.astype(F32)

    for direction in range(2):
        reverse = direction == 1
        q_ref, k_ref, v_ref, gc_ref, gr_ref, h_ref = (
            (qb_ref, kb_ref, vb_ref, gcb_ref, grb_ref, hb_ref) if reverse
            else (qf_ref, kf_ref, vf_ref, gcf_ref, grf_ref, hf_ref))
        tri, tri_t, mask = (upper_f, lower_f, upper) if reverse else (lower_f, upper_f, lower)
        gc = gc_ref[...]
        gr = gr_ref[...]
        for hd in range(M_HEADS):
            ch = direction * M_HEADS + hd
            sl = slice(hd * M_DH, (hd + 1) * M_DH)
            ci = 2 * direction * M_HEADS + hd
            cf = ci + M_HEADS
            h, caug, m_new = _mlstm_chain(
                q_ref[:, sl], k_ref[:, sl], v_ref[:, sl],
                gc[:, ci:ci + 1], gc[:, cf:cf + 1], gr[ci:ci + 1, :], gr[cf:cf + 1, :],
                c_scr[ch], m_scr[ch][0:1, 0:1], tri, tri_t, mask, reverse)
            h_ref[:, sl] = h
            c_scr[ch] = caug
            m_scr[ch] = jnp.broadcast_to(m_new, m_scr.shape[1:])

    @pl.when(c == nc - 1)
    def _():
        cfin_ref[0] = c_scr[...]
        mfin_ref[0] = m_scr[...]


def _mlstm_call(mq, mk, mv, gcol, grow, c0, m0, batch, seq):
    n = mq.shape[0]
    L = MLSTM_CHUNK
    nc = seq // L
    n_ch = 2 * M_HEADS

    def fwd(b, c):
        return (b * nc + c, 0)

    def bwd(b, c):
        return (b * nc + nc - 1 - c, 0)

    def fwd_t(b, c):
        return (0, b * nc + c)

    def bwd_t(b, c):
        return (0, b * nc + nc - 1 - c)

    tok = pl.BlockSpec((L, M_WIDTH), fwd)
    tok_b = pl.BlockSpec((L, M_WIDTH), bwd)
    in_specs = [tok, tok, tok, pl.BlockSpec((L, N_GATES), fwd), pl.BlockSpec((N_GATES, L), fwd_t),
                tok_b, tok_b, tok_b, pl.BlockSpec((L, N_GATES), bwd), pl.BlockSpec((N_GATES, L), bwd_t),
                pl.BlockSpec((1, n_ch, M_DH, 2 * M_DH), lambda b, c: (b, 0, 0, 0)),
                pl.BlockSpec((1, n_ch, 8, M_DH), lambda b, c: (b, 0, 0, 0))]
    out_specs = [tok, tok_b,
                 pl.BlockSpec((1, n_ch, M_DH, 2 * M_DH), lambda b, c: (b, 0, 0, 0)),
                 pl.BlockSpec((1, n_ch, 8, M_DH), lambda b, c: (b, 0, 0, 0))]
    out_shape = [jax.ShapeDtypeStruct((n, M_WIDTH), F32), jax.ShapeDtypeStruct((n, M_WIDTH), F32),
                 jax.ShapeDtypeStruct((batch, n_ch, M_DH, 2 * M_DH), F32),
                 jax.ShapeDtypeStruct((batch, n_ch, 8, M_DH), F32)]
    return pl.pallas_call(
        _mlstm_kernel, grid=(batch, nc), in_specs=in_specs, out_specs=out_specs, out_shape=out_shape,
        scratch_shapes=[pltpu.VMEM((n_ch, M_DH, 2 * M_DH), F32), pltpu.VMEM((n_ch, 8, M_DH), F32)],
        compiler_params=pltpu.CompilerParams(dimension_semantics=("parallel", "arbitrary")),
        name="mlstm",
    )(mq, mk, mv, gcol, grow, mq, mk, mv, gcol, grow, c0, m0)


def _sink_column(sink_ref, grp, rows_per_head):
    return jnp.concatenate(
        [jnp.full((rows_per_head, 1), sink_ref[grp * A_REP + r], F32) for r in range(A_REP)], axis=0)


def _store_heads(out_ref, o, grp, rows_per_head):
    for r in range(A_REP):
        hd = grp * A_REP + r
        out_ref[:, hd * A_DH:(hd + 1) * A_DH] = o[r * rows_per_head:(r + 1) * rows_per_head,
                                                  grp * A_DH:(grp + 1) * A_DH].astype(out_ref.dtype)


def _attn_ctx_kernel(sink_ref, q_ref, k_ref, v_ref, out_ref):
    s_len = k_ref.shape[0]
    k = k_ref[...].astype(BF16)
    v = v_ref[...].astype(BF16)
    for grp in range(A_KV):
        q = q_ref[grp * A_REP:(grp + 1) * A_REP].reshape(A_REP * s_len, KV_WIDTH)
        s = _dot_t(q, k)
        sk = _sink_column(sink_ref, grp, s_len)
        mx = jnp.maximum(jnp.max(s, axis=1, keepdims=True), sk)
        p = jnp.exp(s - mx)
        den = jnp.sum(p, axis=1, keepdims=True) + jnp.exp(sk - mx)
        o = jnp.dot(p.astype(BF16), v, preferred_element_type=F32) / den
        _store_heads(out_ref, o, grp, s_len)


def _attn_ctx_call(sink, aq, ak, av, batch, seq):
    n = ak.shape[0]
    return pl.pallas_call(
        _attn_ctx_kernel, grid=(batch,),
        in_specs=[pl.BlockSpec(memory_space=pltpu.SMEM),
                  pl.BlockSpec((A_HEADS, seq, KV_WIDTH), lambda b: (0, b, 0)),
                  pl.BlockSpec((seq, KV_WIDTH), lambda b: (b, 0)),
                  pl.BlockSpec((seq, KV_WIDTH), lambda b: (b, 0))],
        out_specs=pl.BlockSpec((seq, A_WIDTH), lambda b: (b, 0)),
        out_shape=jax.ShapeDtypeStruct((n, A_WIDTH), BF16),
        compiler_params=pltpu.CompilerParams(dimension_semantics=("parallel",)),
        name="attn_ctx",
    )(sink, aq, ak, av)


def _attn_lat_kernel(sink_ref, q_ref, kc_ref, vc_ref, kp_ref, kq_ref, kn_ref, vp_ref, vq_ref, vn_ref, out_ref):
    i = pl.program_id(1)
    nb = pl.num_programs(1)
    kc = kc_ref[0].astype(BF16)
    vc = vc_ref[0].astype(BF16)
    kp, kq, kn = kp_ref[...].astype(BF16), kq_ref[...].astype(BF16), kn_ref[...].astype(BF16)
    vp, vq, vn = vp_ref[...].astype(BF16), vq_ref[...].astype(BF16), vn_ref[...].astype(BF16)
    rows = A_REP * BLOCK
    qpos = lax.broadcasted_iota(I32, (rows, BLOCK), 0) % BLOCK
    kpos = lax.broadcasted_iota(I32, (rows, BLOCK), 1)
    mask_p = (kpos >= qpos) & (i > 0)
    mask_n = (kpos <= qpos) & (i < nb - 1)
    for grp in range(A_KV):
        q = q_ref[grp * A_REP:(grp + 1) * A_REP].reshape(rows, KV_WIDTH)
        s_c = _dot_t(q, kc)
        s_p = jnp.where(mask_p, _dot_t(q, kp), NEG)
        s_q = _dot_t(q, kq)
        s_n = jnp.where(mask_n, _dot_t(q, kn), NEG)
        sk = _sink_column(sink_ref, grp, BLOCK)
        mx = jnp.maximum(jnp.maximum(jnp.max(s_c, axis=1, keepdims=True), jnp.max(s_p, axis=1, keepdims=True)),
                         jnp.maximum(jnp.max(s_q, axis=1, keepdims=True), jnp.max(s_n, axis=1, keepdims=True)))
        mx = jnp.maximum(mx, sk)
        p_c, p_p, p_q, p_n = jnp.exp(s_c - mx), jnp.exp(s_p - mx), jnp.exp(s_q - mx), jnp.exp(s_n - mx)
        den = (jnp.sum(p_c, axis=1, keepdims=True) + jnp.sum(p_p, axis=1, keepdims=True)
               + jnp.sum(p_q, axis=1, keepdims=True) + jnp.sum(p_n, axis=1, keepdims=True) + jnp.exp(sk - mx))
        o = (jnp.dot(p_c.astype(BF16), vc, preferred_element_type=F32)
             + jnp.dot(p_p.astype(BF16), vp, preferred_element_type=F32)
             + jnp.dot(p_q.astype(BF16), vq, preferred_element_type=F32)
             + jnp.dot(p_n.astype(BF16), vn, preferred_element_type=F32)) / den
        _store_heads(out_ref, o, grp, BLOCK)


def _attn_lat_call(sink, aq, ak, av, kc, vc, batch, seq):
    n = ak.shape[0]
    nb = seq // BLOCK
    past = kc.shape[1]

    def cur(b, i):
        return (b * nb + i, 0)

    def prev(b, i):
        return (b * nb + jnp.maximum(i - 1, 0), 0)

    def nxt(b, i):
        return (b * nb + jnp.minimum(i + 1, nb - 1), 0)

    blk = functools.partial(pl.BlockSpec, (BLOCK, KV_WIDTH))
    cache = pl.BlockSpec((1, past, KV_WIDTH), lambda b, i: (b, 0, 0))
    return pl.pallas_call(
        _attn_lat_kernel, grid=(batch, nb),
        in_specs=[pl.BlockSpec(memory_space=pltpu.SMEM),
                  pl.BlockSpec((A_HEADS, BLOCK, KV_WIDTH), lambda b, i: (0, b * nb + i, 0)),
                  cache, cache, blk(prev), blk(cur), blk(nxt), blk(prev), blk(cur), blk(nxt)],
        out_specs=pl.BlockSpec((BLOCK, A_WIDTH), cur),
        out_shape=jax.ShapeDtypeStruct((n, A_WIDTH), BF16),
        compiler_params=pltpu.CompilerParams(dimension_semantics=("parallel", "parallel")),
        name="attn_lat",
    )(sink, aq, kc, vc, ak, ak, ak, av, av, av)


def _top16_rows(s, payload=None):
    n_rows = s.shape[0]
    rows = lax.broadcasted_iota(I32, s.shape, 0).astype(F32)
    vals, idxs, pays = [], [], []
    for _ in range(P_TOPK):
        mx = jnp.max(s, axis=0, keepdims=True)
        ix = jnp.min(jnp.where(s == mx, rows, float(n_rows)), axis=0, keepdims=True)
        hit = rows == ix
        vals.append(mx)
        idxs.append(ix)
        if payload is not None:
            pays.append(jnp.sum(jnp.where(hit, payload, 0.0), axis=0, keepdims=True))
        s = jnp.where(hit, -jnp.inf, s)
    out = (jnp.concatenate(vals, axis=0), jnp.concatenate(idxs, axis=0))
    if payload is not None:
        out += (jnp.concatenate(pays, axis=0),)
    return out


def _mix_kernel(x_ref, hf_ref, hb_ref, mo_ref, ao_ref, mod_ref, mhg_ref, n2_ref, wm_ref, wa_ref, wq_ref,
                sa_ref, sb_ref, x1_ref, h2_ref, eidx_ref, gate_ref, qp_scr, e_scr, g_scr):
    tm = x_ref.shape[0]
    hs = hf_ref[...] + hb_ref[...]
    parts = []
    for hd in range(M_HEADS):
        blk = hs[:, hd * M_DH:(hd + 1) * M_DH]
        parts.append(blk * lax.rsqrt(jnp.mean(blk * blk, axis=-1, keepdims=True) + EPS))
    m_out = _sigmoid(mo_ref[...]) * (jnp.concatenate(parts, axis=1) * mhg_ref[...])
    mix = (jnp.dot(m_out.astype(BF16), wm_ref[...], preferred_element_type=F32)
           + jnp.dot(ao_ref[...], wa_ref[...], preferred_element_type=F32))
    x1 = x_ref[...] + mod_ref[0, 2:3, :] * mix
    x1_ref[...] = x1
    h2 = x1 * lax.rsqrt(jnp.mean(x1 * x1, axis=-1, keepdims=True) + EPS) * n2_ref[...]
    h2 = h2 * (1.0 + mod_ref[0, 4:5, :]) + mod_ref[0, 3:4, :]
    h2_ref[...] = h2
    qp = jnp.dot(h2.astype(BF16), wq_ref[...], preferred_element_type=F32)
    for p in range(P_HEADS):
        qp_scr[p] = qp[:, p * P_DKEY:(p + 1) * P_DKEY].astype(BF16)
    sub_a = sa_ref[...]
    sub_b = sb_ref[...]

    def head_body(p, carry):
        for half in range(tm // N_KEYS):
            cols = slice(half * N_KEYS, (half + 1) * N_KEYS)
            qh = qp_scr[p, pl.ds(half * N_KEYS, N_KEYS), :]
            s_a = _dot_t(sub_a, qh[:, 0:P_HALF])
            s_b = _dot_t(sub_b, qh[:, P_HALF:P_DKEY])
            va, ia = _top16_rows(s_a)
            vb, ib = _top16_rows(s_b)
            cand = jnp.concatenate([va[i:i + 1, :] + vb for i in range(P_TOPK)], axis=0)
            cidx = jnp.concatenate([ia[i:i + 1, :] * float(N_KEYS) + ib for i in range(P_TOPK)], axis=0)
            top, _, eidx = _top16_rows(cand, cidx)
            ex = jnp.exp(top - jnp.max(top, axis=0, keepdims=True))
            gates = ex / jnp.sum(ex, axis=0, keepdims=True)
            r0 = pl.multiple_of(p * P_TOPK, P_TOPK)
            e_scr[pl.ds(r0, P_TOPK), cols] = eidx
            g_scr[pl.ds(r0, P_TOPK), cols] = gates
        return carry

    lax.fori_loop(0, P_HEADS, head_body, 0)
    for half in range(tm // N_KEYS):
        cols = slice(half * N_KEYS, (half + 1) * N_KEYS)
        eidx_ref[cols, :] = e_scr[:, cols].T.astype(I32)
        gate_ref[cols, :] = g_scr[:, cols].T


def _mix_call(x2d, seq, h_f, h_b, mo, a_out, mod3, mod_row0, mhg, n2, w_m, w_a, w_q, sub_a, sub_b):
    n = x2d.shape[0]
    tm = TOKEN_TILE
    per_seq = seq // tm

    def tok(i):
        return (i, 0)

    def const2(i):
        return (0, 0)

    in_specs = [
        pl.BlockSpec((tm, D_MODEL), tok),
        pl.BlockSpec((tm, M_WIDTH), tok), pl.BlockSpec((tm, M_WIDTH), tok), pl.BlockSpec((tm, M_WIDTH), tok),
        pl.BlockSpec((tm, A_WIDTH), tok),
        pl.BlockSpec((1, 6, D_MODEL), lambda i: (mod_row0 + (i // per_seq if mod_row0 else 0), 0, 0)),
        pl.BlockSpec((1, M_WIDTH), const2),
        pl.BlockSpec((1, D_MODEL), const2),
        pl.BlockSpec((M_WIDTH, D_MODEL), const2),
        pl.BlockSpec((A_WIDTH, D_MODEL), const2),
        pl.BlockSpec((D_MODEL, P_HEADS * P_DKEY), const2),
        pl.BlockSpec((N_KEYS, P_HALF), const2),
        pl.BlockSpec((N_KEYS, P_HALF), const2),
    ]
    out_specs = [pl.BlockSpec((tm, D_MODEL), tok), pl.BlockSpec((tm, D_MODEL), tok),
                 pl.BlockSpec((tm, N_SEL), tok), pl.BlockSpec((tm, N_SEL), tok)]
    out_shape = [jax.ShapeDtypeStruct((n, D_MODEL), F32), jax.ShapeDtypeStruct((n, D_MODEL), F32),
                 jax.ShapeDtypeStruct((n, N_SEL), I32), jax.ShapeDtypeStruct((n, N_SEL), F32)]
    return pl.pallas_call(
        _mix_kernel, grid=(n // tm,), in_specs=in_specs, out_specs=out_specs, out_shape=out_shape,
        scratch_shapes=[pltpu.VMEM((P_HEADS, tm, P_DKEY), BF16), pltpu.VMEM((N_SEL, tm), F32),
                        pltpu.VMEM((N_SEL, tm), F32)],
        compiler_params=pltpu.CompilerParams(dimension_semantics=("parallel",)),
        name="mix",
    )(x2d, h_f, h_b, mo, a_out, mod3, mhg, n2, w_m, w_a, w_q, sub_a, sub_b)


SC_LANES = 16
SC_CORES = 2
SC_SUBCORES = 16
SC_WORKERS = SC_CORES * SC_SUBCORES
SC_TOKENS = 8
SC_GROUP = SC_LANES
SC_NGROUPS = N_SEL // SC_GROUP
SC_XCHUNK = 4 * SC_LANES
SC_OCHUNK = 32 * SC_LANES
GELU_C0 = 0.7978845608028654
GELU_C1 = 0.044715


def _sc_gelu(a):
    z = GELU_C0 * (a + GELU_C1 * (a * a * a))
    tanh = 1.0 - 2.0 / (jnp.exp(2.0 * z) + 1.0)
    return 0.5 * a * (1.0 + tanh)


def _peer_sc_kernel(h2_hbm, eidx_hbm, gate_hbm, u_hbm, v_hbm, out_hbm,
                    xbuf, ibuf, gbuf, obuf, ubuf, vbuf, mbuf, wbuf, sem_u0, sem_u1, sem_v0, sem_v1):
    n = h2_hbm.shape[0]
    per_worker = n // SC_WORKERS
    wid = lax.axis_index("c") * SC_SUBCORES + lax.axis_index("s")
    sems_u = (sem_u0, sem_u1)
    sems_v = (sem_v0, sem_v1)
    lane = lax.iota(I32, SC_LANES)

    def split_item(item):
        return lax.shift_right_logical(item, SC_NGROUPS.bit_length() - 1), item & (SC_NGROUPS - 1)

    def gather_copies(item, slot):
        t, g = split_item(item)
        idx = ibuf[t, pl.ds(g * SC_GROUP, SC_GROUP)]
        return (pltpu.make_async_copy(u_hbm.at[idx], ubuf.at[slot], sems_u[slot]),
                pltpu.make_async_copy(v_hbm.at[idx], vbuf.at[slot], sems_v[slot]))

    def dots(t, slot):
        def sweep(jc, accs):
            base = jc * SC_XCHUNK
            xs = [xbuf[t, pl.ds(base + k * SC_LANES, SC_LANES)] for k in range(SC_XCHUNK // SC_LANES)]
            out = []
            for r in range(SC_GROUP):
                a = accs[r]
                for k in range(SC_XCHUNK // SC_LANES):
                    a = a + xs[k] * ubuf[slot, r, pl.ds(base + k * SC_LANES, SC_LANES)]
                out.append(a)
            return tuple(out)

        zero = jnp.zeros((SC_LANES,), F32)
        accs = lax.fori_loop(0, D_MODEL // SC_XCHUNK, sweep, tuple(zero for _ in range(SC_GROUP)))
        for r in range(SC_GROUP):
            mbuf[r, :] = accs[r]
        tot = zero
        for c in range(SC_LANES):
            tot = tot + plsc.load_gather(mbuf, [lane, jnp.full((SC_LANES,), c, I32)])
        return tot

    def accumulate(t, slot):
        for oc in range(D_MODEL // SC_OCHUNK):
            nv = SC_OCHUNK // SC_LANES
            accs = tuple(obuf[t, pl.ds(oc * SC_OCHUNK + j * SC_LANES, SC_LANES)] for j in range(nv))

            def row(r, accs):
                wb = plsc.load_gather(wbuf, [jnp.full((SC_LANES,), r, I32)])
                return tuple(accs[j] + wb * vbuf[slot, r, pl.ds(oc * SC_OCHUNK + j * SC_LANES, SC_LANES)]
                             for j in range(nv))

            accs = lax.fori_loop(0, SC_GROUP, row, accs)
            for j in range(nv):
                obuf[t, pl.ds(oc * SC_OCHUNK + j * SC_LANES, SC_LANES)] = accs[j]

    n_items = SC_TOKENS * SC_NGROUPS

    @pl.loop(0, per_worker // SC_TOKENS)
    def _(blk):
        tok0 = pl.multiple_of(wid * per_worker + blk * SC_TOKENS, SC_TOKENS)
        pltpu.sync_copy(h2_hbm.at[pl.ds(tok0, SC_TOKENS)], xbuf)
        pltpu.sync_copy(eidx_hbm.at[pl.ds(tok0, SC_TOKENS)], ibuf)
        pltpu.sync_copy(gate_hbm.at[pl.ds(tok0, SC_TOKENS)], gbuf)

        @pl.loop(0, SC_TOKENS)
        def _(t):
            zero = jnp.zeros((SC_LANES,), F32)
            for j in range(D_MODEL // SC_LANES):
                obuf[t, pl.ds(j * SC_LANES, SC_LANES)] = zero

        for c in gather_copies(0, 0):
            c.start()

        @pl.loop(0, n_items, step=2)
        def _(item0):
            for slot in range(2):
                item = item0 + slot
                t, g = split_item(item)

                @pl.when(item + 1 < n_items)
                def _():
                    for c in gather_copies(item + 1, 1 - slot):
                        c.start()

                for c in gather_copies(item, slot):
                    c.wait()
                a = dots(t, slot)
                wbuf[...] = gbuf[t, pl.ds(g * SC_GROUP, SC_GROUP)] * _sc_gelu(a)
                accumulate(t, slot)

        pltpu.sync_copy(obuf, out_hbm.at[pl.ds(tok0, SC_TOKENS)])


def _peer_experts(h2, eidx, gates, u_tab, v_tab):
    n = h2.shape[0]
    mesh = plsc.VectorSubcoreMesh(core_axis_name="c", subcore_axis_name="s")
    fn = pl.kernel(
        _peer_sc_kernel,
        out_type=jax.ShapeDtypeStruct((n, D_MODEL), F32),
        mesh=mesh,
        scratch_types=[
            pltpu.VMEM((SC_TOKENS, D_MODEL), F32),
            pltpu.VMEM((SC_TOKENS, N_SEL), I32),
            pltpu.VMEM((SC_TOKENS, N_SEL), F32),
            pltpu.VMEM((SC_TOKENS, D_MODEL), F32),
            pltpu.VMEM((2, SC_GROUP, D_MODEL), F32),
            pltpu.VMEM((2, SC_GROUP, D_MODEL), F32),
            pltpu.VMEM((SC_GROUP, SC_LANES), F32),
            pltpu.VMEM((SC_LANES,), F32),
            pltpu.SemaphoreType.DMA, pltpu.SemaphoreType.DMA, pltpu.SemaphoreType.DMA, pltpu.SemaphoreType.DMA,
        ],
        compiler_params=pltpu.CompilerParams(needs_layout_passes=False),
        name="peer_experts",
    )
    return fn(h2, eidx, gates, u_tab, v_tab)


def _resid_kernel(x1_ref, p_ref, mod_ref, o_ref):
    o_ref[...] = x1_ref[...] + mod_ref[0, 5:6, :] * p_ref[...]


def _resid_call(x1, peer_out, seq, mod3, mod_row0):
    n = x1.shape[0]
    tm = TOKEN_TILE
    per_seq = seq // tm
    tok = pl.BlockSpec((tm, D_MODEL), lambda i: (i, 0))
    return pl.pallas_call(
        _resid_kernel, grid=(n // tm,),
        in_specs=[tok, tok,
                  pl.BlockSpec((1, 6, D_MODEL), lambda i: (mod_row0 + (i // per_seq if mod_row0 else 0), 0, 0))],
        out_specs=tok, out_shape=jax.ShapeDtypeStruct((n, D_MODEL), F32),
        compiler_params=pltpu.CompilerParams(dimension_semantics=("parallel",)),
        name="resid",
    )(x1, peer_out, mod3)


def _rope_tables(seq, rotate):
    if not rotate:
        return jnp.ones((seq, KV_WIDTH), F32), jnp.zeros((seq, KV_WIDTH), F32)
    quarter = A_DH // 4
    t = jnp.arange(seq)
    row = (t // GRID_W).astype(F32)
    col = (t % GRID_W).astype(F32)
    inv = ROPE_BASE ** (-jnp.arange(quarter, dtype=F32) / quarter)
    d = jnp.arange(A_DH)
    pos = jnp.where(d[None, :] < A_DH // 2, row[:, None], col[:, None])
    ang = pos * inv[d % quarter][None, :]
    sign = jnp.where((d % (A_DH // 2)) < quarter, -1.0, 1.0).astype(F32)
    cos = jnp.cos(ang)
    sin = jnp.sin(ang) * sign[None, :]
    return jnp.tile(cos, (1, KV_WIDTH // A_DH)), jnp.tile(sin, (1, KV_WIDTH // A_DH))


def _layer(x, mod3, mod_row0, prm, cache, rotate):
    (n1, n2, w_main, w_g, w_gt, b_g, b_gt, mhg, qg_t, kg_t, bd, sink, w_m, w_a, w_q, sub_a, sub_b, u_tab, v_tab) = prm
    batch, seq, _ = x.shape
    n = batch * seq
    x2d = x.reshape(n, D_MODEL)
    cos, sin = _rope_tables(seq, rotate)
    mq, mk, mv, mo, gcol, grow, aq, ak, av = _inproj_call(
        x2d, seq, mod3, mod_row0, n1, w_main, w_g, w_gt, b_g, b_gt, qg_t, kg_t, bd, cos, sin)
    kc, vc, c0, m0 = cache
    h_f, h_b, c_fin, m_fin = _mlstm_call(mq, mk, mv, gcol, grow, c0, m0, batch, seq)
    if kc is None:
        a_out = _attn_ctx_call(sink, aq, ak, av, batch, seq)
    else:
        a_out = _attn_lat_call(sink, aq, ak, av, kc, vc, batch, seq)
    routed = _mix_call(x2d, seq, h_f, h_b, mo, a_out, mod3, mod_row0, mhg, n2, w_m, w_a, w_q, sub_a, sub_b)
    return routed, ak, av, c_fin, m_fin


def _layer_experts(routed, shape, mod3, mod_row0, prm):
    x1, h2, eidx, gates = routed
    batch, seq, _ = shape
    peer_out = _peer_experts(h2, eidx, gates, prm[-2], prm[-1])
    return _resid_call(x1, peer_out, seq, mod3, mod_row0).reshape(batch, seq, D_MODEL)


def _pack_state(C, n_vec, m):
    b = C.shape[0]
    caug = jnp.concatenate([C, jnp.broadcast_to(n_vec[..., None], C.shape)], axis=-1)
    caug = caug.reshape(b, 2 * M_HEADS, M_DH, 2 * M_DH)
    m_rep = jnp.broadcast_to(m.reshape(b, 2 * M_HEADS, 1, 1), (b, 2 * M_HEADS, 8, M_DH))
    return caug.astype(F32), m_rep.astype(F32)


def kernel(x_prompt, x_sample, c, cache_attn_k, cache_attn_v, state_mlstm_C, state_mlstm_n, state_mlstm_m,
           c_ctx, w_ada, b_ada, norm1_g, norm2_g, w_in, b_gates, mh_norm_g, q_norm_g, k_norm_g, sink_logits,
           w_out, peer_w_q, peer_sub_a, peer_sub_b, peer_u, peer_v):
    depth = w_ada.shape[0]
    assert depth == 1
    batch, seq, _ = x_prompt.shape
    dec_batch, dec_seq, _ = x_sample.shape
    l = 0

    cond = jnp.concatenate([c_ctx[None, :], c, jnp.zeros((MOD_ROWS - 1 - dec_batch, D_MODEL), F32)], axis=0)
    mod3 = _ada_call(cond, w_ada[l], b_ada[l]).reshape(MOD_ROWS, 6, D_MODEL)

    wi = w_in[l]
    g0 = 4 * M_WIDTH
    w_main = jnp.concatenate([wi[:, :g0], wi[:, g0 + N_GATES:]], axis=1).astype(BF16)
    w_g = wi[:, g0:g0 + N_GATES]
    seg = jnp.arange(A_WIDTH) // A_DH
    bd = jnp.where(seg[:, None] == seg[None, :], 1.0 / A_DH, 0.0).astype(F32)
    prm = (norm1_g[l][None, :], norm2_g[l][None, :], w_main, w_g, w_g.T, b_gates[l][None, :], b_gates[l][:, None],
           mh_norm_g[l][None, :], jnp.tile(q_norm_g[l], A_HEADS)[None, :], jnp.tile(k_norm_g[l], A_KV)[None, :], bd,
           sink_logits[l], w_out[l][:M_WIDTH].astype(BF16), w_out[l][M_WIDTH:].astype(BF16),
           peer_w_q[l].astype(BF16), peer_sub_a[l].astype(BF16), peer_sub_b[l].astype(BF16), peer_u[l], peer_v[l])

    zeros_c = jnp.zeros((batch, 2, M_HEADS, M_DH, M_DH), F32)
    c0, m0 = _pack_state(zeros_c, zeros_c[..., 0], jnp.full((batch, 2, M_HEADS), NEG, F32))
    routed_p, k_new, v_new, c_fin, m_fin = _layer(x_prompt, mod3, 0, prm, (None, None, c0, m0), False)
    routed_p, x_sample = lax.optimization_barrier((routed_p, x_sample))
    y_p = _layer_experts(routed_p, x_prompt.shape, mod3, 0, prm)

    c0s, m0s = _pack_state(state_mlstm_C[:, l], state_mlstm_n[:, l], state_mlstm_m[:, l])
    past = cache_attn_k.shape[2]
    kc = cache_attn_k[:, l].reshape(dec_batch, past, KV_WIDTH)
    vc = cache_attn_v[:, l].reshape(dec_batch, past, KV_WIDTH)
    routed_s, _, _, _, _ = _layer(x_sample, mod3, 1, prm, (kc, vc, c0s, m0s), True)
    y_s = _layer_experts(routed_s, x_sample.shape, mod3, 1, prm)

    c_fin = c_fin.reshape(batch, 2, M_HEADS, M_DH, 2 * M_DH)
    new_c = c_fin[..., :M_DH][:, None]
    new_n = c_fin[..., M_DH][:, None]
    new_m = m_fin[:, :, 0, 0].reshape(batch, 2, M_HEADS)[:, None]
    new_k = k_new.reshape(batch, 1, seq, A_KV, A_DH)
    new_v = v_new.reshape(batch, 1, seq, A_KV, A_DH)
    return y_p, y_s, new_k, new_v, new_c, new_n, new_m
```

```python
import functools

import jax
import jax.numpy as jnp
from jax import lax
from jax.experimental import pallas as pl
from jax.experimental.pallas import tpu as pltpu
from jax.experimental.pallas import tpu_sc as plsc

F32 = jnp.float32
BF16 = jnp.bfloat16
I32 = jnp.int32
HI = lax.Precision.HIGHEST

D_MODEL = 1024
EPS = 1e-6
NEG = -1e30
GRID_W = 64
M_HEADS = 4
M_WIDTH = 512
M_DH = 128
A_HEADS = 8
A_KV = 2
A_REP = 4
A_DH = 64
A_WIDTH = 512
KV_WIDTH = A_KV * A_DH
BLOCK = 128
ROPE_BASE = 10000.0
N_KEYS = 128
P_HEADS = 8
P_DKEY = 256
P_HALF = 128
P_TOPK = 16
N_SEL = P_HEADS * P_TOPK
N_GATES = 4 * M_HEADS
MAIN_COLS = 4 * M_WIDTH + A_WIDTH + 2 * KV_WIDTH
MOD_ROWS = 16

TOKEN_TILE = 256
MLSTM_CHUNK = 128
ADA_COL_TILE = 768
LATENT_CHUNKS = 4


def _sigmoid(x):
    return 1.0 / (1.0 + jnp.exp(-x))


def _log_sigmoid(x):
    return jnp.minimum(x, 0.0) - jnp.log1p(jnp.exp(-jnp.abs(x)))


def _dot_t(a, b, precision=None):
    return lax.dot_general(a, b, (((1,), (1,)), ((), ())), precision=precision,
                           preferred_element_type=F32)


def _ada_kernel(c_ref, w_ref, b_ref, o_ref):
    c = c_ref[...]
    s = c * _sigmoid(c)
    o_ref[...] = jnp.dot(s, w_ref[...], precision=HI, preferred_element_type=F32) + b_ref[...]


def _ada_call(cond, w_ada, b_ada):
    n_out = w_ada.shape[1]
    return pl.pallas_call(
        _ada_kernel,
        grid=(n_out // ADA_COL_TILE,),
        in_specs=[pl.BlockSpec((MOD_ROWS, D_MODEL), lambda j: (0, 0)),
                  pl.BlockSpec((D_MODEL, ADA_COL_TILE), lambda j: (0, j)),
                  pl.BlockSpec((1, ADA_COL_TILE), lambda j: (0, j))],
        out_specs=pl.BlockSpec((MOD_ROWS, ADA_COL_TILE), lambda j: (0, j)),
        out_shape=jax.ShapeDtypeStruct((MOD_ROWS, n_out), F32),
        name="ada",
    )(cond, w_ada, b_ada.reshape(1, n_out))


def _swap16(x):
    n = x.shape[-1]
    lane = lax.broadcasted_iota(I32, x.shape, x.ndim - 1)
    return jnp.where((lane & 16) == 0, pltpu.roll(x, n - 16, x.ndim - 1), pltpu.roll(x, 16, x.ndim - 1))


def _inproj_kernel(x_ref, mod_ref, n1_ref, w_ref, wg_ref, wgt_ref, bg_ref, bgt_ref, qg_ref, kg_ref,
                   bd_ref, cos_ref, sin_ref,
                   mq_ref, mk_ref, mv_ref, mo_ref, gc_ref, gr_ref, aq_ref, ak_ref, av_ref):
    x = x_ref[...]
    h = x * lax.rsqrt(jnp.mean(x * x, axis=-1, keepdims=True) + EPS) * n1_ref[...]
    h = h * (1.0 + mod_ref[0, 1:2, :]) + mod_ref[0, 0:1, :]
    z = jnp.dot(h.astype(BF16), w_ref[...], preferred_element_type=F32)

    mq_ref[...] = (z[:, 0:M_WIDTH] * (M_DH ** -0.5)).astype(BF16)
    mk_ref[...] = z[:, M_WIDTH:2 * M_WIDTH].astype(BF16)
    mv_ref[...] = z[:, 2 * M_WIDTH:3 * M_WIDTH].astype(BF16)
    mo_ref[...] = z[:, 3 * M_WIDTH:4 * M_WIDTH]

    g = jnp.dot(h, wg_ref[...], precision=HI, preferred_element_type=F32) + bg_ref[...]
    kind = lax.broadcasted_iota(I32, g.shape, 1) // M_HEADS
    gc_ref[...] = jnp.where((kind & 1) == 1, _log_sigmoid(g), g)
    gt = _dot_t(wgt_ref[...], h, precision=HI) + bgt_ref[...]
    kind_t = lax.broadcasted_iota(I32, gt.shape, 0) // M_HEADS
    gr_ref[...] = jnp.where((kind_t & 1) == 1, _log_sigmoid(gt), gt)

    o = 4 * M_WIDTH
    aq = z[:, o:o + A_WIDTH]
    ak = z[:, o + A_WIDTH:o + A_WIDTH + KV_WIDTH]
    av_ref[...] = z[:, o + A_WIDTH + KV_WIDTH:o + A_WIDTH + 2 * KV_WIDTH]
    bd = bd_ref[...]
    cos = cos_ref[...]
    sin = sin_ref[...]
    aq = aq * lax.rsqrt(jnp.dot(aq * aq, bd, precision=HI, preferred_element_type=F32) + EPS) * qg_ref[...]
    cos4 = jnp.concatenate([cos] * (A_WIDTH // KV_WIDTH), axis=1)
    sin4 = jnp.concatenate([sin] * (A_WIDTH // KV_WIDTH), axis=1)
    aq = (aq * cos4 + _swap16(aq) * sin4) * (A_DH ** -0.5)
    ak = ak * lax.rsqrt(jnp.dot(ak * ak, bd[0:KV_WIDTH, 0:KV_WIDTH], precision=HI,
                                preferred_element_type=F32) + EPS) * kg_ref[...]
    ak_ref[...] = ak * cos + _swap16(ak) * sin

    lane = lax.broadcasted_iota(I32, (aq.shape[0], KV_WIDTH), 1)
    for hd in range(A_HEADS):
        grp = hd // A_REP
        blk = aq[:, (hd // 2) * KV_WIDTH:(hd // 2 + 1) * KV_WIDTH]
        if hd % 2 != grp:
            blk = pltpu.roll(blk, A_DH, 1)
        keep = (lane >= grp * A_DH) & (lane < (grp + 1) * A_DH)
        aq_ref[hd] = jnp.where(keep, blk, 0.0).astype(BF16)


def _inproj_call(x2d, seq, mod3, mod_row0, n1, w_main, w_g, w_gt, b_g, b_gt, qg_t, kg_t, bd, cos, sin):
    n = x2d.shape[0]
    tm = TOKEN_TILE
    per_seq = seq // tm

    def tok(i):
        return (i, 0)

    def const2(i):
        return (0, 0)

    in_specs = [
        pl.BlockSpec((tm, D_MODEL), tok),
        pl.BlockSpec((1, 6, D_MODEL), lambda i: (mod_row0 + (i // per_seq if mod_row0 else 0), 0, 0)),
        pl.BlockSpec((1, D_MODEL), const2),
        pl.BlockSpec((D_MODEL, MAIN_COLS), const2),
        pl.BlockSpec((D_MODEL, N_GATES), const2),
        pl.BlockSpec((N_GATES, D_MODEL), const2),
        pl.BlockSpec((1, N_GATES), const2),
        pl.BlockSpec((N_GATES, 1), const2),
        pl.BlockSpec((1, A_WIDTH), const2),
        pl.BlockSpec((1, KV_WIDTH), const2),
        pl.BlockSpec((A_WIDTH, A_WIDTH), const2),
        pl.BlockSpec((tm, KV_WIDTH), lambda i: (i % per_seq, 0)),
        pl.BlockSpec((tm, KV_WIDTH), lambda i: (i % per_seq, 0)),
    ]
    out_specs = [
        pl.BlockSpec((tm, M_WIDTH), tok),
        pl.BlockSpec((tm, M_WIDTH), tok),
        pl.BlockSpec((tm, M_WIDTH), tok),
        pl.BlockSpec((tm, M_WIDTH), tok),
        pl.BlockSpec((tm, N_GATES), tok),
        pl.BlockSpec((N_GATES, tm), lambda i: (0, i)),
        pl.BlockSpec((A_HEADS, tm, KV_WIDTH), lambda i: (0, i, 0)),
        pl.BlockSpec((tm, KV_WIDTH), tok),
        pl.BlockSpec((tm, KV_WIDTH), tok),
    ]
    out_shape = [
        jax.ShapeDtypeStruct((n, M_WIDTH), BF16),
        jax.ShapeDtypeStruct((n, M_WIDTH), BF16),
        jax.ShapeDtypeStruct((n, M_WIDTH), BF16),
        jax.ShapeDtypeStruct((n, M_WIDTH), F32),
        jax.ShapeDtypeStruct((n, N_GATES), F32),
        jax.ShapeDtypeStruct((N_GATES, n), F32),
        jax.ShapeDtypeStruct((A_HEADS, n, KV_WIDTH), BF16),
        jax.ShapeDtypeStruct((n, KV_WIDTH), F32),
        jax.ShapeDtypeStruct((n, KV_WIDTH), F32),
    ]
    return pl.pallas_call(
        _inproj_kernel, grid=(n // tm,), in_specs=in_specs, out_specs=out_specs, out_shape=out_shape,
        compiler_params=pltpu.CompilerParams(dimension_semantics=("parallel",)),
        name="inproj",
    )(x2d, mod3, n1, w_main, w_g, w_gt, b_g, b_gt, qg_t, kg_t, bd, cos, sin)


def _mlstm_chain(q, k, v, li_c, lf_c, li_r, lf_r, caug, m, tri, tri_t, mask, reverse):
    L = q.shape[0]
    last = 0 if reverse else L - 1
    b_c = jnp.dot(tri, jnp.broadcast_to(lf_c, (L, L)), precision=HI, preferred_element_type=F32)
    b_r = jnp.dot(jnp.broadcast_to(lf_r, (8, L)), tri_t, precision=HI, preferred_element_type=F32)[0:1, :]
    a_inter = b_c[:, 0:1] + m
    d = jnp.where(mask, b_c - b_r + li_r, -jnp.inf)
    m_t = jnp.maximum(a_inter, jnp.max(d, axis=1, keepdims=True))
    w_inter = jnp.exp(a_inter - m_t)
    s = _dot_t(q, k) * jnp.exp(d - m_t)
    qc = jnp.dot(q, caug.astype(BF16), preferred_element_type=F32)
    num = jnp.dot(s.astype(BF16), v, preferred_element_type=F32) + w_inter * qc[:, 0:M_DH]
    den = jnp.sum(s, axis=1, keepdims=True) + w_inter * qc[:, M_DH:M_DH + 1]
    den = jnp.maximum(jnp.abs(den), jnp.exp(-m_t))
    h = num / den
    m_new = m_t[last:last + 1, :]
    b_last = b_c[last:last + 1, 0:1]
    g_c = jnp.exp(b_last - b_c[:, 0:1] + li_c - m_new)
    decay = jnp.exp(b_last + m - m_new)
    kw = (k.astype(F32) * g_c).astype(BF16)
    vaug = jnp.concatenate([v, jnp.ones_like(v)], axis=1)
    upd = lax.dot_general(kw, vaug, (((0,), (0,)), ((), ())), preferred_element_type=F32)
    return h, decay * caug + upd, m_new


def _mlstm_kernel(qf_ref, kf_ref, vf_ref, gcf_ref, grf_ref, qb_ref, kb_ref, vb_ref, gcb_ref, grb_ref,
                  c0_ref, m0_ref, hf_ref, hb_ref, cfin_ref, mfin_ref, c_scr, m_scr):
    c = pl.program_id(1)
    nc = pl.num_programs(1)
    L = qf_ref.shape[0]

    @pl.when(c == 0)
    def _():
        c_scr[...] = c0_ref[0]
        m_scr[...] = m0_ref[0]

    row = lax.broadcasted_iota(I32, (L, L), 0)
    col = lax.broadcasted_iota(I32, (L, L), 1)
    lower = row >= col
    upper = row <= col
    lower_f = lower.astype(F32)
    upper_f = upper.astype(F32)

    for direction in range(2):
        reverse = direction == 1
        q_ref, k_ref, v_ref, gc_ref, gr_ref, h_ref = (
            (qb_ref, kb_ref, vb_ref, gcb_ref, grb_ref, hb_ref) if reverse
            else (qf_ref, kf_ref, vf_ref, gcf_ref, grf_ref, hf_ref))
        tri, tri_t, mask = (upper_f, lower_f, upper) if reverse else (lower_f, upper_f, lower)
        gc = gc_ref[...]
        gr = gr_ref[...]
        for hd in range(M_HEADS):
            ch = direction * M_HEADS + hd
            sl = slice(hd * M_DH, (hd + 1) * M_DH)
            ci = 2 * direction * M_HEADS + hd
            cf = ci + M_HEADS
            h, caug, m_new = _mlstm_chain(
                q_ref[:, sl], k_ref[:, sl], v_ref[:, sl],
                gc[:, ci:ci + 1], gc[:, cf:cf + 1], gr[ci:ci + 1, :], gr[cf:cf + 1, :],
                c_scr[ch], m_scr[ch][0:1, 0:1], tri, tri_t, mask, reverse)
            h_ref[:, sl] = h
            c_scr[ch] = caug
            m_scr[ch] = jnp.broadcast_to(m_new, m_scr.shape[1:])

    @pl.when(c == nc - 1)
    def _():
        cfin_ref[0] = c_scr[...]
        mfin_ref[0] = m_scr[...]


def _mlstm_call(mq, mk, mv, gcol, grow, c0, m0, batch, seq):
    n = mq.shape[0]
    L = MLSTM_CHUNK
    nc = seq // L
    n_ch = 2 * M_HEADS

    def fwd(b, c):
        return (b * nc + c, 0)

    def bwd(b, c):
        return (b * nc + nc - 1 - c, 0)

    def fwd_t(b, c):
        return (0, b * nc + c)

    def bwd_t(b, c):
        return (0, b * nc + nc - 1 - c)

    tok = pl.BlockSpec((L, M_WIDTH), fwd)
    tok_b = pl.BlockSpec((L, M_WIDTH), bwd)
    in_specs = [tok, tok, tok, pl.BlockSpec((L, N_GATES), fwd), pl.BlockSpec((N_GATES, L), fwd_t),
                tok_b, tok_b, tok_b, pl.BlockSpec((L, N_GATES), bwd), pl.BlockSpec((N_GATES, L), bwd_t),
                pl.BlockSpec((1, n_ch, M_DH, 2 * M_DH), lambda b, c: (b, 0, 0, 0)),
                pl.BlockSpec((1, n_ch, 8, M_DH), lambda b, c: (b, 0, 0, 0))]
    out_specs = [tok, tok_b,
                 pl.BlockSpec((1, n_ch, M_DH, 2 * M_DH), lambda b, c: (b, 0, 0, 0)),
                 pl.BlockSpec((1, n_ch, 8, M_DH), lambda b, c: (b, 0, 0, 0))]
    out_shape = [jax.ShapeDtypeStruct((n, M_WIDTH), F32), jax.ShapeDtypeStruct((n, M_WIDTH), F32),
                 jax.ShapeDtypeStruct((batch, n_ch, M_DH, 2 * M_DH), F32),
                 jax.ShapeDtypeStruct((batch, n_ch, 8, M_DH), F32)]
    return pl.pallas_call(
        _mlstm_kernel, grid=(batch, nc), in_specs=in_specs, out_specs=out_specs, out_shape=out_shape,
        scratch_shapes=[pltpu.VMEM((n_ch, M_DH, 2 * M_DH), F32), pltpu.VMEM((n_ch, 8, M_DH), F32)],
        compiler_params=pltpu.CompilerParams(dimension_semantics=("parallel", "arbitrary")),
        name="mlstm",
    )(mq, mk, mv, gcol, grow, mq, mk, mv, gcol, grow, c0, m0)


def _sink_column(sink_ref, grp, rows_per_head):
    return jnp.concatenate(
        [jnp.full((rows_per_head, 1), sink_ref[grp * A_REP + r], F32) for r in range(A_REP)], axis=0)


def _store_heads(out_ref, o, grp, rows_per_head):
    for r in range(A_REP):
        hd = grp * A_REP + r
        out_ref[:, hd * A_DH:(hd + 1) * A_DH] = o[r * rows_per_head:(r + 1) * rows_per_head,
                                                  grp * A_DH:(grp + 1) * A_DH].astype(out_ref.dtype)


def _attn_ctx_kernel(sink_ref, q_ref, k_ref, v_ref, out_ref):
    s_len = k_ref.shape[0]
    k = k_ref[...].astype(BF16)
    v = v_ref[...].astype(BF16)
    for grp in range(A_KV):
        q = q_ref[grp * A_REP:(grp + 1) * A_REP].reshape(A_REP * s_len, KV_WIDTH)
        s = _dot_t(q, k)
        sk = _sink_column(sink_ref, grp, s_len)
        mx = jnp.maximum(jnp.max(s, axis=1, keepdims=True), sk)
        p = jnp.exp(s - mx)
        den = jnp.sum(p, axis=1, keepdims=True) + jnp.exp(sk - mx)
        o = jnp.dot(p.astype(BF16), v, preferred_element_type=F32) / den
        _store_heads(out_ref, o, grp, s_len)


def _attn_ctx_call(sink, aq, ak, av, batch, seq):
    n = ak.shape[0]
    return pl.pallas_call(
        _attn_ctx_kernel, grid=(batch,),
        in_specs=[pl.BlockSpec(memory_space=pltpu.SMEM),
                  pl.BlockSpec((A_HEADS, seq, KV_WIDTH), lambda b: (0, b, 0)),
                  pl.BlockSpec((seq, KV_WIDTH), lambda b: (b, 0)),
                  pl.BlockSpec((seq, KV_WIDTH), lambda b: (b, 0))],
        out_specs=pl.BlockSpec((seq, A_WIDTH), lambda b: (b, 0)),
        out_shape=jax.ShapeDtypeStruct((n, A_WIDTH), BF16),
        compiler_params=pltpu.CompilerParams(dimension_semantics=("parallel",)),
        name="attn_ctx",
    )(sink, aq, ak, av)


def _attn_lat_kernel(sink_ref, q_ref, kc_ref, vc_ref, kp_ref, kq_ref, kn_ref, vp_ref, vq_ref, vn_ref, out_ref):
    i = pl.program_id(1)
    nb = pl.num_programs(1)
    kc = kc_ref[0].astype(BF16)
    vc = vc_ref[0].astype(BF16)
    kp, kq, kn = kp_ref[...].astype(BF16), kq_ref[...].astype(BF16), kn_ref[...].astype(BF16)
    vp, vq, vn = vp_ref[...].astype(BF16), vq_ref[...].astype(BF16), vn_ref[...].astype(BF16)
    rows = A_REP * BLOCK
    qpos = lax.broadcasted_iota(I32, (rows, BLOCK), 0) % BLOCK
    kpos = lax.broadcasted_iota(I32, (rows, BLOCK), 1)
    mask_p = (kpos >= qpos) & (i > 0)
    mask_n = (kpos <= qpos) & (i < nb - 1)
    for grp in range(A_KV):
        q = q_ref[grp * A_REP:(grp + 1) * A_REP].reshape(rows, KV_WIDTH)
        s_c = _dot_t(q, kc)
        s_p = jnp.where(mask_p, _dot_t(q, kp), NEG)
        s_q = _dot_t(q, kq)
        s_n = jnp.where(mask_n, _dot_t(q, kn), NEG)
        sk = _sink_column(sink_ref, grp, BLOCK)
        mx = jnp.maximum(jnp.maximum(jnp.max(s_c, axis=1, keepdims=True), jnp.max(s_p, axis=1, keepdims=True)),
                         jnp.maximum(jnp.max(s_q, axis=1, keepdims=True), jnp.max(s_n, axis=1, keepdims=True)))
        mx = jnp.maximum(mx, sk)
        p_c, p_p, p_q, p_n = jnp.exp(s_c - mx), jnp.exp(s_p - mx), jnp.exp(s_q - mx), jnp.exp(s_n - mx)
        den = (jnp.sum(p_c, axis=1, keepdims=True) + jnp.sum(p_p, axis=1, keepdims=True)
               + jnp.sum(p_q, axis=1, keepdims=True) + jnp.sum(p_n, axis=1, keepdims=True) + jnp.exp(sk - mx))
        o = (jnp.dot(p_c.astype(BF16), vc, preferred_element_type=F32)
             + jnp.dot(p_p.astype(BF16), vp, preferred_element_type=F32)
             + jnp.dot(p_q.astype(BF16), vq, preferred_element_type=F32)
             + jnp.dot(p_n.astype(BF16), vn, preferred_element_type=F32)) / den
        _store_heads(out_ref, o, grp, BLOCK)


def _attn_lat_call(sink, aq, ak, av, kc, vc, batch, seq):
    n = ak.shape[0]
    nb = seq // BLOCK
    past = kc.shape[1]

    def cur(b, i):
        return (b * nb + i, 0)

    def prev(b, i):
        return (b * nb + jnp.maximum(i - 1, 0), 0)

    def nxt(b, i):
        return (b * nb + jnp.minimum(i + 1, nb - 1), 0)

    blk = functools.partial(pl.BlockSpec, (BLOCK, KV_WIDTH))
    cache = pl.BlockSpec((1, past, KV_WIDTH), lambda b, i: (b, 0, 0))
    return pl.pallas_call(
        _attn_lat_kernel, grid=(batch, nb),
        in_specs=[pl.BlockSpec(memory_space=pltpu.SMEM),
                  pl.BlockSpec((A_HEADS, BLOCK, KV_WIDTH), lambda b, i: (0, b * nb + i, 0)),
                  cache, cache, blk(prev), blk(cur), blk(nxt), blk(prev), blk(cur), blk(nxt)],
        out_specs=pl.BlockSpec((BLOCK, A_WIDTH), cur),
        out_shape=jax.ShapeDtypeStruct((n, A_WIDTH), BF16),
        compiler_params=pltpu.CompilerParams(dimension_semantics=("parallel", "parallel")),
        name="attn_lat",
    )(sink, aq, kc, vc, ak, ak, ak, av, av, av)


def _top16_rows(s, payload=None):
    n_rows = s.shape[0]
    rows = lax.broadcasted_iota(I32, s.shape, 0).astype(F32)
    vals, idxs, pays = [], [], []
    for _ in range(P_TOPK):
        mx = jnp.max(s, axis=0, keepdims=True)
        ix = jnp.min(jnp.where(s == mx, rows, float(n_rows)), axis=0, keepdims=True)
        hit = rows == ix
        vals.append(mx)
        idxs.append(ix)
        if payload is not None:
            pays.append(jnp.sum(jnp.where(hit, payload, 0.0), axis=0, keepdims=True))
        s = jnp.where(hit, -jnp.inf, s)
    out = (jnp.concatenate(vals, axis=0), jnp.concatenate(idxs, axis=0))
    if payload is not None:
        out += (jnp.concatenate(pays, axis=0),)
    return out


def _mix_kernel(x_ref, hf_ref, hb_ref, mo_ref, ao_ref, mod_ref, mhg_ref, n2_ref, wm_ref, wa_ref, wq_ref,
                sa_ref, sb_ref, x1_ref, h2_ref, eidx_ref, gate_ref, qp_scr, e_scr, g_scr):
    tm = x_ref.shape[0]
    hs = hf_ref[...] + hb_ref[...]
    parts = []
    for hd in range(M_HEADS):
        blk = hs[:, hd * M_DH:(hd + 1) * M_DH]
        parts.append(blk * lax.rsqrt(jnp.mean(blk * blk, axis=-1, keepdims=True) + EPS))
    m_out = _sigmoid(mo_ref[...]) * (jnp.concatenate(parts, axis=1) * mhg_ref[...])
    mix = (jnp.dot(m_out.astype(BF16), wm_ref[...], preferred_element_type=F32)
           + jnp.dot(ao_ref[...], wa_ref[...], preferred_element_type=F32))
    x1 = x_ref[...] + mod_ref[0, 2:3, :] * mix
    x1_ref[...] = x1
    h2 = x1 * lax.rsqrt(jnp.mean(x1 * x1, axis=-1, keepdims=True) + EPS) * n2_ref[...]
    h2 = h2 * (1.0 + mod_ref[0, 4:5, :]) + mod_ref[0, 3:4, :]
    h2_ref[...] = h2
    qp = jnp.dot(h2.astype(BF16), wq_ref[...], preferred_element_type=F32)
    for p in range(P_HEADS):
        qp_scr[p] = qp[:, p * P_DKEY:(p + 1) * P_DKEY].astype(BF16)
    sub_a = sa_ref[...]
    sub_b = sb_ref[...]

    def head_body(p, carry):
        for half in range(tm // N_KEYS):
            cols = slice(half * N_KEYS, (half + 1) * N_KEYS)
            qh = qp_scr[p, pl.ds(half * N_KEYS, N_KEYS), :]
            s_a = _dot_t(sub_a, qh[:, 0:P_HALF])
            s_b = _dot_t(sub_b, qh[:, P_HALF:P_DKEY])
            va, ia = _top16_rows(s_a)
            vb, ib = _top16_rows(s_b)
            cand = jnp.concatenate([va[i:i + 1, :] + vb for i in range(P_TOPK)], axis=0)
            cidx = jnp.concatenate([ia[i:i + 1, :] * float(N_KEYS) + ib for i in range(P_TOPK)], axis=0)
            top, _, eidx = _top16_rows(cand, cidx)
            ex = jnp.exp(top - jnp.max(top, axis=0, keepdims=True))
            gates = ex / jnp.sum(ex, axis=0, keepdims=True)
            r0 = pl.multiple_of(p * P_TOPK, P_TOPK)
            e_scr[pl.ds(r0, P_TOPK), cols] = eidx
            g_scr[pl.ds(r0, P_TOPK), cols] = gates
        return carry

    lax.fori_loop(0, P_HEADS, head_body, 0)
    for half in range(tm // N_KEYS):
        cols = slice(half * N_KEYS, (half + 1) * N_KEYS)
        eidx_ref[cols, :] = e_scr[:, cols].T.astype(I32)
        gate_ref[cols, :] = g_scr[:, cols].T


def _mix_call(x2d, seq, h_f, h_b, mo, a_out, mod3, mod_row0, mhg, n2, w_m, w_a, w_q, sub_a, sub_b, tile0, n_tiles):
    tm = TOKEN_TILE
    per_seq = seq // tm
    n = n_tiles * tm

    def tok_in(i):
        return (tile0 + i, 0)

    def tok(i):
        return (i, 0)

    def const2(i):
        return (0, 0)

    in_specs = [
        pl.BlockSpec((tm, D_MODEL), tok_in),
        pl.BlockSpec((tm, M_WIDTH), tok_in), pl.BlockSpec((tm, M_WIDTH), tok_in),
        pl.BlockSpec((tm, M_WIDTH), tok_in),
        pl.BlockSpec((tm, A_WIDTH), tok_in),
        pl.BlockSpec((1, 6, D_MODEL), lambda i: (mod_row0 + ((tile0 + i) // per_seq if mod_row0 else 0), 0, 0)),
        pl.BlockSpec((1, M_WIDTH), const2),
        pl.BlockSpec((1, D_MODEL), const2),
        pl.BlockSpec((M_WIDTH, D_MODEL), const2),
        pl.BlockSpec((A_WIDTH, D_MODEL), const2),
        pl.BlockSpec((D_MODEL, P_HEADS * P_DKEY), const2),
        pl.BlockSpec((N_KEYS, P_HALF), const2),
        pl.BlockSpec((N_KEYS, P_HALF), const2),
    ]
    out_specs = [pl.BlockSpec((tm, D_MODEL), tok), pl.BlockSpec((tm, D_MODEL), tok),
                 pl.BlockSpec((tm, N_SEL), tok), pl.BlockSpec((tm, N_SEL), tok)]
    out_shape = [jax.ShapeDtypeStruct((n, D_MODEL), F32), jax.ShapeDtypeStruct((n, D_MODEL), F32),
                 jax.ShapeDtypeStruct((n, N_SEL), I32), jax.ShapeDtypeStruct((n, N_SEL), F32)]
    return pl.pallas_call(
        _mix_kernel, grid=(n_tiles,), in_specs=in_specs, out_specs=out_specs, out_shape=out_shape,
        scratch_shapes=[pltpu.VMEM((P_HEADS, tm, P_DKEY), BF16), pltpu.VMEM((N_SEL, tm), F32),
                        pltpu.VMEM((N_SEL, tm), F32)],
        compiler_params=pltpu.CompilerParams(dimension_semantics=("parallel",)),
        name="mix",
    )(x2d, h_f, h_b, mo, a_out, mod3, mhg, n2, w_m, w_a, w_q, sub_a, sub_b)


SC_LANES = 16
SC_CORES = 2
SC_SUBCORES = 16
SC_WORKERS = SC_CORES * SC_SUBCORES
SC_TOKENS = 8
SC_GROUP = SC_LANES
SC_NGROUPS = N_SEL // SC_GROUP
SC_XCHUNK = 4 * SC_LANES
SC_OCHUNK = 32 * SC_LANES
GELU_C0 = 0.7978845608028654
GELU_C1 = 0.044715


def _sc_gelu(a):
    z = GELU_C0 * (a + GELU_C1 * (a * a * a))
    tanh = 1.0 - 2.0 / (jnp.exp(2.0 * z) + 1.0)
    return 0.5 * a * (1.0 + tanh)


def _peer_sc_kernel(h2_hbm, eidx_hbm, gate_hbm, u_hbm, v_hbm, out_hbm,
                    xbuf, ibuf, gbuf, obuf, ubuf, vbuf, mbuf, wbuf, sem_u0, sem_u1, sem_v0, sem_v1):
    n = h2_hbm.shape[0]
    per_worker = n // SC_WORKERS
    wid = lax.axis_index("c") * SC_SUBCORES + lax.axis_index("s")
    sems_u = (sem_u0, sem_u1)
    sems_v = (sem_v0, sem_v1)
    lane = lax.iota(I32, SC_LANES)

    def split_item(item):
        return lax.shift_right_logical(item, SC_NGROUPS.bit_length() - 1), item & (SC_NGROUPS - 1)

    def gather_copies(item, slot):
        t, g = split_item(item)
        idx = ibuf[t, pl.ds(g * SC_GROUP, SC_GROUP)]
        return (pltpu.make_async_copy(u_hbm.at[idx], ubuf.at[slot], sems_u[slot]),
                pltpu.make_async_copy(v_hbm.at[idx], vbuf.at[slot], sems_v[slot]))

    def dots(t, slot):
        def sweep(jc, accs):
            base = jc * SC_XCHUNK
            xs = [xbuf[t, pl.ds(base + k * SC_LANES, SC_LANES)] for k in range(SC_XCHUNK // SC_LANES)]
            out = []
            for r in range(SC_GROUP):
                a = accs[r]
                for k in range(SC_XCHUNK // SC_LANES):
                    a = a + xs[k] * ubuf[slot, r, pl.ds(base + k * SC_LANES, SC_LANES)]
                out.append(a)
            return tuple(out)

        zero = jnp.zeros((SC_LANES,), F32)
        accs = lax.fori_loop(0, D_MODEL // SC_XCHUNK, sweep, tuple(zero for _ in range(SC_GROUP)))
        for r in range(SC_GROUP):
            mbuf[r, :] = accs[r]
        tot = zero
        for c in range(SC_LANES):
            tot = tot + plsc.load_gather(mbuf, [lane, jnp.full((SC_LANES,), c, I32)])
        return tot

    def accumulate(t, slot):
        for oc in range(D_MODEL // SC_OCHUNK):
            nv = SC_OCHUNK // SC_LANES
            accs = tuple(obuf[t, pl.ds(oc * SC_OCHUNK + j * SC_LANES, SC_LANES)] for j in range(nv))

            def row(r, accs):
                wb = plsc.load_gather(wbuf, [jnp.full((SC_LANES,), r, I32)])
                return tuple(accs[j] + wb * vbuf[slot, r, pl.ds(oc * SC_OCHUNK + j * SC_LANES, SC_LANES)]
                             for j in range(nv))

            accs = lax.fori_loop(0, SC_GROUP, row, accs)
            for j in range(nv):
                obuf[t, pl.ds(oc * SC_OCHUNK + j * SC_LANES, SC_LANES)] = accs[j]

    n_items = SC_TOKENS * SC_NGROUPS

    @pl.loop(0, per_worker // SC_TOKENS)
    def _(blk):
        tok0 = pl.multiple_of(wid * per_worker + blk * SC_TOKENS, SC_TOKENS)
        pltpu.sync_copy(h2_hbm.at[pl.ds(tok0, SC_TOKENS)], xbuf)
        pltpu.sync_copy(eidx_hbm.at[pl.ds(tok0, SC_TOKENS)], ibuf)
        pltpu.sync_copy(gate_hbm.at[pl.ds(tok0, SC_TOKENS)], gbuf)

        @pl.loop(0, SC_TOKENS)
        def _(t):
            zero = jnp.zeros((SC_LANES,), F32)
            for j in range(D_MODEL // SC_LANES):
                obuf[t, pl.ds(j * SC_LANES, SC_LANES)] = zero

        for c in gather_copies(0, 0):
            c.start()

        @pl.loop(0, n_items, step=2)
        def _(item0):
            for slot in range(2):
                item = item0 + slot
                t, g = split_item(item)

                @pl.when(item + 1 < n_items)
                def _():
                    for c in gather_copies(item + 1, 1 - slot):
                        c.start()

                for c in gather_copies(item, slot):
                    c.wait()
                a = dots(t, slot)
                wbuf[...] = gbuf[t, pl.ds(g * SC_GROUP, SC_GROUP)] * _sc_gelu(a)
                accumulate(t, slot)

        pltpu.sync_copy(obuf, out_hbm.at[pl.ds(tok0, SC_TOKENS)])


def _peer_experts(h2, eidx, gates, u_tab, v_tab):
    n = h2.shape[0]
    mesh = plsc.VectorSubcoreMesh(core_axis_name="c", subcore_axis_name="s")
    fn = pl.kernel(
        _peer_sc_kernel,
        out_type=jax.ShapeDtypeStruct((n, D_MODEL), F32),
        mesh=mesh,
        scratch_types=[
            pltpu.VMEM((SC_TOKENS, D_MODEL), F32),
            pltpu.VMEM((SC_TOKENS, N_SEL), I32),
            pltpu.VMEM((SC_TOKENS, N_SEL), F32),
            pltpu.VMEM((SC_TOKENS, D_MODEL), F32),
            pltpu.VMEM((2, SC_GROUP, D_MODEL), F32),
            pltpu.VMEM((2, SC_GROUP, D_MODEL), F32),
            pltpu.VMEM((SC_GROUP, SC_LANES), F32),
            pltpu.VMEM((SC_LANES,), F32),
            pltpu.SemaphoreType.DMA, pltpu.SemaphoreType.DMA, pltpu.SemaphoreType.DMA, pltpu.SemaphoreType.DMA,
        ],
        compiler_params=pltpu.CompilerParams(needs_layout_passes=False),
        name="peer_experts",
    )
    return fn(h2, eidx, gates, u_tab, v_tab)


def _resid_kernel(x1_ref, p_ref, mod_ref, o_ref):
    o_ref[...] = x1_ref[...] + mod_ref[0, 5:6, :] * p_ref[...]


def _resid_call(x1, peer_out, seq, mod3, mod_row0, tile0):
    n = x1.shape[0]
    tm = TOKEN_TILE
    per_seq = seq // tm
    tok = pl.BlockSpec((tm, D_MODEL), lambda i: (i, 0))
    return pl.pallas_call(
        _resid_kernel, grid=(n // tm,),
        in_specs=[tok, tok,
                  pl.BlockSpec((1, 6, D_MODEL),
                               lambda i: (mod_row0 + ((tile0 + i) // per_seq if mod_row0 else 0), 0, 0))],
        out_specs=tok, out_shape=jax.ShapeDtypeStruct((n, D_MODEL), F32),
        compiler_params=pltpu.CompilerParams(dimension_semantics=("parallel",)),
        name="resid",
    )(x1, peer_out, mod3)


def _rope_tables(seq, rotate):
    if not rotate:
        return jnp.ones((seq, KV_WIDTH), F32), jnp.zeros((seq, KV_WIDTH), F32)
    quarter = A_DH // 4
    t = jnp.arange(seq)
    row = (t // GRID_W).astype(F32)
    col = (t % GRID_W).astype(F32)
    inv = ROPE_BASE ** (-jnp.arange(quarter, dtype=F32) / quarter)
    d = jnp.arange(A_DH)
    pos = jnp.where(d[None, :] < A_DH // 2, row[:, None], col[:, None])
    ang = pos * inv[d % quarter][None, :]
    sign = jnp.where((d % (A_DH // 2)) < quarter, -1.0, 1.0).astype(F32)
    cos = jnp.cos(ang)
    sin = jnp.sin(ang) * sign[None, :]
    return jnp.tile(cos, (1, KV_WIDTH // A_DH)), jnp.tile(sin, (1, KV_WIDTH // A_DH))


def _layer_mixers(x, mod3, mod_row0, prm, cache, rotate):
    (n1, _, w_main, w_g, w_gt, b_g, b_gt, _, qg_t, kg_t, bd, sink) = prm[:12]
    batch, seq, _ = x.shape
    n = batch * seq
    x2d = x.reshape(n, D_MODEL)
    cos, sin = _rope_tables(seq, rotate)
    mq, mk, mv, mo, gcol, grow, aq, ak, av = _inproj_call(
        x2d, seq, mod3, mod_row0, n1, w_main, w_g, w_gt, b_g, b_gt, qg_t, kg_t, bd, cos, sin)
    kc, vc, c0, m0 = cache
    h_f, h_b, c_fin, m_fin = _mlstm_call(mq, mk, mv, gcol, grow, c0, m0, batch, seq)
    if kc is None:
        a_out = _attn_ctx_call(sink, aq, ak, av, batch, seq)
    else:
        a_out = _attn_lat_call(sink, aq, ak, av, kc, vc, batch, seq)
    return (x2d, h_f, h_b, mo, a_out), ak, av, c_fin, m_fin


def _layer_tail(mixed, seq, mod3, mod_row0, prm, tile0, n_tiles, after=None):
    (_, n2, _, _, _, _, _, mhg, _, _, _, _, w_m, w_a, w_q, sub_a, sub_b, u_tab, v_tab) = prm
    x2d, h_f, h_b, mo, a_out = mixed
    if after is not None:
        x2d, _ = lax.optimization_barrier((x2d, after))
    x1, h2, eidx, gates = _mix_call(x2d, seq, h_f, h_b, mo, a_out, mod3, mod_row0, mhg, n2, w_m, w_a, w_q,
                                    sub_a, sub_b, tile0, n_tiles)
    peer_out = _peer_experts(h2, eidx, gates, u_tab, v_tab)
    return _resid_call(x1, peer_out, seq, mod3, mod_row0, tile0), h2


def _pack_state(C, n_vec, m):
    b = C.shape[0]
    caug = jnp.concatenate([C, jnp.broadcast_to(n_vec[..., None], C.shape)], axis=-1)
    caug = caug.reshape(b, 2 * M_HEADS, M_DH, 2 * M_DH)
    m_rep = jnp.broadcast_to(m.reshape(b, 2 * M_HEADS, 1, 1), (b, 2 * M_HEADS, 8, M_DH))
    return caug.astype(F32), m_rep.astype(F32)


def kernel(x_prompt, x_sample, c, cache_attn_k, cache_attn_v, state_mlstm_C, state_mlstm_n, state_mlstm_m,
           c_ctx, w_ada, b_ada, norm1_g, norm2_g, w_in, b_gates, mh_norm_g, q_norm_g, k_norm_g, sink_logits,
           w_out, peer_w_q, peer_sub_a, peer_sub_b, peer_u, peer_v):
    depth = w_ada.shape[0]
    assert depth == 1
    batch, seq, _ = x_prompt.shape
    dec_batch, dec_seq, _ = x_sample.shape
    l = 0

    cond = jnp.concatenate([c_ctx[None, :], c, jnp.zeros((MOD_ROWS - 1 - dec_batch, D_MODEL), F32)], axis=0)
    mod3 = _ada_call(cond, w_ada[l], b_ada[l]).reshape(MOD_ROWS, 6, D_MODEL)

    wi = w_in[l]
    g0 = 4 * M_WIDTH
    w_main = jnp.concatenate([wi[:, :g0], wi[:, g0 + N_GATES:]], axis=1).astype(BF16)
    w_g = wi[:, g0:g0 + N_GATES]
    seg = jnp.arange(A_WIDTH) // A_DH
    bd = jnp.where(seg[:, None] == seg[None, :], 1.0 / A_DH, 0.0).astype(F32)
    prm = (norm1_g[l][None, :], norm2_g[l][None, :], w_main, w_g, w_g.T, b_gates[l][None, :], b_gates[l][:, None],
           mh_norm_g[l][None, :], jnp.tile(q_norm_g[l], A_HEADS)[None, :], jnp.tile(k_norm_g[l], A_KV)[None, :], bd,
           sink_logits[l], w_out[l][:M_WIDTH].astype(BF16), w_out[l][M_WIDTH:].astype(BF16),
           peer_w_q[l].astype(BF16), peer_sub_a[l].astype(BF16), peer_sub_b[l].astype(BF16), peer_u[l], peer_v[l])

    zeros_c = jnp.zeros((batch, 2, M_HEADS, M_DH, M_DH), F32)
    c0, m0 = _pack_state(zeros_c, zeros_c[..., 0], jnp.full((batch, 2, M_HEADS), NEG, F32))
    mixed_p, k_new, v_new, c_fin, m_fin = _layer_mixers(x_prompt, mod3, 0, prm, (None, None, c0, m0), False)
    y_p, order = _layer_tail(mixed_p, seq, mod3, 0, prm, 0, batch * seq // TOKEN_TILE)
    y_p = y_p.reshape(batch, seq, D_MODEL)

    c0s, m0s = _pack_state(state_mlstm_C[:, l], state_mlstm_n[:, l], state_mlstm_m[:, l])
    past = cache_attn_k.shape[2]
    kc = cache_attn_k[:, l].reshape(dec_batch, past, KV_WIDTH)
    vc = cache_attn_v[:, l].reshape(dec_batch, past, KV_WIDTH)
    x_sample, _ = lax.optimization_barrier((x_sample, order))
    mixed_s, _, _, _, _ = _layer_mixers(x_sample, mod3, 1, prm, (kc, vc, c0s, m0s), True)
    chunk = dec_batch * dec_seq // TOKEN_TILE // LATENT_CHUNKS
    ys = []
    for ci in range(LATENT_CHUNKS):
        y_c, order = _layer_tail(mixed_s, dec_seq, mod3, 1, prm, ci * chunk, chunk, after=order)
        ys.append(y_c)
    y_s = jnp.concatenate(ys, axis=0).reshape(dec_batch, dec_seq, D_MODEL)

    c_fin = c_fin.reshape(batch, 2, M_HEADS, M_DH, 2 * M_DH)
    new_c = c_fin[..., :M_DH][:, None]
    new_n = c_fin[..., M_DH][:, None]
    new_m = m_fin[:, :, 0, 0].reshape(batch, 2, M_HEADS)[:, None]
    new_k = k_new.reshape(batch, 1, seq, A_KV, A_DH)
    new_v = v_new.reshape(batch, 1, seq, A_KV, A_DH)
    return y_p, y_s, new_k, new_v, new_c, new_n, new_m
```

```python
import functools

import jax
import jax.numpy as jnp
from jax import lax
from jax.experimental import pallas as pl
from jax.experimental.pallas import tpu as pltpu
from jax.experimental.pallas import tpu_sc as plsc

F32 = jnp.float32
BF16 = jnp.bfloat16
I32 = jnp.int32
HI = lax.Precision.HIGHEST

D_MODEL = 1024
EPS = 1e-6
NEG = -1e30
GRID_W = 64
M_HEADS = 4
M_WIDTH = 512
M_DH = 128
A_HEADS = 8
A_KV = 2
A_REP = 4
A_DH = 64
A_WIDTH = 512
KV_WIDTH = A_KV * A_DH
BLOCK = 128
ROPE_BASE = 10000.0
N_KEYS = 128
P_HEADS = 8
P_DKEY = 256
P_HALF = 128
P_TOPK = 16
N_SEL = P_HEADS * P_TOPK
N_GATES = 4 * M_HEADS
MAIN_COLS = 4 * M_WIDTH + A_WIDTH + 2 * KV_WIDTH
MOD_ROWS = 16

TOKEN_TILE = 256
MLSTM_CHUNK = 128
ADA_COL_TILE = 768
LATENT_CHUNKS = 4


def _sigmoid(x):
    return 1.0 / (1.0 + jnp.exp(-x))


def _log_sigmoid(x):
    return jnp.minimum(x, 0.0) - jnp.log1p(jnp.exp(-jnp.abs(x)))


def _dot_t(a, b, precision=None):
    return lax.dot_general(a, b, (((1,), (1,)), ((), ())), precision=precision,
                           preferred_element_type=F32)


def _ada_kernel(c_ref, w_ref, b_ref, o_ref):
    c = c_ref[...]
    s = c * _sigmoid(c)
    o_ref[...] = jnp.dot(s, w_ref[...], precision=HI, preferred_element_type=F32) + b_ref[...]


def _ada_call(cond, w_ada, b_ada):
    n_out = w_ada.shape[1]
    return pl.pallas_call(
        _ada_kernel,
        grid=(n_out // ADA_COL_TILE,),
        in_specs=[pl.BlockSpec((MOD_ROWS, D_MODEL), lambda j: (0, 0)),
                  pl.BlockSpec((D_MODEL, ADA_COL_TILE), lambda j: (0, j)),
                  pl.BlockSpec((1, ADA_COL_TILE), lambda j: (0, j))],
        out_specs=pl.BlockSpec((MOD_ROWS, ADA_COL_TILE), lambda j: (0, j)),
        out_shape=jax.ShapeDtypeStruct((MOD_ROWS, n_out), F32),
        name="ada",
    )(cond, w_ada, b_ada.reshape(1, n_out))


def _swap16(x):
    n = x.shape[-1]
    lane = lax.broadcasted_iota(I32, x.shape, x.ndim - 1)
    return jnp.where((lane & 16) == 0, pltpu.roll(x, n - 16, x.ndim - 1), pltpu.roll(x, 16, x.ndim - 1))


def _inproj_kernel(x_ref, mod_ref, n1_ref, w_ref, wg_ref, wgt_ref, bg_ref, bgt_ref, qg_ref, kg_ref,
                   bd_ref, cos_ref, sin_ref,
                   mq_ref, mk_ref, mv_ref, mo_ref, gc_ref, gr_ref, aq_ref, ak_ref, av_ref):
    x = x_ref[...]
    h = x * lax.rsqrt(jnp.mean(x * x, axis=-1, keepdims=True) + EPS) * n1_ref[...]
    h = h * (1.0 + mod_ref[0, 1:2, :]) + mod_ref[0, 0:1, :]
    z = jnp.dot(h.astype(BF16), w_ref[...], preferred_element_type=F32)

    mq_ref[...] = (z[:, 0:M_WIDTH] * (M_DH ** -0.5)).astype(BF16)
    mk_ref[...] = z[:, M_WIDTH:2 * M_WIDTH].astype(BF16)
    mv_ref[...] = z[:, 2 * M_WIDTH:3 * M_WIDTH].astype(BF16)
    mo_ref[...] = z[:, 3 * M_WIDTH:4 * M_WIDTH]

    g = jnp.dot(h, wg_ref[...], precision=HI, preferred_element_type=F32) + bg_ref[...]
    kind = lax.broadcasted_iota(I32, g.shape, 1) // M_HEADS
    gc_ref[...] = jnp.where((kind & 1) == 1, _log_sigmoid(g), g)
    gt = _dot_t(wgt_ref[...], h, precision=HI) + bgt_ref[...]
    kind_t = lax.broadcasted_iota(I32, gt.shape, 0) // M_HEADS
    gr_ref[...] = jnp.where((kind_t & 1) == 1, _log_sigmoid(gt), gt)

    o = 4 * M_WIDTH
    aq = z[:, o:o + A_WIDTH]
    ak = z[:, o + A_WIDTH:o + A_WIDTH + KV_WIDTH]
    av_ref[...] = z[:, o + A_WIDTH + KV_WIDTH:o + A_WIDTH + 2 * KV_WIDTH]
    bd = bd_ref[...]
    cos = cos_ref[...]
    sin = sin_ref[...]
    aq = aq * lax.rsqrt(jnp.dot(aq * aq, bd, precision=HI, preferred_element_type=F32) + EPS) * qg_ref[...]
    cos4 = jnp.concatenate([cos] * (A_WIDTH // KV_WIDTH), axis=1)
    sin4 = jnp.concatenate([sin] * (A_WIDTH // KV_WIDTH), axis=1)
    aq = (aq * cos4 + _swap16(aq) * sin4) * (A_DH ** -0.5)
    ak = ak * lax.rsqrt(jnp.dot(ak * ak, bd[0:KV_WIDTH, 0:KV_WIDTH], precision=HI,
                                preferred_element_type=F32) + EPS) * kg_ref[...]
    ak_ref[...] = ak * cos + _swap16(ak) * sin

    lane = lax.broadcasted_iota(I32, (aq.shape[0], KV_WIDTH), 1)
    for hd in range(A_HEADS):
        grp = hd // A_REP
        blk = aq[:, (hd // 2) * KV_WIDTH:(hd // 2 + 1) * KV_WIDTH]
        if hd % 2 != grp:
            blk = pltpu.roll(blk, A_DH, 1)
        keep = (lane >= grp * A_DH) & (lane < (grp + 1) * A_DH)
        aq_ref[hd] = jnp.where(keep, blk, 0.0).astype(BF16)


def _inproj_call(x2d, seq, mod3, mod_row0, n1, w_main, w_g, w_gt, b_g, b_gt, qg_t, kg_t, bd, cos, sin):
    n = x2d.shape[0]
    tm = TOKEN_TILE
    per_seq = seq // tm

    def tok(i):
        return (i, 0)

    def const2(i):
        return (0, 0)

    in_specs = [
        pl.BlockSpec((tm, D_MODEL), tok),
        pl.BlockSpec((1, 6, D_MODEL), lambda i: (mod_row0 + (i // per_seq if mod_row0 else 0), 0, 0)),
        pl.BlockSpec((1, D_MODEL), const2),
        pl.BlockSpec((D_MODEL, MAIN_COLS), const2),
        pl.BlockSpec((D_MODEL, N_GATES), const2),
        pl.BlockSpec((N_GATES, D_MODEL), const2),
        pl.BlockSpec((1, N_GATES), const2),
        pl.BlockSpec((N_GATES, 1), const2),
        pl.BlockSpec((1, A_WIDTH), const2),
        pl.BlockSpec((1, KV_WIDTH), const2),
        pl.BlockSpec((A_WIDTH, A_WIDTH), const2),
        pl.BlockSpec((tm, KV_WIDTH), lambda i: (i % per_seq, 0)),
        pl.BlockSpec((tm, KV_WIDTH), lambda i: (i % per_seq, 0)),
    ]
    out_specs = [
        pl.BlockSpec((tm, M_WIDTH), tok),
        pl.BlockSpec((tm, M_WIDTH), tok),
        pl.BlockSpec((tm, M_WIDTH), tok),
        pl.BlockSpec((tm, M_WIDTH), tok),
        pl.BlockSpec((tm, N_GATES), tok),
        pl.BlockSpec((N_GATES, tm), lambda i: (0, i)),
        pl.BlockSpec((A_HEADS, tm, KV_WIDTH), lambda i: (0, i, 0)),
        pl.BlockSpec((tm, KV_WIDTH), tok),
        pl.BlockSpec((tm, KV_WIDTH), tok),
    ]
    out_shape = [
        jax.ShapeDtypeStruct((n, M_WIDTH), BF16),
        jax.ShapeDtypeStruct((n, M_WIDTH), BF16),
        jax.ShapeDtypeStruct((n, M_WIDTH), BF16),
        jax.ShapeDtypeStruct((n, M_WIDTH), F32),
        jax.ShapeDtypeStruct((n, N_GATES), F32),
        jax.ShapeDtypeStruct((N_GATES, n), F32),
        jax.ShapeDtypeStruct((A_HEADS, n, KV_WIDTH), BF16),
        jax.ShapeDtypeStruct((n, KV_WIDTH), F32),
        jax.ShapeDtypeStruct((n, KV_WIDTH), F32),
    ]
    return pl.pallas_call(
        _inproj_kernel, grid=(n // tm,), in_specs=in_specs, out_specs=out_specs, out_shape=out_shape,
        compiler_params=pltpu.CompilerParams(dimension_semantics=("parallel",)),
        name="inproj",
    )(x2d, mod3, n1, w_main, w_g, w_gt, b_g, b_gt, qg_t, kg_t, bd, cos, sin)


def _mlstm_chain(q, k, v, li_c, lf_c, li_r, lf_r, caug, m, tri, tri_t, mask, reverse):
    L = q.shape[0]
    last = 0 if reverse else L - 1
    b_c = jnp.dot(tri, jnp.broadcast_to(lf_c, (L, L)), precision=HI, preferred_element_type=F32)
    b_r = jnp.dot(jnp.broadcast_to(lf_r, (8, L)), tri_t, precision=HI, preferred_element_type=F32)[0:1, :]
    a_inter = b_c[:, 0:1] + m
    d = jnp.where(mask, b_c - b_r + li_r, -jnp.inf)
    m_t = jnp.maximum(a_inter, jnp.max(d, axis=1, keepdims=True))
    w_inter = jnp.exp(a_inter - m_t)
    s = _dot_t(q, k) * jnp.exp(d - m_t)
    qc = jnp.dot(q, caug.astype(BF16), preferred_element_type=F32)
    num = jnp.dot(s.astype(BF16), v, preferred_element_type=F32) + w_inter * qc[:, 0:M_DH]
    den = jnp.sum(s, axis=1, keepdims=True) + w_inter * qc[:, M_DH:M_DH + 1]
    den = jnp.maximum(jnp.abs(den), jnp.exp(-m_t))
    h = num / den
    m_new = m_t[last:last + 1, :]
    b_last = b_c[last:last + 1, 0:1]
    g_c = jnp.exp(b_last - b_c[:, 0:1] + li_c - m_new)
    decay = jnp.exp(b_last + m - m_new)
    kw = (k.astype(F32) * g_c).astype(BF16)
    vaug = jnp.concatenate([v, jnp.ones_like(v)], axis=1)
    upd = lax.dot_general(kw, vaug, (((0,), (0,)), ((), ())), preferred_element_type=F32)
    return h, decay * caug + upd, m_new


def _mlstm_kernel(qf_ref, kf_ref, vf_ref, gcf_ref, grf_ref, qb_ref, kb_ref, vb_ref, gcb_ref, grb_ref,
                  c0_ref, m0_ref, hf_ref, hb_ref, cfin_ref, mfin_ref, c_scr, m_scr):
    c = pl.program_id(1)
    nc = pl.num_programs(1)
    L = qf_ref.shape[0]

    @pl.when(c == 0)
    def _():
        c_scr[...] = c0_ref[0]
        m_scr[...] = m0_ref[0]

    row = lax.broadcasted_iota(I32, (L, L), 0)
    col = lax.broadcasted_iota(I32, (L, L), 1)
    lower = row >= col
    upper = row <= col
    lower_f = lower.astype(F32)
    upper_f = upper.astype(F32)

    for direction in range(2):
        reverse = direction == 1
        q_ref, k_ref, v_ref, gc_ref, gr_ref, h_ref = (
            (qb_ref, kb_ref, vb_ref, gcb_ref, grb_ref, hb_ref) if reverse
            else (qf_ref, kf_ref, vf_ref, gcf_ref, grf_ref, hf_ref))
        tri, tri_t, mask = (upper_f, lower_f, upper) if reverse else (lower_f, upper_f, lower)
        gc = gc_ref[...]
        gr = gr_ref[...]
        for hd in range(M_HEADS):
            ch = direction * M_HEADS + hd
            sl = slice(hd * M_DH, (hd + 1) * M_DH)
            ci = 2 * direction * M_HEADS + hd
            cf = ci + M_HEADS
            h, caug, m_new = _mlstm_chain(
                q_ref[:, sl], k_ref[:, sl], v_ref[:, sl],
                gc[:, ci:ci + 1], gc[:, cf:cf + 1], gr[ci:ci + 1, :], gr[cf:cf + 1, :],
                c_scr[ch], m_scr[ch][0:1, 0:1], tri, tri_t, mask, reverse)
            h_ref[:, sl] = h
            c_scr[ch] = caug
            m_scr[ch] = jnp.broadcast_to(m_new, m_scr.shape[1:])

    @pl.when(c == nc - 1)
    def _():
        cfin_ref[0] = c_scr[...]
        mfin_ref[0] = m_scr[...]


def _mlstm_call(mq, mk, mv, gcol, grow, c0, m0, batch, seq):
    n = mq.shape[0]
    L = MLSTM_CHUNK
    nc = seq // L
    n_ch = 2 * M_HEADS

    def fwd(b, c):
        return (b * nc + c, 0)

    def bwd(b, c):
        return (b * nc + nc - 1 - c, 0)

    def fwd_t(b, c):
        return (0, b * nc + c)

    def bwd_t(b, c):
        return (0, b * nc + nc - 1 - c)

    tok = pl.BlockSpec((L, M_WIDTH), fwd)
    tok_b = pl.BlockSpec((L, M_WIDTH), bwd)
    in_specs = [tok, tok, tok, pl.BlockSpec((L, N_GATES), fwd), pl.BlockSpec((N_GATES, L), fwd_t),
                tok_b, tok_b, tok_b, pl.BlockSpec((L, N_GATES), bwd), pl.BlockSpec((N_GATES, L), bwd_t),
                pl.BlockSpec((1, n_ch, M_DH, 2 * M_DH), lambda b, c: (b, 0, 0, 0)),
                pl.BlockSpec((1, n_ch, 8, M_DH), lambda b, c: (b, 0, 0, 0))]
    out_specs = [tok, tok_b,
                 pl.BlockSpec((1, n_ch, M_DH, 2 * M_DH), lambda b, c: (b, 0, 0, 0)),
                 pl.BlockSpec((1, n_ch, 8, M_DH), lambda b, c: (b, 0, 0, 0))]
    out_shape = [jax.ShapeDtypeStruct((n, M_WIDTH), F32), jax.ShapeDtypeStruct((n, M_WIDTH), F32),
                 jax.ShapeDtypeStruct((batch, n_ch, M_DH, 2 * M_DH), F32),
                 jax.ShapeDtypeStruct((batch, n_ch, 8, M_DH), F32)]
    return pl.pallas_call(
        _mlstm_kernel, grid=(batch, nc), in_specs=in_specs, out_specs=out_specs, out_shape=out_shape,
        scratch_shapes=[pltpu.VMEM((n_ch, M_DH, 2 * M_DH), F32), pltpu.VMEM((n_ch, 8, M_DH), F32)],
        compiler_params=pltpu.CompilerParams(dimension_semantics=("parallel", "arbitrary")),
        name="mlstm",
    )(mq, mk, mv, gcol, grow, mq, mk, mv, gcol, grow, c0, m0)


def _sink_column(sink_ref, grp, rows_per_head):
    return jnp.concatenate(
        [jnp.full((rows_per_head, 1), sink_ref[grp * A_REP + r], F32) for r in range(A_REP)], axis=0)


def _store_heads(out_ref, o, grp, rows_per_head):
    for r in range(A_REP):
        hd = grp * A_REP + r
        out_ref[:, hd * A_DH:(hd + 1) * A_DH] = o[r * rows_per_head:(r + 1) * rows_per_head,
                                                  grp * A_DH:(grp + 1) * A_DH].astype(out_ref.dtype)


def _attn_ctx_kernel(sink_ref, q_ref, k_ref, v_ref, out_ref):
    s_len = k_ref.shape[0]
    k = k_ref[...].astype(BF16)
    v = v_ref[...].astype(BF16)
    for grp in range(A_KV):
        q = q_ref[grp * A_REP:(grp + 1) * A_REP].reshape(A_REP * s_len, KV_WIDTH)
        s = _dot_t(q, k)
        sk = _sink_column(sink_ref, grp, s_len)
        mx = jnp.maximum(jnp.max(s, axis=1, keepdims=True), sk)
        p = jnp.exp(s - mx)
        den = jnp.sum(p, axis=1, keepdims=True) + jnp.exp(sk - mx)
        o = jnp.dot(p.astype(BF16), v, preferred_element_type=F32) / den
        _store_heads(out_ref, o, grp, s_len)


def _attn_ctx_call(sink, aq, ak, av, batch, seq):
    n = ak.shape[0]
    return pl.pallas_call(
        _attn_ctx_kernel, grid=(batch,),
        in_specs=[pl.BlockSpec(memory_space=pltpu.SMEM),
                  pl.BlockSpec((A_HEADS, seq, KV_WIDTH), lambda b: (0, b, 0)),
                  pl.BlockSpec((seq, KV_WIDTH), lambda b: (b, 0)),
                  pl.BlockSpec((seq, KV_WIDTH), lambda b: (b, 0))],
        out_specs=pl.BlockSpec((seq, A_WIDTH), lambda b: (b, 0)),
        out_shape=jax.ShapeDtypeStruct((n, A_WIDTH), BF16),
        compiler_params=pltpu.CompilerParams(dimension_semantics=("parallel",)),
        name="attn_ctx",
    )(sink, aq, ak, av)


def _attn_lat_kernel(sink_ref, q_ref, kc_ref, vc_ref, kp_ref, kq_ref, kn_ref, vp_ref, vq_ref, vn_ref, out_ref):
    i = pl.program_id(1)
    nb = pl.num_programs(1)
    kc = kc_ref[0].astype(BF16)
    vc = vc_ref[0].astype(BF16)
    kp, kq, kn = kp_ref[...].astype(BF16), kq_ref[...].astype(BF16), kn_ref[...].astype(BF16)
    vp, vq, vn = vp_ref[...].astype(BF16), vq_ref[...].astype(BF16), vn_ref[...].astype(BF16)
    rows = A_REP * BLOCK
    qpos = lax.broadcasted_iota(I32, (rows, BLOCK), 0) % BLOCK
    kpos = lax.broadcasted_iota(I32, (rows, BLOCK), 1)
    mask_p = (kpos >= qpos) & (i > 0)
    mask_n = (kpos <= qpos) & (i < nb - 1)
    for grp in range(A_KV):
        q = q_ref[grp * A_REP:(grp + 1) * A_REP].reshape(rows, KV_WIDTH)
        s_c = _dot_t(q, kc)
        s_p = jnp.where(mask_p, _dot_t(q, kp), NEG)
        s_q = _dot_t(q, kq)
        s_n = jnp.where(mask_n, _dot_t(q, kn), NEG)
        sk = _sink_column(sink_ref, grp, BLOCK)
        mx = jnp.maximum(jnp.maximum(jnp.max(s_c, axis=1, keepdims=True), jnp.max(s_p, axis=1, keepdims=True)),
                         jnp.maximum(jnp.max(s_q, axis=1, keepdims=True), jnp.max(s_n, axis=1, keepdims=True)))
        mx = jnp.maximum(mx, sk)
        p_c, p_p, p_q, p_n = jnp.exp(s_c - mx), jnp.exp(s_p - mx), jnp.exp(s_q - mx), jnp.exp(s_n - mx)
        den = (jnp.sum(p_c, axis=1, keepdims=True) + jnp.sum(p_p, axis=1, keepdims=True)
               + jnp.sum(p_q, axis=1, keepdims=True) + jnp.sum(p_n, axis=1, keepdims=True) + jnp.exp(sk - mx))
        o = (jnp.dot(p_c.astype(BF16), vc, preferred_element_type=F32)
             + jnp.dot(p_p.astype(BF16), vp, preferred_element_type=F32)
             + jnp.dot(p_q.astype(BF16), vq, preferred_element_type=F32)
             + jnp.dot(p_n.astype(BF16), vn, preferred_element_type=F32)) / den
        _store_heads(out_ref, o, grp, BLOCK)


def _attn_lat_call(sink, aq, ak, av, kc, vc, batch, seq):
    n = ak.shape[0]
    nb = seq // BLOCK
    past = kc.shape[1]

    def cur(b, i):
        return (b * nb + i, 0)

    def prev(b, i):
        return (b * nb + jnp.maximum(i - 1, 0), 0)

    def nxt(b, i):
        return (b * nb + jnp.minimum(i + 1, nb - 1), 0)

    blk = functools.partial(pl.BlockSpec, (BLOCK, KV_WIDTH))
    cache = pl.BlockSpec((1, past, KV_WIDTH), lambda b, i: (b, 0, 0))
    return pl.pallas_call(
        _attn_lat_kernel, grid=(batch, nb),
        in_specs=[pl.BlockSpec(memory_space=pltpu.SMEM),
                  pl.BlockSpec((A_HEADS, BLOCK, KV_WIDTH), lambda b, i: (0, b * nb + i, 0)),
                  cache, cache, blk(prev), blk(cur), blk(nxt), blk(prev), blk(cur), blk(nxt)],
        out_specs=pl.BlockSpec((BLOCK, A_WIDTH), cur),
        out_shape=jax.ShapeDtypeStruct((n, A_WIDTH), BF16),
        compiler_params=pltpu.CompilerParams(dimension_semantics=("parallel", "parallel")),
        name="attn_lat",
    )(sink, aq, kc, vc, ak, ak, ak, av, av, av)


def _top16_rows(s, payload=None):
    n_rows = s.shape[0]
    rows = lax.broadcasted_iota(I32, s.shape, 0).astype(F32)
    vals, idxs, pays = [], [], []
    for _ in range(P_TOPK):
        mx = jnp.max(s, axis=0, keepdims=True)
        ix = jnp.min(jnp.where(s == mx, rows, float(n_rows)), axis=0, keepdims=True)
        hit = rows == ix
        vals.append(mx)
        idxs.append(ix)
        if payload is not None:
            pays.append(jnp.sum(jnp.where(hit, payload, 0.0), axis=0, keepdims=True))
        s = jnp.where(hit, -jnp.inf, s)
    out = (jnp.concatenate(vals, axis=0), jnp.concatenate(idxs, axis=0))
    if payload is not None:
        out += (jnp.concatenate(pays, axis=0),)
    return out


def _mix_kernel(x_ref, hf_ref, hb_ref, mo_ref, ao_ref, mod_ref, mhg_ref, n2_ref, wm_ref, wa_ref, wq_ref,
                sa_ref, sb_ref, x1_ref, h2_ref, eidx_ref, gate_ref, qp_scr, e_scr, g_scr):
    tm = x_ref.shape[0]
    hs = hf_ref[...] + hb_ref[...]
    parts = []
    for hd in range(M_HEADS):
        blk = hs[:, hd * M_DH:(hd + 1) * M_DH]
        parts.append(blk * lax.rsqrt(jnp.mean(blk * blk, axis=-1, keepdims=True) + EPS))
    m_out = _sigmoid(mo_ref[...]) * (jnp.concatenate(parts, axis=1) * mhg_ref[...])
    mix = (jnp.dot(m_out.astype(BF16), wm_ref[...], preferred_element_type=F32)
           + jnp.dot(ao_ref[...], wa_ref[...], preferred_element_type=F32))
    x1 = x_ref[...] + mod_ref[0, 2:3, :] * mix
    x1_ref[...] = x1
    h2 = x1 * lax.rsqrt(jnp.mean(x1 * x1, axis=-1, keepdims=True) + EPS) * n2_ref[...]
    h2 = h2 * (1.0 + mod_ref[0, 4:5, :]) + mod_ref[0, 3:4, :]
    h2_ref[...] = h2
    qp = jnp.dot(h2.astype(BF16), wq_ref[...], preferred_element_type=F32)
    for p in range(P_HEADS):
        qp_scr[p] = qp[:, p * P_DKEY:(p + 1) * P_DKEY].astype(BF16)
    sub_a = sa_ref[...]
    sub_b = sb_ref[...]

    def head_body(p, carry):
        for half in range(tm // N_KEYS):
            cols = slice(half * N_KEYS, (half + 1) * N_KEYS)
            qh = qp_scr[p, pl.ds(half * N_KEYS, N_KEYS), :]
            s_a = _dot_t(sub_a, qh[:, 0:P_HALF])
            s_b = _dot_t(sub_b, qh[:, P_HALF:P_DKEY])
            va, ia = _top16_rows(s_a)
            vb, ib = _top16_rows(s_b)
            cand = jnp.concatenate([va[i:i + 1, :] + vb for i in range(P_TOPK)], axis=0)
            cidx = jnp.concatenate([ia[i:i + 1, :] * float(N_KEYS) + ib for i in range(P_TOPK)], axis=0)
            top, _, eidx = _top16_rows(cand, cidx)
            ex = jnp.exp(top - jnp.max(top, axis=0, keepdims=True))
            gates = ex / jnp.sum(ex, axis=0, keepdims=True)
            r0 = pl.multiple_of(p * P_TOPK, P_TOPK)
            e_scr[pl.ds(r0, P_TOPK), cols] = eidx
            g_scr[pl.ds(r0, P_TOPK), cols] = gates
        return carry

    lax.fori_loop(0, P_HEADS, head_body, 0)
    for half in range(tm // N_KEYS):
        cols = slice(half * N_KEYS, (half + 1) * N_KEYS)
        eidx_ref[cols, :] = e_scr[:, cols].T.astype(I32)
        gate_ref[cols, :] = g_scr[:, cols].T


def _mix_call(x2d, seq, h_f, h_b, mo, a_out, mod3, mod_row0, mhg, n2, w_m, w_a, w_q, sub_a, sub_b, tile0, n_tiles):
    tm = TOKEN_TILE
    per_seq = seq // tm
    n = n_tiles * tm

    def tok_in(i):
        return (tile0 + i, 0)

    def tok(i):
        return (i, 0)

    def const2(i):
        return (0, 0)

    in_specs = [
        pl.BlockSpec((tm, D_MODEL), tok_in),
        pl.BlockSpec((tm, M_WIDTH), tok_in), pl.BlockSpec((tm, M_WIDTH), tok_in),
        pl.BlockSpec((tm, M_WIDTH), tok_in),
        pl.BlockSpec((tm, A_WIDTH), tok_in),
        pl.BlockSpec((1, 6, D_MODEL), lambda i: (mod_row0 + ((tile0 + i) // per_seq if mod_row0 else 0), 0, 0)),
        pl.BlockSpec((1, M_WIDTH), const2),
        pl.BlockSpec((1, D_MODEL), const2),
        pl.BlockSpec((M_WIDTH, D_MODEL), const2),
        pl.BlockSpec((A_WIDTH, D_MODEL), const2),
        pl.BlockSpec((D_MODEL, P_HEADS * P_DKEY), const2),
        pl.BlockSpec((N_KEYS, P_HALF), const2),
        pl.BlockSpec((N_KEYS, P_HALF), const2),
    ]
    out_specs = [pl.BlockSpec((tm, D_MODEL), tok), pl.BlockSpec((tm, D_MODEL), tok),
                 pl.BlockSpec((tm, N_SEL), tok), pl.BlockSpec((tm, N_SEL), tok)]
    out_shape = [jax.ShapeDtypeStruct((n, D_MODEL), F32), jax.ShapeDtypeStruct((n, D_MODEL), F32),
                 jax.ShapeDtypeStruct((n, N_SEL), I32), jax.ShapeDtypeStruct((n, N_SEL), F32)]
    return pl.pallas_call(
        _mix_kernel, grid=(n_tiles,), in_specs=in_specs, out_specs=out_specs, out_shape=out_shape,
        scratch_shapes=[pltpu.VMEM((P_HEADS, tm, P_DKEY), BF16), pltpu.VMEM((N_SEL, tm), F32),
                        pltpu.VMEM((N_SEL, tm), F32)],
        compiler_params=pltpu.CompilerParams(dimension_semantics=("parallel",)),
        name="mix",
    )(x2d, h_f, h_b, mo, a_out, mod3, mhg, n2, w_m, w_a, w_q, sub_a, sub_b)


SC_LANES = 16
SC_CORES = 2
SC_SUBCORES = 16
SC_WORKERS = SC_CORES * SC_SUBCORES
SC_TOKENS = 8
SC_GROUP = SC_LANES
SC_NGROUPS = N_SEL // SC_GROUP
SC_XCHUNK = 4 * SC_LANES
SC_OCHUNK = 32 * SC_LANES
GELU_C0 = 0.7978845608028654
GELU_C1 = 0.044715


def _sc_gelu(a):
    z = GELU_C0 * (a + GELU_C1 * (a * a * a))
    tanh = 1.0 - 2.0 / (jnp.exp(2.0 * z) + 1.0)
    return 0.5 * a * (1.0 + tanh)


def _peer_sc_kernel(h2_hbm, eidx_hbm, gate_hbm, u_hbm, v_hbm, out_hbm,
                    xbuf, ibuf, gbuf, obuf, ubuf, vbuf, mbuf, wbuf, sem_u0, sem_u1, sem_v0, sem_v1):
    n = h2_hbm.shape[0]
    per_worker = n // SC_WORKERS
    wid = lax.axis_index("c") * SC_SUBCORES + lax.axis_index("s")
    sems_u = (sem_u0, sem_u1)
    sems_v = (sem_v0, sem_v1)
    lane = lax.iota(I32, SC_LANES)

    def split_item(item):
        return lax.shift_right_logical(item, SC_NGROUPS.bit_length() - 1), item & (SC_NGROUPS - 1)

    def gather_copies(item, slot):
        t, g = split_item(item)
        idx = ibuf[t, pl.ds(g * SC_GROUP, SC_GROUP)]
        return (pltpu.make_async_copy(u_hbm.at[idx], ubuf.at[slot], sems_u[slot]),
                pltpu.make_async_copy(v_hbm.at[idx], vbuf.at[slot], sems_v[slot]))

    def dots(t, slot):
        def sweep(jc, accs):
            base = jc * SC_XCHUNK
            xs = [xbuf[t, pl.ds(base + k * SC_LANES, SC_LANES)] for k in range(SC_XCHUNK // SC_LANES)]
            out = []
            for r in range(SC_GROUP):
                a = accs[r]
                for k in range(SC_XCHUNK // SC_LANES):
                    a = a + xs[k] * ubuf[slot, r, pl.ds(base + k * SC_LANES, SC_LANES)]
                out.append(a)
            return tuple(out)

        zero = jnp.zeros((SC_LANES,), F32)
        accs = lax.fori_loop(0, D_MODEL // SC_XCHUNK, sweep, tuple(zero for _ in range(SC_GROUP)))
        for r in range(SC_GROUP):
            mbuf[r, :] = accs[r]
        tot = zero
        for c in range(SC_LANES):
            tot = tot + plsc.load_gather(mbuf, [lane, jnp.full((SC_LANES,), c, I32)])
        return tot

    def accumulate(t, slot):
        for oc in range(D_MODEL // SC_OCHUNK):
            nv = SC_OCHUNK // SC_LANES
            accs = tuple(obuf[t, pl.ds(oc * SC_OCHUNK + j * SC_LANES, SC_LANES)] for j in range(nv))

            def row(r, accs):
                wb = plsc.load_gather(wbuf, [jnp.full((SC_LANES,), r, I32)])
                return tuple(accs[j] + wb * vbuf[slot, r, pl.ds(oc * SC_OCHUNK + j * SC_LANES, SC_LANES)]
                             for j in range(nv))

            accs = lax.fori_loop(0, SC_GROUP, row, accs)
            for j in range(nv):
                obuf[t, pl.ds(oc * SC_OCHUNK + j * SC_LANES, SC_LANES)] = accs[j]

    n_items = SC_TOKENS * SC_NGROUPS

    @pl.loop(0, per_worker // SC_TOKENS)
    def _(blk):
        tok0 = pl.multiple_of(wid * per_worker + blk * SC_TOKENS, SC_TOKENS)
        pltpu.sync_copy(h2_hbm.at[pl.ds(tok0, SC_TOKENS)], xbuf)
        pltpu.sync_copy(eidx_hbm.at[pl.ds(tok0, SC_TOKENS)], ibuf)
        pltpu.sync_copy(gate_hbm.at[pl.ds(tok0, SC_TOKENS)], gbuf)

        @pl.loop(0, SC_TOKENS)
        def _(t):
            zero = jnp.zeros((SC_LANES,), F32)
            for j in range(D_MODEL // SC_LANES):
                obuf[t, pl.ds(j * SC_LANES, SC_LANES)] = zero

        for c in gather_copies(0, 0):
            c.start()

        @pl.loop(0, n_items, step=2)
        def _(item0):
            for slot in range(2):
                item = item0 + slot
                t, g = split_item(item)

                @pl.when(item + 1 < n_items)
                def _():
                    for c in gather_copies(item + 1, 1 - slot):
                        c.start()

                for c in gather_copies(item, slot):
                    c.wait()
                a = dots(t, slot)
                wbuf[...] = gbuf[t, pl.ds(g * SC_GROUP, SC_GROUP)] * _sc_gelu(a)
                accumulate(t, slot)

        pltpu.sync_copy(obuf, out_hbm.at[pl.ds(tok0, SC_TOKENS)])


def _peer_experts(h2, eidx, gates, u_tab, v_tab):
    n = h2.shape[0]
    mesh = plsc.VectorSubcoreMesh(core_axis_name="c", subcore_axis_name="s")
    fn = pl.kernel(
        _peer_sc_kernel,
        out_type=jax.ShapeDtypeStruct((n, D_MODEL), F32),
        mesh=mesh,
        scratch_types=[
            pltpu.VMEM((SC_TOKENS, D_MODEL), F32),
            pltpu.VMEM((SC_TOKENS, N_SEL), I32),
            pltpu.VMEM((SC_TOKENS, N_SEL), F32),
            pltpu.VMEM((SC_TOKENS, D_MODEL), F32),
            pltpu.VMEM((2, SC_GROUP, D_MODEL), F32),
            pltpu.VMEM((2, SC_GROUP, D_MODEL), F32),
            pltpu.VMEM((SC_GROUP, SC_LANES), F32),
            pltpu.VMEM((SC_LANES,), F32),
            pltpu.SemaphoreType.DMA, pltpu.SemaphoreType.DMA, pltpu.SemaphoreType.DMA, pltpu.SemaphoreType.DMA,
        ],
        compiler_params=pltpu.CompilerParams(needs_layout_passes=False),
        name="peer_experts",
    )
    return fn(h2, eidx, gates, u_tab, v_tab)


def _resid_kernel(x1_ref, p_ref, mod_ref, o_ref):
    o_ref[...] = x1_ref[...] + mod_ref[0, 5:6, :] * p_ref[...]


def _resid_call(x1, peer_out, seq, mod3, mod_row0, tile0):
    n = x1.shape[0]
    tm = TOKEN_TILE
    per_seq = seq // tm
    tok = pl.BlockSpec((tm, D_MODEL), lambda i: (i, 0))
    return pl.pallas_call(
        _resid_kernel, grid=(n // tm,),
        in_specs=[tok, tok,
                  pl.BlockSpec((1, 6, D_MODEL),
                               lambda i: (mod_row0 + ((tile0 + i) // per_seq if mod_row0 else 0), 0, 0))],
        out_specs=tok, out_shape=jax.ShapeDtypeStruct((n, D_MODEL), F32),
        compiler_params=pltpu.CompilerParams(dimension_semantics=("parallel",)),
        name="resid",
    )(x1, peer_out, mod3)


def _rope_tables(seq, rotate):
    if not rotate:
        return jnp.ones((seq, KV_WIDTH), F32), jnp.zeros((seq, KV_WIDTH), F32)
    quarter = A_DH // 4
    t = jnp.arange(seq)
    row = (t // GRID_W).astype(F32)
    col = (t % GRID_W).astype(F32)
    inv = ROPE_BASE ** (-jnp.arange(quarter, dtype=F32) / quarter)
    d = jnp.arange(A_DH)
    pos = jnp.where(d[None, :] < A_DH // 2, row[:, None], col[:, None])
    ang = pos * inv[d % quarter][None, :]
    sign = jnp.where((d % (A_DH // 2)) < quarter, -1.0, 1.0).astype(F32)
    cos = jnp.cos(ang)
    sin = jnp.sin(ang) * sign[None, :]
    return jnp.tile(cos, (1, KV_WIDTH // A_DH)), jnp.tile(sin, (1, KV_WIDTH // A_DH))


def _layer_mixers(x, mod3, mod_row0, prm, cache, rotate):
    (n1, _, w_main, w_g, w_gt, b_g, b_gt, _, qg_t, kg_t, bd, sink) = prm[:12]
    batch, seq, _ = x.shape
    n = batch * seq
    x2d = x.reshape(n, D_MODEL)
    cos, sin = _rope_tables(seq, rotate)
    mq, mk, mv, mo, gcol, grow, aq, ak, av = _inproj_call(
        x2d, seq, mod3, mod_row0, n1, w_main, w_g, w_gt, b_g, b_gt, qg_t, kg_t, bd, cos, sin)
    kc, vc, c0, m0 = cache
    h_f, h_b, c_fin, m_fin = _mlstm_call(mq, mk, mv, gcol, grow, c0, m0, batch, seq)
    if kc is None:
        a_out = _attn_ctx_call(sink, aq, ak, av, batch, seq)
    else:
        a_out = _attn_lat_call(sink, aq, ak, av, kc, vc, batch, seq)
    return (x2d, h_f, h_b, mo, a_out), ak, av, c_fin, m_fin


def _layer_tail(mixed, seq, mod3, mod_row0, prm, tile0, n_tiles, next_input):
    (_, n2, _, _, _, _, _, mhg, _, _, _, _, w_m, w_a, w_q, sub_a, sub_b, u_tab, v_tab) = prm
    x2d, h_f, h_b, mo, a_out = mixed
    routed = _mix_call(x2d, seq, h_f, h_b, mo, a_out, mod3, mod_row0, mhg, n2, w_m, w_a, w_q,
                       sub_a, sub_b, tile0, n_tiles)
    if next_input is not None:
        routed, next_input = lax.optimization_barrier((routed, next_input))
    x1, h2, eidx, gates = routed
    peer_out = _peer_experts(h2, eidx, gates, u_tab, v_tab)
    return _resid_call(x1, peer_out, seq, mod3, mod_row0, tile0), next_input


def _pack_state(C, n_vec, m):
    b = C.shape[0]
    caug = jnp.concatenate([C, jnp.broadcast_to(n_vec[..., None], C.shape)], axis=-1)
    caug = caug.reshape(b, 2 * M_HEADS, M_DH, 2 * M_DH)
    m_rep = jnp.broadcast_to(m.reshape(b, 2 * M_HEADS, 1, 1), (b, 2 * M_HEADS, 8, M_DH))
    return caug.astype(F32), m_rep.astype(F32)


def kernel(x_prompt, x_sample, c, cache_attn_k, cache_attn_v, state_mlstm_C, state_mlstm_n, state_mlstm_m,
           c_ctx, w_ada, b_ada, norm1_g, norm2_g, w_in, b_gates, mh_norm_g, q_norm_g, k_norm_g, sink_logits,
           w_out, peer_w_q, peer_sub_a, peer_sub_b, peer_u, peer_v):
    depth = w_ada.shape[0]
    assert depth == 1
    batch, seq, _ = x_prompt.shape
    dec_batch, dec_seq, _ = x_sample.shape
    l = 0

    cond = jnp.concatenate([c_ctx[None, :], c, jnp.zeros((MOD_ROWS - 1 - dec_batch, D_MODEL), F32)], axis=0)
    mod3 = _ada_call(cond, w_ada[l], b_ada[l]).reshape(MOD_ROWS, 6, D_MODEL)

    wi = w_in[l]
    g0 = 4 * M_WIDTH
    w_main = jnp.concatenate([wi[:, :g0], wi[:, g0 + N_GATES:]], axis=1).astype(BF16)
    w_g = wi[:, g0:g0 + N_GATES]
    seg = jnp.arange(A_WIDTH) // A_DH
    bd = jnp.where(seg[:, None] == seg[None, :], 1.0 / A_DH, 0.0).astype(F32)
    prm = (norm1_g[l][None, :], norm2_g[l][None, :], w_main, w_g, w_g.T, b_gates[l][None, :], b_gates[l][:, None],
           mh_norm_g[l][None, :], jnp.tile(q_norm_g[l], A_HEADS)[None, :], jnp.tile(k_norm_g[l], A_KV)[None, :], bd,
           sink_logits[l], w_out[l][:M_WIDTH].astype(BF16), w_out[l][M_WIDTH:].astype(BF16),
           peer_w_q[l].astype(BF16), peer_sub_a[l].astype(BF16), peer_sub_b[l].astype(BF16), peer_u[l], peer_v[l])

    zeros_c = jnp.zeros((batch, 2, M_HEADS, M_DH, M_DH), F32)
    c0, m0 = _pack_state(zeros_c, zeros_c[..., 0], jnp.full((batch, 2, M_HEADS), NEG, F32))
    mixed_p, k_new, v_new, c_fin, m_fin = _layer_mixers(x_prompt, mod3, 0, prm, (None, None, c0, m0), False)
    y_p, x_sample = _layer_tail(mixed_p, seq, mod3, 0, prm, 0, batch * seq // TOKEN_TILE, x_sample)
    y_p = y_p.reshape(batch, seq, D_MODEL)

    c0s, m0s = _pack_state(state_mlstm_C[:, l], state_mlstm_n[:, l], state_mlstm_m[:, l])
    past = cache_attn_k.shape[2]
    kc = cache_attn_k[:, l].reshape(dec_batch, past, KV_WIDTH)
    vc = cache_attn_v[:, l].reshape(dec_batch, past, KV_WIDTH)
    mixed_s, _, _, _, _ = _layer_mixers(x_sample, mod3, 1, prm, (kc, vc, c0s, m0s), True)
    chunk = dec_batch * dec_seq // TOKEN_TILE // LATENT_CHUNKS
    ys = []
    for ci in range(LATENT_CHUNKS):
        last = ci == LATENT_CHUNKS - 1
        y_c, nxt = _layer_tail(mixed_s, dec_seq, mod3, 1, prm, ci * chunk, chunk, None if last else mixed_s[0])
        if not last:
            mixed_s = (nxt,) + mixed_s[1:]
        ys.append(y_c)
    y_s = jnp.concatenate(ys, axis=0).reshape(dec_batch, dec_seq, D_MODEL)

    c_fin = c_fin.reshape(batch, 2, M_HEADS, M_DH, 2 * M_DH)
    new_c = c_fin[..., :M_DH][:, None]
    new_n = c_fin[..., M_DH][:, None]
    new_m = m_fin[:, :, 0, 0].reshape(batch, 2, M_HEADS)[:, None]
    new_k = k_new.reshape(batch, 1, seq, A_KV, A_DH)
    new_v = v_new.reshape(batch, 1, seq, A_KV, A_DH)
    return y_p, y_s, new_k, new_v, new_c, new_n, new_m
```

```python
import functools

import jax
import jax.numpy as jnp
from jax import lax
from jax.experimental import pallas as pl
from jax.experimental.pallas import tpu as pltpu
from jax.experimental.pallas import tpu_sc as plsc

F32 = jnp.float32
BF16 = jnp.bfloat16
I32 = jnp.int32
HI = lax.Precision.HIGHEST

D_MODEL = 1024
EPS = 1e-6
NEG = -1e30
GRID_W = 64
M_HEADS = 4
M_WIDTH = 512
M_DH = 128
A_HEADS = 8
A_KV = 2
A_REP = 4
A_DH = 64
A_WIDTH = 512
KV_WIDTH = A_KV * A_DH
BLOCK = 128
ROPE_BASE = 10000.0
N_KEYS = 128
P_HEADS = 8
P_DKEY = 256
P_HALF = 128
P_TOPK = 16
N_SEL = P_HEADS * P_TOPK
N_GATES = 4 * M_HEADS
MAIN_COLS = 4 * M_WIDTH + A_WIDTH + 2 * KV_WIDTH
MOD_ROWS = 16

TOKEN_TILE = 256
MLSTM_CHUNK = 128
ADA_COL_TILE = 768
LATENT_CHUNKS = 4


def _sigmoid(x):
    return 1.0 / (1.0 + jnp.exp(-x))


def _log_sigmoid(x):
    return jnp.minimum(x, 0.0) - jnp.log1p(jnp.exp(-jnp.abs(x)))


def _dot_t(a, b, precision=None):
    return lax.dot_general(a, b, (((1,), (1,)), ((), ())), precision=precision,
                           preferred_element_type=F32)


def _ada_kernel(c_ref, w_ref, b_ref, o_ref):
    c = c_ref[...]
    s = c * _sigmoid(c)
    o_ref[...] = jnp.dot(s, w_ref[...], precision=HI, preferred_element_type=F32) + b_ref[...]


def _ada_call(cond, w_ada, b_ada):
    n_out = w_ada.shape[1]
    return pl.pallas_call(
        _ada_kernel,
        grid=(n_out // ADA_COL_TILE,),
        in_specs=[pl.BlockSpec((MOD_ROWS, D_MODEL), lambda j: (0, 0)),
                  pl.BlockSpec((D_MODEL, ADA_COL_TILE), lambda j: (0, j)),
                  pl.BlockSpec((1, ADA_COL_TILE), lambda j: (0, j))],
        out_specs=pl.BlockSpec((MOD_ROWS, ADA_COL_TILE), lambda j: (0, j)),
        out_shape=jax.ShapeDtypeStruct((MOD_ROWS, n_out), F32),
        name="ada",
    )(cond, w_ada, b_ada.reshape(1, n_out))


def _swap16(x):
    n = x.shape[-1]
    lane = lax.broadcasted_iota(I32, x.shape, x.ndim - 1)
    return jnp.where((lane & 16) == 0, pltpu.roll(x, n - 16, x.ndim - 1), pltpu.roll(x, 16, x.ndim - 1))


def _inproj_kernel(x_ref, mod_ref, n1_ref, w_ref, wg_ref, wgt_ref, bg_ref, bgt_ref, qg_ref, kg_ref,
                   bd_ref, cos_ref, sin_ref,
                   mq_ref, mk_ref, mv_ref, mo_ref, gc_ref, gr_ref, aq_ref, ak_ref, av_ref):
    x = x_ref[...]
    h = x * lax.rsqrt(jnp.mean(x * x, axis=-1, keepdims=True) + EPS) * n1_ref[...]
    h = h * (1.0 + mod_ref[0, 1:2, :]) + mod_ref[0, 0:1, :]
    z = jnp.dot(h.astype(BF16), w_ref[...], preferred_element_type=F32)

    mq_ref[...] = (z[:, 0:M_WIDTH] * (M_DH ** -0.5)).astype(BF16)
    mk_ref[...] = z[:, M_WIDTH:2 * M_WIDTH].astype(BF16)
    mv_ref[...] = z[:, 2 * M_WIDTH:3 * M_WIDTH].astype(BF16)
    mo_ref[...] = z[:, 3 * M_WIDTH:4 * M_WIDTH]

    g = jnp.dot(h, wg_ref[...], precision=HI, preferred_element_type=F32) + bg_ref[...]
    kind = lax.broadcasted_iota(I32, g.shape, 1) // M_HEADS
    gc_ref[...] = jnp.where((kind & 1) == 1, _log_sigmoid(g), g)
    gt = _dot_t(wgt_ref[...], h, precision=HI) + bgt_ref[...]
    kind_t = lax.broadcasted_iota(I32, gt.shape, 0) // M_HEADS
    gr_ref[...] = jnp.where((kind_t & 1) == 1, _log_sigmoid(gt), gt)

    o = 4 * M_WIDTH
    aq = z[:, o:o + A_WIDTH]
    ak = z[:, o + A_WIDTH:o + A_WIDTH + KV_WIDTH]
    av_ref[...] = z[:, o + A_WIDTH + KV_WIDTH:o + A_WIDTH + 2 * KV_WIDTH]
    bd = bd_ref[...]
    cos = cos_ref[...]
    sin = sin_ref[...]
    aq = aq * lax.rsqrt(jnp.dot(aq * aq, bd, precision=HI, preferred_element_type=F32) + EPS) * qg_ref[...]
    cos4 = jnp.concatenate([cos] * (A_WIDTH // KV_WIDTH), axis=1)
    sin4 = jnp.concatenate([sin] * (A_WIDTH // KV_WIDTH), axis=1)
    aq = (aq * cos4 + _swap16(aq) * sin4) * (A_DH ** -0.5)
    ak = ak * lax.rsqrt(jnp.dot(ak * ak, bd[0:KV_WIDTH, 0:KV_WIDTH], precision=HI,
                                preferred_element_type=F32) + EPS) * kg_ref[...]
    ak_ref[...] = ak * cos + _swap16(ak) * sin

    lane = lax.broadcasted_iota(I32, (aq.shape[0], KV_WIDTH), 1)
    for hd in range(A_HEADS):
        grp = hd // A_REP
        blk = aq[:, (hd // 2) * KV_WIDTH:(hd // 2 + 1) * KV_WIDTH]
        if hd % 2 != grp:
            blk = pltpu.roll(blk, A_DH, 1)
        keep = (lane >= grp * A_DH) & (lane < (grp + 1) * A_DH)
        aq_ref[hd] = jnp.where(keep, blk, 0.0).astype(BF16)


def _inproj_call(x2d, seq, mod3, mod_row0, n1, w_main, w_g, w_gt, b_g, b_gt, qg_t, kg_t, bd, cos, sin):
    n = x2d.shape[0]
    tm = TOKEN_TILE
    per_seq = seq // tm

    def tok(i):
        return (i, 0)

    def const2(i):
        return (0, 0)

    in_specs = [
        pl.BlockSpec((tm, D_MODEL), tok),
        pl.BlockSpec((1, 6, D_MODEL), lambda i: (mod_row0 + (i // per_seq if mod_row0 else 0), 0, 0)),
        pl.BlockSpec((1, D_MODEL), const2),
        pl.BlockSpec((D_MODEL, MAIN_COLS), const2),
        pl.BlockSpec((D_MODEL, N_GATES), const2),
        pl.BlockSpec((N_GATES, D_MODEL), const2),
        pl.BlockSpec((1, N_GATES), const2),
        pl.BlockSpec((N_GATES, 1), const2),
        pl.BlockSpec((1, A_WIDTH), const2),
        pl.BlockSpec((1, KV_WIDTH), const2),
        pl.BlockSpec((A_WIDTH, A_WIDTH), const2),
        pl.BlockSpec((tm, KV_WIDTH), lambda i: (i % per_seq, 0)),
        pl.BlockSpec((tm, KV_WIDTH), lambda i: (i % per_seq, 0)),
    ]
    out_specs = [
        pl.BlockSpec((tm, M_WIDTH), tok),
        pl.BlockSpec((tm, M_WIDTH), tok),
        pl.BlockSpec((tm, M_WIDTH), tok),
        pl.BlockSpec((tm, M_WIDTH), tok),
        pl.BlockSpec((tm, N_GATES), tok),
        pl.BlockSpec((N_GATES, tm), lambda i: (0, i)),
        pl.BlockSpec((A_HEADS, tm, KV_WIDTH), lambda i: (0, i, 0)),
        pl.BlockSpec((tm, KV_WIDTH), tok),
        pl.BlockSpec((tm, KV_WIDTH), tok),
    ]
    out_shape = [
        jax.ShapeDtypeStruct((n, M_WIDTH), BF16),
        jax.ShapeDtypeStruct((n, M_WIDTH), BF16),
        jax.ShapeDtypeStruct((n, M_WIDTH), BF16),
        jax.ShapeDtypeStruct((n, M_WIDTH), F32),
        jax.ShapeDtypeStruct((n, N_GATES), F32),
        jax.ShapeDtypeStruct((N_GATES, n), F32),
        jax.ShapeDtypeStruct((A_HEADS, n, KV_WIDTH), BF16),
        jax.ShapeDtypeStruct((n, KV_WIDTH), F32),
        jax.ShapeDtypeStruct((n, KV_WIDTH), F32),
    ]
    return pl.pallas_call(
        _inproj_kernel, grid=(n // tm,), in_specs=in_specs, out_specs=out_specs, out_shape=out_shape,
        compiler_params=pltpu.CompilerParams(dimension_semantics=("parallel",)),
        name="inproj",
    )(x2d, mod3, n1, w_main, w_g, w_gt, b_g, b_gt, qg_t, kg_t, bd, cos, sin)


def _mlstm_chain(q, k, v, li_c, lf_c, li_r, lf_r, caug, m, tri, tri_t, mask, reverse):
    L = q.shape[0]
    last = 0 if reverse else L - 1
    b_c = jnp.dot(tri, jnp.broadcast_to(lf_c, (L, L)), precision=HI, preferred_element_type=F32)
    b_r = jnp.dot(jnp.broadcast_to(lf_r, (8, L)), tri_t, precision=HI, preferred_element_type=F32)[0:1, :]
    a_inter = b_c[:, 0:1] + m
    d = jnp.where(mask, b_c - b_r + li_r, -jnp.inf)
    m_t = jnp.maximum(a_inter, jnp.max(d, axis=1, keepdims=True))
    w_inter = jnp.exp(a_inter - m_t)
    s = _dot_t(q, k) * jnp.exp(d - m_t)
    qc = jnp.dot(q, caug.astype(BF16), preferred_element_type=F32)
    num = jnp.dot(s.astype(BF16), v, preferred_element_type=F32) + w_inter * qc[:, 0:M_DH]
    den = jnp.sum(s, axis=1, keepdims=True) + w_inter * qc[:, M_DH:M_DH + 1]
    den = jnp.maximum(jnp.abs(den), jnp.exp(-m_t))
    h = num / den
    m_new = m_t[last:last + 1, :]
    b_last = b_c[last:last + 1, 0:1]
    g_c = jnp.exp(b_last - b_c[:, 0:1] + li_c - m_new)
    decay = jnp.exp(b_last + m - m_new)
    kw = (k.astype(F32) * g_c).astype(BF16)
    vaug = jnp.concatenate([v, jnp.ones_like(v)], axis=1)
    upd = lax.dot_general(kw, vaug, (((0,), (0,)), ((), ())), preferred_element_type=F32)
    return h, decay * caug + upd, m_new


def _mlstm_kernel(qf_ref, kf_ref, vf_ref, gcf_ref, grf_ref, qb_ref, kb_ref, vb_ref, gcb_ref, grb_ref,
                  c0_ref, m0_ref, hf_ref, hb_ref, cfin_ref, mfin_ref, c_scr, m_scr):
    c = pl.program_id(1)
    nc = pl.num_programs(1)
    L = qf_ref.shape[0]

    @pl.when(c == 0)
    def _():
        c_scr[...] = c0_ref[0]
        m_scr[...] = m0_ref[0]

    row = lax.broadcasted_iota(I32, (L, L), 0)
    col = lax.broadcasted_iota(I32, (L, L), 1)
    lower = row >= col
    upper = row <= col
    lower_f = lower.astype(F32)
    upper_f = upper.astype(F32)

    for direction in range(2):
        reverse = direction == 1
        q_ref, k_ref, v_ref, gc_ref, gr_ref, h_ref = (
            (qb_ref, kb_ref, vb_ref, gcb_ref, grb_ref, hb_ref) if reverse
            else (qf_ref, kf_ref, vf_ref, gcf_ref, grf_ref, hf_ref))
        tri, tri_t, mask = (upper_f, lower_f, upper) if reverse else (lower_f, upper_f, lower)
        gc = gc_ref[...]
        gr = gr_ref[...]
        for hd in range(M_HEADS):
            ch = direction * M_HEADS + hd
            sl = slice(hd * M_DH, (hd + 1) * M_DH)
            ci = 2 * direction * M_HEADS + hd
            cf = ci + M_HEADS
            h, caug, m_new = _mlstm_chain(
                q_ref[:, sl], k_ref[:, sl], v_ref[:, sl],
                gc[:, ci:ci + 1], gc[:, cf:cf + 1], gr[ci:ci + 1, :], gr[cf:cf + 1, :],
                c_scr[ch], m_scr[ch][0:1, 0:1], tri, tri_t, mask, reverse)
            h_ref[:, sl] = h
            c_scr[ch] = caug
            m_scr[ch] = jnp.broadcast_to(m_new, m_scr.shape[1:])

    @pl.when(c == nc - 1)
    def _():
        cfin_ref[0] = c_scr[...]
        mfin_ref[0] = m_scr[...]


def _mlstm_call(mq, mk, mv, gcol, grow, c0, m0, batch, seq):
    n = mq.shape[0]
    L = MLSTM_CHUNK
    nc = seq // L
    n_ch = 2 * M_HEADS

    def fwd(b, c):
        return (b * nc + c, 0)

    def bwd(b, c):
        return (b * nc + nc - 1 - c, 0)

    def fwd_t(b, c):
        return (0, b * nc + c)

    def bwd_t(b, c):
        return (0, b * nc + nc - 1 - c)

    tok = pl.BlockSpec((L, M_WIDTH), fwd)
    tok_b = pl.BlockSpec((L, M_WIDTH), bwd)
    in_specs = [tok, tok, tok, pl.BlockSpec((L, N_GATES), fwd), pl.BlockSpec((N_GATES, L), fwd_t),
                tok_b, tok_b, tok_b, pl.BlockSpec((L, N_GATES), bwd), pl.BlockSpec((N_GATES, L), bwd_t),
                pl.BlockSpec((1, n_ch, M_DH, 2 * M_DH), lambda b, c: (b, 0, 0, 0)),
                pl.BlockSpec((1, n_ch, 8, M_DH), lambda b, c: (b, 0, 0, 0))]
    out_specs = [tok, tok_b,
                 pl.BlockSpec((1, n_ch, M_DH, 2 * M_DH), lambda b, c: (b, 0, 0, 0)),
                 pl.BlockSpec((1, n_ch, 8, M_DH), lambda b, c: (b, 0, 0, 0))]
    out_shape = [jax.ShapeDtypeStruct((n, M_WIDTH), F32), jax.ShapeDtypeStruct((n, M_WIDTH), F32),
                 jax.ShapeDtypeStruct((batch, n_ch, M_DH, 2 * M_DH), F32),
                 jax.ShapeDtypeStruct((batch, n_ch, 8, M_DH), F32)]
    return pl.pallas_call(
        _mlstm_kernel, grid=(batch, nc), in_specs=in_specs, out_specs=out_specs, out_shape=out_shape,
        scratch_shapes=[pltpu.VMEM((n_ch, M_DH, 2 * M_DH), F32), pltpu.VMEM((n_ch, 8, M_DH), F32)],
        compiler_params=pltpu.CompilerParams(dimension_semantics=("parallel", "arbitrary")),
        name="mlstm",
    )(mq, mk, mv, gcol, grow, mq, mk, mv, gcol, grow, c0, m0)


def _sink_column(sink_ref, grp, rows_per_head):
    return jnp.concatenate(
        [jnp.full((rows_per_head, 1), sink_ref[grp * A_REP + r], F32) for r in range(A_REP)], axis=0)


def _store_heads(out_ref, o, grp, rows_per_head):
    for r in range(A_REP):
        hd = grp * A_REP + r
        out_ref[:, hd * A_DH:(hd + 1) * A_DH] = o[r * rows_per_head:(r + 1) * rows_per_head,
                                                  grp * A_DH:(grp + 1) * A_DH].astype(out_ref.dtype)


def _attn_ctx_kernel(sink_ref, q_ref, k_ref, v_ref, out_ref):
    s_len = k_ref.shape[0]
    k = k_ref[...].astype(BF16)
    v = v_ref[...].astype(BF16)
    for grp in range(A_KV):
        q = q_ref[grp * A_REP:(grp + 1) * A_REP].reshape(A_REP * s_len, KV_WIDTH)
        s = _dot_t(q, k)
        sk = _sink_column(sink_ref, grp, s_len)
        mx = jnp.maximum(jnp.max(s, axis=1, keepdims=True), sk)
        p = jnp.exp(s - mx)
        den = jnp.sum(p, axis=1, keepdims=True) + jnp.exp(sk - mx)
        o = jnp.dot(p.astype(BF16), v, preferred_element_type=F32) / den
        _store_heads(out_ref, o, grp, s_len)


def _attn_ctx_call(sink, aq, ak, av, batch, seq):
    n = ak.shape[0]
    return pl.pallas_call(
        _attn_ctx_kernel, grid=(batch,),
        in_specs=[pl.BlockSpec(memory_space=pltpu.SMEM),
                  pl.BlockSpec((A_HEADS, seq, KV_WIDTH), lambda b: (0, b, 0)),
                  pl.BlockSpec((seq, KV_WIDTH), lambda b: (b, 0)),
                  pl.BlockSpec((seq, KV_WIDTH), lambda b: (b, 0))],
        out_specs=pl.BlockSpec((seq, A_WIDTH), lambda b: (b, 0)),
        out_shape=jax.ShapeDtypeStruct((n, A_WIDTH), BF16),
        compiler_params=pltpu.CompilerParams(dimension_semantics=("parallel",)),
        name="attn_ctx",
    )(sink, aq, ak, av)


def _attn_lat_kernel(sink_ref, q_ref, kc_ref, vc_ref, kp_ref, kq_ref, kn_ref, vp_ref, vq_ref, vn_ref, out_ref):
    i = pl.program_id(1)
    nb = pl.num_programs(1)
    kc = kc_ref[0].astype(BF16)
    vc = vc_ref[0].astype(BF16)
    kp, kq, kn = kp_ref[...].astype(BF16), kq_ref[...].astype(BF16), kn_ref[...].astype(BF16)
    vp, vq, vn = vp_ref[...].astype(BF16), vq_ref[...].astype(BF16), vn_ref[...].astype(BF16)
    rows = A_REP * BLOCK
    qpos = lax.broadcasted_iota(I32, (rows, BLOCK), 0) % BLOCK
    kpos = lax.broadcasted_iota(I32, (rows, BLOCK), 1)
    mask_p = (kpos >= qpos) & (i > 0)
    mask_n = (kpos <= qpos) & (i < nb - 1)
    for grp in range(A_KV):
        q = q_ref[grp * A_REP:(grp + 1) * A_REP].reshape(rows, KV_WIDTH)
        s_c = _dot_t(q, kc)
        s_p = jnp.where(mask_p, _dot_t(q, kp), NEG)
        s_q = _dot_t(q, kq)
        s_n = jnp.where(mask_n, _dot_t(q, kn), NEG)
        sk = _sink_column(sink_ref, grp, BLOCK)
        mx = jnp.maximum(jnp.maximum(jnp.max(s_c, axis=1, keepdims=True), jnp.max(s_p, axis=1, keepdims=True)),
                         jnp.maximum(jnp.max(s_q, axis=1, keepdims=True), jnp.max(s_n, axis=1, keepdims=True)))
        mx = jnp.maximum(mx, sk)
        p_c, p_p, p_q, p_n = jnp.exp(s_c - mx), jnp.exp(s_p - mx), jnp.exp(s_q - mx), jnp.exp(s_n - mx)
        den = (jnp.sum(p_c, axis=1, keepdims=True) + jnp.sum(p_p, axis=1, keepdims=True)
               + jnp.sum(p_q, axis=1, keepdims=True) + jnp.sum(p_n, axis=1, keepdims=True) + jnp.exp(sk - mx))
        o = (jnp.dot(p_c.astype(BF16), vc, preferred_element_type=F32)
             + jnp.dot(p_p.astype(BF16), vp, preferred_element_type=F32)
             + jnp.dot(p_q.astype(BF16), vq, preferred_element_type=F32)
             + jnp.dot(p_n.astype(BF16), vn, preferred_element_type=F32)) / den
        _store_heads(out_ref, o, grp, BLOCK)


def _attn_lat_call(sink, aq, ak, av, kc, vc, batch, seq):
    n = ak.shape[0]
    nb = seq // BLOCK
    past = kc.shape[1]

    def cur(b, i):
        return (b * nb + i, 0)

    def prev(b, i):
        return (b * nb + jnp.maximum(i - 1, 0), 0)

    def nxt(b, i):
        return (b * nb + jnp.minimum(i + 1, nb - 1), 0)

    blk = functools.partial(pl.BlockSpec, (BLOCK, KV_WIDTH))
    cache = pl.BlockSpec((1, past, KV_WIDTH), lambda b, i: (b, 0, 0))
    return pl.pallas_call(
        _attn_lat_kernel, grid=(batch, nb),
        in_specs=[pl.BlockSpec(memory_space=pltpu.SMEM),
                  pl.BlockSpec((A_HEADS, BLOCK, KV_WIDTH), lambda b, i: (0, b * nb + i, 0)),
                  cache, cache, blk(prev), blk(cur), blk(nxt), blk(prev), blk(cur), blk(nxt)],
        out_specs=pl.BlockSpec((BLOCK, A_WIDTH), cur),
        out_shape=jax.ShapeDtypeStruct((n, A_WIDTH), BF16),
        compiler_params=pltpu.CompilerParams(dimension_semantics=("parallel", "parallel")),
        name="attn_lat",
    )(sink, aq, kc, vc, ak, ak, ak, av, av, av)


def _top16_rows(s, payload=None):
    n_rows = s.shape[0]
    rows = lax.broadcasted_iota(I32, s.shape, 0).astype(F32)
    vals, idxs, pays = [], [], []
    for _ in range(P_TOPK):
        mx = jnp.max(s, axis=0, keepdims=True)
        ix = jnp.min(jnp.where(s == mx, rows, float(n_rows)), axis=0, keepdims=True)
        hit = rows == ix
        vals.append(mx)
        idxs.append(ix)
        if payload is not None:
            pays.append(jnp.sum(jnp.where(hit, payload, 0.0), axis=0, keepdims=True))
        s = jnp.where(hit, -jnp.inf, s)
    out = (jnp.concatenate(vals, axis=0), jnp.concatenate(idxs, axis=0))
    if payload is not None:
        out += (jnp.concatenate(pays, axis=0),)
    return out


def _mix_kernel(x_ref, hf_ref, hb_ref, mo_ref, ao_ref, mod_ref, mhg_ref, n2_ref, wm_ref, wa_ref, wq_ref,
                sa_ref, sb_ref, x1_ref, h2_ref, eidx_ref, gate_ref, qp_scr, e_scr, g_scr):
    tm = x_ref.shape[0]
    hs = hf_ref[...] + hb_ref[...]
    parts = []
    for hd in range(M_HEADS):
        blk = hs[:, hd * M_DH:(hd + 1) * M_DH]
        parts.append(blk * lax.rsqrt(jnp.mean(blk * blk, axis=-1, keepdims=True) + EPS))
    m_out = _sigmoid(mo_ref[...]) * (jnp.concatenate(parts, axis=1) * mhg_ref[...])
    mix = (jnp.dot(m_out.astype(BF16), wm_ref[...], preferred_element_type=F32)
           + jnp.dot(ao_ref[...], wa_ref[...], preferred_element_type=F32))
    x1 = x_ref[...] + mod_ref[0, 2:3, :] * mix
    x1_ref[...] = x1
    h2 = x1 * lax.rsqrt(jnp.mean(x1 * x1, axis=-1, keepdims=True) + EPS) * n2_ref[...]
    h2 = h2 * (1.0 + mod_ref[0, 4:5, :]) + mod_ref[0, 3:4, :]
    h2_ref[...] = h2
    qp = jnp.dot(h2.astype(BF16), wq_ref[...], preferred_element_type=F32)
    for p in range(P_HEADS):
        qp_scr[p] = qp[:, p * P_DKEY:(p + 1) * P_DKEY].astype(BF16)
    sub_a = sa_ref[...]
    sub_b = sb_ref[...]

    def head_body(p, carry):
        for half in range(tm // N_KEYS):
            cols = slice(half * N_KEYS, (half + 1) * N_KEYS)
            qh = qp_scr[p, pl.ds(half * N_KEYS, N_KEYS), :]
            s_a = _dot_t(sub_a, qh[:, 0:P_HALF])
            s_b = _dot_t(sub_b, qh[:, P_HALF:P_DKEY])
            va, ia = _top16_rows(s_a)
            vb, ib = _top16_rows(s_b)
            cand = jnp.concatenate([va[i:i + 1, :] + vb for i in range(P_TOPK)], axis=0)
            cidx = jnp.concatenate([ia[i:i + 1, :] * float(N_KEYS) + ib for i in range(P_TOPK)], axis=0)
            top, _, eidx = _top16_rows(cand, cidx)
            ex = jnp.exp(top - jnp.max(top, axis=0, keepdims=True))
            gates = ex / jnp.sum(ex, axis=0, keepdims=True)
            r0 = pl.multiple_of(p * P_TOPK, P_TOPK)
            e_scr[pl.ds(r0, P_TOPK), cols] = eidx
            g_scr[pl.ds(r0, P_TOPK), cols] = gates
        return carry

    lax.fori_loop(0, P_HEADS, head_body, 0)
    for half in range(tm // N_KEYS):
        cols = slice(half * N_KEYS, (half + 1) * N_KEYS)
        eidx_ref[cols, :] = e_scr[:, cols].T.astype(I32)
        gate_ref[cols, :] = g_scr[:, cols].T


def _mix_call(x2d, seq, h_f, h_b, mo, a_out, mod3, mod_row0, mhg, n2, w_m, w_a, w_q, sub_a, sub_b, tile0, n_tiles):
    tm = TOKEN_TILE
    per_seq = seq // tm
    n = n_tiles * tm

    def tok_in(i):
        return (tile0 + i, 0)

    def tok(i):
        return (i, 0)

    def const2(i):
        return (0, 0)

    in_specs = [
        pl.BlockSpec((tm, D_MODEL), tok_in),
        pl.BlockSpec((tm, M_WIDTH), tok_in), pl.BlockSpec((tm, M_WIDTH), tok_in),
        pl.BlockSpec((tm, M_WIDTH), tok_in),
        pl.BlockSpec((tm, A_WIDTH), tok_in),
        pl.BlockSpec((1, 6, D_MODEL), lambda i: (mod_row0 + ((tile0 + i) // per_seq if mod_row0 else 0), 0, 0)),
        pl.BlockSpec((1, M_WIDTH), const2),
        pl.BlockSpec((1, D_MODEL), const2),
        pl.BlockSpec((M_WIDTH, D_MODEL), const2),
        pl.BlockSpec((A_WIDTH, D_MODEL), const2),
        pl.BlockSpec((D_MODEL, P_HEADS * P_DKEY), const2),
        pl.BlockSpec((N_KEYS, P_HALF), const2),
        pl.BlockSpec((N_KEYS, P_HALF), const2),
    ]
    out_specs = [pl.BlockSpec((tm, D_MODEL), tok), pl.BlockSpec((tm, D_MODEL), tok),
                 pl.BlockSpec((tm, N_SEL), tok), pl.BlockSpec((tm, N_SEL), tok)]
    out_shape = [jax.ShapeDtypeStruct((n, D_MODEL), F32), jax.ShapeDtypeStruct((n, D_MODEL), F32),
                 jax.ShapeDtypeStruct((n, N_SEL), I32), jax.ShapeDtypeStruct((n, N_SEL), F32)]
    return pl.pallas_call(
        _mix_kernel, grid=(n_tiles,), in_specs=in_specs, out_specs=out_specs, out_shape=out_shape,
        scratch_shapes=[pltpu.VMEM((P_HEADS, tm, P_DKEY), BF16), pltpu.VMEM((N_SEL, tm), F32),
                        pltpu.VMEM((N_SEL, tm), F32)],
        compiler_params=pltpu.CompilerParams(dimension_semantics=("parallel",)),
        name="mix",
    )(x2d, h_f, h_b, mo, a_out, mod3, mhg, n2, w_m, w_a, w_q, sub_a, sub_b)


SC_LANES = 16
SC_CORES = 2
SC_SUBCORES = 16
SC_WORKERS = SC_CORES * SC_SUBCORES
SC_TOKENS = 8
SC_GROUP = SC_LANES
SC_NGROUPS = N_SEL // SC_GROUP
SC_XCHUNK = 4 * SC_LANES
SC_OCHUNK = 32 * SC_LANES
GELU_C0 = 0.7978845608028654
GELU_C1 = 0.044715


def _sc_gelu(a):
    z = GELU_C0 * (a + GELU_C1 * (a * a * a))
    tanh = 1.0 - 2.0 / (jnp.exp(2.0 * z) + 1.0)
    return 0.5 * a * (1.0 + tanh)


def _peer_sc_kernel(h2_hbm, eidx_hbm, gate_hbm, u_hbm, v_hbm, out_hbm,
                    xbuf, ibuf, gbuf, obuf, ubuf, vbuf, mbuf, wbuf, sem_u0, sem_u1, sem_v0, sem_v1):
    n = h2_hbm.shape[0]
    per_worker = n // SC_WORKERS
    wid = lax.axis_index("c") * SC_SUBCORES + lax.axis_index("s")
    sems_u = (sem_u0, sem_u1)
    sems_v = (sem_v0, sem_v1)
    lane = lax.iota(I32, SC_LANES)

    def split_item(item):
        return lax.shift_right_logical(item, SC_NGROUPS.bit_length() - 1), item & (SC_NGROUPS - 1)

    def gather_copies(item, slot):
        t, g = split_item(item)
        idx = ibuf[t, pl.ds(g * SC_GROUP, SC_GROUP)]
        return (pltpu.make_async_copy(u_hbm.at[idx], ubuf.at[slot], sems_u[slot]),
                pltpu.make_async_copy(v_hbm.at[idx], vbuf.at[slot], sems_v[slot]))

    def dots(t, slot):
        def sweep(jc, accs):
            base = jc * SC_XCHUNK
            xs = [xbuf[t, pl.ds(base + k * SC_LANES, SC_LANES)] for k in range(SC_XCHUNK // SC_LANES)]
            out = []
            for r in range(SC_GROUP):
                a = accs[r]
                for k in range(SC_XCHUNK // SC_LANES):
                    a = a + xs[k] * ubuf[slot, r, pl.ds(base + k * SC_LANES, SC_LANES)]
                out.append(a)
            return tuple(out)

        zero = jnp.zeros((SC_LANES,), F32)
        accs = lax.fori_loop(0, D_MODEL // SC_XCHUNK, sweep, tuple(zero for _ in range(SC_GROUP)))
        for r in range(SC_GROUP):
            mbuf[r, :] = accs[r]
        tot = zero
        for c in range(SC_LANES):
            tot = tot + plsc.load_gather(mbuf, [lane, jnp.full((SC_LANES,), c, I32)])
        return tot

    def accumulate(t, slot):
        for oc in range(D_MODEL // SC_OCHUNK):
            nv = SC_OCHUNK // SC_LANES
            accs = tuple(obuf[t, pl.ds(oc * SC_OCHUNK + j * SC_LANES, SC_LANES)] for j in range(nv))

            def row(r, accs):
                wb = plsc.load_gather(wbuf, [jnp.full((SC_LANES,), r, I32)])
                return tuple(accs[j] + wb * vbuf[slot, r, pl.ds(oc * SC_OCHUNK + j * SC_LANES, SC_LANES)]
                             for j in range(nv))

            accs = lax.fori_loop(0, SC_GROUP, row, accs)
            for j in range(nv):
                obuf[t, pl.ds(oc * SC_OCHUNK + j * SC_LANES, SC_LANES)] = accs[j]

    n_items = SC_TOKENS * SC_NGROUPS

    @pl.loop(0, per_worker // SC_TOKENS)
    def _(blk):
        tok0 = pl.multiple_of(wid * per_worker + blk * SC_TOKENS, SC_TOKENS)
        pltpu.sync_copy(h2_hbm.at[pl.ds(tok0, SC_TOKENS)], xbuf)
        pltpu.sync_copy(eidx_hbm.at[pl.ds(tok0, SC_TOKENS)], ibuf)
        pltpu.sync_copy(gate_hbm.at[pl.ds(tok0, SC_TOKENS)], gbuf)

        @pl.loop(0, SC_TOKENS)
        def _(t):
            zero = jnp.zeros((SC_LANES,), F32)
            for j in range(D_MODEL // SC_LANES):
                obuf[t, pl.ds(j * SC_LANES, SC_LANES)] = zero

        for c in gather_copies(0, 0):
            c.start()

        @pl.loop(0, n_items, step=2)
        def _(item0):
            for slot in range(2):
                item = item0 + slot
                t, g = split_item(item)

                @pl.when(item + 1 < n_items)
                def _():
                    for c in gather_copies(item + 1, 1 - slot):
                        c.start()

                for c in gather_copies(item, slot):
                    c.wait()
                a = dots(t, slot)
                wbuf[...] = gbuf[t, pl.ds(g * SC_GROUP, SC_GROUP)] * _sc_gelu(a)
                accumulate(t, slot)

        pltpu.sync_copy(obuf, out_hbm.at[pl.ds(tok0, SC_TOKENS)])


def _peer_experts(h2, eidx, gates, u_tab, v_tab):
    n = h2.shape[0]
    mesh = plsc.VectorSubcoreMesh(core_axis_name="c", subcore_axis_name="s")
    fn = pl.kernel(
        _peer_sc_kernel,
        out_type=jax.ShapeDtypeStruct((n, D_MODEL), F32),
        mesh=mesh,
        scratch_types=[
            pltpu.VMEM((SC_TOKENS, D_MODEL), F32),
            pltpu.VMEM((SC_TOKENS, N_SEL), I32),
            pltpu.VMEM((SC_TOKENS, N_SEL), F32),
            pltpu.VMEM((SC_TOKENS, D_MODEL), F32),
            pltpu.VMEM((2, SC_GROUP, D_MODEL), F32),
            pltpu.VMEM((2, SC_GROUP, D_MODEL), F32),
            pltpu.VMEM((SC_GROUP, SC_LANES), F32),
            pltpu.VMEM((SC_LANES,), F32),
            pltpu.SemaphoreType.DMA, pltpu.SemaphoreType.DMA, pltpu.SemaphoreType.DMA, pltpu.SemaphoreType.DMA,
        ],
        compiler_params=pltpu.CompilerParams(needs_layout_passes=False),
        cost_estimate=pl.CostEstimate(
            flops=4 * n * N_SEL * D_MODEL, transcendentals=n * N_SEL,
            bytes_accessed=4 * (2 * n * N_SEL * D_MODEL + 2 * n * D_MODEL + 2 * n * N_SEL)),
        name="peer_experts",
    )
    return fn(h2, eidx, gates, u_tab, v_tab)


def _resid_kernel(x1_ref, p_ref, mod_ref, o_ref):
    o_ref[...] = x1_ref[...] + mod_ref[0, 5:6, :] * p_ref[...]


def _resid_call(x1, peer_out, seq, mod3, mod_row0, tile0):
    n = x1.shape[0]
    tm = TOKEN_TILE
    per_seq = seq // tm
    tok = pl.BlockSpec((tm, D_MODEL), lambda i: (i, 0))
    return pl.pallas_call(
        _resid_kernel, grid=(n // tm,),
        in_specs=[tok, tok,
                  pl.BlockSpec((1, 6, D_MODEL),
                               lambda i: (mod_row0 + ((tile0 + i) // per_seq if mod_row0 else 0), 0, 0))],
        out_specs=tok, out_shape=jax.ShapeDtypeStruct((n, D_MODEL), F32),
        compiler_params=pltpu.CompilerParams(dimension_semantics=("parallel",)),
        name="resid",
    )(x1, peer_out, mod3)


def _rope_tables(seq, rotate):
    if not rotate:
        return jnp.ones((seq, KV_WIDTH), F32), jnp.zeros((seq, KV_WIDTH), F32)
    quarter = A_DH // 4
    t = jnp.arange(seq)
    row = (t // GRID_W).astype(F32)
    col = (t % GRID_W).astype(F32)
    inv = ROPE_BASE ** (-jnp.arange(quarter, dtype=F32) / quarter)
    d = jnp.arange(A_DH)
    pos = jnp.where(d[None, :] < A_DH // 2, row[:, None], col[:, None])
    ang = pos * inv[d % quarter][None, :]
    sign = jnp.where((d % (A_DH // 2)) < quarter, -1.0, 1.0).astype(F32)
    cos = jnp.cos(ang)
    sin = jnp.sin(ang) * sign[None, :]
    return jnp.tile(cos, (1, KV_WIDTH // A_DH)), jnp.tile(sin, (1, KV_WIDTH // A_DH))


def _layer_mixers(x, mod3, mod_row0, prm, cache, rotate):
    (n1, _, w_main, w_g, w_gt, b_g, b_gt, _, qg_t, kg_t, bd, sink) = prm[:12]
    batch, seq, _ = x.shape
    n = batch * seq
    x2d = x.reshape(n, D_MODEL)
    cos, sin = _rope_tables(seq, rotate)
    mq, mk, mv, mo, gcol, grow, aq, ak, av = _inproj_call(
        x2d, seq, mod3, mod_row0, n1, w_main, w_g, w_gt, b_g, b_gt, qg_t, kg_t, bd, cos, sin)
    kc, vc, c0, m0 = cache
    h_f, h_b, c_fin, m_fin = _mlstm_call(mq, mk, mv, gcol, grow, c0, m0, batch, seq)
    if kc is None:
        a_out = _attn_ctx_call(sink, aq, ak, av, batch, seq)
    else:
        a_out = _attn_lat_call(sink, aq, ak, av, kc, vc, batch, seq)
    return (x2d, h_f, h_b, mo, a_out), ak, av, c_fin, m_fin


def _layer_tail(mixed, seq, mod3, mod_row0, prm, tile0, n_tiles, next_input=None, gate_on=None):
    (_, n2, _, _, _, _, _, mhg, _, _, _, _, w_m, w_a, w_q, sub_a, sub_b, u_tab, v_tab) = prm
    x2d, h_f, h_b, mo, a_out = mixed
    if gate_on is not None:
        x2d, _ = lax.optimization_barrier((x2d, gate_on))
    routed = _mix_call(x2d, seq, h_f, h_b, mo, a_out, mod3, mod_row0, mhg, n2, w_m, w_a, w_q,
                       sub_a, sub_b, tile0, n_tiles)
    if next_input is not None:
        routed, next_input = lax.optimization_barrier((routed, next_input))
    x1, h2, eidx, gates = routed
    peer_out = _peer_experts(h2, eidx, gates, u_tab, v_tab)
    return _resid_call(x1, peer_out, seq, mod3, mod_row0, tile0), peer_out, next_input


def _pack_state(C, n_vec, m):
    b = C.shape[0]
    caug = jnp.concatenate([C, jnp.broadcast_to(n_vec[..., None], C.shape)], axis=-1)
    caug = caug.reshape(b, 2 * M_HEADS, M_DH, 2 * M_DH)
    m_rep = jnp.broadcast_to(m.reshape(b, 2 * M_HEADS, 1, 1), (b, 2 * M_HEADS, 8, M_DH))
    return caug.astype(F32), m_rep.astype(F32)


def kernel(x_prompt, x_sample, c, cache_attn_k, cache_attn_v, state_mlstm_C, state_mlstm_n, state_mlstm_m,
           c_ctx, w_ada, b_ada, norm1_g, norm2_g, w_in, b_gates, mh_norm_g, q_norm_g, k_norm_g, sink_logits,
           w_out, peer_w_q, peer_sub_a, peer_sub_b, peer_u, peer_v):
    depth = w_ada.shape[0]
    assert depth == 1
    batch, seq, _ = x_prompt.shape
    dec_batch, dec_seq, _ = x_sample.shape
    l = 0

    cond = jnp.concatenate([c_ctx[None, :], c, jnp.zeros((MOD_ROWS - 1 - dec_batch, D_MODEL), F32)], axis=0)
    mod3 = _ada_call(cond, w_ada[l], b_ada[l]).reshape(MOD_ROWS, 6, D_MODEL)

    wi = w_in[l]
    g0 = 4 * M_WIDTH
    w_main = jnp.concatenate([wi[:, :g0], wi[:, g0 + N_GATES:]], axis=1).astype(BF16)
    w_g = wi[:, g0:g0 + N_GATES]
    seg = jnp.arange(A_WIDTH) // A_DH
    bd = jnp.where(seg[:, None] == seg[None, :], 1.0 / A_DH, 0.0).astype(F32)
    prm = (norm1_g[l][None, :], norm2_g[l][None, :], w_main, w_g, w_g.T, b_gates[l][None, :], b_gates[l][:, None],
           mh_norm_g[l][None, :], jnp.tile(q_norm_g[l], A_HEADS)[None, :], jnp.tile(k_norm_g[l], A_KV)[None, :], bd,
           sink_logits[l], w_out[l][:M_WIDTH].astype(BF16), w_out[l][M_WIDTH:].astype(BF16),
           peer_w_q[l].astype(BF16), peer_sub_a[l].astype(BF16), peer_sub_b[l].astype(BF16), peer_u[l], peer_v[l])

    zeros_c = jnp.zeros((batch, 2, M_HEADS, M_DH, M_DH), F32)
    c0, m0 = _pack_state(zeros_c, zeros_c[..., 0], jnp.full((batch, 2, M_HEADS), NEG, F32))
    mixed_p, k_new, v_new, c_fin, m_fin = _layer_mixers(x_prompt, mod3, 0, prm, (None, None, c0, m0), False)
    y_p, peer_p, x_sample = _layer_tail(mixed_p, seq, mod3, 0, prm, 0, batch * seq // TOKEN_TILE,
                                        next_input=x_sample)
    y_p = y_p.reshape(batch, seq, D_MODEL)

    c0s, m0s = _pack_state(state_mlstm_C[:, l], state_mlstm_n[:, l], state_mlstm_m[:, l])
    past = cache_attn_k.shape[2]
    kc = cache_attn_k[:, l].reshape(dec_batch, past, KV_WIDTH)
    vc = cache_attn_v[:, l].reshape(dec_batch, past, KV_WIDTH)
    mixed_s, _, _, _, _ = _layer_mixers(x_sample, mod3, 1, prm, (kc, vc, c0s, m0s), True)
    chunk = dec_batch * dec_seq // TOKEN_TILE // LATENT_CHUNKS
    ys = []
    peers = [None, peer_p]
    for ci in range(LATENT_CHUNKS):
        y_c, peer_c, _ = _layer_tail(mixed_s, dec_seq, mod3, 1, prm, ci * chunk, chunk, gate_on=peers[ci])
        peers.append(peer_c)
        ys.append(y_c)
    y_s = jnp.concatenate(ys, axis=0).reshape(dec_batch, dec_seq, D_MODEL)

    c_fin = c_fin.reshape(batch, 2, M_HEADS, M_DH, 2 * M_DH)
    new_c = c_fin[..., :M_DH][:, None]
    new_n = c_fin[..., M_DH][:, None]
    new_m = m_fin[:, :, 0, 0].reshape(batch, 2, M_HEADS)[:, None]
    new_k = k_new.reshape(batch, 1, seq, A_KV, A_DH)
    new_v = v_new.reshape(batch, 1, seq, A_KV, A_DH)
    return y_p, y_s, new_k, new_v, new_c, new_n, new_m
```

```python
import functools

import jax
import jax.numpy as jnp
from jax import lax
from jax.experimental import pallas as pl
from jax.experimental.pallas import tpu as pltpu
from jax.experimental.pallas import tpu_sc as plsc

F32 = jnp.float32
BF16 = jnp.bfloat16
I32 = jnp.int32
HI = lax.Precision.HIGHEST

D_MODEL = 1024
EPS = 1e-6
NEG = -1e30
GRID_W = 64
M_HEADS = 4
M_WIDTH = 512
M_DH = 128
A_HEADS = 8
A_KV = 2
A_REP = 4
A_DH = 64
A_WIDTH = 512
KV_WIDTH = A_KV * A_DH
BLOCK = 128
ROPE_BASE = 10000.0
N_KEYS = 128
P_HEADS = 8
P_DKEY = 256
P_HALF = 128
P_TOPK = 16
N_SEL = P_HEADS * P_TOPK
N_GATES = 4 * M_HEADS
MAIN_COLS = 4 * M_WIDTH + A_WIDTH + 2 * KV_WIDTH
MOD_ROWS = 16

TOKEN_TILE = 256
MLSTM_CHUNK = 128
ADA_COL_TILE = 768
LATENT_CHUNKS = 4


def _sigmoid(x):
    return 1.0 / (1.0 + jnp.exp(-x))


def _log_sigmoid(x):
    return jnp.minimum(x, 0.0) - jnp.log1p(jnp.exp(-jnp.abs(x)))


def _dot_t(a, b, precision=None):
    return lax.dot_general(a, b, (((1,), (1,)), ((), ())), precision=precision,
                           preferred_element_type=F32)


def _ada_kernel(c_ref, w_ref, b_ref, o_ref):
    c = c_ref[...]
    s = c * _sigmoid(c)
    o_ref[...] = jnp.dot(s, w_ref[...], precision=HI, preferred_element_type=F32) + b_ref[...]


def _ada_call(cond, w_ada, b_ada):
    n_out = w_ada.shape[1]
    return pl.pallas_call(
        _ada_kernel,
        grid=(n_out // ADA_COL_TILE,),
        in_specs=[pl.BlockSpec((MOD_ROWS, D_MODEL), lambda j: (0, 0)),
                  pl.BlockSpec((D_MODEL, ADA_COL_TILE), lambda j: (0, j)),
                  pl.BlockSpec((1, ADA_COL_TILE), lambda j: (0, j))],
        out_specs=pl.BlockSpec((MOD_ROWS, ADA_COL_TILE), lambda j: (0, j)),
        out_shape=jax.ShapeDtypeStruct((MOD_ROWS, n_out), F32),
        name="ada",
    )(cond, w_ada, b_ada.reshape(1, n_out))


def _swap16(x):
    n = x.shape[-1]
    lane = lax.broadcasted_iota(I32, x.shape, x.ndim - 1)
    return jnp.where((lane & 16) == 0, pltpu.roll(x, n - 16, x.ndim - 1), pltpu.roll(x, 16, x.ndim - 1))


def _inproj_kernel(x_ref, mod_ref, n1_ref, w_ref, wg_ref, wgt_ref, bg_ref, bgt_ref, qg_ref, kg_ref,
                   bd_ref, cos_ref, sin_ref,
                   mq_ref, mk_ref, mv_ref, mo_ref, gc_ref, gr_ref, aq_ref, ak_ref, av_ref):
    x = x_ref[...]
    h = x * lax.rsqrt(jnp.mean(x * x, axis=-1, keepdims=True) + EPS) * n1_ref[...]
    h = h * (1.0 + mod_ref[0, 1:2, :]) + mod_ref[0, 0:1, :]
    z = jnp.dot(h.astype(BF16), w_ref[...], preferred_element_type=F32)

    mq_ref[...] = (z[:, 0:M_WIDTH] * (M_DH ** -0.5)).astype(BF16)
    mk_ref[...] = z[:, M_WIDTH:2 * M_WIDTH].astype(BF16)
    mv_ref[...] = z[:, 2 * M_WIDTH:3 * M_WIDTH].astype(BF16)
    mo_ref[...] = z[:, 3 * M_WIDTH:4 * M_WIDTH]

    g = jnp.dot(h, wg_ref[...], precision=HI, preferred_element_type=F32) + bg_ref[...]
    kind = lax.broadcasted_iota(I32, g.shape, 1) // M_HEADS
    gc_ref[...] = jnp.where((kind & 1) == 1, _log_sigmoid(g), g)
    gt = _dot_t(wgt_ref[...], h, precision=HI) + bgt_ref[...]
    kind_t = lax.broadcasted_iota(I32, gt.shape, 0) // M_HEADS
    gr_ref[...] = jnp.where((kind_t & 1) == 1, _log_sigmoid(gt), gt)

    o = 4 * M_WIDTH
    aq = z[:, o:o + A_WIDTH]
    ak = z[:, o + A_WIDTH:o + A_WIDTH + KV_WIDTH]
    av_ref[...] = z[:, o + A_WIDTH + KV_WIDTH:o + A_WIDTH + 2 * KV_WIDTH]
    bd = bd_ref[...]
    cos = cos_ref[...]
    sin = sin_ref[...]
    aq = aq * lax.rsqrt(jnp.dot(aq * aq, bd, precision=HI, preferred_element_type=F32) + EPS) * qg_ref[...]
    cos4 = jnp.concatenate([cos] * (A_WIDTH // KV_WIDTH), axis=1)
    sin4 = jnp.concatenate([sin] * (A_WIDTH // KV_WIDTH), axis=1)
    aq = (aq * cos4 + _swap16(aq) * sin4) * (A_DH ** -0.5)
    ak = ak * lax.rsqrt(jnp.dot(ak * ak, bd[0:KV_WIDTH, 0:KV_WIDTH], precision=HI,
                                preferred_element_type=F32) + EPS) * kg_ref[...]
    ak_ref[...] = ak * cos + _swap16(ak) * sin

    lane = lax.broadcasted_iota(I32, (aq.shape[0], KV_WIDTH), 1)
    for hd in range(A_HEADS):
        grp = hd // A_REP
        blk = aq[:, (hd // 2) * KV_WIDTH:(hd // 2 + 1) * KV_WIDTH]
        if hd % 2 != grp:
            blk = pltpu.roll(blk, A_DH, 1)
        keep = (lane >= grp * A_DH) & (lane < (grp + 1) * A_DH)
        aq_ref[hd] = jnp.where(keep, blk, 0.0).astype(BF16)


def _inproj_call(x2d, seq, mod3, mod_row0, n1, w_main, w_g, w_gt, b_g, b_gt, qg_t, kg_t, bd, cos, sin):
    n = x2d.shape[0]
    tm = TOKEN_TILE
    per_seq = seq // tm

    def tok(i):
        return (i, 0)

    def const2(i):
        return (0, 0)

    in_specs = [
        pl.BlockSpec((tm, D_MODEL), tok),
        pl.BlockSpec((1, 6, D_MODEL), lambda i: (mod_row0 + (i // per_seq if mod_row0 else 0), 0, 0)),
        pl.BlockSpec((1, D_MODEL), const2),
        pl.BlockSpec((D_MODEL, MAIN_COLS), const2),
        pl.BlockSpec((D_MODEL, N_GATES), const2),
        pl.BlockSpec((N_GATES, D_MODEL), const2),
        pl.BlockSpec((1, N_GATES), const2),
        pl.BlockSpec((N_GATES, 1), const2),
        pl.BlockSpec((1, A_WIDTH), const2),
        pl.BlockSpec((1, KV_WIDTH), const2),
        pl.BlockSpec((A_WIDTH, A_WIDTH), const2),
        pl.BlockSpec((tm, KV_WIDTH), lambda i: (i % per_seq, 0)),
        pl.BlockSpec((tm, KV_WIDTH), lambda i: (i % per_seq, 0)),
    ]
    out_specs = [
        pl.BlockSpec((tm, M_WIDTH), tok),
        pl.BlockSpec((tm, M_WIDTH), tok),
        pl.BlockSpec((tm, M_WIDTH), tok),
        pl.BlockSpec((tm, M_WIDTH), tok),
        pl.BlockSpec((tm, N_GATES), tok),
        pl.BlockSpec((N_GATES, tm), lambda i: (0, i)),
        pl.BlockSpec((A_HEADS, tm, KV_WIDTH), lambda i: (0, i, 0)),
        pl.BlockSpec((tm, KV_WIDTH), tok),
        pl.BlockSpec((tm, KV_WIDTH), tok),
    ]
    out_shape = [
        jax.ShapeDtypeStruct((n, M_WIDTH), BF16),
        jax.ShapeDtypeStruct((n, M_WIDTH), BF16),
        jax.ShapeDtypeStruct((n, M_WIDTH), BF16),
        jax.ShapeDtypeStruct((n, M_WIDTH), F32),
        jax.ShapeDtypeStruct((n, N_GATES), F32),
        jax.ShapeDtypeStruct((N_GATES, n), F32),
        jax.ShapeDtypeStruct((A_HEADS, n, KV_WIDTH), BF16),
        jax.ShapeDtypeStruct((n, KV_WIDTH), F32),
        jax.ShapeDtypeStruct((n, KV_WIDTH), F32),
    ]
    return pl.pallas_call(
        _inproj_kernel, grid=(n // tm,), in_specs=in_specs, out_specs=out_specs, out_shape=out_shape,
        compiler_params=pltpu.CompilerParams(dimension_semantics=("parallel",)),
        name="inproj",
    )(x2d, mod3, n1, w_main, w_g, w_gt, b_g, b_gt, qg_t, kg_t, bd, cos, sin)


def _mlstm_chain(q, k, v, li_c, lf_c, li_r, lf_r, caug, m, tri, tri_t, mask, reverse):
    L = q.shape[0]
    last = 0 if reverse else L - 1
    b_c = jnp.dot(tri, jnp.broadcast_to(lf_c, (L, L)), precision=HI, preferred_element_type=F32)
    b_r = jnp.dot(jnp.broadcast_to(lf_r, (8, L)), tri_t, precision=HI, preferred_element_type=F32)[0:1, :]
    a_inter = b_c[:, 0:1] + m
    d = jnp.where(mask, b_c - b_r + li_r, -jnp.inf)
    m_t = jnp.maximum(a_inter, jnp.max(d, axis=1, keepdims=True))
    w_inter = jnp.exp(a_inter - m_t)
    s = _dot_t(q, k) * jnp.exp(d - m_t)
    qc = jnp.dot(q, caug.astype(BF16), preferred_element_type=F32)
    num = jnp.dot(s.astype(BF16), v, preferred_element_type=F32) + w_inter * qc[:, 0:M_DH]
    den = jnp.sum(s, axis=1, keepdims=True) + w_inter * qc[:, M_DH:M_DH + 1]
    den = jnp.maximum(jnp.abs(den), jnp.exp(-m_t))
    h = num / den
    m_new = m_t[last:last + 1, :]
    b_last = b_c[last:last + 1, 0:1]
    g_c = jnp.exp(b_last - b_c[:, 0:1] + li_c - m_new)
    decay = jnp.exp(b_last + m - m_new)
    kw = (k.astype(F32) * g_c).astype(BF16)
    vaug = jnp.concatenate([v, jnp.ones_like(v)], axis=1)
    upd = lax.dot_general(kw, vaug, (((0,), (0,)), ((), ())), preferred_element_type=F32)
    return h, decay * caug + upd, m_new


def _mlstm_kernel(qf_ref, kf_ref, vf_ref, gcf_ref, grf_ref, qb_ref, kb_ref, vb_ref, gcb_ref, grb_ref,
                  c0_ref, m0_ref, hf_ref, hb_ref, cfin_ref, mfin_ref, c_scr, m_scr):
    c = pl.program_id(1)
    nc = pl.num_programs(1)
    L = qf_ref.shape[0]

    @pl.when(c == 0)
    def _():
        c_scr[...] = c0_ref[0]
        m_scr[...] = m0_ref[0]

    row = lax.broadcasted_iota(I32, (L, L), 0)
    col = lax.broadcasted_iota(I32, (L, L), 1)
    lower = row >= col
    upper = row <= col
    lower_f = lower.astype(F32)
    upper_f = upper.astype(F32)

    for direction in range(2):
        reverse = direction == 1
        q_ref, k_ref, v_ref, gc_ref, gr_ref, h_ref = (
            (qb_ref, kb_ref, vb_ref, gcb_ref, grb_ref, hb_ref) if reverse
            else (qf_ref, kf_ref, vf_ref, gcf_ref, grf_ref, hf_ref))
        tri, tri_t, mask = (upper_f, lower_f, upper) if reverse else (lower_f, upper_f, lower)
        gc = gc_ref[...]
        gr = gr_ref[...]
        for hd in range(M_HEADS):
            ch = direction * M_HEADS + hd
            sl = slice(hd * M_DH, (hd + 1) * M_DH)
            ci = 2 * direction * M_HEADS + hd
            cf = ci + M_HEADS
            h, caug, m_new = _mlstm_chain(
                q_ref[:, sl], k_ref[:, sl], v_ref[:, sl],
                gc[:, ci:ci + 1], gc[:, cf:cf + 1], gr[ci:ci + 1, :], gr[cf:cf + 1, :],
                c_scr[ch], m_scr[ch][0:1, 0:1], tri, tri_t, mask, reverse)
            h_ref[:, sl] = h
            c_scr[ch] = caug
            m_scr[ch] = jnp.broadcast_to(m_new, m_scr.shape[1:])

    @pl.when(c == nc - 1)
    def _():
        cfin_ref[0] = c_scr[...]
        mfin_ref[0] = m_scr[...]


def _mlstm_call(mq, mk, mv, gcol, grow, c0, m0, batch, seq):
    n = mq.shape[0]
    L = MLSTM_CHUNK
    nc = seq // L
    n_ch = 2 * M_HEADS

    def fwd(b, c):
        return (b * nc + c, 0)

    def bwd(b, c):
        return (b * nc + nc - 1 - c, 0)

    def fwd_t(b, c):
        return (0, b * nc + c)

    def bwd_t(b, c):
        return (0, b * nc + nc - 1 - c)

    tok = pl.BlockSpec((L, M_WIDTH), fwd)
    tok_b = pl.BlockSpec((L, M_WIDTH), bwd)
    in_specs = [tok, tok, tok, pl.BlockSpec((L, N_GATES), fwd), pl.BlockSpec((N_GATES, L), fwd_t),
                tok_b, tok_b, tok_b, pl.BlockSpec((L, N_GATES), bwd), pl.BlockSpec((N_GATES, L), bwd_t),
                pl.BlockSpec((1, n_ch, M_DH, 2 * M_DH), lambda b, c: (b, 0, 0, 0)),
                pl.BlockSpec((1, n_ch, 8, M_DH), lambda b, c: (b, 0, 0, 0))]
    out_specs = [tok, tok_b,
                 pl.BlockSpec((1, n_ch, M_DH, 2 * M_DH), lambda b, c: (b, 0, 0, 0)),
                 pl.BlockSpec((1, n_ch, 8, M_DH), lambda b, c: (b, 0, 0, 0))]
    out_shape = [jax.ShapeDtypeStruct((n, M_WIDTH), F32), jax.ShapeDtypeStruct((n, M_WIDTH), F32),
                 jax.ShapeDtypeStruct((batch, n_ch, M_DH, 2 * M_DH), F32),
                 jax.ShapeDtypeStruct((batch, n_ch, 8, M_DH), F32)]
    return pl.pallas_call(
        _mlstm_kernel, grid=(batch, nc), in_specs=in_specs, out_specs=out_specs, out_shape=out_shape,
        scratch_shapes=[pltpu.VMEM((n_ch, M_DH, 2 * M_DH), F32), pltpu.VMEM((n_ch, 8, M_DH), F32)],
        compiler_params=pltpu.CompilerParams(dimension_semantics=("parallel", "arbitrary")),
        name="mlstm",
    )(mq, mk, mv, gcol, grow, mq, mk, mv, gcol, grow, c0, m0)


def _sink_column(sink_ref, grp, rows_per_head):
    return jnp.concatenate(
        [jnp.full((rows_per_head, 1), sink_ref[grp * A_REP + r], F32) for r in range(A_REP)], axis=0)


def _store_heads(out_ref, o, grp, rows_per_head):
    for r in range(A_REP):
        hd = grp * A_REP + r
        out_ref[:, hd * A_DH:(hd + 1) * A_DH] = o[r * rows_per_head:(r + 1) * rows_per_head,
                                                  grp * A_DH:(grp + 1) * A_DH].astype(out_ref.dtype)


def _attn_ctx_kernel(sink_ref, q_ref, k_ref, v_ref, out_ref):
    s_len = k_ref.shape[0]
    k = k_ref[...].astype(BF16)
    v = v_ref[...].astype(BF16)
    for grp in range(A_KV):
        q = q_ref[grp * A_REP:(grp + 1) * A_REP].reshape(A_REP * s_len, KV_WIDTH)
        s = _dot_t(q, k)
        sk = _sink_column(sink_ref, grp, s_len)
        mx = jnp.maximum(jnp.max(s, axis=1, keepdims=True), sk)
        p = jnp.exp(s - mx)
        den = jnp.sum(p, axis=1, keepdims=True) + jnp.exp(sk - mx)
        o = jnp.dot(p.astype(BF16), v, preferred_element_type=F32) / den
        _store_heads(out_ref, o, grp, s_len)


def _attn_ctx_call(sink, aq, ak, av, batch, seq):
    n = ak.shape[0]
    return pl.pallas_call(
        _attn_ctx_kernel, grid=(batch,),
        in_specs=[pl.BlockSpec(memory_space=pltpu.SMEM),
                  pl.BlockSpec((A_HEADS, seq, KV_WIDTH), lambda b: (0, b, 0)),
                  pl.BlockSpec((seq, KV_WIDTH), lambda b: (b, 0)),
                  pl.BlockSpec((seq, KV_WIDTH), lambda b: (b, 0))],
        out_specs=pl.BlockSpec((seq, A_WIDTH), lambda b: (b, 0)),
        out_shape=jax.ShapeDtypeStruct((n, A_WIDTH), BF16),
        compiler_params=pltpu.CompilerParams(dimension_semantics=("parallel",)),
        name="attn_ctx",
    )(sink, aq, ak, av)


def _attn_lat_kernel(sink_ref, q_ref, kc_ref, vc_ref, kp_ref, kq_ref, kn_ref, vp_ref, vq_ref, vn_ref, out_ref):
    i = pl.program_id(1)
    nb = pl.num_programs(1)
    kc = kc_ref[0].astype(BF16)
    vc = vc_ref[0].astype(BF16)
    kp, kq, kn = kp_ref[...].astype(BF16), kq_ref[...].astype(BF16), kn_ref[...].astype(BF16)
    vp, vq, vn = vp_ref[...].astype(BF16), vq_ref[...].astype(BF16), vn_ref[...].astype(BF16)
    rows = A_REP * BLOCK
    qpos = lax.broadcasted_iota(I32, (rows, BLOCK), 0) % BLOCK
    kpos = lax.broadcasted_iota(I32, (rows, BLOCK), 1)
    mask_p = (kpos >= qpos) & (i > 0)
    mask_n = (kpos <= qpos) & (i < nb - 1)
    for grp in range(A_KV):
        q = q_ref[grp * A_REP:(grp + 1) * A_REP].reshape(rows, KV_WIDTH)
        s_c = _dot_t(q, kc)
        s_p = jnp.where(mask_p, _dot_t(q, kp), NEG)
        s_q = _dot_t(q, kq)
        s_n = jnp.where(mask_n, _dot_t(q, kn), NEG)
        sk = _sink_column(sink_ref, grp, BLOCK)
        mx = jnp.maximum(jnp.maximum(jnp.max(s_c, axis=1, keepdims=True), jnp.max(s_p, axis=1, keepdims=True)),
                         jnp.maximum(jnp.max(s_q, axis=1, keepdims=True), jnp.max(s_n, axis=1, keepdims=True)))
        mx = jnp.maximum(mx, sk)
        p_c, p_p, p_q, p_n = jnp.exp(s_c - mx), jnp.exp(s_p - mx), jnp.exp(s_q - mx), jnp.exp(s_n - mx)
        den = (jnp.sum(p_c, axis=1, keepdims=True) + jnp.sum(p_p, axis=1, keepdims=True)
               + jnp.sum(p_q, axis=1, keepdims=True) + jnp.sum(p_n, axis=1, keepdims=True) + jnp.exp(sk - mx))
        o = (jnp.dot(p_c.astype(BF16), vc, preferred_element_type=F32)
             + jnp.dot(p_p.astype(BF16), vp, preferred_element_type=F32)
             + jnp.dot(p_q.astype(BF16), vq, preferred_element_type=F32)
             + jnp.dot(p_n.astype(BF16), vn, preferred_element_type=F32)) / den
        _store_heads(out_ref, o, grp, BLOCK)


def _attn_lat_call(sink, aq, ak, av, kc, vc, batch, seq):
    n = ak.shape[0]
    nb = seq // BLOCK
    past = kc.shape[1]

    def cur(b, i):
        return (b * nb + i, 0)

    def prev(b, i):
        return (b * nb + jnp.maximum(i - 1, 0), 0)

    def nxt(b, i):
        return (b * nb + jnp.minimum(i + 1, nb - 1), 0)

    blk = functools.partial(pl.BlockSpec, (BLOCK, KV_WIDTH))
    cache = pl.BlockSpec((1, past, KV_WIDTH), lambda b, i: (b, 0, 0))
    return pl.pallas_call(
        _attn_lat_kernel, grid=(batch, nb),
        in_specs=[pl.BlockSpec(memory_space=pltpu.SMEM),
                  pl.BlockSpec((A_HEADS, BLOCK, KV_WIDTH), lambda b, i: (0, b * nb + i, 0)),
                  cache, cache, blk(prev), blk(cur), blk(nxt), blk(prev), blk(cur), blk(nxt)],
        out_specs=pl.BlockSpec((BLOCK, A_WIDTH), cur),
        out_shape=jax.ShapeDtypeStruct((n, A_WIDTH), BF16),
        compiler_params=pltpu.CompilerParams(dimension_semantics=("parallel", "parallel")),
        name="attn_lat",
    )(sink, aq, kc, vc, ak, ak, ak, av, av, av)


def _top16_rows(s, payload=None):
    n_rows = s.shape[0]
    rows = lax.broadcasted_iota(I32, s.shape, 0).astype(F32)
    vals, idxs, pays = [], [], []
    for _ in range(P_TOPK):
        mx = jnp.max(s, axis=0, keepdims=True)
        ix = jnp.min(jnp.where(s == mx, rows, float(n_rows)), axis=0, keepdims=True)
        hit = rows == ix
        vals.append(mx)
        idxs.append(ix)
        if payload is not None:
            pays.append(jnp.sum(jnp.where(hit, payload, 0.0), axis=0, keepdims=True))
        s = jnp.where(hit, -jnp.inf, s)
    out = (jnp.concatenate(vals, axis=0), jnp.concatenate(idxs, axis=0))
    if payload is not None:
        out += (jnp.concatenate(pays, axis=0),)
    return out


def _mix_kernel(x_ref, hf_ref, hb_ref, mo_ref, ao_ref, mod_ref, mhg_ref, n2_ref, wm_ref, wa_ref, wq_ref,
                sa_ref, sb_ref, x1_ref, h2_ref, eidx_ref, gate_ref, qp_scr, e_scr, g_scr):
    tm = x_ref.shape[0]
    hs = hf_ref[...] + hb_ref[...]
    parts = []
    for hd in range(M_HEADS):
        blk = hs[:, hd * M_DH:(hd + 1) * M_DH]
        parts.append(blk * lax.rsqrt(jnp.mean(blk * blk, axis=-1, keepdims=True) + EPS))
    m_out = _sigmoid(mo_ref[...]) * (jnp.concatenate(parts, axis=1) * mhg_ref[...])
    mix = (jnp.dot(m_out.astype(BF16), wm_ref[...], preferred_element_type=F32)
           + jnp.dot(ao_ref[...], wa_ref[...], preferred_element_type=F32))
    x1 = x_ref[...] + mod_ref[0, 2:3, :] * mix
    x1_ref[...] = x1
    h2 = x1 * lax.rsqrt(jnp.mean(x1 * x1, axis=-1, keepdims=True) + EPS) * n2_ref[...]
    h2 = h2 * (1.0 + mod_ref[0, 4:5, :]) + mod_ref[0, 3:4, :]
    h2_ref[...] = _pack_bf16_pairs(h2)
    qp = jnp.dot(h2.astype(BF16), wq_ref[...], preferred_element_type=F32)
    for p in range(P_HEADS):
        qp_scr[p] = qp[:, p * P_DKEY:(p + 1) * P_DKEY].astype(BF16)
    sub_a = sa_ref[...]
    sub_b = sb_ref[...]

    def head_body(p, carry):
        for half in range(tm // N_KEYS):
            cols = slice(half * N_KEYS, (half + 1) * N_KEYS)
            qh = qp_scr[p, pl.ds(half * N_KEYS, N_KEYS), :]
            s_a = _dot_t(sub_a, qh[:, 0:P_HALF])
            s_b = _dot_t(sub_b, qh[:, P_HALF:P_DKEY])
            va, ia = _top16_rows(s_a)
            vb, ib = _top16_rows(s_b)
            cand = jnp.concatenate([va[i:i + 1, :] + vb for i in range(P_TOPK)], axis=0)
            cidx = jnp.concatenate([ia[i:i + 1, :] * float(N_KEYS) + ib for i in range(P_TOPK)], axis=0)
            top, _, eidx = _top16_rows(cand, cidx)
            ex = jnp.exp(top - jnp.max(top, axis=0, keepdims=True))
            gates = ex / jnp.sum(ex, axis=0, keepdims=True)
            r0 = pl.multiple_of(p * P_TOPK, P_TOPK)
            e_scr[pl.ds(r0, P_TOPK), cols] = eidx
            g_scr[pl.ds(r0, P_TOPK), cols] = gates
        return carry

    lax.fori_loop(0, P_HEADS, head_body, 0)
    for half in range(tm // N_KEYS):
        cols = slice(half * N_KEYS, (half + 1) * N_KEYS)
        eidx_ref[cols, :] = e_scr[:, cols].T.astype(I32)
        gate_ref[cols, :] = g_scr[:, cols].T


def _mix_call(x2d, seq, h_f, h_b, mo, a_out, mod3, mod_row0, mhg, n2, w_m, w_a, w_q, sub_a, sub_b, tile0, n_tiles):
    tm = TOKEN_TILE
    per_seq = seq // tm
    n = n_tiles * tm

    def tok_in(i):
        return (tile0 + i, 0)

    def tok(i):
        return (i, 0)

    def const2(i):
        return (0, 0)

    in_specs = [
        pl.BlockSpec((tm, D_MODEL), tok_in),
        pl.BlockSpec((tm, M_WIDTH), tok_in), pl.BlockSpec((tm, M_WIDTH), tok_in),
        pl.BlockSpec((tm, M_WIDTH), tok_in),
        pl.BlockSpec((tm, A_WIDTH), tok_in),
        pl.BlockSpec((1, 6, D_MODEL), lambda i: (mod_row0 + ((tile0 + i) // per_seq if mod_row0 else 0), 0, 0)),
        pl.BlockSpec((1, M_WIDTH), const2),
        pl.BlockSpec((1, D_MODEL), const2),
        pl.BlockSpec((M_WIDTH, D_MODEL), const2),
        pl.BlockSpec((A_WIDTH, D_MODEL), const2),
        pl.BlockSpec((D_MODEL, P_HEADS * P_DKEY), const2),
        pl.BlockSpec((N_KEYS, P_HALF), const2),
        pl.BlockSpec((N_KEYS, P_HALF), const2),
    ]
    out_specs = [pl.BlockSpec((tm, D_MODEL), tok), pl.BlockSpec((tm, D_MODEL // 2), tok),
                 pl.BlockSpec((tm, N_SEL), tok), pl.BlockSpec((tm, N_SEL), tok)]
    out_shape = [jax.ShapeDtypeStruct((n, D_MODEL), F32), jax.ShapeDtypeStruct((n, D_MODEL // 2), I32),
                 jax.ShapeDtypeStruct((n, N_SEL), I32), jax.ShapeDtypeStruct((n, N_SEL), F32)]
    return pl.pallas_call(
        _mix_kernel, grid=(n_tiles,), in_specs=in_specs, out_specs=out_specs, out_shape=out_shape,
        scratch_shapes=[pltpu.VMEM((P_HEADS, tm, P_DKEY), BF16), pltpu.VMEM((N_SEL, tm), F32),
                        pltpu.VMEM((N_SEL, tm), F32)],
        compiler_params=pltpu.CompilerParams(dimension_semantics=("parallel",)),
        name="mix",
    )(x2d, h_f, h_b, mo, a_out, mod3, mhg, n2, w_m, w_a, w_q, sub_a, sub_b)


SC_LANES = 16
SC_CORES = 2
SC_SUBCORES = 16
SC_WORKERS = SC_CORES * SC_SUBCORES
SC_TOKENS = 8
SC_GROUP = SC_LANES
SC_NGROUPS = N_SEL // SC_GROUP
ROW_WORDS = D_MODEL // 2
SC_DOT_ROWS = 4
SC_DOT_PARTIALS = 2
SC_OWORDS = 16 * SC_LANES
HI_MASK = -65536
GELU_C0 = 0.7978845608028654
GELU_C1 = 0.044715


def _pack_bf16_pairs(x):
    half = x.shape[1] // 2
    bits = lax.bitcast_convert_type(x.astype(BF16).astype(F32), I32)
    return (bits[:, :half] & HI_MASK) | lax.shift_right_logical(bits[:, half:], jnp.int32(16))


def _sc_gelu(a):
    z = GELU_C0 * (a + GELU_C1 * (a * a * a))
    tanh = 1.0 - 2.0 / (jnp.exp(2.0 * z) + 1.0)
    return 0.5 * a * (1.0 + tanh)


def _sc_split(words):
    return (plsc.bitcast(words & HI_MASK, F32), plsc.bitcast(lax.shift_left(words, jnp.int32(16)), F32))


def _sc_mul_packed(a_words, b_words):
    return plsc.bitcast(plsc.bitcast(a_words, BF16) * plsc.bitcast(b_words, BF16), I32)


def _peer_sc_kernel(h2_hbm, eidx_hbm, gate_hbm, u_hbm, v_hbm, out_hbm,
                    xbuf, ibuf, gbuf, obuf, ubuf, vbuf, mbuf, wbuf, sem_u0, sem_u1, sem_v0, sem_v1):
    n = h2_hbm.shape[0]
    per_worker = n // SC_WORKERS
    wid = lax.axis_index("c") * SC_SUBCORES + lax.axis_index("s")
    sems_u = (sem_u0, sem_u1)
    sems_v = (sem_v0, sem_v1)
    lane = lax.iota(I32, SC_LANES)

    def split_item(item):
        return lax.shift_right_logical(item, SC_NGROUPS.bit_length() - 1), item & (SC_NGROUPS - 1)

    def gather_copies(item, slot):
        t, g = split_item(item)
        idx = ibuf[t, pl.ds(g * SC_GROUP, SC_GROUP)]
        return (pltpu.make_async_copy(u_hbm.at[idx], ubuf.at[slot], sems_u[slot]),
                pltpu.make_async_copy(v_hbm.at[idx], vbuf.at[slot], sems_v[slot]))

    def dots(t, slot):
        zero = jnp.zeros((SC_LANES,), F32)

        @pl.loop(0, SC_GROUP, step=SC_DOT_ROWS)
        def _(r0):
            accs = [[zero] * SC_DOT_PARTIALS for _ in range(SC_DOT_ROWS)]
            for k in range(ROW_WORDS // SC_LANES):
                xk = xbuf[t, pl.ds(k * SC_LANES, SC_LANES)]
                for i in range(SC_DOT_ROWS):
                    hi, lo = _sc_split(_sc_mul_packed(xk, ubuf[slot, r0 + i, pl.ds(k * SC_LANES, SC_LANES)]))
                    accs[i][k % SC_DOT_PARTIALS] = accs[i][k % SC_DOT_PARTIALS] + (hi + lo)
            for i in range(SC_DOT_ROWS):
                mbuf[r0 + i, :] = functools.reduce(lambda a, b: a + b, accs[i])

        tot = zero
        for c in range(SC_LANES):
            tot = tot + plsc.load_gather(mbuf, [lane, jnp.full((SC_LANES,), c, I32)])
        return tot

    def accumulate(t, slot):
        nv = SC_OWORDS // SC_LANES
        for oc in range(ROW_WORDS // SC_OWORDS):
            w0 = oc * SC_OWORDS
            accs = (tuple(obuf[t, pl.ds(w0 + j * SC_LANES, SC_LANES)] for j in range(nv))
                    + tuple(obuf[t, pl.ds(ROW_WORDS + w0 + j * SC_LANES, SC_LANES)] for j in range(nv)))

            def row(r, accs):
                wsplat = plsc.load_gather(wbuf, [jnp.full((SC_LANES,), r, I32)])
                his, los = [], []
                for j in range(nv):
                    hi, lo = _sc_split(_sc_mul_packed(wsplat, vbuf[slot, r, pl.ds(w0 + j * SC_LANES, SC_LANES)]))
                    his.append(accs[j] + hi)
                    los.append(accs[nv + j] + lo)
                return tuple(his) + tuple(los)

            accs = lax.fori_loop(0, SC_GROUP, row, accs)
            for j in range(nv):
                obuf[t, pl.ds(w0 + j * SC_LANES, SC_LANES)] = accs[j]
                obuf[t, pl.ds(ROW_WORDS + w0 + j * SC_LANES, SC_LANES)] = accs[nv + j]

    def pack_weights(w):
        bits = plsc.bitcast(w, I32)
        rounded = (bits + 0x7FFF + (lax.shift_right_logical(bits, jnp.int32(16)) & 1)) & HI_MASK
        return rounded | lax.shift_right_logical(rounded, jnp.int32(16))

    n_items = SC_TOKENS * SC_NGROUPS

    @pl.loop(0, per_worker // SC_TOKENS)
    def _(blk):
        tok0 = pl.multiple_of(wid * per_worker + blk * SC_TOKENS, SC_TOKENS)
        pltpu.sync_copy(h2_hbm.at[pl.ds(tok0, SC_TOKENS)], xbuf)
        pltpu.sync_copy(eidx_hbm.at[pl.ds(tok0, SC_TOKENS)], ibuf)
        pltpu.sync_copy(gate_hbm.at[pl.ds(tok0, SC_TOKENS)], gbuf)

        @pl.loop(0, SC_TOKENS)
        def _(t):
            zero = jnp.zeros((SC_LANES,), F32)
            for j in range(D_MODEL // SC_LANES):
                obuf[t, pl.ds(j * SC_LANES, SC_LANES)] = zero

        for c in gather_copies(0, 0):
            c.start()

        @pl.loop(0, n_items, step=2)
        def _(item0):
            for slot in range(2):
                item = item0 + slot
                t, g = split_item(item)

                @pl.when(item + 1 < n_items)
                def _():
                    for c in gather_copies(item + 1, 1 - slot):
                        c.start()

                for c in gather_copies(item, slot):
                    c.wait()
                a = dots(t, slot)
                wbuf[...] = pack_weights(gbuf[t, pl.ds(g * SC_GROUP, SC_GROUP)] * _sc_gelu(a))
                accumulate(t, slot)

        pltpu.sync_copy(obuf, out_hbm.at[pl.ds(tok0, SC_TOKENS)])


def _peer_experts(h2p, eidx, gates, u_pack, v_pack):
    n = h2p.shape[0]
    mesh = plsc.VectorSubcoreMesh(core_axis_name="c", subcore_axis_name="s")
    fn = pl.kernel(
        _peer_sc_kernel,
        out_type=jax.ShapeDtypeStruct((n, D_MODEL), F32),
        mesh=mesh,
        scratch_types=[
            pltpu.VMEM((SC_TOKENS, ROW_WORDS), I32),
            pltpu.VMEM((SC_TOKENS, N_SEL), I32),
            pltpu.VMEM((SC_TOKENS, N_SEL), F32),
            pltpu.VMEM((SC_TOKENS, D_MODEL), F32),
            pltpu.VMEM((2, SC_GROUP, ROW_WORDS), I32),
            pltpu.VMEM((2, SC_GROUP, ROW_WORDS), I32),
            pltpu.VMEM((SC_GROUP, SC_LANES), F32),
            pltpu.VMEM((SC_LANES,), I32),
            pltpu.SemaphoreType.DMA, pltpu.SemaphoreType.DMA, pltpu.SemaphoreType.DMA, pltpu.SemaphoreType.DMA,
        ],
        compiler_params=pltpu.CompilerParams(needs_layout_passes=False),
        cost_estimate=pl.CostEstimate(
            flops=4 * n * N_SEL * D_MODEL, transcendentals=n * N_SEL,
            bytes_accessed=4 * (2 * n * N_SEL * ROW_WORDS + n * ROW_WORDS + n * D_MODEL + 2 * n * N_SEL)),
        name="peer_experts",
    )
    return fn(h2p, eidx, gates, u_pack, v_pack)


def _resid_kernel(x1_ref, p_ref, mod_ref, o_ref):
    o_ref[...] = x1_ref[...] + mod_ref[0, 5:6, :] * p_ref[...]


def _resid_call(x1, peer_out, seq, mod3, mod_row0, tile0):
    n = x1.shape[0]
    tm = TOKEN_TILE
    per_seq = seq // tm
    tok = pl.BlockSpec((tm, D_MODEL), lambda i: (i, 0))
    return pl.pallas_call(
        _resid_kernel, grid=(n // tm,),
        in_specs=[tok, tok,
                  pl.BlockSpec((1, 6, D_MODEL),
                               lambda i: (mod_row0 + ((tile0 + i) // per_seq if mod_row0 else 0), 0, 0))],
        out_specs=tok, out_shape=jax.ShapeDtypeStruct((n, D_MODEL), F32),
        compiler_params=pltpu.CompilerParams(dimension_semantics=("parallel",)),
        name="resid",
    )(x1, peer_out, mod3)


def _rope_tables(seq, rotate):
    if not rotate:
        return jnp.ones((seq, KV_WIDTH), F32), jnp.zeros((seq, KV_WIDTH), F32)
    quarter = A_DH // 4
    t = jnp.arange(seq)
    row = (t // GRID_W).astype(F32)
    col = (t % GRID_W).astype(F32)
    inv = ROPE_BASE ** (-jnp.arange(quarter, dtype=F32) / quarter)
    d = jnp.arange(A_DH)
    pos = jnp.where(d[None, :] < A_DH // 2, row[:, None], col[:, None])
    ang = pos * inv[d % quarter][None, :]
    sign = jnp.where((d % (A_DH // 2)) < quarter, -1.0, 1.0).astype(F32)
    cos = jnp.cos(ang)
    sin = jnp.sin(ang) * sign[None, :]
    return jnp.tile(cos, (1, KV_WIDTH // A_DH)), jnp.tile(sin, (1, KV_WIDTH // A_DH))


def _layer_mixers(x, mod3, mod_row0, prm, cache, rotate):
    (n1, _, w_main, w_g, w_gt, b_g, b_gt, _, qg_t, kg_t, bd, sink) = prm[:12]
    batch, seq, _ = x.shape
    n = batch * seq
    x2d = x.reshape(n, D_MODEL)
    cos, sin = _rope_tables(seq, rotate)
    mq, mk, mv, mo, gcol, grow, aq, ak, av = _inproj_call(
        x2d, seq, mod3, mod_row0, n1, w_main, w_g, w_gt, b_g, b_gt, qg_t, kg_t, bd, cos, sin)
    kc, vc, c0, m0 = cache
    h_f, h_b, c_fin, m_fin = _mlstm_call(mq, mk, mv, gcol, grow, c0, m0, batch, seq)
    if kc is None:
        a_out = _attn_ctx_call(sink, aq, ak, av, batch, seq)
    else:
        a_out = _attn_lat_call(sink, aq, ak, av, kc, vc, batch, seq)
    return (x2d, h_f, h_b, mo, a_out), ak, av, c_fin, m_fin


def _layer_tail(mixed, seq, mod3, mod_row0, prm, tile0, n_tiles, next_input=None, gate_on=None):
    (_, n2, _, _, _, _, _, mhg, _, _, _, _, w_m, w_a, w_q, sub_a, sub_b, u_tab, v_tab) = prm
    x2d, h_f, h_b, mo, a_out = mixed
    if gate_on is not None:
        x2d, _ = lax.optimization_barrier((x2d, gate_on))
    routed = _mix_call(x2d, seq, h_f, h_b, mo, a_out, mod3, mod_row0, mhg, n2, w_m, w_a, w_q,
                       sub_a, sub_b, tile0, n_tiles)
    if next_input is not None:
        routed, next_input = lax.optimization_barrier((routed, next_input))
    x1, h2, eidx, gates = routed
    peer_out = _peer_experts(h2, eidx, gates, u_tab, v_tab)
    return _resid_call(x1, peer_out, seq, mod3, mod_row0, tile0), peer_out, next_input


def _pack_state(C, n_vec, m):
    b = C.shape[0]
    caug = jnp.concatenate([C, jnp.broadcast_to(n_vec[..., None], C.shape)], axis=-1)
    caug = caug.reshape(b, 2 * M_HEADS, M_DH, 2 * M_DH)
    m_rep = jnp.broadcast_to(m.reshape(b, 2 * M_HEADS, 1, 1), (b, 2 * M_HEADS, 8, M_DH))
    return caug.astype(F32), m_rep.astype(F32)


def kernel(x_prompt, x_sample, c, cache_attn_k, cache_attn_v, state_mlstm_C, state_mlstm_n, state_mlstm_m,
           c_ctx, w_ada, b_ada, norm1_g, norm2_g, w_in, b_gates, mh_norm_g, q_norm_g, k_norm_g, sink_logits,
           w_out, peer_w_q, peer_sub_a, peer_sub_b, peer_u, peer_v):
    depth = w_ada.shape[0]
    assert depth == 1
    batch, seq, _ = x_prompt.shape
    dec_batch, dec_seq, _ = x_sample.shape
    l = 0

    cond = jnp.concatenate([c_ctx[None, :], c, jnp.zeros((MOD_ROWS - 1 - dec_batch, D_MODEL), F32)], axis=0)
    mod3 = _ada_call(cond, w_ada[l], b_ada[l]).reshape(MOD_ROWS, 6, D_MODEL)

    wi = w_in[l]
    g0 = 4 * M_WIDTH
    w_main = jnp.concatenate([wi[:, :g0], wi[:, g0 + N_GATES:]], axis=1).astype(BF16)
    w_g = wi[:, g0:g0 + N_GATES]
    seg = jnp.arange(A_WIDTH) // A_DH
    bd = jnp.where(seg[:, None] == seg[None, :], 1.0 / A_DH, 0.0).astype(F32)
    prm = (norm1_g[l][None, :], norm2_g[l][None, :], w_main, w_g, w_g.T, b_gates[l][None, :], b_gates[l][:, None],
           mh_norm_g[l][None, :], jnp.tile(q_norm_g[l], A_HEADS)[None, :], jnp.tile(k_norm_g[l], A_KV)[None, :], bd,
           sink_logits[l], w_out[l][:M_WIDTH].astype(BF16), w_out[l][M_WIDTH:].astype(BF16),
           peer_w_q[l].astype(BF16), peer_sub_a[l].astype(BF16), peer_sub_b[l].astype(BF16),
           _pack_bf16_pairs(peer_u[l]), _pack_bf16_pairs(peer_v[l]))

    zeros_c = jnp.zeros((batch, 2, M_HEADS, M_DH, M_DH), F32)
    c0, m0 = _pack_state(zeros_c, zeros_c[..., 0], jnp.full((batch, 2, M_HEADS), NEG, F32))
    mixed_p, k_new, v_new, c_fin, m_fin = _layer_mixers(x_prompt, mod3, 0, prm, (None, None, c0, m0), False)
    y_p, peer_p, x_sample = _layer_tail(mixed_p, seq, mod3, 0, prm, 0, batch * seq // TOKEN_TILE,
                                        next_input=x_sample)
    y_p = y_p.reshape(batch, seq, D_MODEL)

    c0s, m0s = _pack_state(state_mlstm_C[:, l], state_mlstm_n[:, l], state_mlstm_m[:, l])
    past = cache_attn_k.shape[2]
    kc = cache_attn_k[:, l].reshape(dec_batch, past, KV_WIDTH)
    vc = cache_attn_v[:, l].reshape(dec_batch, past, KV_WIDTH)
    mixed_s, _, _, _, _ = _layer_mixers(x_sample, mod3, 1, prm, (kc, vc, c0s, m0s), True)
    chunk = dec_batch * dec_seq // TOKEN_TILE // LATENT_CHUNKS
    ys = []
    peers = [None, peer_p]
    for ci in range(LATENT_CHUNKS):
        y_c, peer_c, _ = _layer_tail(mixed_s, dec_seq, mod3, 1, prm, ci * chunk, chunk, gate_on=peers[ci])
        peers.append(peer_c)
        ys.append(y_c)
    y_s = jnp.concatenate(ys, axis=0).reshape(dec_batch, dec_seq, D_MODEL)

    c_fin = c_fin.reshape(batch, 2, M_HEADS, M_DH, 2 * M_DH)
    new_c = c_fin[..., :M_DH][:, None]
    new_n = c_fin[..., M_DH][:, None]
    new_m = m_fin[:, :, 0, 0].reshape(batch, 2, M_HEADS)[:, None]
    new_k = k_new.reshape(batch, 1, seq, A_KV, A_DH)
    new_v = v_new.reshape(batch, 1, seq, A_KV, A_DH)
    return y_p, y_s, new_k, new_v, new_c, new_n, new_m
```

```python
import functools

import jax
import jax.numpy as jnp
from jax import lax
from jax.experimental import pallas as pl
from jax.experimental.pallas import tpu as pltpu
from jax.experimental.pallas import tpu_sc as plsc

F32 = jnp.float32
BF16 = jnp.bfloat16
I32 = jnp.int32
HI = lax.Precision.HIGHEST

D_MODEL = 1024
EPS = 1e-6
NEG = -1e30
GRID_W = 64
M_HEADS = 4
M_WIDTH = 512
M_DH = 128
A_HEADS = 8
A_KV = 2
A_REP = 4
A_DH = 64
A_WIDTH = 512
KV_WIDTH = A_KV * A_DH
BLOCK = 128
ROPE_BASE = 10000.0
N_KEYS = 128
P_HEADS = 8
P_DKEY = 256
P_HALF = 128
P_TOPK = 16
N_SEL = P_HEADS * P_TOPK
N_GATES = 4 * M_HEADS
MAIN_COLS = 4 * M_WIDTH + A_WIDTH + 2 * KV_WIDTH
MOD_ROWS = 16

TOKEN_TILE = 256
MLSTM_CHUNK = 128
ADA_COL_TILE = 768
LATENT_CHUNKS = 4


def _sigmoid(x):
    return 1.0 / (1.0 + jnp.exp(-x))


def _log_sigmoid(x):
    return jnp.minimum(x, 0.0) - jnp.log1p(jnp.exp(-jnp.abs(x)))


def _dot_t(a, b, precision=None):
    return lax.dot_general(a, b, (((1,), (1,)), ((), ())), precision=precision,
                           preferred_element_type=F32)


def _ada_kernel(c_ref, w_ref, b_ref, o_ref):
    c = c_ref[...]
    s = c * _sigmoid(c)
    o_ref[...] = jnp.dot(s, w_ref[...], precision=HI, preferred_element_type=F32) + b_ref[...]


def _ada_call(cond, w_ada, b_ada):
    n_out = w_ada.shape[1]
    return pl.pallas_call(
        _ada_kernel,
        grid=(n_out // ADA_COL_TILE,),
        in_specs=[pl.BlockSpec((MOD_ROWS, D_MODEL), lambda j: (0, 0)),
                  pl.BlockSpec((D_MODEL, ADA_COL_TILE), lambda j: (0, j)),
                  pl.BlockSpec((1, ADA_COL_TILE), lambda j: (0, j))],
        out_specs=pl.BlockSpec((MOD_ROWS, ADA_COL_TILE), lambda j: (0, j)),
        out_shape=jax.ShapeDtypeStruct((MOD_ROWS, n_out), F32),
        name="ada",
    )(cond, w_ada, b_ada.reshape(1, n_out))


def _swap16(x):
    n = x.shape[-1]
    lane = lax.broadcasted_iota(I32, x.shape, x.ndim - 1)
    return jnp.where((lane & 16) == 0, pltpu.roll(x, n - 16, x.ndim - 1), pltpu.roll(x, 16, x.ndim - 1))


def _inproj_kernel(x_ref, mod_ref, n1_ref, w_ref, wg_ref, wgt_ref, bg_ref, bgt_ref, qg_ref, kg_ref,
                   bd_ref, cos_ref, sin_ref,
                   mq_ref, mk_ref, mv_ref, mo_ref, gc_ref, gr_ref, aq_ref, ak_ref, av_ref):
    x = x_ref[...]
    h = x * lax.rsqrt(jnp.mean(x * x, axis=-1, keepdims=True) + EPS) * n1_ref[...]
    h = h * (1.0 + mod_ref[0, 1:2, :]) + mod_ref[0, 0:1, :]
    z = jnp.dot(h.astype(BF16), w_ref[...], preferred_element_type=F32)

    mq_ref[...] = (z[:, 0:M_WIDTH] * (M_DH ** -0.5)).astype(BF16)
    mk_ref[...] = z[:, M_WIDTH:2 * M_WIDTH].astype(BF16)
    mv_ref[...] = z[:, 2 * M_WIDTH:3 * M_WIDTH].astype(BF16)
    mo_ref[...] = z[:, 3 * M_WIDTH:4 * M_WIDTH]

    g = jnp.dot(h, wg_ref[...], precision=HI, preferred_element_type=F32) + bg_ref[...]
    kind = lax.broadcasted_iota(I32, g.shape, 1) // M_HEADS
    gc_ref[...] = jnp.where((kind & 1) == 1, _log_sigmoid(g), g)
    gt = _dot_t(wgt_ref[...], h, precision=HI) + bgt_ref[...]
    kind_t = lax.broadcasted_iota(I32, gt.shape, 0) // M_HEADS
    gr_ref[...] = jnp.where((kind_t & 1) == 1, _log_sigmoid(gt), gt)

    o = 4 * M_WIDTH
    aq = z[:, o:o + A_WIDTH]
    ak = z[:, o + A_WIDTH:o + A_WIDTH + KV_WIDTH]
    av_ref[...] = z[:, o + A_WIDTH + KV_WIDTH:o + A_WIDTH + 2 * KV_WIDTH]
    bd = bd_ref[...]
    cos = cos_ref[...]
    sin = sin_ref[...]
    aq = aq * lax.rsqrt(jnp.dot(aq * aq, bd, precision=HI, preferred_element_type=F32) + EPS) * qg_ref[...]
    cos4 = jnp.concatenate([cos] * (A_WIDTH // KV_WIDTH), axis=1)
    sin4 = jnp.concatenate([sin] * (A_WIDTH // KV_WIDTH), axis=1)
    aq = (aq * cos4 + _swap16(aq) * sin4) * (A_DH ** -0.5)
    ak = ak * lax.rsqrt(jnp.dot(ak * ak, bd[0:KV_WIDTH, 0:KV_WIDTH], precision=HI,
                                preferred_element_type=F32) + EPS) * kg_ref[...]
    ak_ref[...] = ak * cos + _swap16(ak) * sin

    lane = lax.broadcasted_iota(I32, (aq.shape[0], KV_WIDTH), 1)
    for hd in range(A_HEADS):
        grp = hd // A_REP
        blk = aq[:, (hd // 2) * KV_WIDTH:(hd // 2 + 1) * KV_WIDTH]
        if hd % 2 != grp:
            blk = pltpu.roll(blk, A_DH, 1)
        keep = (lane >= grp * A_DH) & (lane < (grp + 1) * A_DH)
        aq_ref[hd] = jnp.where(keep, blk, 0.0).astype(BF16)


def _inproj_call(x2d, seq, mod3, mod_row0, n1, w_main, w_g, w_gt, b_g, b_gt, qg_t, kg_t, bd, cos, sin):
    n = x2d.shape[0]
    tm = TOKEN_TILE
    per_seq = seq // tm

    def tok(i):
        return (i, 0)

    def const2(i):
        return (0, 0)

    in_specs = [
        pl.BlockSpec((tm, D_MODEL), tok),
        pl.BlockSpec((1, 6, D_MODEL), lambda i: (mod_row0 + (i // per_seq if mod_row0 else 0), 0, 0)),
        pl.BlockSpec((1, D_MODEL), const2),
        pl.BlockSpec((D_MODEL, MAIN_COLS), const2),
        pl.BlockSpec((D_MODEL, N_GATES), const2),
        pl.BlockSpec((N_GATES, D_MODEL), const2),
        pl.BlockSpec((1, N_GATES), const2),
        pl.BlockSpec((N_GATES, 1), const2),
        pl.BlockSpec((1, A_WIDTH), const2),
        pl.BlockSpec((1, KV_WIDTH), const2),
        pl.BlockSpec((A_WIDTH, A_WIDTH), const2),
        pl.BlockSpec((tm, KV_WIDTH), lambda i: (i % per_seq, 0)),
        pl.BlockSpec((tm, KV_WIDTH), lambda i: (i % per_seq, 0)),
    ]
    out_specs = [
        pl.BlockSpec((tm, M_WIDTH), tok),
        pl.BlockSpec((tm, M_WIDTH), tok),
        pl.BlockSpec((tm, M_WIDTH), tok),
        pl.BlockSpec((tm, M_WIDTH), tok),
        pl.BlockSpec((tm, N_GATES), tok),
        pl.BlockSpec((N_GATES, tm), lambda i: (0, i)),
        pl.BlockSpec((A_HEADS, tm, KV_WIDTH), lambda i: (0, i, 0)),
        pl.BlockSpec((tm, KV_WIDTH), tok),
        pl.BlockSpec((tm, KV_WIDTH), tok),
    ]
    out_shape = [
        jax.ShapeDtypeStruct((n, M_WIDTH), BF16),
        jax.ShapeDtypeStruct((n, M_WIDTH), BF16),
        jax.ShapeDtypeStruct((n, M_WIDTH), BF16),
        jax.ShapeDtypeStruct((n, M_WIDTH), F32),
        jax.ShapeDtypeStruct((n, N_GATES), F32),
        jax.ShapeDtypeStruct((N_GATES, n), F32),
        jax.ShapeDtypeStruct((A_HEADS, n, KV_WIDTH), BF16),
        jax.ShapeDtypeStruct((n, KV_WIDTH), F32),
        jax.ShapeDtypeStruct((n, KV_WIDTH), F32),
    ]
    return pl.pallas_call(
        _inproj_kernel, grid=(n // tm,), in_specs=in_specs, out_specs=out_specs, out_shape=out_shape,
        compiler_params=pltpu.CompilerParams(dimension_semantics=("parallel",)),
        name="inproj",
    )(x2d, mod3, n1, w_main, w_g, w_gt, b_g, b_gt, qg_t, kg_t, bd, cos, sin)


def _mlstm_chain(q, k, v, li_c, lf_c, li_r, lf_r, caug, m, tri, tri_t, mask, reverse):
    L = q.shape[0]
    last = 0 if reverse else L - 1
    b_c = jnp.dot(tri, jnp.broadcast_to(lf_c, (L, L)), precision=HI, preferred_element_type=F32)
    b_r = jnp.dot(jnp.broadcast_to(lf_r, (8, L)), tri_t, precision=HI, preferred_element_type=F32)[0:1, :]
    a_inter = b_c[:, 0:1] + m
    d = jnp.where(mask, b_c - b_r + li_r, -jnp.inf)
    m_t = jnp.maximum(a_inter, jnp.max(d, axis=1, keepdims=True))
    w_inter = jnp.exp(a_inter - m_t)
    s = _dot_t(q, k) * jnp.exp(d - m_t)
    qc = jnp.dot(q, caug.astype(BF16), preferred_element_type=F32)
    num = jnp.dot(s.astype(BF16), v, preferred_element_type=F32) + w_inter * qc[:, 0:M_DH]
    den = jnp.sum(s, axis=1, keepdims=True) + w_inter * qc[:, M_DH:M_DH + 1]
    den = jnp.maximum(jnp.abs(den), jnp.exp(-m_t))
    h = num / den
    m_new = m_t[last:last + 1, :]
    b_last = b_c[last:last + 1, 0:1]
    g_c = jnp.exp(b_last - b_c[:, 0:1] + li_c - m_new)
    decay = jnp.exp(b_last + m - m_new)
    kw = (k.astype(F32) * g_c).astype(BF16)
    vaug = jnp.concatenate([v, jnp.ones_like(v)], axis=1)
    upd = lax.dot_general(kw, vaug, (((0,), (0,)), ((), ())), preferred_element_type=F32)
    return h, decay * caug + upd, m_new


def _mlstm_kernel(qf_ref, kf_ref, vf_ref, gcf_ref, grf_ref, qb_ref, kb_ref, vb_ref, gcb_ref, grb_ref,
                  c0_ref, m0_ref, hf_ref, hb_ref, cfin_ref, mfin_ref, c_scr, m_scr):
    c = pl.program_id(1)
    nc = pl.num_programs(1)
    L = qf_ref.shape[0]

    @pl.when(c == 0)
    def _():
        c_scr[...] = c0_ref[0]
        m_scr[...] = m0_ref[0]

    row = lax.broadcasted_iota(I32, (L, L), 0)
    col = lax.broadcasted_iota(I32, (L, L), 1)
    lower = row >= col
    upper = row <= col
    lower_f = lower.astype(F32)
    upper_f = upper.astype(F32)

    for direction in range(2):
        reverse = direction == 1
        q_ref, k_ref, v_ref, gc_ref, gr_ref, h_ref = (
            (qb_ref, kb_ref, vb_ref, gcb_ref, grb_ref, hb_ref) if reverse
            else (qf_ref, kf_ref, vf_ref, gcf_ref, grf_ref, hf_ref))
        tri, tri_t, mask = (upper_f, lower_f, upper) if reverse else (lower_f, upper_f, lower)
        gc = gc_ref[...]
        gr = gr_ref[...]
        for hd in range(M_HEADS):
            ch = direction * M_HEADS + hd
            sl = slice(hd * M_DH, (hd + 1) * M_DH)
            ci = 2 * direction * M_HEADS + hd
            cf = ci + M_HEADS
            h, caug, m_new = _mlstm_chain(
                q_ref[:, sl], k_ref[:, sl], v_ref[:, sl],
                gc[:, ci:ci + 1], gc[:, cf:cf + 1], gr[ci:ci + 1, :], gr[cf:cf + 1, :],
                c_scr[ch], m_scr[ch][0:1, 0:1], tri, tri_t, mask, reverse)
            h_ref[:, sl] = h
            c_scr[ch] = caug
            m_scr[ch] = jnp.broadcast_to(m_new, m_scr.shape[1:])

    @pl.when(c == nc - 1)
    def _():
        cfin_ref[0] = c_scr[...]
        mfin_ref[0] = m_scr[...]


def _mlstm_call(mq, mk, mv, gcol, grow, c0, m0, batch, seq):
    n = mq.shape[0]
    L = MLSTM_CHUNK
    nc = seq // L
    n_ch = 2 * M_HEADS

    def fwd(b, c):
        return (b * nc + c, 0)

    def bwd(b, c):
        return (b * nc + nc - 1 - c, 0)

    def fwd_t(b, c):
        return (0, b * nc + c)

    def bwd_t(b, c):
        return (0, b * nc + nc - 1 - c)

    tok = pl.BlockSpec((L, M_WIDTH), fwd)
    tok_b = pl.BlockSpec((L, M_WIDTH), bwd)
    in_specs = [tok, tok, tok, pl.BlockSpec((L, N_GATES), fwd), pl.BlockSpec((N_GATES, L), fwd_t),
                tok_b, tok_b, tok_b, pl.BlockSpec((L, N_GATES), bwd), pl.BlockSpec((N_GATES, L), bwd_t),
                pl.BlockSpec((1, n_ch, M_DH, 2 * M_DH), lambda b, c: (b, 0, 0, 0)),
                pl.BlockSpec((1, n_ch, 8, M_DH), lambda b, c: (b, 0, 0, 0))]
    out_specs = [tok, tok_b,
                 pl.BlockSpec((1, n_ch, M_DH, 2 * M_DH), lambda b, c: (b, 0, 0, 0)),
                 pl.BlockSpec((1, n_ch, 8, M_DH), lambda b, c: (b, 0, 0, 0))]
    out_shape = [jax.ShapeDtypeStruct((n, M_WIDTH), F32), jax.ShapeDtypeStruct((n, M_WIDTH), F32),
                 jax.ShapeDtypeStruct((batch, n_ch, M_DH, 2 * M_DH), F32),
                 jax.ShapeDtypeStruct((batch, n_ch, 8, M_DH), F32)]
    return pl.pallas_call(
        _mlstm_kernel, grid=(batch, nc), in_specs=in_specs, out_specs=out_specs, out_shape=out_shape,
        scratch_shapes=[pltpu.VMEM((n_ch, M_DH, 2 * M_DH), F32), pltpu.VMEM((n_ch, 8, M_DH), F32)],
        compiler_params=pltpu.CompilerParams(dimension_semantics=("parallel", "arbitrary")),
        name="mlstm",
    )(mq, mk, mv, gcol, grow, mq, mk, mv, gcol, grow, c0, m0)


def _sink_column(sink_ref, grp, rows_per_head):
    return jnp.concatenate(
        [jnp.full((rows_per_head, 1), sink_ref[grp * A_REP + r], F32) for r in range(A_REP)], axis=0)


def _store_heads(out_ref, o, grp, rows_per_head):
    for r in range(A_REP):
        hd = grp * A_REP + r
        out_ref[:, hd * A_DH:(hd + 1) * A_DH] = o[r * rows_per_head:(r + 1) * rows_per_head,
                                                  grp * A_DH:(grp + 1) * A_DH].astype(out_ref.dtype)


def _attn_ctx_kernel(sink_ref, q_ref, k_ref, v_ref, out_ref):
    s_len = k_ref.shape[0]
    k = k_ref[...].astype(BF16)
    v = v_ref[...].astype(BF16)
    for grp in range(A_KV):
        q = q_ref[grp * A_REP:(grp + 1) * A_REP].reshape(A_REP * s_len, KV_WIDTH)
        s = _dot_t(q, k)
        sk = _sink_column(sink_ref, grp, s_len)
        mx = jnp.maximum(jnp.max(s, axis=1, keepdims=True), sk)
        p = jnp.exp(s - mx)
        den = jnp.sum(p, axis=1, keepdims=True) + jnp.exp(sk - mx)
        o = jnp.dot(p.astype(BF16), v, preferred_element_type=F32) / den
        _store_heads(out_ref, o, grp, s_len)


def _attn_ctx_call(sink, aq, ak, av, batch, seq):
    n = ak.shape[0]
    return pl.pallas_call(
        _attn_ctx_kernel, grid=(batch,),
        in_specs=[pl.BlockSpec(memory_space=pltpu.SMEM),
                  pl.BlockSpec((A_HEADS, seq, KV_WIDTH), lambda b: (0, b, 0)),
                  pl.BlockSpec((seq, KV_WIDTH), lambda b: (b, 0)),
                  pl.BlockSpec((seq, KV_WIDTH), lambda b: (b, 0))],
        out_specs=pl.BlockSpec((seq, A_WIDTH), lambda b: (b, 0)),
        out_shape=jax.ShapeDtypeStruct((n, A_WIDTH), BF16),
        compiler_params=pltpu.CompilerParams(dimension_semantics=("parallel",)),
        name="attn_ctx",
    )(sink, aq, ak, av)


def _attn_lat_kernel(sink_ref, q_ref, kc_ref, vc_ref, kp_ref, kq_ref, kn_ref, vp_ref, vq_ref, vn_ref, out_ref):
    i = pl.program_id(1)
    nb = pl.num_programs(1)
    kc = kc_ref[0].astype(BF16)
    vc = vc_ref[0].astype(BF16)
    kp, kq, kn = kp_ref[...].astype(BF16), kq_ref[...].astype(BF16), kn_ref[...].astype(BF16)
    vp, vq, vn = vp_ref[...].astype(BF16), vq_ref[...].astype(BF16), vn_ref[...].astype(BF16)
    rows = A_REP * BLOCK
    qpos = lax.broadcasted_iota(I32, (rows, BLOCK), 0) % BLOCK
    kpos = lax.broadcasted_iota(I32, (rows, BLOCK), 1)
    mask_p = (kpos >= qpos) & (i > 0)
    mask_n = (kpos <= qpos) & (i < nb - 1)
    for grp in range(A_KV):
        q = q_ref[grp * A_REP:(grp + 1) * A_REP].reshape(rows, KV_WIDTH)
        s_c = _dot_t(q, kc)
        s_p = jnp.where(mask_p, _dot_t(q, kp), NEG)
        s_q = _dot_t(q, kq)
        s_n = jnp.where(mask_n, _dot_t(q, kn), NEG)
        sk = _sink_column(sink_ref, grp, BLOCK)
        mx = jnp.maximum(jnp.maximum(jnp.max(s_c, axis=1, keepdims=True), jnp.max(s_p, axis=1, keepdims=True)),
                         jnp.maximum(jnp.max(s_q, axis=1, keepdims=True), jnp.max(s_n, axis=1, keepdims=True)))
        mx = jnp.maximum(mx, sk)
        p_c, p_p, p_q, p_n = jnp.exp(s_c - mx), jnp.exp(s_p - mx), jnp.exp(s_q - mx), jnp.exp(s_n - mx)
        den = (jnp.sum(p_c, axis=1, keepdims=True) + jnp.sum(p_p, axis=1, keepdims=True)
               + jnp.sum(p_q, axis=1, keepdims=True) + jnp.sum(p_n, axis=1, keepdims=True) + jnp.exp(sk - mx))
        o = (jnp.dot(p_c.astype(BF16), vc, preferred_element_type=F32)
             + jnp.dot(p_p.astype(BF16), vp, preferred_element_type=F32)
             + jnp.dot(p_q.astype(BF16), vq, preferred_element_type=F32)
             + jnp.dot(p_n.astype(BF16), vn, preferred_element_type=F32)) / den
        _store_heads(out_ref, o, grp, BLOCK)


def _attn_lat_call(sink, aq, ak, av, kc, vc, batch, seq):
    n = ak.shape[0]
    nb = seq // BLOCK
    past = kc.shape[1]

    def cur(b, i):
        return (b * nb + i, 0)

    def prev(b, i):
        return (b * nb + jnp.maximum(i - 1, 0), 0)

    def nxt(b, i):
        return (b * nb + jnp.minimum(i + 1, nb - 1), 0)

    blk = functools.partial(pl.BlockSpec, (BLOCK, KV_WIDTH))
    cache = pl.BlockSpec((1, past, KV_WIDTH), lambda b, i: (b, 0, 0))
    return pl.pallas_call(
        _attn_lat_kernel, grid=(batch, nb),
        in_specs=[pl.BlockSpec(memory_space=pltpu.SMEM),
                  pl.BlockSpec((A_HEADS, BLOCK, KV_WIDTH), lambda b, i: (0, b * nb + i, 0)),
                  cache, cache, blk(prev), blk(cur), blk(nxt), blk(prev), blk(cur), blk(nxt)],
        out_specs=pl.BlockSpec((BLOCK, A_WIDTH), cur),
        out_shape=jax.ShapeDtypeStruct((n, A_WIDTH), BF16),
        compiler_params=pltpu.CompilerParams(dimension_semantics=("parallel", "parallel")),
        name="attn_lat",
    )(sink, aq, kc, vc, ak, ak, ak, av, av, av)


def _top16_rows(s, payload=None):
    n_rows = s.shape[0]
    rows = lax.broadcasted_iota(I32, s.shape, 0).astype(F32)
    vals, idxs, pays = [], [], []
    for _ in range(P_TOPK):
        mx = jnp.max(s, axis=0, keepdims=True)
        ix = jnp.min(jnp.where(s == mx, rows, float(n_rows)), axis=0, keepdims=True)
        hit = rows == ix
        vals.append(mx)
        idxs.append(ix)
        if payload is not None:
            pays.append(jnp.sum(jnp.where(hit, payload, 0.0), axis=0, keepdims=True))
        s = jnp.where(hit, -jnp.inf, s)
    out = (jnp.concatenate(vals, axis=0), jnp.concatenate(idxs, axis=0))
    if payload is not None:
        out += (jnp.concatenate(pays, axis=0),)
    return out


def _mix_kernel(x_ref, hf_ref, hb_ref, mo_ref, ao_ref, mod_ref, mhg_ref, n2_ref, wm_ref, wa_ref, wq_ref,
                sa_ref, sb_ref, x1_ref, h2_ref, eidx_ref, gate_ref, qp_scr, e_scr, g_scr):
    tm = x_ref.shape[0]
    hs = hf_ref[...] + hb_ref[...]
    parts = []
    for hd in range(M_HEADS):
        blk = hs[:, hd * M_DH:(hd + 1) * M_DH]
        parts.append(blk * lax.rsqrt(jnp.mean(blk * blk, axis=-1, keepdims=True) + EPS))
    m_out = _sigmoid(mo_ref[...]) * (jnp.concatenate(parts, axis=1) * mhg_ref[...])
    mix = (jnp.dot(m_out.astype(BF16), wm_ref[...], preferred_element_type=F32)
           + jnp.dot(ao_ref[...], wa_ref[...], preferred_element_type=F32))
    x1 = x_ref[...] + mod_ref[0, 2:3, :] * mix
    x1_ref[...] = x1
    h2 = x1 * lax.rsqrt(jnp.mean(x1 * x1, axis=-1, keepdims=True) + EPS) * n2_ref[...]
    h2 = h2 * (1.0 + mod_ref[0, 4:5, :]) + mod_ref[0, 3:4, :]
    h2_ref[...] = _pack_bf16_pairs(h2)
    qp = jnp.dot(h2.astype(BF16), wq_ref[...], preferred_element_type=F32)
    for p in range(P_HEADS):
        qp_scr[p] = qp[:, p * P_DKEY:(p + 1) * P_DKEY].astype(BF16)
    sub_a = sa_ref[...]
    sub_b = sb_ref[...]

    def head_body(p, carry):
        for half in range(tm // N_KEYS):
            cols = slice(half * N_KEYS, (half + 1) * N_KEYS)
            qh = qp_scr[p, pl.ds(half * N_KEYS, N_KEYS), :]
            s_a = _dot_t(sub_a, qh[:, 0:P_HALF])
            s_b = _dot_t(sub_b, qh[:, P_HALF:P_DKEY])
            va, ia = _top16_rows(s_a)
            vb, ib = _top16_rows(s_b)
            cand = jnp.concatenate([va[i:i + 1, :] + vb for i in range(P_TOPK)], axis=0)
            cidx = jnp.concatenate([ia[i:i + 1, :] * float(N_KEYS) + ib for i in range(P_TOPK)], axis=0)
            top, _, eidx = _top16_rows(cand, cidx)
            ex = jnp.exp(top - jnp.max(top, axis=0, keepdims=True))
            gates = ex / jnp.sum(ex, axis=0, keepdims=True)
            r0 = pl.multiple_of(p * P_TOPK, P_TOPK)
            e_scr[pl.ds(r0, P_TOPK), cols] = eidx
            g_scr[pl.ds(r0, P_TOPK), cols] = gates
        return carry

    lax.fori_loop(0, P_HEADS, head_body, 0)
    for half in range(tm // N_KEYS):
        cols = slice(half * N_KEYS, (half + 1) * N_KEYS)
        eidx_ref[cols, :] = e_scr[:, cols].T.astype(I32)
        gate_ref[cols, :] = g_scr[:, cols].T


def _mix_call(x2d, seq, h_f, h_b, mo, a_out, mod3, mod_row0, mhg, n2, w_m, w_a, w_q, sub_a, sub_b, tile0, n_tiles):
    tm = TOKEN_TILE
    per_seq = seq // tm
    n = n_tiles * tm

    def tok_in(i):
        return (tile0 + i, 0)

    def tok(i):
        return (i, 0)

    def const2(i):
        return (0, 0)

    in_specs = [
        pl.BlockSpec((tm, D_MODEL), tok_in),
        pl.BlockSpec((tm, M_WIDTH), tok_in), pl.BlockSpec((tm, M_WIDTH), tok_in),
        pl.BlockSpec((tm, M_WIDTH), tok_in),
        pl.BlockSpec((tm, A_WIDTH), tok_in),
        pl.BlockSpec((1, 6, D_MODEL), lambda i: (mod_row0 + ((tile0 + i) // per_seq if mod_row0 else 0), 0, 0)),
        pl.BlockSpec((1, M_WIDTH), const2),
        pl.BlockSpec((1, D_MODEL), const2),
        pl.BlockSpec((M_WIDTH, D_MODEL), const2),
        pl.BlockSpec((A_WIDTH, D_MODEL), const2),
        pl.BlockSpec((D_MODEL, P_HEADS * P_DKEY), const2),
        pl.BlockSpec((N_KEYS, P_HALF), const2),
        pl.BlockSpec((N_KEYS, P_HALF), const2),
    ]
    out_specs = [pl.BlockSpec((tm, D_MODEL), tok), pl.BlockSpec((tm, D_MODEL // 2), tok),
                 pl.BlockSpec((tm, N_SEL), tok), pl.BlockSpec((tm, N_SEL), tok)]
    out_shape = [jax.ShapeDtypeStruct((n, D_MODEL), F32), jax.ShapeDtypeStruct((n, D_MODEL // 2), I32),
                 jax.ShapeDtypeStruct((n, N_SEL), I32), jax.ShapeDtypeStruct((n, N_SEL), F32)]
    return pl.pallas_call(
        _mix_kernel, grid=(n_tiles,), in_specs=in_specs, out_specs=out_specs, out_shape=out_shape,
        scratch_shapes=[pltpu.VMEM((P_HEADS, tm, P_DKEY), BF16), pltpu.VMEM((N_SEL, tm), F32),
                        pltpu.VMEM((N_SEL, tm), F32)],
        compiler_params=pltpu.CompilerParams(dimension_semantics=("parallel",)),
        name="mix",
    )(x2d, h_f, h_b, mo, a_out, mod3, mhg, n2, w_m, w_a, w_q, sub_a, sub_b)


SC_LANES = 16
SC_CORES = 2
SC_SUBCORES = 16
SC_WORKERS = SC_CORES * SC_SUBCORES
SC_TOKENS = 8
SC_GROUP = SC_LANES
SC_NGROUPS = N_SEL // SC_GROUP
ROW_WORDS = D_MODEL // 2
SC_DOT_ROWS = 4
SC_DOT_PARTIALS = 2
SC_OWORDS = 16 * SC_LANES
HI_MASK = -65536
GELU_C0 = 0.7978845608028654
GELU_C1 = 0.044715


def _pack_bf16_pairs(x):
    half = x.shape[1] // 2
    bits = lax.bitcast_convert_type(x.astype(BF16).astype(F32), I32)
    return (bits[:, :half] & HI_MASK) | lax.shift_right_logical(bits[:, half:], jnp.int32(16))


def _sc_gelu(a):
    z = GELU_C0 * (a + GELU_C1 * (a * a * a))
    tanh = 1.0 - 2.0 / (jnp.exp(2.0 * z) + 1.0)
    return 0.5 * a * (1.0 + tanh)


def _sc_split(words):
    return (plsc.bitcast(words & HI_MASK, F32), plsc.bitcast(lax.shift_left(words, jnp.int32(16)), F32))


def _sc_mul_bf16(a_words, b_words):
    return plsc.bitcast(a_words, BF16) * plsc.bitcast(b_words, BF16)


def _sc_split_sum(p, q):
    return _sc_split(plsc.bitcast(p + q, I32))


def _peer_sc_kernel(h2_hbm, eidx_hbm, gate_hbm, u_hbm, v_hbm, out_hbm,
                    xbuf, ibuf, gbuf, obuf, ubuf, vbuf, mbuf, wbuf, sem_u0, sem_u1, sem_v0, sem_v1):
    n = h2_hbm.shape[0]
    per_worker = n // SC_WORKERS
    wid = lax.axis_index("c") * SC_SUBCORES + lax.axis_index("s")
    sems_u = (sem_u0, sem_u1)
    sems_v = (sem_v0, sem_v1)
    lane = lax.iota(I32, SC_LANES)

    def split_item(item):
        return lax.shift_right_logical(item, SC_NGROUPS.bit_length() - 1), item & (SC_NGROUPS - 1)

    def gather_copies(item, slot):
        t, g = split_item(item)
        idx = ibuf[t, pl.ds(g * SC_GROUP, SC_GROUP)]
        return (pltpu.make_async_copy(u_hbm.at[idx], ubuf.at[slot], sems_u[slot]),
                pltpu.make_async_copy(v_hbm.at[idx], vbuf.at[slot], sems_v[slot]))

    def dots(t, slot):
        zero = jnp.zeros((SC_LANES,), F32)

        @pl.loop(0, SC_GROUP, step=SC_DOT_ROWS)
        def _(r0):
            accs = [[zero] * SC_DOT_PARTIALS for _ in range(SC_DOT_ROWS)]
            for k in range(0, ROW_WORDS // SC_LANES, 2):
                xa = xbuf[t, pl.ds(k * SC_LANES, SC_LANES)]
                xb = xbuf[t, pl.ds((k + 1) * SC_LANES, SC_LANES)]
                for i in range(SC_DOT_ROWS):
                    hi, lo = _sc_split_sum(
                        _sc_mul_bf16(xa, ubuf[slot, r0 + i, pl.ds(k * SC_LANES, SC_LANES)]),
                        _sc_mul_bf16(xb, ubuf[slot, r0 + i, pl.ds((k + 1) * SC_LANES, SC_LANES)]))
                    p = (k // 2) % SC_DOT_PARTIALS
                    accs[i][p] = accs[i][p] + (hi + lo)
            for i in range(SC_DOT_ROWS):
                mbuf[r0 + i, :] = functools.reduce(lambda a, b: a + b, accs[i])

        tot = zero
        for c in range(SC_LANES):
            tot = tot + plsc.load_gather(mbuf, [lane, jnp.full((SC_LANES,), c, I32)])
        return tot

    def accumulate(t, slot):
        nv = SC_OWORDS // SC_LANES
        for oc in range(ROW_WORDS // SC_OWORDS):
            w0 = oc * SC_OWORDS
            accs = (tuple(obuf[t, pl.ds(w0 + j * SC_LANES, SC_LANES)] for j in range(nv))
                    + tuple(obuf[t, pl.ds(ROW_WORDS + w0 + j * SC_LANES, SC_LANES)] for j in range(nv)))

            def row_pair(rp, accs):
                r = 2 * rp
                wa = plsc.load_gather(wbuf, [jnp.full((SC_LANES,), r, I32)])
                wb = plsc.load_gather(wbuf, [jnp.full((SC_LANES,), r + 1, I32)])
                his, los = [], []
                for j in range(nv):
                    hi, lo = _sc_split_sum(
                        _sc_mul_bf16(wa, vbuf[slot, r, pl.ds(w0 + j * SC_LANES, SC_LANES)]),
                        _sc_mul_bf16(wb, vbuf[slot, r + 1, pl.ds(w0 + j * SC_LANES, SC_LANES)]))
                    his.append(accs[j] + hi)
                    los.append(accs[nv + j] + lo)
                return tuple(his) + tuple(los)

            accs = lax.fori_loop(0, SC_GROUP // 2, row_pair, accs)
            for j in range(nv):
                obuf[t, pl.ds(w0 + j * SC_LANES, SC_LANES)] = accs[j]
                obuf[t, pl.ds(ROW_WORDS + w0 + j * SC_LANES, SC_LANES)] = accs[nv + j]

    def pack_weights(w):
        bits = plsc.bitcast(w, I32)
        rounded = (bits + 0x7FFF + (lax.shift_right_logical(bits, jnp.int32(16)) & 1)) & HI_MASK
        return rounded | lax.shift_right_logical(rounded, jnp.int32(16))

    n_items = SC_TOKENS * SC_NGROUPS

    @pl.loop(0, per_worker // SC_TOKENS)
    def _(blk):
        tok0 = pl.multiple_of(wid * per_worker + blk * SC_TOKENS, SC_TOKENS)
        pltpu.sync_copy(h2_hbm.at[pl.ds(tok0, SC_TOKENS)], xbuf)
        pltpu.sync_copy(eidx_hbm.at[pl.ds(tok0, SC_TOKENS)], ibuf)
        pltpu.sync_copy(gate_hbm.at[pl.ds(tok0, SC_TOKENS)], gbuf)

        @pl.loop(0, SC_TOKENS)
        def _(t):
            zero = jnp.zeros((SC_LANES,), F32)
            for j in range(D_MODEL // SC_LANES):
                obuf[t, pl.ds(j * SC_LANES, SC_LANES)] = zero

        for c in gather_copies(0, 0):
            c.start()

        @pl.loop(0, n_items, step=2)
        def _(item0):
            for slot in range(2):
                item = item0 + slot
                t, g = split_item(item)

                @pl.when(item + 1 < n_items)
                def _():
                    for c in gather_copies(item + 1, 1 - slot):
                        c.start()

                for c in gather_copies(item, slot):
                    c.wait()
                a = dots(t, slot)
                wbuf[...] = pack_weights(gbuf[t, pl.ds(g * SC_GROUP, SC_GROUP)] * _sc_gelu(a))
                accumulate(t, slot)

        pltpu.sync_copy(obuf, out_hbm.at[pl.ds(tok0, SC_TOKENS)])


def _peer_experts(h2p, eidx, gates, u_pack, v_pack):
    n = h2p.shape[0]
    mesh = plsc.VectorSubcoreMesh(core_axis_name="c", subcore_axis_name="s")
    fn = pl.kernel(
        _peer_sc_kernel,
        out_type=jax.ShapeDtypeStruct((n, D_MODEL), F32),
        mesh=mesh,
        scratch_types=[
            pltpu.VMEM((SC_TOKENS, ROW_WORDS), I32),
            pltpu.VMEM((SC_TOKENS, N_SEL), I32),
            pltpu.VMEM((SC_TOKENS, N_SEL), F32),
            pltpu.VMEM((SC_TOKENS, D_MODEL), F32),
            pltpu.VMEM((2, SC_GROUP, ROW_WORDS), I32),
            pltpu.VMEM((2, SC_GROUP, ROW_WORDS), I32),
            pltpu.VMEM((SC_GROUP, SC_LANES), F32),
            pltpu.VMEM((SC_LANES,), I32),
            pltpu.SemaphoreType.DMA, pltpu.SemaphoreType.DMA, pltpu.SemaphoreType.DMA, pltpu.SemaphoreType.DMA,
        ],
        compiler_params=pltpu.CompilerParams(needs_layout_passes=False),
        cost_estimate=pl.CostEstimate(
            flops=4 * n * N_SEL * D_MODEL, transcendentals=n * N_SEL,
            bytes_accessed=4 * (2 * n * N_SEL * ROW_WORDS + n * ROW_WORDS + n * D_MODEL + 2 * n * N_SEL)),
        name="peer_experts",
    )
    return fn(h2p, eidx, gates, u_pack, v_pack)


def _resid_kernel(x1_ref, p_ref, mod_ref, o_ref):
    o_ref[...] = x1_ref[...] + mod_ref[0, 5:6, :] * p_ref[...]


def _resid_call(x1, peer_out, seq, mod3, mod_row0, tile0):
    n = x1.shape[0]
    tm = TOKEN_TILE
    per_seq = seq // tm
    tok = pl.BlockSpec((tm, D_MODEL), lambda i: (i, 0))
    return pl.pallas_call(
        _resid_kernel, grid=(n // tm,),
        in_specs=[tok, tok,
                  pl.BlockSpec((1, 6, D_MODEL),
                               lambda i: (mod_row0 + ((tile0 + i) // per_seq if mod_row0 else 0), 0, 0))],
        out_specs=tok, out_shape=jax.ShapeDtypeStruct((n, D_MODEL), F32),
        compiler_params=pltpu.CompilerParams(dimension_semantics=("parallel",)),
        name="resid",
    )(x1, peer_out, mod3)


def _rope_tables(seq, rotate):
    if not rotate:
        return jnp.ones((seq, KV_WIDTH), F32), jnp.zeros((seq, KV_WIDTH), F32)
    quarter = A_DH // 4
    t = jnp.arange(seq)
    row = (t // GRID_W).astype(F32)
    col = (t % GRID_W).astype(F32)
    inv = ROPE_BASE ** (-jnp.arange(quarter, dtype=F32) / quarter)
    d = jnp.arange(A_DH)
    pos = jnp.where(d[None, :] < A_DH // 2, row[:, None], col[:, None])
    ang = pos * inv[d % quarter][None, :]
    sign = jnp.where((d % (A_DH // 2)) < quarter, -1.0, 1.0).astype(F32)
    cos = jnp.cos(ang)
    sin = jnp.sin(ang) * sign[None, :]
    return jnp.tile(cos, (1, KV_WIDTH // A_DH)), jnp.tile(sin, (1, KV_WIDTH // A_DH))


def _layer_mixers(x, mod3, mod_row0, prm, cache, rotate):
    (n1, _, w_main, w_g, w_gt, b_g, b_gt, _, qg_t, kg_t, bd, sink) = prm[:12]
    batch, seq, _ = x.shape
    n = batch * seq
    x2d = x.reshape(n, D_MODEL)
    cos, sin = _rope_tables(seq, rotate)
    mq, mk, mv, mo, gcol, grow, aq, ak, av = _inproj_call(
        x2d, seq, mod3, mod_row0, n1, w_main, w_g, w_gt, b_g, b_gt, qg_t, kg_t, bd, cos, sin)
    kc, vc, c0, m0 = cache
    h_f, h_b, c_fin, m_fin = _mlstm_call(mq, mk, mv, gcol, grow, c0, m0, batch, seq)
    if kc is None:
        a_out = _attn_ctx_call(sink, aq, ak, av, batch, seq)
    else:
        a_out = _attn_lat_call(sink, aq, ak, av, kc, vc, batch, seq)
    return (x2d, h_f, h_b, mo, a_out), ak, av, c_fin, m_fin


def _layer_tail(mixed, seq, mod3, mod_row0, prm, tile0, n_tiles, next_input=None, gate_on=None):
    (_, n2, _, _, _, _, _, mhg, _, _, _, _, w_m, w_a, w_q, sub_a, sub_b, u_tab, v_tab) = prm
    x2d, h_f, h_b, mo, a_out = mixed
    if gate_on is not None:
        x2d, _ = lax.optimization_barrier((x2d, gate_on))
    routed = _mix_call(x2d, seq, h_f, h_b, mo, a_out, mod3, mod_row0, mhg, n2, w_m, w_a, w_q,
                       sub_a, sub_b, tile0, n_tiles)
    if next_input is not None:
        routed, next_input = lax.optimization_barrier((routed, next_input))
    x1, h2, eidx, gates = routed
    peer_out = _peer_experts(h2, eidx, gates, u_tab, v_tab)
    return _resid_call(x1, peer_out, seq, mod3, mod_row0, tile0), peer_out, next_input


def _pack_state(C, n_vec, m):
    b = C.shape[0]
    caug = jnp.concatenate([C, jnp.broadcast_to(n_vec[..., None], C.shape)], axis=-1)
    caug = caug.reshape(b, 2 * M_HEADS, M_DH, 2 * M_DH)
    m_rep = jnp.broadcast_to(m.reshape(b, 2 * M_HEADS, 1, 1), (b, 2 * M_HEADS, 8, M_DH))
    return caug.astype(F32), m_rep.astype(F32)


def kernel(x_prompt, x_sample, c, cache_attn_k, cache_attn_v, state_mlstm_C, state_mlstm_n, state_mlstm_m,
           c_ctx, w_ada, b_ada, norm1_g, norm2_g, w_in, b_gates, mh_norm_g, q_norm_g, k_norm_g, sink_logits,
           w_out, peer_w_q, peer_sub_a, peer_sub_b, peer_u, peer_v):
    depth = w_ada.shape[0]
    assert depth == 1
    batch, seq, _ = x_prompt.shape
    dec_batch, dec_seq, _ = x_sample.shape
    l = 0

    cond = jnp.concatenate([c_ctx[None, :], c, jnp.zeros((MOD_ROWS - 1 - dec_batch, D_MODEL), F32)], axis=0)
    mod3 = _ada_call(cond, w_ada[l], b_ada[l]).reshape(MOD_ROWS, 6, D_MODEL)

    wi = w_in[l]
    g0 = 4 * M_WIDTH
    w_main = jnp.concatenate([wi[:, :g0], wi[:, g0 + N_GATES:]], axis=1).astype(BF16)
    w_g = wi[:, g0:g0 + N_GATES]
    seg = jnp.arange(A_WIDTH) // A_DH
    bd = jnp.where(seg[:, None] == seg[None, :], 1.0 / A_DH, 0.0).astype(F32)
    prm = (norm1_g[l][None, :], norm2_g[l][None, :], w_main, w_g, w_g.T, b_gates[l][None, :], b_gates[l][:, None],
           mh_norm_g[l][None, :], jnp.tile(q_norm_g[l], A_HEADS)[None, :], jnp.tile(k_norm_g[l], A_KV)[None, :], bd,
           sink_logits[l], w_out[l][:M_WIDTH].astype(BF16), w_out[l][M_WIDTH:].astype(BF16),
           peer_w_q[l].astype(BF16), peer_sub_a[l].astype(BF16), peer_sub_b[l].astype(BF16),
           _pack_bf16_pairs(peer_u[l]), _pack_bf16_pairs(peer_v[l]))

    zeros_c = jnp.zeros((batch, 2, M_HEADS, M_DH, M_DH), F32)
    c0, m0 = _pack_state(zeros_c, zeros_c[..., 0], jnp.full((batch, 2, M_HEADS), NEG, F32))
    mixed_p, k_new, v_new, c_fin, m_fin = _layer_mixers(x_prompt, mod3, 0, prm, (None, None, c0, m0), False)
    y_p, peer_p, x_sample = _layer_tail(mixed_p, seq, mod3, 0, prm, 0, batch * seq // TOKEN_TILE,
                                        next_input=x_sample)
    y_p = y_p.reshape(batch, seq, D_MODEL)

    c0s, m0s = _pack_state(state_mlstm_C[:, l], state_mlstm_n[:, l], state_mlstm_m[:, l])
    past = cache_attn_k.shape[2]
    kc = cache_attn_k[:, l].reshape(dec_batch, past, KV_WIDTH)
    vc = cache_attn_v[:, l].reshape(dec_batch, past, KV_WIDTH)
    mixed_s, _, _, _, _ = _layer_mixers(x_sample, mod3, 1, prm, (kc, vc, c0s, m0s), True)
    chunk = dec_batch * dec_seq // TOKEN_TILE // LATENT_CHUNKS
    ys = []
    peers = [None, peer_p]
    for ci in range(LATENT_CHUNKS):
        y_c, peer_c, _ = _layer_tail(mixed_s, dec_seq, mod3, 1, prm, ci * chunk, chunk, gate_on=peers[ci])
        peers.append(peer_c)
        ys.append(y_c)
    y_s = jnp.concatenate(ys, axis=0).reshape(dec_batch, dec_seq, D_MODEL)

    c_fin = c_fin.reshape(batch, 2, M_HEADS, M_DH, 2 * M_DH)
    new_c = c_fin[..., :M_DH][:, None]
    new_n = c_fin[..., M_DH][:, None]
    new_m = m_fin[:, :, 0, 0].reshape(batch, 2, M_HEADS)[:, None]
    new_k = k_new.reshape(batch, 1, seq, A_KV, A_DH)
    new_v = v_new.reshape(batch, 1, seq, A_KV, A_DH)
    return y_p, y_s, new_k, new_v, new_c, new_n, new_m
```

```python
import functools

import jax
import jax.numpy as jnp
from jax import lax
from jax.experimental import pallas as pl
from jax.experimental.pallas import tpu as pltpu
from jax.experimental.pallas import tpu_sc as plsc

F32 = jnp.float32
BF16 = jnp.bfloat16
I32 = jnp.int32
HI = lax.Precision.HIGHEST

D_MODEL = 1024
EPS = 1e-6
NEG = -1e30
GRID_W = 64
M_HEADS = 4
M_WIDTH = 512
M_DH = 128
A_HEADS = 8
A_KV = 2
A_REP = 4
A_DH = 64
A_WIDTH = 512
KV_WIDTH = A_KV * A_DH
BLOCK = 128
ROPE_BASE = 10000.0
N_KEYS = 128
P_HEADS = 8
P_DKEY = 256
P_HALF = 128
P_TOPK = 16
N_SEL = P_HEADS * P_TOPK
N_GATES = 4 * M_HEADS
MAIN_COLS = 4 * M_WIDTH + A_WIDTH + 2 * KV_WIDTH
MOD_ROWS = 16

TOKEN_TILE = 256
MLSTM_CHUNK = 128
ADA_COL_TILE = 768
LATENT_CHUNKS = 4


def _sigmoid(x):
    return 1.0 / (1.0 + jnp.exp(-x))


def _log_sigmoid(x):
    return jnp.minimum(x, 0.0) - jnp.log1p(jnp.exp(-jnp.abs(x)))


def _dot_t(a, b, precision=None):
    return lax.dot_general(a, b, (((1,), (1,)), ((), ())), precision=precision,
                           preferred_element_type=F32)


def _ada_kernel(c_ref, w_ref, b_ref, o_ref):
    c = c_ref[...]
    s = c * _sigmoid(c)
    o_ref[...] = jnp.dot(s, w_ref[...], precision=HI, preferred_element_type=F32) + b_ref[...]


def _ada_call(cond, w_ada, b_ada):
    n_out = w_ada.shape[1]
    return pl.pallas_call(
        _ada_kernel,
        grid=(n_out // ADA_COL_TILE,),
        in_specs=[pl.BlockSpec((MOD_ROWS, D_MODEL), lambda j: (0, 0)),
                  pl.BlockSpec((D_MODEL, ADA_COL_TILE), lambda j: (0, j)),
                  pl.BlockSpec((1, ADA_COL_TILE), lambda j: (0, j))],
        out_specs=pl.BlockSpec((MOD_ROWS, ADA_COL_TILE), lambda j: (0, j)),
        out_shape=jax.ShapeDtypeStruct((MOD_ROWS, n_out), F32),
        name="ada",
    )(cond, w_ada, b_ada.reshape(1, n_out))


def _swap16(x):
    n = x.shape[-1]
    lane = lax.broadcasted_iota(I32, x.shape, x.ndim - 1)
    return jnp.where((lane & 16) == 0, pltpu.roll(x, n - 16, x.ndim - 1), pltpu.roll(x, 16, x.ndim - 1))


def _inproj_kernel(x_ref, mod_ref, n1_ref, w_ref, wg_ref, wgt_ref, bg_ref, bgt_ref, qg_ref, kg_ref,
                   bd_ref, cos_ref, sin_ref,
                   mq_ref, mk_ref, mv_ref, mo_ref, gc_ref, gr_ref, aq_ref, ak_ref, av_ref):
    x = x_ref[...]
    h = x * lax.rsqrt(jnp.mean(x * x, axis=-1, keepdims=True) + EPS) * n1_ref[...]
    h = h * (1.0 + mod_ref[0, 1:2, :]) + mod_ref[0, 0:1, :]
    z = jnp.dot(h.astype(BF16), w_ref[...], preferred_element_type=F32)

    mq_ref[...] = (z[:, 0:M_WIDTH] * (M_DH ** -0.5)).astype(BF16)
    mk_ref[...] = z[:, M_WIDTH:2 * M_WIDTH].astype(BF16)
    mv_ref[...] = z[:, 2 * M_WIDTH:3 * M_WIDTH].astype(BF16)
    mo_ref[...] = z[:, 3 * M_WIDTH:4 * M_WIDTH]

    g = jnp.dot(h, wg_ref[...], precision=HI, preferred_element_type=F32) + bg_ref[...]
    kind = lax.broadcasted_iota(I32, g.shape, 1) // M_HEADS
    gc_ref[...] = jnp.where((kind & 1) == 1, _log_sigmoid(g), g)
    gt = _dot_t(wgt_ref[...], h, precision=HI) + bgt_ref[...]
    kind_t = lax.broadcasted_iota(I32, gt.shape, 0) // M_HEADS
    gr_ref[...] = jnp.where((kind_t & 1) == 1, _log_sigmoid(gt), gt)

    o = 4 * M_WIDTH
    aq = z[:, o:o + A_WIDTH]
    ak = z[:, o + A_WIDTH:o + A_WIDTH + KV_WIDTH]
    av_ref[...] = z[:, o + A_WIDTH + KV_WIDTH:o + A_WIDTH + 2 * KV_WIDTH]
    bd = bd_ref[...]
    cos = cos_ref[...]
    sin = sin_ref[...]
    aq = aq * lax.rsqrt(jnp.dot(aq * aq, bd, precision=HI, preferred_element_type=F32) + EPS) * qg_ref[...]
    cos4 = jnp.concatenate([cos] * (A_WIDTH // KV_WIDTH), axis=1)
    sin4 = jnp.concatenate([sin] * (A_WIDTH // KV_WIDTH), axis=1)
    aq = (aq * cos4 + _swap16(aq) * sin4) * (A_DH ** -0.5)
    ak = ak * lax.rsqrt(jnp.dot(ak * ak, bd[0:KV_WIDTH, 0:KV_WIDTH], precision=HI,
                                preferred_element_type=F32) + EPS) * kg_ref[...]
    ak_ref[...] = ak * cos + _swap16(ak) * sin

    lane = lax.broadcasted_iota(I32, (aq.shape[0], KV_WIDTH), 1)
    for hd in range(A_HEADS):
        grp = hd // A_REP
        blk = aq[:, (hd // 2) * KV_WIDTH:(hd // 2 + 1) * KV_WIDTH]
        if hd % 2 != grp:
            blk = pltpu.roll(blk, A_DH, 1)
        keep = (lane >= grp * A_DH) & (lane < (grp + 1) * A_DH)
        aq_ref[hd] = jnp.where(keep, blk, 0.0).astype(BF16)


def _inproj_call(x2d, seq, mod3, mod_row0, n1, w_main, w_g, w_gt, b_g, b_gt, qg_t, kg_t, bd, cos, sin):
    n = x2d.shape[0]
    tm = TOKEN_TILE
    per_seq = seq // tm

    def tok(i):
        return (i, 0)

    def const2(i):
        return (0, 0)

    in_specs = [
        pl.BlockSpec((tm, D_MODEL), tok),
        pl.BlockSpec((1, 6, D_MODEL), lambda i: (mod_row0 + (i // per_seq if mod_row0 else 0), 0, 0)),
        pl.BlockSpec((1, D_MODEL), const2),
        pl.BlockSpec((D_MODEL, MAIN_COLS), const2),
        pl.BlockSpec((D_MODEL, N_GATES), const2),
        pl.BlockSpec((N_GATES, D_MODEL), const2),
        pl.BlockSpec((1, N_GATES), const2),
        pl.BlockSpec((N_GATES, 1), const2),
        pl.BlockSpec((1, A_WIDTH), const2),
        pl.BlockSpec((1, KV_WIDTH), const2),
        pl.BlockSpec((A_WIDTH, A_WIDTH), const2),
        pl.BlockSpec((tm, KV_WIDTH), lambda i: (i % per_seq, 0)),
        pl.BlockSpec((tm, KV_WIDTH), lambda i: (i % per_seq, 0)),
    ]
    out_specs = [
        pl.BlockSpec((tm, M_WIDTH), tok),
        pl.BlockSpec((tm, M_WIDTH), tok),
        pl.BlockSpec((tm, M_WIDTH), tok),
        pl.BlockSpec((tm, M_WIDTH), tok),
        pl.BlockSpec((tm, N_GATES), tok),
        pl.BlockSpec((N_GATES, tm), lambda i: (0, i)),
        pl.BlockSpec((A_HEADS, tm, KV_WIDTH), lambda i: (0, i, 0)),
        pl.BlockSpec((tm, KV_WIDTH), tok),
        pl.BlockSpec((tm, KV_WIDTH), tok),
    ]
    out_shape = [
        jax.ShapeDtypeStruct((n, M_WIDTH), BF16),
        jax.ShapeDtypeStruct((n, M_WIDTH), BF16),
        jax.ShapeDtypeStruct((n, M_WIDTH), BF16),
        jax.ShapeDtypeStruct((n, M_WIDTH), F32),
        jax.ShapeDtypeStruct((n, N_GATES), F32),
        jax.ShapeDtypeStruct((N_GATES, n), F32),
        jax.ShapeDtypeStruct((A_HEADS, n, KV_WIDTH), BF16),
        jax.ShapeDtypeStruct((n, KV_WIDTH), F32),
        jax.ShapeDtypeStruct((n, KV_WIDTH), F32),
    ]
    return pl.pallas_call(
        _inproj_kernel, grid=(n // tm,), in_specs=in_specs, out_specs=out_specs, out_shape=out_shape,
        compiler_params=pltpu.CompilerParams(dimension_semantics=("parallel",)),
        name="inproj",
    )(x2d, mod3, n1, w_main, w_g, w_gt, b_g, b_gt, qg_t, kg_t, bd, cos, sin)


def _mlstm_chain(q, k, v, li_c, lf_c, li_r, lf_r, caug, m, tri, tri_t, mask, reverse):
    L = q.shape[0]
    last = 0 if reverse else L - 1
    b_c = jnp.dot(tri, jnp.broadcast_to(lf_c, (L, L)), precision=HI, preferred_element_type=F32)
    b_r = jnp.dot(jnp.broadcast_to(lf_r, (8, L)), tri_t, precision=HI, preferred_element_type=F32)[0:1, :]
    a_inter = b_c[:, 0:1] + m
    d = jnp.where(mask, b_c - b_r + li_r, -jnp.inf)
    m_t = jnp.maximum(a_inter, jnp.max(d, axis=1, keepdims=True))
    w_inter = jnp.exp(a_inter - m_t)
    s = _dot_t(q, k) * jnp.exp(d - m_t)
    qc = jnp.dot(q, caug.astype(BF16), preferred_element_type=F32)
    num = jnp.dot(s.astype(BF16), v, preferred_element_type=F32) + w_inter * qc[:, 0:M_DH]
    den = jnp.sum(s, axis=1, keepdims=True) + w_inter * qc[:, M_DH:M_DH + 1]
    den = jnp.maximum(jnp.abs(den), jnp.exp(-m_t))
    h = num / den
    m_new = m_t[last:last + 1, :]
    b_last = b_c[last:last + 1, 0:1]
    g_c = jnp.exp(b_last - b_c[:, 0:1] + li_c - m_new)
    decay = jnp.exp(b_last + m - m_new)
    kw = (k.astype(F32) * g_c).astype(BF16)
    vaug = jnp.concatenate([v, jnp.ones_like(v)], axis=1)
    upd = lax.dot_general(kw, vaug, (((0,), (0,)), ((), ())), preferred_element_type=F32)
    return h, decay * caug + upd, m_new


def _mlstm_kernel(qf_ref, kf_ref, vf_ref, gcf_ref, grf_ref, qb_ref, kb_ref, vb_ref, gcb_ref, grb_ref,
                  c0_ref, m0_ref, hf_ref, hb_ref, cfin_ref, mfin_ref, c_scr, m_scr):
    c = pl.program_id(1)
    nc = pl.num_programs(1)
    L = qf_ref.shape[0]

    @pl.when(c == 0)
    def _():
        c_scr[...] = c0_ref[0]
        m_scr[...] = m0_ref[0]

    row = lax.broadcasted_iota(I32, (L, L), 0)
    col = lax.broadcasted_iota(I32, (L, L), 1)
    lower = row >= col
    upper = row <= col
    lower_f = lower.astype(F32)
    upper_f = upper.astype(F32)

    for direction in range(2):
        reverse = direction == 1
        q_ref, k_ref, v_ref, gc_ref, gr_ref, h_ref = (
            (qb_ref, kb_ref, vb_ref, gcb_ref, grb_ref, hb_ref) if reverse
            else (qf_ref, kf_ref, vf_ref, gcf_ref, grf_ref, hf_ref))
        tri, tri_t, mask = (upper_f, lower_f, upper) if reverse else (lower_f, upper_f, lower)
        gc = gc_ref[...]
        gr = gr_ref[...]
        for hd in range(M_HEADS):
            ch = direction * M_HEADS + hd
            sl = slice(hd * M_DH, (hd + 1) * M_DH)
            ci = 2 * direction * M_HEADS + hd
            cf = ci + M_HEADS
            h, caug, m_new = _mlstm_chain(
                q_ref[:, sl], k_ref[:, sl], v_ref[:, sl],
                gc[:, ci:ci + 1], gc[:, cf:cf + 1], gr[ci:ci + 1, :], gr[cf:cf + 1, :],
                c_scr[ch], m_scr[ch][0:1, 0:1], tri, tri_t, mask, reverse)
            h_ref[:, sl] = h
            c_scr[ch] = caug
            m_scr[ch] = jnp.broadcast_to(m_new, m_scr.shape[1:])

    @pl.when(c == nc - 1)
    def _():
        cfin_ref[0] = c_scr[...]
        mfin_ref[0] = m_scr[...]


def _mlstm_call(mq, mk, mv, gcol, grow, c0, m0, batch, seq):
    n = mq.shape[0]
    L = MLSTM_CHUNK
    nc = seq // L
    n_ch = 2 * M_HEADS

    def fwd(b, c):
        return (b * nc + c, 0)

    def bwd(b, c):
        return (b * nc + nc - 1 - c, 0)

    def fwd_t(b, c):
        return (0, b * nc + c)

    def bwd_t(b, c):
        return (0, b * nc + nc - 1 - c)

    tok = pl.BlockSpec((L, M_WIDTH), fwd)
    tok_b = pl.BlockSpec((L, M_WIDTH), bwd)
    in_specs = [tok, tok, tok, pl.BlockSpec((L, N_GATES), fwd), pl.BlockSpec((N_GATES, L), fwd_t),
                tok_b, tok_b, tok_b, pl.BlockSpec((L, N_GATES), bwd), pl.BlockSpec((N_GATES, L), bwd_t),
                pl.BlockSpec((1, n_ch, M_DH, 2 * M_DH), lambda b, c: (b, 0, 0, 0)),
                pl.BlockSpec((1, n_ch, 8, M_DH), lambda b, c: (b, 0, 0, 0))]
    out_specs = [tok, tok_b,
                 pl.BlockSpec((1, n_ch, M_DH, 2 * M_DH), lambda b, c: (b, 0, 0, 0)),
                 pl.BlockSpec((1, n_ch, 8, M_DH), lambda b, c: (b, 0, 0, 0))]
    out_shape = [jax.ShapeDtypeStruct((n, M_WIDTH), F32), jax.ShapeDtypeStruct((n, M_WIDTH), F32),
                 jax.ShapeDtypeStruct((batch, n_ch, M_DH, 2 * M_DH), F32),
                 jax.ShapeDtypeStruct((batch, n_ch, 8, M_DH), F32)]
    return pl.pallas_call(
        _mlstm_kernel, grid=(batch, nc), in_specs=in_specs, out_specs=out_specs, out_shape=out_shape,
        scratch_shapes=[pltpu.VMEM((n_ch, M_DH, 2 * M_DH), F32), pltpu.VMEM((n_ch, 8, M_DH), F32)],
        compiler_params=pltpu.CompilerParams(dimension_semantics=("parallel", "arbitrary")),
        name="mlstm",
    )(mq, mk, mv, gcol, grow, mq, mk, mv, gcol, grow, c0, m0)


def _sink_column(sink_ref, grp, rows_per_head):
    return jnp.concatenate(
        [jnp.full((rows_per_head, 1), sink_ref[grp * A_REP + r], F32) for r in range(A_REP)], axis=0)


def _store_heads(out_ref, o, grp, rows_per_head):
    for r in range(A_REP):
        hd = grp * A_REP + r
        out_ref[:, hd * A_DH:(hd + 1) * A_DH] = o[r * rows_per_head:(r + 1) * rows_per_head,
                                                  grp * A_DH:(grp + 1) * A_DH].astype(out_ref.dtype)


def _attn_ctx_kernel(sink_ref, q_ref, k_ref, v_ref, out_ref):
    s_len = k_ref.shape[0]
    k = k_ref[...].astype(BF16)
    v = v_ref[...].astype(BF16)
    for grp in range(A_KV):
        q = q_ref[grp * A_REP:(grp + 1) * A_REP].reshape(A_REP * s_len, KV_WIDTH)
        s = _dot_t(q, k)
        sk = _sink_column(sink_ref, grp, s_len)
        mx = jnp.maximum(jnp.max(s, axis=1, keepdims=True), sk)
        p = jnp.exp(s - mx)
        den = jnp.sum(p, axis=1, keepdims=True) + jnp.exp(sk - mx)
        o = jnp.dot(p.astype(BF16), v, preferred_element_type=F32) / den
        _store_heads(out_ref, o, grp, s_len)


def _attn_ctx_call(sink, aq, ak, av, batch, seq):
    n = ak.shape[0]
    return pl.pallas_call(
        _attn_ctx_kernel, grid=(batch,),
        in_specs=[pl.BlockSpec(memory_space=pltpu.SMEM),
                  pl.BlockSpec((A_HEADS, seq, KV_WIDTH), lambda b: (0, b, 0)),
                  pl.BlockSpec((seq, KV_WIDTH), lambda b: (b, 0)),
                  pl.BlockSpec((seq, KV_WIDTH), lambda b: (b, 0))],
        out_specs=pl.BlockSpec((seq, A_WIDTH), lambda b: (b, 0)),
        out_shape=jax.ShapeDtypeStruct((n, A_WIDTH), BF16),
        compiler_params=pltpu.CompilerParams(dimension_semantics=("parallel",)),
        name="attn_ctx",
    )(sink, aq, ak, av)


def _attn_lat_kernel(sink_ref, q_ref, kc_ref, vc_ref, kp_ref, kq_ref, kn_ref, vp_ref, vq_ref, vn_ref, out_ref):
    i = pl.program_id(1)
    nb = pl.num_programs(1)
    kc = kc_ref[0].astype(BF16)
    vc = vc_ref[0].astype(BF16)
    kp, kq, kn = kp_ref[...].astype(BF16), kq_ref[...].astype(BF16), kn_ref[...].astype(BF16)
    vp, vq, vn = vp_ref[...].astype(BF16), vq_ref[...].astype(BF16), vn_ref[...].astype(BF16)
    rows = A_REP * BLOCK
    qpos = lax.broadcasted_iota(I32, (rows, BLOCK), 0) % BLOCK
    kpos = lax.broadcasted_iota(I32, (rows, BLOCK), 1)
    mask_p = (kpos >= qpos) & (i > 0)
    mask_n = (kpos <= qpos) & (i < nb - 1)
    for grp in range(A_KV):
        q = q_ref[grp * A_REP:(grp + 1) * A_REP].reshape(rows, KV_WIDTH)
        s_c = _dot_t(q, kc)
        s_p = jnp.where(mask_p, _dot_t(q, kp), NEG)
        s_q = _dot_t(q, kq)
        s_n = jnp.where(mask_n, _dot_t(q, kn), NEG)
        sk = _sink_column(sink_ref, grp, BLOCK)
        mx = jnp.maximum(jnp.maximum(jnp.max(s_c, axis=1, keepdims=True), jnp.max(s_p, axis=1, keepdims=True)),
                         jnp.maximum(jnp.max(s_q, axis=1, keepdims=True), jnp.max(s_n, axis=1, keepdims=True)))
        mx = jnp.maximum(mx, sk)
        p_c, p_p, p_q, p_n = jnp.exp(s_c - mx), jnp.exp(s_p - mx), jnp.exp(s_q - mx), jnp.exp(s_n - mx)
        den = (jnp.sum(p_c, axis=1, keepdims=True) + jnp.sum(p_p, axis=1, keepdims=True)
               + jnp.sum(p_q, axis=1, keepdims=True) + jnp.sum(p_n, axis=1, keepdims=True) + jnp.exp(sk - mx))
        o = (jnp.dot(p_c.astype(BF16), vc, preferred_element_type=F32)
             + jnp.dot(p_p.astype(BF16), vp, preferred_element_type=F32)
             + jnp.dot(p_q.astype(BF16), vq, preferred_element_type=F32)
             + jnp.dot(p_n.astype(BF16), vn, preferred_element_type=F32)) / den
        _store_heads(out_ref, o, grp, BLOCK)


def _attn_lat_call(sink, aq, ak, av, kc, vc, batch, seq):
    n = ak.shape[0]
    nb = seq // BLOCK
    past = kc.shape[1]

    def cur(b, i):
        return (b * nb + i, 0)

    def prev(b, i):
        return (b * nb + jnp.maximum(i - 1, 0), 0)

    def nxt(b, i):
        return (b * nb + jnp.minimum(i + 1, nb - 1), 0)

    blk = functools.partial(pl.BlockSpec, (BLOCK, KV_WIDTH))
    cache = pl.BlockSpec((1, past, KV_WIDTH), lambda b, i: (b, 0, 0))
    return pl.pallas_call(
        _attn_lat_kernel, grid=(batch, nb),
        in_specs=[pl.BlockSpec(memory_space=pltpu.SMEM),
                  pl.BlockSpec((A_HEADS, BLOCK, KV_WIDTH), lambda b, i: (0, b * nb + i, 0)),
                  cache, cache, blk(prev), blk(cur), blk(nxt), blk(prev), blk(cur), blk(nxt)],
        out_specs=pl.BlockSpec((BLOCK, A_WIDTH), cur),
        out_shape=jax.ShapeDtypeStruct((n, A_WIDTH), BF16),
        compiler_params=pltpu.CompilerParams(dimension_semantics=("parallel", "parallel")),
        name="attn_lat",
    )(sink, aq, kc, vc, ak, ak, ak, av, av, av)


def _top16_rows(s, payload=None):
    n_rows = s.shape[0]
    rows = lax.broadcasted_iota(I32, s.shape, 0).astype(F32)
    vals, idxs, pays = [], [], []
    for _ in range(P_TOPK):
        mx = jnp.max(s, axis=0, keepdims=True)
        ix = jnp.min(jnp.where(s == mx, rows, float(n_rows)), axis=0, keepdims=True)
        hit = rows == ix
        vals.append(mx)
        idxs.append(ix)
        if payload is not None:
            pays.append(jnp.sum(jnp.where(hit, payload, 0.0), axis=0, keepdims=True))
        s = jnp.where(hit, -jnp.inf, s)
    out = (jnp.concatenate(vals, axis=0), jnp.concatenate(idxs, axis=0))
    if payload is not None:
        out += (jnp.concatenate(pays, axis=0),)
    return out


def _mix_kernel(x_ref, hf_ref, hb_ref, mo_ref, ao_ref, mod_ref, mhg_ref, n2_ref, wm_ref, wa_ref, wq_ref,
                sa_ref, sb_ref, x1_ref, h2_ref, eidx_ref, gate_ref, qp_scr, e_scr, g_scr):
    tm = x_ref.shape[0]
    hs = hf_ref[...] + hb_ref[...]
    parts = []
    for hd in range(M_HEADS):
        blk = hs[:, hd * M_DH:(hd + 1) * M_DH]
        parts.append(blk * lax.rsqrt(jnp.mean(blk * blk, axis=-1, keepdims=True) + EPS))
    m_out = _sigmoid(mo_ref[...]) * (jnp.concatenate(parts, axis=1) * mhg_ref[...])
    mix = (jnp.dot(m_out.astype(BF16), wm_ref[...], preferred_element_type=F32)
           + jnp.dot(ao_ref[...], wa_ref[...], preferred_element_type=F32))
    x1 = x_ref[...] + mod_ref[0, 2:3, :] * mix
    x1_ref[...] = x1
    h2 = x1 * lax.rsqrt(jnp.mean(x1 * x1, axis=-1, keepdims=True) + EPS) * n2_ref[...]
    h2 = h2 * (1.0 + mod_ref[0, 4:5, :]) + mod_ref[0, 3:4, :]
    h2_ref[...] = _pack_bf16_pairs(h2)
    qp = jnp.dot(h2.astype(BF16), wq_ref[...], preferred_element_type=F32)
    for p in range(P_HEADS):
        qp_scr[p] = qp[:, p * P_DKEY:(p + 1) * P_DKEY].astype(BF16)
    sub_a = sa_ref[...]
    sub_b = sb_ref[...]

    def head_body(p, carry):
        for half in range(tm // N_KEYS):
            cols = slice(half * N_KEYS, (half + 1) * N_KEYS)
            qh = qp_scr[p, pl.ds(half * N_KEYS, N_KEYS), :]
            s_a = _dot_t(sub_a, qh[:, 0:P_HALF])
            s_b = _dot_t(sub_b, qh[:, P_HALF:P_DKEY])
            va, ia = _top16_rows(s_a)
            vb, ib = _top16_rows(s_b)
            cand = jnp.concatenate([va[i:i + 1, :] + vb for i in range(P_TOPK)], axis=0)
            cidx = jnp.concatenate([ia[i:i + 1, :] * float(N_KEYS) + ib for i in range(P_TOPK)], axis=0)
            top, _, eidx = _top16_rows(cand, cidx)
            ex = jnp.exp(top - jnp.max(top, axis=0, keepdims=True))
            gates = ex / jnp.sum(ex, axis=0, keepdims=True)
            r0 = pl.multiple_of(p * P_TOPK, P_TOPK)
            e_scr[pl.ds(r0, P_TOPK), cols] = eidx
            g_scr[pl.ds(r0, P_TOPK), cols] = gates
        return carry

    lax.fori_loop(0, P_HEADS, head_body, 0)
    for half in range(tm // N_KEYS):
        cols = slice(half * N_KEYS, (half + 1) * N_KEYS)
        eidx_ref[cols, :] = e_scr[:, cols].T.astype(I32)
        gate_ref[cols, :] = g_scr[:, cols].T


def _mix_call(x2d, seq, h_f, h_b, mo, a_out, mod3, mod_row0, mhg, n2, w_m, w_a, w_q, sub_a, sub_b, tile0, n_tiles):
    tm = TOKEN_TILE
    per_seq = seq // tm
    n = n_tiles * tm

    def tok_in(i):
        return (tile0 + i, 0)

    def tok(i):
        return (i, 0)

    def const2(i):
        return (0, 0)

    in_specs = [
        pl.BlockSpec((tm, D_MODEL), tok_in),
        pl.BlockSpec((tm, M_WIDTH), tok_in), pl.BlockSpec((tm, M_WIDTH), tok_in),
        pl.BlockSpec((tm, M_WIDTH), tok_in),
        pl.BlockSpec((tm, A_WIDTH), tok_in),
        pl.BlockSpec((1, 6, D_MODEL), lambda i: (mod_row0 + ((tile0 + i) // per_seq if mod_row0 else 0), 0, 0)),
        pl.BlockSpec((1, M_WIDTH), const2),
        pl.BlockSpec((1, D_MODEL), const2),
        pl.BlockSpec((M_WIDTH, D_MODEL), const2),
        pl.BlockSpec((A_WIDTH, D_MODEL), const2),
        pl.BlockSpec((D_MODEL, P_HEADS * P_DKEY), const2),
        pl.BlockSpec((N_KEYS, P_HALF), const2),
        pl.BlockSpec((N_KEYS, P_HALF), const2),
    ]
    out_specs = [pl.BlockSpec((tm, D_MODEL), tok), pl.BlockSpec((tm, D_MODEL // 2), tok),
                 pl.BlockSpec((tm, N_SEL), tok), pl.BlockSpec((tm, N_SEL), tok)]
    out_shape = [jax.ShapeDtypeStruct((n, D_MODEL), F32), jax.ShapeDtypeStruct((n, D_MODEL // 2), I32),
                 jax.ShapeDtypeStruct((n, N_SEL), I32), jax.ShapeDtypeStruct((n, N_SEL), F32)]
    return pl.pallas_call(
        _mix_kernel, grid=(n_tiles,), in_specs=in_specs, out_specs=out_specs, out_shape=out_shape,
        scratch_shapes=[pltpu.VMEM((P_HEADS, tm, P_DKEY), BF16), pltpu.VMEM((N_SEL, tm), F32),
                        pltpu.VMEM((N_SEL, tm), F32)],
        compiler_params=pltpu.CompilerParams(dimension_semantics=("parallel",)),
        name="mix",
    )(x2d, h_f, h_b, mo, a_out, mod3, mhg, n2, w_m, w_a, w_q, sub_a, sub_b)


SC_LANES = 16
SC_CORES = 2
SC_SUBCORES = 16
SC_WORKERS = SC_CORES * SC_SUBCORES
SC_TOKENS = 8
SC_GROUP = SC_LANES
SC_NGROUPS = N_SEL // SC_GROUP
SC_SLOTS = 4
ROW_WORDS = D_MODEL // 2
SC_DOT_ROWS = 4
SC_DOT_PARTIALS = 2
SC_OWORDS = 16 * SC_LANES
HI_MASK = -65536
GELU_C0 = 0.7978845608028654
GELU_C1 = 0.044715


def _pack_bf16_pairs(x):
    half = x.shape[1] // 2
    bits = lax.bitcast_convert_type(x.astype(BF16).astype(F32), I32)
    return (bits[:, :half] & HI_MASK) | lax.shift_right_logical(bits[:, half:], jnp.int32(16))


def _sc_gelu(a):
    z = GELU_C0 * (a + GELU_C1 * (a * a * a))
    tanh = 1.0 - 2.0 / (jnp.exp(2.0 * z) + 1.0)
    return 0.5 * a * (1.0 + tanh)


def _sc_split(words):
    return (plsc.bitcast(words & HI_MASK, F32), plsc.bitcast(lax.shift_left(words, jnp.int32(16)), F32))


def _sc_mul_bf16(a_words, b_words):
    return plsc.bitcast(a_words, BF16) * plsc.bitcast(b_words, BF16)


def _sc_split_sum(p, q):
    return _sc_split(plsc.bitcast(p + q, I32))


def _peer_sc_kernel(h2_hbm, eidx_hbm, gate_hbm, u_hbm, v_hbm, out_hbm,
                    xbuf, ibuf, gbuf, obuf, ubuf, vbuf, mbuf, wbuf, sems_u, sems_v):
    n = h2_hbm.shape[0]
    per_worker = n // SC_WORKERS
    wid = lax.axis_index("c") * SC_SUBCORES + lax.axis_index("s")
    lane = lax.iota(I32, SC_LANES)

    def split_item(item):
        return lax.shift_right_logical(item, SC_NGROUPS.bit_length() - 1), item & (SC_NGROUPS - 1)

    def gather_copies(item, slot):
        t, g = split_item(item)
        idx = ibuf[t, pl.ds(g * SC_GROUP, SC_GROUP)]
        return (pltpu.make_async_copy(u_hbm.at[idx], ubuf.at[slot], sems_u.at[slot]),
                pltpu.make_async_copy(v_hbm.at[idx], vbuf.at[slot], sems_v.at[slot]))

    def dots(t, slot):
        zero = jnp.zeros((SC_LANES,), F32)

        @pl.loop(0, SC_GROUP, step=SC_DOT_ROWS)
        def _(r0):
            accs = [[zero] * SC_DOT_PARTIALS for _ in range(SC_DOT_ROWS)]
            for k in range(0, ROW_WORDS // SC_LANES, 2):
                xa = xbuf[t, pl.ds(k * SC_LANES, SC_LANES)]
                xb = xbuf[t, pl.ds((k + 1) * SC_LANES, SC_LANES)]
                for i in range(SC_DOT_ROWS):
                    hi, lo = _sc_split_sum(
                        _sc_mul_bf16(xa, ubuf[slot, r0 + i, pl.ds(k * SC_LANES, SC_LANES)]),
                        _sc_mul_bf16(xb, ubuf[slot, r0 + i, pl.ds((k + 1) * SC_LANES, SC_LANES)]))
                    p = (k // 2) % SC_DOT_PARTIALS
                    accs[i][p] = accs[i][p] + (hi + lo)
            for i in range(SC_DOT_ROWS):
                mbuf[r0 + i, :] = functools.reduce(lambda a, b: a + b, accs[i])

        tot = zero
        for c in range(SC_LANES):
            tot = tot + plsc.load_gather(mbuf, [lane, jnp.full((SC_LANES,), c, I32)])
        return tot

    def accumulate(t, slot):
        nv = SC_OWORDS // SC_LANES
        for oc in range(ROW_WORDS // SC_OWORDS):
            w0 = oc * SC_OWORDS
            accs = (tuple(obuf[t, pl.ds(w0 + j * SC_LANES, SC_LANES)] for j in range(nv))
                    + tuple(obuf[t, pl.ds(ROW_WORDS + w0 + j * SC_LANES, SC_LANES)] for j in range(nv)))

            def row_pair(rp, accs):
                r = 2 * rp
                wa = plsc.load_gather(wbuf, [jnp.full((SC_LANES,), r, I32)])
                wb = plsc.load_gather(wbuf, [jnp.full((SC_LANES,), r + 1, I32)])
                his, los = [], []
                for j in range(nv):
                    hi, lo = _sc_split_sum(
                        _sc_mul_bf16(wa, vbuf[slot, r, pl.ds(w0 + j * SC_LANES, SC_LANES)]),
                        _sc_mul_bf16(wb, vbuf[slot, r + 1, pl.ds(w0 + j * SC_LANES, SC_LANES)]))
                    his.append(accs[j] + hi)
                    los.append(accs[nv + j] + lo)
                return tuple(his) + tuple(los)

            accs = lax.fori_loop(0, SC_GROUP // 2, row_pair, accs)
            for j in range(nv):
                obuf[t, pl.ds(w0 + j * SC_LANES, SC_LANES)] = accs[j]
                obuf[t, pl.ds(ROW_WORDS + w0 + j * SC_LANES, SC_LANES)] = accs[nv + j]

    def pack_weights(w):
        bits = plsc.bitcast(w, I32)
        rounded = (bits + 0x7FFF + (lax.shift_right_logical(bits, jnp.int32(16)) & 1)) & HI_MASK
        return rounded | lax.shift_right_logical(rounded, jnp.int32(16))

    n_items = SC_TOKENS * SC_NGROUPS

    @pl.loop(0, per_worker // SC_TOKENS)
    def _(blk):
        tok0 = pl.multiple_of(wid * per_worker + blk * SC_TOKENS, SC_TOKENS)
        pltpu.sync_copy(h2_hbm.at[pl.ds(tok0, SC_TOKENS)], xbuf)
        pltpu.sync_copy(eidx_hbm.at[pl.ds(tok0, SC_TOKENS)], ibuf)
        pltpu.sync_copy(gate_hbm.at[pl.ds(tok0, SC_TOKENS)], gbuf)

        @pl.loop(0, SC_TOKENS)
        def _(t):
            zero = jnp.zeros((SC_LANES,), F32)
            for j in range(D_MODEL // SC_LANES):
                obuf[t, pl.ds(j * SC_LANES, SC_LANES)] = zero

        for ahead in range(SC_SLOTS - 1):
            for c in gather_copies(ahead, ahead):
                c.start()

        @pl.loop(0, n_items)
        def _(item):
            t, g = split_item(item)
            slot = item & (SC_SLOTS - 1)
            ahead = item + (SC_SLOTS - 1)

            @pl.when(ahead < n_items)
            def _():
                for c in gather_copies(ahead, ahead & (SC_SLOTS - 1)):
                    c.start()

            for c in gather_copies(item, slot):
                c.wait()
            a = dots(t, slot)
            wbuf[...] = pack_weights(gbuf[t, pl.ds(g * SC_GROUP, SC_GROUP)] * _sc_gelu(a))
            accumulate(t, slot)

        pltpu.sync_copy(obuf, out_hbm.at[pl.ds(tok0, SC_TOKENS)])


def _peer_experts(h2p, eidx, gates, u_pack, v_pack):
    n = h2p.shape[0]
    mesh = plsc.VectorSubcoreMesh(core_axis_name="c", subcore_axis_name="s")
    fn = pl.kernel(
        _peer_sc_kernel,
        out_type=jax.ShapeDtypeStruct((n, D_MODEL), F32),
        mesh=mesh,
        scratch_types=[
            pltpu.VMEM((SC_TOKENS, ROW_WORDS), I32),
            pltpu.VMEM((SC_TOKENS, N_SEL), I32),
            pltpu.VMEM((SC_TOKENS, N_SEL), F32),
            pltpu.VMEM((SC_TOKENS, D_MODEL), F32),
            pltpu.VMEM((SC_SLOTS, SC_GROUP, ROW_WORDS), I32),
            pltpu.VMEM((SC_SLOTS, SC_GROUP, ROW_WORDS), I32),
            pltpu.VMEM((SC_GROUP, SC_LANES), F32),
            pltpu.VMEM((SC_LANES,), I32),
            pltpu.SemaphoreType.DMA((SC_SLOTS,)), pltpu.SemaphoreType.DMA((SC_SLOTS,)),
        ],
        compiler_params=pltpu.CompilerParams(needs_layout_passes=False),
        cost_estimate=pl.CostEstimate(
            flops=4 * n * N_SEL * D_MODEL, transcendentals=n * N_SEL,
            bytes_accessed=4 * (2 * n * N_SEL * ROW_WORDS + n * ROW_WORDS + n * D_MODEL + 2 * n * N_SEL)),
        name="peer_experts",
    )
    return fn(h2p, eidx, gates, u_pack, v_pack)


def _resid_kernel(x1_ref, p_ref, mod_ref, o_ref):
    o_ref[...] = x1_ref[...] + mod_ref[0, 5:6, :] * p_ref[...]


def _resid_call(x1, peer_out, seq, mod3, mod_row0, tile0):
    n = x1.shape[0]
    tm = TOKEN_TILE
    per_seq = seq // tm
    tok = pl.BlockSpec((tm, D_MODEL), lambda i: (i, 0))
    return pl.pallas_call(
        _resid_kernel, grid=(n // tm,),
        in_specs=[tok, tok,
                  pl.BlockSpec((1, 6, D_MODEL),
                               lambda i: (mod_row0 + ((tile0 + i) // per_seq if mod_row0 else 0), 0, 0))],
        out_specs=tok, out_shape=jax.ShapeDtypeStruct((n, D_MODEL), F32),
        compiler_params=pltpu.CompilerParams(dimension_semantics=("parallel",)),
        name="resid",
    )(x1, peer_out, mod3)


def _rope_tables(seq, rotate):
    if not rotate:
        return jnp.ones((seq, KV_WIDTH), F32), jnp.zeros((seq, KV_WIDTH), F32)
    quarter = A_DH // 4
    t = jnp.arange(seq)
    row = (t // GRID_W).astype(F32)
    col = (t % GRID_W).astype(F32)
    inv = ROPE_BASE ** (-jnp.arange(quarter, dtype=F32) / quarter)
    d = jnp.arange(A_DH)
    pos = jnp.where(d[None, :] < A_DH // 2, row[:, None], col[:, None])
    ang = pos * inv[d % quarter][None, :]
    sign = jnp.where((d % (A_DH // 2)) < quarter, -1.0, 1.0).astype(F32)
    cos = jnp.cos(ang)
    sin = jnp.sin(ang) * sign[None, :]
    return jnp.tile(cos, (1, KV_WIDTH // A_DH)), jnp.tile(sin, (1, KV_WIDTH // A_DH))


def _layer_mixers(x, mod3, mod_row0, prm, cache, rotate):
    (n1, _, w_main, w_g, w_gt, b_g, b_gt, _, qg_t, kg_t, bd, sink) = prm[:12]
    batch, seq, _ = x.shape
    n = batch * seq
    x2d = x.reshape(n, D_MODEL)
    cos, sin = _rope_tables(seq, rotate)
    mq, mk, mv, mo, gcol, grow, aq, ak, av = _inproj_call(
        x2d, seq, mod3, mod_row0, n1, w_main, w_g, w_gt, b_g, b_gt, qg_t, kg_t, bd, cos, sin)
    kc, vc, c0, m0 = cache
    h_f, h_b, c_fin, m_fin = _mlstm_call(mq, mk, mv, gcol, grow, c0, m0, batch, seq)
    if kc is None:
        a_out = _attn_ctx_call(sink, aq, ak, av, batch, seq)
    else:
        a_out = _attn_lat_call(sink, aq, ak, av, kc, vc, batch, seq)
    return (x2d, h_f, h_b, mo, a_out), ak, av, c_fin, m_fin


def _layer_tail(mixed, seq, mod3, mod_row0, prm, tile0, n_tiles, next_input=None, gate_on=None):
    (_, n2, _, _, _, _, _, mhg, _, _, _, _, w_m, w_a, w_q, sub_a, sub_b, u_tab, v_tab) = prm
    x2d, h_f, h_b, mo, a_out = mixed
    if gate_on is not None:
        x2d, _ = lax.optimization_barrier((x2d, gate_on))
    routed = _mix_call(x2d, seq, h_f, h_b, mo, a_out, mod3, mod_row0, mhg, n2, w_m, w_a, w_q,
                       sub_a, sub_b, tile0, n_tiles)
    if next_input is not None:
        routed, next_input = lax.optimization_barrier((routed, next_input))
    x1, h2, eidx, gates = routed
    peer_out = _peer_experts(h2, eidx, gates, u_tab, v_tab)
    return _resid_call(x1, peer_out, seq, mod3, mod_row0, tile0), peer_out, next_input


def _pack_state(C, n_vec, m):
    b = C.shape[0]
    caug = jnp.concatenate([C, jnp.broadcast_to(n_vec[..., None], C.shape)], axis=-1)
    caug = caug.reshape(b, 2 * M_HEADS, M_DH, 2 * M_DH)
    m_rep = jnp.broadcast_to(m.reshape(b, 2 * M_HEADS, 1, 1), (b, 2 * M_HEADS, 8, M_DH))
    return caug.astype(F32), m_rep.astype(F32)


def kernel(x_prompt, x_sample, c, cache_attn_k, cache_attn_v, state_mlstm_C, state_mlstm_n, state_mlstm_m,
           c_ctx, w_ada, b_ada, norm1_g, norm2_g, w_in, b_gates, mh_norm_g, q_norm_g, k_norm_g, sink_logits,
           w_out, peer_w_q, peer_sub_a, peer_sub_b, peer_u, peer_v):
    depth = w_ada.shape[0]
    assert depth == 1
    batch, seq, _ = x_prompt.shape
    dec_batch, dec_seq, _ = x_sample.shape
    l = 0

    cond = jnp.concatenate([c_ctx[None, :], c, jnp.zeros((MOD_ROWS - 1 - dec_batch, D_MODEL), F32)], axis=0)
    mod3 = _ada_call(cond, w_ada[l], b_ada[l]).reshape(MOD_ROWS, 6, D_MODEL)

    wi = w_in[l]
    g0 = 4 * M_WIDTH
    w_main = jnp.concatenate([wi[:, :g0], wi[:, g0 + N_GATES:]], axis=1).astype(BF16)
    w_g = wi[:, g0:g0 + N_GATES]
    seg = jnp.arange(A_WIDTH) // A_DH
    bd = jnp.where(seg[:, None] == seg[None, :], 1.0 / A_DH, 0.0).astype(F32)
    prm = (norm1_g[l][None, :], norm2_g[l][None, :], w_main, w_g, w_g.T, b_gates[l][None, :], b_gates[l][:, None],
           mh_norm_g[l][None, :], jnp.tile(q_norm_g[l], A_HEADS)[None, :], jnp.tile(k_norm_g[l], A_KV)[None, :], bd,
           sink_logits[l], w_out[l][:M_WIDTH].astype(BF16), w_out[l][M_WIDTH:].astype(BF16),
           peer_w_q[l].astype(BF16), peer_sub_a[l].astype(BF16), peer_sub_b[l].astype(BF16),
           _pack_bf16_pairs(peer_u[l]), _pack_bf16_pairs(peer_v[l]))

    zeros_c = jnp.zeros((batch, 2, M_HEADS, M_DH, M_DH), F32)
    c0, m0 = _pack_state(zeros_c, zeros_c[..., 0], jnp.full((batch, 2, M_HEADS), NEG, F32))
    mixed_p, k_new, v_new, c_fin, m_fin = _layer_mixers(x_prompt, mod3, 0, prm, (None, None, c0, m0), False)
    y_p, peer_p, x_sample = _layer_tail(mixed_p, seq, mod3, 0, prm, 0, batch * seq // TOKEN_TILE,
                                        next_input=x_sample)
    y_p = y_p.reshape(batch, seq, D_MODEL)

    c0s, m0s = _pack_state(state_mlstm_C[:, l], state_mlstm_n[:, l], state_mlstm_m[:, l])
    past = cache_attn_k.shape[2]
    kc = cache_attn_k[:, l].reshape(dec_batch, past, KV_WIDTH)
    vc = cache_attn_v[:, l].reshape(dec_batch, past, KV_WIDTH)
    mixed_s, _, _, _, _ = _layer_mixers(x_sample, mod3, 1, prm, (kc, vc, c0s, m0s), True)
    chunk = dec_batch * dec_seq // TOKEN_TILE // LATENT_CHUNKS
    ys = []
    peers = [None, peer_p]
    for ci in range(LATENT_CHUNKS):
        y_c, peer_c, _ = _layer_tail(mixed_s, dec_seq, mod3, 1, prm, ci * chunk, chunk, gate_on=peers[ci])
        peers.append(peer_c)
        ys.append(y_c)
    y_s = jnp.concatenate(ys, axis=0).reshape(dec_batch, dec_seq, D_MODEL)

    c_fin = c_fin.reshape(batch, 2, M_HEADS, M_DH, 2 * M_DH)
    new_c = c_fin[..., :M_DH][:, None]
    new_n = c_fin[..., M_DH][:, None]
    new_m = m_fin[:, :, 0, 0].reshape(batch, 2, M_HEADS)[:, None]
    new_k = k_new.reshape(batch, 1, seq, A_KV, A_DH)
    new_v = v_new.reshape(batch, 1, seq, A_KV, A_DH)
    return y_p, y_s, new_k, new_v, new_c, new_n, new_m
```

```python
import functools

import jax
import jax.numpy as jnp
from jax import lax
from jax.experimental import pallas as pl
from jax.experimental.pallas import tpu as pltpu
from jax.experimental.pallas import tpu_sc as plsc

F32 = jnp.float32
BF16 = jnp.bfloat16
I32 = jnp.int32
HI = lax.Precision.HIGHEST

D_MODEL = 1024
EPS = 1e-6
NEG = -1e30
GRID_W = 64
M_HEADS = 4
M_WIDTH = 512
M_DH = 128
A_HEADS = 8
A_KV = 2
A_REP = 4
A_DH = 64
A_WIDTH = 512
KV_WIDTH = A_KV * A_DH
BLOCK = 128
ROPE_BASE = 10000.0
N_KEYS = 128
P_HEADS = 8
P_DKEY = 256
P_HALF = 128
P_TOPK = 16
N_SEL = P_HEADS * P_TOPK
N_GATES = 4 * M_HEADS
MAIN_COLS = 4 * M_WIDTH + A_WIDTH + 2 * KV_WIDTH
MOD_ROWS = 16

TOKEN_TILE = 256
MLSTM_CHUNK = 128
ADA_COL_TILE = 768
CTX_CHUNKS = 2
LATENT_CHUNKS = 4


def _sigmoid(x):
    return 1.0 / (1.0 + jnp.exp(-x))


def _log_sigmoid(x):
    return jnp.minimum(x, 0.0) - jnp.log1p(jnp.exp(-jnp.abs(x)))


def _dot_t(a, b, precision=None):
    return lax.dot_general(a, b, (((1,), (1,)), ((), ())), precision=precision,
                           preferred_element_type=F32)


def _ada_kernel(c_ref, w_ref, b_ref, o_ref):
    c = c_ref[...]
    s = c * _sigmoid(c)
    o_ref[...] = jnp.dot(s, w_ref[...], precision=HI, preferred_element_type=F32) + b_ref[...]


def _ada_call(cond, w_ada, b_ada):
    n_out = w_ada.shape[1]
    return pl.pallas_call(
        _ada_kernel,
        grid=(n_out // ADA_COL_TILE,),
        in_specs=[pl.BlockSpec((MOD_ROWS, D_MODEL), lambda j: (0, 0)),
                  pl.BlockSpec((D_MODEL, ADA_COL_TILE), lambda j: (0, j)),
                  pl.BlockSpec((1, ADA_COL_TILE), lambda j: (0, j))],
        out_specs=pl.BlockSpec((MOD_ROWS, ADA_COL_TILE), lambda j: (0, j)),
        out_shape=jax.ShapeDtypeStruct((MOD_ROWS, n_out), F32),
        name="ada",
    )(cond, w_ada, b_ada.reshape(1, n_out))


def _swap16(x):
    n = x.shape[-1]
    lane = lax.broadcasted_iota(I32, x.shape, x.ndim - 1)
    return jnp.where((lane & 16) == 0, pltpu.roll(x, n - 16, x.ndim - 1), pltpu.roll(x, 16, x.ndim - 1))


def _inproj_kernel(x_ref, mod_ref, n1_ref, w_ref, wg_ref, wgt_ref, bg_ref, bgt_ref, qg_ref, kg_ref,
                   bd_ref, cos_ref, sin_ref,
                   mq_ref, mk_ref, mv_ref, mo_ref, gc_ref, gr_ref, aq_ref, ak_ref, av_ref):
    x = x_ref[...]
    h = x * lax.rsqrt(jnp.mean(x * x, axis=-1, keepdims=True) + EPS) * n1_ref[...]
    h = h * (1.0 + mod_ref[0, 1:2, :]) + mod_ref[0, 0:1, :]
    z = jnp.dot(h.astype(BF16), w_ref[...], preferred_element_type=F32)

    mq_ref[...] = (z[:, 0:M_WIDTH] * (M_DH ** -0.5)).astype(BF16)
    mk_ref[...] = z[:, M_WIDTH:2 * M_WIDTH].astype(BF16)
    mv_ref[...] = z[:, 2 * M_WIDTH:3 * M_WIDTH].astype(BF16)
    mo_ref[...] = z[:, 3 * M_WIDTH:4 * M_WIDTH]

    g = jnp.dot(h, wg_ref[...], precision=HI, preferred_element_type=F32) + bg_ref[...]
    kind = lax.broadcasted_iota(I32, g.shape, 1) // M_HEADS
    gc_ref[...] = jnp.where((kind & 1) == 1, _log_sigmoid(g), g)
    gt = _dot_t(wgt_ref[...], h, precision=HI) + bgt_ref[...]
    kind_t = lax.broadcasted_iota(I32, gt.shape, 0) // M_HEADS
    gr_ref[...] = jnp.where((kind_t & 1) == 1, _log_sigmoid(gt), gt)

    o = 4 * M_WIDTH
    aq = z[:, o:o + A_WIDTH]
    ak = z[:, o + A_WIDTH:o + A_WIDTH + KV_WIDTH]
    av_ref[...] = z[:, o + A_WIDTH + KV_WIDTH:o + A_WIDTH + 2 * KV_WIDTH]
    bd = bd_ref[...]
    cos = cos_ref[...]
    sin = sin_ref[...]
    aq = aq * lax.rsqrt(jnp.dot(aq * aq, bd, precision=HI, preferred_element_type=F32) + EPS) * qg_ref[...]
    cos4 = jnp.concatenate([cos] * (A_WIDTH // KV_WIDTH), axis=1)
    sin4 = jnp.concatenate([sin] * (A_WIDTH // KV_WIDTH), axis=1)
    aq = (aq * cos4 + _swap16(aq) * sin4) * (A_DH ** -0.5)
    ak = ak * lax.rsqrt(jnp.dot(ak * ak, bd[0:KV_WIDTH, 0:KV_WIDTH], precision=HI,
                                preferred_element_type=F32) + EPS) * kg_ref[...]
    ak_ref[...] = ak * cos + _swap16(ak) * sin

    lane = lax.broadcasted_iota(I32, (aq.shape[0], KV_WIDTH), 1)
    for hd in range(A_HEADS):
        grp = hd // A_REP
        blk = aq[:, (hd // 2) * KV_WIDTH:(hd // 2 + 1) * KV_WIDTH]
        if hd % 2 != grp:
            blk = pltpu.roll(blk, A_DH, 1)
        keep = (lane >= grp * A_DH) & (lane < (grp + 1) * A_DH)
        aq_ref[hd] = jnp.where(keep, blk, 0.0).astype(BF16)


def _inproj_call(x2d, seq, mod3, mod_row0, n1, w_main, w_g, w_gt, b_g, b_gt, qg_t, kg_t, bd, cos, sin):
    n = x2d.shape[0]
    tm = TOKEN_TILE
    per_seq = seq // tm

    def tok(i):
        return (i, 0)

    def const2(i):
        return (0, 0)

    in_specs = [
        pl.BlockSpec((tm, D_MODEL), tok),
        pl.BlockSpec((1, 6, D_MODEL), lambda i: (mod_row0 + (i // per_seq if mod_row0 else 0), 0, 0)),
        pl.BlockSpec((1, D_MODEL), const2),
        pl.BlockSpec((D_MODEL, MAIN_COLS), const2),
        pl.BlockSpec((D_MODEL, N_GATES), const2),
        pl.BlockSpec((N_GATES, D_MODEL), const2),
        pl.BlockSpec((1, N_GATES), const2),
        pl.BlockSpec((N_GATES, 1), const2),
        pl.BlockSpec((1, A_WIDTH), const2),
        pl.BlockSpec((1, KV_WIDTH), const2),
        pl.BlockSpec((A_WIDTH, A_WIDTH), const2),
        pl.BlockSpec((tm, KV_WIDTH), lambda i: (i % per_seq, 0)),
        pl.BlockSpec((tm, KV_WIDTH), lambda i: (i % per_seq, 0)),
    ]
    out_specs = [
        pl.BlockSpec((tm, M_WIDTH), tok),
        pl.BlockSpec((tm, M_WIDTH), tok),
        pl.BlockSpec((tm, M_WIDTH), tok),
        pl.BlockSpec((tm, M_WIDTH), tok),
        pl.BlockSpec((tm, N_GATES), tok),
        pl.BlockSpec((N_GATES, tm), lambda i: (0, i)),
        pl.BlockSpec((A_HEADS, tm, KV_WIDTH), lambda i: (0, i, 0)),
        pl.BlockSpec((tm, KV_WIDTH), tok),
        pl.BlockSpec((tm, KV_WIDTH), tok),
    ]
    out_shape = [
        jax.ShapeDtypeStruct((n, M_WIDTH), BF16),
        jax.ShapeDtypeStruct((n, M_WIDTH), BF16),
        jax.ShapeDtypeStruct((n, M_WIDTH), BF16),
        jax.ShapeDtypeStruct((n, M_WIDTH), F32),
        jax.ShapeDtypeStruct((n, N_GATES), F32),
        jax.ShapeDtypeStruct((N_GATES, n), F32),
        jax.ShapeDtypeStruct((A_HEADS, n, KV_WIDTH), BF16),
        jax.ShapeDtypeStruct((n, KV_WIDTH), F32),
        jax.ShapeDtypeStruct((n, KV_WIDTH), F32),
    ]
    return pl.pallas_call(
        _inproj_kernel, grid=(n // tm,), in_specs=in_specs, out_specs=out_specs, out_shape=out_shape,
        compiler_params=pltpu.CompilerParams(dimension_semantics=("parallel",)),
        name="inproj",
    )(x2d, mod3, n1, w_main, w_g, w_gt, b_g, b_gt, qg_t, kg_t, bd, cos, sin)


def _mlstm_chain(q, k, v, li_c, lf_c, li_r, lf_r, caug, m, tri, tri_t, mask, reverse):
    L = q.shape[0]
    last = 0 if reverse else L - 1
    b_c = jnp.dot(tri, jnp.broadcast_to(lf_c, (L, L)), precision=HI, preferred_element_type=F32)
    b_r = jnp.dot(jnp.broadcast_to(lf_r, (8, L)), tri_t, precision=HI, preferred_element_type=F32)[0:1, :]
    a_inter = b_c[:, 0:1] + m
    d = jnp.where(mask, b_c - b_r + li_r, -jnp.inf)
    m_t = jnp.maximum(a_inter, jnp.max(d, axis=1, keepdims=True))
    w_inter = jnp.exp(a_inter - m_t)
    s = _dot_t(q, k) * jnp.exp(d - m_t)
    qc = jnp.dot(q, caug.astype(BF16), preferred_element_type=F32)
    num = jnp.dot(s.astype(BF16), v, preferred_element_type=F32) + w_inter * qc[:, 0:M_DH]
    den = jnp.sum(s, axis=1, keepdims=True) + w_inter * qc[:, M_DH:M_DH + 1]
    den = jnp.maximum(jnp.abs(den), jnp.exp(-m_t))
    h = num / den
    m_new = m_t[last:last + 1, :]
    b_last = b_c[last:last + 1, 0:1]
    g_c = jnp.exp(b_last - b_c[:, 0:1] + li_c - m_new)
    decay = jnp.exp(b_last + m - m_new)
    kw = (k.astype(F32) * g_c).astype(BF16)
    vaug = jnp.concatenate([v, jnp.ones_like(v)], axis=1)
    upd = lax.dot_general(kw, vaug, (((0,), (0,)), ((), ())), preferred_element_type=F32)
    return h, decay * caug + upd, m_new


def _mlstm_kernel(qf_ref, kf_ref, vf_ref, gcf_ref, grf_ref, qb_ref, kb_ref, vb_ref, gcb_ref, grb_ref,
                  c0_ref, m0_ref, hf_ref, hb_ref, cfin_ref, mfin_ref, c_scr, m_scr):
    c = pl.program_id(1)
    nc = pl.num_programs(1)
    L = qf_ref.shape[0]

    @pl.when(c == 0)
    def _():
        c_scr[...] = c0_ref[0]
        m_scr[...] = m0_ref[0]

    row = lax.broadcasted_iota(I32, (L, L), 0)
    col = lax.broadcasted_iota(I32, (L, L), 1)
    lower = row >= col
    upper = row <= col
    lower_f = lower.astype(F32)
    upper_f = upper.astype(F32)

    for direction in range(2):
        reverse = direction == 1
        q_ref, k_ref, v_ref, gc_ref, gr_ref, h_ref = (
            (qb_ref, kb_ref, vb_ref, gcb_ref, grb_ref, hb_ref) if reverse
            else (qf_ref, kf_ref, vf_ref, gcf_ref, grf_ref, hf_ref))
        tri, tri_t, mask = (upper_f, lower_f, upper) if reverse else (lower_f, upper_f, lower)
        gc = gc_ref[...]
        gr = gr_ref[...]
        for hd in range(M_HEADS):
            ch = direction * M_HEADS + hd
            sl = slice(hd * M_DH, (hd + 1) * M_DH)
            ci = 2 * direction * M_HEADS + hd
            cf = ci + M_HEADS
            h, caug, m_new = _mlstm_chain(
                q_ref[:, sl], k_ref[:, sl], v_ref[:, sl],
                gc[:, ci:ci + 1], gc[:, cf:cf + 1], gr[ci:ci + 1, :], gr[cf:cf + 1, :],
                c_scr[ch], m_scr[ch][0:1, 0:1], tri, tri_t, mask, reverse)
            h_ref[:, sl] = h
            c_scr[ch] = caug
            m_scr[ch] = jnp.broadcast_to(m_new, m_scr.shape[1:])

    @pl.when(c == nc - 1)
    def _():
        cfin_ref[0] = c_scr[...]
        mfin_ref[0] = m_scr[...]


def _mlstm_call(mq, mk, mv, gcol, grow, c0, m0, batch, seq):
    n = mq.shape[0]
    L = MLSTM_CHUNK
    nc = seq // L
    n_ch = 2 * M_HEADS

    def fwd(b, c):
        return (b * nc + c, 0)

    def bwd(b, c):
        return (b * nc + nc - 1 - c, 0)

    def fwd_t(b, c):
        return (0, b * nc + c)

    def bwd_t(b, c):
        return (0, b * nc + nc - 1 - c)

    tok = pl.BlockSpec((L, M_WIDTH), fwd)
    tok_b = pl.BlockSpec((L, M_WIDTH), bwd)
    in_specs = [tok, tok, tok, pl.BlockSpec((L, N_GATES), fwd), pl.BlockSpec((N_GATES, L), fwd_t),
                tok_b, tok_b, tok_b, pl.BlockSpec((L, N_GATES), bwd), pl.BlockSpec((N_GATES, L), bwd_t),
                pl.BlockSpec((1, n_ch, M_DH, 2 * M_DH), lambda b, c: (b, 0, 0, 0)),
                pl.BlockSpec((1, n_ch, 8, M_DH), lambda b, c: (b, 0, 0, 0))]
    out_specs = [tok, tok_b,
                 pl.BlockSpec((1, n_ch, M_DH, 2 * M_DH), lambda b, c: (b, 0, 0, 0)),
                 pl.BlockSpec((1, n_ch, 8, M_DH), lambda b, c: (b, 0, 0, 0))]
    out_shape = [jax.ShapeDtypeStruct((n, M_WIDTH), F32), jax.ShapeDtypeStruct((n, M_WIDTH), F32),
                 jax.ShapeDtypeStruct((batch, n_ch, M_DH, 2 * M_DH), F32),
                 jax.ShapeDtypeStruct((batch, n_ch, 8, M_DH), F32)]
    return pl.pallas_call(
        _mlstm_kernel, grid=(batch, nc), in_specs=in_specs, out_specs=out_specs, out_shape=out_shape,
        scratch_shapes=[pltpu.VMEM((n_ch, M_DH, 2 * M_DH), F32), pltpu.VMEM((n_ch, 8, M_DH), F32)],
        compiler_params=pltpu.CompilerParams(dimension_semantics=("parallel", "arbitrary")),
        name="mlstm",
    )(mq, mk, mv, gcol, grow, mq, mk, mv, gcol, grow, c0, m0)


def _sink_column(sink_ref, grp, rows_per_head):
    return jnp.concatenate(
        [jnp.full((rows_per_head, 1), sink_ref[grp * A_REP + r], F32) for r in range(A_REP)], axis=0)


def _store_heads(out_ref, o, grp, rows_per_head):
    for r in range(A_REP):
        hd = grp * A_REP + r
        out_ref[:, hd * A_DH:(hd + 1) * A_DH] = o[r * rows_per_head:(r + 1) * rows_per_head,
                                                  grp * A_DH:(grp + 1) * A_DH].astype(out_ref.dtype)


def _attn_ctx_kernel(sink_ref, q_ref, k_ref, v_ref, out_ref):
    s_len = k_ref.shape[0]
    k = k_ref[...].astype(BF16)
    v = v_ref[...].astype(BF16)
    for grp in range(A_KV):
        q = q_ref[grp * A_REP:(grp + 1) * A_REP].reshape(A_REP * s_len, KV_WIDTH)
        s = _dot_t(q, k)
        sk = _sink_column(sink_ref, grp, s_len)
        mx = jnp.maximum(jnp.max(s, axis=1, keepdims=True), sk)
        p = jnp.exp(s - mx)
        den = jnp.sum(p, axis=1, keepdims=True) + jnp.exp(sk - mx)
        o = jnp.dot(p.astype(BF16), v, preferred_element_type=F32) / den
        _store_heads(out_ref, o, grp, s_len)


def _attn_ctx_call(sink, aq, ak, av, batch, seq):
    n = ak.shape[0]
    return pl.pallas_call(
        _attn_ctx_kernel, grid=(batch,),
        in_specs=[pl.BlockSpec(memory_space=pltpu.SMEM),
                  pl.BlockSpec((A_HEADS, seq, KV_WIDTH), lambda b: (0, b, 0)),
                  pl.BlockSpec((seq, KV_WIDTH), lambda b: (b, 0)),
                  pl.BlockSpec((seq, KV_WIDTH), lambda b: (b, 0))],
        out_specs=pl.BlockSpec((seq, A_WIDTH), lambda b: (b, 0)),
        out_shape=jax.ShapeDtypeStruct((n, A_WIDTH), BF16),
        compiler_params=pltpu.CompilerParams(dimension_semantics=("parallel",)),
        name="attn_ctx",
    )(sink, aq, ak, av)


def _attn_lat_kernel(sink_ref, q_ref, kc_ref, vc_ref, kp_ref, kq_ref, kn_ref, vp_ref, vq_ref, vn_ref, out_ref):
    i = pl.program_id(1)
    nb = pl.num_programs(1)
    kc = kc_ref[0].astype(BF16)
    vc = vc_ref[0].astype(BF16)
    kp, kq, kn = kp_ref[...].astype(BF16), kq_ref[...].astype(BF16), kn_ref[...].astype(BF16)
    vp, vq, vn = vp_ref[...].astype(BF16), vq_ref[...].astype(BF16), vn_ref[...].astype(BF16)
    rows = A_REP * BLOCK
    qpos = lax.broadcasted_iota(I32, (rows, BLOCK), 0) % BLOCK
    kpos = lax.broadcasted_iota(I32, (rows, BLOCK), 1)
    mask_p = (kpos >= qpos) & (i > 0)
    mask_n = (kpos <= qpos) & (i < nb - 1)
    for grp in range(A_KV):
        q = q_ref[grp * A_REP:(grp + 1) * A_REP].reshape(rows, KV_WIDTH)
        s_c = _dot_t(q, kc)
        s_p = jnp.where(mask_p, _dot_t(q, kp), NEG)
        s_q = _dot_t(q, kq)
        s_n = jnp.where(mask_n, _dot_t(q, kn), NEG)
        sk = _sink_column(sink_ref, grp, BLOCK)
        mx = jnp.maximum(jnp.maximum(jnp.max(s_c, axis=1, keepdims=True), jnp.max(s_p, axis=1, keepdims=True)),
                         jnp.maximum(jnp.max(s_q, axis=1, keepdims=True), jnp.max(s_n, axis=1, keepdims=True)))
        mx = jnp.maximum(mx, sk)
        p_c, p_p, p_q, p_n = jnp.exp(s_c - mx), jnp.exp(s_p - mx), jnp.exp(s_q - mx), jnp.exp(s_n - mx)
        den = (jnp.sum(p_c, axis=1, keepdims=True) + jnp.sum(p_p, axis=1, keepdims=True)
               + jnp.sum(p_q, axis=1, keepdims=True) + jnp.sum(p_n, axis=1, keepdims=True) + jnp.exp(sk - mx))
        o = (jnp.dot(p_c.astype(BF16), vc, preferred_element_type=F32)
             + jnp.dot(p_p.astype(BF16), vp, preferred_element_type=F32)
             + jnp.dot(p_q.astype(BF16), vq, preferred_element_type=F32)
             + jnp.dot(p_n.astype(BF16), vn, preferred_element_type=F32)) / den
        _store_heads(out_ref, o, grp, BLOCK)


def _attn_lat_call(sink, aq, ak, av, kc, vc, batch, seq):
    n = ak.shape[0]
    nb = seq // BLOCK
    past = kc.shape[1]

    def cur(b, i):
        return (b * nb + i, 0)

    def prev(b, i):
        return (b * nb + jnp.maximum(i - 1, 0), 0)

    def nxt(b, i):
        return (b * nb + jnp.minimum(i + 1, nb - 1), 0)

    blk = functools.partial(pl.BlockSpec, (BLOCK, KV_WIDTH))
    cache = pl.BlockSpec((1, past, KV_WIDTH), lambda b, i: (b, 0, 0))
    return pl.pallas_call(
        _attn_lat_kernel, grid=(batch, nb),
        in_specs=[pl.BlockSpec(memory_space=pltpu.SMEM),
                  pl.BlockSpec((A_HEADS, BLOCK, KV_WIDTH), lambda b, i: (0, b * nb + i, 0)),
                  cache, cache, blk(prev), blk(cur), blk(nxt), blk(prev), blk(cur), blk(nxt)],
        out_specs=pl.BlockSpec((BLOCK, A_WIDTH), cur),
        out_shape=jax.ShapeDtypeStruct((n, A_WIDTH), BF16),
        compiler_params=pltpu.CompilerParams(dimension_semantics=("parallel", "parallel")),
        name="attn_lat",
    )(sink, aq, kc, vc, ak, ak, ak, av, av, av)


def _top16_rows(s, payload=None):
    n_rows = s.shape[0]
    rows = lax.broadcasted_iota(I32, s.shape, 0).astype(F32)
    vals, idxs, pays = [], [], []
    for _ in range(P_TOPK):
        mx = jnp.max(s, axis=0, keepdims=True)
        ix = jnp.min(jnp.where(s == mx, rows, float(n_rows)), axis=0, keepdims=True)
        hit = rows == ix
        vals.append(mx)
        idxs.append(ix)
        if payload is not None:
            pays.append(jnp.sum(jnp.where(hit, payload, 0.0), axis=0, keepdims=True))
        s = jnp.where(hit, -jnp.inf, s)
    out = (jnp.concatenate(vals, axis=0), jnp.concatenate(idxs, axis=0))
    if payload is not None:
        out += (jnp.concatenate(pays, axis=0),)
    return out


def _mix_kernel(x_ref, hf_ref, hb_ref, mo_ref, ao_ref, mod_ref, mhg_ref, n2_ref, wm_ref, wa_ref, wq_ref,
                sa_ref, sb_ref, x1_ref, h2_ref, eidx_ref, gate_ref, qp_scr, e_scr, g_scr):
    tm = x_ref.shape[0]
    hs = hf_ref[...] + hb_ref[...]
    parts = []
    for hd in range(M_HEADS):
        blk = hs[:, hd * M_DH:(hd + 1) * M_DH]
        parts.append(blk * lax.rsqrt(jnp.mean(blk * blk, axis=-1, keepdims=True) + EPS))
    m_out = _sigmoid(mo_ref[...]) * (jnp.concatenate(parts, axis=1) * mhg_ref[...])
    mix = (jnp.dot(m_out.astype(BF16), wm_ref[...], preferred_element_type=F32)
           + jnp.dot(ao_ref[...], wa_ref[...], preferred_element_type=F32))
    x1 = x_ref[...] + mod_ref[0, 2:3, :] * mix
    x1_ref[...] = x1
    h2 = x1 * lax.rsqrt(jnp.mean(x1 * x1, axis=-1, keepdims=True) + EPS) * n2_ref[...]
    h2 = h2 * (1.0 + mod_ref[0, 4:5, :]) + mod_ref[0, 3:4, :]
    h2_ref[...] = _pack_bf16_pairs(h2)
    qp = jnp.dot(h2.astype(BF16), wq_ref[...], preferred_element_type=F32)
    for p in range(P_HEADS):
        qp_scr[p] = qp[:, p * P_DKEY:(p + 1) * P_DKEY].astype(BF16)
    sub_a = sa_ref[...]
    sub_b = sb_ref[...]

    def head_body(p, carry):
        for half in range(tm // N_KEYS):
            cols = slice(half * N_KEYS, (half + 1) * N_KEYS)
            qh = qp_scr[p, pl.ds(half * N_KEYS, N_KEYS), :]
            s_a = _dot_t(sub_a, qh[:, 0:P_HALF])
            s_b = _dot_t(sub_b, qh[:, P_HALF:P_DKEY])
            va, ia = _top16_rows(s_a)
            vb, ib = _top16_rows(s_b)
            cand = jnp.concatenate([va[i:i + 1, :] + vb for i in range(P_TOPK)], axis=0)
            cidx = jnp.concatenate([ia[i:i + 1, :] * float(N_KEYS) + ib for i in range(P_TOPK)], axis=0)
            top, _, eidx = _top16_rows(cand, cidx)
            ex = jnp.exp(top - jnp.max(top, axis=0, keepdims=True))
            gates = ex / jnp.sum(ex, axis=0, keepdims=True)
            r0 = pl.multiple_of(p * P_TOPK, P_TOPK)
            e_scr[pl.ds(r0, P_TOPK), cols] = eidx
            g_scr[pl.ds(r0, P_TOPK), cols] = gates
        return carry

    lax.fori_loop(0, P_HEADS, head_body, 0)
    for half in range(tm // N_KEYS):
        cols = slice(half * N_KEYS, (half + 1) * N_KEYS)
        eidx_ref[cols, :] = e_scr[:, cols].T.astype(I32)
        gate_ref[cols, :] = g_scr[:, cols].T


def _mix_call(x2d, seq, h_f, h_b, mo, a_out, mod3, mod_row0, mhg, n2, w_m, w_a, w_q, sub_a, sub_b, tile0, n_tiles):
    tm = TOKEN_TILE
    per_seq = seq // tm
    n = n_tiles * tm

    def tok_in(i):
        return (tile0 + i, 0)

    def tok(i):
        return (i, 0)

    def const2(i):
        return (0, 0)

    in_specs = [
        pl.BlockSpec((tm, D_MODEL), tok_in),
        pl.BlockSpec((tm, M_WIDTH), tok_in), pl.BlockSpec((tm, M_WIDTH), tok_in),
        pl.BlockSpec((tm, M_WIDTH), tok_in),
        pl.BlockSpec((tm, A_WIDTH), tok_in),
        pl.BlockSpec((1, 6, D_MODEL), lambda i: (mod_row0 + ((tile0 + i) // per_seq if mod_row0 else 0), 0, 0)),
        pl.BlockSpec((1, M_WIDTH), const2),
        pl.BlockSpec((1, D_MODEL), const2),
        pl.BlockSpec((M_WIDTH, D_MODEL), const2),
        pl.BlockSpec((A_WIDTH, D_MODEL), const2),
        pl.BlockSpec((D_MODEL, P_HEADS * P_DKEY), const2),
        pl.BlockSpec((N_KEYS, P_HALF), const2),
        pl.BlockSpec((N_KEYS, P_HALF), const2),
    ]
    out_specs = [pl.BlockSpec((tm, D_MODEL), tok), pl.BlockSpec((tm, D_MODEL // 2), tok),
                 pl.BlockSpec((tm, N_SEL), tok), pl.BlockSpec((tm, N_SEL), tok)]
    out_shape = [jax.ShapeDtypeStruct((n, D_MODEL), F32), jax.ShapeDtypeStruct((n, D_MODEL // 2), I32),
                 jax.ShapeDtypeStruct((n, N_SEL), I32), jax.ShapeDtypeStruct((n, N_SEL), F32)]
    return pl.pallas_call(
        _mix_kernel, grid=(n_tiles,), in_specs=in_specs, out_specs=out_specs, out_shape=out_shape,
        scratch_shapes=[pltpu.VMEM((P_HEADS, tm, P_DKEY), BF16), pltpu.VMEM((N_SEL, tm), F32),
                        pltpu.VMEM((N_SEL, tm), F32)],
        compiler_params=pltpu.CompilerParams(dimension_semantics=("parallel",)),
        name="mix",
    )(x2d, h_f, h_b, mo, a_out, mod3, mhg, n2, w_m, w_a, w_q, sub_a, sub_b)


SC_LANES = 16
SC_CORES = 2
SC_SUBCORES = 16
SC_WORKERS = SC_CORES * SC_SUBCORES
SC_TOKENS = 8
SC_GROUP = SC_LANES
SC_NGROUPS = N_SEL // SC_GROUP
SC_SLOTS = 4
ROW_WORDS = D_MODEL // 2
SC_DOT_ROWS = 4
SC_DOT_PARTIALS = 2
SC_OWORDS = 16 * SC_LANES
HI_MASK = -65536
GELU_C0 = 0.7978845608028654
GELU_C1 = 0.044715


def _pack_bf16_pairs(x):
    half = x.shape[1] // 2
    bits = lax.bitcast_convert_type(x.astype(BF16).astype(F32), I32)
    return (bits[:, :half] & HI_MASK) | lax.shift_right_logical(bits[:, half:], jnp.int32(16))


def _sc_gelu(a):
    z = GELU_C0 * (a + GELU_C1 * (a * a * a))
    tanh = 1.0 - 2.0 / (jnp.exp(2.0 * z) + 1.0)
    return 0.5 * a * (1.0 + tanh)


def _sc_split(words):
    return (plsc.bitcast(words & HI_MASK, F32), plsc.bitcast(lax.shift_left(words, jnp.int32(16)), F32))


def _sc_mul_bf16(a_words, b_words):
    return plsc.bitcast(a_words, BF16) * plsc.bitcast(b_words, BF16)


def _sc_split_sum(p, q):
    return _sc_split(plsc.bitcast(p + q, I32))


def _peer_sc_kernel(h2_hbm, eidx_hbm, gate_hbm, u_hbm, v_hbm, out_hbm,
                    xbuf, ibuf, gbuf, obuf, ubuf, vbuf, mbuf, wbuf, sems_u, sems_v):
    n = h2_hbm.shape[0]
    per_worker = n // SC_WORKERS
    wid = lax.axis_index("c") * SC_SUBCORES + lax.axis_index("s")
    lane = lax.iota(I32, SC_LANES)

    def split_item(item):
        return lax.shift_right_logical(item, SC_NGROUPS.bit_length() - 1), item & (SC_NGROUPS - 1)

    def gather_copies(item, slot):
        t, g = split_item(item)
        idx = ibuf[t, pl.ds(g * SC_GROUP, SC_GROUP)]
        return (pltpu.make_async_copy(u_hbm.at[idx], ubuf.at[slot], sems_u.at[slot]),
                pltpu.make_async_copy(v_hbm.at[idx], vbuf.at[slot], sems_v.at[slot]))

    def dots(t, slot):
        zero = jnp.zeros((SC_LANES,), F32)

        @pl.loop(0, SC_GROUP, step=SC_DOT_ROWS)
        def _(r0):
            accs = [[zero] * SC_DOT_PARTIALS for _ in range(SC_DOT_ROWS)]
            for k in range(0, ROW_WORDS // SC_LANES, 2):
                xa = xbuf[t, pl.ds(k * SC_LANES, SC_LANES)]
                xb = xbuf[t, pl.ds((k + 1) * SC_LANES, SC_LANES)]
                for i in range(SC_DOT_ROWS):
                    hi, lo = _sc_split_sum(
                        _sc_mul_bf16(xa, ubuf[slot, r0 + i, pl.ds(k * SC_LANES, SC_LANES)]),
                        _sc_mul_bf16(xb, ubuf[slot, r0 + i, pl.ds((k + 1) * SC_LANES, SC_LANES)]))
                    p = (k // 2) % SC_DOT_PARTIALS
                    accs[i][p] = accs[i][p] + (hi + lo)
            for i in range(SC_DOT_ROWS):
                mbuf[r0 + i, :] = functools.reduce(lambda a, b: a + b, accs[i])

        tot = zero
        for c in range(SC_LANES):
            tot = tot + plsc.load_gather(mbuf, [lane, jnp.full((SC_LANES,), c, I32)])
        return tot

    def accumulate(t, slot):
        nv = SC_OWORDS // SC_LANES
        for oc in range(ROW_WORDS // SC_OWORDS):
            w0 = oc * SC_OWORDS
            accs = (tuple(obuf[t, pl.ds(w0 + j * SC_LANES, SC_LANES)] for j in range(nv))
                    + tuple(obuf[t, pl.ds(ROW_WORDS + w0 + j * SC_LANES, SC_LANES)] for j in range(nv)))

            def row_pair(rp, accs):
                r = 2 * rp
                wa = plsc.load_gather(wbuf, [jnp.full((SC_LANES,), r, I32)])
                wb = plsc.load_gather(wbuf, [jnp.full((SC_LANES,), r + 1, I32)])
                his, los = [], []
                for j in range(nv):
                    hi, lo = _sc_split_sum(
                        _sc_mul_bf16(wa, vbuf[slot, r, pl.ds(w0 + j * SC_LANES, SC_LANES)]),
                        _sc_mul_bf16(wb, vbuf[slot, r + 1, pl.ds(w0 + j * SC_LANES, SC_LANES)]))
                    his.append(accs[j] + hi)
                    los.append(accs[nv + j] + lo)
                return tuple(his) + tuple(los)

            accs = lax.fori_loop(0, SC_GROUP // 2, row_pair, accs)
            for j in range(nv):
                obuf[t, pl.ds(w0 + j * SC_LANES, SC_LANES)] = accs[j]
                obuf[t, pl.ds(ROW_WORDS + w0 + j * SC_LANES, SC_LANES)] = accs[nv + j]

    def pack_weights(w):
        bits = plsc.bitcast(w, I32)
        rounded = (bits + 0x7FFF + (lax.shift_right_logical(bits, jnp.int32(16)) & 1)) & HI_MASK
        return rounded | lax.shift_right_logical(rounded, jnp.int32(16))

    n_items = SC_TOKENS * SC_NGROUPS

    @pl.loop(0, per_worker // SC_TOKENS)
    def _(blk):
        tok0 = pl.multiple_of(wid * per_worker + blk * SC_TOKENS, SC_TOKENS)
        pltpu.sync_copy(h2_hbm.at[pl.ds(tok0, SC_TOKENS)], xbuf)
        pltpu.sync_copy(eidx_hbm.at[pl.ds(tok0, SC_TOKENS)], ibuf)
        pltpu.sync_copy(gate_hbm.at[pl.ds(tok0, SC_TOKENS)], gbuf)

        @pl.loop(0, SC_TOKENS)
        def _(t):
            zero = jnp.zeros((SC_LANES,), F32)
            for j in range(D_MODEL // SC_LANES):
                obuf[t, pl.ds(j * SC_LANES, SC_LANES)] = zero

        for ahead in range(SC_SLOTS - 1):
            for c in gather_copies(ahead, ahead):
                c.start()

        @pl.loop(0, n_items)
        def _(item):
            t, g = split_item(item)
            slot = item & (SC_SLOTS - 1)
            ahead = item + (SC_SLOTS - 1)

            @pl.when(ahead < n_items)
            def _():
                for c in gather_copies(ahead, ahead & (SC_SLOTS - 1)):
                    c.start()

            for c in gather_copies(item, slot):
                c.wait()
            a = dots(t, slot)
            wbuf[...] = pack_weights(gbuf[t, pl.ds(g * SC_GROUP, SC_GROUP)] * _sc_gelu(a))
            accumulate(t, slot)

        pltpu.sync_copy(obuf, out_hbm.at[pl.ds(tok0, SC_TOKENS)])


def _peer_experts(h2p, eidx, gates, u_pack, v_pack):
    n = h2p.shape[0]
    mesh = plsc.VectorSubcoreMesh(core_axis_name="c", subcore_axis_name="s")
    fn = pl.kernel(
        _peer_sc_kernel,
        out_type=jax.ShapeDtypeStruct((n, D_MODEL), F32),
        mesh=mesh,
        scratch_types=[
            pltpu.VMEM((SC_TOKENS, ROW_WORDS), I32),
            pltpu.VMEM((SC_TOKENS, N_SEL), I32),
            pltpu.VMEM((SC_TOKENS, N_SEL), F32),
            pltpu.VMEM((SC_TOKENS, D_MODEL), F32),
            pltpu.VMEM((SC_SLOTS, SC_GROUP, ROW_WORDS), I32),
            pltpu.VMEM((SC_SLOTS, SC_GROUP, ROW_WORDS), I32),
            pltpu.VMEM((SC_GROUP, SC_LANES), F32),
            pltpu.VMEM((SC_LANES,), I32),
            pltpu.SemaphoreType.DMA((SC_SLOTS,)), pltpu.SemaphoreType.DMA((SC_SLOTS,)),
        ],
        compiler_params=pltpu.CompilerParams(needs_layout_passes=False),
        cost_estimate=pl.CostEstimate(
            flops=4 * n * N_SEL * D_MODEL, transcendentals=n * N_SEL,
            bytes_accessed=4 * (2 * n * N_SEL * ROW_WORDS + n * ROW_WORDS + n * D_MODEL + 2 * n * N_SEL)),
        name="peer_experts",
    )
    return fn(h2p, eidx, gates, u_pack, v_pack)


def _resid_kernel(x1_ref, p_ref, mod_ref, o_ref):
    o_ref[...] = x1_ref[...] + mod_ref[0, 5:6, :] * p_ref[...]


def _resid_call(x1, peer_out, seq, mod3, mod_row0, tile0):
    n = x1.shape[0]
    tm = TOKEN_TILE
    per_seq = seq // tm
    tok = pl.BlockSpec((tm, D_MODEL), lambda i: (i, 0))
    return pl.pallas_call(
        _resid_kernel, grid=(n // tm,),
        in_specs=[tok, tok,
                  pl.BlockSpec((1, 6, D_MODEL),
                               lambda i: (mod_row0 + ((tile0 + i) // per_seq if mod_row0 else 0), 0, 0))],
        out_specs=tok, out_shape=jax.ShapeDtypeStruct((n, D_MODEL), F32),
        compiler_params=pltpu.CompilerParams(dimension_semantics=("parallel",)),
        name="resid",
    )(x1, peer_out, mod3)


def _rope_tables(seq, rotate):
    if not rotate:
        return jnp.ones((seq, KV_WIDTH), F32), jnp.zeros((seq, KV_WIDTH), F32)
    quarter = A_DH // 4
    t = jnp.arange(seq)
    row = (t // GRID_W).astype(F32)
    col = (t % GRID_W).astype(F32)
    inv = ROPE_BASE ** (-jnp.arange(quarter, dtype=F32) / quarter)
    d = jnp.arange(A_DH)
    pos = jnp.where(d[None, :] < A_DH // 2, row[:, None], col[:, None])
    ang = pos * inv[d % quarter][None, :]
    sign = jnp.where((d % (A_DH // 2)) < quarter, -1.0, 1.0).astype(F32)
    cos = jnp.cos(ang)
    sin = jnp.sin(ang) * sign[None, :]
    return jnp.tile(cos, (1, KV_WIDTH // A_DH)), jnp.tile(sin, (1, KV_WIDTH // A_DH))


def _run_chunk(x, mod3, mod_row0, prm, cache, rotate, gate_on):
    (n1, n2, w_main, w_g, w_gt, b_g, b_gt, mhg, qg_t, kg_t, bd, sink, w_m, w_a, w_q, sub_a, sub_b,
     u_pack, v_pack) = prm
    batch, seq, _ = x.shape
    n = batch * seq
    x2d = x.reshape(n, D_MODEL)
    if gate_on is not None:
        x2d, _ = lax.optimization_barrier((x2d, gate_on))
    cos, sin = _rope_tables(seq, rotate)
    mq, mk, mv, mo, gcol, grow, aq, ak, av = _inproj_call(
        x2d, seq, mod3, mod_row0, n1, w_main, w_g, w_gt, b_g, b_gt, qg_t, kg_t, bd, cos, sin)
    kc, vc, c0, m0 = cache
    h_f, h_b, c_fin, m_fin = _mlstm_call(mq, mk, mv, gcol, grow, c0, m0, batch, seq)
    if kc is None:
        a_out = _attn_ctx_call(sink, aq, ak, av, batch, seq)
    else:
        a_out = _attn_lat_call(sink, aq, ak, av, kc, vc, batch, seq)
    x1, h2p, eidx, gates = _mix_call(x2d, seq, h_f, h_b, mo, a_out, mod3, mod_row0, mhg, n2, w_m, w_a, w_q,
                                     sub_a, sub_b, 0, n // TOKEN_TILE)
    peer_out = _peer_experts(h2p, eidx, gates, u_pack, v_pack)
    y = _resid_call(x1, peer_out, seq, mod3, mod_row0, 0).reshape(batch, seq, D_MODEL)
    return y, h2p, peer_out, ak, av, c_fin, m_fin


def _pack_state(C, n_vec, m):
    b = C.shape[0]
    caug = jnp.concatenate([C, jnp.broadcast_to(n_vec[..., None], C.shape)], axis=-1)
    caug = caug.reshape(b, 2 * M_HEADS, M_DH, 2 * M_DH)
    m_rep = jnp.broadcast_to(m.reshape(b, 2 * M_HEADS, 1, 1), (b, 2 * M_HEADS, 8, M_DH))
    return caug.astype(F32), m_rep.astype(F32)


def kernel(x_prompt, x_sample, c, cache_attn_k, cache_attn_v, state_mlstm_C, state_mlstm_n, state_mlstm_m,
           c_ctx, w_ada, b_ada, norm1_g, norm2_g, w_in, b_gates, mh_norm_g, q_norm_g, k_norm_g, sink_logits,
           w_out, peer_w_q, peer_sub_a, peer_sub_b, peer_u, peer_v):
    depth = w_ada.shape[0]
    assert depth == 1
    batch, seq, _ = x_prompt.shape
    dec_batch, dec_seq, _ = x_sample.shape
    l = 0

    cond = jnp.concatenate([c_ctx[None, :], c, jnp.zeros((MOD_ROWS - 1 - dec_batch, D_MODEL), F32)], axis=0)
    mod3 = _ada_call(cond, w_ada[l], b_ada[l]).reshape(MOD_ROWS, 6, D_MODEL)

    wi = w_in[l]
    g0 = 4 * M_WIDTH
    w_main = jnp.concatenate([wi[:, :g0], wi[:, g0 + N_GATES:]], axis=1).astype(BF16)
    w_g = wi[:, g0:g0 + N_GATES]
    seg = jnp.arange(A_WIDTH) // A_DH
    bd = jnp.where(seg[:, None] == seg[None, :], 1.0 / A_DH, 0.0).astype(F32)
    prm = (norm1_g[l][None, :], norm2_g[l][None, :], w_main, w_g, w_g.T, b_gates[l][None, :], b_gates[l][:, None],
           mh_norm_g[l][None, :], jnp.tile(q_norm_g[l], A_HEADS)[None, :], jnp.tile(k_norm_g[l], A_KV)[None, :], bd,
           sink_logits[l], w_out[l][:M_WIDTH].astype(BF16), w_out[l][M_WIDTH:].astype(BF16),
           peer_w_q[l].astype(BF16), peer_sub_a[l].astype(BF16), peer_sub_b[l].astype(BF16),
           _pack_bf16_pairs(peer_u[l]), _pack_bf16_pairs(peer_v[l]))

    zeros_c = jnp.zeros((batch, 2, M_HEADS, M_DH, M_DH), F32)
    c0, m0 = _pack_state(zeros_c, zeros_c[..., 0], jnp.full((batch, 2, M_HEADS), NEG, F32))
    c0s, m0s = _pack_state(state_mlstm_C[:, l], state_mlstm_n[:, l], state_mlstm_m[:, l])
    past = cache_attn_k.shape[2]
    kc = cache_attn_k[:, l].reshape(dec_batch, past, KV_WIDTH)
    vc = cache_attn_v[:, l].reshape(dec_batch, past, KV_WIDTH)

    jobs = []
    for b0 in range(0, batch, batch // CTX_CHUNKS):
        b1 = b0 + batch // CTX_CHUNKS
        jobs.append((x_prompt[b0:b1], 0, (None, None, c0[b0:b1], m0[b0:b1]), False))
    for b0 in range(0, dec_batch, dec_batch // LATENT_CHUNKS):
        b1 = b0 + dec_batch // LATENT_CHUNKS
        jobs.append((x_sample[b0:b1], 1 + b0, (kc[b0:b1], vc[b0:b1], c0s[b0:b1], m0s[b0:b1]), True))
    outs = []
    for i, (x_c, mod_row0, cache, rotate) in enumerate(jobs):
        gate = tuple(g for g in (outs[i - 1][1] if i >= 1 else None, outs[i - 2][2] if i >= 2 else None)
                     if g is not None)
        outs.append(_run_chunk(x_c, mod3, mod_row0, prm, cache, rotate, gate or None))
    ctx, lat = outs[:CTX_CHUNKS], outs[CTX_CHUNKS:]
    y_p = jnp.concatenate([o[0] for o in ctx], axis=0)
    y_s = jnp.concatenate([o[0] for o in lat], axis=0)
    k_new = jnp.concatenate([o[3] for o in ctx], axis=0)
    v_new = jnp.concatenate([o[4] for o in ctx], axis=0)
    c_fin = jnp.concatenate([o[5] for o in ctx], axis=0)
    m_fin = jnp.concatenate([o[6] for o in ctx], axis=0)

    c_fin = c_fin.reshape(batch, 2, M_HEADS, M_DH, 2 * M_DH)
    new_c = c_fin[..., :M_DH][:, None]
    new_n = c_fin[..., M_DH][:, None]
    new_m = m_fin[:, :, 0, 0].reshape(batch, 2, M_HEADS)[:, None]
    new_k = k_new.reshape(batch, 1, seq, A_KV, A_DH)
    new_v = v_new.reshape(batch, 1, seq, A_KV, A_DH)
    return y_p, y_s, new_k, new_v, new_c, new_n, new_m
```

```python
import functools

import jax
import jax.numpy as jnp
from jax import lax
from jax.experimental import pallas as pl
from jax.experimental.pallas import tpu as pltpu
from jax.experimental.pallas import tpu_sc as plsc

F32 = jnp.float32
BF16 = jnp.bfloat16
I32 = jnp.int32
HI = lax.Precision.HIGHEST

D_MODEL = 1024
EPS = 1e-6
NEG = -1e30
GRID_W = 64
M_HEADS = 4
M_WIDTH = 512
M_DH = 128
A_HEADS = 8
A_KV = 2
A_REP = 4
A_DH = 64
A_WIDTH = 512
KV_WIDTH = A_KV * A_DH
BLOCK = 128
ROPE_BASE = 10000.0
N_KEYS = 128
P_HEADS = 8
P_DKEY = 256
P_HALF = 128
P_TOPK = 16
N_SEL = P_HEADS * P_TOPK
N_GATES = 4 * M_HEADS
MAIN_COLS = 4 * M_WIDTH + A_WIDTH + 2 * KV_WIDTH
MOD_ROWS = 16

TOKEN_TILE = 256
MLSTM_CHUNK = 128
ADA_COL_TILE = 768
CTX_CHUNKS = 2
LATENT_CHUNKS = 4


def _sigmoid(x):
    return 1.0 / (1.0 + jnp.exp(-x))


def _log_sigmoid(x):
    return jnp.minimum(x, 0.0) - jnp.log1p(jnp.exp(-jnp.abs(x)))


def _dot_t(a, b, precision=None):
    return lax.dot_general(a, b, (((1,), (1,)), ((), ())), precision=precision,
                           preferred_element_type=F32)


def _ada_kernel(c_ref, w_ref, b_ref, o_ref):
    c = c_ref[...]
    s = c * _sigmoid(c)
    o_ref[...] = jnp.dot(s, w_ref[...], precision=HI, preferred_element_type=F32) + b_ref[...]


def _ada_call(cond, w_ada, b_ada):
    n_out = w_ada.shape[1]
    return pl.pallas_call(
        _ada_kernel,
        grid=(n_out // ADA_COL_TILE,),
        in_specs=[pl.BlockSpec((MOD_ROWS, D_MODEL), lambda j: (0, 0)),
                  pl.BlockSpec((D_MODEL, ADA_COL_TILE), lambda j: (0, j)),
                  pl.BlockSpec((1, ADA_COL_TILE), lambda j: (0, j))],
        out_specs=pl.BlockSpec((MOD_ROWS, ADA_COL_TILE), lambda j: (0, j)),
        out_shape=jax.ShapeDtypeStruct((MOD_ROWS, n_out), F32),
        name="ada",
    )(cond, w_ada, b_ada.reshape(1, n_out))


def _swap16(x):
    n = x.shape[-1]
    lane = lax.broadcasted_iota(I32, x.shape, x.ndim - 1)
    return jnp.where((lane & 16) == 0, pltpu.roll(x, n - 16, x.ndim - 1), pltpu.roll(x, 16, x.ndim - 1))


def _inproj_kernel(x_ref, mod_ref, n1_ref, w_ref, wg_ref, wgt_ref, bg_ref, bgt_ref, qg_ref, kg_ref,
                   bd_ref, cos_ref, sin_ref,
                   mq_ref, mk_ref, mv_ref, mo_ref, gc_ref, gr_ref, aq_ref, ak_ref, av_ref):
    x = x_ref[...]
    h = x * lax.rsqrt(jnp.mean(x * x, axis=-1, keepdims=True) + EPS) * n1_ref[...]
    h = h * (1.0 + mod_ref[0, 1:2, :]) + mod_ref[0, 0:1, :]
    z = jnp.dot(h.astype(BF16), w_ref[...], preferred_element_type=F32)

    mq_ref[...] = (z[:, 0:M_WIDTH] * (M_DH ** -0.5)).astype(BF16)
    mk_ref[...] = z[:, M_WIDTH:2 * M_WIDTH].astype(BF16)
    mv_ref[...] = z[:, 2 * M_WIDTH:3 * M_WIDTH].astype(BF16)
    mo_ref[...] = z[:, 3 * M_WIDTH:4 * M_WIDTH]

    g = jnp.dot(h, wg_ref[...], precision=HI, preferred_element_type=F32) + bg_ref[...]
    kind = lax.broadcasted_iota(I32, g.shape, 1) // M_HEADS
    gc_ref[...] = jnp.where((kind & 1) == 1, _log_sigmoid(g), g)
    gt = _dot_t(wgt_ref[...], h, precision=HI) + bgt_ref[...]
    kind_t = lax.broadcasted_iota(I32, gt.shape, 0) // M_HEADS
    gr_ref[...] = jnp.where((kind_t & 1) == 1, _log_sigmoid(gt), gt)

    o = 4 * M_WIDTH
    aq = z[:, o:o + A_WIDTH]
    ak = z[:, o + A_WIDTH:o + A_WIDTH + KV_WIDTH]
    av_ref[...] = z[:, o + A_WIDTH + KV_WIDTH:o + A_WIDTH + 2 * KV_WIDTH]
    bd = bd_ref[...]
    cos = cos_ref[...]
    sin = sin_ref[...]
    aq = aq * lax.rsqrt(jnp.dot(aq * aq, bd, precision=HI, preferred_element_type=F32) + EPS) * qg_ref[...]
    cos4 = jnp.concatenate([cos] * (A_WIDTH // KV_WIDTH), axis=1)
    sin4 = jnp.concatenate([sin] * (A_WIDTH // KV_WIDTH), axis=1)
    aq = (aq * cos4 + _swap16(aq) * sin4) * (A_DH ** -0.5)
    ak = ak * lax.rsqrt(jnp.dot(ak * ak, bd[0:KV_WIDTH, 0:KV_WIDTH], precision=HI,
                                preferred_element_type=F32) + EPS) * kg_ref[...]
    ak_ref[...] = ak * cos + _swap16(ak) * sin

    lane = lax.broadcasted_iota(I32, (aq.shape[0], KV_WIDTH), 1)
    for hd in range(A_HEADS):
        grp = hd // A_REP
        blk = aq[:, (hd // 2) * KV_WIDTH:(hd // 2 + 1) * KV_WIDTH]
        if hd % 2 != grp:
            blk = pltpu.roll(blk, A_DH, 1)
        keep = (lane >= grp * A_DH) & (lane < (grp + 1) * A_DH)
        aq_ref[hd] = jnp.where(keep, blk, 0.0).astype(BF16)


def _inproj_call(x2d, seq, mod3, mod_row0, n1, w_main, w_g, w_gt, b_g, b_gt, qg_t, kg_t, bd, cos, sin):
    n = x2d.shape[0]
    tm = TOKEN_TILE
    per_seq = seq // tm

    def tok(i):
        return (i, 0)

    def const2(i):
        return (0, 0)

    in_specs = [
        pl.BlockSpec((tm, D_MODEL), tok),
        pl.BlockSpec((1, 6, D_MODEL), lambda i: (mod_row0 + (i // per_seq if mod_row0 else 0), 0, 0)),
        pl.BlockSpec((1, D_MODEL), const2),
        pl.BlockSpec((D_MODEL, MAIN_COLS), const2),
        pl.BlockSpec((D_MODEL, N_GATES), const2),
        pl.BlockSpec((N_GATES, D_MODEL), const2),
        pl.BlockSpec((1, N_GATES), const2),
        pl.BlockSpec((N_GATES, 1), const2),
        pl.BlockSpec((1, A_WIDTH), const2),
        pl.BlockSpec((1, KV_WIDTH), const2),
        pl.BlockSpec((A_WIDTH, A_WIDTH), const2),
        pl.BlockSpec((tm, KV_WIDTH), lambda i: (i % per_seq, 0)),
        pl.BlockSpec((tm, KV_WIDTH), lambda i: (i % per_seq, 0)),
    ]
    out_specs = [
        pl.BlockSpec((tm, M_WIDTH), tok),
        pl.BlockSpec((tm, M_WIDTH), tok),
        pl.BlockSpec((tm, M_WIDTH), tok),
        pl.BlockSpec((tm, M_WIDTH), tok),
        pl.BlockSpec((tm, N_GATES), tok),
        pl.BlockSpec((N_GATES, tm), lambda i: (0, i)),
        pl.BlockSpec((A_HEADS, tm, KV_WIDTH), lambda i: (0, i, 0)),
        pl.BlockSpec((tm, KV_WIDTH), tok),
        pl.BlockSpec((tm, KV_WIDTH), tok),
    ]
    out_shape = [
        jax.ShapeDtypeStruct((n, M_WIDTH), BF16),
        jax.ShapeDtypeStruct((n, M_WIDTH), BF16),
        jax.ShapeDtypeStruct((n, M_WIDTH), BF16),
        jax.ShapeDtypeStruct((n, M_WIDTH), F32),
        jax.ShapeDtypeStruct((n, N_GATES), F32),
        jax.ShapeDtypeStruct((N_GATES, n), F32),
        jax.ShapeDtypeStruct((A_HEADS, n, KV_WIDTH), BF16),
        jax.ShapeDtypeStruct((n, KV_WIDTH), F32),
        jax.ShapeDtypeStruct((n, KV_WIDTH), F32),
    ]
    return pl.pallas_call(
        _inproj_kernel, grid=(n // tm,), in_specs=in_specs, out_specs=out_specs, out_shape=out_shape,
        compiler_params=pltpu.CompilerParams(dimension_semantics=("parallel",)),
        name="inproj",
    )(x2d, mod3, n1, w_main, w_g, w_gt, b_g, b_gt, qg_t, kg_t, bd, cos, sin)


def _mlstm_chain(q, k, v, li_c, lf_c, li_r, lf_r, caug, m, tri, tri_t, mask, reverse):
    L = q.shape[0]
    last = 0 if reverse else L - 1
    b_c = jnp.dot(tri, jnp.broadcast_to(lf_c, (L, L)), precision=HI, preferred_element_type=F32)
    b_r = jnp.dot(jnp.broadcast_to(lf_r, (8, L)), tri_t, precision=HI, preferred_element_type=F32)[0:1, :]
    a_inter = b_c[:, 0:1] + m
    d = jnp.where(mask, b_c - b_r + li_r, -jnp.inf)
    m_t = jnp.maximum(a_inter, jnp.max(d, axis=1, keepdims=True))
    w_inter = jnp.exp(a_inter - m_t)
    s = _dot_t(q, k) * jnp.exp(d - m_t)
    qc = jnp.dot(q, caug.astype(BF16), preferred_element_type=F32)
    num = jnp.dot(s.astype(BF16), v, preferred_element_type=F32) + w_inter * qc[:, 0:M_DH]
    den = jnp.sum(s, axis=1, keepdims=True) + w_inter * qc[:, M_DH:M_DH + 1]
    den = jnp.maximum(jnp.abs(den), jnp.exp(-m_t))
    h = num / den
    m_new = m_t[last:last + 1, :]
    b_last = b_c[last:last + 1, 0:1]
    g_c = jnp.exp(b_last - b_c[:, 0:1] + li_c - m_new)
    decay = jnp.exp(b_last + m - m_new)
    kw = (k.astype(F32) * g_c).astype(BF16)
    vaug = jnp.concatenate([v, jnp.ones_like(v)], axis=1)
    upd = lax.dot_general(kw, vaug, (((0,), (0,)), ((), ())), preferred_element_type=F32)
    return h, decay * caug + upd, m_new


def _mlstm_kernel(qf_ref, kf_ref, vf_ref, gcf_ref, grf_ref, qb_ref, kb_ref, vb_ref, gcb_ref, grb_ref,
                  c0_ref, m0_ref, hf_ref, hb_ref, cfin_ref, mfin_ref, c_scr, m_scr):
    c = pl.program_id(1)
    nc = pl.num_programs(1)
    L = qf_ref.shape[0]

    @pl.when(c == 0)
    def _():
        c_scr[...] = c0_ref[0]
        m_scr[...] = m0_ref[0]

    row = lax.broadcasted_iota(I32, (L, L), 0)
    col = lax.broadcasted_iota(I32, (L, L), 1)
    lower = row >= col
    upper = row <= col
    lower_f = lower.astype(F32)
    upper_f = upper.astype(F32)

    for direction in range(2):
        reverse = direction == 1
        q_ref, k_ref, v_ref, gc_ref, gr_ref, h_ref = (
            (qb_ref, kb_ref, vb_ref, gcb_ref, grb_ref, hb_ref) if reverse
            else (qf_ref, kf_ref, vf_ref, gcf_ref, grf_ref, hf_ref))
        tri, tri_t, mask = (upper_f, lower_f, upper) if reverse else (lower_f, upper_f, lower)
        gc = gc_ref[...]
        gr = gr_ref[...]
        for hd in range(M_HEADS):
            ch = direction * M_HEADS + hd
            sl = slice(hd * M_DH, (hd + 1) * M_DH)
            ci = 2 * direction * M_HEADS + hd
            cf = ci + M_HEADS
            h, caug, m_new = _mlstm_chain(
                q_ref[:, sl], k_ref[:, sl], v_ref[:, sl],
                gc[:, ci:ci + 1], gc[:, cf:cf + 1], gr[ci:ci + 1, :], gr[cf:cf + 1, :],
                c_scr[ch], m_scr[ch][0:1, 0:1], tri, tri_t, mask, reverse)
            h_ref[:, sl] = h
            c_scr[ch] = caug
            m_scr[ch] = jnp.broadcast_to(m_new, m_scr.shape[1:])

    @pl.when(c == nc - 1)
    def _():
        cfin_ref[0] = c_scr[...]
        mfin_ref[0] = m_scr[...]


def _mlstm_call(mq, mk, mv, gcol, grow, c0, m0, batch, seq):
    n = mq.shape[0]
    L = MLSTM_CHUNK
    nc = seq // L
    n_ch = 2 * M_HEADS

    def fwd(b, c):
        return (b * nc + c, 0)

    def bwd(b, c):
        return (b * nc + nc - 1 - c, 0)

    def fwd_t(b, c):
        return (0, b * nc + c)

    def bwd_t(b, c):
        return (0, b * nc + nc - 1 - c)

    tok = pl.BlockSpec((L, M_WIDTH), fwd)
    tok_b = pl.BlockSpec((L, M_WIDTH), bwd)
    in_specs = [tok, tok, tok, pl.BlockSpec((L, N_GATES), fwd), pl.BlockSpec((N_GATES, L), fwd_t),
                tok_b, tok_b, tok_b, pl.BlockSpec((L, N_GATES), bwd), pl.BlockSpec((N_GATES, L), bwd_t),
                pl.BlockSpec((1, n_ch, M_DH, 2 * M_DH), lambda b, c: (b, 0, 0, 0)),
                pl.BlockSpec((1, n_ch, 8, M_DH), lambda b, c: (b, 0, 0, 0))]
    out_specs = [tok, tok_b,
                 pl.BlockSpec((1, n_ch, M_DH, 2 * M_DH), lambda b, c: (b, 0, 0, 0)),
                 pl.BlockSpec((1, n_ch, 8, M_DH), lambda b, c: (b, 0, 0, 0))]
    out_shape = [jax.ShapeDtypeStruct((n, M_WIDTH), F32), jax.ShapeDtypeStruct((n, M_WIDTH), F32),
                 jax.ShapeDtypeStruct((batch, n_ch, M_DH, 2 * M_DH), F32),
                 jax.ShapeDtypeStruct((batch, n_ch, 8, M_DH), F32)]
    return pl.pallas_call(
        _mlstm_kernel, grid=(batch, nc), in_specs=in_specs, out_specs=out_specs, out_shape=out_shape,
        scratch_shapes=[pltpu.VMEM((n_ch, M_DH, 2 * M_DH), F32), pltpu.VMEM((n_ch, 8, M_DH), F32)],
        compiler_params=pltpu.CompilerParams(dimension_semantics=("parallel", "arbitrary")),
        name="mlstm",
    )(mq, mk, mv, gcol, grow, mq, mk, mv, gcol, grow, c0, m0)


def _sink_column(sink_ref, grp, rows_per_head):
    return jnp.concatenate(
        [jnp.full((rows_per_head, 1), sink_ref[grp * A_REP + r], F32) for r in range(A_REP)], axis=0)


def _store_heads(out_ref, o, grp, rows_per_head):
    for r in range(A_REP):
        hd = grp * A_REP + r
        out_ref[:, hd * A_DH:(hd + 1) * A_DH] = o[r * rows_per_head:(r + 1) * rows_per_head,
                                                  grp * A_DH:(grp + 1) * A_DH].astype(out_ref.dtype)


def _attn_ctx_kernel(sink_ref, q_ref, k_ref, v_ref, out_ref):
    s_len = k_ref.shape[0]
    k = k_ref[...].astype(BF16)
    v = v_ref[...].astype(BF16)
    for grp in range(A_KV):
        q = q_ref[grp * A_REP:(grp + 1) * A_REP].reshape(A_REP * s_len, KV_WIDTH)
        s = _dot_t(q, k)
        sk = _sink_column(sink_ref, grp, s_len)
        mx = jnp.maximum(jnp.max(s, axis=1, keepdims=True), sk)
        p = jnp.exp(s - mx)
        den = jnp.sum(p, axis=1, keepdims=True) + jnp.exp(sk - mx)
        o = jnp.dot(p.astype(BF16), v, preferred_element_type=F32) / den
        _store_heads(out_ref, o, grp, s_len)


def _attn_ctx_call(sink, aq, ak, av, batch, seq):
    n = ak.shape[0]
    return pl.pallas_call(
        _attn_ctx_kernel, grid=(batch,),
        in_specs=[pl.BlockSpec(memory_space=pltpu.SMEM),
                  pl.BlockSpec((A_HEADS, seq, KV_WIDTH), lambda b: (0, b, 0)),
                  pl.BlockSpec((seq, KV_WIDTH), lambda b: (b, 0)),
                  pl.BlockSpec((seq, KV_WIDTH), lambda b: (b, 0))],
        out_specs=pl.BlockSpec((seq, A_WIDTH), lambda b: (b, 0)),
        out_shape=jax.ShapeDtypeStruct((n, A_WIDTH), BF16),
        compiler_params=pltpu.CompilerParams(dimension_semantics=("parallel",)),
        name="attn_ctx",
    )(sink, aq, ak, av)


def _attn_lat_kernel(sink_ref, q_ref, kc_ref, vc_ref, kp_ref, kq_ref, kn_ref, vp_ref, vq_ref, vn_ref, out_ref):
    i = pl.program_id(1)
    nb = pl.num_programs(1)
    kc = kc_ref[0].astype(BF16)
    vc = vc_ref[0].astype(BF16)
    kp, kq, kn = kp_ref[...].astype(BF16), kq_ref[...].astype(BF16), kn_ref[...].astype(BF16)
    vp, vq, vn = vp_ref[...].astype(BF16), vq_ref[...].astype(BF16), vn_ref[...].astype(BF16)
    rows = A_REP * BLOCK
    qpos = lax.broadcasted_iota(I32, (rows, BLOCK), 0) % BLOCK
    kpos = lax.broadcasted_iota(I32, (rows, BLOCK), 1)
    mask_p = (kpos >= qpos) & (i > 0)
    mask_n = (kpos <= qpos) & (i < nb - 1)
    for grp in range(A_KV):
        q = q_ref[grp * A_REP:(grp + 1) * A_REP].reshape(rows, KV_WIDTH)
        s_c = _dot_t(q, kc)
        s_p = jnp.where(mask_p, _dot_t(q, kp), NEG)
        s_q = _dot_t(q, kq)
        s_n = jnp.where(mask_n, _dot_t(q, kn), NEG)
        sk = _sink_column(sink_ref, grp, BLOCK)
        mx = jnp.maximum(jnp.maximum(jnp.max(s_c, axis=1, keepdims=True), jnp.max(s_p, axis=1, keepdims=True)),
                         jnp.maximum(jnp.max(s_q, axis=1, keepdims=True), jnp.max(s_n, axis=1, keepdims=True)))
        mx = jnp.maximum(mx, sk)
        p_c, p_p, p_q, p_n = jnp.exp(s_c - mx), jnp.exp(s_p - mx), jnp.exp(s_q - mx), jnp.exp(s_n - mx)
        den = (jnp.sum(p_c, axis=1, keepdims=True) + jnp.sum(p_p, axis=1, keepdims=True)
               + jnp.sum(p_q, axis=1, keepdims=True) + jnp.sum(p_n, axis=1, keepdims=True) + jnp.exp(sk - mx))
        o = (jnp.dot(p_c.astype(BF16), vc, preferred_element_type=F32)
             + jnp.dot(p_p.astype(BF16), vp, preferred_element_type=F32)
             + jnp.dot(p_q.astype(BF16), vq, preferred_element_type=F32)
             + jnp.dot(p_n.astype(BF16), vn, preferred_element_type=F32)) / den
        _store_heads(out_ref, o, grp, BLOCK)


def _attn_lat_call(sink, aq, ak, av, kc, vc, batch, seq):
    n = ak.shape[0]
    nb = seq // BLOCK
    past = kc.shape[1]

    def cur(b, i):
        return (b * nb + i, 0)

    def prev(b, i):
        return (b * nb + jnp.maximum(i - 1, 0), 0)

    def nxt(b, i):
        return (b * nb + jnp.minimum(i + 1, nb - 1), 0)

    blk = functools.partial(pl.BlockSpec, (BLOCK, KV_WIDTH))
    cache = pl.BlockSpec((1, past, KV_WIDTH), lambda b, i: (b, 0, 0))
    return pl.pallas_call(
        _attn_lat_kernel, grid=(batch, nb),
        in_specs=[pl.BlockSpec(memory_space=pltpu.SMEM),
                  pl.BlockSpec((A_HEADS, BLOCK, KV_WIDTH), lambda b, i: (0, b * nb + i, 0)),
                  cache, cache, blk(prev), blk(cur), blk(nxt), blk(prev), blk(cur), blk(nxt)],
        out_specs=pl.BlockSpec((BLOCK, A_WIDTH), cur),
        out_shape=jax.ShapeDtypeStruct((n, A_WIDTH), BF16),
        compiler_params=pltpu.CompilerParams(dimension_semantics=("parallel", "parallel")),
        name="attn_lat",
    )(sink, aq, kc, vc, ak, ak, ak, av, av, av)


def _top16_rows(s, payload=None):
    n_rows = s.shape[0]
    rows = lax.broadcasted_iota(I32, s.shape, 0).astype(F32)
    vals, idxs, pays = [], [], []
    for _ in range(P_TOPK):
        mx = jnp.max(s, axis=0, keepdims=True)
        ix = jnp.min(jnp.where(s == mx, rows, float(n_rows)), axis=0, keepdims=True)
        hit = rows == ix
        vals.append(mx)
        idxs.append(ix)
        if payload is not None:
            pays.append(jnp.sum(jnp.where(hit, payload, 0.0), axis=0, keepdims=True))
        s = jnp.where(hit, -jnp.inf, s)
    out = (jnp.concatenate(vals, axis=0), jnp.concatenate(idxs, axis=0))
    if payload is not None:
        out += (jnp.concatenate(pays, axis=0),)
    return out


def _mix_kernel(x_ref, hf_ref, hb_ref, mo_ref, ao_ref, mod_ref, mhg_ref, n2_ref, wm_ref, wa_ref, wq_ref,
                sa_ref, sb_ref, x1_ref, h2_ref, eidx_ref, gate_ref, qp_scr, e_scr, g_scr):
    tm = x_ref.shape[0]
    hs = hf_ref[...] + hb_ref[...]
    parts = []
    for hd in range(M_HEADS):
        blk = hs[:, hd * M_DH:(hd + 1) * M_DH]
        parts.append(blk * lax.rsqrt(jnp.mean(blk * blk, axis=-1, keepdims=True) + EPS))
    m_out = _sigmoid(mo_ref[...]) * (jnp.concatenate(parts, axis=1) * mhg_ref[...])
    mix = (jnp.dot(m_out.astype(BF16), wm_ref[...], preferred_element_type=F32)
           + jnp.dot(ao_ref[...], wa_ref[...], preferred_element_type=F32))
    x1 = x_ref[...] + mod_ref[0, 2:3, :] * mix
    x1_ref[...] = x1
    h2 = x1 * lax.rsqrt(jnp.mean(x1 * x1, axis=-1, keepdims=True) + EPS) * n2_ref[...]
    h2 = h2 * (1.0 + mod_ref[0, 4:5, :]) + mod_ref[0, 3:4, :]
    h2_ref[...] = _pack_bf16_pairs(h2)
    qp = jnp.dot(h2.astype(BF16), wq_ref[...], preferred_element_type=F32)
    for p in range(P_HEADS):
        qp_scr[p] = qp[:, p * P_DKEY:(p + 1) * P_DKEY].astype(BF16)
    sub_a = sa_ref[...]
    sub_b = sb_ref[...]

    def head_body(p, carry):
        for half in range(tm // N_KEYS):
            cols = slice(half * N_KEYS, (half + 1) * N_KEYS)
            qh = qp_scr[p, pl.ds(half * N_KEYS, N_KEYS), :]
            s_a = _dot_t(sub_a, qh[:, 0:P_HALF])
            s_b = _dot_t(sub_b, qh[:, P_HALF:P_DKEY])
            va, ia = _top16_rows(s_a)
            vb, ib = _top16_rows(s_b)
            cand = jnp.concatenate([va[i:i + 1, :] + vb for i in range(P_TOPK)], axis=0)
            cidx = jnp.concatenate([ia[i:i + 1, :] * float(N_KEYS) + ib for i in range(P_TOPK)], axis=0)
            top, _, eidx = _top16_rows(cand, cidx)
            ex = jnp.exp(top - jnp.max(top, axis=0, keepdims=True))
            gates = ex / jnp.sum(ex, axis=0, keepdims=True)
            r0 = pl.multiple_of(p * P_TOPK, P_TOPK)
            e_scr[pl.ds(r0, P_TOPK), cols] = eidx
            g_scr[pl.ds(r0, P_TOPK), cols] = gates
        return carry

    lax.fori_loop(0, P_HEADS, head_body, 0)
    for half in range(tm // N_KEYS):
        cols = slice(half * N_KEYS, (half + 1) * N_KEYS)
        eidx_ref[cols, :] = e_scr[:, cols].T.astype(I32)
        gate_ref[cols, :] = g_scr[:, cols].T


def _mix_call(x2d, seq, h_f, h_b, mo, a_out, mod3, mod_row0, mhg, n2, w_m, w_a, w_q, sub_a, sub_b, tile0, n_tiles):
    tm = TOKEN_TILE
    per_seq = seq // tm
    n = n_tiles * tm

    def tok_in(i):
        return (tile0 + i, 0)

    def tok(i):
        return (i, 0)

    def const2(i):
        return (0, 0)

    in_specs = [
        pl.BlockSpec((tm, D_MODEL), tok_in),
        pl.BlockSpec((tm, M_WIDTH), tok_in), pl.BlockSpec((tm, M_WIDTH), tok_in),
        pl.BlockSpec((tm, M_WIDTH), tok_in),
        pl.BlockSpec((tm, A_WIDTH), tok_in),
        pl.BlockSpec((1, 6, D_MODEL), lambda i: (mod_row0 + ((tile0 + i) // per_seq if mod_row0 else 0), 0, 0)),
        pl.BlockSpec((1, M_WIDTH), const2),
        pl.BlockSpec((1, D_MODEL), const2),
        pl.BlockSpec((M_WIDTH, D_MODEL), const2),
        pl.BlockSpec((A_WIDTH, D_MODEL), const2),
        pl.BlockSpec((D_MODEL, P_HEADS * P_DKEY), const2),
        pl.BlockSpec((N_KEYS, P_HALF), const2),
        pl.BlockSpec((N_KEYS, P_HALF), const2),
    ]
    out_specs = [pl.BlockSpec((tm, D_MODEL), tok), pl.BlockSpec((tm, D_MODEL // 2), tok),
                 pl.BlockSpec((tm, N_SEL), tok), pl.BlockSpec((tm, N_SEL), tok)]
    out_shape = [jax.ShapeDtypeStruct((n, D_MODEL), F32), jax.ShapeDtypeStruct((n, D_MODEL // 2), I32),
                 jax.ShapeDtypeStruct((n, N_SEL), I32), jax.ShapeDtypeStruct((n, N_SEL), F32)]
    return pl.pallas_call(
        _mix_kernel, grid=(n_tiles,), in_specs=in_specs, out_specs=out_specs, out_shape=out_shape,
        scratch_shapes=[pltpu.VMEM((P_HEADS, tm, P_DKEY), BF16), pltpu.VMEM((N_SEL, tm), F32),
                        pltpu.VMEM((N_SEL, tm), F32)],
        compiler_params=pltpu.CompilerParams(dimension_semantics=("parallel",)),
        name="mix",
    )(x2d, h_f, h_b, mo, a_out, mod3, mhg, n2, w_m, w_a, w_q, sub_a, sub_b)


SC_LANES = 16
SC_CORES = 2
SC_SUBCORES = 16
SC_WORKERS = SC_CORES * SC_SUBCORES
SC_TOKENS = 8
SC_GROUP = SC_LANES
SC_NGROUPS = N_SEL // SC_GROUP
SC_SLOTS = 4
ROW_WORDS = D_MODEL // 2
SC_DOT_ROWS = 8
SC_DOT_PARTIALS = 2
SC_OWORDS = 16 * SC_LANES
HI_MASK = -65536
GELU_C0 = 0.7978845608028654
GELU_C1 = 0.044715


def _pack_bf16_pairs(x):
    half = x.shape[1] // 2
    bits = lax.bitcast_convert_type(x.astype(BF16).astype(F32), I32)
    return (bits[:, :half] & HI_MASK) | lax.shift_right_logical(bits[:, half:], jnp.int32(16))


def _sc_gelu(a):
    z = GELU_C0 * (a + GELU_C1 * (a * a * a))
    tanh = 1.0 - 2.0 / (jnp.exp(2.0 * z) + 1.0)
    return 0.5 * a * (1.0 + tanh)


def _sc_split(words):
    return (plsc.bitcast(words & HI_MASK, F32), plsc.bitcast(lax.shift_left(words, jnp.int32(16)), F32))


def _sc_mul_bf16(a_words, b_words):
    return plsc.bitcast(a_words, BF16) * plsc.bitcast(b_words, BF16)


def _sc_split_sum(p, q):
    return _sc_split(plsc.bitcast(p + q, I32))


def _peer_sc_kernel(h2_hbm, eidx_hbm, gate_hbm, u_hbm, v_hbm, out_hbm,
                    xbuf, ibuf, gbuf, obuf, ubuf, vbuf, mbuf, wbuf, sems_u, sems_v):
    n = h2_hbm.shape[0]
    per_worker = n // SC_WORKERS
    wid = lax.axis_index("c") * SC_SUBCORES + lax.axis_index("s")
    lane = lax.iota(I32, SC_LANES)

    def split_item(item):
        return lax.shift_right_logical(item, SC_NGROUPS.bit_length() - 1), item & (SC_NGROUPS - 1)

    def gather_copies(item, slot):
        t, g = split_item(item)
        idx = ibuf[t, pl.ds(g * SC_GROUP, SC_GROUP)]
        return (pltpu.make_async_copy(u_hbm.at[idx], ubuf.at[slot], sems_u.at[slot]),
                pltpu.make_async_copy(v_hbm.at[idx], vbuf.at[slot], sems_v.at[slot]))

    def dots(t, slot):
        zero = jnp.zeros((SC_LANES,), F32)

        @pl.loop(0, SC_GROUP, step=SC_DOT_ROWS)
        def _(r0):
            accs = [[zero] * SC_DOT_PARTIALS for _ in range(SC_DOT_ROWS)]
            for k in range(0, ROW_WORDS // SC_LANES, 4):
                xs = [xbuf[t, pl.ds((k + q) * SC_LANES, SC_LANES)] for q in range(4)]
                for i in range(SC_DOT_ROWS):
                    m = [_sc_mul_bf16(xs[q], ubuf[slot, r0 + i, pl.ds((k + q) * SC_LANES, SC_LANES)])
                         for q in range(4)]
                    hi, lo = _sc_split_sum(m[0] + m[1], m[2] + m[3])
                    p = (k // 4) % SC_DOT_PARTIALS
                    accs[i][p] = accs[i][p] + (hi + lo)
            for i in range(SC_DOT_ROWS):
                mbuf[r0 + i, :] = functools.reduce(lambda a, b: a + b, accs[i])

        tot = zero
        for c in range(SC_LANES):
            tot = tot + plsc.load_gather(mbuf, [lane, jnp.full((SC_LANES,), c, I32)])
        return tot

    def accumulate(t, slot):
        nv = SC_OWORDS // SC_LANES
        for oc in range(ROW_WORDS // SC_OWORDS):
            w0 = oc * SC_OWORDS
            accs = (tuple(obuf[t, pl.ds(w0 + j * SC_LANES, SC_LANES)] for j in range(nv))
                    + tuple(obuf[t, pl.ds(ROW_WORDS + w0 + j * SC_LANES, SC_LANES)] for j in range(nv)))

            def row_quad(rq, accs):
                r = 4 * rq
                ws = [plsc.load_gather(wbuf, [jnp.full((SC_LANES,), r + q, I32)]) for q in range(4)]
                his, los = [], []
                for j in range(nv):
                    m = [_sc_mul_bf16(ws[q], vbuf[slot, r + q, pl.ds(w0 + j * SC_LANES, SC_LANES)])
                         for q in range(4)]
                    hi, lo = _sc_split_sum(m[0] + m[1], m[2] + m[3])
                    his.append(accs[j] + hi)
                    los.append(accs[nv + j] + lo)
                return tuple(his) + tuple(los)

            accs = lax.fori_loop(0, SC_GROUP // 4, row_quad, accs)
            for j in range(nv):
                obuf[t, pl.ds(w0 + j * SC_LANES, SC_LANES)] = accs[j]
                obuf[t, pl.ds(ROW_WORDS + w0 + j * SC_LANES, SC_LANES)] = accs[nv + j]

    def pack_weights(w):
        bits = plsc.bitcast(w, I32)
        rounded = (bits + 0x7FFF + (lax.shift_right_logical(bits, jnp.int32(16)) & 1)) & HI_MASK
        return rounded | lax.shift_right_logical(rounded, jnp.int32(16))

    n_items = SC_TOKENS * SC_NGROUPS

    @pl.loop(0, per_worker // SC_TOKENS)
    def _(blk):
        tok0 = pl.multiple_of(wid * per_worker + blk * SC_TOKENS, SC_TOKENS)
        pltpu.sync_copy(h2_hbm.at[pl.ds(tok0, SC_TOKENS)], xbuf)
        pltpu.sync_copy(eidx_hbm.at[pl.ds(tok0, SC_TOKENS)], ibuf)
        pltpu.sync_copy(gate_hbm.at[pl.ds(tok0, SC_TOKENS)], gbuf)

        @pl.loop(0, SC_TOKENS)
        def _(t):
            zero = jnp.zeros((SC_LANES,), F32)
            for j in range(D_MODEL // SC_LANES):
                obuf[t, pl.ds(j * SC_LANES, SC_LANES)] = zero

        for ahead in range(SC_SLOTS - 1):
            for c in gather_copies(ahead, ahead):
                c.start()

        @pl.loop(0, n_items)
        def _(item):
            t, g = split_item(item)
            slot = item & (SC_SLOTS - 1)
            ahead = item + (SC_SLOTS - 1)

            @pl.when(ahead < n_items)
            def _():
                for c in gather_copies(ahead, ahead & (SC_SLOTS - 1)):
                    c.start()

            for c in gather_copies(item, slot):
                c.wait()
            a = dots(t, slot)
            wbuf[...] = pack_weights(gbuf[t, pl.ds(g * SC_GROUP, SC_GROUP)] * _sc_gelu(a))
            accumulate(t, slot)

        pltpu.sync_copy(obuf, out_hbm.at[pl.ds(tok0, SC_TOKENS)])


def _peer_experts(h2p, eidx, gates, u_pack, v_pack):
    n = h2p.shape[0]
    mesh = plsc.VectorSubcoreMesh(core_axis_name="c", subcore_axis_name="s")
    fn = pl.kernel(
        _peer_sc_kernel,
        out_type=jax.ShapeDtypeStruct((n, D_MODEL), F32),
        mesh=mesh,
        scratch_types=[
            pltpu.VMEM((SC_TOKENS, ROW_WORDS), I32),
            pltpu.VMEM((SC_TOKENS, N_SEL), I32),
            pltpu.VMEM((SC_TOKENS, N_SEL), F32),
            pltpu.VMEM((SC_TOKENS, D_MODEL), F32),
            pltpu.VMEM((SC_SLOTS, SC_GROUP, ROW_WORDS), I32),
            pltpu.VMEM((SC_SLOTS, SC_GROUP, ROW_WORDS), I32),
            pltpu.VMEM((SC_GROUP, SC_LANES), F32),
            pltpu.VMEM((SC_LANES,), I32),
            pltpu.SemaphoreType.DMA((SC_SLOTS,)), pltpu.SemaphoreType.DMA((SC_SLOTS,)),
        ],
        compiler_params=pltpu.CompilerParams(needs_layout_passes=False),
        cost_estimate=pl.CostEstimate(
            flops=4 * n * N_SEL * D_MODEL, transcendentals=n * N_SEL,
            bytes_accessed=4 * (2 * n * N_SEL * ROW_WORDS + n * ROW_WORDS + n * D_MODEL + 2 * n * N_SEL)),
        name="peer_experts",
    )
    return fn(h2p, eidx, gates, u_pack, v_pack)


def _resid_kernel(x1_ref, p_ref, mod_ref, o_ref):
    o_ref[...] = x1_ref[...] + mod_ref[0, 5:6, :] * p_ref[...]


def _resid_call(x1, peer_out, seq, mod3, mod_row0, tile0):
    n = x1.shape[0]
    tm = TOKEN_TILE
    per_seq = seq // tm
    tok = pl.BlockSpec((tm, D_MODEL), lambda i: (i, 0))
    return pl.pallas_call(
        _resid_kernel, grid=(n // tm,),
        in_specs=[tok, tok,
                  pl.BlockSpec((1, 6, D_MODEL),
                               lambda i: (mod_row0 + ((tile0 + i) // per_seq if mod_row0 else 0), 0, 0))],
        out_specs=tok, out_shape=jax.ShapeDtypeStruct((n, D_MODEL), F32),
        compiler_params=pltpu.CompilerParams(dimension_semantics=("parallel",)),
        name="resid",
    )(x1, peer_out, mod3)


def _rope_tables(seq, rotate):
    if not rotate:
        return jnp.ones((seq, KV_WIDTH), F32), jnp.zeros((seq, KV_WIDTH), F32)
    quarter = A_DH // 4
    t = jnp.arange(seq)
    row = (t // GRID_W).astype(F32)
    col = (t % GRID_W).astype(F32)
    inv = ROPE_BASE ** (-jnp.arange(quarter, dtype=F32) / quarter)
    d = jnp.arange(A_DH)
    pos = jnp.where(d[None, :] < A_DH // 2, row[:, None], col[:, None])
    ang = pos * inv[d % quarter][None, :]
    sign = jnp.where((d % (A_DH // 2)) < quarter, -1.0, 1.0).astype(F32)
    cos = jnp.cos(ang)
    sin = jnp.sin(ang) * sign[None, :]
    return jnp.tile(cos, (1, KV_WIDTH // A_DH)), jnp.tile(sin, (1, KV_WIDTH // A_DH))


def _run_chunk(x, mod3, mod_row0, prm, cache, rotate, gate_on):
    (n1, n2, w_main, w_g, w_gt, b_g, b_gt, mhg, qg_t, kg_t, bd, sink, w_m, w_a, w_q, sub_a, sub_b,
     u_pack, v_pack) = prm
    batch, seq, _ = x.shape
    n = batch * seq
    x2d = x.reshape(n, D_MODEL)
    if gate_on is not None:
        x2d, _ = lax.optimization_barrier((x2d, gate_on))
    cos, sin = _rope_tables(seq, rotate)
    mq, mk, mv, mo, gcol, grow, aq, ak, av = _inproj_call(
        x2d, seq, mod3, mod_row0, n1, w_main, w_g, w_gt, b_g, b_gt, qg_t, kg_t, bd, cos, sin)
    kc, vc, c0, m0 = cache
    h_f, h_b, c_fin, m_fin = _mlstm_call(mq, mk, mv, gcol, grow, c0, m0, batch, seq)
    if kc is None:
        a_out = _attn_ctx_call(sink, aq, ak, av, batch, seq)
    else:
        a_out = _attn_lat_call(sink, aq, ak, av, kc, vc, batch, seq)
    x1, h2p, eidx, gates = _mix_call(x2d, seq, h_f, h_b, mo, a_out, mod3, mod_row0, mhg, n2, w_m, w_a, w_q,
                                     sub_a, sub_b, 0, n // TOKEN_TILE)
    peer_out = _peer_experts(h2p, eidx, gates, u_pack, v_pack)
    y = _resid_call(x1, peer_out, seq, mod3, mod_row0, 0).reshape(batch, seq, D_MODEL)
    return y, h2p, peer_out, ak, av, c_fin, m_fin


def _pack_state(C, n_vec, m):
    b = C.shape[0]
    caug = jnp.concatenate([C, jnp.broadcast_to(n_vec[..., None], C.shape)], axis=-1)
    caug = caug.reshape(b, 2 * M_HEADS, M_DH, 2 * M_DH)
    m_rep = jnp.broadcast_to(m.reshape(b, 2 * M_HEADS, 1, 1), (b, 2 * M_HEADS, 8, M_DH))
    return caug.astype(F32), m_rep.astype(F32)


def kernel(x_prompt, x_sample, c, cache_attn_k, cache_attn_v, state_mlstm_C, state_mlstm_n, state_mlstm_m,
           c_ctx, w_ada, b_ada, norm1_g, norm2_g, w_in, b_gates, mh_norm_g, q_norm_g, k_norm_g, sink_logits,
           w_out, peer_w_q, peer_sub_a, peer_sub_b, peer_u, peer_v):
    depth = w_ada.shape[0]
    assert depth == 1
    batch, seq, _ = x_prompt.shape
    dec_batch, dec_seq, _ = x_sample.shape
    l = 0

    cond = jnp.concatenate([c_ctx[None, :], c, jnp.zeros((MOD_ROWS - 1 - dec_batch, D_MODEL), F32)], axis=0)
    mod3 = _ada_call(cond, w_ada[l], b_ada[l]).reshape(MOD_ROWS, 6, D_MODEL)

    wi = w_in[l]
    g0 = 4 * M_WIDTH
    w_main = jnp.concatenate([wi[:, :g0], wi[:, g0 + N_GATES:]], axis=1).astype(BF16)
    w_g = wi[:, g0:g0 + N_GATES]
    seg = jnp.arange(A_WIDTH) // A_DH
    bd = jnp.where(seg[:, None] == seg[None, :], 1.0 / A_DH, 0.0).astype(F32)
    prm = (norm1_g[l][None, :], norm2_g[l][None, :], w_main, w_g, w_g.T, b_gates[l][None, :], b_gates[l][:, None],
           mh_norm_g[l][None, :], jnp.tile(q_norm_g[l], A_HEADS)[None, :], jnp.tile(k_norm_g[l], A_KV)[None, :], bd,
           sink_logits[l], w_out[l][:M_WIDTH].astype(BF16), w_out[l][M_WIDTH:].astype(BF16),
           peer_w_q[l].astype(BF16), peer_sub_a[l].astype(BF16), peer_sub_b[l].astype(BF16),
           _pack_bf16_pairs(peer_u[l]), _pack_bf16_pairs(peer_v[l]))

    zeros_c = jnp.zeros((batch, 2, M_HEADS, M_DH, M_DH), F32)
    c0, m0 = _pack_state(zeros_c, zeros_c[..., 0], jnp.full((batch, 2, M_HEADS), NEG, F32))
    c0s, m0s = _pack_state(state_mlstm_C[:, l], state_mlstm_n[:, l], state_mlstm_m[:, l])
    past = cache_attn_k.shape[2]
    kc = cache_attn_k[:, l].reshape(dec_batch, past, KV_WIDTH)
    vc = cache_attn_v[:, l].reshape(dec_batch, past, KV_WIDTH)

    jobs = []
    for b0 in range(0, batch, batch // CTX_CHUNKS):
        b1 = b0 + batch // CTX_CHUNKS
        jobs.append((x_prompt[b0:b1], 0, (None, None, c0[b0:b1], m0[b0:b1]), False))
    for b0 in range(0, dec_batch, dec_batch // LATENT_CHUNKS):
        b1 = b0 + dec_batch // LATENT_CHUNKS
        jobs.append((x_sample[b0:b1], 1 + b0, (kc[b0:b1], vc[b0:b1], c0s[b0:b1], m0s[b0:b1]), True))
    outs = []
    for i, (x_c, mod_row0, cache, rotate) in enumerate(jobs):
        gate = tuple(g for g in (outs[i - 1][1] if i >= 1 else None, outs[i - 2][2] if i >= 2 else None)
                     if g is not None)
        outs.append(_run_chunk(x_c, mod3, mod_row0, prm, cache, rotate, gate or None))
    ctx, lat = outs[:CTX_CHUNKS], outs[CTX_CHUNKS:]
    y_p = jnp.concatenate([o[0] for o in ctx], axis=0)
    y_s = jnp.concatenate([o[0] for o in lat], axis=0)
    k_new = jnp.concatenate([o[3] for o in ctx], axis=0)
    v_new = jnp.concatenate([o[4] for o in ctx], axis=0)
    c_fin = jnp.concatenate([o[5] for o in ctx], axis=0)
    m_fin = jnp.concatenate([o[6] for o in ctx], axis=0)

    c_fin = c_fin.reshape(batch, 2, M_HEADS, M_DH, 2 * M_DH)
    new_c = c_fin[..., :M_DH][:, None]
    new_n = c_fin[..., M_DH][:, None]
    new_m = m_fin[:, :, 0, 0].reshape(batch, 2, M_HEADS)[:, None]
    new_k = k_new.reshape(batch, 1, seq, A_KV, A_DH)
    new_v = v_new.reshape(batch, 1, seq, A_KV, A_DH)
    return y_p, y_s, new_k, new_v, new_c, new_n, new_m
```

```python
import functools

import jax
import jax.numpy as jnp
from jax import lax
from jax.experimental import pallas as pl
from jax.experimental.pallas import tpu as pltpu
from jax.experimental.pallas import tpu_sc as plsc

F32 = jnp.float32
BF16 = jnp.bfloat16
I32 = jnp.int32
HI = lax.Precision.HIGHEST

D_MODEL = 1024
EPS = 1e-6
NEG = -1e30
GRID_W = 64
M_HEADS = 4
M_WIDTH = 512
M_DH = 128
A_HEADS = 8
A_KV = 2
A_REP = 4
A_DH = 64
A_WIDTH = 512
KV_WIDTH = A_KV * A_DH
BLOCK = 128
ROPE_BASE = 10000.0
N_KEYS = 128
P_HEADS = 8
P_DKEY = 256
P_HALF = 128
P_TOPK = 16
N_SEL = P_HEADS * P_TOPK
N_GATES = 4 * M_HEADS
MAIN_COLS = 4 * M_WIDTH + A_WIDTH + 2 * KV_WIDTH
MOD_ROWS = 16

TOKEN_TILE = 256
MLSTM_CHUNK = 128
ADA_COL_TILE = 768
CTX_CHUNKS = 4
EXPERT_LAG = 3
LATENT_CHUNKS = 4


def _sigmoid(x):
    return 1.0 / (1.0 + jnp.exp(-x))


def _log_sigmoid(x):
    return jnp.minimum(x, 0.0) - jnp.log1p(jnp.exp(-jnp.abs(x)))


def _dot_t(a, b, precision=None):
    return lax.dot_general(a, b, (((1,), (1,)), ((), ())), precision=precision,
                           preferred_element_type=F32)


def _ada_kernel(c_ref, w_ref, b_ref, o_ref):
    c = c_ref[...]
    s = c * _sigmoid(c)
    o_ref[...] = jnp.dot(s, w_ref[...], precision=HI, preferred_element_type=F32) + b_ref[...]


def _ada_call(cond, w_ada, b_ada):
    n_out = w_ada.shape[1]
    return pl.pallas_call(
        _ada_kernel,
        grid=(n_out // ADA_COL_TILE,),
        in_specs=[pl.BlockSpec((MOD_ROWS, D_MODEL), lambda j: (0, 0)),
                  pl.BlockSpec((D_MODEL, ADA_COL_TILE), lambda j: (0, j)),
                  pl.BlockSpec((1, ADA_COL_TILE), lambda j: (0, j))],
        out_specs=pl.BlockSpec((MOD_ROWS, ADA_COL_TILE), lambda j: (0, j)),
        out_shape=jax.ShapeDtypeStruct((MOD_ROWS, n_out), F32),
        name="ada",
    )(cond, w_ada, b_ada.reshape(1, n_out))


def _swap16(x):
    n = x.shape[-1]
    lane = lax.broadcasted_iota(I32, x.shape, x.ndim - 1)
    return jnp.where((lane & 16) == 0, pltpu.roll(x, n - 16, x.ndim - 1), pltpu.roll(x, 16, x.ndim - 1))


def _inproj_kernel(x_ref, mod_ref, n1_ref, w_ref, wg_ref, wgt_ref, bg_ref, bgt_ref, qg_ref, kg_ref,
                   bd_ref, cos_ref, sin_ref,
                   mq_ref, mk_ref, mv_ref, mo_ref, gc_ref, gr_ref, aq_ref, ak_ref, av_ref):
    x = x_ref[...]
    h = x * lax.rsqrt(jnp.mean(x * x, axis=-1, keepdims=True) + EPS) * n1_ref[...]
    h = h * (1.0 + mod_ref[0, 1:2, :]) + mod_ref[0, 0:1, :]
    z = jnp.dot(h.astype(BF16), w_ref[...], preferred_element_type=F32)

    mq_ref[...] = (z[:, 0:M_WIDTH] * (M_DH ** -0.5)).astype(BF16)
    mk_ref[...] = z[:, M_WIDTH:2 * M_WIDTH].astype(BF16)
    mv_ref[...] = z[:, 2 * M_WIDTH:3 * M_WIDTH].astype(BF16)
    mo_ref[...] = z[:, 3 * M_WIDTH:4 * M_WIDTH]

    g = jnp.dot(h, wg_ref[...], precision=HI, preferred_element_type=F32) + bg_ref[...]
    kind = lax.broadcasted_iota(I32, g.shape, 1) // M_HEADS
    gc_ref[...] = jnp.where((kind & 1) == 1, _log_sigmoid(g), g)
    gt = _dot_t(wgt_ref[...], h, precision=HI) + bgt_ref[...]
    kind_t = lax.broadcasted_iota(I32, gt.shape, 0) // M_HEADS
    gr_ref[...] = jnp.where((kind_t & 1) == 1, _log_sigmoid(gt), gt)

    o = 4 * M_WIDTH
    aq = z[:, o:o + A_WIDTH]
    ak = z[:, o + A_WIDTH:o + A_WIDTH + KV_WIDTH]
    av_ref[...] = z[:, o + A_WIDTH + KV_WIDTH:o + A_WIDTH + 2 * KV_WIDTH]
    bd = bd_ref[...]
    cos = cos_ref[...]
    sin = sin_ref[...]
    aq = aq * lax.rsqrt(jnp.dot(aq * aq, bd, precision=HI, preferred_element_type=F32) + EPS) * qg_ref[...]
    cos4 = jnp.concatenate([cos] * (A_WIDTH // KV_WIDTH), axis=1)
    sin4 = jnp.concatenate([sin] * (A_WIDTH // KV_WIDTH), axis=1)
    aq = (aq * cos4 + _swap16(aq) * sin4) * (A_DH ** -0.5)
    ak = ak * lax.rsqrt(jnp.dot(ak * ak, bd[0:KV_WIDTH, 0:KV_WIDTH], precision=HI,
                                preferred_element_type=F32) + EPS) * kg_ref[...]
    ak_ref[...] = ak * cos + _swap16(ak) * sin

    lane = lax.broadcasted_iota(I32, (aq.shape[0], KV_WIDTH), 1)
    for hd in range(A_HEADS):
        grp = hd // A_REP
        blk = aq[:, (hd // 2) * KV_WIDTH:(hd // 2 + 1) * KV_WIDTH]
        if hd % 2 != grp:
            blk = pltpu.roll(blk, A_DH, 1)
        keep = (lane >= grp * A_DH) & (lane < (grp + 1) * A_DH)
        aq_ref[hd] = jnp.where(keep, blk, 0.0).astype(BF16)


def _inproj_call(x2d, seq, mod3, mod_row0, n1, w_main, w_g, w_gt, b_g, b_gt, qg_t, kg_t, bd, cos, sin):
    n = x2d.shape[0]
    tm = TOKEN_TILE
    per_seq = seq // tm

    def tok(i):
        return (i, 0)

    def const2(i):
        return (0, 0)

    in_specs = [
        pl.BlockSpec((tm, D_MODEL), tok),
        pl.BlockSpec((1, 6, D_MODEL), lambda i: (mod_row0 + (i // per_seq if mod_row0 else 0), 0, 0)),
        pl.BlockSpec((1, D_MODEL), const2),
        pl.BlockSpec((D_MODEL, MAIN_COLS), const2),
        pl.BlockSpec((D_MODEL, N_GATES), const2),
        pl.BlockSpec((N_GATES, D_MODEL), const2),
        pl.BlockSpec((1, N_GATES), const2),
        pl.BlockSpec((N_GATES, 1), const2),
        pl.BlockSpec((1, A_WIDTH), const2),
        pl.BlockSpec((1, KV_WIDTH), const2),
        pl.BlockSpec((A_WIDTH, A_WIDTH), const2),
        pl.BlockSpec((tm, KV_WIDTH), lambda i: (i % per_seq, 0)),
        pl.BlockSpec((tm, KV_WIDTH), lambda i: (i % per_seq, 0)),
    ]
    out_specs = [
        pl.BlockSpec((tm, M_WIDTH), tok),
        pl.BlockSpec((tm, M_WIDTH), tok),
        pl.BlockSpec((tm, M_WIDTH), tok),
        pl.BlockSpec((tm, M_WIDTH), tok),
        pl.BlockSpec((tm, N_GATES), tok),
        pl.BlockSpec((N_GATES, tm), lambda i: (0, i)),
        pl.BlockSpec((A_HEADS, tm, KV_WIDTH), lambda i: (0, i, 0)),
        pl.BlockSpec((tm, KV_WIDTH), tok),
        pl.BlockSpec((tm, KV_WIDTH), tok),
    ]
    out_shape = [
        jax.ShapeDtypeStruct((n, M_WIDTH), BF16),
        jax.ShapeDtypeStruct((n, M_WIDTH), BF16),
        jax.ShapeDtypeStruct((n, M_WIDTH), BF16),
        jax.ShapeDtypeStruct((n, M_WIDTH), F32),
        jax.ShapeDtypeStruct((n, N_GATES), F32),
        jax.ShapeDtypeStruct((N_GATES, n), F32),
        jax.ShapeDtypeStruct((A_HEADS, n, KV_WIDTH), BF16),
        jax.ShapeDtypeStruct((n, KV_WIDTH), F32),
        jax.ShapeDtypeStruct((n, KV_WIDTH), F32),
    ]
    return pl.pallas_call(
        _inproj_kernel, grid=(n // tm,), in_specs=in_specs, out_specs=out_specs, out_shape=out_shape,
        compiler_params=pltpu.CompilerParams(dimension_semantics=("parallel",)),
        name="inproj",
    )(x2d, mod3, n1, w_main, w_g, w_gt, b_g, b_gt, qg_t, kg_t, bd, cos, sin)


def _mlstm_chain(q, k, v, li_c, lf_c, li_r, lf_r, caug, m, tri, tri_t, mask, reverse):
    L = q.shape[0]
    last = 0 if reverse else L - 1
    b_c = jnp.dot(tri, jnp.broadcast_to(lf_c, (L, L)), precision=HI, preferred_element_type=F32)
    b_r = jnp.dot(jnp.broadcast_to(lf_r, (8, L)), tri_t, precision=HI, preferred_element_type=F32)[0:1, :]
    a_inter = b_c[:, 0:1] + m
    d = jnp.where(mask, b_c - b_r + li_r, -jnp.inf)
    m_t = jnp.maximum(a_inter, jnp.max(d, axis=1, keepdims=True))
    w_inter = jnp.exp(a_inter - m_t)
    s = _dot_t(q, k) * jnp.exp(d - m_t)
    qc = jnp.dot(q, caug.astype(BF16), preferred_element_type=F32)
    num = jnp.dot(s.astype(BF16), v, preferred_element_type=F32) + w_inter * qc[:, 0:M_DH]
    den = jnp.sum(s, axis=1, keepdims=True) + w_inter * qc[:, M_DH:M_DH + 1]
    den = jnp.maximum(jnp.abs(den), jnp.exp(-m_t))
    h = num / den
    m_new = m_t[last:last + 1, :]
    b_last = b_c[last:last + 1, 0:1]
    g_c = jnp.exp(b_last - b_c[:, 0:1] + li_c - m_new)
    decay = jnp.exp(b_last + m - m_new)
    kw = (k.astype(F32) * g_c).astype(BF16)
    vaug = jnp.concatenate([v, jnp.ones_like(v)], axis=1)
    upd = lax.dot_general(kw, vaug, (((0,), (0,)), ((), ())), preferred_element_type=F32)
    return h, decay * caug + upd, m_new


def _mlstm_kernel(qf_ref, kf_ref, vf_ref, gcf_ref, grf_ref, qb_ref, kb_ref, vb_ref, gcb_ref, grb_ref,
                  c0_ref, m0_ref, hf_ref, hb_ref, cfin_ref, mfin_ref, c_scr, m_scr):
    c = pl.program_id(1)
    nc = pl.num_programs(1)
    L = qf_ref.shape[0]

    @pl.when(c == 0)
    def _():
        c_scr[...] = c0_ref[0]
        m_scr[...] = m0_ref[0]

    row = lax.broadcasted_iota(I32, (L, L), 0)
    col = lax.broadcasted_iota(I32, (L, L), 1)
    lower = row >= col
    upper = row <= col
    lower_f = lower.astype(F32)
    upper_f = upper.astype(F32)

    for direction in range(2):
        reverse = direction == 1
        q_ref, k_ref, v_ref, gc_ref, gr_ref, h_ref = (
            (qb_ref, kb_ref, vb_ref, gcb_ref, grb_ref, hb_ref) if reverse
            else (qf_ref, kf_ref, vf_ref, gcf_ref, grf_ref, hf_ref))
        tri, tri_t, mask = (upper_f, lower_f, upper) if reverse else (lower_f, upper_f, lower)
        gc = gc_ref[...]
        gr = gr_ref[...]
        for hd in range(M_HEADS):
            ch = direction * M_HEADS + hd
            sl = slice(hd * M_DH, (hd + 1) * M_DH)
            ci = 2 * direction * M_HEADS + hd
            cf = ci + M_HEADS
            h, caug, m_new = _mlstm_chain(
                q_ref[:, sl], k_ref[:, sl], v_ref[:, sl],
                gc[:, ci:ci + 1], gc[:, cf:cf + 1], gr[ci:ci + 1, :], gr[cf:cf + 1, :],
                c_scr[ch], m_scr[ch][0:1, 0:1], tri, tri_t, mask, reverse)
            h_ref[:, sl] = h
            c_scr[ch] = caug
            m_scr[ch] = jnp.broadcast_to(m_new, m_scr.shape[1:])

    @pl.when(c == nc - 1)
    def _():
        cfin_ref[0] = c_scr[...]
        mfin_ref[0] = m_scr[...]


def _mlstm_call(mq, mk, mv, gcol, grow, c0, m0, batch, seq):
    n = mq.shape[0]
    L = MLSTM_CHUNK
    nc = seq // L
    n_ch = 2 * M_HEADS

    def fwd(b, c):
        return (b * nc + c, 0)

    def bwd(b, c):
        return (b * nc + nc - 1 - c, 0)

    def fwd_t(b, c):
        return (0, b * nc + c)

    def bwd_t(b, c):
        return (0, b * nc + nc - 1 - c)

    tok = pl.BlockSpec((L, M_WIDTH), fwd)
    tok_b = pl.BlockSpec((L, M_WIDTH), bwd)
    in_specs = [tok, tok, tok, pl.BlockSpec((L, N_GATES), fwd), pl.BlockSpec((N_GATES, L), fwd_t),
                tok_b, tok_b, tok_b, pl.BlockSpec((L, N_GATES), bwd), pl.BlockSpec((N_GATES, L), bwd_t),
                pl.BlockSpec((1, n_ch, M_DH, 2 * M_DH), lambda b, c: (b, 0, 0, 0)),
                pl.BlockSpec((1, n_ch, 8, M_DH), lambda b, c: (b, 0, 0, 0))]
    out_specs = [tok, tok_b,
                 pl.BlockSpec((1, n_ch, M_DH, 2 * M_DH), lambda b, c: (b, 0, 0, 0)),
                 pl.BlockSpec((1, n_ch, 8, M_DH), lambda b, c: (b, 0, 0, 0))]
    out_shape = [jax.ShapeDtypeStruct((n, M_WIDTH), F32), jax.ShapeDtypeStruct((n, M_WIDTH), F32),
                 jax.ShapeDtypeStruct((batch, n_ch, M_DH, 2 * M_DH), F32),
                 jax.ShapeDtypeStruct((batch, n_ch, 8, M_DH), F32)]
    return pl.pallas_call(
        _mlstm_kernel, grid=(batch, nc), in_specs=in_specs, out_specs=out_specs, out_shape=out_shape,
        scratch_shapes=[pltpu.VMEM((n_ch, M_DH, 2 * M_DH), F32), pltpu.VMEM((n_ch, 8, M_DH), F32)],
        compiler_params=pltpu.CompilerParams(dimension_semantics=("parallel", "arbitrary")),
        name="mlstm",
    )(mq, mk, mv, gcol, grow, mq, mk, mv, gcol, grow, c0, m0)


def _sink_column(sink_ref, grp, rows_per_head):
    return jnp.concatenate(
        [jnp.full((rows_per_head, 1), sink_ref[grp * A_REP + r], F32) for r in range(A_REP)], axis=0)


def _store_heads(out_ref, o, grp, rows_per_head):
    for r in range(A_REP):
        hd = grp * A_REP + r
        out_ref[:, hd * A_DH:(hd + 1) * A_DH] = o[r * rows_per_head:(r + 1) * rows_per_head,
                                                  grp * A_DH:(grp + 1) * A_DH].astype(out_ref.dtype)


def _attn_ctx_kernel(sink_ref, q_ref, k_ref, v_ref, out_ref):
    s_len = k_ref.shape[0]
    k = k_ref[...].astype(BF16)
    v = v_ref[...].astype(BF16)
    for grp in range(A_KV):
        q = q_ref[grp * A_REP:(grp + 1) * A_REP].reshape(A_REP * s_len, KV_WIDTH)
        s = _dot_t(q, k)
        sk = _sink_column(sink_ref, grp, s_len)
        mx = jnp.maximum(jnp.max(s, axis=1, keepdims=True), sk)
        p = jnp.exp(s - mx)
        den = jnp.sum(p, axis=1, keepdims=True) + jnp.exp(sk - mx)
        o = jnp.dot(p.astype(BF16), v, preferred_element_type=F32) / den
        _store_heads(out_ref, o, grp, s_len)


def _attn_ctx_call(sink, aq, ak, av, batch, seq):
    n = ak.shape[0]
    return pl.pallas_call(
        _attn_ctx_kernel, grid=(batch,),
        in_specs=[pl.BlockSpec(memory_space=pltpu.SMEM),
                  pl.BlockSpec((A_HEADS, seq, KV_WIDTH), lambda b: (0, b, 0)),
                  pl.BlockSpec((seq, KV_WIDTH), lambda b: (b, 0)),
                  pl.BlockSpec((seq, KV_WIDTH), lambda b: (b, 0))],
        out_specs=pl.BlockSpec((seq, A_WIDTH), lambda b: (b, 0)),
        out_shape=jax.ShapeDtypeStruct((n, A_WIDTH), BF16),
        compiler_params=pltpu.CompilerParams(dimension_semantics=("parallel",)),
        name="attn_ctx",
    )(sink, aq, ak, av)


def _attn_lat_kernel(sink_ref, q_ref, kc_ref, vc_ref, kp_ref, kq_ref, kn_ref, vp_ref, vq_ref, vn_ref, out_ref):
    i = pl.program_id(1)
    nb = pl.num_programs(1)
    kc = kc_ref[0].astype(BF16)
    vc = vc_ref[0].astype(BF16)
    kp, kq, kn = kp_ref[...].astype(BF16), kq_ref[...].astype(BF16), kn_ref[...].astype(BF16)
    vp, vq, vn = vp_ref[...].astype(BF16), vq_ref[...].astype(BF16), vn_ref[...].astype(BF16)
    rows = A_REP * BLOCK
    qpos = lax.broadcasted_iota(I32, (rows, BLOCK), 0) % BLOCK
    kpos = lax.broadcasted_iota(I32, (rows, BLOCK), 1)
    mask_p = (kpos >= qpos) & (i > 0)
    mask_n = (kpos <= qpos) & (i < nb - 1)
    for grp in range(A_KV):
        q = q_ref[grp * A_REP:(grp + 1) * A_REP].reshape(rows, KV_WIDTH)
        s_c = _dot_t(q, kc)
        s_p = jnp.where(mask_p, _dot_t(q, kp), NEG)
        s_q = _dot_t(q, kq)
        s_n = jnp.where(mask_n, _dot_t(q, kn), NEG)
        sk = _sink_column(sink_ref, grp, BLOCK)
        mx = jnp.maximum(jnp.maximum(jnp.max(s_c, axis=1, keepdims=True), jnp.max(s_p, axis=1, keepdims=True)),
                         jnp.maximum(jnp.max(s_q, axis=1, keepdims=True), jnp.max(s_n, axis=1, keepdims=True)))
        mx = jnp.maximum(mx, sk)
        p_c, p_p, p_q, p_n = jnp.exp(s_c - mx), jnp.exp(s_p - mx), jnp.exp(s_q - mx), jnp.exp(s_n - mx)
        den = (jnp.sum(p_c, axis=1, keepdims=True) + jnp.sum(p_p, axis=1, keepdims=True)
               + jnp.sum(p_q, axis=1, keepdims=True) + jnp.sum(p_n, axis=1, keepdims=True) + jnp.exp(sk - mx))
        o = (jnp.dot(p_c.astype(BF16), vc, preferred_element_type=F32)
             + jnp.dot(p_p.astype(BF16), vp, preferred_element_type=F32)
             + jnp.dot(p_q.astype(BF16), vq, preferred_element_type=F32)
             + jnp.dot(p_n.astype(BF16), vn, preferred_element_type=F32)) / den
        _store_heads(out_ref, o, grp, BLOCK)


def _attn_lat_call(sink, aq, ak, av, kc, vc, batch, seq):
    n = ak.shape[0]
    nb = seq // BLOCK
    past = kc.shape[1]

    def cur(b, i):
        return (b * nb + i, 0)

    def prev(b, i):
        return (b * nb + jnp.maximum(i - 1, 0), 0)

    def nxt(b, i):
        return (b * nb + jnp.minimum(i + 1, nb - 1), 0)

    blk = functools.partial(pl.BlockSpec, (BLOCK, KV_WIDTH))
    cache = pl.BlockSpec((1, past, KV_WIDTH), lambda b, i: (b, 0, 0))
    return pl.pallas_call(
        _attn_lat_kernel, grid=(batch, nb),
        in_specs=[pl.BlockSpec(memory_space=pltpu.SMEM),
                  pl.BlockSpec((A_HEADS, BLOCK, KV_WIDTH), lambda b, i: (0, b * nb + i, 0)),
                  cache, cache, blk(prev), blk(cur), blk(nxt), blk(prev), blk(cur), blk(nxt)],
        out_specs=pl.BlockSpec((BLOCK, A_WIDTH), cur),
        out_shape=jax.ShapeDtypeStruct((n, A_WIDTH), BF16),
        compiler_params=pltpu.CompilerParams(dimension_semantics=("parallel", "parallel")),
        name="attn_lat",
    )(sink, aq, kc, vc, ak, ak, ak, av, av, av)


def _top16_rows(s, payload=None):
    n_rows = s.shape[0]
    rows = lax.broadcasted_iota(I32, s.shape, 0).astype(F32)
    vals, idxs, pays = [], [], []
    for _ in range(P_TOPK):
        mx = jnp.max(s, axis=0, keepdims=True)
        ix = jnp.min(jnp.where(s == mx, rows, float(n_rows)), axis=0, keepdims=True)
        hit = rows == ix
        vals.append(mx)
        idxs.append(ix)
        if payload is not None:
            pays.append(jnp.sum(jnp.where(hit, payload, 0.0), axis=0, keepdims=True))
        s = jnp.where(hit, -jnp.inf, s)
    out = (jnp.concatenate(vals, axis=0), jnp.concatenate(idxs, axis=0))
    if payload is not None:
        out += (jnp.concatenate(pays, axis=0),)
    return out


def _mix_kernel(x_ref, hf_ref, hb_ref, mo_ref, ao_ref, mod_ref, mhg_ref, n2_ref, wm_ref, wa_ref, wq_ref,
                sa_ref, sb_ref, x1_ref, h2_ref, eidx_ref, gate_ref, qp_scr, e_scr, g_scr):
    tm = x_ref.shape[0]
    hs = hf_ref[...] + hb_ref[...]
    parts = []
    for hd in range(M_HEADS):
        blk = hs[:, hd * M_DH:(hd + 1) * M_DH]
        parts.append(blk * lax.rsqrt(jnp.mean(blk * blk, axis=-1, keepdims=True) + EPS))
    m_out = _sigmoid(mo_ref[...]) * (jnp.concatenate(parts, axis=1) * mhg_ref[...])
    mix = (jnp.dot(m_out.astype(BF16), wm_ref[...], preferred_element_type=F32)
           + jnp.dot(ao_ref[...], wa_ref[...], preferred_element_type=F32))
    x1 = x_ref[...] + mod_ref[0, 2:3, :] * mix
    x1_ref[...] = x1
    h2 = x1 * lax.rsqrt(jnp.mean(x1 * x1, axis=-1, keepdims=True) + EPS) * n2_ref[...]
    h2 = h2 * (1.0 + mod_ref[0, 4:5, :]) + mod_ref[0, 3:4, :]
    h2_ref[...] = _pack_bf16_pairs(h2)
    qp = jnp.dot(h2.astype(BF16), wq_ref[...], preferred_element_type=F32)
    for p in range(P_HEADS):
        qp_scr[p] = qp[:, p * P_DKEY:(p + 1) * P_DKEY].astype(BF16)
    sub_a = sa_ref[...]
    sub_b = sb_ref[...]

    def head_body(p, carry):
        for half in range(tm // N_KEYS):
            cols = slice(half * N_KEYS, (half + 1) * N_KEYS)
            qh = qp_scr[p, pl.ds(half * N_KEYS, N_KEYS), :]
            s_a = _dot_t(sub_a, qh[:, 0:P_HALF])
            s_b = _dot_t(sub_b, qh[:, P_HALF:P_DKEY])
            va, ia = _top16_rows(s_a)
            vb, ib = _top16_rows(s_b)
            cand = jnp.concatenate([va[i:i + 1, :] + vb for i in range(P_TOPK)], axis=0)
            cidx = jnp.concatenate([ia[i:i + 1, :] * float(N_KEYS) + ib for i in range(P_TOPK)], axis=0)
            top, _, eidx = _top16_rows(cand, cidx)
            ex = jnp.exp(top - jnp.max(top, axis=0, keepdims=True))
            gates = ex / jnp.sum(ex, axis=0, keepdims=True)
            r0 = pl.multiple_of(p * P_TOPK, P_TOPK)
            e_scr[pl.ds(r0, P_TOPK), cols] = eidx
            g_scr[pl.ds(r0, P_TOPK), cols] = gates
        return carry

    lax.fori_loop(0, P_HEADS, head_body, 0)
    for half in range(tm // N_KEYS):
        cols = slice(half * N_KEYS, (half + 1) * N_KEYS)
        eidx_ref[cols, :] = e_scr[:, cols].T.astype(I32)
        gate_ref[cols, :] = g_scr[:, cols].T


def _mix_call(x2d, seq, h_f, h_b, mo, a_out, mod3, mod_row0, mhg, n2, w_m, w_a, w_q, sub_a, sub_b, tile0, n_tiles):
    tm = TOKEN_TILE
    per_seq = seq // tm
    n = n_tiles * tm

    def tok_in(i):
        return (tile0 + i, 0)

    def tok(i):
        return (i, 0)

    def const2(i):
        return (0, 0)

    in_specs = [
        pl.BlockSpec((tm, D_MODEL), tok_in),
        pl.BlockSpec((tm, M_WIDTH), tok_in), pl.BlockSpec((tm, M_WIDTH), tok_in),
        pl.BlockSpec((tm, M_WIDTH), tok_in),
        pl.BlockSpec((tm, A_WIDTH), tok_in),
        pl.BlockSpec((1, 6, D_MODEL), lambda i: (mod_row0 + ((tile0 + i) // per_seq if mod_row0 else 0), 0, 0)),
        pl.BlockSpec((1, M_WIDTH), const2),
        pl.BlockSpec((1, D_MODEL), const2),
        pl.BlockSpec((M_WIDTH, D_MODEL), const2),
        pl.BlockSpec((A_WIDTH, D_MODEL), const2),
        pl.BlockSpec((D_MODEL, P_HEADS * P_DKEY), const2),
        pl.BlockSpec((N_KEYS, P_HALF), const2),
        pl.BlockSpec((N_KEYS, P_HALF), const2),
    ]
    out_specs = [pl.BlockSpec((tm, D_MODEL), tok), pl.BlockSpec((tm, D_MODEL // 2), tok),
                 pl.BlockSpec((tm, N_SEL), tok), pl.BlockSpec((tm, N_SEL), tok)]
    out_shape = [jax.ShapeDtypeStruct((n, D_MODEL), F32), jax.ShapeDtypeStruct((n, D_MODEL // 2), I32),
                 jax.ShapeDtypeStruct((n, N_SEL), I32), jax.ShapeDtypeStruct((n, N_SEL), F32)]
    return pl.pallas_call(
        _mix_kernel, grid=(n_tiles,), in_specs=in_specs, out_specs=out_specs, out_shape=out_shape,
        scratch_shapes=[pltpu.VMEM((P_HEADS, tm, P_DKEY), BF16), pltpu.VMEM((N_SEL, tm), F32),
                        pltpu.VMEM((N_SEL, tm), F32)],
        compiler_params=pltpu.CompilerParams(dimension_semantics=("parallel",)),
        name="mix",
    )(x2d, h_f, h_b, mo, a_out, mod3, mhg, n2, w_m, w_a, w_q, sub_a, sub_b)


SC_LANES = 16
SC_CORES = 2
SC_SUBCORES = 16
SC_WORKERS = SC_CORES * SC_SUBCORES
SC_TOKENS = 8
SC_GROUP = SC_LANES
SC_NGROUPS = N_SEL // SC_GROUP
SC_SLOTS = 4
ROW_WORDS = D_MODEL // 2
SC_DOT_ROWS = 8
SC_DOT_PARTIALS = 2
SC_OWORDS = 16 * SC_LANES
HI_MASK = -65536
GELU_C0 = 0.7978845608028654
GELU_C1 = 0.044715


def _pack_bf16_pairs(x):
    half = x.shape[1] // 2
    bits = lax.bitcast_convert_type(x.astype(BF16).astype(F32), I32)
    return (bits[:, :half] & HI_MASK) | lax.shift_right_logical(bits[:, half:], jnp.int32(16))


def _sc_gelu(a):
    z = GELU_C0 * (a + GELU_C1 * (a * a * a))
    tanh = 1.0 - 2.0 / (jnp.exp(2.0 * z) + 1.0)
    return 0.5 * a * (1.0 + tanh)


def _sc_split(words):
    return (plsc.bitcast(words & HI_MASK, F32), plsc.bitcast(lax.shift_left(words, jnp.int32(16)), F32))


def _sc_mul_bf16(a_words, b_words):
    return plsc.bitcast(a_words, BF16) * plsc.bitcast(b_words, BF16)


def _sc_split_sum(p, q):
    return _sc_split(plsc.bitcast(p + q, I32))


def _peer_sc_kernel(h2_hbm, eidx_hbm, gate_hbm, u_hbm, v_hbm, out_hbm,
                    xbuf, ibuf, gbuf, obuf, ubuf, vbuf, mbuf, wbuf, sems_u, sems_v):
    n = h2_hbm.shape[0]
    per_worker = n // SC_WORKERS
    wid = lax.axis_index("c") * SC_SUBCORES + lax.axis_index("s")
    lane = lax.iota(I32, SC_LANES)

    def split_item(item):
        return lax.shift_right_logical(item, SC_NGROUPS.bit_length() - 1), item & (SC_NGROUPS - 1)

    def gather_copies(item, slot):
        t, g = split_item(item)
        idx = ibuf[t, pl.ds(g * SC_GROUP, SC_GROUP)]
        return (pltpu.make_async_copy(u_hbm.at[idx], ubuf.at[slot], sems_u.at[slot]),
                pltpu.make_async_copy(v_hbm.at[idx], vbuf.at[slot], sems_v.at[slot]))

    def dots(t, slot):
        zero = jnp.zeros((SC_LANES,), F32)

        @pl.loop(0, SC_GROUP, step=SC_DOT_ROWS)
        def _(r0):
            accs = [[zero] * SC_DOT_PARTIALS for _ in range(SC_DOT_ROWS)]
            for k in range(0, ROW_WORDS // SC_LANES, 4):
                xs = [xbuf[t, pl.ds((k + q) * SC_LANES, SC_LANES)] for q in range(4)]
                for i in range(SC_DOT_ROWS):
                    m = [_sc_mul_bf16(xs[q], ubuf[slot, r0 + i, pl.ds((k + q) * SC_LANES, SC_LANES)])
                         for q in range(4)]
                    hi, lo = _sc_split_sum(m[0] + m[1], m[2] + m[3])
                    p = (k // 4) % SC_DOT_PARTIALS
                    accs[i][p] = accs[i][p] + (hi + lo)
            for i in range(SC_DOT_ROWS):
                mbuf[r0 + i, :] = functools.reduce(lambda a, b: a + b, accs[i])

        tot = zero
        for c in range(SC_LANES):
            tot = tot + plsc.load_gather(mbuf, [lane, jnp.full((SC_LANES,), c, I32)])
        return tot

    def accumulate(t, slot):
        nv = SC_OWORDS // SC_LANES
        for oc in range(ROW_WORDS // SC_OWORDS):
            w0 = oc * SC_OWORDS
            accs = (tuple(obuf[t, pl.ds(w0 + j * SC_LANES, SC_LANES)] for j in range(nv))
                    + tuple(obuf[t, pl.ds(ROW_WORDS + w0 + j * SC_LANES, SC_LANES)] for j in range(nv)))

            def row_quad(rq, accs):
                r = 4 * rq
                ws = [plsc.load_gather(wbuf, [jnp.full((SC_LANES,), r + q, I32)]) for q in range(4)]
                his, los = [], []
                for j in range(nv):
                    m = [_sc_mul_bf16(ws[q], vbuf[slot, r + q, pl.ds(w0 + j * SC_LANES, SC_LANES)])
                         for q in range(4)]
                    hi, lo = _sc_split_sum(m[0] + m[1], m[2] + m[3])
                    his.append(accs[j] + hi)
                    los.append(accs[nv + j] + lo)
                return tuple(his) + tuple(los)

            accs = lax.fori_loop(0, SC_GROUP // 4, row_quad, accs)
            for j in range(nv):
                obuf[t, pl.ds(w0 + j * SC_LANES, SC_LANES)] = accs[j]
                obuf[t, pl.ds(ROW_WORDS + w0 + j * SC_LANES, SC_LANES)] = accs[nv + j]

    def pack_weights(w):
        bits = plsc.bitcast(w, I32)
        rounded = (bits + 0x7FFF + (lax.shift_right_logical(bits, jnp.int32(16)) & 1)) & HI_MASK
        return rounded | lax.shift_right_logical(rounded, jnp.int32(16))

    n_items = SC_TOKENS * SC_NGROUPS

    @pl.loop(0, per_worker // SC_TOKENS)
    def _(blk):
        tok0 = pl.multiple_of(wid * per_worker + blk * SC_TOKENS, SC_TOKENS)
        pltpu.sync_copy(h2_hbm.at[pl.ds(tok0, SC_TOKENS)], xbuf)
        pltpu.sync_copy(eidx_hbm.at[pl.ds(tok0, SC_TOKENS)], ibuf)
        pltpu.sync_copy(gate_hbm.at[pl.ds(tok0, SC_TOKENS)], gbuf)

        @pl.loop(0, SC_TOKENS)
        def _(t):
            zero = jnp.zeros((SC_LANES,), F32)
            for j in range(D_MODEL // SC_LANES):
                obuf[t, pl.ds(j * SC_LANES, SC_LANES)] = zero

        for ahead in range(SC_SLOTS - 1):
            for c in gather_copies(ahead, ahead):
                c.start()

        @pl.loop(0, n_items)
        def _(item):
            t, g = split_item(item)
            slot = item & (SC_SLOTS - 1)
            ahead = item + (SC_SLOTS - 1)

            @pl.when(ahead < n_items)
            def _():
                for c in gather_copies(ahead, ahead & (SC_SLOTS - 1)):
                    c.start()

            for c in gather_copies(item, slot):
                c.wait()
            a = dots(t, slot)
            wbuf[...] = pack_weights(gbuf[t, pl.ds(g * SC_GROUP, SC_GROUP)] * _sc_gelu(a))
            accumulate(t, slot)

        pltpu.sync_copy(obuf, out_hbm.at[pl.ds(tok0, SC_TOKENS)])


def _peer_experts(h2p, eidx, gates, u_pack, v_pack):
    n = h2p.shape[0]
    mesh = plsc.VectorSubcoreMesh(core_axis_name="c", subcore_axis_name="s")
    fn = pl.kernel(
        _peer_sc_kernel,
        out_type=jax.ShapeDtypeStruct((n, D_MODEL), F32),
        mesh=mesh,
        scratch_types=[
            pltpu.VMEM((SC_TOKENS, ROW_WORDS), I32),
            pltpu.VMEM((SC_TOKENS, N_SEL), I32),
            pltpu.VMEM((SC_TOKENS, N_SEL), F32),
            pltpu.VMEM((SC_TOKENS, D_MODEL), F32),
            pltpu.VMEM((SC_SLOTS, SC_GROUP, ROW_WORDS), I32),
            pltpu.VMEM((SC_SLOTS, SC_GROUP, ROW_WORDS), I32),
            pltpu.VMEM((SC_GROUP, SC_LANES), F32),
            pltpu.VMEM((SC_LANES,), I32),
            pltpu.SemaphoreType.DMA((SC_SLOTS,)), pltpu.SemaphoreType.DMA((SC_SLOTS,)),
        ],
        compiler_params=pltpu.CompilerParams(needs_layout_passes=False),
        cost_estimate=pl.CostEstimate(
            flops=4 * n * N_SEL * D_MODEL, transcendentals=n * N_SEL,
            bytes_accessed=4 * (2 * n * N_SEL * ROW_WORDS + n * ROW_WORDS + n * D_MODEL + 2 * n * N_SEL)),
        name="peer_experts",
    )
    return fn(h2p, eidx, gates, u_pack, v_pack)


def _resid_kernel(x1_ref, p_ref, mod_ref, o_ref):
    o_ref[...] = x1_ref[...] + mod_ref[0, 5:6, :] * p_ref[...]


def _resid_call(x1, peer_out, seq, mod3, mod_row0, tile0):
    n = x1.shape[0]
    tm = TOKEN_TILE
    per_seq = seq // tm
    tok = pl.BlockSpec((tm, D_MODEL), lambda i: (i, 0))
    return pl.pallas_call(
        _resid_kernel, grid=(n // tm,),
        in_specs=[tok, tok,
                  pl.BlockSpec((1, 6, D_MODEL),
                               lambda i: (mod_row0 + ((tile0 + i) // per_seq if mod_row0 else 0), 0, 0))],
        out_specs=tok, out_shape=jax.ShapeDtypeStruct((n, D_MODEL), F32),
        compiler_params=pltpu.CompilerParams(dimension_semantics=("parallel",)),
        name="resid",
    )(x1, peer_out, mod3)


def _rope_tables(seq, rotate):
    if not rotate:
        return jnp.ones((seq, KV_WIDTH), F32), jnp.zeros((seq, KV_WIDTH), F32)
    quarter = A_DH // 4
    t = jnp.arange(seq)
    row = (t // GRID_W).astype(F32)
    col = (t % GRID_W).astype(F32)
    inv = ROPE_BASE ** (-jnp.arange(quarter, dtype=F32) / quarter)
    d = jnp.arange(A_DH)
    pos = jnp.where(d[None, :] < A_DH // 2, row[:, None], col[:, None])
    ang = pos * inv[d % quarter][None, :]
    sign = jnp.where((d % (A_DH // 2)) < quarter, -1.0, 1.0).astype(F32)
    cos = jnp.cos(ang)
    sin = jnp.sin(ang) * sign[None, :]
    return jnp.tile(cos, (1, KV_WIDTH // A_DH)), jnp.tile(sin, (1, KV_WIDTH // A_DH))


def _run_chunk(x, mod3, mod_row0, prm, cache, rotate, gate_on):
    (n1, n2, w_main, w_g, w_gt, b_g, b_gt, mhg, qg_t, kg_t, bd, sink, w_m, w_a, w_q, sub_a, sub_b,
     u_pack, v_pack) = prm
    batch, seq, _ = x.shape
    n = batch * seq
    x2d = x.reshape(n, D_MODEL)
    if gate_on is not None:
        x2d, _ = lax.optimization_barrier((x2d, gate_on))
    cos, sin = _rope_tables(seq, rotate)
    mq, mk, mv, mo, gcol, grow, aq, ak, av = _inproj_call(
        x2d, seq, mod3, mod_row0, n1, w_main, w_g, w_gt, b_g, b_gt, qg_t, kg_t, bd, cos, sin)
    kc, vc, c0, m0 = cache
    h_f, h_b, c_fin, m_fin = _mlstm_call(mq, mk, mv, gcol, grow, c0, m0, batch, seq)
    if kc is None:
        a_out = _attn_ctx_call(sink, aq, ak, av, batch, seq)
    else:
        a_out = _attn_lat_call(sink, aq, ak, av, kc, vc, batch, seq)
    x1, h2p, eidx, gates = _mix_call(x2d, seq, h_f, h_b, mo, a_out, mod3, mod_row0, mhg, n2, w_m, w_a, w_q,
                                     sub_a, sub_b, 0, n // TOKEN_TILE)
    peer_out = _peer_experts(h2p, eidx, gates, u_pack, v_pack)
    y = _resid_call(x1, peer_out, seq, mod3, mod_row0, 0).reshape(batch, seq, D_MODEL)
    return y, h2p, peer_out, ak, av, c_fin, m_fin


def _pack_state(C, n_vec, m):
    b = C.shape[0]
    caug = jnp.concatenate([C, jnp.broadcast_to(n_vec[..., None], C.shape)], axis=-1)
    caug = caug.reshape(b, 2 * M_HEADS, M_DH, 2 * M_DH)
    m_rep = jnp.broadcast_to(m.reshape(b, 2 * M_HEADS, 1, 1), (b, 2 * M_HEADS, 8, M_DH))
    return caug.astype(F32), m_rep.astype(F32)


def kernel(x_prompt, x_sample, c, cache_attn_k, cache_attn_v, state_mlstm_C, state_mlstm_n, state_mlstm_m,
           c_ctx, w_ada, b_ada, norm1_g, norm2_g, w_in, b_gates, mh_norm_g, q_norm_g, k_norm_g, sink_logits,
           w_out, peer_w_q, peer_sub_a, peer_sub_b, peer_u, peer_v):
    depth = w_ada.shape[0]
    assert depth == 1
    batch, seq, _ = x_prompt.shape
    dec_batch, dec_seq, _ = x_sample.shape
    l = 0

    cond = jnp.concatenate([c_ctx[None, :], c, jnp.zeros((MOD_ROWS - 1 - dec_batch, D_MODEL), F32)], axis=0)
    mod3 = _ada_call(cond, w_ada[l], b_ada[l]).reshape(MOD_ROWS, 6, D_MODEL)

    wi = w_in[l]
    g0 = 4 * M_WIDTH
    w_main = jnp.concatenate([wi[:, :g0], wi[:, g0 + N_GATES:]], axis=1).astype(BF16)
    w_g = wi[:, g0:g0 + N_GATES]
    seg = jnp.arange(A_WIDTH) // A_DH
    bd = jnp.where(seg[:, None] == seg[None, :], 1.0 / A_DH, 0.0).astype(F32)
    prm = (norm1_g[l][None, :], norm2_g[l][None, :], w_main, w_g, w_g.T, b_gates[l][None, :], b_gates[l][:, None],
           mh_norm_g[l][None, :], jnp.tile(q_norm_g[l], A_HEADS)[None, :], jnp.tile(k_norm_g[l], A_KV)[None, :], bd,
           sink_logits[l], w_out[l][:M_WIDTH].astype(BF16), w_out[l][M_WIDTH:].astype(BF16),
           peer_w_q[l].astype(BF16), peer_sub_a[l].astype(BF16), peer_sub_b[l].astype(BF16),
           _pack_bf16_pairs(peer_u[l]), _pack_bf16_pairs(peer_v[l]))

    zeros_c = jnp.zeros((batch, 2, M_HEADS, M_DH, M_DH), F32)
    c0, m0 = _pack_state(zeros_c, zeros_c[..., 0], jnp.full((batch, 2, M_HEADS), NEG, F32))
    c0s, m0s = _pack_state(state_mlstm_C[:, l], state_mlstm_n[:, l], state_mlstm_m[:, l])
    past = cache_attn_k.shape[2]
    kc = cache_attn_k[:, l].reshape(dec_batch, past, KV_WIDTH)
    vc = cache_attn_v[:, l].reshape(dec_batch, past, KV_WIDTH)

    jobs = []
    for b0 in range(0, batch, batch // CTX_CHUNKS):
        b1 = b0 + batch // CTX_CHUNKS
        jobs.append((x_prompt[b0:b1], 0, (None, None, c0[b0:b1], m0[b0:b1]), False))
    for b0 in range(0, dec_batch, dec_batch // LATENT_CHUNKS):
        b1 = b0 + dec_batch // LATENT_CHUNKS
        jobs.append((x_sample[b0:b1], 1 + b0, (kc[b0:b1], vc[b0:b1], c0s[b0:b1], m0s[b0:b1]), True))
    outs = []
    for i, (x_c, mod_row0, cache, rotate) in enumerate(jobs):
        gate = tuple(g for g in (outs[i - 1][1] if i >= 1 else None,
                                 outs[i - EXPERT_LAG][2] if i >= EXPERT_LAG else None) if g is not None)
        outs.append(_run_chunk(x_c, mod3, mod_row0, prm, cache, rotate, gate or None))
    ctx, lat = outs[:CTX_CHUNKS], outs[CTX_CHUNKS:]
    y_p = jnp.concatenate([o[0] for o in ctx], axis=0)
    y_s = jnp.concatenate([o[0] for o in lat], axis=0)
    k_new = jnp.concatenate([o[3] for o in ctx], axis=0)
    v_new = jnp.concatenate([o[4] for o in ctx], axis=0)
    c_fin = jnp.concatenate([o[5] for o in ctx], axis=0)
    m_fin = jnp.concatenate([o[6] for o in ctx], axis=0)

    c_fin = c_fin.reshape(batch, 2, M_HEADS, M_DH, 2 * M_DH)
    new_c = c_fin[..., :M_DH][:, None]
    new_n = c_fin[..., M_DH][:, None]
    new_m = m_fin[:, :, 0, 0].reshape(batch, 2, M_HEADS)[:, None]
    new_k = k_new.reshape(batch, 1, seq, A_KV, A_DH)
    new_v = v_new.reshape(batch, 1, seq, A_KV, A_DH)
    return y_p, y_s, new_k, new_v, new_c, new_n, new_m
```

```python
import functools

import jax
import jax.numpy as jnp
from jax import lax
from jax.experimental import pallas as pl
from jax.experimental.pallas import tpu as pltpu
from jax.experimental.pallas import tpu_sc as plsc

F32 = jnp.float32
BF16 = jnp.bfloat16
I32 = jnp.int32
HI = lax.Precision.HIGHEST

D_MODEL = 1024
EPS = 1e-6
NEG = -1e30
GRID_W = 64
M_HEADS = 4
M_WIDTH = 512
M_DH = 128
A_HEADS = 8
A_KV = 2
A_REP = 4
A_DH = 64
A_WIDTH = 512
KV_WIDTH = A_KV * A_DH
BLOCK = 128
ROPE_BASE = 10000.0
N_KEYS = 128
P_HEADS = 8
P_DKEY = 256
P_HALF = 128
P_TOPK = 16
N_SEL = P_HEADS * P_TOPK
N_GATES = 4 * M_HEADS
MAIN_COLS = 4 * M_WIDTH + A_WIDTH + 2 * KV_WIDTH
MOD_ROWS = 16

TOKEN_TILE = 256
MLSTM_CHUNK = 128
ADA_COL_TILE = 768
CTX_CHUNKS = 4
EXPERT_LAG = 3
LATENT_CHUNKS = 8


def _sigmoid(x):
    return 1.0 / (1.0 + jnp.exp(-x))


def _log_sigmoid(x):
    return jnp.minimum(x, 0.0) - jnp.log1p(jnp.exp(-jnp.abs(x)))


def _dot_t(a, b, precision=None):
    return lax.dot_general(a, b, (((1,), (1,)), ((), ())), precision=precision,
                           preferred_element_type=F32)


def _ada_kernel(c_ref, w_ref, b_ref, o_ref):
    c = c_ref[...]
    s = c * _sigmoid(c)
    o_ref[...] = jnp.dot(s, w_ref[...], precision=HI, preferred_element_type=F32) + b_ref[...]


def _ada_call(cond, w_ada, b_ada):
    n_out = w_ada.shape[1]
    return pl.pallas_call(
        _ada_kernel,
        grid=(n_out // ADA_COL_TILE,),
        in_specs=[pl.BlockSpec((MOD_ROWS, D_MODEL), lambda j: (0, 0)),
                  pl.BlockSpec((D_MODEL, ADA_COL_TILE), lambda j: (0, j)),
                  pl.BlockSpec((1, ADA_COL_TILE), lambda j: (0, j))],
        out_specs=pl.BlockSpec((MOD_ROWS, ADA_COL_TILE), lambda j: (0, j)),
        out_shape=jax.ShapeDtypeStruct((MOD_ROWS, n_out), F32),
        name="ada",
    )(cond, w_ada, b_ada.reshape(1, n_out))


def _swap16(x):
    n = x.shape[-1]
    lane = lax.broadcasted_iota(I32, x.shape, x.ndim - 1)
    return jnp.where((lane & 16) == 0, pltpu.roll(x, n - 16, x.ndim - 1), pltpu.roll(x, 16, x.ndim - 1))


def _inproj_kernel(x_ref, mod_ref, n1_ref, w_ref, wg_ref, wgt_ref, bg_ref, bgt_ref, qg_ref, kg_ref,
                   bd_ref, cos_ref, sin_ref,
                   mq_ref, mk_ref, mv_ref, mo_ref, gc_ref, gr_ref, aq_ref, ak_ref, av_ref):
    x = x_ref[...]
    h = x * lax.rsqrt(jnp.mean(x * x, axis=-1, keepdims=True) + EPS) * n1_ref[...]
    h = h * (1.0 + mod_ref[0, 1:2, :]) + mod_ref[0, 0:1, :]
    z = jnp.dot(h.astype(BF16), w_ref[...], preferred_element_type=F32)

    mq_ref[...] = (z[:, 0:M_WIDTH] * (M_DH ** -0.5)).astype(BF16)
    mk_ref[...] = z[:, M_WIDTH:2 * M_WIDTH].astype(BF16)
    mv_ref[...] = z[:, 2 * M_WIDTH:3 * M_WIDTH].astype(BF16)
    mo_ref[...] = z[:, 3 * M_WIDTH:4 * M_WIDTH]

    g = jnp.dot(h, wg_ref[...], precision=HI, preferred_element_type=F32) + bg_ref[...]
    kind = lax.broadcasted_iota(I32, g.shape, 1) // M_HEADS
    gc_ref[...] = jnp.where((kind & 1) == 1, _log_sigmoid(g), g)
    gt = _dot_t(wgt_ref[...], h, precision=HI) + bgt_ref[...]
    kind_t = lax.broadcasted_iota(I32, gt.shape, 0) // M_HEADS
    gr_ref[...] = jnp.where((kind_t & 1) == 1, _log_sigmoid(gt), gt)

    o = 4 * M_WIDTH
    aq = z[:, o:o + A_WIDTH]
    ak = z[:, o + A_WIDTH:o + A_WIDTH + KV_WIDTH]
    av_ref[...] = z[:, o + A_WIDTH + KV_WIDTH:o + A_WIDTH + 2 * KV_WIDTH]
    bd = bd_ref[...]
    cos = cos_ref[...]
    sin = sin_ref[...]
    aq = aq * lax.rsqrt(jnp.dot(aq * aq, bd, precision=HI, preferred_element_type=F32) + EPS) * qg_ref[...]
    cos4 = jnp.concatenate([cos] * (A_WIDTH // KV_WIDTH), axis=1)
    sin4 = jnp.concatenate([sin] * (A_WIDTH // KV_WIDTH), axis=1)
    aq = (aq * cos4 + _swap16(aq) * sin4) * (A_DH ** -0.5)
    ak = ak * lax.rsqrt(jnp.dot(ak * ak, bd[0:KV_WIDTH, 0:KV_WIDTH], precision=HI,
                                preferred_element_type=F32) + EPS) * kg_ref[...]
    ak_ref[...] = ak * cos + _swap16(ak) * sin

    lane = lax.broadcasted_iota(I32, (aq.shape[0], KV_WIDTH), 1)
    for hd in range(A_HEADS):
        grp = hd // A_REP
        blk = aq[:, (hd // 2) * KV_WIDTH:(hd // 2 + 1) * KV_WIDTH]
        if hd % 2 != grp:
            blk = pltpu.roll(blk, A_DH, 1)
        keep = (lane >= grp * A_DH) & (lane < (grp + 1) * A_DH)
        aq_ref[hd] = jnp.where(keep, blk, 0.0).astype(BF16)


def _inproj_call(x2d, seq, mod3, mod_row0, n1, w_main, w_g, w_gt, b_g, b_gt, qg_t, kg_t, bd, cos, sin):
    n = x2d.shape[0]
    tm = TOKEN_TILE
    per_seq = seq // tm

    def tok(i):
        return (i, 0)

    def const2(i):
        return (0, 0)

    in_specs = [
        pl.BlockSpec((tm, D_MODEL), tok),
        pl.BlockSpec((1, 6, D_MODEL), lambda i: (mod_row0 + (i // per_seq if mod_row0 else 0), 0, 0)),
        pl.BlockSpec((1, D_MODEL), const2),
        pl.BlockSpec((D_MODEL, MAIN_COLS), const2),
        pl.BlockSpec((D_MODEL, N_GATES), const2),
        pl.BlockSpec((N_GATES, D_MODEL), const2),
        pl.BlockSpec((1, N_GATES), const2),
        pl.BlockSpec((N_GATES, 1), const2),
        pl.BlockSpec((1, A_WIDTH), const2),
        pl.BlockSpec((1, KV_WIDTH), const2),
        pl.BlockSpec((A_WIDTH, A_WIDTH), const2),
        pl.BlockSpec((tm, KV_WIDTH), lambda i: (i % per_seq, 0)),
        pl.BlockSpec((tm, KV_WIDTH), lambda i: (i % per_seq, 0)),
    ]
    out_specs = [
        pl.BlockSpec((tm, M_WIDTH), tok),
        pl.BlockSpec((tm, M_WIDTH), tok),
        pl.BlockSpec((tm, M_WIDTH), tok),
        pl.BlockSpec((tm, M_WIDTH), tok),
        pl.BlockSpec((tm, N_GATES), tok),
        pl.BlockSpec((N_GATES, tm), lambda i: (0, i)),
        pl.BlockSpec((A_HEADS, tm, KV_WIDTH), lambda i: (0, i, 0)),
        pl.BlockSpec((tm, KV_WIDTH), tok),
        pl.BlockSpec((tm, KV_WIDTH), tok),
    ]
    out_shape = [
        jax.ShapeDtypeStruct((n, M_WIDTH), BF16),
        jax.ShapeDtypeStruct((n, M_WIDTH), BF16),
        jax.ShapeDtypeStruct((n, M_WIDTH), BF16),
        jax.ShapeDtypeStruct((n, M_WIDTH), F32),
        jax.ShapeDtypeStruct((n, N_GATES), F32),
        jax.ShapeDtypeStruct((N_GATES, n), F32),
        jax.ShapeDtypeStruct((A_HEADS, n, KV_WIDTH), BF16),
        jax.ShapeDtypeStruct((n, KV_WIDTH), F32),
        jax.ShapeDtypeStruct((n, KV_WIDTH), F32),
    ]
    return pl.pallas_call(
        _inproj_kernel, grid=(n // tm,), in_specs=in_specs, out_specs=out_specs, out_shape=out_shape,
        compiler_params=pltpu.CompilerParams(dimension_semantics=("parallel",)),
        name="inproj",
    )(x2d, mod3, n1, w_main, w_g, w_gt, b_g, b_gt, qg_t, kg_t, bd, cos, sin)


def _mlstm_chain(q, k, v, li_c, lf_c, li_r, lf_r, caug, m, tri, tri_t, mask, reverse):
    L = q.shape[0]
    last = 0 if reverse else L - 1
    b_c = jnp.dot(tri, jnp.broadcast_to(lf_c, (L, L)), precision=HI, preferred_element_type=F32)
    b_r = jnp.dot(jnp.broadcast_to(lf_r, (8, L)), tri_t, precision=HI, preferred_element_type=F32)[0:1, :]
    a_inter = b_c[:, 0:1] + m
    d = jnp.where(mask, b_c - b_r + li_r, -jnp.inf)
    m_t = jnp.maximum(a_inter, jnp.max(d, axis=1, keepdims=True))
    w_inter = jnp.exp(a_inter - m_t)
    s = _dot_t(q, k) * jnp.exp(d - m_t)
    qc = jnp.dot(q, caug.astype(BF16), preferred_element_type=F32)
    num = jnp.dot(s.astype(BF16), v, preferred_element_type=F32) + w_inter * qc[:, 0:M_DH]
    den = jnp.sum(s, axis=1, keepdims=True) + w_inter * qc[:, M_DH:M_DH + 1]
    den = jnp.maximum(jnp.abs(den), jnp.exp(-m_t))
    h = num / den
    m_new = m_t[last:last + 1, :]
    b_last = b_c[last:last + 1, 0:1]
    g_c = jnp.exp(b_last - b_c[:, 0:1] + li_c - m_new)
    decay = jnp.exp(b_last + m - m_new)
    kw = (k.astype(F32) * g_c).astype(BF16)
    vaug = jnp.concatenate([v, jnp.ones_like(v)], axis=1)
    upd = lax.dot_general(kw, vaug, (((0,), (0,)), ((), ())), preferred_element_type=F32)
    return h, decay * caug + upd, m_new


def _mlstm_kernel(qf_ref, kf_ref, vf_ref, gcf_ref, grf_ref, qb_ref, kb_ref, vb_ref, gcb_ref, grb_ref,
                  c0_ref, m0_ref, hf_ref, hb_ref, cfin_ref, mfin_ref, c_scr, m_scr):
    c = pl.program_id(1)
    nc = pl.num_programs(1)
    L = qf_ref.shape[0]

    @pl.when(c == 0)
    def _():
        c_scr[...] = c0_ref[0]
        m_scr[...] = m0_ref[0]

    row = lax.broadcasted_iota(I32, (L, L), 0)
    col = lax.broadcasted_iota(I32, (L, L), 1)
    lower = row >= col
    upper = row <= col
    lower_f = lower.astype(F32)
    upper_f = upper.astype(F32)

    for direction in range(2):
        reverse = direction == 1
        q_ref, k_ref, v_ref, gc_ref, gr_ref, h_ref = (
            (qb_ref, kb_ref, vb_ref, gcb_ref, grb_ref, hb_ref) if reverse
            else (qf_ref, kf_ref, vf_ref, gcf_ref, grf_ref, hf_ref))
        tri, tri_t, mask = (upper_f, lower_f, upper) if reverse else (lower_f, upper_f, lower)
        gc = gc_ref[...]
        gr = gr_ref[...]
        for hd in range(M_HEADS):
            ch = direction * M_HEADS + hd
            sl = slice(hd * M_DH, (hd + 1) * M_DH)
            ci = 2 * direction * M_HEADS + hd
            cf = ci + M_HEADS
            h, caug, m_new = _mlstm_chain(
                q_ref[:, sl], k_ref[:, sl], v_ref[:, sl],
                gc[:, ci:ci + 1], gc[:, cf:cf + 1], gr[ci:ci + 1, :], gr[cf:cf + 1, :],
                c_scr[ch], m_scr[ch][0:1, 0:1], tri, tri_t, mask, reverse)
            h_ref[:, sl] = h
            c_scr[ch] = caug
            m_scr[ch] = jnp.broadcast_to(m_new, m_scr.shape[1:])

    @pl.when(c == nc - 1)
    def _():
        cfin_ref[0] = c_scr[...]
        mfin_ref[0] = m_scr[...]


def _mlstm_call(mq, mk, mv, gcol, grow, c0, m0, batch, seq):
    n = mq.shape[0]
    L = MLSTM_CHUNK
    nc = seq // L
    n_ch = 2 * M_HEADS

    def fwd(b, c):
        return (b * nc + c, 0)

    def bwd(b, c):
        return (b * nc + nc - 1 - c, 0)

    def fwd_t(b, c):
        return (0, b * nc + c)

    def bwd_t(b, c):
        return (0, b * nc + nc - 1 - c)

    tok = pl.BlockSpec((L, M_WIDTH), fwd)
    tok_b = pl.BlockSpec((L, M_WIDTH), bwd)
    in_specs = [tok, tok, tok, pl.BlockSpec((L, N_GATES), fwd), pl.BlockSpec((N_GATES, L), fwd_t),
                tok_b, tok_b, tok_b, pl.BlockSpec((L, N_GATES), bwd), pl.BlockSpec((N_GATES, L), bwd_t),
                pl.BlockSpec((1, n_ch, M_DH, 2 * M_DH), lambda b, c: (b, 0, 0, 0)),
                pl.BlockSpec((1, n_ch, 8, M_DH), lambda b, c: (b, 0, 0, 0))]
    out_specs = [tok, tok_b,
                 pl.BlockSpec((1, n_ch, M_DH, 2 * M_DH), lambda b, c: (b, 0, 0, 0)),
                 pl.BlockSpec((1, n_ch, 8, M_DH), lambda b, c: (b, 0, 0, 0))]
    out_shape = [jax.ShapeDtypeStruct((n, M_WIDTH), F32), jax.ShapeDtypeStruct((n, M_WIDTH), F32),
                 jax.ShapeDtypeStruct((batch, n_ch, M_DH, 2 * M_DH), F32),
                 jax.ShapeDtypeStruct((batch, n_ch, 8, M_DH), F32)]
    return pl.pallas_call(
        _mlstm_kernel, grid=(batch, nc), in_specs=in_specs, out_specs=out_specs, out_shape=out_shape,
        scratch_shapes=[pltpu.VMEM((n_ch, M_DH, 2 * M_DH), F32), pltpu.VMEM((n_ch, 8, M_DH), F32)],
        compiler_params=pltpu.CompilerParams(dimension_semantics=("parallel", "arbitrary")),
        name="mlstm",
    )(mq, mk, mv, gcol, grow, mq, mk, mv, gcol, grow, c0, m0)


def _sink_column(sink_ref, grp, rows_per_head):
    return jnp.concatenate(
        [jnp.full((rows_per_head, 1), sink_ref[grp * A_REP + r], F32) for r in range(A_REP)], axis=0)


def _store_heads(out_ref, o, grp, rows_per_head):
    for r in range(A_REP):
        hd = grp * A_REP + r
        out_ref[:, hd * A_DH:(hd + 1) * A_DH] = o[r * rows_per_head:(r + 1) * rows_per_head,
                                                  grp * A_DH:(grp + 1) * A_DH].astype(out_ref.dtype)


def _attn_ctx_kernel(sink_ref, q_ref, k_ref, v_ref, out_ref):
    s_len = k_ref.shape[0]
    k = k_ref[...].astype(BF16)
    v = v_ref[...].astype(BF16)
    for grp in range(A_KV):
        q = q_ref[grp * A_REP:(grp + 1) * A_REP].reshape(A_REP * s_len, KV_WIDTH)
        s = _dot_t(q, k)
        sk = _sink_column(sink_ref, grp, s_len)
        mx = jnp.maximum(jnp.max(s, axis=1, keepdims=True), sk)
        p = jnp.exp(s - mx)
        den = jnp.sum(p, axis=1, keepdims=True) + jnp.exp(sk - mx)
        o = jnp.dot(p.astype(BF16), v, preferred_element_type=F32) / den
        _store_heads(out_ref, o, grp, s_len)


def _attn_ctx_call(sink, aq, ak, av, batch, seq):
    n = ak.shape[0]
    return pl.pallas_call(
        _attn_ctx_kernel, grid=(batch,),
        in_specs=[pl.BlockSpec(memory_space=pltpu.SMEM),
                  pl.BlockSpec((A_HEADS, seq, KV_WIDTH), lambda b: (0, b, 0)),
                  pl.BlockSpec((seq, KV_WIDTH), lambda b: (b, 0)),
                  pl.BlockSpec((seq, KV_WIDTH), lambda b: (b, 0))],
        out_specs=pl.BlockSpec((seq, A_WIDTH), lambda b: (b, 0)),
        out_shape=jax.ShapeDtypeStruct((n, A_WIDTH), BF16),
        compiler_params=pltpu.CompilerParams(dimension_semantics=("parallel",)),
        name="attn_ctx",
    )(sink, aq, ak, av)


def _attn_lat_kernel(sink_ref, q_ref, kc_ref, vc_ref, kp_ref, kq_ref, kn_ref, vp_ref, vq_ref, vn_ref, out_ref):
    i = pl.program_id(1)
    nb = pl.num_programs(1)
    kc = kc_ref[0].astype(BF16)
    vc = vc_ref[0].astype(BF16)
    kp, kq, kn = kp_ref[...].astype(BF16), kq_ref[...].astype(BF16), kn_ref[...].astype(BF16)
    vp, vq, vn = vp_ref[...].astype(BF16), vq_ref[...].astype(BF16), vn_ref[...].astype(BF16)
    rows = A_REP * BLOCK
    qpos = lax.broadcasted_iota(I32, (rows, BLOCK), 0) % BLOCK
    kpos = lax.broadcasted_iota(I32, (rows, BLOCK), 1)
    mask_p = (kpos >= qpos) & (i > 0)
    mask_n = (kpos <= qpos) & (i < nb - 1)
    for grp in range(A_KV):
        q = q_ref[grp * A_REP:(grp + 1) * A_REP].reshape(rows, KV_WIDTH)
        s_c = _dot_t(q, kc)
        s_p = jnp.where(mask_p, _dot_t(q, kp), NEG)
        s_q = _dot_t(q, kq)
        s_n = jnp.where(mask_n, _dot_t(q, kn), NEG)
        sk = _sink_column(sink_ref, grp, BLOCK)
        mx = jnp.maximum(jnp.maximum(jnp.max(s_c, axis=1, keepdims=True), jnp.max(s_p, axis=1, keepdims=True)),
                         jnp.maximum(jnp.max(s_q, axis=1, keepdims=True), jnp.max(s_n, axis=1, keepdims=True)))
        mx = jnp.maximum(mx, sk)
        p_c, p_p, p_q, p_n = jnp.exp(s_c - mx), jnp.exp(s_p - mx), jnp.exp(s_q - mx), jnp.exp(s_n - mx)
        den = (jnp.sum(p_c, axis=1, keepdims=True) + jnp.sum(p_p, axis=1, keepdims=True)
               + jnp.sum(p_q, axis=1, keepdims=True) + jnp.sum(p_n, axis=1, keepdims=True) + jnp.exp(sk - mx))
        o = (jnp.dot(p_c.astype(BF16), vc, preferred_element_type=F32)
             + jnp.dot(p_p.astype(BF16), vp, preferred_element_type=F32)
             + jnp.dot(p_q.astype(BF16), vq, preferred_element_type=F32)
             + jnp.dot(p_n.astype(BF16), vn, preferred_element_type=F32)) / den
        _store_heads(out_ref, o, grp, BLOCK)


def _attn_lat_call(sink, aq, ak, av, kc, vc, batch, seq):
    n = ak.shape[0]
    nb = seq // BLOCK
    past = kc.shape[1]

    def cur(b, i):
        return (b * nb + i, 0)

    def prev(b, i):
        return (b * nb + jnp.maximum(i - 1, 0), 0)

    def nxt(b, i):
        return (b * nb + jnp.minimum(i + 1, nb - 1), 0)

    blk = functools.partial(pl.BlockSpec, (BLOCK, KV_WIDTH))
    cache = pl.BlockSpec((1, past, KV_WIDTH), lambda b, i: (b, 0, 0))
    return pl.pallas_call(
        _attn_lat_kernel, grid=(batch, nb),
        in_specs=[pl.BlockSpec(memory_space=pltpu.SMEM),
                  pl.BlockSpec((A_HEADS, BLOCK, KV_WIDTH), lambda b, i: (0, b * nb + i, 0)),
                  cache, cache, blk(prev), blk(cur), blk(nxt), blk(prev), blk(cur), blk(nxt)],
        out_specs=pl.BlockSpec((BLOCK, A_WIDTH), cur),
        out_shape=jax.ShapeDtypeStruct((n, A_WIDTH), BF16),
        compiler_params=pltpu.CompilerParams(dimension_semantics=("parallel", "parallel")),
        name="attn_lat",
    )(sink, aq, kc, vc, ak, ak, ak, av, av, av)


def _top16_rows(s, payload=None):
    n_rows = s.shape[0]
    rows = lax.broadcasted_iota(I32, s.shape, 0).astype(F32)
    vals, idxs, pays = [], [], []
    for _ in range(P_TOPK):
        mx = jnp.max(s, axis=0, keepdims=True)
        ix = jnp.min(jnp.where(s == mx, rows, float(n_rows)), axis=0, keepdims=True)
        hit = rows == ix
        vals.append(mx)
        idxs.append(ix)
        if payload is not None:
            pays.append(jnp.sum(jnp.where(hit, payload, 0.0), axis=0, keepdims=True))
        s = jnp.where(hit, -jnp.inf, s)
    out = (jnp.concatenate(vals, axis=0), jnp.concatenate(idxs, axis=0))
    if payload is not None:
        out += (jnp.concatenate(pays, axis=0),)
    return out


def _mix_kernel(x_ref, hf_ref, hb_ref, mo_ref, ao_ref, mod_ref, mhg_ref, n2_ref, wm_ref, wa_ref, wq_ref,
                sa_ref, sb_ref, x1_ref, h2_ref, eidx_ref, gate_ref, qp_scr, e_scr, g_scr):
    tm = x_ref.shape[0]
    hs = hf_ref[...] + hb_ref[...]
    parts = []
    for hd in range(M_HEADS):
        blk = hs[:, hd * M_DH:(hd + 1) * M_DH]
        parts.append(blk * lax.rsqrt(jnp.mean(blk * blk, axis=-1, keepdims=True) + EPS))
    m_out = _sigmoid(mo_ref[...]) * (jnp.concatenate(parts, axis=1) * mhg_ref[...])
    mix = (jnp.dot(m_out.astype(BF16), wm_ref[...], preferred_element_type=F32)
           + jnp.dot(ao_ref[...], wa_ref[...], preferred_element_type=F32))
    x1 = x_ref[...] + mod_ref[0, 2:3, :] * mix
    x1_ref[...] = x1
    h2 = x1 * lax.rsqrt(jnp.mean(x1 * x1, axis=-1, keepdims=True) + EPS) * n2_ref[...]
    h2 = h2 * (1.0 + mod_ref[0, 4:5, :]) + mod_ref[0, 3:4, :]
    h2_ref[...] = _pack_bf16_pairs(h2)
    qp = jnp.dot(h2.astype(BF16), wq_ref[...], preferred_element_type=F32)
    for p in range(P_HEADS):
        qp_scr[p] = qp[:, p * P_DKEY:(p + 1) * P_DKEY].astype(BF16)
    sub_a = sa_ref[...]
    sub_b = sb_ref[...]

    def head_body(p, carry):
        for half in range(tm // N_KEYS):
            cols = slice(half * N_KEYS, (half + 1) * N_KEYS)
            qh = qp_scr[p, pl.ds(half * N_KEYS, N_KEYS), :]
            s_a = _dot_t(sub_a, qh[:, 0:P_HALF])
            s_b = _dot_t(sub_b, qh[:, P_HALF:P_DKEY])
            va, ia = _top16_rows(s_a)
            vb, ib = _top16_rows(s_b)
            cand = jnp.concatenate([va[i:i + 1, :] + vb for i in range(P_TOPK)], axis=0)
            cidx = jnp.concatenate([ia[i:i + 1, :] * float(N_KEYS) + ib for i in range(P_TOPK)], axis=0)
            top, _, eidx = _top16_rows(cand, cidx)
            ex = jnp.exp(top - jnp.max(top, axis=0, keepdims=True))
            gates = ex / jnp.sum(ex, axis=0, keepdims=True)
            r0 = pl.multiple_of(p * P_TOPK, P_TOPK)
            e_scr[pl.ds(r0, P_TOPK), cols] = eidx
            g_scr[pl.ds(r0, P_TOPK), cols] = gates
        return carry

    lax.fori_loop(0, P_HEADS, head_body, 0)
    for half in range(tm // N_KEYS):
        cols = slice(half * N_KEYS, (half + 1) * N_KEYS)
        eidx_ref[cols, :] = e_scr[:, cols].T.astype(I32)
        gate_ref[cols, :] = g_scr[:, cols].T


def _mix_call(x2d, seq, h_f, h_b, mo, a_out, mod3, mod_row0, mhg, n2, w_m, w_a, w_q, sub_a, sub_b, tile0, n_tiles):
    tm = TOKEN_TILE
    per_seq = seq // tm
    n = n_tiles * tm

    def tok_in(i):
        return (tile0 + i, 0)

    def tok(i):
        return (i, 0)

    def const2(i):
        return (0, 0)

    in_specs = [
        pl.BlockSpec((tm, D_MODEL), tok_in),
        pl.BlockSpec((tm, M_WIDTH), tok_in), pl.BlockSpec((tm, M_WIDTH), tok_in),
        pl.BlockSpec((tm, M_WIDTH), tok_in),
        pl.BlockSpec((tm, A_WIDTH), tok_in),
        pl.BlockSpec((1, 6, D_MODEL), lambda i: (mod_row0 + ((tile0 + i) // per_seq if mod_row0 else 0), 0, 0)),
        pl.BlockSpec((1, M_WIDTH), const2),
        pl.BlockSpec((1, D_MODEL), const2),
        pl.BlockSpec((M_WIDTH, D_MODEL), const2),
        pl.BlockSpec((A_WIDTH, D_MODEL), const2),
        pl.BlockSpec((D_MODEL, P_HEADS * P_DKEY), const2),
        pl.BlockSpec((N_KEYS, P_HALF), const2),
        pl.BlockSpec((N_KEYS, P_HALF), const2),
    ]
    out_specs = [pl.BlockSpec((tm, D_MODEL), tok), pl.BlockSpec((tm, D_MODEL // 2), tok),
                 pl.BlockSpec((tm, N_SEL), tok), pl.BlockSpec((tm, N_SEL), tok)]
    out_shape = [jax.ShapeDtypeStruct((n, D_MODEL), F32), jax.ShapeDtypeStruct((n, D_MODEL // 2), I32),
                 jax.ShapeDtypeStruct((n, N_SEL), I32), jax.ShapeDtypeStruct((n, N_SEL), F32)]
    return pl.pallas_call(
        _mix_kernel, grid=(n_tiles,), in_specs=in_specs, out_specs=out_specs, out_shape=out_shape,
        scratch_shapes=[pltpu.VMEM((P_HEADS, tm, P_DKEY), BF16), pltpu.VMEM((N_SEL, tm), F32),
                        pltpu.VMEM((N_SEL, tm), F32)],
        compiler_params=pltpu.CompilerParams(dimension_semantics=("parallel",)),
        name="mix",
    )(x2d, h_f, h_b, mo, a_out, mod3, mhg, n2, w_m, w_a, w_q, sub_a, sub_b)


SC_LANES = 16
SC_CORES = 2
SC_SUBCORES = 16
SC_WORKERS = SC_CORES * SC_SUBCORES
SC_TOKENS = 8
SC_GROUP = SC_LANES
SC_NGROUPS = N_SEL // SC_GROUP
SC_SLOTS = 4
ROW_WORDS = D_MODEL // 2
SC_DOT_ROWS = 8
SC_DOT_PARTIALS = 2
SC_OWORDS = 16 * SC_LANES
HI_MASK = -65536
GELU_C0 = 0.7978845608028654
GELU_C1 = 0.044715


def _pack_bf16_pairs(x):
    half = x.shape[1] // 2
    bits = lax.bitcast_convert_type(x.astype(BF16).astype(F32), I32)
    return (bits[:, :half] & HI_MASK) | lax.shift_right_logical(bits[:, half:], jnp.int32(16))


def _sc_gelu(a):
    z = GELU_C0 * (a + GELU_C1 * (a * a * a))
    tanh = 1.0 - 2.0 / (jnp.exp(2.0 * z) + 1.0)
    return 0.5 * a * (1.0 + tanh)


def _sc_split(words):
    return (plsc.bitcast(words & HI_MASK, F32), plsc.bitcast(lax.shift_left(words, jnp.int32(16)), F32))


def _sc_mul_bf16(a_words, b_words):
    return plsc.bitcast(a_words, BF16) * plsc.bitcast(b_words, BF16)


def _sc_split_sum(p, q):
    return _sc_split(plsc.bitcast(p + q, I32))


def _peer_sc_kernel(h2_hbm, eidx_hbm, gate_hbm, u_hbm, v_hbm, out_hbm,
                    xbuf, ibuf, gbuf, obuf, ubuf, vbuf, mbuf, wbuf, sems_u, sems_v):
    n = h2_hbm.shape[0]
    per_worker = n // SC_WORKERS
    wid = lax.axis_index("c") * SC_SUBCORES + lax.axis_index("s")
    lane = lax.iota(I32, SC_LANES)

    def split_item(item):
        return lax.shift_right_logical(item, SC_NGROUPS.bit_length() - 1), item & (SC_NGROUPS - 1)

    def gather_copies(item, slot):
        t, g = split_item(item)
        idx = ibuf[t, pl.ds(g * SC_GROUP, SC_GROUP)]
        return (pltpu.make_async_copy(u_hbm.at[idx], ubuf.at[slot], sems_u.at[slot]),
                pltpu.make_async_copy(v_hbm.at[idx], vbuf.at[slot], sems_v.at[slot]))

    def dots(t, slot):
        zero = jnp.zeros((SC_LANES,), F32)

        @pl.loop(0, SC_GROUP, step=SC_DOT_ROWS)
        def _(r0):
            accs = [[zero] * SC_DOT_PARTIALS for _ in range(SC_DOT_ROWS)]
            for k in range(0, ROW_WORDS // SC_LANES, 4):
                xs = [xbuf[t, pl.ds((k + q) * SC_LANES, SC_LANES)] for q in range(4)]
                for i in range(SC_DOT_ROWS):
                    m = [_sc_mul_bf16(xs[q], ubuf[slot, r0 + i, pl.ds((k + q) * SC_LANES, SC_LANES)])
                         for q in range(4)]
                    hi, lo = _sc_split_sum(m[0] + m[1], m[2] + m[3])
                    p = (k // 4) % SC_DOT_PARTIALS
                    accs[i][p] = accs[i][p] + (hi + lo)
            for i in range(SC_DOT_ROWS):
                mbuf[r0 + i, :] = functools.reduce(lambda a, b: a + b, accs[i])

        tot = zero
        for c in range(SC_LANES):
            tot = tot + plsc.load_gather(mbuf, [lane, jnp.full((SC_LANES,), c, I32)])
        return tot

    def accumulate(t, slot):
        nv = SC_OWORDS // SC_LANES
        for oc in range(ROW_WORDS // SC_OWORDS):
            w0 = oc * SC_OWORDS
            accs = (tuple(obuf[t, pl.ds(w0 + j * SC_LANES, SC_LANES)] for j in range(nv))
                    + tuple(obuf[t, pl.ds(ROW_WORDS + w0 + j * SC_LANES, SC_LANES)] for j in range(nv)))

            def row_quad(rq, accs):
                r = 4 * rq
                ws = [plsc.load_gather(wbuf, [jnp.full((SC_LANES,), r + q, I32)]) for q in range(4)]
                his, los = [], []
                for j in range(nv):
                    m = [_sc_mul_bf16(ws[q], vbuf[slot, r + q, pl.ds(w0 + j * SC_LANES, SC_LANES)])
                         for q in range(4)]
                    hi, lo = _sc_split_sum(m[0] + m[1], m[2] + m[3])
                    his.append(accs[j] + hi)
                    los.append(accs[nv + j] + lo)
                return tuple(his) + tuple(los)

            accs = lax.fori_loop(0, SC_GROUP // 4, row_quad, accs)
            for j in range(nv):
                obuf[t, pl.ds(w0 + j * SC_LANES, SC_LANES)] = accs[j]
                obuf[t, pl.ds(ROW_WORDS + w0 + j * SC_LANES, SC_LANES)] = accs[nv + j]

    def pack_weights(w):
        bits = plsc.bitcast(w, I32)
        rounded = (bits + 0x7FFF + (lax.shift_right_logical(bits, jnp.int32(16)) & 1)) & HI_MASK
        return rounded | lax.shift_right_logical(rounded, jnp.int32(16))

    n_items = SC_TOKENS * SC_NGROUPS

    @pl.loop(0, per_worker // SC_TOKENS)
    def _(blk):
        tok0 = pl.multiple_of(wid * per_worker + blk * SC_TOKENS, SC_TOKENS)
        pltpu.sync_copy(h2_hbm.at[pl.ds(tok0, SC_TOKENS)], xbuf)
        pltpu.sync_copy(eidx_hbm.at[pl.ds(tok0, SC_TOKENS)], ibuf)
        pltpu.sync_copy(gate_hbm.at[pl.ds(tok0, SC_TOKENS)], gbuf)

        @pl.loop(0, SC_TOKENS)
        def _(t):
            zero = jnp.zeros((SC_LANES,), F32)
            for j in range(D_MODEL // SC_LANES):
                obuf[t, pl.ds(j * SC_LANES, SC_LANES)] = zero

        for ahead in range(SC_SLOTS - 1):
            for c in gather_copies(ahead, ahead):
                c.start()

        @pl.loop(0, n_items)
        def _(item):
            t, g = split_item(item)
            slot = item & (SC_SLOTS - 1)
            ahead = item + (SC_SLOTS - 1)

            @pl.when(ahead < n_items)
            def _():
                for c in gather_copies(ahead, ahead & (SC_SLOTS - 1)):
                    c.start()

            for c in gather_copies(item, slot):
                c.wait()
            a = dots(t, slot)
            wbuf[...] = pack_weights(gbuf[t, pl.ds(g * SC_GROUP, SC_GROUP)] * _sc_gelu(a))
            accumulate(t, slot)

        pltpu.sync_copy(obuf, out_hbm.at[pl.ds(tok0, SC_TOKENS)])


def _peer_experts(h2p, eidx, gates, u_pack, v_pack):
    n = h2p.shape[0]
    mesh = plsc.VectorSubcoreMesh(core_axis_name="c", subcore_axis_name="s")
    fn = pl.kernel(
        _peer_sc_kernel,
        out_type=jax.ShapeDtypeStruct((n, D_MODEL), F32),
        mesh=mesh,
        scratch_types=[
            pltpu.VMEM((SC_TOKENS, ROW_WORDS), I32),
            pltpu.VMEM((SC_TOKENS, N_SEL), I32),
            pltpu.VMEM((SC_TOKENS, N_SEL), F32),
            pltpu.VMEM((SC_TOKENS, D_MODEL), F32),
            pltpu.VMEM((SC_SLOTS, SC_GROUP, ROW_WORDS), I32),
            pltpu.VMEM((SC_SLOTS, SC_GROUP, ROW_WORDS), I32),
            pltpu.VMEM((SC_GROUP, SC_LANES), F32),
            pltpu.VMEM((SC_LANES,), I32),
            pltpu.SemaphoreType.DMA((SC_SLOTS,)), pltpu.SemaphoreType.DMA((SC_SLOTS,)),
        ],
        compiler_params=pltpu.CompilerParams(needs_layout_passes=False),
        cost_estimate=pl.CostEstimate(
            flops=4 * n * N_SEL * D_MODEL, transcendentals=n * N_SEL,
            bytes_accessed=4 * (2 * n * N_SEL * ROW_WORDS + n * ROW_WORDS + n * D_MODEL + 2 * n * N_SEL)),
        name="peer_experts",
    )
    return fn(h2p, eidx, gates, u_pack, v_pack)


def _resid_kernel(x1_ref, p_ref, mod_ref, o_ref):
    o_ref[...] = x1_ref[...] + mod_ref[0, 5:6, :] * p_ref[...]


def _resid_call(x1, peer_out, seq, mod3, mod_row0, tile0):
    n = x1.shape[0]
    tm = TOKEN_TILE
    per_seq = seq // tm
    tok = pl.BlockSpec((tm, D_MODEL), lambda i: (i, 0))
    return pl.pallas_call(
        _resid_kernel, grid=(n // tm,),
        in_specs=[tok, tok,
                  pl.BlockSpec((1, 6, D_MODEL),
                               lambda i: (mod_row0 + ((tile0 + i) // per_seq if mod_row0 else 0), 0, 0))],
        out_specs=tok, out_shape=jax.ShapeDtypeStruct((n, D_MODEL), F32),
        compiler_params=pltpu.CompilerParams(dimension_semantics=("parallel",)),
        name="resid",
    )(x1, peer_out, mod3)


def _rope_tables(seq, rotate):
    if not rotate:
        return jnp.ones((seq, KV_WIDTH), F32), jnp.zeros((seq, KV_WIDTH), F32)
    quarter = A_DH // 4
    t = jnp.arange(seq)
    row = (t // GRID_W).astype(F32)
    col = (t % GRID_W).astype(F32)
    inv = ROPE_BASE ** (-jnp.arange(quarter, dtype=F32) / quarter)
    d = jnp.arange(A_DH)
    pos = jnp.where(d[None, :] < A_DH // 2, row[:, None], col[:, None])
    ang = pos * inv[d % quarter][None, :]
    sign = jnp.where((d % (A_DH // 2)) < quarter, -1.0, 1.0).astype(F32)
    cos = jnp.cos(ang)
    sin = jnp.sin(ang) * sign[None, :]
    return jnp.tile(cos, (1, KV_WIDTH // A_DH)), jnp.tile(sin, (1, KV_WIDTH // A_DH))


def _run_chunk(x, mod3, mod_row0, prm, cache, rotate, gate_on):
    (n1, n2, w_main, w_g, w_gt, b_g, b_gt, mhg, qg_t, kg_t, bd, sink, w_m, w_a, w_q, sub_a, sub_b,
     u_pack, v_pack) = prm
    batch, seq, _ = x.shape
    n = batch * seq
    x2d = x.reshape(n, D_MODEL)
    if gate_on is not None:
        x2d, _ = lax.optimization_barrier((x2d, gate_on))
    cos, sin = _rope_tables(seq, rotate)
    mq, mk, mv, mo, gcol, grow, aq, ak, av = _inproj_call(
        x2d, seq, mod3, mod_row0, n1, w_main, w_g, w_gt, b_g, b_gt, qg_t, kg_t, bd, cos, sin)
    kc, vc, c0, m0 = cache
    h_f, h_b, c_fin, m_fin = _mlstm_call(mq, mk, mv, gcol, grow, c0, m0, batch, seq)
    if kc is None:
        a_out = _attn_ctx_call(sink, aq, ak, av, batch, seq)
    else:
        a_out = _attn_lat_call(sink, aq, ak, av, kc, vc, batch, seq)
    x1, h2p, eidx, gates = _mix_call(x2d, seq, h_f, h_b, mo, a_out, mod3, mod_row0, mhg, n2, w_m, w_a, w_q,
                                     sub_a, sub_b, 0, n // TOKEN_TILE)
    peer_out = _peer_experts(h2p, eidx, gates, u_pack, v_pack)
    y = _resid_call(x1, peer_out, seq, mod3, mod_row0, 0).reshape(batch, seq, D_MODEL)
    return y, h2p, peer_out, ak, av, c_fin, m_fin


def _pack_state(C, n_vec, m):
    b = C.shape[0]
    caug = jnp.concatenate([C, jnp.broadcast_to(n_vec[..., None], C.shape)], axis=-1)
    caug = caug.reshape(b, 2 * M_HEADS, M_DH, 2 * M_DH)
    m_rep = jnp.broadcast_to(m.reshape(b, 2 * M_HEADS, 1, 1), (b, 2 * M_HEADS, 8, M_DH))
    return caug.astype(F32), m_rep.astype(F32)


def kernel(x_prompt, x_sample, c, cache_attn_k, cache_attn_v, state_mlstm_C, state_mlstm_n, state_mlstm_m,
           c_ctx, w_ada, b_ada, norm1_g, norm2_g, w_in, b_gates, mh_norm_g, q_norm_g, k_norm_g, sink_logits,
           w_out, peer_w_q, peer_sub_a, peer_sub_b, peer_u, peer_v):
    depth = w_ada.shape[0]
    assert depth == 1
    batch, seq, _ = x_prompt.shape
    dec_batch, dec_seq, _ = x_sample.shape
    l = 0

    cond = jnp.concatenate([c_ctx[None, :], c, jnp.zeros((MOD_ROWS - 1 - dec_batch, D_MODEL), F32)], axis=0)
    mod3 = _ada_call(cond, w_ada[l], b_ada[l]).reshape(MOD_ROWS, 6, D_MODEL)

    wi = w_in[l]
    g0 = 4 * M_WIDTH
    w_main = jnp.concatenate([wi[:, :g0], wi[:, g0 + N_GATES:]], axis=1).astype(BF16)
    w_g = wi[:, g0:g0 + N_GATES]
    seg = jnp.arange(A_WIDTH) // A_DH
    bd = jnp.where(seg[:, None] == seg[None, :], 1.0 / A_DH, 0.0).astype(F32)
    prm = (norm1_g[l][None, :], norm2_g[l][None, :], w_main, w_g, w_g.T, b_gates[l][None, :], b_gates[l][:, None],
           mh_norm_g[l][None, :], jnp.tile(q_norm_g[l], A_HEADS)[None, :], jnp.tile(k_norm_g[l], A_KV)[None, :], bd,
           sink_logits[l], w_out[l][:M_WIDTH].astype(BF16), w_out[l][M_WIDTH:].astype(BF16),
           peer_w_q[l].astype(BF16), peer_sub_a[l].astype(BF16), peer_sub_b[l].astype(BF16),
           _pack_bf16_pairs(peer_u[l]), _pack_bf16_pairs(peer_v[l]))

    zeros_c = jnp.zeros((batch, 2, M_HEADS, M_DH, M_DH), F32)
    c0, m0 = _pack_state(zeros_c, zeros_c[..., 0], jnp.full((batch, 2, M_HEADS), NEG, F32))
    c0s, m0s = _pack_state(state_mlstm_C[:, l], state_mlstm_n[:, l], state_mlstm_m[:, l])
    past = cache_attn_k.shape[2]
    kc = cache_attn_k[:, l].reshape(dec_batch, past, KV_WIDTH)
    vc = cache_attn_v[:, l].reshape(dec_batch, past, KV_WIDTH)

    jobs = []
    for b0 in range(0, batch, batch // CTX_CHUNKS):
        b1 = b0 + batch // CTX_CHUNKS
        jobs.append((x_prompt[b0:b1], 0, (None, None, c0[b0:b1], m0[b0:b1]), False))
    for b0 in range(0, dec_batch, dec_batch // LATENT_CHUNKS):
        b1 = b0 + dec_batch // LATENT_CHUNKS
        jobs.append((x_sample[b0:b1], 1 + b0, (kc[b0:b1], vc[b0:b1], c0s[b0:b1], m0s[b0:b1]), True))
    outs = []
    for i, (x_c, mod_row0, cache, rotate) in enumerate(jobs):
        gate = tuple(g for g in (outs[i - 1][1] if i >= 1 else None,
                                 outs[i - EXPERT_LAG][2] if i >= EXPERT_LAG else None) if g is not None)
        outs.append(_run_chunk(x_c, mod3, mod_row0, prm, cache, rotate, gate or None))
    ctx, lat = outs[:CTX_CHUNKS], outs[CTX_CHUNKS:]
    y_p = jnp.concatenate([o[0] for o in ctx], axis=0)
    y_s = jnp.concatenate([o[0] for o in lat], axis=0)
    k_new = jnp.concatenate([o[3] for o in ctx], axis=0)
    v_new = jnp.concatenate([o[4] for o in ctx], axis=0)
    c_fin = jnp.concatenate([o[5] for o in ctx], axis=0)
    m_fin = jnp.concatenate([o[6] for o in ctx], axis=0)

    c_fin = c_fin.reshape(batch, 2, M_HEADS, M_DH, 2 * M_DH)
    new_c = c_fin[..., :M_DH][:, None]
    new_n = c_fin[..., M_DH][:, None]
    new_m = m_fin[:, :, 0, 0].reshape(batch, 2, M_HEADS)[:, None]
    new_k = k_new.reshape(batch, 1, seq, A_KV, A_DH)
    new_v = v_new.reshape(batch, 1, seq, A_KV, A_DH)
    return y_p, y_s, new_k, new_v, new_c, new_n, new_m
```

```python
import functools

import jax
import jax.numpy as jnp
from jax import lax
from jax.experimental import pallas as pl
from jax.experimental.pallas import tpu as pltpu
from jax.experimental.pallas import tpu_sc as plsc

F32 = jnp.float32
BF16 = jnp.bfloat16
I32 = jnp.int32
HI = lax.Precision.HIGHEST

D_MODEL = 1024
EPS = 1e-6
NEG = -1e30
GRID_W = 64
M_HEADS = 4
M_WIDTH = 512
M_DH = 128
A_HEADS = 8
A_KV = 2
A_REP = 4
A_DH = 64
A_WIDTH = 512
KV_WIDTH = A_KV * A_DH
BLOCK = 128
ROPE_BASE = 10000.0
N_KEYS = 128
P_HEADS = 8
P_DKEY = 256
P_HALF = 128
P_TOPK = 16
N_SEL = P_HEADS * P_TOPK
N_GATES = 4 * M_HEADS
MAIN_COLS = 4 * M_WIDTH + A_WIDTH + 2 * KV_WIDTH
MOD_ROWS = 16

TOKEN_TILE = 256
MLSTM_CHUNK = 128
ADA_COL_TILE = 768
CTX_CHUNKS = 4
EXPERT_LAG = 3
LATENT_CHUNKS = 8


def _sigmoid(x):
    return 1.0 / (1.0 + jnp.exp(-x))


def _log_sigmoid(x):
    return jnp.minimum(x, 0.0) - jnp.log1p(jnp.exp(-jnp.abs(x)))


def _dot_t(a, b, precision=None):
    return lax.dot_general(a, b, (((1,), (1,)), ((), ())), precision=precision,
                           preferred_element_type=F32)


def _ada_kernel(c_ref, w_ref, b_ref, o_ref):
    c = c_ref[...]
    s = c * _sigmoid(c)
    o_ref[...] = jnp.dot(s, w_ref[...], precision=HI, preferred_element_type=F32) + b_ref[...]


def _ada_call(cond, w_ada, b_ada):
    n_out = w_ada.shape[1]
    return pl.pallas_call(
        _ada_kernel,
        grid=(n_out // ADA_COL_TILE,),
        in_specs=[pl.BlockSpec((MOD_ROWS, D_MODEL), lambda j: (0, 0)),
                  pl.BlockSpec((D_MODEL, ADA_COL_TILE), lambda j: (0, j)),
                  pl.BlockSpec((1, ADA_COL_TILE), lambda j: (0, j))],
        out_specs=pl.BlockSpec((MOD_ROWS, ADA_COL_TILE), lambda j: (0, j)),
        out_shape=jax.ShapeDtypeStruct((MOD_ROWS, n_out), F32),
        name="ada",
    )(cond, w_ada, b_ada.reshape(1, n_out))


def _swap16(x):
    n = x.shape[-1]
    lane = lax.broadcasted_iota(I32, x.shape, x.ndim - 1)
    return jnp.where((lane & 16) == 0, pltpu.roll(x, n - 16, x.ndim - 1), pltpu.roll(x, 16, x.ndim - 1))


def _inproj_kernel(x_ref, mod_ref, n1_ref, w_ref, wg_ref, wgt_ref, bg_ref, bgt_ref, qg_ref, kg_ref,
                   bd_ref, cos_ref, sin_ref,
                   mq_ref, mk_ref, mv_ref, mo_ref, gc_ref, gr_ref, aq_ref, ak_ref, av_ref):
    x = x_ref[...]
    h = x * lax.rsqrt(jnp.mean(x * x, axis=-1, keepdims=True) + EPS) * n1_ref[...]
    h = h * (1.0 + mod_ref[0, 1:2, :]) + mod_ref[0, 0:1, :]
    z = jnp.dot(h.astype(BF16), w_ref[...], preferred_element_type=F32)

    mq_ref[...] = (z[:, 0:M_WIDTH] * (M_DH ** -0.5)).astype(BF16)
    mk_ref[...] = z[:, M_WIDTH:2 * M_WIDTH].astype(BF16)
    mv_ref[...] = z[:, 2 * M_WIDTH:3 * M_WIDTH].astype(BF16)
    mo_ref[...] = z[:, 3 * M_WIDTH:4 * M_WIDTH]

    g = jnp.dot(h, wg_ref[...], precision=HI, preferred_element_type=F32) + bg_ref[...]
    kind = lax.broadcasted_iota(I32, g.shape, 1) // M_HEADS
    gc_ref[...] = jnp.where((kind & 1) == 1, _log_sigmoid(g), g)
    gt = _dot_t(wgt_ref[...], h, precision=HI) + bgt_ref[...]
    kind_t = lax.broadcasted_iota(I32, gt.shape, 0) // M_HEADS
    gr_ref[...] = jnp.where((kind_t & 1) == 1, _log_sigmoid(gt), gt)

    o = 4 * M_WIDTH
    aq = z[:, o:o + A_WIDTH]
    ak = z[:, o + A_WIDTH:o + A_WIDTH + KV_WIDTH]
    av_ref[...] = z[:, o + A_WIDTH + KV_WIDTH:o + A_WIDTH + 2 * KV_WIDTH]
    bd = bd_ref[...]
    cos = cos_ref[...]
    sin = sin_ref[...]
    aq = aq * lax.rsqrt(jnp.dot(aq * aq, bd, precision=HI, preferred_element_type=F32) + EPS) * qg_ref[...]
    cos4 = jnp.concatenate([cos] * (A_WIDTH // KV_WIDTH), axis=1)
    sin4 = jnp.concatenate([sin] * (A_WIDTH // KV_WIDTH), axis=1)
    aq = (aq * cos4 + _swap16(aq) * sin4) * (A_DH ** -0.5)
    ak = ak * lax.rsqrt(jnp.dot(ak * ak, bd[0:KV_WIDTH, 0:KV_WIDTH], precision=HI,
                                preferred_element_type=F32) + EPS) * kg_ref[...]
    ak_ref[...] = ak * cos + _swap16(ak) * sin

    lane = lax.broadcasted_iota(I32, (aq.shape[0], KV_WIDTH), 1)
    for hd in range(A_HEADS):
        grp = hd // A_REP
        blk = aq[:, (hd // 2) * KV_WIDTH:(hd // 2 + 1) * KV_WIDTH]
        if hd % 2 != grp:
            blk = pltpu.roll(blk, A_DH, 1)
        keep = (lane >= grp * A_DH) & (lane < (grp + 1) * A_DH)
        aq_ref[hd] = jnp.where(keep, blk, 0.0).astype(BF16)


def _inproj_call(x2d, seq, mod3, mod_row0, n1, w_main, w_g, w_gt, b_g, b_gt, qg_t, kg_t, bd, cos, sin):
    n = x2d.shape[0]
    tm = TOKEN_TILE
    per_seq = seq // tm

    def tok(i):
        return (i, 0)

    def const2(i):
        return (0, 0)

    in_specs = [
        pl.BlockSpec((tm, D_MODEL), tok),
        pl.BlockSpec((1, 6, D_MODEL), lambda i: (mod_row0 + (i // per_seq if mod_row0 else 0), 0, 0)),
        pl.BlockSpec((1, D_MODEL), const2),
        pl.BlockSpec((D_MODEL, MAIN_COLS), const2),
        pl.BlockSpec((D_MODEL, N_GATES), const2),
        pl.BlockSpec((N_GATES, D_MODEL), const2),
        pl.BlockSpec((1, N_GATES), const2),
        pl.BlockSpec((N_GATES, 1), const2),
        pl.BlockSpec((1, A_WIDTH), const2),
        pl.BlockSpec((1, KV_WIDTH), const2),
        pl.BlockSpec((A_WIDTH, A_WIDTH), const2),
        pl.BlockSpec((tm, KV_WIDTH), lambda i: (i % per_seq, 0)),
        pl.BlockSpec((tm, KV_WIDTH), lambda i: (i % per_seq, 0)),
    ]
    out_specs = [
        pl.BlockSpec((tm, M_WIDTH), tok),
        pl.BlockSpec((tm, M_WIDTH), tok),
        pl.BlockSpec((tm, M_WIDTH), tok),
        pl.BlockSpec((tm, M_WIDTH), tok),
        pl.BlockSpec((tm, N_GATES), tok),
        pl.BlockSpec((N_GATES, tm), lambda i: (0, i)),
        pl.BlockSpec((A_HEADS, tm, KV_WIDTH), lambda i: (0, i, 0)),
        pl.BlockSpec((tm, KV_WIDTH), tok),
        pl.BlockSpec((tm, KV_WIDTH), tok),
    ]
    out_shape = [
        jax.ShapeDtypeStruct((n, M_WIDTH), BF16),
        jax.ShapeDtypeStruct((n, M_WIDTH), BF16),
        jax.ShapeDtypeStruct((n, M_WIDTH), BF16),
        jax.ShapeDtypeStruct((n, M_WIDTH), F32),
        jax.ShapeDtypeStruct((n, N_GATES), F32),
        jax.ShapeDtypeStruct((N_GATES, n), F32),
        jax.ShapeDtypeStruct((A_HEADS, n, KV_WIDTH), BF16),
        jax.ShapeDtypeStruct((n, KV_WIDTH), F32),
        jax.ShapeDtypeStruct((n, KV_WIDTH), F32),
    ]
    return pl.pallas_call(
        _inproj_kernel, grid=(n // tm,), in_specs=in_specs, out_specs=out_specs, out_shape=out_shape,
        compiler_params=pltpu.CompilerParams(dimension_semantics=("parallel",)),
        name="inproj",
    )(x2d, mod3, n1, w_main, w_g, w_gt, b_g, b_gt, qg_t, kg_t, bd, cos, sin)


def _mlstm_chain(q, k, v, li_c, lf_c, li_r, lf_r, caug, m, tri, tri_t, mask, reverse):
    L = q.shape[0]
    last = 0 if reverse else L - 1
    b_c = jnp.dot(tri, jnp.broadcast_to(lf_c, (L, L)), precision=HI, preferred_element_type=F32)
    b_r = jnp.dot(jnp.broadcast_to(lf_r, (8, L)), tri_t, precision=HI, preferred_element_type=F32)[0:1, :]
    a_inter = b_c[:, 0:1] + m
    d = jnp.where(mask, b_c - b_r + li_r, -jnp.inf)
    m_t = jnp.maximum(a_inter, jnp.max(d, axis=1, keepdims=True))
    w_inter = jnp.exp(a_inter - m_t)
    s = _dot_t(q, k) * jnp.exp(d - m_t)
    qc = jnp.dot(q, caug.astype(BF16), preferred_element_type=F32)
    num = jnp.dot(s.astype(BF16), v, preferred_element_type=F32) + w_inter * qc[:, 0:M_DH]
    den = jnp.sum(s, axis=1, keepdims=True) + w_inter * qc[:, M_DH:M_DH + 1]
    den = jnp.maximum(jnp.abs(den), jnp.exp(-m_t))
    h = num / den
    m_new = m_t[last:last + 1, :]
    b_last = b_c[last:last + 1, 0:1]
    g_c = jnp.exp(b_last - b_c[:, 0:1] + li_c - m_new)
    decay = jnp.exp(b_last + m - m_new)
    kw = (k.astype(F32) * g_c).astype(BF16)
    vaug = jnp.concatenate([v, jnp.ones_like(v)], axis=1)
    upd = lax.dot_general(kw, vaug, (((0,), (0,)), ((), ())), preferred_element_type=F32)
    return h, decay * caug + upd, m_new


def _mlstm_kernel(qf_ref, kf_ref, vf_ref, gcf_ref, grf_ref, qb_ref, kb_ref, vb_ref, gcb_ref, grb_ref,
                  c0_ref, m0_ref, hf_ref, hb_ref, cfin_ref, mfin_ref, c_scr, m_scr):
    c = pl.program_id(1)
    nc = pl.num_programs(1)
    L = qf_ref.shape[0]

    @pl.when(c == 0)
    def _():
        c_scr[...] = c0_ref[0]
        m_scr[...] = m0_ref[0]

    row = lax.broadcasted_iota(I32, (L, L), 0)
    col = lax.broadcasted_iota(I32, (L, L), 1)
    lower = row >= col
    upper = row <= col
    lower_f = lower.astype(F32)
    upper_f = upper.astype(F32)

    for direction in range(2):
        reverse = direction == 1
        q_ref, k_ref, v_ref, gc_ref, gr_ref, h_ref = (
            (qb_ref, kb_ref, vb_ref, gcb_ref, grb_ref, hb_ref) if reverse
            else (qf_ref, kf_ref, vf_ref, gcf_ref, grf_ref, hf_ref))
        tri, tri_t, mask = (upper_f, lower_f, upper) if reverse else (lower_f, upper_f, lower)
        gc = gc_ref[...]
        gr = gr_ref[...]
        for hd in range(M_HEADS):
            ch = direction * M_HEADS + hd
            sl = slice(hd * M_DH, (hd + 1) * M_DH)
            ci = 2 * direction * M_HEADS + hd
            cf = ci + M_HEADS
            h, caug, m_new = _mlstm_chain(
                q_ref[:, sl], k_ref[:, sl], v_ref[:, sl],
                gc[:, ci:ci + 1], gc[:, cf:cf + 1], gr[ci:ci + 1, :], gr[cf:cf + 1, :],
                c_scr[ch], m_scr[ch][0:1, 0:1], tri, tri_t, mask, reverse)
            h_ref[:, sl] = h
            c_scr[ch] = caug
            m_scr[ch] = jnp.broadcast_to(m_new, m_scr.shape[1:])

    @pl.when(c == nc - 1)
    def _():
        cfin_ref[0] = c_scr[...]
        mfin_ref[0] = m_scr[...]


def _mlstm_call(mq, mk, mv, gcol, grow, c0, m0, batch, seq):
    n = mq.shape[0]
    L = MLSTM_CHUNK
    nc = seq // L
    n_ch = 2 * M_HEADS

    def fwd(b, c):
        return (b * nc + c, 0)

    def bwd(b, c):
        return (b * nc + nc - 1 - c, 0)

    def fwd_t(b, c):
        return (0, b * nc + c)

    def bwd_t(b, c):
        return (0, b * nc + nc - 1 - c)

    tok = pl.BlockSpec((L, M_WIDTH), fwd)
    tok_b = pl.BlockSpec((L, M_WIDTH), bwd)
    in_specs = [tok, tok, tok, pl.BlockSpec((L, N_GATES), fwd), pl.BlockSpec((N_GATES, L), fwd_t),
                tok_b, tok_b, tok_b, pl.BlockSpec((L, N_GATES), bwd), pl.BlockSpec((N_GATES, L), bwd_t),
                pl.BlockSpec((1, n_ch, M_DH, 2 * M_DH), lambda b, c: (b, 0, 0, 0)),
                pl.BlockSpec((1, n_ch, 8, M_DH), lambda b, c: (b, 0, 0, 0))]
    out_specs = [tok, tok_b,
                 pl.BlockSpec((1, n_ch, M_DH, 2 * M_DH), lambda b, c: (b, 0, 0, 0)),
                 pl.BlockSpec((1, n_ch, 8, M_DH), lambda b, c: (b, 0, 0, 0))]
    out_shape = [jax.ShapeDtypeStruct((n, M_WIDTH), F32), jax.ShapeDtypeStruct((n, M_WIDTH), F32),
                 jax.ShapeDtypeStruct((batch, n_ch, M_DH, 2 * M_DH), F32),
                 jax.ShapeDtypeStruct((batch, n_ch, 8, M_DH), F32)]
    return pl.pallas_call(
        _mlstm_kernel, grid=(batch, nc), in_specs=in_specs, out_specs=out_specs, out_shape=out_shape,
        scratch_shapes=[pltpu.VMEM((n_ch, M_DH, 2 * M_DH), F32), pltpu.VMEM((n_ch, 8, M_DH), F32)],
        compiler_params=pltpu.CompilerParams(dimension_semantics=("parallel", "arbitrary")),
        name="mlstm",
    )(mq, mk, mv, gcol, grow, mq, mk, mv, gcol, grow, c0, m0)


def _sink_column(sink_ref, grp, rows_per_head):
    return jnp.concatenate(
        [jnp.full((rows_per_head, 1), sink_ref[grp * A_REP + r], F32) for r in range(A_REP)], axis=0)


def _store_heads(out_ref, o, grp, rows_per_head):
    for r in range(A_REP):
        hd = grp * A_REP + r
        out_ref[:, hd * A_DH:(hd + 1) * A_DH] = o[r * rows_per_head:(r + 1) * rows_per_head,
                                                  grp * A_DH:(grp + 1) * A_DH].astype(out_ref.dtype)


def _attn_ctx_kernel(sink_ref, q_ref, k_ref, v_ref, out_ref):
    s_len = k_ref.shape[0]
    k = k_ref[...].astype(BF16)
    v = v_ref[...].astype(BF16)
    for grp in range(A_KV):
        q = q_ref[grp * A_REP:(grp + 1) * A_REP].reshape(A_REP * s_len, KV_WIDTH)
        s = _dot_t(q, k)
        sk = _sink_column(sink_ref, grp, s_len)
        mx = jnp.maximum(jnp.max(s, axis=1, keepdims=True), sk)
        p = jnp.exp(s - mx)
        den = jnp.sum(p, axis=1, keepdims=True) + jnp.exp(sk - mx)
        o = jnp.dot(p.astype(BF16), v, preferred_element_type=F32) / den
        _store_heads(out_ref, o, grp, s_len)


def _attn_ctx_call(sink, aq, ak, av, batch, seq):
    n = ak.shape[0]
    return pl.pallas_call(
        _attn_ctx_kernel, grid=(batch,),
        in_specs=[pl.BlockSpec(memory_space=pltpu.SMEM),
                  pl.BlockSpec((A_HEADS, seq, KV_WIDTH), lambda b: (0, b, 0)),
                  pl.BlockSpec((seq, KV_WIDTH), lambda b: (b, 0)),
                  pl.BlockSpec((seq, KV_WIDTH), lambda b: (b, 0))],
        out_specs=pl.BlockSpec((seq, A_WIDTH), lambda b: (b, 0)),
        out_shape=jax.ShapeDtypeStruct((n, A_WIDTH), BF16),
        compiler_params=pltpu.CompilerParams(dimension_semantics=("parallel",)),
        name="attn_ctx",
    )(sink, aq, ak, av)


def _attn_lat_kernel(sink_ref, q_ref, kc_ref, vc_ref, kp_ref, kq_ref, kn_ref, vp_ref, vq_ref, vn_ref, out_ref):
    i = pl.program_id(1)
    nb = pl.num_programs(1)
    kc = kc_ref[0].astype(BF16)
    vc = vc_ref[0].astype(BF16)
    kp, kq, kn = kp_ref[...].astype(BF16), kq_ref[...].astype(BF16), kn_ref[...].astype(BF16)
    vp, vq, vn = vp_ref[...].astype(BF16), vq_ref[...].astype(BF16), vn_ref[...].astype(BF16)
    rows = A_REP * BLOCK
    qpos = lax.broadcasted_iota(I32, (rows, BLOCK), 0) % BLOCK
    kpos = lax.broadcasted_iota(I32, (rows, BLOCK), 1)
    mask_p = (kpos >= qpos) & (i > 0)
    mask_n = (kpos <= qpos) & (i < nb - 1)
    for grp in range(A_KV):
        q = q_ref[grp * A_REP:(grp + 1) * A_REP].reshape(rows, KV_WIDTH)
        s_c = _dot_t(q, kc)
        s_p = jnp.where(mask_p, _dot_t(q, kp), NEG)
        s_q = _dot_t(q, kq)
        s_n = jnp.where(mask_n, _dot_t(q, kn), NEG)
        sk = _sink_column(sink_ref, grp, BLOCK)
        mx = jnp.maximum(jnp.maximum(jnp.max(s_c, axis=1, keepdims=True), jnp.max(s_p, axis=1, keepdims=True)),
                         jnp.maximum(jnp.max(s_q, axis=1, keepdims=True), jnp.max(s_n, axis=1, keepdims=True)))
        mx = jnp.maximum(mx, sk)
        p_c, p_p, p_q, p_n = jnp.exp(s_c - mx), jnp.exp(s_p - mx), jnp.exp(s_q - mx), jnp.exp(s_n - mx)
        den = (jnp.sum(p_c, axis=1, keepdims=True) + jnp.sum(p_p, axis=1, keepdims=True)
               + jnp.sum(p_q, axis=1, keepdims=True) + jnp.sum(p_n, axis=1, keepdims=True) + jnp.exp(sk - mx))
        o = (jnp.dot(p_c.astype(BF16), vc, preferred_element_type=F32)
             + jnp.dot(p_p.astype(BF16), vp, preferred_element_type=F32)
             + jnp.dot(p_q.astype(BF16), vq, preferred_element_type=F32)
             + jnp.dot(p_n.astype(BF16), vn, preferred_element_type=F32)) / den
        _store_heads(out_ref, o, grp, BLOCK)


def _attn_lat_call(sink, aq, ak, av, kc, vc, batch, seq):
    n = ak.shape[0]
    nb = seq // BLOCK
    past = kc.shape[1]

    def cur(b, i):
        return (b * nb + i, 0)

    def prev(b, i):
        return (b * nb + jnp.maximum(i - 1, 0), 0)

    def nxt(b, i):
        return (b * nb + jnp.minimum(i + 1, nb - 1), 0)

    blk = functools.partial(pl.BlockSpec, (BLOCK, KV_WIDTH))
    cache = pl.BlockSpec((1, past, KV_WIDTH), lambda b, i: (b, 0, 0))
    return pl.pallas_call(
        _attn_lat_kernel, grid=(batch, nb),
        in_specs=[pl.BlockSpec(memory_space=pltpu.SMEM),
                  pl.BlockSpec((A_HEADS, BLOCK, KV_WIDTH), lambda b, i: (0, b * nb + i, 0)),
                  cache, cache, blk(prev), blk(cur), blk(nxt), blk(prev), blk(cur), blk(nxt)],
        out_specs=pl.BlockSpec((BLOCK, A_WIDTH), cur),
        out_shape=jax.ShapeDtypeStruct((n, A_WIDTH), BF16),
        compiler_params=pltpu.CompilerParams(dimension_semantics=("parallel", "parallel")),
        name="attn_lat",
    )(sink, aq, kc, vc, ak, ak, ak, av, av, av)


def _top16_rows(s, payload=None):
    n_rows = s.shape[0]
    rows = lax.broadcasted_iota(I32, s.shape, 0).astype(F32)
    vals, idxs, pays = [], [], []
    for _ in range(P_TOPK):
        mx = jnp.max(s, axis=0, keepdims=True)
        ix = jnp.min(jnp.where(s == mx, rows, float(n_rows)), axis=0, keepdims=True)
        hit = rows == ix
        vals.append(mx)
        idxs.append(ix)
        if payload is not None:
            pays.append(jnp.sum(jnp.where(hit, payload, 0.0), axis=0, keepdims=True))
        s = jnp.where(hit, -jnp.inf, s)
    out = (jnp.concatenate(vals, axis=0), jnp.concatenate(idxs, axis=0))
    if payload is not None:
        out += (jnp.concatenate(pays, axis=0),)
    return out


def _mix_kernel(x_ref, hf_ref, hb_ref, mo_ref, ao_ref, mod_ref, mhg_ref, n2_ref, wm_ref, wa_ref, wq_ref,
                sa_ref, sb_ref, x1_ref, h2_ref, eidx_ref, gate_ref, qp_scr, e_scr, g_scr):
    tm = x_ref.shape[0]
    hs = hf_ref[...] + hb_ref[...]
    parts = []
    for hd in range(M_HEADS):
        blk = hs[:, hd * M_DH:(hd + 1) * M_DH]
        parts.append(blk * lax.rsqrt(jnp.mean(blk * blk, axis=-1, keepdims=True) + EPS))
    m_out = _sigmoid(mo_ref[...]) * (jnp.concatenate(parts, axis=1) * mhg_ref[...])
    mix = (jnp.dot(m_out.astype(BF16), wm_ref[...], preferred_element_type=F32)
           + jnp.dot(ao_ref[...], wa_ref[...], preferred_element_type=F32))
    x1 = x_ref[...] + mod_ref[0, 2:3, :] * mix
    x1_ref[...] = x1
    h2 = x1 * lax.rsqrt(jnp.mean(x1 * x1, axis=-1, keepdims=True) + EPS) * n2_ref[...]
    h2 = h2 * (1.0 + mod_ref[0, 4:5, :]) + mod_ref[0, 3:4, :]
    h2_ref[...] = _pack_bf16_pairs(h2)
    qp = jnp.dot(h2.astype(BF16), wq_ref[...], preferred_element_type=F32)
    for p in range(P_HEADS):
        qp_scr[p] = qp[:, p * P_DKEY:(p + 1) * P_DKEY].astype(BF16)
    sub_a = sa_ref[...]
    sub_b = sb_ref[...]

    def head_body(p, carry):
        for half in range(tm // N_KEYS):
            cols = slice(half * N_KEYS, (half + 1) * N_KEYS)
            qh = qp_scr[p, pl.ds(half * N_KEYS, N_KEYS), :]
            s_a = _dot_t(sub_a, qh[:, 0:P_HALF])
            s_b = _dot_t(sub_b, qh[:, P_HALF:P_DKEY])
            va, ia = _top16_rows(s_a)
            vb, ib = _top16_rows(s_b)
            cand = jnp.concatenate([va[i:i + 1, :] + vb for i in range(P_TOPK)], axis=0)
            cidx = jnp.concatenate([ia[i:i + 1, :] * float(N_KEYS) + ib for i in range(P_TOPK)], axis=0)
            top, _, eidx = _top16_rows(cand, cidx)
            ex = jnp.exp(top - jnp.max(top, axis=0, keepdims=True))
            gates = ex / jnp.sum(ex, axis=0, keepdims=True)
            r0 = pl.multiple_of(p * P_TOPK, P_TOPK)
            e_scr[pl.ds(r0, P_TOPK), cols] = eidx
            g_scr[pl.ds(r0, P_TOPK), cols] = gates
        return carry

    lax.fori_loop(0, P_HEADS, head_body, 0)
    for half in range(tm // N_KEYS):
        cols = slice(half * N_KEYS, (half + 1) * N_KEYS)
        eidx_ref[cols, :] = e_scr[:, cols].T.astype(I32)
        gate_ref[cols, :] = g_scr[:, cols].T


def _mix_call(x2d, seq, h_f, h_b, mo, a_out, mod3, mod_row0, mhg, n2, w_m, w_a, w_q, sub_a, sub_b, tile0, n_tiles):
    tm = TOKEN_TILE
    per_seq = seq // tm
    n = n_tiles * tm

    def tok_in(i):
        return (tile0 + i, 0)

    def tok(i):
        return (i, 0)

    def const2(i):
        return (0, 0)

    in_specs = [
        pl.BlockSpec((tm, D_MODEL), tok_in),
        pl.BlockSpec((tm, M_WIDTH), tok_in), pl.BlockSpec((tm, M_WIDTH), tok_in),
        pl.BlockSpec((tm, M_WIDTH), tok_in),
        pl.BlockSpec((tm, A_WIDTH), tok_in),
        pl.BlockSpec((1, 6, D_MODEL), lambda i: (mod_row0 + ((tile0 + i) // per_seq if mod_row0 else 0), 0, 0)),
        pl.BlockSpec((1, M_WIDTH), const2),
        pl.BlockSpec((1, D_MODEL), const2),
        pl.BlockSpec((M_WIDTH, D_MODEL), const2),
        pl.BlockSpec((A_WIDTH, D_MODEL), const2),
        pl.BlockSpec((D_MODEL, P_HEADS * P_DKEY), const2),
        pl.BlockSpec((N_KEYS, P_HALF), const2),
        pl.BlockSpec((N_KEYS, P_HALF), const2),
    ]
    out_specs = [pl.BlockSpec((tm, D_MODEL), tok), pl.BlockSpec((tm, D_MODEL // 2), tok),
                 pl.BlockSpec((tm, N_SEL), tok), pl.BlockSpec((tm, N_SEL), tok)]
    out_shape = [jax.ShapeDtypeStruct((n, D_MODEL), F32), jax.ShapeDtypeStruct((n, D_MODEL // 2), I32),
                 jax.ShapeDtypeStruct((n, N_SEL), I32), jax.ShapeDtypeStruct((n, N_SEL), F32)]
    return pl.pallas_call(
        _mix_kernel, grid=(n_tiles,), in_specs=in_specs, out_specs=out_specs, out_shape=out_shape,
        scratch_shapes=[pltpu.VMEM((P_HEADS, tm, P_DKEY), BF16), pltpu.VMEM((N_SEL, tm), F32),
                        pltpu.VMEM((N_SEL, tm), F32)],
        compiler_params=pltpu.CompilerParams(dimension_semantics=("parallel",)),
        name="mix",
    )(x2d, h_f, h_b, mo, a_out, mod3, mhg, n2, w_m, w_a, w_q, sub_a, sub_b)


SC_LANES = 16
SC_CORES = 2
SC_SUBCORES = 16
SC_WORKERS = SC_CORES * SC_SUBCORES
SC_TOKENS = 32
SC_GROUP = SC_LANES
SC_NGROUPS = N_SEL // SC_GROUP
SC_SLOTS = 4
ROW_WORDS = D_MODEL // 2
SC_DOT_ROWS = 8
SC_DOT_PARTIALS = 2
SC_OWORDS = 16 * SC_LANES
HI_MASK = -65536
GELU_C0 = 0.7978845608028654
GELU_C1 = 0.044715


def _pack_bf16_pairs(x):
    half = x.shape[1] // 2
    bits = lax.bitcast_convert_type(x.astype(BF16).astype(F32), I32)
    return (bits[:, :half] & HI_MASK) | lax.shift_right_logical(bits[:, half:], jnp.int32(16))


def _sc_gelu(a):
    z = GELU_C0 * (a + GELU_C1 * (a * a * a))
    tanh = 1.0 - 2.0 / (jnp.exp(2.0 * z) + 1.0)
    return 0.5 * a * (1.0 + tanh)


def _sc_split(words):
    return (plsc.bitcast(words & HI_MASK, F32), plsc.bitcast(lax.shift_left(words, jnp.int32(16)), F32))


def _sc_mul_bf16(a_words, b_words):
    return plsc.bitcast(a_words, BF16) * plsc.bitcast(b_words, BF16)


def _sc_split_sum(p, q):
    return _sc_split(plsc.bitcast(p + q, I32))


def _peer_sc_kernel(h2_hbm, eidx_hbm, gate_hbm, u_hbm, v_hbm, out_hbm,
                    xbuf, ibuf, gbuf, obuf, ubuf, vbuf, mbuf, wbuf, sems_u, sems_v):
    n = h2_hbm.shape[0]
    per_worker = n // SC_WORKERS
    wid = lax.axis_index("c") * SC_SUBCORES + lax.axis_index("s")
    lane = lax.iota(I32, SC_LANES)

    def split_item(item):
        return lax.shift_right_logical(item, SC_NGROUPS.bit_length() - 1), item & (SC_NGROUPS - 1)

    def gather_copies(item, slot):
        t, g = split_item(item)
        idx = ibuf[t, pl.ds(g * SC_GROUP, SC_GROUP)]
        return (pltpu.make_async_copy(u_hbm.at[idx], ubuf.at[slot], sems_u.at[slot]),
                pltpu.make_async_copy(v_hbm.at[idx], vbuf.at[slot], sems_v.at[slot]))

    def dots(t, slot):
        zero = jnp.zeros((SC_LANES,), F32)

        @pl.loop(0, SC_GROUP, step=SC_DOT_ROWS)
        def _(r0):
            accs = [[zero] * SC_DOT_PARTIALS for _ in range(SC_DOT_ROWS)]
            for k in range(0, ROW_WORDS // SC_LANES, 4):
                xs = [xbuf[t, pl.ds((k + q) * SC_LANES, SC_LANES)] for q in range(4)]
                for i in range(SC_DOT_ROWS):
                    m = [_sc_mul_bf16(xs[q], ubuf[slot, r0 + i, pl.ds((k + q) * SC_LANES, SC_LANES)])
                         for q in range(4)]
                    hi, lo = _sc_split_sum(m[0] + m[1], m[2] + m[3])
                    p = (k // 4) % SC_DOT_PARTIALS
                    accs[i][p] = accs[i][p] + (hi + lo)
            for i in range(SC_DOT_ROWS):
                mbuf[r0 + i, :] = functools.reduce(lambda a, b: a + b, accs[i])

        tot = zero
        for c in range(SC_LANES):
            tot = tot + plsc.load_gather(mbuf, [lane, jnp.full((SC_LANES,), c, I32)])
        return tot

    def accumulate(t, slot):
        nv = SC_OWORDS // SC_LANES
        for oc in range(ROW_WORDS // SC_OWORDS):
            w0 = oc * SC_OWORDS
            accs = (tuple(obuf[t, pl.ds(w0 + j * SC_LANES, SC_LANES)] for j in range(nv))
                    + tuple(obuf[t, pl.ds(ROW_WORDS + w0 + j * SC_LANES, SC_LANES)] for j in range(nv)))

            def row_quad(rq, accs):
                r = 4 * rq
                ws = [plsc.load_gather(wbuf, [jnp.full((SC_LANES,), r + q, I32)]) for q in range(4)]
                his, los = [], []
                for j in range(nv):
                    m = [_sc_mul_bf16(ws[q], vbuf[slot, r + q, pl.ds(w0 + j * SC_LANES, SC_LANES)])
                         for q in range(4)]
                    hi, lo = _sc_split_sum(m[0] + m[1], m[2] + m[3])
                    his.append(accs[j] + hi)
                    los.append(accs[nv + j] + lo)
                return tuple(his) + tuple(los)

            accs = lax.fori_loop(0, SC_GROUP // 4, row_quad, accs)
            for j in range(nv):
                obuf[t, pl.ds(w0 + j * SC_LANES, SC_LANES)] = accs[j]
                obuf[t, pl.ds(ROW_WORDS + w0 + j * SC_LANES, SC_LANES)] = accs[nv + j]

    def pack_weights(w):
        bits = plsc.bitcast(w, I32)
        rounded = (bits + 0x7FFF + (lax.shift_right_logical(bits, jnp.int32(16)) & 1)) & HI_MASK
        return rounded | lax.shift_right_logical(rounded, jnp.int32(16))

    n_items = SC_TOKENS * SC_NGROUPS

    @pl.loop(0, per_worker // SC_TOKENS)
    def _(blk):
        tok0 = pl.multiple_of(wid * per_worker + blk * SC_TOKENS, SC_TOKENS)
        pltpu.sync_copy(h2_hbm.at[pl.ds(tok0, SC_TOKENS)], xbuf)
        pltpu.sync_copy(eidx_hbm.at[pl.ds(tok0, SC_TOKENS)], ibuf)
        pltpu.sync_copy(gate_hbm.at[pl.ds(tok0, SC_TOKENS)], gbuf)

        @pl.loop(0, SC_TOKENS)
        def _(t):
            zero = jnp.zeros((SC_LANES,), F32)
            for j in range(D_MODEL // SC_LANES):
                obuf[t, pl.ds(j * SC_LANES, SC_LANES)] = zero

        for ahead in range(SC_SLOTS - 1):
            for c in gather_copies(ahead, ahead):
                c.start()

        @pl.loop(0, n_items)
        def _(item):
            t, g = split_item(item)
            slot = item & (SC_SLOTS - 1)
            ahead = item + (SC_SLOTS - 1)

            @pl.when(ahead < n_items)
            def _():
                for c in gather_copies(ahead, ahead & (SC_SLOTS - 1)):
                    c.start()

            for c in gather_copies(item, slot):
                c.wait()
            a = dots(t, slot)
            wbuf[...] = pack_weights(gbuf[t, pl.ds(g * SC_GROUP, SC_GROUP)] * _sc_gelu(a))
            accumulate(t, slot)

        pltpu.sync_copy(obuf, out_hbm.at[pl.ds(tok0, SC_TOKENS)])


def _peer_experts(h2p, eidx, gates, u_pack, v_pack):
    n = h2p.shape[0]
    mesh = plsc.VectorSubcoreMesh(core_axis_name="c", subcore_axis_name="s")
    fn = pl.kernel(
        _peer_sc_kernel,
        out_type=jax.ShapeDtypeStruct((n, D_MODEL), F32),
        mesh=mesh,
        scratch_types=[
            pltpu.VMEM((SC_TOKENS, ROW_WORDS), I32),
            pltpu.VMEM((SC_TOKENS, N_SEL), I32),
            pltpu.VMEM((SC_TOKENS, N_SEL), F32),
            pltpu.VMEM((SC_TOKENS, D_MODEL), F32),
            pltpu.VMEM((SC_SLOTS, SC_GROUP, ROW_WORDS), I32),
            pltpu.VMEM((SC_SLOTS, SC_GROUP, ROW_WORDS), I32),
            pltpu.VMEM((SC_GROUP, SC_LANES), F32),
            pltpu.VMEM((SC_LANES,), I32),
            pltpu.SemaphoreType.DMA((SC_SLOTS,)), pltpu.SemaphoreType.DMA((SC_SLOTS,)),
        ],
        compiler_params=pltpu.CompilerParams(needs_layout_passes=False),
        cost_estimate=pl.CostEstimate(
            flops=4 * n * N_SEL * D_MODEL, transcendentals=n * N_SEL,
            bytes_accessed=4 * (2 * n * N_SEL * ROW_WORDS + n * ROW_WORDS + n * D_MODEL + 2 * n * N_SEL)),
        name="peer_experts",
    )
    return fn(h2p, eidx, gates, u_pack, v_pack)


def _resid_kernel(x1_ref, p_ref, mod_ref, o_ref):
    o_ref[...] = x1_ref[...] + mod_ref[0, 5:6, :] * p_ref[...]


def _resid_call(x1, peer_out, seq, mod3, mod_row0, tile0):
    n = x1.shape[0]
    tm = TOKEN_TILE
    per_seq = seq // tm
    tok = pl.BlockSpec((tm, D_MODEL), lambda i: (i, 0))
    return pl.pallas_call(
        _resid_kernel, grid=(n // tm,),
        in_specs=[tok, tok,
                  pl.BlockSpec((1, 6, D_MODEL),
                               lambda i: (mod_row0 + ((tile0 + i) // per_seq if mod_row0 else 0), 0, 0))],
        out_specs=tok, out_shape=jax.ShapeDtypeStruct((n, D_MODEL), F32),
        compiler_params=pltpu.CompilerParams(dimension_semantics=("parallel",)),
        name="resid",
    )(x1, peer_out, mod3)


def _rope_tables(seq, rotate):
    if not rotate:
        return jnp.ones((seq, KV_WIDTH), F32), jnp.zeros((seq, KV_WIDTH), F32)
    quarter = A_DH // 4
    t = jnp.arange(seq)
    row = (t // GRID_W).astype(F32)
    col = (t % GRID_W).astype(F32)
    inv = ROPE_BASE ** (-jnp.arange(quarter, dtype=F32) / quarter)
    d = jnp.arange(A_DH)
    pos = jnp.where(d[None, :] < A_DH // 2, row[:, None], col[:, None])
    ang = pos * inv[d % quarter][None, :]
    sign = jnp.where((d % (A_DH // 2)) < quarter, -1.0, 1.0).astype(F32)
    cos = jnp.cos(ang)
    sin = jnp.sin(ang) * sign[None, :]
    return jnp.tile(cos, (1, KV_WIDTH // A_DH)), jnp.tile(sin, (1, KV_WIDTH // A_DH))


def _run_chunk(x, mod3, mod_row0, prm, cache, rotate, gate_on):
    (n1, n2, w_main, w_g, w_gt, b_g, b_gt, mhg, qg_t, kg_t, bd, sink, w_m, w_a, w_q, sub_a, sub_b,
     u_pack, v_pack) = prm
    batch, seq, _ = x.shape
    n = batch * seq
    x2d = x.reshape(n, D_MODEL)
    if gate_on is not None:
        x2d, _ = lax.optimization_barrier((x2d, gate_on))
    cos, sin = _rope_tables(seq, rotate)
    mq, mk, mv, mo, gcol, grow, aq, ak, av = _inproj_call(
        x2d, seq, mod3, mod_row0, n1, w_main, w_g, w_gt, b_g, b_gt, qg_t, kg_t, bd, cos, sin)
    kc, vc, c0, m0 = cache
    h_f, h_b, c_fin, m_fin = _mlstm_call(mq, mk, mv, gcol, grow, c0, m0, batch, seq)
    if kc is None:
        a_out = _attn_ctx_call(sink, aq, ak, av, batch, seq)
    else:
        a_out = _attn_lat_call(sink, aq, ak, av, kc, vc, batch, seq)
    x1, h2p, eidx, gates = _mix_call(x2d, seq, h_f, h_b, mo, a_out, mod3, mod_row0, mhg, n2, w_m, w_a, w_q,
                                     sub_a, sub_b, 0, n // TOKEN_TILE)
    peer_out = _peer_experts(h2p, eidx, gates, u_pack, v_pack)
    y = _resid_call(x1, peer_out, seq, mod3, mod_row0, 0).reshape(batch, seq, D_MODEL)
    return y, h2p, peer_out, ak, av, c_fin, m_fin


def _pack_state(C, n_vec, m):
    b = C.shape[0]
    caug = jnp.concatenate([C, jnp.broadcast_to(n_vec[..., None], C.shape)], axis=-1)
    caug = caug.reshape(b, 2 * M_HEADS, M_DH, 2 * M_DH)
    m_rep = jnp.broadcast_to(m.reshape(b, 2 * M_HEADS, 1, 1), (b, 2 * M_HEADS, 8, M_DH))
    return caug.astype(F32), m_rep.astype(F32)


def kernel(x_prompt, x_sample, c, cache_attn_k, cache_attn_v, state_mlstm_C, state_mlstm_n, state_mlstm_m,
           c_ctx, w_ada, b_ada, norm1_g, norm2_g, w_in, b_gates, mh_norm_g, q_norm_g, k_norm_g, sink_logits,
           w_out, peer_w_q, peer_sub_a, peer_sub_b, peer_u, peer_v):
    depth = w_ada.shape[0]
    assert depth == 1
    batch, seq, _ = x_prompt.shape
    dec_batch, dec_seq, _ = x_sample.shape
    l = 0

    cond = jnp.concatenate([c_ctx[None, :], c, jnp.zeros((MOD_ROWS - 1 - dec_batch, D_MODEL), F32)], axis=0)
    mod3 = _ada_call(cond, w_ada[l], b_ada[l]).reshape(MOD_ROWS, 6, D_MODEL)

    wi = w_in[l]
    g0 = 4 * M_WIDTH
    w_main = jnp.concatenate([wi[:, :g0], wi[:, g0 + N_GATES:]], axis=1).astype(BF16)
    w_g = wi[:, g0:g0 + N_GATES]
    seg = jnp.arange(A_WIDTH) // A_DH
    bd = jnp.where(seg[:, None] == seg[None, :], 1.0 / A_DH, 0.0).astype(F32)
    prm = (norm1_g[l][None, :], norm2_g[l][None, :], w_main, w_g, w_g.T, b_gates[l][None, :], b_gates[l][:, None],
           mh_norm_g[l][None, :], jnp.tile(q_norm_g[l], A_HEADS)[None, :], jnp.tile(k_norm_g[l], A_KV)[None, :], bd,
           sink_logits[l], w_out[l][:M_WIDTH].astype(BF16), w_out[l][M_WIDTH:].astype(BF16),
           peer_w_q[l].astype(BF16), peer_sub_a[l].astype(BF16), peer_sub_b[l].astype(BF16),
           _pack_bf16_pairs(peer_u[l]), _pack_bf16_pairs(peer_v[l]))

    zeros_c = jnp.zeros((batch, 2, M_HEADS, M_DH, M_DH), F32)
    c0, m0 = _pack_state(zeros_c, zeros_c[..., 0], jnp.full((batch, 2, M_HEADS), NEG, F32))
    c0s, m0s = _pack_state(state_mlstm_C[:, l], state_mlstm_n[:, l], state_mlstm_m[:, l])
    past = cache_attn_k.shape[2]
    kc = cache_attn_k[:, l].reshape(dec_batch, past, KV_WIDTH)
    vc = cache_attn_v[:, l].reshape(dec_batch, past, KV_WIDTH)

    jobs = []
    for b0 in range(0, batch, batch // CTX_CHUNKS):
        b1 = b0 + batch // CTX_CHUNKS
        jobs.append((x_prompt[b0:b1], 0, (None, None, c0[b0:b1], m0[b0:b1]), False))
    for b0 in range(0, dec_batch, dec_batch // LATENT_CHUNKS):
        b1 = b0 + dec_batch // LATENT_CHUNKS
        jobs.append((x_sample[b0:b1], 1 + b0, (kc[b0:b1], vc[b0:b1], c0s[b0:b1], m0s[b0:b1]), True))
    outs = []
    for i, (x_c, mod_row0, cache, rotate) in enumerate(jobs):
        gate = tuple(g for g in (outs[i - 1][1] if i >= 1 else None,
                                 outs[i - EXPERT_LAG][2] if i >= EXPERT_LAG else None) if g is not None)
        outs.append(_run_chunk(x_c, mod3, mod_row0, prm, cache, rotate, gate or None))
    ctx, lat = outs[:CTX_CHUNKS], outs[CTX_CHUNKS:]
    y_p = jnp.concatenate([o[0] for o in ctx], axis=0)
    y_s = jnp.concatenate([o[0] for o in lat], axis=0)
    k_new = jnp.concatenate([o[3] for o in ctx], axis=0)
    v_new = jnp.concatenate([o[4] for o in ctx], axis=0)
    c_fin = jnp.concatenate([o[5] for o in ctx], axis=0)
    m_fin = jnp.concatenate([o[6] for o in ctx], axis=0)

    c_fin = c_fin.reshape(batch, 2, M_HEADS, M_DH, 2 * M_DH)
    new_c = c_fin[..., :M_DH][:, None]
    new_n = c_fin[..., M_DH][:, None]
    new_m = m_fin[:, :, 0, 0].reshape(batch, 2, M_HEADS)[:, None]
    new_k = k_new.reshape(batch, 1, seq, A_KV, A_DH)
    new_v = v_new.reshape(batch, 1, seq, A_KV, A_DH)
    return y_p, y_s, new_k, new_v, new_c, new_n, new_m
```

```python
import functools

import jax
import jax.numpy as jnp
from jax import lax
from jax.experimental import pallas as pl
from jax.experimental.pallas import tpu as pltpu
from jax.experimental.pallas import tpu_sc as plsc

F32 = jnp.float32
BF16 = jnp.bfloat16
I32 = jnp.int32
HI = lax.Precision.HIGHEST

D_MODEL = 1024
EPS = 1e-6
NEG = -1e30
GRID_W = 64
M_HEADS = 4
M_WIDTH = 512
M_DH = 128
A_HEADS = 8
A_KV = 2
A_REP = 4
A_DH = 64
A_WIDTH = 512
KV_WIDTH = A_KV * A_DH
BLOCK = 128
ROPE_BASE = 10000.0
N_KEYS = 128
P_HEADS = 8
P_DKEY = 256
P_HALF = 128
P_TOPK = 16
N_SEL = P_HEADS * P_TOPK
N_GATES = 4 * M_HEADS
MAIN_COLS = 4 * M_WIDTH + A_WIDTH + 2 * KV_WIDTH
MOD_ROWS = 16

TOKEN_TILE = 256
MLSTM_CHUNK = 128
ADA_COL_TILE = 768
CTX_CHUNKS = 4
EXPERT_LAG = 3
LATENT_CHUNKS = 8


def _sigmoid(x):
    return 1.0 / (1.0 + jnp.exp(-x))


def _log_sigmoid(x):
    return jnp.minimum(x, 0.0) - jnp.log1p(jnp.exp(-jnp.abs(x)))


def _dot_t(a, b, precision=None):
    return lax.dot_general(a, b, (((1,), (1,)), ((), ())), precision=precision,
                           preferred_element_type=F32)


def _ada_kernel(c_ref, w_ref, b_ref, o_ref):
    c = c_ref[...]
    s = c * _sigmoid(c)
    o_ref[...] = jnp.dot(s, w_ref[...], precision=HI, preferred_element_type=F32) + b_ref[...]


def _ada_call(cond, w_ada, b_ada):
    n_out = w_ada.shape[1]
    return pl.pallas_call(
        _ada_kernel,
        grid=(n_out // ADA_COL_TILE,),
        in_specs=[pl.BlockSpec((MOD_ROWS, D_MODEL), lambda j: (0, 0)),
                  pl.BlockSpec((D_MODEL, ADA_COL_TILE), lambda j: (0, j)),
                  pl.BlockSpec((1, ADA_COL_TILE), lambda j: (0, j))],
        out_specs=pl.BlockSpec((MOD_ROWS, ADA_COL_TILE), lambda j: (0, j)),
        out_shape=jax.ShapeDtypeStruct((MOD_ROWS, n_out), F32),
        name="ada",
    )(cond, w_ada, b_ada.reshape(1, n_out))


def _swap16(x):
    n = x.shape[-1]
    lane = lax.broadcasted_iota(I32, x.shape, x.ndim - 1)
    return jnp.where((lane & 16) == 0, pltpu.roll(x, n - 16, x.ndim - 1), pltpu.roll(x, 16, x.ndim - 1))


def _inproj_kernel(x_ref, mod_ref, n1_ref, w_ref, wg_ref, wgt_ref, bg_ref, bgt_ref, qg_ref, kg_ref,
                   bd_ref, cos_ref, sin_ref,
                   mq_ref, mk_ref, mv_ref, mo_ref, gc_ref, gr_ref, aq_ref, ak_ref, av_ref):
    x = x_ref[...]
    h = x * lax.rsqrt(jnp.mean(x * x, axis=-1, keepdims=True) + EPS) * n1_ref[...]
    h = h * (1.0 + mod_ref[0, 1:2, :]) + mod_ref[0, 0:1, :]
    z = jnp.dot(h.astype(BF16), w_ref[...], preferred_element_type=F32)

    mq_ref[...] = (z[:, 0:M_WIDTH] * (M_DH ** -0.5)).astype(BF16)
    mk_ref[...] = z[:, M_WIDTH:2 * M_WIDTH].astype(BF16)
    mv_ref[...] = z[:, 2 * M_WIDTH:3 * M_WIDTH].astype(BF16)
    mo_ref[...] = z[:, 3 * M_WIDTH:4 * M_WIDTH]

    g = jnp.dot(h, wg_ref[...], precision=HI, preferred_element_type=F32) + bg_ref[...]
    kind = lax.broadcasted_iota(I32, g.shape, 1) // M_HEADS
    gc_ref[...] = jnp.where((kind & 1) == 1, _log_sigmoid(g), g)
    gt = _dot_t(wgt_ref[...], h, precision=HI) + bgt_ref[...]
    kind_t = lax.broadcasted_iota(I32, gt.shape, 0) // M_HEADS
    gr_ref[...] = jnp.where((kind_t & 1) == 1, _log_sigmoid(gt), gt)

    o = 4 * M_WIDTH
    aq = z[:, o:o + A_WIDTH]
    ak = z[:, o + A_WIDTH:o + A_WIDTH + KV_WIDTH]
    av_ref[...] = z[:, o + A_WIDTH + KV_WIDTH:o + A_WIDTH + 2 * KV_WIDTH]
    bd = bd_ref[...]
    cos = cos_ref[...]
    sin = sin_ref[...]
    aq = aq * lax.rsqrt(jnp.dot(aq * aq, bd, precision=HI, preferred_element_type=F32) + EPS) * qg_ref[...]
    cos4 = jnp.concatenate([cos] * (A_WIDTH // KV_WIDTH), axis=1)
    sin4 = jnp.concatenate([sin] * (A_WIDTH // KV_WIDTH), axis=1)
    aq = (aq * cos4 + _swap16(aq) * sin4) * (A_DH ** -0.5)
    ak = ak * lax.rsqrt(jnp.dot(ak * ak, bd[0:KV_WIDTH, 0:KV_WIDTH], precision=HI,
                                preferred_element_type=F32) + EPS) * kg_ref[...]
    ak_ref[...] = ak * cos + _swap16(ak) * sin

    lane = lax.broadcasted_iota(I32, (aq.shape[0], KV_WIDTH), 1)
    for hd in range(A_HEADS):
        grp = hd // A_REP
        blk = aq[:, (hd // 2) * KV_WIDTH:(hd // 2 + 1) * KV_WIDTH]
        if hd % 2 != grp:
            blk = pltpu.roll(blk, A_DH, 1)
        keep = (lane >= grp * A_DH) & (lane < (grp + 1) * A_DH)
        aq_ref[hd] = jnp.where(keep, blk, 0.0).astype(BF16)


def _inproj_call(x2d, seq, mod3, mod_row0, n1, w_main, w_g, w_gt, b_g, b_gt, qg_t, kg_t, bd, cos, sin):
    n = x2d.shape[0]
    tm = TOKEN_TILE
    per_seq = seq // tm

    def tok(i):
        return (i, 0)

    def const2(i):
        return (0, 0)

    in_specs = [
        pl.BlockSpec((tm, D_MODEL), tok),
        pl.BlockSpec((1, 6, D_MODEL), lambda i: (mod_row0 + (i // per_seq if mod_row0 else 0), 0, 0)),
        pl.BlockSpec((1, D_MODEL), const2),
        pl.BlockSpec((D_MODEL, MAIN_COLS), const2),
        pl.BlockSpec((D_MODEL, N_GATES), const2),
        pl.BlockSpec((N_GATES, D_MODEL), const2),
        pl.BlockSpec((1, N_GATES), const2),
        pl.BlockSpec((N_GATES, 1), const2),
        pl.BlockSpec((1, A_WIDTH), const2),
        pl.BlockSpec((1, KV_WIDTH), const2),
        pl.BlockSpec((A_WIDTH, A_WIDTH), const2),
        pl.BlockSpec((tm, KV_WIDTH), lambda i: (i % per_seq, 0)),
        pl.BlockSpec((tm, KV_WIDTH), lambda i: (i % per_seq, 0)),
    ]
    out_specs = [
        pl.BlockSpec((tm, M_WIDTH), tok),
        pl.BlockSpec((tm, M_WIDTH), tok),
        pl.BlockSpec((tm, M_WIDTH), tok),
        pl.BlockSpec((tm, M_WIDTH), tok),
        pl.BlockSpec((tm, N_GATES), tok),
        pl.BlockSpec((N_GATES, tm), lambda i: (0, i)),
        pl.BlockSpec((A_HEADS, tm, KV_WIDTH), lambda i: (0, i, 0)),
        pl.BlockSpec((tm, KV_WIDTH), tok),
        pl.BlockSpec((tm, KV_WIDTH), tok),
    ]
    out_shape = [
        jax.ShapeDtypeStruct((n, M_WIDTH), BF16),
        jax.ShapeDtypeStruct((n, M_WIDTH), BF16),
        jax.ShapeDtypeStruct((n, M_WIDTH), BF16),
        jax.ShapeDtypeStruct((n, M_WIDTH), F32),
        jax.ShapeDtypeStruct((n, N_GATES), F32),
        jax.ShapeDtypeStruct((N_GATES, n), F32),
        jax.ShapeDtypeStruct((A_HEADS, n, KV_WIDTH), BF16),
        jax.ShapeDtypeStruct((n, KV_WIDTH), F32),
        jax.ShapeDtypeStruct((n, KV_WIDTH), F32),
    ]
    return pl.pallas_call(
        _inproj_kernel, grid=(n // tm,), in_specs=in_specs, out_specs=out_specs, out_shape=out_shape,
        compiler_params=pltpu.CompilerParams(dimension_semantics=("parallel",)),
        name="inproj",
    )(x2d, mod3, n1, w_main, w_g, w_gt, b_g, b_gt, qg_t, kg_t, bd, cos, sin)


def _mlstm_chain(q, k, v, li_c, lf_c, li_r, lf_r, caug, m, tri, tri_t, mask, reverse):
    L = q.shape[0]
    last = 0 if reverse else L - 1
    b_c = jnp.dot(tri, jnp.broadcast_to(lf_c, (L, L)), precision=HI, preferred_element_type=F32)
    b_r = jnp.dot(jnp.broadcast_to(lf_r, (8, L)), tri_t, precision=HI, preferred_element_type=F32)[0:1, :]
    a_inter = b_c[:, 0:1] + m
    d = jnp.where(mask, b_c - b_r + li_r, -jnp.inf)
    m_t = jnp.maximum(a_inter, jnp.max(d, axis=1, keepdims=True))
    w_inter = jnp.exp(a_inter - m_t)
    s = _dot_t(q, k) * jnp.exp(d - m_t)
    qc = jnp.dot(q, caug.astype(BF16), preferred_element_type=F32)
    num = jnp.dot(s.astype(BF16), v, preferred_element_type=F32) + w_inter * qc[:, 0:M_DH]
    den = jnp.sum(s, axis=1, keepdims=True) + w_inter * qc[:, M_DH:M_DH + 1]
    den = jnp.maximum(jnp.abs(den), jnp.exp(-m_t))
    h = num / den
    m_new = m_t[last:last + 1, :]
    b_last = b_c[last:last + 1, 0:1]
    g_c = jnp.exp(b_last - b_c[:, 0:1] + li_c - m_new)
    decay = jnp.exp(b_last + m - m_new)
    kw = (k.astype(F32) * g_c).astype(BF16)
    vaug = jnp.concatenate([v, jnp.ones_like(v)], axis=1)
    upd = lax.dot_general(kw, vaug, (((0,), (0,)), ((), ())), preferred_element_type=F32)
    return h, decay * caug + upd, m_new


def _mlstm_kernel(qf_ref, kf_ref, vf_ref, gcf_ref, grf_ref, qb_ref, kb_ref, vb_ref, gcb_ref, grb_ref,
                  c0_ref, m0_ref, hf_ref, hb_ref, cfin_ref, mfin_ref, c_scr, m_scr):
    c = pl.program_id(1)
    nc = pl.num_programs(1)
    L = qf_ref.shape[0]

    @pl.when(c == 0)
    def _():
        c_scr[...] = c0_ref[0]
        m_scr[...] = m0_ref[0]

    row = lax.broadcasted_iota(I32, (L, L), 0)
    col = lax.broadcasted_iota(I32, (L, L), 1)
    lower = row >= col
    upper = row <= col
    lower_f = lower.astype(F32)
    upper_f = upper.astype(F32)

    for direction in range(2):
        reverse = direction == 1
        q_ref, k_ref, v_ref, gc_ref, gr_ref, h_ref = (
            (qb_ref, kb_ref, vb_ref, gcb_ref, grb_ref, hb_ref) if reverse
            else (qf_ref, kf_ref, vf_ref, gcf_ref, grf_ref, hf_ref))
        tri, tri_t, mask = (upper_f, lower_f, upper) if reverse else (lower_f, upper_f, lower)
        gc = gc_ref[...]
        gr = gr_ref[...]
        for hd in range(M_HEADS):
            ch = direction * M_HEADS + hd
            sl = slice(hd * M_DH, (hd + 1) * M_DH)
            ci = 2 * direction * M_HEADS + hd
            cf = ci + M_HEADS
            h, caug, m_new = _mlstm_chain(
                q_ref[:, sl], k_ref[:, sl], v_ref[:, sl],
                gc[:, ci:ci + 1], gc[:, cf:cf + 1], gr[ci:ci + 1, :], gr[cf:cf + 1, :],
                c_scr[ch], m_scr[ch][0:1, 0:1], tri, tri_t, mask, reverse)
            h_ref[:, sl] = h
            c_scr[ch] = caug
            m_scr[ch] = jnp.broadcast_to(m_new, m_scr.shape[1:])

    @pl.when(c == nc - 1)
    def _():
        cfin_ref[0] = c_scr[...]
        mfin_ref[0] = m_scr[...]


def _mlstm_call(mq, mk, mv, gcol, grow, c0, m0, batch, seq):
    n = mq.shape[0]
    L = MLSTM_CHUNK
    nc = seq // L
    n_ch = 2 * M_HEADS

    def fwd(b, c):
        return (b * nc + c, 0)

    def bwd(b, c):
        return (b * nc + nc - 1 - c, 0)

    def fwd_t(b, c):
        return (0, b * nc + c)

    def bwd_t(b, c):
        return (0, b * nc + nc - 1 - c)

    tok = pl.BlockSpec((L, M_WIDTH), fwd)
    tok_b = pl.BlockSpec((L, M_WIDTH), bwd)
    in_specs = [tok, tok, tok, pl.BlockSpec((L, N_GATES), fwd), pl.BlockSpec((N_GATES, L), fwd_t),
                tok_b, tok_b, tok_b, pl.BlockSpec((L, N_GATES), bwd), pl.BlockSpec((N_GATES, L), bwd_t),
                pl.BlockSpec((1, n_ch, M_DH, 2 * M_DH), lambda b, c: (b, 0, 0, 0)),
                pl.BlockSpec((1, n_ch, 8, M_DH), lambda b, c: (b, 0, 0, 0))]
    out_specs = [tok, tok_b,
                 pl.BlockSpec((1, n_ch, M_DH, 2 * M_DH), lambda b, c: (b, 0, 0, 0)),
                 pl.BlockSpec((1, n_ch, 8, M_DH), lambda b, c: (b, 0, 0, 0))]
    out_shape = [jax.ShapeDtypeStruct((n, M_WIDTH), F32), jax.ShapeDtypeStruct((n, M_WIDTH), F32),
                 jax.ShapeDtypeStruct((batch, n_ch, M_DH, 2 * M_DH), F32),
                 jax.ShapeDtypeStruct((batch, n_ch, 8, M_DH), F32)]
    return pl.pallas_call(
        _mlstm_kernel, grid=(batch, nc), in_specs=in_specs, out_specs=out_specs, out_shape=out_shape,
        scratch_shapes=[pltpu.VMEM((n_ch, M_DH, 2 * M_DH), F32), pltpu.VMEM((n_ch, 8, M_DH), F32)],
        compiler_params=pltpu.CompilerParams(dimension_semantics=("parallel", "arbitrary")),
        name="mlstm",
    )(mq, mk, mv, gcol, grow, mq, mk, mv, gcol, grow, c0, m0)


def _sink_column(sink_ref, grp, rows_per_head):
    return jnp.concatenate(
        [jnp.full((rows_per_head, 1), sink_ref[grp * A_REP + r], F32) for r in range(A_REP)], axis=0)


def _store_heads(out_ref, o, grp, rows_per_head):
    for r in range(A_REP):
        hd = grp * A_REP + r
        out_ref[:, hd * A_DH:(hd + 1) * A_DH] = o[r * rows_per_head:(r + 1) * rows_per_head,
                                                  grp * A_DH:(grp + 1) * A_DH].astype(out_ref.dtype)


def _attn_ctx_kernel(sink_ref, q_ref, k_ref, v_ref, out_ref):
    s_len = k_ref.shape[0]
    k = k_ref[...].astype(BF16)
    v = v_ref[...].astype(BF16)
    for grp in range(A_KV):
        q = q_ref[grp * A_REP:(grp + 1) * A_REP].reshape(A_REP * s_len, KV_WIDTH)
        s = _dot_t(q, k)
        sk = _sink_column(sink_ref, grp, s_len)
        mx = jnp.maximum(jnp.max(s, axis=1, keepdims=True), sk)
        p = jnp.exp(s - mx)
        den = jnp.sum(p, axis=1, keepdims=True) + jnp.exp(sk - mx)
        o = jnp.dot(p.astype(BF16), v, preferred_element_type=F32) / den
        _store_heads(out_ref, o, grp, s_len)


def _attn_ctx_call(sink, aq, ak, av, batch, seq):
    n = ak.shape[0]
    return pl.pallas_call(
        _attn_ctx_kernel, grid=(batch,),
        in_specs=[pl.BlockSpec(memory_space=pltpu.SMEM),
                  pl.BlockSpec((A_HEADS, seq, KV_WIDTH), lambda b: (0, b, 0)),
                  pl.BlockSpec((seq, KV_WIDTH), lambda b: (b, 0)),
                  pl.BlockSpec((seq, KV_WIDTH), lambda b: (b, 0))],
        out_specs=pl.BlockSpec((seq, A_WIDTH), lambda b: (b, 0)),
        out_shape=jax.ShapeDtypeStruct((n, A_WIDTH), BF16),
        compiler_params=pltpu.CompilerParams(dimension_semantics=("parallel",)),
        name="attn_ctx",
    )(sink, aq, ak, av)


def _attn_lat_kernel(sink_ref, q_ref, kc_ref, vc_ref, kp_ref, kq_ref, kn_ref, vp_ref, vq_ref, vn_ref, out_ref):
    i = pl.program_id(1)
    nb = pl.num_programs(1)
    kc = kc_ref[0].astype(BF16)
    vc = vc_ref[0].astype(BF16)
    kp, kq, kn = kp_ref[...].astype(BF16), kq_ref[...].astype(BF16), kn_ref[...].astype(BF16)
    vp, vq, vn = vp_ref[...].astype(BF16), vq_ref[...].astype(BF16), vn_ref[...].astype(BF16)
    rows = A_REP * BLOCK
    qpos = lax.broadcasted_iota(I32, (rows, BLOCK), 0) % BLOCK
    kpos = lax.broadcasted_iota(I32, (rows, BLOCK), 1)
    mask_p = (kpos >= qpos) & (i > 0)
    mask_n = (kpos <= qpos) & (i < nb - 1)
    for grp in range(A_KV):
        q = q_ref[grp * A_REP:(grp + 1) * A_REP].reshape(rows, KV_WIDTH)
        s_c = _dot_t(q, kc)
        s_p = jnp.where(mask_p, _dot_t(q, kp), NEG)
        s_q = _dot_t(q, kq)
        s_n = jnp.where(mask_n, _dot_t(q, kn), NEG)
        sk = _sink_column(sink_ref, grp, BLOCK)
        mx = jnp.maximum(jnp.maximum(jnp.max(s_c, axis=1, keepdims=True), jnp.max(s_p, axis=1, keepdims=True)),
                         jnp.maximum(jnp.max(s_q, axis=1, keepdims=True), jnp.max(s_n, axis=1, keepdims=True)))
        mx = jnp.maximum(mx, sk)
        p_c, p_p, p_q, p_n = jnp.exp(s_c - mx), jnp.exp(s_p - mx), jnp.exp(s_q - mx), jnp.exp(s_n - mx)
        den = (jnp.sum(p_c, axis=1, keepdims=True) + jnp.sum(p_p, axis=1, keepdims=True)
               + jnp.sum(p_q, axis=1, keepdims=True) + jnp.sum(p_n, axis=1, keepdims=True) + jnp.exp(sk - mx))
        o = (jnp.dot(p_c.astype(BF16), vc, preferred_element_type=F32)
             + jnp.dot(p_p.astype(BF16), vp, preferred_element_type=F32)
             + jnp.dot(p_q.astype(BF16), vq, preferred_element_type=F32)
             + jnp.dot(p_n.astype(BF16), vn, preferred_element_type=F32)) / den
        _store_heads(out_ref, o, grp, BLOCK)


def _attn_lat_call(sink, aq, ak, av, kc, vc, batch, seq):
    n = ak.shape[0]
    nb = seq // BLOCK
    past = kc.shape[1]

    def cur(b, i):
        return (b * nb + i, 0)

    def prev(b, i):
        return (b * nb + jnp.maximum(i - 1, 0), 0)

    def nxt(b, i):
        return (b * nb + jnp.minimum(i + 1, nb - 1), 0)

    blk = functools.partial(pl.BlockSpec, (BLOCK, KV_WIDTH))
    cache = pl.BlockSpec((1, past, KV_WIDTH), lambda b, i: (b, 0, 0))
    return pl.pallas_call(
        _attn_lat_kernel, grid=(batch, nb),
        in_specs=[pl.BlockSpec(memory_space=pltpu.SMEM),
                  pl.BlockSpec((A_HEADS, BLOCK, KV_WIDTH), lambda b, i: (0, b * nb + i, 0)),
                  cache, cache, blk(prev), blk(cur), blk(nxt), blk(prev), blk(cur), blk(nxt)],
        out_specs=pl.BlockSpec((BLOCK, A_WIDTH), cur),
        out_shape=jax.ShapeDtypeStruct((n, A_WIDTH), BF16),
        compiler_params=pltpu.CompilerParams(dimension_semantics=("parallel", "parallel")),
        name="attn_lat",
    )(sink, aq, kc, vc, ak, ak, ak, av, av, av)


def _top16_rows(s, payload=None):
    n_rows = s.shape[0]
    rows = lax.broadcasted_iota(I32, s.shape, 0).astype(F32)
    vals, idxs, pays = [], [], []
    for _ in range(P_TOPK):
        mx = jnp.max(s, axis=0, keepdims=True)
        ix = jnp.min(jnp.where(s == mx, rows, float(n_rows)), axis=0, keepdims=True)
        hit = rows == ix
        vals.append(mx)
        idxs.append(ix)
        if payload is not None:
            pays.append(jnp.sum(jnp.where(hit, payload, 0.0), axis=0, keepdims=True))
        s = jnp.where(hit, -jnp.inf, s)
    out = (jnp.concatenate(vals, axis=0), jnp.concatenate(idxs, axis=0))
    if payload is not None:
        out += (jnp.concatenate(pays, axis=0),)
    return out


def _mix_kernel(x_ref, hf_ref, hb_ref, mo_ref, ao_ref, mod_ref, mhg_ref, n2_ref, wm_ref, wa_ref, wq_ref,
                sa_ref, sb_ref, x1_ref, h2_ref, eidx_ref, gate_ref, qp_scr, e_scr, g_scr):
    tm = x_ref.shape[0]
    hs = hf_ref[...] + hb_ref[...]
    parts = []
    for hd in range(M_HEADS):
        blk = hs[:, hd * M_DH:(hd + 1) * M_DH]
        parts.append(blk * lax.rsqrt(jnp.mean(blk * blk, axis=-1, keepdims=True) + EPS))
    m_out = _sigmoid(mo_ref[...]) * (jnp.concatenate(parts, axis=1) * mhg_ref[...])
    mix = (jnp.dot(m_out.astype(BF16), wm_ref[...], preferred_element_type=F32)
           + jnp.dot(ao_ref[...], wa_ref[...], preferred_element_type=F32))
    x1 = x_ref[...] + mod_ref[0, 2:3, :] * mix
    x1_ref[...] = x1
    h2 = x1 * lax.rsqrt(jnp.mean(x1 * x1, axis=-1, keepdims=True) + EPS) * n2_ref[...]
    h2 = h2 * (1.0 + mod_ref[0, 4:5, :]) + mod_ref[0, 3:4, :]
    h2_ref[...] = _pack_bf16_pairs(h2)
    qp = jnp.dot(h2.astype(BF16), wq_ref[...], preferred_element_type=F32)
    for p in range(P_HEADS):
        qp_scr[p] = qp[:, p * P_DKEY:(p + 1) * P_DKEY].astype(BF16)
    sub_a = sa_ref[...]
    sub_b = sb_ref[...]

    def head_body(p, carry):
        for half in range(tm // N_KEYS):
            cols = slice(half * N_KEYS, (half + 1) * N_KEYS)
            qh = qp_scr[p, pl.ds(half * N_KEYS, N_KEYS), :]
            s_a = _dot_t(sub_a, qh[:, 0:P_HALF])
            s_b = _dot_t(sub_b, qh[:, P_HALF:P_DKEY])
            va, ia = _top16_rows(s_a)
            vb, ib = _top16_rows(s_b)
            cand = jnp.concatenate([va[i:i + 1, :] + vb for i in range(P_TOPK)], axis=0)
            cidx = jnp.concatenate([ia[i:i + 1, :] * float(N_KEYS) + ib for i in range(P_TOPK)], axis=0)
            top, _, eidx = _top16_rows(cand, cidx)
            ex = jnp.exp(top - jnp.max(top, axis=0, keepdims=True))
            gates = ex / jnp.sum(ex, axis=0, keepdims=True)
            r0 = pl.multiple_of(p * P_TOPK, P_TOPK)
            e_scr[pl.ds(r0, P_TOPK), cols] = eidx
            g_scr[pl.ds(r0, P_TOPK), cols] = gates
        return carry

    lax.fori_loop(0, P_HEADS, head_body, 0)
    for half in range(tm // N_KEYS):
        cols = slice(half * N_KEYS, (half + 1) * N_KEYS)
        eidx_ref[cols, :] = e_scr[:, cols].T.astype(I32)
        gate_ref[cols, :] = g_scr[:, cols].T


def _mix_call(x2d, seq, h_f, h_b, mo, a_out, mod3, mod_row0, mhg, n2, w_m, w_a, w_q, sub_a, sub_b, tile0, n_tiles):
    tm = TOKEN_TILE
    per_seq = seq // tm
    n = n_tiles * tm

    def tok_in(i):
        return (tile0 + i, 0)

    def tok(i):
        return (i, 0)

    def const2(i):
        return (0, 0)

    in_specs = [
        pl.BlockSpec((tm, D_MODEL), tok_in),
        pl.BlockSpec((tm, M_WIDTH), tok_in), pl.BlockSpec((tm, M_WIDTH), tok_in),
        pl.BlockSpec((tm, M_WIDTH), tok_in),
        pl.BlockSpec((tm, A_WIDTH), tok_in),
        pl.BlockSpec((1, 6, D_MODEL), lambda i: (mod_row0 + ((tile0 + i) // per_seq if mod_row0 else 0), 0, 0)),
        pl.BlockSpec((1, M_WIDTH), const2),
        pl.BlockSpec((1, D_MODEL), const2),
        pl.BlockSpec((M_WIDTH, D_MODEL), const2),
        pl.BlockSpec((A_WIDTH, D_MODEL), const2),
        pl.BlockSpec((D_MODEL, P_HEADS * P_DKEY), const2),
        pl.BlockSpec((N_KEYS, P_HALF), const2),
        pl.BlockSpec((N_KEYS, P_HALF), const2),
    ]
    out_specs = [pl.BlockSpec((tm, D_MODEL), tok), pl.BlockSpec((tm, D_MODEL // 2), tok),
                 pl.BlockSpec((tm, N_SEL), tok), pl.BlockSpec((tm, N_SEL), tok)]
    out_shape = [jax.ShapeDtypeStruct((n, D_MODEL), F32), jax.ShapeDtypeStruct((n, D_MODEL // 2), I32),
                 jax.ShapeDtypeStruct((n, N_SEL), I32), jax.ShapeDtypeStruct((n, N_SEL), F32)]
    return pl.pallas_call(
        _mix_kernel, grid=(n_tiles,), in_specs=in_specs, out_specs=out_specs, out_shape=out_shape,
        scratch_shapes=[pltpu.VMEM((P_HEADS, tm, P_DKEY), BF16), pltpu.VMEM((N_SEL, tm), F32),
                        pltpu.VMEM((N_SEL, tm), F32)],
        compiler_params=pltpu.CompilerParams(dimension_semantics=("parallel",)),
        name="mix",
    )(x2d, h_f, h_b, mo, a_out, mod3, mhg, n2, w_m, w_a, w_q, sub_a, sub_b)


SC_LANES = 16
SC_CORES = 2
SC_SUBCORES = 16
SC_WORKERS = SC_CORES * SC_SUBCORES
SC_TOKENS = 32
SC_GROUP = SC_LANES
SC_NGROUPS = N_SEL // SC_GROUP
SC_SLOTS = 4
ROW_WORDS = D_MODEL // 2
SC_DOT_ROWS = 8
SC_DOT_PARTIALS = 2
SC_OWORDS = 16 * SC_LANES
HI_MASK = -65536
GELU_C0 = 0.7978845608028654
GELU_C1 = 0.044715


def _pack_bf16_pairs(x):
    half = x.shape[1] // 2
    bits = lax.bitcast_convert_type(x.astype(BF16).astype(F32), I32)
    return (bits[:, :half] & HI_MASK) | lax.shift_right_logical(bits[:, half:], jnp.int32(16))


def _sc_gelu(a):
    z = GELU_C0 * (a + GELU_C1 * (a * a * a))
    tanh = 1.0 - 2.0 / (jnp.exp(2.0 * z) + 1.0)
    return 0.5 * a * (1.0 + tanh)


def _sc_split(words):
    return (plsc.bitcast(words & HI_MASK, F32), plsc.bitcast(lax.shift_left(words, jnp.int32(16)), F32))


def _sc_mul_bf16(a_words, b_words):
    return plsc.bitcast(a_words, BF16) * plsc.bitcast(b_words, BF16)


def _sc_split_sum(p, q):
    return _sc_split(plsc.bitcast(p + q, I32))


def _peer_sc_kernel(h2_hbm, eidx_hbm, gate_hbm, uv_hbm, out_hbm,
                    xbuf, ibuf, gbuf, obuf, uvbuf, mbuf, wbuf, sems):
    n = h2_hbm.shape[0]
    per_worker = n // SC_WORKERS
    wid = lax.axis_index("c") * SC_SUBCORES + lax.axis_index("s")
    lane = lax.iota(I32, SC_LANES)

    def split_item(item):
        return lax.shift_right_logical(item, SC_NGROUPS.bit_length() - 1), item & (SC_NGROUPS - 1)

    def gather_copies(item, slot):
        t, g = split_item(item)
        idx = ibuf[t, pl.ds(g * SC_GROUP, SC_GROUP)]
        return (pltpu.make_async_copy(uv_hbm.at[idx], uvbuf.at[slot], sems.at[slot]),)

    def dots(t, slot):
        zero = jnp.zeros((SC_LANES,), F32)

        @pl.loop(0, SC_GROUP, step=SC_DOT_ROWS)
        def _(r0):
            accs = [[zero] * SC_DOT_PARTIALS for _ in range(SC_DOT_ROWS)]
            for k in range(0, ROW_WORDS // SC_LANES, 4):
                xs = [xbuf[t, pl.ds((k + q) * SC_LANES, SC_LANES)] for q in range(4)]
                for i in range(SC_DOT_ROWS):
                    m = [_sc_mul_bf16(xs[q], uvbuf[slot, r0 + i, pl.ds((k + q) * SC_LANES, SC_LANES)])
                         for q in range(4)]
                    hi, lo = _sc_split_sum(m[0] + m[1], m[2] + m[3])
                    p = (k // 4) % SC_DOT_PARTIALS
                    accs[i][p] = accs[i][p] + (hi + lo)
            for i in range(SC_DOT_ROWS):
                mbuf[r0 + i, :] = functools.reduce(lambda a, b: a + b, accs[i])

        tot = zero
        for c in range(SC_LANES):
            tot = tot + plsc.load_gather(mbuf, [lane, jnp.full((SC_LANES,), c, I32)])
        return tot

    def accumulate(t, slot):
        nv = SC_OWORDS // SC_LANES
        for oc in range(ROW_WORDS // SC_OWORDS):
            w0 = oc * SC_OWORDS
            accs = (tuple(obuf[t, pl.ds(w0 + j * SC_LANES, SC_LANES)] for j in range(nv))
                    + tuple(obuf[t, pl.ds(ROW_WORDS + w0 + j * SC_LANES, SC_LANES)] for j in range(nv)))

            def row_quad(rq, accs):
                r = 4 * rq
                ws = [plsc.load_gather(wbuf, [jnp.full((SC_LANES,), r + q, I32)]) for q in range(4)]
                his, los = [], []
                for j in range(nv):
                    m = [_sc_mul_bf16(ws[q], uvbuf[slot, r + q, pl.ds(ROW_WORDS + w0 + j * SC_LANES, SC_LANES)])
                         for q in range(4)]
                    hi, lo = _sc_split_sum(m[0] + m[1], m[2] + m[3])
                    his.append(accs[j] + hi)
                    los.append(accs[nv + j] + lo)
                return tuple(his) + tuple(los)

            accs = lax.fori_loop(0, SC_GROUP // 4, row_quad, accs)
            for j in range(nv):
                obuf[t, pl.ds(w0 + j * SC_LANES, SC_LANES)] = accs[j]
                obuf[t, pl.ds(ROW_WORDS + w0 + j * SC_LANES, SC_LANES)] = accs[nv + j]

    def pack_weights(w):
        bits = plsc.bitcast(w, I32)
        rounded = (bits + 0x7FFF + (lax.shift_right_logical(bits, jnp.int32(16)) & 1)) & HI_MASK
        return rounded | lax.shift_right_logical(rounded, jnp.int32(16))

    n_items = SC_TOKENS * SC_NGROUPS

    @pl.loop(0, per_worker // SC_TOKENS)
    def _(blk):
        tok0 = pl.multiple_of(wid * per_worker + blk * SC_TOKENS, SC_TOKENS)
        pltpu.sync_copy(h2_hbm.at[pl.ds(tok0, SC_TOKENS)], xbuf)
        pltpu.sync_copy(eidx_hbm.at[pl.ds(tok0, SC_TOKENS)], ibuf)
        pltpu.sync_copy(gate_hbm.at[pl.ds(tok0, SC_TOKENS)], gbuf)

        @pl.loop(0, SC_TOKENS)
        def _(t):
            zero = jnp.zeros((SC_LANES,), F32)
            for j in range(D_MODEL // SC_LANES):
                obuf[t, pl.ds(j * SC_LANES, SC_LANES)] = zero

        for ahead in range(SC_SLOTS - 1):
            for c in gather_copies(ahead, ahead):
                c.start()

        @pl.loop(0, n_items)
        def _(item):
            t, g = split_item(item)
            slot = item & (SC_SLOTS - 1)
            ahead = item + (SC_SLOTS - 1)

            @pl.when(ahead < n_items)
            def _():
                for c in gather_copies(ahead, ahead & (SC_SLOTS - 1)):
                    c.start()

            for c in gather_copies(item, slot):
                c.wait()
            a = dots(t, slot)
            wbuf[...] = pack_weights(gbuf[t, pl.ds(g * SC_GROUP, SC_GROUP)] * _sc_gelu(a))
            accumulate(t, slot)

        pltpu.sync_copy(obuf, out_hbm.at[pl.ds(tok0, SC_TOKENS)])


def _peer_experts(h2p, eidx, gates, uv_pack):
    n = h2p.shape[0]
    mesh = plsc.VectorSubcoreMesh(core_axis_name="c", subcore_axis_name="s")
    fn = pl.kernel(
        _peer_sc_kernel,
        out_type=jax.ShapeDtypeStruct((n, D_MODEL), F32),
        mesh=mesh,
        scratch_types=[
            pltpu.VMEM((SC_TOKENS, ROW_WORDS), I32),
            pltpu.VMEM((SC_TOKENS, N_SEL), I32),
            pltpu.VMEM((SC_TOKENS, N_SEL), F32),
            pltpu.VMEM((SC_TOKENS, D_MODEL), F32),
            pltpu.VMEM((SC_SLOTS, SC_GROUP, 2 * ROW_WORDS), I32),
            pltpu.VMEM((SC_GROUP, SC_LANES), F32),
            pltpu.VMEM((SC_LANES,), I32),
            pltpu.SemaphoreType.DMA((SC_SLOTS,)),
        ],
        compiler_params=pltpu.CompilerParams(needs_layout_passes=False),
        cost_estimate=pl.CostEstimate(
            flops=4 * n * N_SEL * D_MODEL, transcendentals=n * N_SEL,
            bytes_accessed=4 * (2 * n * N_SEL * ROW_WORDS + n * ROW_WORDS + n * D_MODEL + 2 * n * N_SEL)),
        name="peer_experts",
    )
    return fn(h2p, eidx, gates, uv_pack)


def _resid_kernel(x1_ref, p_ref, mod_ref, o_ref):
    o_ref[...] = x1_ref[...] + mod_ref[0, 5:6, :] * p_ref[...]


def _resid_call(x1, peer_out, seq, mod3, mod_row0, tile0):
    n = x1.shape[0]
    tm = TOKEN_TILE
    per_seq = seq // tm
    tok = pl.BlockSpec((tm, D_MODEL), lambda i: (i, 0))
    return pl.pallas_call(
        _resid_kernel, grid=(n // tm,),
        in_specs=[tok, tok,
                  pl.BlockSpec((1, 6, D_MODEL),
                               lambda i: (mod_row0 + ((tile0 + i) // per_seq if mod_row0 else 0), 0, 0))],
        out_specs=tok, out_shape=jax.ShapeDtypeStruct((n, D_MODEL), F32),
        compiler_params=pltpu.CompilerParams(dimension_semantics=("parallel",)),
        name="resid",
    )(x1, peer_out, mod3)


def _rope_tables(seq, rotate):
    if not rotate:
        return jnp.ones((seq, KV_WIDTH), F32), jnp.zeros((seq, KV_WIDTH), F32)
    quarter = A_DH // 4
    t = jnp.arange(seq)
    row = (t // GRID_W).astype(F32)
    col = (t % GRID_W).astype(F32)
    inv = ROPE_BASE ** (-jnp.arange(quarter, dtype=F32) / quarter)
    d = jnp.arange(A_DH)
    pos = jnp.where(d[None, :] < A_DH // 2, row[:, None], col[:, None])
    ang = pos * inv[d % quarter][None, :]
    sign = jnp.where((d % (A_DH // 2)) < quarter, -1.0, 1.0).astype(F32)
    cos = jnp.cos(ang)
    sin = jnp.sin(ang) * sign[None, :]
    return jnp.tile(cos, (1, KV_WIDTH // A_DH)), jnp.tile(sin, (1, KV_WIDTH // A_DH))


def _run_chunk(x, mod3, mod_row0, prm, cache, rotate, gate_on):
    (n1, n2, w_main, w_g, w_gt, b_g, b_gt, mhg, qg_t, kg_t, bd, sink, w_m, w_a, w_q, sub_a, sub_b,
     uv_pack) = prm
    batch, seq, _ = x.shape
    n = batch * seq
    x2d = x.reshape(n, D_MODEL)
    if gate_on is not None:
        x2d, _ = lax.optimization_barrier((x2d, gate_on))
    cos, sin = _rope_tables(seq, rotate)
    mq, mk, mv, mo, gcol, grow, aq, ak, av = _inproj_call(
        x2d, seq, mod3, mod_row0, n1, w_main, w_g, w_gt, b_g, b_gt, qg_t, kg_t, bd, cos, sin)
    kc, vc, c0, m0 = cache
    h_f, h_b, c_fin, m_fin = _mlstm_call(mq, mk, mv, gcol, grow, c0, m0, batch, seq)
    if kc is None:
        a_out = _attn_ctx_call(sink, aq, ak, av, batch, seq)
    else:
        a_out = _attn_lat_call(sink, aq, ak, av, kc, vc, batch, seq)
    x1, h2p, eidx, gates = _mix_call(x2d, seq, h_f, h_b, mo, a_out, mod3, mod_row0, mhg, n2, w_m, w_a, w_q,
                                     sub_a, sub_b, 0, n // TOKEN_TILE)
    peer_out = _peer_experts(h2p, eidx, gates, uv_pack)
    y = _resid_call(x1, peer_out, seq, mod3, mod_row0, 0).reshape(batch, seq, D_MODEL)
    return y, h2p, peer_out, ak, av, c_fin, m_fin


def _pack_state(C, n_vec, m):
    b = C.shape[0]
    caug = jnp.concatenate([C, jnp.broadcast_to(n_vec[..., None], C.shape)], axis=-1)
    caug = caug.reshape(b, 2 * M_HEADS, M_DH, 2 * M_DH)
    m_rep = jnp.broadcast_to(m.reshape(b, 2 * M_HEADS, 1, 1), (b, 2 * M_HEADS, 8, M_DH))
    return caug.astype(F32), m_rep.astype(F32)


def kernel(x_prompt, x_sample, c, cache_attn_k, cache_attn_v, state_mlstm_C, state_mlstm_n, state_mlstm_m,
           c_ctx, w_ada, b_ada, norm1_g, norm2_g, w_in, b_gates, mh_norm_g, q_norm_g, k_norm_g, sink_logits,
           w_out, peer_w_q, peer_sub_a, peer_sub_b, peer_u, peer_v):
    depth = w_ada.shape[0]
    assert depth == 1
    batch, seq, _ = x_prompt.shape
    dec_batch, dec_seq, _ = x_sample.shape
    l = 0

    cond = jnp.concatenate([c_ctx[None, :], c, jnp.zeros((MOD_ROWS - 1 - dec_batch, D_MODEL), F32)], axis=0)
    mod3 = _ada_call(cond, w_ada[l], b_ada[l]).reshape(MOD_ROWS, 6, D_MODEL)

    wi = w_in[l]
    g0 = 4 * M_WIDTH
    w_main = jnp.concatenate([wi[:, :g0], wi[:, g0 + N_GATES:]], axis=1).astype(BF16)
    w_g = wi[:, g0:g0 + N_GATES]
    seg = jnp.arange(A_WIDTH) // A_DH
    bd = jnp.where(seg[:, None] == seg[None, :], 1.0 / A_DH, 0.0).astype(F32)
    prm = (norm1_g[l][None, :], norm2_g[l][None, :], w_main, w_g, w_g.T, b_gates[l][None, :], b_gates[l][:, None],
           mh_norm_g[l][None, :], jnp.tile(q_norm_g[l], A_HEADS)[None, :], jnp.tile(k_norm_g[l], A_KV)[None, :], bd,
           sink_logits[l], w_out[l][:M_WIDTH].astype(BF16), w_out[l][M_WIDTH:].astype(BF16),
           peer_w_q[l].astype(BF16), peer_sub_a[l].astype(BF16), peer_sub_b[l].astype(BF16),
           jnp.concatenate([_pack_bf16_pairs(peer_u[l]), _pack_bf16_pairs(peer_v[l])], axis=1))

    zeros_c = jnp.zeros((batch, 2, M_HEADS, M_DH, M_DH), F32)
    c0, m0 = _pack_state(zeros_c, zeros_c[..., 0], jnp.full((batch, 2, M_HEADS), NEG, F32))
    c0s, m0s = _pack_state(state_mlstm_C[:, l], state_mlstm_n[:, l], state_mlstm_m[:, l])
    past = cache_attn_k.shape[2]
    kc = cache_attn_k[:, l].reshape(dec_batch, past, KV_WIDTH)
    vc = cache_attn_v[:, l].reshape(dec_batch, past, KV_WIDTH)

    jobs = []
    for b0 in range(0, batch, batch // CTX_CHUNKS):
        b1 = b0 + batch // CTX_CHUNKS
        jobs.append((x_prompt[b0:b1], 0, (None, None, c0[b0:b1], m0[b0:b1]), False))
    for b0 in range(0, dec_batch, dec_batch // LATENT_CHUNKS):
        b1 = b0 + dec_batch // LATENT_CHUNKS
        jobs.append((x_sample[b0:b1], 1 + b0, (kc[b0:b1], vc[b0:b1], c0s[b0:b1], m0s[b0:b1]), True))
    outs = []
    for i, (x_c, mod_row0, cache, rotate) in enumerate(jobs):
        gate = tuple(g for g in (outs[i - 1][1] if i >= 1 else None,
                                 outs[i - EXPERT_LAG][2] if i >= EXPERT_LAG else None) if g is not None)
        outs.append(_run_chunk(x_c, mod3, mod_row0, prm, cache, rotate, gate or None))
    ctx, lat = outs[:CTX_CHUNKS], outs[CTX_CHUNKS:]
    y_p = jnp.concatenate([o[0] for o in ctx], axis=0)
    y_s = jnp.concatenate([o[0] for o in lat], axis=0)
    k_new = jnp.concatenate([o[3] for o in ctx], axis=0)
    v_new = jnp.concatenate([o[4] for o in ctx], axis=0)
    c_fin = jnp.concatenate([o[5] for o in ctx], axis=0)
    m_fin = jnp.concatenate([o[6] for o in ctx], axis=0)

    c_fin = c_fin.reshape(batch, 2, M_HEADS, M_DH, 2 * M_DH)
    new_c = c_fin[..., :M_DH][:, None]
    new_n = c_fin[..., M_DH][:, None]
    new_m = m_fin[:, :, 0, 0].reshape(batch, 2, M_HEADS)[:, None]
    new_k = k_new.reshape(batch, 1, seq, A_KV, A_DH)
    new_v = v_new.reshape(batch, 1, seq, A_KV, A_DH)
    return y_p, y_s, new_k, new_v, new_c, new_n, new_m
```

```python
import functools
import math

import jax
import jax.numpy as jnp
from jax import lax
from jax.experimental import pallas as pl
from jax.experimental.pallas import tpu as pltpu
from jax.experimental.pallas import tpu_sc as plsc

F32 = jnp.float32
BF16 = jnp.bfloat16
I32 = jnp.int32
HI = lax.Precision.HIGHEST

D_MODEL = 1024
EPS = 1e-6
NEG = -1e30
GRID_W = 64
M_HEADS = 4
M_WIDTH = 512
M_DH = 128
A_HEADS = 8
A_KV = 2
A_REP = 4
A_DH = 64
A_WIDTH = 512
KV_WIDTH = A_KV * A_DH
BLOCK = 128
ROPE_BASE = 10000.0
N_KEYS = 128
P_HEADS = 8
P_DKEY = 256
P_HALF = 128
P_TOPK = 16
N_SEL = P_HEADS * P_TOPK
N_GATES = 4 * M_HEADS
MAIN_COLS = 4 * M_WIDTH + A_WIDTH + 2 * KV_WIDTH
MOD_ROWS = 16

TOKEN_TILE = 256
MLSTM_CHUNK = 128
ADA_COL_TILE = 768
CTX_CHUNK_WEIGHTS = (1, 2, 3, 4, 6)
EXPERT_LAG = 3
LATENT_CHUNKS = 8


def _sigmoid(x):
    return 1.0 / (1.0 + jnp.exp(-x))


def _log_sigmoid(x):
    return jnp.minimum(x, 0.0) - jnp.log1p(jnp.exp(-jnp.abs(x)))


def _dot_t(a, b, precision=None):
    return lax.dot_general(a, b, (((1,), (1,)), ((), ())), precision=precision,
                           preferred_element_type=F32)


def _ada_kernel(c_ref, w_ref, b_ref, o_ref):
    c = c_ref[...]
    s = c * _sigmoid(c)
    o_ref[...] = jnp.dot(s, w_ref[...], precision=HI, preferred_element_type=F32) + b_ref[...]


def _ada_call(cond, w_ada, b_ada):
    n_out = w_ada.shape[1]
    return pl.pallas_call(
        _ada_kernel,
        grid=(n_out // ADA_COL_TILE,),
        in_specs=[pl.BlockSpec((MOD_ROWS, D_MODEL), lambda j: (0, 0)),
                  pl.BlockSpec((D_MODEL, ADA_COL_TILE), lambda j: (0, j)),
                  pl.BlockSpec((1, ADA_COL_TILE), lambda j: (0, j))],
        out_specs=pl.BlockSpec((MOD_ROWS, ADA_COL_TILE), lambda j: (0, j)),
        out_shape=jax.ShapeDtypeStruct((MOD_ROWS, n_out), F32),
        name="ada",
    )(cond, w_ada, b_ada.reshape(1, n_out))


def _swap16(x):
    n = x.shape[-1]
    lane = lax.broadcasted_iota(I32, x.shape, x.ndim - 1)
    return jnp.where((lane & 16) == 0, pltpu.roll(x, n - 16, x.ndim - 1), pltpu.roll(x, 16, x.ndim - 1))


def _inproj_kernel(x_ref, mod_ref, n1_ref, w_ref, wg_ref, wgt_ref, bg_ref, bgt_ref, qg_ref, kg_ref,
                   bd_ref, cos_ref, sin_ref,
                   mq_ref, mk_ref, mv_ref, mo_ref, gc_ref, gr_ref, aq_ref, ak_ref, av_ref):
    x = x_ref[...]
    h = x * lax.rsqrt(jnp.mean(x * x, axis=-1, keepdims=True) + EPS) * n1_ref[...]
    h = h * (1.0 + mod_ref[0, 1:2, :]) + mod_ref[0, 0:1, :]
    z = jnp.dot(h.astype(BF16), w_ref[...], preferred_element_type=F32)

    mq_ref[...] = (z[:, 0:M_WIDTH] * (M_DH ** -0.5)).astype(BF16)
    mk_ref[...] = z[:, M_WIDTH:2 * M_WIDTH].astype(BF16)
    mv_ref[...] = z[:, 2 * M_WIDTH:3 * M_WIDTH].astype(BF16)
    mo_ref[...] = z[:, 3 * M_WIDTH:4 * M_WIDTH]

    g = jnp.dot(h, wg_ref[...], precision=HI, preferred_element_type=F32) + bg_ref[...]
    kind = lax.broadcasted_iota(I32, g.shape, 1) // M_HEADS
    gc_ref[...] = jnp.where((kind & 1) == 1, _log_sigmoid(g), g)
    gt = _dot_t(wgt_ref[...], h, precision=HI) + bgt_ref[...]
    kind_t = lax.broadcasted_iota(I32, gt.shape, 0) // M_HEADS
    gr_ref[...] = jnp.where((kind_t & 1) == 1, _log_sigmoid(gt), gt)

    o = 4 * M_WIDTH
    aq = z[:, o:o + A_WIDTH]
    ak = z[:, o + A_WIDTH:o + A_WIDTH + KV_WIDTH]
    av_ref[...] = z[:, o + A_WIDTH + KV_WIDTH:o + A_WIDTH + 2 * KV_WIDTH]
    bd = bd_ref[...]
    cos = cos_ref[...]
    sin = sin_ref[...]
    aq = aq * lax.rsqrt(jnp.dot(aq * aq, bd, precision=HI, preferred_element_type=F32) + EPS) * qg_ref[...]
    cos4 = jnp.concatenate([cos] * (A_WIDTH // KV_WIDTH), axis=1)
    sin4 = jnp.concatenate([sin] * (A_WIDTH // KV_WIDTH), axis=1)
    aq = (aq * cos4 + _swap16(aq) * sin4) * (A_DH ** -0.5)
    ak = ak * lax.rsqrt(jnp.dot(ak * ak, bd[0:KV_WIDTH, 0:KV_WIDTH], precision=HI,
                                preferred_element_type=F32) + EPS) * kg_ref[...]
    ak_ref[...] = ak * cos + _swap16(ak) * sin

    lane = lax.broadcasted_iota(I32, (aq.shape[0], KV_WIDTH), 1)
    for hd in range(A_HEADS):
        grp = hd // A_REP
        blk = aq[:, (hd // 2) * KV_WIDTH:(hd // 2 + 1) * KV_WIDTH]
        if hd % 2 != grp:
            blk = pltpu.roll(blk, A_DH, 1)
        keep = (lane >= grp * A_DH) & (lane < (grp + 1) * A_DH)
        aq_ref[hd] = jnp.where(keep, blk, 0.0).astype(BF16)


def _inproj_call(x2d, seq, mod3, mod_row0, n1, w_main, w_g, w_gt, b_g, b_gt, qg_t, kg_t, bd, cos, sin):
    n = x2d.shape[0]
    tm = TOKEN_TILE
    per_seq = seq // tm

    def tok(i):
        return (i, 0)

    def const2(i):
        return (0, 0)

    in_specs = [
        pl.BlockSpec((tm, D_MODEL), tok),
        pl.BlockSpec((1, 6, D_MODEL), lambda i: (mod_row0 + (i // per_seq if mod_row0 else 0), 0, 0)),
        pl.BlockSpec((1, D_MODEL), const2),
        pl.BlockSpec((D_MODEL, MAIN_COLS), const2),
        pl.BlockSpec((D_MODEL, N_GATES), const2),
        pl.BlockSpec((N_GATES, D_MODEL), const2),
        pl.BlockSpec((1, N_GATES), const2),
        pl.BlockSpec((N_GATES, 1), const2),
        pl.BlockSpec((1, A_WIDTH), const2),
        pl.BlockSpec((1, KV_WIDTH), const2),
        pl.BlockSpec((A_WIDTH, A_WIDTH), const2),
        pl.BlockSpec((tm, KV_WIDTH), lambda i: (i % per_seq, 0)),
        pl.BlockSpec((tm, KV_WIDTH), lambda i: (i % per_seq, 0)),
    ]
    out_specs = [
        pl.BlockSpec((tm, M_WIDTH), tok),
        pl.BlockSpec((tm, M_WIDTH), tok),
        pl.BlockSpec((tm, M_WIDTH), tok),
        pl.BlockSpec((tm, M_WIDTH), tok),
        pl.BlockSpec((tm, N_GATES), tok),
        pl.BlockSpec((N_GATES, tm), lambda i: (0, i)),
        pl.BlockSpec((A_HEADS, tm, KV_WIDTH), lambda i: (0, i, 0)),
        pl.BlockSpec((tm, KV_WIDTH), tok),
        pl.BlockSpec((tm, KV_WIDTH), tok),
    ]
    out_shape = [
        jax.ShapeDtypeStruct((n, M_WIDTH), BF16),
        jax.ShapeDtypeStruct((n, M_WIDTH), BF16),
        jax.ShapeDtypeStruct((n, M_WIDTH), BF16),
        jax.ShapeDtypeStruct((n, M_WIDTH), F32),
        jax.ShapeDtypeStruct((n, N_GATES), F32),
        jax.ShapeDtypeStruct((N_GATES, n), F32),
        jax.ShapeDtypeStruct((A_HEADS, n, KV_WIDTH), BF16),
        jax.ShapeDtypeStruct((n, KV_WIDTH), F32),
        jax.ShapeDtypeStruct((n, KV_WIDTH), F32),
    ]
    return pl.pallas_call(
        _inproj_kernel, grid=(n // tm,), in_specs=in_specs, out_specs=out_specs, out_shape=out_shape,
        compiler_params=pltpu.CompilerParams(dimension_semantics=("parallel",)),
        name="inproj",
    )(x2d, mod3, n1, w_main, w_g, w_gt, b_g, b_gt, qg_t, kg_t, bd, cos, sin)


def _mlstm_chain(q, k, v, li_c, lf_c, li_r, lf_r, caug, m, tri, tri_t, mask, reverse):
    L = q.shape[0]
    last = 0 if reverse else L - 1
    b_c = jnp.dot(tri, jnp.broadcast_to(lf_c, (L, L)), precision=HI, preferred_element_type=F32)
    b_r = jnp.dot(jnp.broadcast_to(lf_r, (8, L)), tri_t, precision=HI, preferred_element_type=F32)[0:1, :]
    a_inter = b_c[:, 0:1] + m
    d = jnp.where(mask, b_c - b_r + li_r, -jnp.inf)
    m_t = jnp.maximum(a_inter, jnp.max(d, axis=1, keepdims=True))
    w_inter = jnp.exp(a_inter - m_t)
    s = _dot_t(q, k) * jnp.exp(d - m_t)
    qc = jnp.dot(q, caug.astype(BF16), preferred_element_type=F32)
    num = jnp.dot(s.astype(BF16), v, preferred_element_type=F32) + w_inter * qc[:, 0:M_DH]
    den = jnp.sum(s, axis=1, keepdims=True) + w_inter * qc[:, M_DH:M_DH + 1]
    den = jnp.maximum(jnp.abs(den), jnp.exp(-m_t))
    h = num / den
    m_new = m_t[last:last + 1, :]
    b_last = b_c[last:last + 1, 0:1]
    g_c = jnp.exp(b_last - b_c[:, 0:1] + li_c - m_new)
    decay = jnp.exp(b_last + m - m_new)
    kw = (k.astype(F32) * g_c).astype(BF16)
    vaug = jnp.concatenate([v, jnp.ones_like(v)], axis=1)
    upd = lax.dot_general(kw, vaug, (((0,), (0,)), ((), ())), preferred_element_type=F32)
    return h, decay * caug + upd, m_new


def _mlstm_kernel(qf_ref, kf_ref, vf_ref, gcf_ref, grf_ref, qb_ref, kb_ref, vb_ref, gcb_ref, grb_ref,
                  c0_ref, m0_ref, hf_ref, hb_ref, cfin_ref, mfin_ref, c_scr, m_scr):
    c = pl.program_id(1)
    nc = pl.num_programs(1)
    L = qf_ref.shape[0]

    @pl.when(c == 0)
    def _():
        c_scr[...] = c0_ref[0]
        m_scr[...] = m0_ref[0]

    row = lax.broadcasted_iota(I32, (L, L), 0)
    col = lax.broadcasted_iota(I32, (L, L), 1)
    lower = row >= col
    upper = row <= col
    lower_f = lower.astype(F32)
    upper_f = upper.astype(F32)

    for direction in range(2):
        reverse = direction == 1
        q_ref, k_ref, v_ref, gc_ref, gr_ref, h_ref = (
            (qb_ref, kb_ref, vb_ref, gcb_ref, grb_ref, hb_ref) if reverse
            else (qf_ref, kf_ref, vf_ref, gcf_ref, grf_ref, hf_ref))
        tri, tri_t, mask = (upper_f, lower_f, upper) if reverse else (lower_f, upper_f, lower)
        gc = gc_ref[...]
        gr = gr_ref[...]
        for hd in range(M_HEADS):
            ch = direction * M_HEADS + hd
            sl = slice(hd * M_DH, (hd + 1) * M_DH)
            ci = 2 * direction * M_HEADS + hd
            cf = ci + M_HEADS
            h, caug, m_new = _mlstm_chain(
                q_ref[:, sl], k_ref[:, sl], v_ref[:, sl],
                gc[:, ci:ci + 1], gc[:, cf:cf + 1], gr[ci:ci + 1, :], gr[cf:cf + 1, :],
                c_scr[ch], m_scr[ch][0:1, 0:1], tri, tri_t, mask, reverse)
            h_ref[:, sl] = h
            c_scr[ch] = caug
            m_scr[ch] = jnp.broadcast_to(m_new, m_scr.shape[1:])

    @pl.when(c == nc - 1)
    def _():
        cfin_ref[0] = c_scr[...]
        mfin_ref[0] = m_scr[...]


def _mlstm_call(mq, mk, mv, gcol, grow, c0, m0, batch, seq):
    n = mq.shape[0]
    L = MLSTM_CHUNK
    nc = seq // L
    n_ch = 2 * M_HEADS

    def fwd(b, c):
        return (b * nc + c, 0)

    def bwd(b, c):
        return (b * nc + nc - 1 - c, 0)

    def fwd_t(b, c):
        return (0, b * nc + c)

    def bwd_t(b, c):
        return (0, b * nc + nc - 1 - c)

    tok = pl.BlockSpec((L, M_WIDTH), fwd)
    tok_b = pl.BlockSpec((L, M_WIDTH), bwd)
    in_specs = [tok, tok, tok, pl.BlockSpec((L, N_GATES), fwd), pl.BlockSpec((N_GATES, L), fwd_t),
                tok_b, tok_b, tok_b, pl.BlockSpec((L, N_GATES), bwd), pl.BlockSpec((N_GATES, L), bwd_t),
                pl.BlockSpec((1, n_ch, M_DH, 2 * M_DH), lambda b, c: (b, 0, 0, 0)),
                pl.BlockSpec((1, n_ch, 8, M_DH), lambda b, c: (b, 0, 0, 0))]
    out_specs = [tok, tok_b,
                 pl.BlockSpec((1, n_ch, M_DH, 2 * M_DH), lambda b, c: (b, 0, 0, 0)),
                 pl.BlockSpec((1, n_ch, 8, M_DH), lambda b, c: (b, 0, 0, 0))]
    out_shape = [jax.ShapeDtypeStruct((n, M_WIDTH), F32), jax.ShapeDtypeStruct((n, M_WIDTH), F32),
                 jax.ShapeDtypeStruct((batch, n_ch, M_DH, 2 * M_DH), F32),
                 jax.ShapeDtypeStruct((batch, n_ch, 8, M_DH), F32)]
    return pl.pallas_call(
        _mlstm_kernel, grid=(batch, nc), in_specs=in_specs, out_specs=out_specs, out_shape=out_shape,
        scratch_shapes=[pltpu.VMEM((n_ch, M_DH, 2 * M_DH), F32), pltpu.VMEM((n_ch, 8, M_DH), F32)],
        compiler_params=pltpu.CompilerParams(dimension_semantics=("parallel", "arbitrary")),
        name="mlstm",
    )(mq, mk, mv, gcol, grow, mq, mk, mv, gcol, grow, c0, m0)


def _sink_column(sink_ref, grp, rows_per_head):
    return jnp.concatenate(
        [jnp.full((rows_per_head, 1), sink_ref[grp * A_REP + r], F32) for r in range(A_REP)], axis=0)


def _store_heads(out_ref, o, grp, rows_per_head):
    for r in range(A_REP):
        hd = grp * A_REP + r
        out_ref[:, hd * A_DH:(hd + 1) * A_DH] = o[r * rows_per_head:(r + 1) * rows_per_head,
                                                  grp * A_DH:(grp + 1) * A_DH].astype(out_ref.dtype)


def _attn_ctx_kernel(sink_ref, q_ref, k_ref, v_ref, out_ref):
    s_len = k_ref.shape[0]
    k = k_ref[...].astype(BF16)
    v = v_ref[...].astype(BF16)
    for grp in range(A_KV):
        q = q_ref[grp * A_REP:(grp + 1) * A_REP].reshape(A_REP * s_len, KV_WIDTH)
        s = _dot_t(q, k)
        sk = _sink_column(sink_ref, grp, s_len)
        mx = jnp.maximum(jnp.max(s, axis=1, keepdims=True), sk)
        p = jnp.exp(s - mx)
        den = jnp.sum(p, axis=1, keepdims=True) + jnp.exp(sk - mx)
        o = jnp.dot(p.astype(BF16), v, preferred_element_type=F32) / den
        _store_heads(out_ref, o, grp, s_len)


def _attn_ctx_call(sink, aq, ak, av, batch, seq):
    n = ak.shape[0]
    return pl.pallas_call(
        _attn_ctx_kernel, grid=(batch,),
        in_specs=[pl.BlockSpec(memory_space=pltpu.SMEM),
                  pl.BlockSpec((A_HEADS, seq, KV_WIDTH), lambda b: (0, b, 0)),
                  pl.BlockSpec((seq, KV_WIDTH), lambda b: (b, 0)),
                  pl.BlockSpec((seq, KV_WIDTH), lambda b: (b, 0))],
        out_specs=pl.BlockSpec((seq, A_WIDTH), lambda b: (b, 0)),
        out_shape=jax.ShapeDtypeStruct((n, A_WIDTH), BF16),
        compiler_params=pltpu.CompilerParams(dimension_semantics=("parallel",)),
        name="attn_ctx",
    )(sink, aq, ak, av)


def _attn_lat_kernel(sink_ref, q_ref, kc_ref, vc_ref, kp_ref, kq_ref, kn_ref, vp_ref, vq_ref, vn_ref, out_ref):
    i = pl.program_id(1)
    nb = pl.num_programs(1)
    kc = kc_ref[0].astype(BF16)
    vc = vc_ref[0].astype(BF16)
    kp, kq, kn = kp_ref[...].astype(BF16), kq_ref[...].astype(BF16), kn_ref[...].astype(BF16)
    vp, vq, vn = vp_ref[...].astype(BF16), vq_ref[...].astype(BF16), vn_ref[...].astype(BF16)
    rows = A_REP * BLOCK
    qpos = lax.broadcasted_iota(I32, (rows, BLOCK), 0) % BLOCK
    kpos = lax.broadcasted_iota(I32, (rows, BLOCK), 1)
    mask_p = (kpos >= qpos) & (i > 0)
    mask_n = (kpos <= qpos) & (i < nb - 1)
    for grp in range(A_KV):
        q = q_ref[grp * A_REP:(grp + 1) * A_REP].reshape(rows, KV_WIDTH)
        s_c = _dot_t(q, kc)
        s_p = jnp.where(mask_p, _dot_t(q, kp), NEG)
        s_q = _dot_t(q, kq)
        s_n = jnp.where(mask_n, _dot_t(q, kn), NEG)
        sk = _sink_column(sink_ref, grp, BLOCK)
        mx = jnp.maximum(jnp.maximum(jnp.max(s_c, axis=1, keepdims=True), jnp.max(s_p, axis=1, keepdims=True)),
                         jnp.maximum(jnp.max(s_q, axis=1, keepdims=True), jnp.max(s_n, axis=1, keepdims=True)))
        mx = jnp.maximum(mx, sk)
        p_c, p_p, p_q, p_n = jnp.exp(s_c - mx), jnp.exp(s_p - mx), jnp.exp(s_q - mx), jnp.exp(s_n - mx)
        den = (jnp.sum(p_c, axis=1, keepdims=True) + jnp.sum(p_p, axis=1, keepdims=True)
               + jnp.sum(p_q, axis=1, keepdims=True) + jnp.sum(p_n, axis=1, keepdims=True) + jnp.exp(sk - mx))
        o = (jnp.dot(p_c.astype(BF16), vc, preferred_element_type=F32)
             + jnp.dot(p_p.astype(BF16), vp, preferred_element_type=F32)
             + jnp.dot(p_q.astype(BF16), vq, preferred_element_type=F32)
             + jnp.dot(p_n.astype(BF16), vn, preferred_element_type=F32)) / den
        _store_heads(out_ref, o, grp, BLOCK)


def _attn_lat_call(sink, aq, ak, av, kc, vc, batch, seq):
    n = ak.shape[0]
    nb = seq // BLOCK
    past = kc.shape[1]

    def cur(b, i):
        return (b * nb + i, 0)

    def prev(b, i):
        return (b * nb + jnp.maximum(i - 1, 0), 0)

    def nxt(b, i):
        return (b * nb + jnp.minimum(i + 1, nb - 1), 0)

    blk = functools.partial(pl.BlockSpec, (BLOCK, KV_WIDTH))
    cache = pl.BlockSpec((1, past, KV_WIDTH), lambda b, i: (b, 0, 0))
    return pl.pallas_call(
        _attn_lat_kernel, grid=(batch, nb),
        in_specs=[pl.BlockSpec(memory_space=pltpu.SMEM),
                  pl.BlockSpec((A_HEADS, BLOCK, KV_WIDTH), lambda b, i: (0, b * nb + i, 0)),
                  cache, cache, blk(prev), blk(cur), blk(nxt), blk(prev), blk(cur), blk(nxt)],
        out_specs=pl.BlockSpec((BLOCK, A_WIDTH), cur),
        out_shape=jax.ShapeDtypeStruct((n, A_WIDTH), BF16),
        compiler_params=pltpu.CompilerParams(dimension_semantics=("parallel", "parallel")),
        name="attn_lat",
    )(sink, aq, kc, vc, ak, ak, ak, av, av, av)


def _top16_rows(s, payload=None):
    n_rows = s.shape[0]
    rows = lax.broadcasted_iota(I32, s.shape, 0).astype(F32)
    vals, idxs, pays = [], [], []
    for _ in range(P_TOPK):
        mx = jnp.max(s, axis=0, keepdims=True)
        ix = jnp.min(jnp.where(s == mx, rows, float(n_rows)), axis=0, keepdims=True)
        hit = rows == ix
        vals.append(mx)
        idxs.append(ix)
        if payload is not None:
            pays.append(jnp.sum(jnp.where(hit, payload, 0.0), axis=0, keepdims=True))
        s = jnp.where(hit, -jnp.inf, s)
    out = (jnp.concatenate(vals, axis=0), jnp.concatenate(idxs, axis=0))
    if payload is not None:
        out += (jnp.concatenate(pays, axis=0),)
    return out


def _mix_kernel(x_ref, hf_ref, hb_ref, mo_ref, ao_ref, mod_ref, mhg_ref, n2_ref, wm_ref, wa_ref, wq_ref,
                sa_ref, sb_ref, x1_ref, h2_ref, eidx_ref, gate_ref, qp_scr, e_scr, g_scr):
    tm = x_ref.shape[0]
    hs = hf_ref[...] + hb_ref[...]
    parts = []
    for hd in range(M_HEADS):
        blk = hs[:, hd * M_DH:(hd + 1) * M_DH]
        parts.append(blk * lax.rsqrt(jnp.mean(blk * blk, axis=-1, keepdims=True) + EPS))
    m_out = _sigmoid(mo_ref[...]) * (jnp.concatenate(parts, axis=1) * mhg_ref[...])
    mix = (jnp.dot(m_out.astype(BF16), wm_ref[...], preferred_element_type=F32)
           + jnp.dot(ao_ref[...], wa_ref[...], preferred_element_type=F32))
    x1 = x_ref[...] + mod_ref[0, 2:3, :] * mix
    x1_ref[...] = x1
    h2 = x1 * lax.rsqrt(jnp.mean(x1 * x1, axis=-1, keepdims=True) + EPS) * n2_ref[...]
    h2 = h2 * (1.0 + mod_ref[0, 4:5, :]) + mod_ref[0, 3:4, :]
    h2_ref[...] = _pack_bf16_pairs(h2)
    qp = jnp.dot(h2.astype(BF16), wq_ref[...], preferred_element_type=F32)
    for p in range(P_HEADS):
        qp_scr[p] = qp[:, p * P_DKEY:(p + 1) * P_DKEY].astype(BF16)
    sub_a = sa_ref[...]
    sub_b = sb_ref[...]

    def head_body(p, carry):
        for half in range(tm // N_KEYS):
            cols = slice(half * N_KEYS, (half + 1) * N_KEYS)
            qh = qp_scr[p, pl.ds(half * N_KEYS, N_KEYS), :]
            s_a = _dot_t(sub_a, qh[:, 0:P_HALF])
            s_b = _dot_t(sub_b, qh[:, P_HALF:P_DKEY])
            va, ia = _top16_rows(s_a)
            vb, ib = _top16_rows(s_b)
            cand = jnp.concatenate([va[i:i + 1, :] + vb for i in range(P_TOPK)], axis=0)
            cidx = jnp.concatenate([ia[i:i + 1, :] * float(N_KEYS) + ib for i in range(P_TOPK)], axis=0)
            top, _, eidx = _top16_rows(cand, cidx)
            ex = jnp.exp(top - jnp.max(top, axis=0, keepdims=True))
            gates = ex / jnp.sum(ex, axis=0, keepdims=True)
            r0 = pl.multiple_of(p * P_TOPK, P_TOPK)
            e_scr[pl.ds(r0, P_TOPK), cols] = eidx
            g_scr[pl.ds(r0, P_TOPK), cols] = gates
        return carry

    lax.fori_loop(0, P_HEADS, head_body, 0)
    for half in range(tm // N_KEYS):
        cols = slice(half * N_KEYS, (half + 1) * N_KEYS)
        eidx_ref[cols, :] = e_scr[:, cols].T.astype(I32)
        gate_ref[cols, :] = g_scr[:, cols].T


def _mix_call(x2d, seq, h_f, h_b, mo, a_out, mod3, mod_row0, mhg, n2, w_m, w_a, w_q, sub_a, sub_b, tile0, n_tiles):
    tm = TOKEN_TILE
    per_seq = seq // tm
    n = n_tiles * tm

    def tok_in(i):
        return (tile0 + i, 0)

    def tok(i):
        return (i, 0)

    def const2(i):
        return (0, 0)

    in_specs = [
        pl.BlockSpec((tm, D_MODEL), tok_in),
        pl.BlockSpec((tm, M_WIDTH), tok_in), pl.BlockSpec((tm, M_WIDTH), tok_in),
        pl.BlockSpec((tm, M_WIDTH), tok_in),
        pl.BlockSpec((tm, A_WIDTH), tok_in),
        pl.BlockSpec((1, 6, D_MODEL), lambda i: (mod_row0 + ((tile0 + i) // per_seq if mod_row0 else 0), 0, 0)),
        pl.BlockSpec((1, M_WIDTH), const2),
        pl.BlockSpec((1, D_MODEL), const2),
        pl.BlockSpec((M_WIDTH, D_MODEL), const2),
        pl.BlockSpec((A_WIDTH, D_MODEL), const2),
        pl.BlockSpec((D_MODEL, P_HEADS * P_DKEY), const2),
        pl.BlockSpec((N_KEYS, P_HALF), const2),
        pl.BlockSpec((N_KEYS, P_HALF), const2),
    ]
    out_specs = [pl.BlockSpec((tm, D_MODEL), tok), pl.BlockSpec((tm, D_MODEL // 2), tok),
                 pl.BlockSpec((tm, N_SEL), tok), pl.BlockSpec((tm, N_SEL), tok)]
    out_shape = [jax.ShapeDtypeStruct((n, D_MODEL), F32), jax.ShapeDtypeStruct((n, D_MODEL // 2), I32),
                 jax.ShapeDtypeStruct((n, N_SEL), I32), jax.ShapeDtypeStruct((n, N_SEL), F32)]
    return pl.pallas_call(
        _mix_kernel, grid=(n_tiles,), in_specs=in_specs, out_specs=out_specs, out_shape=out_shape,
        scratch_shapes=[pltpu.VMEM((P_HEADS, tm, P_DKEY), BF16), pltpu.VMEM((N_SEL, tm), F32),
                        pltpu.VMEM((N_SEL, tm), F32)],
        compiler_params=pltpu.CompilerParams(dimension_semantics=("parallel",)),
        name="mix",
    )(x2d, h_f, h_b, mo, a_out, mod3, mhg, n2, w_m, w_a, w_q, sub_a, sub_b)


SC_LANES = 16
SC_CORES = 2
SC_SUBCORES = 16
SC_WORKERS = SC_CORES * SC_SUBCORES
SC_TOKENS = 32
SC_GROUP = SC_LANES
SC_NGROUPS = N_SEL // SC_GROUP
SC_SLOTS = 4
ROW_WORDS = D_MODEL // 2
SC_DOT_ROWS = 8
SC_DOT_PARTIALS = 2
SC_OWORDS = 16 * SC_LANES
HI_MASK = -65536
PACK_ROWS = 512
GELU_C0 = 0.7978845608028654
GELU_C1 = 0.044715


def _pack_bf16_pairs(x):
    half = x.shape[1] // 2
    bits = lax.bitcast_convert_type(x.astype(BF16).astype(F32), I32)
    return (bits[:, :half] & HI_MASK) | lax.shift_right_logical(bits[:, half:], jnp.int32(16))


def _pack_tables_kernel(u_ref, v_ref, o_ref):
    o_ref[:, 0:ROW_WORDS] = _pack_bf16_pairs(u_ref[...])
    o_ref[:, ROW_WORDS:2 * ROW_WORDS] = _pack_bf16_pairs(v_ref[...])


def _pack_tables_call(u_tab, v_tab):
    n_exp = u_tab.shape[0]
    blk = pl.BlockSpec((PACK_ROWS, D_MODEL), lambda i: (i, 0))
    return pl.pallas_call(
        _pack_tables_kernel, grid=(n_exp // PACK_ROWS,), in_specs=[blk, blk], out_specs=blk,
        out_shape=jax.ShapeDtypeStruct((n_exp, 2 * ROW_WORDS), I32),
        compiler_params=pltpu.CompilerParams(dimension_semantics=("parallel",)),
        name="pack_tables",
    )(u_tab, v_tab)


def _sc_gelu(a):
    z = GELU_C0 * (a + GELU_C1 * (a * a * a))
    tanh = 1.0 - 2.0 / (jnp.exp(2.0 * z) + 1.0)
    return 0.5 * a * (1.0 + tanh)


def _sc_split(words):
    return (plsc.bitcast(words & HI_MASK, F32), plsc.bitcast(lax.shift_left(words, jnp.int32(16)), F32))


def _sc_mul_bf16(a_words, b_words):
    return plsc.bitcast(a_words, BF16) * plsc.bitcast(b_words, BF16)


def _sc_split_sum(p, q):
    return _sc_split(plsc.bitcast(p + q, I32))


def _peer_sc_kernel(h2_hbm, eidx_hbm, gate_hbm, uv_hbm, out_hbm,
                    xbuf, ibuf, gbuf, obuf, uvbuf, mbuf, wbuf, sems):
    n = h2_hbm.shape[0]
    per_worker = n // SC_WORKERS
    blk_tokens = xbuf.shape[0]
    wid = lax.axis_index("c") * SC_SUBCORES + lax.axis_index("s")
    lane = lax.iota(I32, SC_LANES)

    def split_item(item):
        return lax.shift_right_logical(item, SC_NGROUPS.bit_length() - 1), item & (SC_NGROUPS - 1)

    def gather_copies(item, slot):
        t, g = split_item(item)
        idx = ibuf[t, pl.ds(g * SC_GROUP, SC_GROUP)]
        return (pltpu.make_async_copy(uv_hbm.at[idx], uvbuf.at[slot], sems.at[slot]),)

    def dots(t, slot):
        zero = jnp.zeros((SC_LANES,), F32)

        @pl.loop(0, SC_GROUP, step=SC_DOT_ROWS)
        def _(r0):
            accs = [[zero] * SC_DOT_PARTIALS for _ in range(SC_DOT_ROWS)]
            for k in range(0, ROW_WORDS // SC_LANES, 4):
                xs = [xbuf[t, pl.ds((k + q) * SC_LANES, SC_LANES)] for q in range(4)]
                for i in range(SC_DOT_ROWS):
                    m = [_sc_mul_bf16(xs[q], uvbuf[slot, r0 + i, pl.ds((k + q) * SC_LANES, SC_LANES)])
                         for q in range(4)]
                    hi, lo = _sc_split_sum(m[0] + m[1], m[2] + m[3])
                    p = (k // 4) % SC_DOT_PARTIALS
                    accs[i][p] = accs[i][p] + (hi + lo)
            for i in range(SC_DOT_ROWS):
                mbuf[r0 + i, :] = functools.reduce(lambda a, b: a + b, accs[i])

        tot = zero
        for c in range(SC_LANES):
            tot = tot + plsc.load_gather(mbuf, [lane, jnp.full((SC_LANES,), c, I32)])
        return tot

    def accumulate(t, slot):
        nv = SC_OWORDS // SC_LANES
        for oc in range(ROW_WORDS // SC_OWORDS):
            w0 = oc * SC_OWORDS
            accs = (tuple(obuf[t, pl.ds(w0 + j * SC_LANES, SC_LANES)] for j in range(nv))
                    + tuple(obuf[t, pl.ds(ROW_WORDS + w0 + j * SC_LANES, SC_LANES)] for j in range(nv)))

            def row_quad(rq, accs):
                r = 4 * rq
                ws = [plsc.load_gather(wbuf, [jnp.full((SC_LANES,), r + q, I32)]) for q in range(4)]
                his, los = [], []
                for j in range(nv):
                    m = [_sc_mul_bf16(ws[q], uvbuf[slot, r + q, pl.ds(ROW_WORDS + w0 + j * SC_LANES, SC_LANES)])
                         for q in range(4)]
                    hi, lo = _sc_split_sum(m[0] + m[1], m[2] + m[3])
                    his.append(accs[j] + hi)
                    los.append(accs[nv + j] + lo)
                return tuple(his) + tuple(los)

            accs = lax.fori_loop(0, SC_GROUP // 4, row_quad, accs)
            for j in range(nv):
                obuf[t, pl.ds(w0 + j * SC_LANES, SC_LANES)] = accs[j]
                obuf[t, pl.ds(ROW_WORDS + w0 + j * SC_LANES, SC_LANES)] = accs[nv + j]

    def pack_weights(w):
        bits = plsc.bitcast(w, I32)
        rounded = (bits + 0x7FFF + (lax.shift_right_logical(bits, jnp.int32(16)) & 1)) & HI_MASK
        return rounded | lax.shift_right_logical(rounded, jnp.int32(16))

    n_items = blk_tokens * SC_NGROUPS

    @pl.loop(0, per_worker // blk_tokens)
    def _(blk):
        tok0 = pl.multiple_of(wid * per_worker + blk * blk_tokens, blk_tokens)
        pltpu.sync_copy(h2_hbm.at[pl.ds(tok0, blk_tokens)], xbuf)
        pltpu.sync_copy(eidx_hbm.at[pl.ds(tok0, blk_tokens)], ibuf)
        pltpu.sync_copy(gate_hbm.at[pl.ds(tok0, blk_tokens)], gbuf)

        @pl.loop(0, blk_tokens)
        def _(t):
            zero = jnp.zeros((SC_LANES,), F32)
            for j in range(D_MODEL // SC_LANES):
                obuf[t, pl.ds(j * SC_LANES, SC_LANES)] = zero

        for ahead in range(SC_SLOTS - 1):
            for c in gather_copies(ahead, ahead):
                c.start()

        @pl.loop(0, n_items)
        def _(item):
            t, g = split_item(item)
            slot = item & (SC_SLOTS - 1)
            ahead = item + (SC_SLOTS - 1)

            @pl.when(ahead < n_items)
            def _():
                for c in gather_copies(ahead, ahead & (SC_SLOTS - 1)):
                    c.start()

            for c in gather_copies(item, slot):
                c.wait()
            a = dots(t, slot)
            wbuf[...] = pack_weights(gbuf[t, pl.ds(g * SC_GROUP, SC_GROUP)] * _sc_gelu(a))
            accumulate(t, slot)

        pltpu.sync_copy(obuf, out_hbm.at[pl.ds(tok0, blk_tokens)])


def _peer_experts(h2p, eidx, gates, uv_pack):
    n = h2p.shape[0]
    blk_tokens = math.gcd(n // SC_WORKERS, SC_TOKENS)
    mesh = plsc.VectorSubcoreMesh(core_axis_name="c", subcore_axis_name="s")
    fn = pl.kernel(
        _peer_sc_kernel,
        out_type=jax.ShapeDtypeStruct((n, D_MODEL), F32),
        mesh=mesh,
        scratch_types=[
            pltpu.VMEM((blk_tokens, ROW_WORDS), I32),
            pltpu.VMEM((blk_tokens, N_SEL), I32),
            pltpu.VMEM((blk_tokens, N_SEL), F32),
            pltpu.VMEM((blk_tokens, D_MODEL), F32),
            pltpu.VMEM((SC_SLOTS, SC_GROUP, 2 * ROW_WORDS), I32),
            pltpu.VMEM((SC_GROUP, SC_LANES), F32),
            pltpu.VMEM((SC_LANES,), I32),
            pltpu.SemaphoreType.DMA((SC_SLOTS,)),
        ],
        compiler_params=pltpu.CompilerParams(needs_layout_passes=False),
        cost_estimate=pl.CostEstimate(
            flops=4 * n * N_SEL * D_MODEL, transcendentals=n * N_SEL,
            bytes_accessed=4 * (2 * n * N_SEL * ROW_WORDS + n * ROW_WORDS + n * D_MODEL + 2 * n * N_SEL)),
        name="peer_experts",
    )
    return fn(h2p, eidx, gates, uv_pack)


def _resid_kernel(x1_ref, p_ref, mod_ref, o_ref):
    o_ref[...] = x1_ref[...] + mod_ref[0, 5:6, :] * p_ref[...]


def _resid_call(x1, peer_out, seq, mod3, mod_row0, tile0):
    n = x1.shape[0]
    tm = TOKEN_TILE
    per_seq = seq // tm
    tok = pl.BlockSpec((tm, D_MODEL), lambda i: (i, 0))
    return pl.pallas_call(
        _resid_kernel, grid=(n // tm,),
        in_specs=[tok, tok,
                  pl.BlockSpec((1, 6, D_MODEL),
                               lambda i: (mod_row0 + ((tile0 + i) // per_seq if mod_row0 else 0), 0, 0))],
        out_specs=tok, out_shape=jax.ShapeDtypeStruct((n, D_MODEL), F32),
        compiler_params=pltpu.CompilerParams(dimension_semantics=("parallel",)),
        name="resid",
    )(x1, peer_out, mod3)


def _rope_tables(seq, rotate):
    if not rotate:
        return jnp.ones((seq, KV_WIDTH), F32), jnp.zeros((seq, KV_WIDTH), F32)
    quarter = A_DH // 4
    t = jnp.arange(seq)
    row = (t // GRID_W).astype(F32)
    col = (t % GRID_W).astype(F32)
    inv = ROPE_BASE ** (-jnp.arange(quarter, dtype=F32) / quarter)
    d = jnp.arange(A_DH)
    pos = jnp.where(d[None, :] < A_DH // 2, row[:, None], col[:, None])
    ang = pos * inv[d % quarter][None, :]
    sign = jnp.where((d % (A_DH // 2)) < quarter, -1.0, 1.0).astype(F32)
    cos = jnp.cos(ang)
    sin = jnp.sin(ang) * sign[None, :]
    return jnp.tile(cos, (1, KV_WIDTH // A_DH)), jnp.tile(sin, (1, KV_WIDTH // A_DH))


def _run_chunk(x, mod3, mod_row0, prm, cache, rotate, gate_on):
    (n1, n2, w_main, w_g, w_gt, b_g, b_gt, mhg, qg_t, kg_t, bd, sink, w_m, w_a, w_q, sub_a, sub_b,
     uv_pack) = prm
    batch, seq, _ = x.shape
    n = batch * seq
    x2d = x.reshape(n, D_MODEL)
    if gate_on is not None:
        x2d, _ = lax.optimization_barrier((x2d, gate_on))
    cos, sin = _rope_tables(seq, rotate)
    mq, mk, mv, mo, gcol, grow, aq, ak, av = _inproj_call(
        x2d, seq, mod3, mod_row0, n1, w_main, w_g, w_gt, b_g, b_gt, qg_t, kg_t, bd, cos, sin)
    kc, vc, c0, m0 = cache
    h_f, h_b, c_fin, m_fin = _mlstm_call(mq, mk, mv, gcol, grow, c0, m0, batch, seq)
    if kc is None:
        a_out = _attn_ctx_call(sink, aq, ak, av, batch, seq)
    else:
        a_out = _attn_lat_call(sink, aq, ak, av, kc, vc, batch, seq)
    x1, h2p, eidx, gates = _mix_call(x2d, seq, h_f, h_b, mo, a_out, mod3, mod_row0, mhg, n2, w_m, w_a, w_q,
                                     sub_a, sub_b, 0, n // TOKEN_TILE)
    peer_out = _peer_experts(h2p, eidx, gates, uv_pack)
    y = _resid_call(x1, peer_out, seq, mod3, mod_row0, 0).reshape(batch, seq, D_MODEL)
    return y, h2p, peer_out, ak, av, c_fin, m_fin


def _pack_state(C, n_vec, m):
    b = C.shape[0]
    caug = jnp.concatenate([C, jnp.broadcast_to(n_vec[..., None], C.shape)], axis=-1)
    caug = caug.reshape(b, 2 * M_HEADS, M_DH, 2 * M_DH)
    m_rep = jnp.broadcast_to(m.reshape(b, 2 * M_HEADS, 1, 1), (b, 2 * M_HEADS, 8, M_DH))
    return caug.astype(F32), m_rep.astype(F32)


def kernel(x_prompt, x_sample, c, cache_attn_k, cache_attn_v, state_mlstm_C, state_mlstm_n, state_mlstm_m,
           c_ctx, w_ada, b_ada, norm1_g, norm2_g, w_in, b_gates, mh_norm_g, q_norm_g, k_norm_g, sink_logits,
           w_out, peer_w_q, peer_sub_a, peer_sub_b, peer_u, peer_v):
    depth = w_ada.shape[0]
    assert depth == 1
    batch, seq, _ = x_prompt.shape
    dec_batch, dec_seq, _ = x_sample.shape
    l = 0

    cond = jnp.concatenate([c_ctx[None, :], c, jnp.zeros((MOD_ROWS - 1 - dec_batch, D_MODEL), F32)], axis=0)
    mod3 = _ada_call(cond, w_ada[l], b_ada[l]).reshape(MOD_ROWS, 6, D_MODEL)

    wi = w_in[l]
    g0 = 4 * M_WIDTH
    w_main = jnp.concatenate([wi[:, :g0], wi[:, g0 + N_GATES:]], axis=1).astype(BF16)
    w_g = wi[:, g0:g0 + N_GATES]
    seg = jnp.arange(A_WIDTH) // A_DH
    bd = jnp.where(seg[:, None] == seg[None, :], 1.0 / A_DH, 0.0).astype(F32)
    prm = (norm1_g[l][None, :], norm2_g[l][None, :], w_main, w_g, w_g.T, b_gates[l][None, :], b_gates[l][:, None],
           mh_norm_g[l][None, :], jnp.tile(q_norm_g[l], A_HEADS)[None, :], jnp.tile(k_norm_g[l], A_KV)[None, :], bd,
           sink_logits[l], w_out[l][:M_WIDTH].astype(BF16), w_out[l][M_WIDTH:].astype(BF16),
           peer_w_q[l].astype(BF16), peer_sub_a[l].astype(BF16), peer_sub_b[l].astype(BF16),
           _pack_tables_call(peer_u[l], peer_v[l]))

    zeros_c = jnp.zeros((batch, 2, M_HEADS, M_DH, M_DH), F32)
    c0, m0 = _pack_state(zeros_c, zeros_c[..., 0], jnp.full((batch, 2, M_HEADS), NEG, F32))
    c0s, m0s = _pack_state(state_mlstm_C[:, l], state_mlstm_n[:, l], state_mlstm_m[:, l])
    past = cache_attn_k.shape[2]
    kc = cache_attn_k[:, l].reshape(dec_batch, past, KV_WIDTH)
    vc = cache_attn_v[:, l].reshape(dec_batch, past, KV_WIDTH)

    jobs = []
    ctx_sizes = [batch * f // sum(CTX_CHUNK_WEIGHTS) for f in CTX_CHUNK_WEIGHTS]
    assert sum(ctx_sizes) == batch
    b0 = 0
    for size in ctx_sizes:
        b1 = b0 + size
        jobs.append((x_prompt[b0:b1], 0, (None, None, c0[b0:b1], m0[b0:b1]), False))
        b0 = b1
    for b0 in range(0, dec_batch, dec_batch // LATENT_CHUNKS):
        b1 = b0 + dec_batch // LATENT_CHUNKS
        jobs.append((x_sample[b0:b1], 1 + b0, (kc[b0:b1], vc[b0:b1], c0s[b0:b1], m0s[b0:b1]), True))
    outs = []
    for i, (x_c, mod_row0, cache, rotate) in enumerate(jobs):
        gate = tuple(g for g in (outs[i - 1][1] if i >= 1 else None,
                                 outs[i - EXPERT_LAG][2] if i >= EXPERT_LAG else None) if g is not None)
        outs.append(_run_chunk(x_c, mod3, mod_row0, prm, cache, rotate, gate or None))
    ctx, lat = outs[:len(ctx_sizes)], outs[len(ctx_sizes):]
    y_p = jnp.concatenate([o[0] for o in ctx], axis=0)
    y_s = jnp.concatenate([o[0] for o in lat], axis=0)
    k_new = jnp.concatenate([o[3] for o in ctx], axis=0)
    v_new = jnp.concatenate([o[4] for o in ctx], axis=0)
    c_fin = jnp.concatenate([o[5] for o in ctx], axis=0)
    m_fin = jnp.concatenate([o[6] for o in ctx], axis=0)

    c_fin = c_fin.reshape(batch, 2, M_HEADS, M_DH, 2 * M_DH)
    new_c = c_fin[..., :M_DH][:, None]
    new_n = c_fin[..., M_DH][:, None]
    new_m = m_fin[:, :, 0, 0].reshape(batch, 2, M_HEADS)[:, None]
    new_k = k_new.reshape(batch, 1, seq, A_KV, A_DH)
    new_v = v_new.reshape(batch, 1, seq, A_KV, A_DH)
    return y_p, y_s, new_k, new_v, new_c, new_n, new_m
```

```python
import functools
import math

import jax
import jax.numpy as jnp
from jax import lax
from jax.experimental import pallas as pl
from jax.experimental.pallas import tpu as pltpu
from jax.experimental.pallas import tpu_sc as plsc

F32 = jnp.float32
BF16 = jnp.bfloat16
I32 = jnp.int32
HI = lax.Precision.HIGHEST

D_MODEL = 1024
EPS = 1e-6
NEG = -1e30
GRID_W = 64
M_HEADS = 4
M_WIDTH = 512
M_DH = 128
A_HEADS = 8
A_KV = 2
A_REP = 4
A_DH = 64
A_WIDTH = 512
KV_WIDTH = A_KV * A_DH
BLOCK = 128
ROPE_BASE = 10000.0
N_KEYS = 128
P_HEADS = 8
P_DKEY = 256
P_HALF = 128
P_TOPK = 16
N_SEL = P_HEADS * P_TOPK
N_GATES = 4 * M_HEADS
MAIN_COLS = 4 * M_WIDTH + A_WIDTH + 2 * KV_WIDTH
MOD_ROWS = 16

TOKEN_TILE = 256
MLSTM_CHUNK = 128
ADA_COL_TILE = 768
CTX_CHUNK_WEIGHTS = (1, 2, 3, 4, 6)
EXPERT_LAG = 3
LATENT_CHUNKS = 8


def _sigmoid(x):
    return 1.0 / (1.0 + jnp.exp(-x))


def _log_sigmoid(x):
    return jnp.minimum(x, 0.0) - jnp.log1p(jnp.exp(-jnp.abs(x)))


def _dot_t(a, b, precision=None):
    return lax.dot_general(a, b, (((1,), (1,)), ((), ())), precision=precision,
                           preferred_element_type=F32)


def _ada_kernel(c_ref, w_ref, b_ref, o_ref):
    c = c_ref[...]
    s = c * _sigmoid(c)
    o_ref[...] = jnp.dot(s, w_ref[...], precision=HI, preferred_element_type=F32) + b_ref[...]


def _ada_call(cond, w_ada, b_ada):
    n_out = w_ada.shape[1]
    return pl.pallas_call(
        _ada_kernel,
        grid=(n_out // ADA_COL_TILE,),
        in_specs=[pl.BlockSpec((MOD_ROWS, D_MODEL), lambda j: (0, 0)),
                  pl.BlockSpec((D_MODEL, ADA_COL_TILE), lambda j: (0, j)),
                  pl.BlockSpec((1, ADA_COL_TILE), lambda j: (0, j))],
        out_specs=pl.BlockSpec((MOD_ROWS, ADA_COL_TILE), lambda j: (0, j)),
        out_shape=jax.ShapeDtypeStruct((MOD_ROWS, n_out), F32),
        name="ada",
    )(cond, w_ada, b_ada.reshape(1, n_out))


def _swap16(x):
    n = x.shape[-1]
    lane = lax.broadcasted_iota(I32, x.shape, x.ndim - 1)
    return jnp.where((lane & 16) == 0, pltpu.roll(x, n - 16, x.ndim - 1), pltpu.roll(x, 16, x.ndim - 1))


def _inproj_kernel(x_ref, mod_ref, n1_ref, w_ref, wg_ref, wgt_ref, bg_ref, bgt_ref, qg_ref, kg_ref,
                   bd_ref, cos_ref, sin_ref,
                   mq_ref, mk_ref, mv_ref, mo_ref, gc_ref, gr_ref, aq_ref, ak_ref, av_ref):
    x = x_ref[...]
    h = x * lax.rsqrt(jnp.mean(x * x, axis=-1, keepdims=True) + EPS) * n1_ref[...]
    h = h * (1.0 + mod_ref[0, 1:2, :]) + mod_ref[0, 0:1, :]
    z = jnp.dot(h.astype(BF16), w_ref[...], preferred_element_type=F32)

    mq_ref[...] = (z[:, 0:M_WIDTH] * (M_DH ** -0.5)).astype(BF16)
    mk_ref[...] = z[:, M_WIDTH:2 * M_WIDTH].astype(BF16)
    mv_ref[...] = z[:, 2 * M_WIDTH:3 * M_WIDTH].astype(BF16)
    mo_ref[...] = z[:, 3 * M_WIDTH:4 * M_WIDTH]

    g = jnp.dot(h, wg_ref[...], precision=HI, preferred_element_type=F32) + bg_ref[...]
    kind = lax.broadcasted_iota(I32, g.shape, 1) // M_HEADS
    gc_ref[...] = jnp.where((kind & 1) == 1, _log_sigmoid(g), g)
    gt = _dot_t(wgt_ref[...], h, precision=HI) + bgt_ref[...]
    kind_t = lax.broadcasted_iota(I32, gt.shape, 0) // M_HEADS
    gr_ref[...] = jnp.where((kind_t & 1) == 1, _log_sigmoid(gt), gt)

    o = 4 * M_WIDTH
    aq = z[:, o:o + A_WIDTH]
    ak = z[:, o + A_WIDTH:o + A_WIDTH + KV_WIDTH]
    av_ref[...] = z[:, o + A_WIDTH + KV_WIDTH:o + A_WIDTH + 2 * KV_WIDTH]
    bd = bd_ref[...]
    cos = cos_ref[...]
    sin = sin_ref[...]
    aq = aq * lax.rsqrt(jnp.dot(aq * aq, bd, precision=HI, preferred_element_type=F32) + EPS) * qg_ref[...]
    cos4 = jnp.concatenate([cos] * (A_WIDTH // KV_WIDTH), axis=1)
    sin4 = jnp.concatenate([sin] * (A_WIDTH // KV_WIDTH), axis=1)
    aq = (aq * cos4 + _swap16(aq) * sin4) * (A_DH ** -0.5)
    ak = ak * lax.rsqrt(jnp.dot(ak * ak, bd[0:KV_WIDTH, 0:KV_WIDTH], precision=HI,
                                preferred_element_type=F32) + EPS) * kg_ref[...]
    ak_ref[...] = ak * cos + _swap16(ak) * sin

    lane = lax.broadcasted_iota(I32, (aq.shape[0], KV_WIDTH), 1)
    for hd in range(A_HEADS):
        grp = hd // A_REP
        blk = aq[:, (hd // 2) * KV_WIDTH:(hd // 2 + 1) * KV_WIDTH]
        if hd % 2 != grp:
            blk = pltpu.roll(blk, A_DH, 1)
        keep = (lane >= grp * A_DH) & (lane < (grp + 1) * A_DH)
        aq_ref[hd] = jnp.where(keep, blk, 0.0).astype(BF16)


def _inproj_call(x2d, seq, mod3, mod_row0, n1, w_main, w_g, w_gt, b_g, b_gt, qg_t, kg_t, bd, cos, sin):
    n = x2d.shape[0]
    tm = TOKEN_TILE
    per_seq = seq // tm

    def tok(i):
        return (i, 0)

    def const2(i):
        return (0, 0)

    in_specs = [
        pl.BlockSpec((tm, D_MODEL), tok),
        pl.BlockSpec((1, 6, D_MODEL), lambda i: (mod_row0 + (i // per_seq if mod_row0 else 0), 0, 0)),
        pl.BlockSpec((1, D_MODEL), const2),
        pl.BlockSpec((D_MODEL, MAIN_COLS), const2),
        pl.BlockSpec((D_MODEL, N_GATES), const2),
        pl.BlockSpec((N_GATES, D_MODEL), const2),
        pl.BlockSpec((1, N_GATES), const2),
        pl.BlockSpec((N_GATES, 1), const2),
        pl.BlockSpec((1, A_WIDTH), const2),
        pl.BlockSpec((1, KV_WIDTH), const2),
        pl.BlockSpec((A_WIDTH, A_WIDTH), const2),
        pl.BlockSpec((tm, KV_WIDTH), lambda i: (i % per_seq, 0)),
        pl.BlockSpec((tm, KV_WIDTH), lambda i: (i % per_seq, 0)),
    ]
    out_specs = [
        pl.BlockSpec((tm, M_WIDTH), tok),
        pl.BlockSpec((tm, M_WIDTH), tok),
        pl.BlockSpec((tm, M_WIDTH), tok),
        pl.BlockSpec((tm, M_WIDTH), tok),
        pl.BlockSpec((tm, N_GATES), tok),
        pl.BlockSpec((N_GATES, tm), lambda i: (0, i)),
        pl.BlockSpec((A_HEADS, tm, KV_WIDTH), lambda i: (0, i, 0)),
        pl.BlockSpec((tm, KV_WIDTH), tok),
        pl.BlockSpec((tm, KV_WIDTH), tok),
    ]
    out_shape = [
        jax.ShapeDtypeStruct((n, M_WIDTH), BF16),
        jax.ShapeDtypeStruct((n, M_WIDTH), BF16),
        jax.ShapeDtypeStruct((n, M_WIDTH), BF16),
        jax.ShapeDtypeStruct((n, M_WIDTH), F32),
        jax.ShapeDtypeStruct((n, N_GATES), F32),
        jax.ShapeDtypeStruct((N_GATES, n), F32),
        jax.ShapeDtypeStruct((A_HEADS, n, KV_WIDTH), BF16),
        jax.ShapeDtypeStruct((n, KV_WIDTH), F32),
        jax.ShapeDtypeStruct((n, KV_WIDTH), F32),
    ]
    return pl.pallas_call(
        _inproj_kernel, grid=(n // tm,), in_specs=in_specs, out_specs=out_specs, out_shape=out_shape,
        compiler_params=pltpu.CompilerParams(dimension_semantics=("parallel",)),
        name="inproj",
    )(x2d, mod3, n1, w_main, w_g, w_gt, b_g, b_gt, qg_t, kg_t, bd, cos, sin)


def _mlstm_chain(q, k, v, li_c, lf_c, li_r, lf_r, caug, m, tri, tri_t, mask, reverse):
    L = q.shape[0]
    last = 0 if reverse else L - 1
    b_c = jnp.dot(tri, jnp.broadcast_to(lf_c, (L, L)), precision=HI, preferred_element_type=F32)
    b_r = jnp.dot(jnp.broadcast_to(lf_r, (8, L)), tri_t, precision=HI, preferred_element_type=F32)[0:1, :]
    a_inter = b_c[:, 0:1] + m
    d = jnp.where(mask, b_c - b_r + li_r, -jnp.inf)
    m_t = jnp.maximum(a_inter, jnp.max(d, axis=1, keepdims=True))
    w_inter = jnp.exp(a_inter - m_t)
    s = _dot_t(q, k) * jnp.exp(d - m_t)
    qc = jnp.dot(q, caug.astype(BF16), preferred_element_type=F32)
    num = jnp.dot(s.astype(BF16), v, preferred_element_type=F32) + w_inter * qc[:, 0:M_DH]
    den = jnp.sum(s, axis=1, keepdims=True) + w_inter * qc[:, M_DH:M_DH + 1]
    den = jnp.maximum(jnp.abs(den), jnp.exp(-m_t))
    h = num / den
    m_new = m_t[last:last + 1, :]
    b_last = b_c[last:last + 1, 0:1]
    g_c = jnp.exp(b_last - b_c[:, 0:1] + li_c - m_new)
    decay = jnp.exp(b_last + m - m_new)
    kw = (k.astype(F32) * g_c).astype(BF16)
    vaug = jnp.concatenate([v, jnp.ones_like(v)], axis=1)
    upd = lax.dot_general(kw, vaug, (((0,), (0,)), ((), ())), preferred_element_type=F32)
    return h, decay * caug + upd, m_new


def _mlstm_kernel(qf_ref, kf_ref, vf_ref, gcf_ref, grf_ref, qb_ref, kb_ref, vb_ref, gcb_ref, grb_ref,
                  c0_ref, m0_ref, hf_ref, hb_ref, cfin_ref, mfin_ref, c_scr, m_scr):
    c = pl.program_id(1)
    nc = pl.num_programs(1)
    L = qf_ref.shape[0]

    @pl.when(c == 0)
    def _():
        c_scr[...] = c0_ref[0]
        m_scr[...] = m0_ref[0]

    row = lax.broadcasted_iota(I32, (L, L), 0)
    col = lax.broadcasted_iota(I32, (L, L), 1)
    lower = row >= col
    upper = row <= col
    lower_f = lower.astype(F32)
    upper_f = upper.astype(F32)

    for direction in range(2):
        reverse = direction == 1
        q_ref, k_ref, v_ref, gc_ref, gr_ref, h_ref = (
            (qb_ref, kb_ref, vb_ref, gcb_ref, grb_ref, hb_ref) if reverse
            else (qf_ref, kf_ref, vf_ref, gcf_ref, grf_ref, hf_ref))
        tri, tri_t, mask = (upper_f, lower_f, upper) if reverse else (lower_f, upper_f, lower)
        gc = gc_ref[...]
        gr = gr_ref[...]
        for hd in range(M_HEADS):
            ch = direction * M_HEADS + hd
            sl = slice(hd * M_DH, (hd + 1) * M_DH)
            ci = 2 * direction * M_HEADS + hd
            cf = ci + M_HEADS
            h, caug, m_new = _mlstm_chain(
                q_ref[:, sl], k_ref[:, sl], v_ref[:, sl],
                gc[:, ci:ci + 1], gc[:, cf:cf + 1], gr[ci:ci + 1, :], gr[cf:cf + 1, :],
                c_scr[ch], m_scr[ch][0:1, 0:1], tri, tri_t, mask, reverse)
            h_ref[:, sl] = h
            c_scr[ch] = caug
            m_scr[ch] = jnp.broadcast_to(m_new, m_scr.shape[1:])

    @pl.when(c == nc - 1)
    def _():
        cfin_ref[0] = c_scr[...]
        mfin_ref[0] = m_scr[...]


def _mlstm_call(mq, mk, mv, gcol, grow, c0, m0, batch, seq):
    n = mq.shape[0]
    L = MLSTM_CHUNK
    nc = seq // L
    n_ch = 2 * M_HEADS

    def fwd(b, c):
        return (b * nc + c, 0)

    def bwd(b, c):
        return (b * nc + nc - 1 - c, 0)

    def fwd_t(b, c):
        return (0, b * nc + c)

    def bwd_t(b, c):
        return (0, b * nc + nc - 1 - c)

    tok = pl.BlockSpec((L, M_WIDTH), fwd)
    tok_b = pl.BlockSpec((L, M_WIDTH), bwd)
    in_specs = [tok, tok, tok, pl.BlockSpec((L, N_GATES), fwd), pl.BlockSpec((N_GATES, L), fwd_t),
                tok_b, tok_b, tok_b, pl.BlockSpec((L, N_GATES), bwd), pl.BlockSpec((N_GATES, L), bwd_t),
                pl.BlockSpec((1, n_ch, M_DH, 2 * M_DH), lambda b, c: (b, 0, 0, 0)),
                pl.BlockSpec((1, n_ch, 8, M_DH), lambda b, c: (b, 0, 0, 0))]
    out_specs = [tok, tok_b,
                 pl.BlockSpec((1, n_ch, M_DH, 2 * M_DH), lambda b, c: (b, 0, 0, 0)),
                 pl.BlockSpec((1, n_ch, 8, M_DH), lambda b, c: (b, 0, 0, 0))]
    out_shape = [jax.ShapeDtypeStruct((n, M_WIDTH), F32), jax.ShapeDtypeStruct((n, M_WIDTH), F32),
                 jax.ShapeDtypeStruct((batch, n_ch, M_DH, 2 * M_DH), F32),
                 jax.ShapeDtypeStruct((batch, n_ch, 8, M_DH), F32)]
    return pl.pallas_call(
        _mlstm_kernel, grid=(batch, nc), in_specs=in_specs, out_specs=out_specs, out_shape=out_shape,
        scratch_shapes=[pltpu.VMEM((n_ch, M_DH, 2 * M_DH), F32), pltpu.VMEM((n_ch, 8, M_DH), F32)],
        compiler_params=pltpu.CompilerParams(dimension_semantics=("parallel", "arbitrary")),
        name="mlstm",
    )(mq, mk, mv, gcol, grow, mq, mk, mv, gcol, grow, c0, m0)


def _sink_column(sink_ref, grp, rows_per_head):
    return jnp.concatenate(
        [jnp.full((rows_per_head, 1), sink_ref[grp * A_REP + r], F32) for r in range(A_REP)], axis=0)


def _store_heads(out_ref, o, grp, rows_per_head):
    for r in range(A_REP):
        hd = grp * A_REP + r
        out_ref[:, hd * A_DH:(hd + 1) * A_DH] = o[r * rows_per_head:(r + 1) * rows_per_head,
                                                  grp * A_DH:(grp + 1) * A_DH].astype(out_ref.dtype)


def _attn_ctx_kernel(sink_ref, q_ref, k_ref, v_ref, out_ref):
    s_len = k_ref.shape[0]
    k = k_ref[...].astype(BF16)
    v = v_ref[...].astype(BF16)
    for grp in range(A_KV):
        q = q_ref[grp * A_REP:(grp + 1) * A_REP].reshape(A_REP * s_len, KV_WIDTH)
        s = _dot_t(q, k)
        sk = _sink_column(sink_ref, grp, s_len)
        mx = jnp.maximum(jnp.max(s, axis=1, keepdims=True), sk)
        p = jnp.exp(s - mx)
        den = jnp.sum(p, axis=1, keepdims=True) + jnp.exp(sk - mx)
        o = jnp.dot(p.astype(BF16), v, preferred_element_type=F32) / den
        _store_heads(out_ref, o, grp, s_len)


def _attn_ctx_call(sink, aq, ak, av, batch, seq):
    n = ak.shape[0]
    return pl.pallas_call(
        _attn_ctx_kernel, grid=(batch,),
        in_specs=[pl.BlockSpec(memory_space=pltpu.SMEM),
                  pl.BlockSpec((A_HEADS, seq, KV_WIDTH), lambda b: (0, b, 0)),
                  pl.BlockSpec((seq, KV_WIDTH), lambda b: (b, 0)),
                  pl.BlockSpec((seq, KV_WIDTH), lambda b: (b, 0))],
        out_specs=pl.BlockSpec((seq, A_WIDTH), lambda b: (b, 0)),
        out_shape=jax.ShapeDtypeStruct((n, A_WIDTH), BF16),
        compiler_params=pltpu.CompilerParams(dimension_semantics=("parallel",)),
        name="attn_ctx",
    )(sink, aq, ak, av)


def _attn_lat_kernel(sink_ref, q_ref, kc_ref, vc_ref, kp_ref, kq_ref, kn_ref, vp_ref, vq_ref, vn_ref, out_ref):
    i = pl.program_id(1)
    nb = pl.num_programs(1)
    kc = kc_ref[0].astype(BF16)
    vc = vc_ref[0].astype(BF16)
    kp, kq, kn = kp_ref[...].astype(BF16), kq_ref[...].astype(BF16), kn_ref[...].astype(BF16)
    vp, vq, vn = vp_ref[...].astype(BF16), vq_ref[...].astype(BF16), vn_ref[...].astype(BF16)
    rows = A_REP * BLOCK
    qpos = lax.broadcasted_iota(I32, (rows, BLOCK), 0) % BLOCK
    kpos = lax.broadcasted_iota(I32, (rows, BLOCK), 1)
    mask_p = (kpos >= qpos) & (i > 0)
    mask_n = (kpos <= qpos) & (i < nb - 1)
    for grp in range(A_KV):
        q = q_ref[grp * A_REP:(grp + 1) * A_REP].reshape(rows, KV_WIDTH)
        s_c = _dot_t(q, kc)
        s_p = jnp.where(mask_p, _dot_t(q, kp), NEG)
        s_q = _dot_t(q, kq)
        s_n = jnp.where(mask_n, _dot_t(q, kn), NEG)
        sk = _sink_column(sink_ref, grp, BLOCK)
        mx = jnp.maximum(jnp.maximum(jnp.max(s_c, axis=1, keepdims=True), jnp.max(s_p, axis=1, keepdims=True)),
                         jnp.maximum(jnp.max(s_q, axis=1, keepdims=True), jnp.max(s_n, axis=1, keepdims=True)))
        mx = jnp.maximum(mx, sk)
        p_c, p_p, p_q, p_n = jnp.exp(s_c - mx), jnp.exp(s_p - mx), jnp.exp(s_q - mx), jnp.exp(s_n - mx)
        den = (jnp.sum(p_c, axis=1, keepdims=True) + jnp.sum(p_p, axis=1, keepdims=True)
               + jnp.sum(p_q, axis=1, keepdims=True) + jnp.sum(p_n, axis=1, keepdims=True) + jnp.exp(sk - mx))
        o = (jnp.dot(p_c.astype(BF16), vc, preferred_element_type=F32)
             + jnp.dot(p_p.astype(BF16), vp, preferred_element_type=F32)
             + jnp.dot(p_q.astype(BF16), vq, preferred_element_type=F32)
             + jnp.dot(p_n.astype(BF16), vn, preferred_element_type=F32)) / den
        _store_heads(out_ref, o, grp, BLOCK)


def _attn_lat_call(sink, aq, ak, av, kc, vc, batch, seq):
    n = ak.shape[0]
    nb = seq // BLOCK
    past = kc.shape[1]

    def cur(b, i):
        return (b * nb + i, 0)

    def prev(b, i):
        return (b * nb + jnp.maximum(i - 1, 0), 0)

    def nxt(b, i):
        return (b * nb + jnp.minimum(i + 1, nb - 1), 0)

    blk = functools.partial(pl.BlockSpec, (BLOCK, KV_WIDTH))
    cache = pl.BlockSpec((1, past, KV_WIDTH), lambda b, i: (b, 0, 0))
    return pl.pallas_call(
        _attn_lat_kernel, grid=(batch, nb),
        in_specs=[pl.BlockSpec(memory_space=pltpu.SMEM),
                  pl.BlockSpec((A_HEADS, BLOCK, KV_WIDTH), lambda b, i: (0, b * nb + i, 0)),
                  cache, cache, blk(prev), blk(cur), blk(nxt), blk(prev), blk(cur), blk(nxt)],
        out_specs=pl.BlockSpec((BLOCK, A_WIDTH), cur),
        out_shape=jax.ShapeDtypeStruct((n, A_WIDTH), BF16),
        compiler_params=pltpu.CompilerParams(dimension_semantics=("parallel", "parallel")),
        name="attn_lat",
    )(sink, aq, kc, vc, ak, ak, ak, av, av, av)


def _top16_rows(s, payload=None):
    n_rows = s.shape[0]
    rows = lax.broadcasted_iota(I32, s.shape, 0).astype(F32)
    vals, idxs, pays = [], [], []
    for _ in range(P_TOPK):
        mx = jnp.max(s, axis=0, keepdims=True)
        ix = jnp.min(jnp.where(s == mx, rows, float(n_rows)), axis=0, keepdims=True)
        hit = rows == ix
        vals.append(mx)
        idxs.append(ix)
        if payload is not None:
            pays.append(jnp.sum(jnp.where(hit, payload, 0.0), axis=0, keepdims=True))
        s = jnp.where(hit, -jnp.inf, s)
    out = (jnp.concatenate(vals, axis=0), jnp.concatenate(idxs, axis=0))
    if payload is not None:
        out += (jnp.concatenate(pays, axis=0),)
    return out


def _mix_kernel(x_ref, hf_ref, hb_ref, mo_ref, ao_ref, mod_ref, mhg_ref, n2_ref, wm_ref, wa_ref, wq_ref,
                sa_ref, sb_ref, x1_ref, h2_ref, eidx_ref, gate_ref, qp_scr, e_scr, g_scr):
    tm = x_ref.shape[0]
    hs = hf_ref[...] + hb_ref[...]
    parts = []
    for hd in range(M_HEADS):
        blk = hs[:, hd * M_DH:(hd + 1) * M_DH]
        parts.append(blk * lax.rsqrt(jnp.mean(blk * blk, axis=-1, keepdims=True) + EPS))
    m_out = _sigmoid(mo_ref[...]) * (jnp.concatenate(parts, axis=1) * mhg_ref[...])
    mix = (jnp.dot(m_out.astype(BF16), wm_ref[...], preferred_element_type=F32)
           + jnp.dot(ao_ref[...], wa_ref[...], preferred_element_type=F32))
    x1 = x_ref[...] + mod_ref[0, 2:3, :] * mix
    x1_ref[...] = x1
    h2 = x1 * lax.rsqrt(jnp.mean(x1 * x1, axis=-1, keepdims=True) + EPS) * n2_ref[...]
    h2 = h2 * (1.0 + mod_ref[0, 4:5, :]) + mod_ref[0, 3:4, :]
    h2_ref[...] = _pack_bf16_pairs(h2)
    qp = jnp.dot(h2.astype(BF16), wq_ref[...], preferred_element_type=F32)
    for p in range(P_HEADS):
        qp_scr[p] = qp[:, p * P_DKEY:(p + 1) * P_DKEY].astype(BF16)
    sub_a = sa_ref[...]
    sub_b = sb_ref[...]

    def head_body(p, carry):
        for half in range(tm // N_KEYS):
            cols = slice(half * N_KEYS, (half + 1) * N_KEYS)
            qh = qp_scr[p, pl.ds(half * N_KEYS, N_KEYS), :]
            s_a = _dot_t(sub_a, qh[:, 0:P_HALF])
            s_b = _dot_t(sub_b, qh[:, P_HALF:P_DKEY])
            va, ia = _top16_rows(s_a)
            vb, ib = _top16_rows(s_b)
            keep = [P_TOPK // (i + 1) for i in range(P_TOPK)]
            pad = -sum(keep) % 8
            cand = jnp.concatenate([va[i:i + 1, :] + vb[0:keep[i], :] for i in range(P_TOPK)]
                                   + [jnp.full((pad, N_KEYS), -jnp.inf, F32)], axis=0)
            cidx = jnp.concatenate([ia[i:i + 1, :] * float(N_KEYS) + ib[0:keep[i], :] for i in range(P_TOPK)]
                                   + [jnp.zeros((pad, N_KEYS), F32)], axis=0)
            top, _, eidx = _top16_rows(cand, cidx)
            ex = jnp.exp(top - jnp.max(top, axis=0, keepdims=True))
            gates = ex / jnp.sum(ex, axis=0, keepdims=True)
            r0 = pl.multiple_of(p * P_TOPK, P_TOPK)
            e_scr[pl.ds(r0, P_TOPK), cols] = eidx
            g_scr[pl.ds(r0, P_TOPK), cols] = gates
        return carry

    lax.fori_loop(0, P_HEADS, head_body, 0)
    for half in range(tm // N_KEYS):
        cols = slice(half * N_KEYS, (half + 1) * N_KEYS)
        eidx_ref[cols, :] = e_scr[:, cols].T.astype(I32)
        gate_ref[cols, :] = g_scr[:, cols].T


def _mix_call(x2d, seq, h_f, h_b, mo, a_out, mod3, mod_row0, mhg, n2, w_m, w_a, w_q, sub_a, sub_b, tile0, n_tiles):
    tm = TOKEN_TILE
    per_seq = seq // tm
    n = n_tiles * tm

    def tok_in(i):
        return (tile0 + i, 0)

    def tok(i):
        return (i, 0)

    def const2(i):
        return (0, 0)

    in_specs = [
        pl.BlockSpec((tm, D_MODEL), tok_in),
        pl.BlockSpec((tm, M_WIDTH), tok_in), pl.BlockSpec((tm, M_WIDTH), tok_in),
        pl.BlockSpec((tm, M_WIDTH), tok_in),
        pl.BlockSpec((tm, A_WIDTH), tok_in),
        pl.BlockSpec((1, 6, D_MODEL), lambda i: (mod_row0 + ((tile0 + i) // per_seq if mod_row0 else 0), 0, 0)),
        pl.BlockSpec((1, M_WIDTH), const2),
        pl.BlockSpec((1, D_MODEL), const2),
        pl.BlockSpec((M_WIDTH, D_MODEL), const2),
        pl.BlockSpec((A_WIDTH, D_MODEL), const2),
        pl.BlockSpec((D_MODEL, P_HEADS * P_DKEY), const2),
        pl.BlockSpec((N_KEYS, P_HALF), const2),
        pl.BlockSpec((N_KEYS, P_HALF), const2),
    ]
    out_specs = [pl.BlockSpec((tm, D_MODEL), tok), pl.BlockSpec((tm, D_MODEL // 2), tok),
                 pl.BlockSpec((tm, N_SEL), tok), pl.BlockSpec((tm, N_SEL), tok)]
    out_shape = [jax.ShapeDtypeStruct((n, D_MODEL), F32), jax.ShapeDtypeStruct((n, D_MODEL // 2), I32),
                 jax.ShapeDtypeStruct((n, N_SEL), I32), jax.ShapeDtypeStruct((n, N_SEL), F32)]
    return pl.pallas_call(
        _mix_kernel, grid=(n_tiles,), in_specs=in_specs, out_specs=out_specs, out_shape=out_shape,
        scratch_shapes=[pltpu.VMEM((P_HEADS, tm, P_DKEY), BF16), pltpu.VMEM((N_SEL, tm), F32),
                        pltpu.VMEM((N_SEL, tm), F32)],
        compiler_params=pltpu.CompilerParams(dimension_semantics=("parallel",)),
        name="mix",
    )(x2d, h_f, h_b, mo, a_out, mod3, mhg, n2, w_m, w_a, w_q, sub_a, sub_b)


SC_LANES = 16
SC_CORES = 2
SC_SUBCORES = 16
SC_WORKERS = SC_CORES * SC_SUBCORES
SC_TOKENS = 32
SC_GROUP = SC_LANES
SC_NGROUPS = N_SEL // SC_GROUP
SC_SLOTS = 4
ROW_WORDS = D_MODEL // 2
SC_DOT_ROWS = 16
SC_DOT_PARTIALS = 1
SC_OWORDS = 16 * SC_LANES
HI_MASK = -65536
PACK_ROWS = 512
GELU_C0 = 0.7978845608028654
GELU_C1 = 0.044715


def _pack_bf16_pairs(x):
    half = x.shape[1] // 2
    bits = lax.bitcast_convert_type(x.astype(BF16).astype(F32), I32)
    return (bits[:, :half] & HI_MASK) | lax.shift_right_logical(bits[:, half:], jnp.int32(16))


def _pack_tables_kernel(u_ref, v_ref, o_ref):
    o_ref[:, 0:ROW_WORDS] = _pack_bf16_pairs(u_ref[...])
    o_ref[:, ROW_WORDS:2 * ROW_WORDS] = _pack_bf16_pairs(v_ref[...])


def _pack_tables_call(u_tab, v_tab):
    n_exp = u_tab.shape[0]
    blk = pl.BlockSpec((PACK_ROWS, D_MODEL), lambda i: (i, 0))
    return pl.pallas_call(
        _pack_tables_kernel, grid=(n_exp // PACK_ROWS,), in_specs=[blk, blk], out_specs=blk,
        out_shape=jax.ShapeDtypeStruct((n_exp, 2 * ROW_WORDS), I32),
        compiler_params=pltpu.CompilerParams(dimension_semantics=("parallel",)),
        name="pack_tables",
    )(u_tab, v_tab)


def _sc_gelu(a):
    z = GELU_C0 * (a + GELU_C1 * (a * a * a))
    tanh = 1.0 - 2.0 / (jnp.exp(2.0 * z) + 1.0)
    return 0.5 * a * (1.0 + tanh)


def _sc_split(words):
    return (plsc.bitcast(words & HI_MASK, F32), plsc.bitcast(lax.shift_left(words, jnp.int32(16)), F32))


def _sc_mul_bf16(a_words, b_words):
    return plsc.bitcast(a_words, BF16) * plsc.bitcast(b_words, BF16)


def _sc_split_sum(p, q):
    return _sc_split(plsc.bitcast(p + q, I32))


def _peer_sc_kernel(h2_hbm, eidx_hbm, gate_hbm, uv_hbm, out_hbm,
                    xbuf, ibuf, gbuf, obuf, uvbuf, mbuf, wbuf, sems):
    n = h2_hbm.shape[0]
    per_worker = n // SC_WORKERS
    blk_tokens = xbuf.shape[0]
    wid = lax.axis_index("c") * SC_SUBCORES + lax.axis_index("s")
    lane = lax.iota(I32, SC_LANES)

    def split_item(item):
        return lax.shift_right_logical(item, SC_NGROUPS.bit_length() - 1), item & (SC_NGROUPS - 1)

    def gather_copies(item, slot):
        t, g = split_item(item)
        idx = ibuf[t, pl.ds(g * SC_GROUP, SC_GROUP)]
        return (pltpu.make_async_copy(uv_hbm.at[idx], uvbuf.at[slot], sems.at[slot]),)

    def dots(t, slot):
        zero = jnp.zeros((SC_LANES,), F32)

        @pl.loop(0, SC_GROUP, step=SC_DOT_ROWS)
        def _(r0):
            accs = [[zero] * SC_DOT_PARTIALS for _ in range(SC_DOT_ROWS)]
            for k in range(0, ROW_WORDS // SC_LANES, 4):
                xs = [xbuf[t, pl.ds((k + q) * SC_LANES, SC_LANES)] for q in range(4)]
                for i in range(SC_DOT_ROWS):
                    m = [_sc_mul_bf16(xs[q], uvbuf[slot, r0 + i, pl.ds((k + q) * SC_LANES, SC_LANES)])
                         for q in range(4)]
                    hi, lo = _sc_split_sum(m[0] + m[1], m[2] + m[3])
                    p = (k // 4) % SC_DOT_PARTIALS
                    accs[i][p] = accs[i][p] + (hi + lo)
            for i in range(SC_DOT_ROWS):
                mbuf[r0 + i, :] = functools.reduce(lambda a, b: a + b, accs[i])

        tot = zero
        for c in range(SC_LANES):
            tot = tot + plsc.load_gather(mbuf, [lane, jnp.full((SC_LANES,), c, I32)])
        return tot

    def accumulate(t, slot):
        nv = SC_OWORDS // SC_LANES
        for oc in range(ROW_WORDS // SC_OWORDS):
            w0 = oc * SC_OWORDS
            accs = (tuple(obuf[t, pl.ds(w0 + j * SC_LANES, SC_LANES)] for j in range(nv))
                    + tuple(obuf[t, pl.ds(ROW_WORDS + w0 + j * SC_LANES, SC_LANES)] for j in range(nv)))

            def row_quad(rq, accs):
                r = 4 * rq
                ws = [plsc.load_gather(wbuf, [jnp.full((SC_LANES,), r + q, I32)]) for q in range(4)]
                his, los = [], []
                for j in range(nv):
                    m = [_sc_mul_bf16(ws[q], uvbuf[slot, r + q, pl.ds(ROW_WORDS + w0 + j * SC_LANES, SC_LANES)])
                         for q in range(4)]
                    hi, lo = _sc_split_sum(m[0] + m[1], m[2] + m[3])
                    his.append(accs[j] + hi)
                    los.append(accs[nv + j] + lo)
                return tuple(his) + tuple(los)

            accs = lax.fori_loop(0, SC_GROUP // 4, row_quad, accs)
            for j in range(nv):
                obuf[t, pl.ds(w0 + j * SC_LANES, SC_LANES)] = accs[j]
                obuf[t, pl.ds(ROW_WORDS + w0 + j * SC_LANES, SC_LANES)] = accs[nv + j]

    def pack_weights(w):
        bits = plsc.bitcast(w, I32)
        rounded = (bits + 0x7FFF + (lax.shift_right_logical(bits, jnp.int32(16)) & 1)) & HI_MASK
        return rounded | lax.shift_right_logical(rounded, jnp.int32(16))

    n_items = blk_tokens * SC_NGROUPS

    @pl.loop(0, per_worker // blk_tokens)
    def _(blk):
        tok0 = pl.multiple_of(wid * per_worker + blk * blk_tokens, blk_tokens)
        pltpu.sync_copy(h2_hbm.at[pl.ds(tok0, blk_tokens)], xbuf)
        pltpu.sync_copy(eidx_hbm.at[pl.ds(tok0, blk_tokens)], ibuf)
        pltpu.sync_copy(gate_hbm.at[pl.ds(tok0, blk_tokens)], gbuf)

        @pl.loop(0, blk_tokens)
        def _(t):
            zero = jnp.zeros((SC_LANES,), F32)
            for j in range(D_MODEL // SC_LANES):
                obuf[t, pl.ds(j * SC_LANES, SC_LANES)] = zero

        for ahead in range(SC_SLOTS - 1):
            for c in gather_copies(ahead, ahead):
                c.start()

        def wait_and_dot(item):
            for c in gather_copies(item, item & (SC_SLOTS - 1)):
                c.wait()
            return dots(split_item(item)[0], item & (SC_SLOTS - 1))

        def store_weights(item, a):
            t, g = split_item(item)
            wbuf[...] = pack_weights(gbuf[t, pl.ds(g * SC_GROUP, SC_GROUP)] * _sc_gelu(a))
            return t, item & (SC_SLOTS - 1)

        def step(item, a):
            ahead = item + (SC_SLOTS - 1)

            @pl.when(ahead < n_items)
            def _():
                for c in gather_copies(ahead, ahead & (SC_SLOTS - 1)):
                    c.start()

            t, slot = store_weights(item, a)
            a_next = wait_and_dot(item + 1)
            accumulate(t, slot)
            return a_next

        a_last = lax.fori_loop(0, n_items - 1, step, wait_and_dot(0))
        accumulate(*store_weights(n_items - 1, a_last))

        pltpu.sync_copy(obuf, out_hbm.at[pl.ds(tok0, blk_tokens)])


def _peer_experts(h2p, eidx, gates, uv_pack):
    n = h2p.shape[0]
    blk_tokens = math.gcd(n // SC_WORKERS, SC_TOKENS)
    mesh = plsc.VectorSubcoreMesh(core_axis_name="c", subcore_axis_name="s")
    fn = pl.kernel(
        _peer_sc_kernel,
        out_type=jax.ShapeDtypeStruct((n, D_MODEL), F32),
        mesh=mesh,
        scratch_types=[
            pltpu.VMEM((blk_tokens, ROW_WORDS), I32),
            pltpu.VMEM((blk_tokens, N_SEL), I32),
            pltpu.VMEM((blk_tokens, N_SEL), F32),
            pltpu.VMEM((blk_tokens, D_MODEL), F32),
            pltpu.VMEM((SC_SLOTS, SC_GROUP, 2 * ROW_WORDS), I32),
            pltpu.VMEM((SC_GROUP, SC_LANES), F32),
            pltpu.VMEM((SC_LANES,), I32),
            pltpu.SemaphoreType.DMA((SC_SLOTS,)),
        ],
        compiler_params=pltpu.CompilerParams(needs_layout_passes=False),
        cost_estimate=pl.CostEstimate(
            flops=4 * n * N_SEL * D_MODEL, transcendentals=n * N_SEL,
            bytes_accessed=4 * (2 * n * N_SEL * ROW_WORDS + n * ROW_WORDS + n * D_MODEL + 2 * n * N_SEL)),
        name="peer_experts",
    )
    return fn(h2p, eidx, gates, uv_pack)


def _resid_kernel(x1_ref, p_ref, mod_ref, o_ref):
    o_ref[...] = x1_ref[...] + mod_ref[0, 5:6, :] * p_ref[...]


def _resid_call(x1, peer_out, seq, mod3, mod_row0, tile0):
    n = x1.shape[0]
    tm = TOKEN_TILE
    per_seq = seq // tm
    tok = pl.BlockSpec((tm, D_MODEL), lambda i: (i, 0))
    return pl.pallas_call(
        _resid_kernel, grid=(n // tm,),
        in_specs=[tok, tok,
                  pl.BlockSpec((1, 6, D_MODEL),
                               lambda i: (mod_row0 + ((tile0 + i) // per_seq if mod_row0 else 0), 0, 0))],
        out_specs=tok, out_shape=jax.ShapeDtypeStruct((n, D_MODEL), F32),
        compiler_params=pltpu.CompilerParams(dimension_semantics=("parallel",)),
        name="resid",
    )(x1, peer_out, mod3)


def _rope_tables(seq, rotate):
    if not rotate:
        return jnp.ones((seq, KV_WIDTH), F32), jnp.zeros((seq, KV_WIDTH), F32)
    quarter = A_DH // 4
    t = jnp.arange(seq)
    row = (t // GRID_W).astype(F32)
    col = (t % GRID_W).astype(F32)
    inv = ROPE_BASE ** (-jnp.arange(quarter, dtype=F32) / quarter)
    d = jnp.arange(A_DH)
    pos = jnp.where(d[None, :] < A_DH // 2, row[:, None], col[:, None])
    ang = pos * inv[d % quarter][None, :]
    sign = jnp.where((d % (A_DH // 2)) < quarter, -1.0, 1.0).astype(F32)
    cos = jnp.cos(ang)
    sin = jnp.sin(ang) * sign[None, :]
    return jnp.tile(cos, (1, KV_WIDTH // A_DH)), jnp.tile(sin, (1, KV_WIDTH // A_DH))


def _run_chunk(x, mod3, mod_row0, prm, cache, rotate, gate_on):
    (n1, n2, w_main, w_g, w_gt, b_g, b_gt, mhg, qg_t, kg_t, bd, sink, w_m, w_a, w_q, sub_a, sub_b,
     uv_pack) = prm
    batch, seq, _ = x.shape
    n = batch * seq
    x2d = x.reshape(n, D_MODEL)
    if gate_on is not None:
        x2d, _ = lax.optimization_barrier((x2d, gate_on))
    cos, sin = _rope_tables(seq, rotate)
    mq, mk, mv, mo, gcol, grow, aq, ak, av = _inproj_call(
        x2d, seq, mod3, mod_row0, n1, w_main, w_g, w_gt, b_g, b_gt, qg_t, kg_t, bd, cos, sin)
    kc, vc, c0, m0 = cache
    h_f, h_b, c_fin, m_fin = _mlstm_call(mq, mk, mv, gcol, grow, c0, m0, batch, seq)
    if kc is None:
        a_out = _attn_ctx_call(sink, aq, ak, av, batch, seq)
    else:
        a_out = _attn_lat_call(sink, aq, ak, av, kc, vc, batch, seq)
    x1, h2p, eidx, gates = _mix_call(x2d, seq, h_f, h_b, mo, a_out, mod3, mod_row0, mhg, n2, w_m, w_a, w_q,
                                     sub_a, sub_b, 0, n // TOKEN_TILE)
    peer_out = _peer_experts(h2p, eidx, gates, uv_pack)
    y = _resid_call(x1, peer_out, seq, mod3, mod_row0, 0).reshape(batch, seq, D_MODEL)
    return y, h2p, peer_out, ak, av, c_fin, m_fin


def _pack_state(C, n_vec, m):
    b = C.shape[0]
    caug = jnp.concatenate([C, jnp.broadcast_to(n_vec[..., None], C.shape)], axis=-1)
    caug = caug.reshape(b, 2 * M_HEADS, M_DH, 2 * M_DH)
    m_rep = jnp.broadcast_to(m.reshape(b, 2 * M_HEADS, 1, 1), (b, 2 * M_HEADS, 8, M_DH))
    return caug.astype(F32), m_rep.astype(F32)


def kernel(x_prompt, x_sample, c, cache_attn_k, cache_attn_v, state_mlstm_C, state_mlstm_n, state_mlstm_m,
           c_ctx, w_ada, b_ada, norm1_g, norm2_g, w_in, b_gates, mh_norm_g, q_norm_g, k_norm_g, sink_logits,
           w_out, peer_w_q, peer_sub_a, peer_sub_b, peer_u, peer_v):
    depth = w_ada.shape[0]
    assert depth == 1
    batch, seq, _ = x_prompt.shape
    dec_batch, dec_seq, _ = x_sample.shape
    l = 0

    cond = jnp.concatenate([c_ctx[None, :], c, jnp.zeros((MOD_ROWS - 1 - dec_batch, D_MODEL), F32)], axis=0)
    mod3 = _ada_call(cond, w_ada[l], b_ada[l]).reshape(MOD_ROWS, 6, D_MODEL)

    wi = w_in[l]
    g0 = 4 * M_WIDTH
    w_main = jnp.concatenate([wi[:, :g0], wi[:, g0 + N_GATES:]], axis=1).astype(BF16)
    w_g = wi[:, g0:g0 + N_GATES]
    seg = jnp.arange(A_WIDTH) // A_DH
    bd = jnp.where(seg[:, None] == seg[None, :], 1.0 / A_DH, 0.0).astype(F32)
    prm = (norm1_g[l][None, :], norm2_g[l][None, :], w_main, w_g, w_g.T, b_gates[l][None, :], b_gates[l][:, None],
           mh_norm_g[l][None, :], jnp.tile(q_norm_g[l], A_HEADS)[None, :], jnp.tile(k_norm_g[l], A_KV)[None, :], bd,
           sink_logits[l], w_out[l][:M_WIDTH].astype(BF16), w_out[l][M_WIDTH:].astype(BF16),
           peer_w_q[l].astype(BF16), peer_sub_a[l].astype(BF16), peer_sub_b[l].astype(BF16),
           _pack_tables_call(peer_u[l], peer_v[l]))

    zeros_c = jnp.zeros((batch, 2, M_HEADS, M_DH, M_DH), F32)
    c0, m0 = _pack_state(zeros_c, zeros_c[..., 0], jnp.full((batch, 2, M_HEADS), NEG, F32))
    c0s, m0s = _pack_state(state_mlstm_C[:, l], state_mlstm_n[:, l], state_mlstm_m[:, l])
    past = cache_attn_k.shape[2]
    kc = cache_attn_k[:, l].reshape(dec_batch, past, KV_WIDTH)
    vc = cache_attn_v[:, l].reshape(dec_batch, past, KV_WIDTH)

    jobs = []
    ctx_sizes = [batch * f // sum(CTX_CHUNK_WEIGHTS) for f in CTX_CHUNK_WEIGHTS]
    assert sum(ctx_sizes) == batch
    b0 = 0
    for size in ctx_sizes:
        b1 = b0 + size
        jobs.append((x_prompt[b0:b1], 0, (None, None, c0[b0:b1], m0[b0:b1]), False))
        b0 = b1
    for b0 in range(0, dec_batch, dec_batch // LATENT_CHUNKS):
        b1 = b0 + dec_batch // LATENT_CHUNKS
        jobs.append((x_sample[b0:b1], 1 + b0, (kc[b0:b1], vc[b0:b1], c0s[b0:b1], m0s[b0:b1]), True))
    outs = []
    for i, (x_c, mod_row0, cache, rotate) in enumerate(jobs):
        gate = tuple(g for g in (outs[i - 1][1] if i >= 1 else None,
                                 outs[i - EXPERT_LAG][2] if i >= EXPERT_LAG else None) if g is not None)
        outs.append(_run_chunk(x_c, mod3, mod_row0, prm, cache, rotate, gate or None))
    ctx, lat = outs[:len(ctx_sizes)], outs[len(ctx_sizes):]
    y_p = jnp.concatenate([o[0] for o in ctx], axis=0)
    y_s = jnp.concatenate([o[0] for o in lat], axis=0)
    k_new = jnp.concatenate([o[3] for o in ctx], axis=0)
    v_new = jnp.concatenate([o[4] for o in ctx], axis=0)
    c_fin = jnp.concatenate([o[5] for o in ctx], axis=0)
    m_fin = jnp.concatenate([o[6] for o in ctx], axis=0)

    c_fin = c_fin.reshape(batch, 2, M_HEADS, M_DH, 2 * M_DH)
    new_c = c_fin[..., :M_DH][:, None]
    new_n = c_fin[..., M_DH][:, None]
    new_m = m_fin[:, :, 0, 0].reshape(batch, 2, M_HEADS)[:, None]
    new_k = k_new.reshape(batch, 1, seq, A_KV, A_DH)
    new_v = v_new.reshape(batch, 1, seq, A_KV, A_DH)
    return y_p, y_s, new_k, new_v, new_c, new_n, new_m
```

```python
import functools
import math

import jax
import jax.numpy as jnp
from jax import lax
from jax.experimental import pallas as pl
from jax.experimental.pallas import tpu as pltpu
from jax.experimental.pallas import tpu_sc as plsc

F32 = jnp.float32
BF16 = jnp.bfloat16
I32 = jnp.int32
HI = lax.Precision.HIGHEST

D_MODEL = 1024
EPS = 1e-6
NEG = -1e30
GRID_W = 64
M_HEADS = 4
M_WIDTH = 512
M_DH = 128
A_HEADS = 8
A_KV = 2
A_REP = 4
A_DH = 64
A_WIDTH = 512
KV_WIDTH = A_KV * A_DH
BLOCK = 128
ROPE_BASE = 10000.0
N_KEYS = 128
P_HEADS = 8
P_DKEY = 256
P_HALF = 128
P_TOPK = 16
N_SEL = P_HEADS * P_TOPK
N_GATES = 4 * M_HEADS
MAIN_COLS = 4 * M_WIDTH + A_WIDTH + 2 * KV_WIDTH
MOD_ROWS = 16

TOKEN_TILE = 256
MLSTM_CHUNK = 128
ADA_COL_TILE = 768
CTX_CHUNK_WEIGHTS = (1, 2, 3, 4, 6)
EXPERT_LAG = 3
LATENT_CHUNKS = 8


def _sigmoid(x):
    return 1.0 / (1.0 + jnp.exp(-x))


def _log_sigmoid(x):
    return jnp.minimum(x, 0.0) - jnp.log1p(jnp.exp(-jnp.abs(x)))


def _dot_t(a, b, precision=None):
    return lax.dot_general(a, b, (((1,), (1,)), ((), ())), precision=precision,
                           preferred_element_type=F32)


def _ada_kernel(c_ref, w_ref, b_ref, o_ref):
    c = c_ref[...]
    s = c * _sigmoid(c)
    o_ref[...] = jnp.dot(s, w_ref[...], precision=HI, preferred_element_type=F32) + b_ref[...]


def _ada_call(cond, w_ada, b_ada):
    n_out = w_ada.shape[1]
    return pl.pallas_call(
        _ada_kernel,
        grid=(n_out // ADA_COL_TILE,),
        in_specs=[pl.BlockSpec((MOD_ROWS, D_MODEL), lambda j: (0, 0)),
                  pl.BlockSpec((D_MODEL, ADA_COL_TILE), lambda j: (0, j)),
                  pl.BlockSpec((1, ADA_COL_TILE), lambda j: (0, j))],
        out_specs=pl.BlockSpec((MOD_ROWS, ADA_COL_TILE), lambda j: (0, j)),
        out_shape=jax.ShapeDtypeStruct((MOD_ROWS, n_out), F32),
        name="ada",
    )(cond, w_ada, b_ada.reshape(1, n_out))


def _swap16(x):
    n = x.shape[-1]
    lane = lax.broadcasted_iota(I32, x.shape, x.ndim - 1)
    return jnp.where((lane & 16) == 0, pltpu.roll(x, n - 16, x.ndim - 1), pltpu.roll(x, 16, x.ndim - 1))


def _inproj_kernel(x_ref, mod_ref, n1_ref, w_ref, wg_ref, wgt_ref, bg_ref, bgt_ref, qg_ref, kg_ref,
                   bd_ref, cos_ref, sin_ref,
                   mq_ref, mk_ref, mv_ref, mo_ref, gc_ref, gr_ref, aq_ref, ak_ref, av_ref):
    x = x_ref[...]
    h = x * lax.rsqrt(jnp.mean(x * x, axis=-1, keepdims=True) + EPS) * n1_ref[...]
    h = h * (1.0 + mod_ref[0, 1:2, :]) + mod_ref[0, 0:1, :]
    z = jnp.dot(h.astype(BF16), w_ref[...], preferred_element_type=F32)

    mq_ref[...] = (z[:, 0:M_WIDTH] * (M_DH ** -0.5)).astype(BF16)
    mk_ref[...] = z[:, M_WIDTH:2 * M_WIDTH].astype(BF16)
    mv_ref[...] = z[:, 2 * M_WIDTH:3 * M_WIDTH].astype(BF16)
    mo_ref[...] = z[:, 3 * M_WIDTH:4 * M_WIDTH]

    g = jnp.dot(h, wg_ref[...], precision=HI, preferred_element_type=F32) + bg_ref[...]
    kind = lax.broadcasted_iota(I32, g.shape, 1) // M_HEADS
    gc_ref[...] = jnp.where((kind & 1) == 1, _log_sigmoid(g), g)
    gt = _dot_t(wgt_ref[...], h, precision=HI) + bgt_ref[...]
    kind_t = lax.broadcasted_iota(I32, gt.shape, 0) // M_HEADS
    gr_ref[...] = jnp.where((kind_t & 1) == 1, _log_sigmoid(gt), gt)

    o = 4 * M_WIDTH
    aq = z[:, o:o + A_WIDTH]
    ak = z[:, o + A_WIDTH:o + A_WIDTH + KV_WIDTH]
    av_ref[...] = z[:, o + A_WIDTH + KV_WIDTH:o + A_WIDTH + 2 * KV_WIDTH]
    bd = bd_ref[...]
    cos = cos_ref[...]
    sin = sin_ref[...]
    aq = aq * lax.rsqrt(jnp.dot(aq * aq, bd, precision=HI, preferred_element_type=F32) + EPS) * qg_ref[...]
    cos4 = jnp.concatenate([cos] * (A_WIDTH // KV_WIDTH), axis=1)
    sin4 = jnp.concatenate([sin] * (A_WIDTH // KV_WIDTH), axis=1)
    aq = (aq * cos4 + _swap16(aq) * sin4) * (A_DH ** -0.5)
    ak = ak * lax.rsqrt(jnp.dot(ak * ak, bd[0:KV_WIDTH, 0:KV_WIDTH], precision=HI,
                                preferred_element_type=F32) + EPS) * kg_ref[...]
    ak_ref[...] = ak * cos + _swap16(ak) * sin

    lane = lax.broadcasted_iota(I32, (aq.shape[0], KV_WIDTH), 1)
    for hd in range(A_HEADS):
        grp = hd // A_REP
        blk = aq[:, (hd // 2) * KV_WIDTH:(hd // 2 + 1) * KV_WIDTH]
        if hd % 2 != grp:
            blk = pltpu.roll(blk, A_DH, 1)
        keep = (lane >= grp * A_DH) & (lane < (grp + 1) * A_DH)
        aq_ref[hd] = jnp.where(keep, blk, 0.0).astype(BF16)


def _inproj_call(x2d, seq, mod3, mod_row0, n1, w_main, w_g, w_gt, b_g, b_gt, qg_t, kg_t, bd, cos, sin):
    n = x2d.shape[0]
    tm = TOKEN_TILE
    per_seq = seq // tm

    def tok(i):
        return (i, 0)

    def const2(i):
        return (0, 0)

    in_specs = [
        pl.BlockSpec((tm, D_MODEL), tok),
        pl.BlockSpec((1, 6, D_MODEL), lambda i: (mod_row0 + (i // per_seq if mod_row0 else 0), 0, 0)),
        pl.BlockSpec((1, D_MODEL), const2),
        pl.BlockSpec((D_MODEL, MAIN_COLS), const2),
        pl.BlockSpec((D_MODEL, N_GATES), const2),
        pl.BlockSpec((N_GATES, D_MODEL), const2),
        pl.BlockSpec((1, N_GATES), const2),
        pl.BlockSpec((N_GATES, 1), const2),
        pl.BlockSpec((1, A_WIDTH), const2),
        pl.BlockSpec((1, KV_WIDTH), const2),
        pl.BlockSpec((A_WIDTH, A_WIDTH), const2),
        pl.BlockSpec((tm, KV_WIDTH), lambda i: (i % per_seq, 0)),
        pl.BlockSpec((tm, KV_WIDTH), lambda i: (i % per_seq, 0)),
    ]
    out_specs = [
        pl.BlockSpec((tm, M_WIDTH), tok),
        pl.BlockSpec((tm, M_WIDTH), tok),
        pl.BlockSpec((tm, M_WIDTH), tok),
        pl.BlockSpec((tm, M_WIDTH), tok),
        pl.BlockSpec((tm, N_GATES), tok),
        pl.BlockSpec((N_GATES, tm), lambda i: (0, i)),
        pl.BlockSpec((A_HEADS, tm, KV_WIDTH), lambda i: (0, i, 0)),
        pl.BlockSpec((tm, KV_WIDTH), tok),
        pl.BlockSpec((tm, KV_WIDTH), tok),
    ]
    out_shape = [
        jax.ShapeDtypeStruct((n, M_WIDTH), BF16),
        jax.ShapeDtypeStruct((n, M_WIDTH), BF16),
        jax.ShapeDtypeStruct((n, M_WIDTH), BF16),
        jax.ShapeDtypeStruct((n, M_WIDTH), F32),
        jax.ShapeDtypeStruct((n, N_GATES), F32),
        jax.ShapeDtypeStruct((N_GATES, n), F32),
        jax.ShapeDtypeStruct((A_HEADS, n, KV_WIDTH), BF16),
        jax.ShapeDtypeStruct((n, KV_WIDTH), F32),
        jax.ShapeDtypeStruct((n, KV_WIDTH), F32),
    ]
    return pl.pallas_call(
        _inproj_kernel, grid=(n // tm,), in_specs=in_specs, out_specs=out_specs, out_shape=out_shape,
        compiler_params=pltpu.CompilerParams(dimension_semantics=("parallel",)),
        name="inproj",
    )(x2d, mod3, n1, w_main, w_g, w_gt, b_g, b_gt, qg_t, kg_t, bd, cos, sin)


def _mlstm_chain(q, k, v, li_c, lf_c, li_r, lf_r, caug, m, tri, tri_t, mask, reverse):
    L = q.shape[0]
    last = 0 if reverse else L - 1
    b_c = jnp.dot(tri, jnp.broadcast_to(lf_c, (L, L)), precision=HI, preferred_element_type=F32)
    b_r = jnp.dot(jnp.broadcast_to(lf_r, (8, L)), tri_t, precision=HI, preferred_element_type=F32)[0:1, :]
    a_inter = b_c[:, 0:1] + m
    d = jnp.where(mask, b_c - b_r + li_r, -jnp.inf)
    m_t = jnp.maximum(a_inter, jnp.max(d, axis=1, keepdims=True))
    w_inter = jnp.exp(a_inter - m_t)
    s = _dot_t(q, k) * jnp.exp(d - m_t)
    qc = jnp.dot(q, caug.astype(BF16), preferred_element_type=F32)
    num = jnp.dot(s.astype(BF16), v, preferred_element_type=F32) + w_inter * qc[:, 0:M_DH]
    den = jnp.sum(s, axis=1, keepdims=True) + w_inter * qc[:, M_DH:M_DH + 1]
    den = jnp.maximum(jnp.abs(den), jnp.exp(-m_t))
    h = num / den
    m_new = m_t[last:last + 1, :]
    b_last = b_c[last:last + 1, 0:1]
    g_c = jnp.exp(b_last - b_c[:, 0:1] + li_c - m_new)
    decay = jnp.exp(b_last + m - m_new)
    kw = (k.astype(F32) * g_c).astype(BF16)
    vaug = jnp.concatenate([v, jnp.ones_like(v)], axis=1)
    upd = lax.dot_general(kw, vaug, (((0,), (0,)), ((), ())), preferred_element_type=F32)
    return h, decay * caug + upd, m_new


def _mlstm_kernel(qf_ref, kf_ref, vf_ref, gcf_ref, grf_ref, qb_ref, kb_ref, vb_ref, gcb_ref, grb_ref,
                  c0_ref, m0_ref, hf_ref, hb_ref, cfin_ref, mfin_ref, c_scr, m_scr):
    c = pl.program_id(1)
    nc = pl.num_programs(1)
    L = qf_ref.shape[0]

    @pl.when(c == 0)
    def _():
        c_scr[...] = c0_ref[0]
        m_scr[...] = m0_ref[0]

    row = lax.broadcasted_iota(I32, (L, L), 0)
    col = lax.broadcasted_iota(I32, (L, L), 1)
    lower = row >= col
    upper = row <= col
    lower_f = lower.astype(F32)
    upper_f = upper.astype(F32)

    for direction in range(2):
        reverse = direction == 1
        q_ref, k_ref, v_ref, gc_ref, gr_ref, h_ref = (
            (qb_ref, kb_ref, vb_ref, gcb_ref, grb_ref, hb_ref) if reverse
            else (qf_ref, kf_ref, vf_ref, gcf_ref, grf_ref, hf_ref))
        tri, tri_t, mask = (upper_f, lower_f, upper) if reverse else (lower_f, upper_f, lower)
        gc = gc_ref[...]
        gr = gr_ref[...]
        for hd in range(M_HEADS):
            ch = direction * M_HEADS + hd
            sl = slice(hd * M_DH, (hd + 1) * M_DH)
            ci = 2 * direction * M_HEADS + hd
            cf = ci + M_HEADS
            h, caug, m_new = _mlstm_chain(
                q_ref[:, sl], k_ref[:, sl], v_ref[:, sl],
                gc[:, ci:ci + 1], gc[:, cf:cf + 1], gr[ci:ci + 1, :], gr[cf:cf + 1, :],
                c_scr[ch], m_scr[ch][0:1, 0:1], tri, tri_t, mask, reverse)
            h_ref[:, sl] = h
            c_scr[ch] = caug
            m_scr[ch] = jnp.broadcast_to(m_new, m_scr.shape[1:])

    @pl.when(c == nc - 1)
    def _():
        cfin_ref[0] = c_scr[...]
        mfin_ref[0] = m_scr[...]


def _mlstm_call(mq, mk, mv, gcol, grow, c0, m0, batch, seq):
    n = mq.shape[0]
    L = MLSTM_CHUNK
    nc = seq // L
    n_ch = 2 * M_HEADS

    def fwd(b, c):
        return (b * nc + c, 0)

    def bwd(b, c):
        return (b * nc + nc - 1 - c, 0)

    def fwd_t(b, c):
        return (0, b * nc + c)

    def bwd_t(b, c):
        return (0, b * nc + nc - 1 - c)

    tok = pl.BlockSpec((L, M_WIDTH), fwd)
    tok_b = pl.BlockSpec((L, M_WIDTH), bwd)
    in_specs = [tok, tok, tok, pl.BlockSpec((L, N_GATES), fwd), pl.BlockSpec((N_GATES, L), fwd_t),
                tok_b, tok_b, tok_b, pl.BlockSpec((L, N_GATES), bwd), pl.BlockSpec((N_GATES, L), bwd_t),
                pl.BlockSpec((1, n_ch, M_DH, 2 * M_DH), lambda b, c: (b, 0, 0, 0)),
                pl.BlockSpec((1, n_ch, 8, M_DH), lambda b, c: (b, 0, 0, 0))]
    out_specs = [tok, tok_b,
                 pl.BlockSpec((1, n_ch, M_DH, 2 * M_DH), lambda b, c: (b, 0, 0, 0)),
                 pl.BlockSpec((1, n_ch, 8, M_DH), lambda b, c: (b, 0, 0, 0))]
    out_shape = [jax.ShapeDtypeStruct((n, M_WIDTH), F32), jax.ShapeDtypeStruct((n, M_WIDTH), F32),
                 jax.ShapeDtypeStruct((batch, n_ch, M_DH, 2 * M_DH), F32),
                 jax.ShapeDtypeStruct((batch, n_ch, 8, M_DH), F32)]
    return pl.pallas_call(
        _mlstm_kernel, grid=(batch, nc), in_specs=in_specs, out_specs=out_specs, out_shape=out_shape,
        scratch_shapes=[pltpu.VMEM((n_ch, M_DH, 2 * M_DH), F32), pltpu.VMEM((n_ch, 8, M_DH), F32)],
        compiler_params=pltpu.CompilerParams(dimension_semantics=("parallel", "arbitrary")),
        name="mlstm",
    )(mq, mk, mv, gcol, grow, mq, mk, mv, gcol, grow, c0, m0)


def _sink_column(sink_ref, grp, rows_per_head):
    return jnp.concatenate(
        [jnp.full((rows_per_head, 1), sink_ref[grp * A_REP + r], F32) for r in range(A_REP)], axis=0)


def _store_heads(out_ref, o, grp, rows_per_head):
    for r in range(A_REP):
        hd = grp * A_REP + r
        out_ref[:, hd * A_DH:(hd + 1) * A_DH] = o[r * rows_per_head:(r + 1) * rows_per_head,
                                                  grp * A_DH:(grp + 1) * A_DH].astype(out_ref.dtype)


def _attn_ctx_kernel(sink_ref, q_ref, k_ref, v_ref, out_ref):
    s_len = k_ref.shape[0]
    k = k_ref[...].astype(BF16)
    v = v_ref[...].astype(BF16)
    for grp in range(A_KV):
        q = q_ref[grp * A_REP:(grp + 1) * A_REP].reshape(A_REP * s_len, KV_WIDTH)
        s = _dot_t(q, k)
        sk = _sink_column(sink_ref, grp, s_len)
        mx = jnp.maximum(jnp.max(s, axis=1, keepdims=True), sk)
        p = jnp.exp(s - mx)
        den = jnp.sum(p, axis=1, keepdims=True) + jnp.exp(sk - mx)
        o = jnp.dot(p.astype(BF16), v, preferred_element_type=F32) / den
        _store_heads(out_ref, o, grp, s_len)


def _attn_ctx_call(sink, aq, ak, av, batch, seq):
    n = ak.shape[0]
    return pl.pallas_call(
        _attn_ctx_kernel, grid=(batch,),
        in_specs=[pl.BlockSpec(memory_space=pltpu.SMEM),
                  pl.BlockSpec((A_HEADS, seq, KV_WIDTH), lambda b: (0, b, 0)),
                  pl.BlockSpec((seq, KV_WIDTH), lambda b: (b, 0)),
                  pl.BlockSpec((seq, KV_WIDTH), lambda b: (b, 0))],
        out_specs=pl.BlockSpec((seq, A_WIDTH), lambda b: (b, 0)),
        out_shape=jax.ShapeDtypeStruct((n, A_WIDTH), BF16),
        compiler_params=pltpu.CompilerParams(dimension_semantics=("parallel",)),
        name="attn_ctx",
    )(sink, aq, ak, av)


def _attn_lat_kernel(sink_ref, q_ref, kc_ref, vc_ref, kp_ref, kq_ref, kn_ref, vp_ref, vq_ref, vn_ref, out_ref):
    i = pl.program_id(1)
    nb = pl.num_programs(1)
    kc = kc_ref[0].astype(BF16)
    vc = vc_ref[0].astype(BF16)
    kp, kq, kn = kp_ref[...].astype(BF16), kq_ref[...].astype(BF16), kn_ref[...].astype(BF16)
    vp, vq, vn = vp_ref[...].astype(BF16), vq_ref[...].astype(BF16), vn_ref[...].astype(BF16)
    rows = A_REP * BLOCK
    qpos = lax.broadcasted_iota(I32, (rows, BLOCK), 0) % BLOCK
    kpos = lax.broadcasted_iota(I32, (rows, BLOCK), 1)
    mask_p = (kpos >= qpos) & (i > 0)
    mask_n = (kpos <= qpos) & (i < nb - 1)
    for grp in range(A_KV):
        q = q_ref[grp * A_REP:(grp + 1) * A_REP].reshape(rows, KV_WIDTH)
        s_c = _dot_t(q, kc)
        s_p = jnp.where(mask_p, _dot_t(q, kp), NEG)
        s_q = _dot_t(q, kq)
        s_n = jnp.where(mask_n, _dot_t(q, kn), NEG)
        sk = _sink_column(sink_ref, grp, BLOCK)
        mx = jnp.maximum(jnp.maximum(jnp.max(s_c, axis=1, keepdims=True), jnp.max(s_p, axis=1, keepdims=True)),
                         jnp.maximum(jnp.max(s_q, axis=1, keepdims=True), jnp.max(s_n, axis=1, keepdims=True)))
        mx = jnp.maximum(mx, sk)
        p_c, p_p, p_q, p_n = jnp.exp(s_c - mx), jnp.exp(s_p - mx), jnp.exp(s_q - mx), jnp.exp(s_n - mx)
        den = (jnp.sum(p_c, axis=1, keepdims=True) + jnp.sum(p_p, axis=1, keepdims=True)
               + jnp.sum(p_q, axis=1, keepdims=True) + jnp.sum(p_n, axis=1, keepdims=True) + jnp.exp(sk - mx))
        o = (jnp.dot(p_c.astype(BF16), vc, preferred_element_type=F32)
             + jnp.dot(p_p.astype(BF16), vp, preferred_element_type=F32)
             + jnp.dot(p_q.astype(BF16), vq, preferred_element_type=F32)
             + jnp.dot(p_n.astype(BF16), vn, preferred_element_type=F32)) / den
        _store_heads(out_ref, o, grp, BLOCK)


def _attn_lat_call(sink, aq, ak, av, kc, vc, batch, seq):
    n = ak.shape[0]
    nb = seq // BLOCK
    past = kc.shape[1]

    def cur(b, i):
        return (b * nb + i, 0)

    def prev(b, i):
        return (b * nb + jnp.maximum(i - 1, 0), 0)

    def nxt(b, i):
        return (b * nb + jnp.minimum(i + 1, nb - 1), 0)

    blk = functools.partial(pl.BlockSpec, (BLOCK, KV_WIDTH))
    cache = pl.BlockSpec((1, past, KV_WIDTH), lambda b, i: (b, 0, 0))
    return pl.pallas_call(
        _attn_lat_kernel, grid=(batch, nb),
        in_specs=[pl.BlockSpec(memory_space=pltpu.SMEM),
                  pl.BlockSpec((A_HEADS, BLOCK, KV_WIDTH), lambda b, i: (0, b * nb + i, 0)),
                  cache, cache, blk(prev), blk(cur), blk(nxt), blk(prev), blk(cur), blk(nxt)],
        out_specs=pl.BlockSpec((BLOCK, A_WIDTH), cur),
        out_shape=jax.ShapeDtypeStruct((n, A_WIDTH), BF16),
        compiler_params=pltpu.CompilerParams(dimension_semantics=("parallel", "parallel")),
        name="attn_lat",
    )(sink, aq, kc, vc, ak, ak, ak, av, av, av)


def _top16_rows(s, payload=None):
    n_rows = s.shape[0]
    rows = lax.broadcasted_iota(I32, s.shape, 0).astype(F32)
    vals, idxs, pays = [], [], []
    for _ in range(P_TOPK):
        mx = jnp.max(s, axis=0, keepdims=True)
        ix = jnp.min(jnp.where(s == mx, rows, float(n_rows)), axis=0, keepdims=True)
        hit = rows == ix
        vals.append(mx)
        idxs.append(ix)
        if payload is not None:
            pays.append(jnp.sum(jnp.where(hit, payload, 0.0), axis=0, keepdims=True))
        s = jnp.where(hit, -jnp.inf, s)
    out = (jnp.concatenate(vals, axis=0), jnp.concatenate(idxs, axis=0))
    if payload is not None:
        out += (jnp.concatenate(pays, axis=0),)
    return out


def _mix_kernel(x_ref, hf_ref, hb_ref, mo_ref, ao_ref, mod_ref, mhg_ref, n2_ref, wm_ref, wa_ref, wq_ref,
                sa_ref, sb_ref, x1_ref, h2_ref, eidx_ref, gate_ref, qp_scr, e_scr, g_scr):
    tm = x_ref.shape[0]
    hs = hf_ref[...] + hb_ref[...]
    parts = []
    for hd in range(M_HEADS):
        blk = hs[:, hd * M_DH:(hd + 1) * M_DH]
        parts.append(blk * lax.rsqrt(jnp.mean(blk * blk, axis=-1, keepdims=True) + EPS))
    m_out = _sigmoid(mo_ref[...]) * (jnp.concatenate(parts, axis=1) * mhg_ref[...])
    mix = (jnp.dot(m_out.astype(BF16), wm_ref[...], preferred_element_type=F32)
           + jnp.dot(ao_ref[...], wa_ref[...], preferred_element_type=F32))
    x1 = x_ref[...] + mod_ref[0, 2:3, :] * mix
    x1_ref[...] = x1
    h2 = x1 * lax.rsqrt(jnp.mean(x1 * x1, axis=-1, keepdims=True) + EPS) * n2_ref[...]
    h2 = h2 * (1.0 + mod_ref[0, 4:5, :]) + mod_ref[0, 3:4, :]
    h2_ref[...] = _pack_bf16_pairs(h2)
    qp = jnp.dot(h2.astype(BF16), wq_ref[...], preferred_element_type=F32)
    for p in range(P_HEADS):
        qp_scr[p] = qp[:, p * P_DKEY:(p + 1) * P_DKEY].astype(BF16)
    sub_a = sa_ref[...]
    sub_b = sb_ref[...]

    def head_body(p, carry):
        for half in range(tm // N_KEYS):
            cols = slice(half * N_KEYS, (half + 1) * N_KEYS)
            qh = qp_scr[p, pl.ds(half * N_KEYS, N_KEYS), :]
            s_a = _dot_t(sub_a, qh[:, 0:P_HALF])
            s_b = _dot_t(sub_b, qh[:, P_HALF:P_DKEY])
            va, ia = _top16_rows(s_a)
            vb, ib = _top16_rows(s_b)
            keep = [P_TOPK // (i + 1) for i in range(P_TOPK)]
            pad = -sum(keep) % 8
            cand = jnp.concatenate([va[i:i + 1, :] + vb[0:keep[i], :] for i in range(P_TOPK)]
                                   + [jnp.full((pad, N_KEYS), -jnp.inf, F32)], axis=0)
            cidx = jnp.concatenate([ia[i:i + 1, :] * float(N_KEYS) + ib[0:keep[i], :] for i in range(P_TOPK)]
                                   + [jnp.zeros((pad, N_KEYS), F32)], axis=0)
            top, _, eidx = _top16_rows(cand, cidx)
            ex = jnp.exp(top - jnp.max(top, axis=0, keepdims=True))
            gates = ex / jnp.sum(ex, axis=0, keepdims=True)
            r0 = pl.multiple_of(p * P_TOPK, P_TOPK)
            e_scr[pl.ds(r0, P_TOPK), cols] = eidx
            g_scr[pl.ds(r0, P_TOPK), cols] = gates
        return carry

    lax.fori_loop(0, P_HEADS, head_body, 0)
    for half in range(tm // N_KEYS):
        cols = slice(half * N_KEYS, (half + 1) * N_KEYS)
        eidx_ref[cols, :] = e_scr[:, cols].T.astype(I32)
        gate_ref[cols, :] = g_scr[:, cols].T


def _mix_call(x2d, seq, h_f, h_b, mo, a_out, mod3, mod_row0, mhg, n2, w_m, w_a, w_q, sub_a, sub_b):
    n = x2d.shape[0]
    tm = TOKEN_TILE
    per_seq = seq // tm

    def tok(i):
        return (i, 0)

    def const2(i):
        return (0, 0)

    in_specs = [
        pl.BlockSpec((tm, D_MODEL), tok),
        pl.BlockSpec((tm, M_WIDTH), tok), pl.BlockSpec((tm, M_WIDTH), tok), pl.BlockSpec((tm, M_WIDTH), tok),
        pl.BlockSpec((tm, A_WIDTH), tok),
        pl.BlockSpec((1, 6, D_MODEL), lambda i: (mod_row0 + (i // per_seq if mod_row0 else 0), 0, 0)),
        pl.BlockSpec((1, M_WIDTH), const2),
        pl.BlockSpec((1, D_MODEL), const2),
        pl.BlockSpec((M_WIDTH, D_MODEL), const2),
        pl.BlockSpec((A_WIDTH, D_MODEL), const2),
        pl.BlockSpec((D_MODEL, P_HEADS * P_DKEY), const2),
        pl.BlockSpec((N_KEYS, P_HALF), const2),
        pl.BlockSpec((N_KEYS, P_HALF), const2),
    ]
    out_specs = [pl.BlockSpec((tm, D_MODEL), tok), pl.BlockSpec((tm, D_MODEL // 2), tok),
                 pl.BlockSpec((tm, N_SEL), tok), pl.BlockSpec((tm, N_SEL), tok)]
    out_shape = [jax.ShapeDtypeStruct((n, D_MODEL), F32), jax.ShapeDtypeStruct((n, D_MODEL // 2), I32),
                 jax.ShapeDtypeStruct((n, N_SEL), I32), jax.ShapeDtypeStruct((n, N_SEL), F32)]
    return pl.pallas_call(
        _mix_kernel, grid=(n // tm,), in_specs=in_specs, out_specs=out_specs, out_shape=out_shape,
        scratch_shapes=[pltpu.VMEM((P_HEADS, tm, P_DKEY), BF16), pltpu.VMEM((N_SEL, tm), F32),
                        pltpu.VMEM((N_SEL, tm), F32)],
        compiler_params=pltpu.CompilerParams(dimension_semantics=("parallel",)),
        name="mix",
    )(x2d, h_f, h_b, mo, a_out, mod3, mhg, n2, w_m, w_a, w_q, sub_a, sub_b)


SC_LANES = 16
SC_CORES = 2
SC_SUBCORES = 16
SC_WORKERS = SC_CORES * SC_SUBCORES
SC_TOKENS = 32
SC_GROUP = SC_LANES
SC_NGROUPS = N_SEL // SC_GROUP
SC_SLOTS = 4
ROW_WORDS = D_MODEL // 2
SC_DOT_ROWS = 16
SC_DOT_PARTIALS = 1
SC_OWORDS = 16 * SC_LANES
HI_MASK = -65536
PACK_ROWS = 512
GELU_C0 = 0.7978845608028654
GELU_C1 = 0.044715


def _pack_bf16_pairs(x):
    half = x.shape[1] // 2
    bits = lax.bitcast_convert_type(x.astype(BF16).astype(F32), I32)
    return (bits[:, :half] & HI_MASK) | lax.shift_right_logical(bits[:, half:], jnp.int32(16))


def _pack_tables_kernel(u_ref, v_ref, o_ref):
    o_ref[:, 0:ROW_WORDS] = _pack_bf16_pairs(u_ref[...])
    o_ref[:, ROW_WORDS:2 * ROW_WORDS] = _pack_bf16_pairs(v_ref[...])


def _pack_tables_call(u_tab, v_tab):
    n_exp = u_tab.shape[0]
    blk = pl.BlockSpec((PACK_ROWS, D_MODEL), lambda i: (i, 0))
    return pl.pallas_call(
        _pack_tables_kernel, grid=(n_exp // PACK_ROWS,), in_specs=[blk, blk], out_specs=blk,
        out_shape=jax.ShapeDtypeStruct((n_exp, 2 * ROW_WORDS), I32),
        compiler_params=pltpu.CompilerParams(dimension_semantics=("parallel",)),
        name="pack_tables",
    )(u_tab, v_tab)


def _sc_gelu(a):
    z = GELU_C0 * (a + GELU_C1 * (a * a * a))
    tanh = 1.0 - 2.0 / (jnp.exp(2.0 * z) + 1.0)
    return 0.5 * a * (1.0 + tanh)


def _sc_split(words):
    return (plsc.bitcast(words & HI_MASK, F32), plsc.bitcast(lax.shift_left(words, jnp.int32(16)), F32))


def _sc_mul_bf16(a_words, b_words):
    return plsc.bitcast(a_words, BF16) * plsc.bitcast(b_words, BF16)


def _sc_split_sum(p, q):
    return _sc_split(plsc.bitcast(p + q, I32))


def _peer_sc_kernel(h2_hbm, eidx_hbm, gate_hbm, uv_hbm, out_hbm,
                    xbuf, ibuf, gbuf, obuf, uvbuf, mbuf, wbuf, sems):
    n = h2_hbm.shape[0]
    per_worker = n // SC_WORKERS
    blk_tokens = xbuf.shape[0]
    wid = lax.axis_index("c") * SC_SUBCORES + lax.axis_index("s")
    lane = lax.iota(I32, SC_LANES)

    def split_item(item):
        return lax.shift_right_logical(item, SC_NGROUPS.bit_length() - 1), item & (SC_NGROUPS - 1)

    def gather_copies(item, slot):
        t, g = split_item(item)
        idx = ibuf[t, pl.ds(g * SC_GROUP, SC_GROUP)]
        return (pltpu.make_async_copy(uv_hbm.at[idx], uvbuf.at[slot], sems.at[slot]),)

    def dots(t, slot):
        zero = jnp.zeros((SC_LANES,), F32)

        @pl.loop(0, SC_GROUP, step=SC_DOT_ROWS)
        def _(r0):
            accs = [[zero] * SC_DOT_PARTIALS for _ in range(SC_DOT_ROWS)]
            for k in range(0, ROW_WORDS // SC_LANES, 4):
                xs = [xbuf[t, pl.ds((k + q) * SC_LANES, SC_LANES)] for q in range(4)]
                for i in range(SC_DOT_ROWS):
                    m = [_sc_mul_bf16(xs[q], uvbuf[slot, r0 + i, pl.ds((k + q) * SC_LANES, SC_LANES)])
                         for q in range(4)]
                    hi, lo = _sc_split_sum(m[0] + m[1], m[2] + m[3])
                    p = (k // 4) % SC_DOT_PARTIALS
                    accs[i][p] = accs[i][p] + (hi + lo)
            for i in range(SC_DOT_ROWS):
                mbuf[r0 + i, :] = functools.reduce(lambda a, b: a + b, accs[i])

        tot = zero
        for c in range(SC_LANES):
            tot = tot + plsc.load_gather(mbuf, [lane, jnp.full((SC_LANES,), c, I32)])
        return tot

    def accumulate(t, slot):
        nv = SC_OWORDS // SC_LANES
        for oc in range(ROW_WORDS // SC_OWORDS):
            w0 = oc * SC_OWORDS
            accs = (tuple(obuf[t, pl.ds(w0 + j * SC_LANES, SC_LANES)] for j in range(nv))
                    + tuple(obuf[t, pl.ds(ROW_WORDS + w0 + j * SC_LANES, SC_LANES)] for j in range(nv)))

            def row_quad(rq, accs):
                r = 4 * rq
                ws = [plsc.load_gather(wbuf, [jnp.full((SC_LANES,), r + q, I32)]) for q in range(4)]
                his, los = [], []
                for j in range(nv):
                    m = [_sc_mul_bf16(ws[q], uvbuf[slot, r + q, pl.ds(ROW_WORDS + w0 + j * SC_LANES, SC_LANES)])
                         for q in range(4)]
                    hi, lo = _sc_split_sum(m[0] + m[1], m[2] + m[3])
                    his.append(accs[j] + hi)
                    los.append(accs[nv + j] + lo)
                return tuple(his) + tuple(los)

            accs = lax.fori_loop(0, SC_GROUP // 4, row_quad, accs)
            for j in range(nv):
                obuf[t, pl.ds(w0 + j * SC_LANES, SC_LANES)] = accs[j]
                obuf[t, pl.ds(ROW_WORDS + w0 + j * SC_LANES, SC_LANES)] = accs[nv + j]

    def pack_weights(w):
        bits = plsc.bitcast(w, I32)
        rounded = (bits + 0x7FFF + (lax.shift_right_logical(bits, jnp.int32(16)) & 1)) & HI_MASK
        return rounded | lax.shift_right_logical(rounded, jnp.int32(16))

    n_items = blk_tokens * SC_NGROUPS

    @pl.loop(0, per_worker // blk_tokens)
    def _(blk):
        tok0 = pl.multiple_of(wid * per_worker + blk * blk_tokens, blk_tokens)
        pltpu.sync_copy(h2_hbm.at[pl.ds(tok0, blk_tokens)], xbuf)
        pltpu.sync_copy(eidx_hbm.at[pl.ds(tok0, blk_tokens)], ibuf)
        pltpu.sync_copy(gate_hbm.at[pl.ds(tok0, blk_tokens)], gbuf)

        @pl.loop(0, blk_tokens)
        def _(t):
            zero = jnp.zeros((SC_LANES,), F32)
            for j in range(D_MODEL // SC_LANES):
                obuf[t, pl.ds(j * SC_LANES, SC_LANES)] = zero

        for ahead in range(SC_SLOTS - 1):
            for c in gather_copies(ahead, ahead):
                c.start()

        @pl.loop(0, n_items)
        def _(item):
            t, g = split_item(item)
            slot = item & (SC_SLOTS - 1)
            ahead = item + (SC_SLOTS - 1)

            @pl.when(ahead < n_items)
            def _():
                for c in gather_copies(ahead, ahead & (SC_SLOTS - 1)):
                    c.start()

            for c in gather_copies(item, slot):
                c.wait()
            a = dots(t, slot)
            wbuf[...] = pack_weights(gbuf[t, pl.ds(g * SC_GROUP, SC_GROUP)] * _sc_gelu(a))
            accumulate(t, slot)

        pltpu.sync_copy(obuf, out_hbm.at[pl.ds(tok0, blk_tokens)])


def _peer_experts(h2p, eidx, gates, uv_pack):
    n = h2p.shape[0]
    assert n % (8 * SC_WORKERS) == 0
    blk_tokens = math.gcd(n // SC_WORKERS, SC_TOKENS)
    mesh = plsc.VectorSubcoreMesh(core_axis_name="c", subcore_axis_name="s")
    fn = pl.kernel(
        _peer_sc_kernel,
        out_type=jax.ShapeDtypeStruct((n, D_MODEL), F32),
        mesh=mesh,
        scratch_types=[
            pltpu.VMEM((blk_tokens, ROW_WORDS), I32),
            pltpu.VMEM((blk_tokens, N_SEL), I32),
            pltpu.VMEM((blk_tokens, N_SEL), F32),
            pltpu.VMEM((blk_tokens, D_MODEL), F32),
            pltpu.VMEM((SC_SLOTS, SC_GROUP, 2 * ROW_WORDS), I32),
            pltpu.VMEM((SC_GROUP, SC_LANES), F32),
            pltpu.VMEM((SC_LANES,), I32),
            pltpu.SemaphoreType.DMA((SC_SLOTS,)),
        ],
        compiler_params=pltpu.CompilerParams(needs_layout_passes=False),
        cost_estimate=pl.CostEstimate(
            flops=4 * n * N_SEL * D_MODEL, transcendentals=n * N_SEL,
            bytes_accessed=4 * (2 * n * N_SEL * ROW_WORDS + n * ROW_WORDS + n * D_MODEL + 2 * n * N_SEL)),
        name="peer_experts",
    )
    return fn(h2p, eidx, gates, uv_pack)


def _resid_kernel(x1_ref, p_ref, mod_ref, o_ref):
    o_ref[...] = x1_ref[...] + mod_ref[0, 5:6, :] * p_ref[...]


def _resid_call(x1, peer_out, seq, mod3, mod_row0):
    n = x1.shape[0]
    tm = TOKEN_TILE
    per_seq = seq // tm
    tok = pl.BlockSpec((tm, D_MODEL), lambda i: (i, 0))
    return pl.pallas_call(
        _resid_kernel, grid=(n // tm,),
        in_specs=[tok, tok,
                  pl.BlockSpec((1, 6, D_MODEL), lambda i: (mod_row0 + (i // per_seq if mod_row0 else 0), 0, 0))],
        out_specs=tok, out_shape=jax.ShapeDtypeStruct((n, D_MODEL), F32),
        compiler_params=pltpu.CompilerParams(dimension_semantics=("parallel",)),
        name="resid",
    )(x1, peer_out, mod3)


def _rope_tables(seq, rotate):
    if not rotate:
        return jnp.ones((seq, KV_WIDTH), F32), jnp.zeros((seq, KV_WIDTH), F32)
    quarter = A_DH // 4
    t = jnp.arange(seq)
    row = (t // GRID_W).astype(F32)
    col = (t % GRID_W).astype(F32)
    inv = ROPE_BASE ** (-jnp.arange(quarter, dtype=F32) / quarter)
    d = jnp.arange(A_DH)
    pos = jnp.where(d[None, :] < A_DH // 2, row[:, None], col[:, None])
    ang = pos * inv[d % quarter][None, :]
    sign = jnp.where((d % (A_DH // 2)) < quarter, -1.0, 1.0).astype(F32)
    cos = jnp.cos(ang)
    sin = jnp.sin(ang) * sign[None, :]
    return jnp.tile(cos, (1, KV_WIDTH // A_DH)), jnp.tile(sin, (1, KV_WIDTH // A_DH))


def _run_chunk(x, mod3, mod_row0, prm, cache, rotate, gate_on):
    (n1, n2, w_main, w_g, w_gt, b_g, b_gt, mhg, qg_t, kg_t, bd, sink, w_m, w_a, w_q, sub_a, sub_b,
     uv_pack) = prm
    batch, seq, _ = x.shape
    n = batch * seq
    x2d = x.reshape(n, D_MODEL)
    if gate_on is not None:
        x2d, _ = lax.optimization_barrier((x2d, gate_on))
    cos, sin = _rope_tables(seq, rotate)
    mq, mk, mv, mo, gcol, grow, aq, ak, av = _inproj_call(
        x2d, seq, mod3, mod_row0, n1, w_main, w_g, w_gt, b_g, b_gt, qg_t, kg_t, bd, cos, sin)
    kc, vc, c0, m0 = cache
    h_f, h_b, c_fin, m_fin = _mlstm_call(mq, mk, mv, gcol, grow, c0, m0, batch, seq)
    if kc is None:
        a_out = _attn_ctx_call(sink, aq, ak, av, batch, seq)
    else:
        a_out = _attn_lat_call(sink, aq, ak, av, kc, vc, batch, seq)
    x1, h2p, eidx, gates = _mix_call(x2d, seq, h_f, h_b, mo, a_out, mod3, mod_row0, mhg, n2, w_m, w_a, w_q,
                                     sub_a, sub_b)
    peer_out = _peer_experts(h2p, eidx, gates, uv_pack)
    y = _resid_call(x1, peer_out, seq, mod3, mod_row0).reshape(batch, seq, D_MODEL)
    return y, h2p, peer_out, ak, av, c_fin, m_fin


def _pack_state(C, n_vec, m):
    b = C.shape[0]
    caug = jnp.concatenate([C, jnp.broadcast_to(n_vec[..., None], C.shape)], axis=-1)
    caug = caug.reshape(b, 2 * M_HEADS, M_DH, 2 * M_DH)
    m_rep = jnp.broadcast_to(m.reshape(b, 2 * M_HEADS, 1, 1), (b, 2 * M_HEADS, 8, M_DH))
    return caug.astype(F32), m_rep.astype(F32)


def kernel(x_prompt, x_sample, c, cache_attn_k, cache_attn_v, state_mlstm_C, state_mlstm_n, state_mlstm_m,
           c_ctx, w_ada, b_ada, norm1_g, norm2_g, w_in, b_gates, mh_norm_g, q_norm_g, k_norm_g, sink_logits,
           w_out, peer_w_q, peer_sub_a, peer_sub_b, peer_u, peer_v):
    depth = w_ada.shape[0]
    assert depth == 1
    batch, seq, _ = x_prompt.shape
    dec_batch, dec_seq, _ = x_sample.shape
    assert dec_batch + 1 <= MOD_ROWS and batch % sum(CTX_CHUNK_WEIGHTS) == 0 and dec_batch % LATENT_CHUNKS == 0
    l = 0

    cond = jnp.concatenate([c_ctx[None, :], c, jnp.zeros((MOD_ROWS - 1 - dec_batch, D_MODEL), F32)], axis=0)
    mod3 = _ada_call(cond, w_ada[l], b_ada[l]).reshape(MOD_ROWS, 6, D_MODEL)

    wi = w_in[l]
    g0 = 4 * M_WIDTH
    w_main = jnp.concatenate([wi[:, :g0], wi[:, g0 + N_GATES:]], axis=1).astype(BF16)
    w_g = wi[:, g0:g0 + N_GATES]
    seg = jnp.arange(A_WIDTH) // A_DH
    bd = jnp.where(seg[:, None] == seg[None, :], 1.0 / A_DH, 0.0).astype(F32)
    prm = (norm1_g[l][None, :], norm2_g[l][None, :], w_main, w_g, w_g.T, b_gates[l][None, :], b_gates[l][:, None],
           mh_norm_g[l][None, :], jnp.tile(q_norm_g[l], A_HEADS)[None, :], jnp.tile(k_norm_g[l], A_KV)[None, :], bd,
           sink_logits[l], w_out[l][:M_WIDTH].astype(BF16), w_out[l][M_WIDTH:].astype(BF16),
           peer_w_q[l].astype(BF16), peer_sub_a[l].astype(BF16), peer_sub_b[l].astype(BF16),
           _pack_tables_call(peer_u[l], peer_v[l]))

    zeros_c = jnp.zeros((batch, 2, M_HEADS, M_DH, M_DH), F32)
    c0, m0 = _pack_state(zeros_c, zeros_c[..., 0], jnp.full((batch, 2, M_HEADS), NEG, F32))
    c0s, m0s = _pack_state(state_mlstm_C[:, l], state_mlstm_n[:, l], state_mlstm_m[:, l])
    past = cache_attn_k.shape[2]
    kc = cache_attn_k[:, l].reshape(dec_batch, past, KV_WIDTH)
    vc = cache_attn_v[:, l].reshape(dec_batch, past, KV_WIDTH)

    jobs = []
    ctx_sizes = [batch * f // sum(CTX_CHUNK_WEIGHTS) for f in CTX_CHUNK_WEIGHTS]
    assert sum(ctx_sizes) == batch
    b0 = 0
    for size in ctx_sizes:
        b1 = b0 + size
        jobs.append((x_prompt[b0:b1], 0, (None, None, c0[b0:b1], m0[b0:b1]), False))
        b0 = b1
    for b0 in range(0, dec_batch, dec_batch // LATENT_CHUNKS):
        b1 = b0 + dec_batch // LATENT_CHUNKS
        jobs.append((x_sample[b0:b1], 1 + b0, (kc[b0:b1], vc[b0:b1], c0s[b0:b1], m0s[b0:b1]), True))
    outs = []
    for i, (x_c, mod_row0, cache, rotate) in enumerate(jobs):
        gate = tuple(g for g in (outs[i - 1][1] if i >= 1 else None,
                                 outs[i - EXPERT_LAG][2] if i >= EXPERT_LAG else None) if g is not None)
        outs.append(_run_chunk(x_c, mod3, mod_row0, prm, cache, rotate, gate or None))
    ctx, lat = outs[:len(ctx_sizes)], outs[len(ctx_sizes):]
    y_p = jnp.concatenate([o[0] for o in ctx], axis=0)
    y_s = jnp.concatenate([o[0] for o in lat], axis=0)
    k_new = jnp.concatenate([o[3] for o in ctx], axis=0)
    v_new = jnp.concatenate([o[4] for o in ctx], axis=0)
    c_fin = jnp.concatenate([o[5] for o in ctx], axis=0)
    m_fin = jnp.concatenate([o[6] for o in ctx], axis=0)

    c_fin = c_fin.reshape(batch, 2, M_HEADS, M_DH, 2 * M_DH)
    new_c = c_fin[..., :M_DH][:, None]
    new_n = c_fin[..., M_DH][:, None]
    new_m = m_fin[:, :, 0, 0].reshape(batch, 2, M_HEADS)[:, None]
    new_k = k_new.reshape(batch, 1, seq, A_KV, A_DH)
    new_v = v_new.reshape(batch, 1, seq, A_KV, A_DH)
    return y_p, y_s, new_k, new_v, new_c, new_n, new_m
```

```python
import functools
import math

import jax
import jax.numpy as jnp
from jax import lax
from jax.experimental import pallas as pl
from jax.experimental.pallas import tpu as pltpu
from jax.experimental.pallas import tpu_sc as plsc

F32 = jnp.float32
BF16 = jnp.bfloat16
I32 = jnp.int32
HI = lax.Precision.HIGHEST

D_MODEL = 1024
EPS = 1e-6
NEG = -1e30
GRID_W = 64
M_HEADS = 4
M_WIDTH = 512
M_DH = 128
A_HEADS = 8
A_KV = 2
A_REP = 4
A_DH = 64
A_WIDTH = 512
KV_WIDTH = A_KV * A_DH
BLOCK = 128
ROPE_BASE = 10000.0
N_KEYS = 128
P_HEADS = 8
P_DKEY = 256
P_HALF = 128
P_TOPK = 16
N_SEL = P_HEADS * P_TOPK
N_GATES = 4 * M_HEADS
MAIN_COLS = 4 * M_WIDTH + A_WIDTH + 2 * KV_WIDTH
MOD_ROWS = 16

TOKEN_TILE = 256
MLSTM_CHUNK = 128
ADA_COL_TILE = 768
CTX_CHUNK_WEIGHTS = (1, 2, 3, 4, 6)
EXPERT_LAG = 3
LATENT_CHUNKS = 8


def _sigmoid(x):
    return 1.0 / (1.0 + jnp.exp(-x))


def _log_sigmoid(x):
    return jnp.minimum(x, 0.0) - jnp.log1p(jnp.exp(-jnp.abs(x)))


def _dot_t(a, b, precision=None):
    return lax.dot_general(a, b, (((1,), (1,)), ((), ())), precision=precision,
                           preferred_element_type=F32)


def _ada_kernel(c_ref, w_ref, b_ref, o_ref):
    c = c_ref[...]
    s = c * _sigmoid(c)
    o_ref[...] = jnp.dot(s, w_ref[...], precision=HI, preferred_element_type=F32) + b_ref[...]


def _ada_call(cond, w_ada, b_ada):
    n_out = w_ada.shape[1]
    return pl.pallas_call(
        _ada_kernel,
        grid=(n_out // ADA_COL_TILE,),
        in_specs=[pl.BlockSpec((MOD_ROWS, D_MODEL), lambda j: (0, 0)),
                  pl.BlockSpec((D_MODEL, ADA_COL_TILE), lambda j: (0, j)),
                  pl.BlockSpec((1, ADA_COL_TILE), lambda j: (0, j))],
        out_specs=pl.BlockSpec((MOD_ROWS, ADA_COL_TILE), lambda j: (0, j)),
        out_shape=jax.ShapeDtypeStruct((MOD_ROWS, n_out), F32),
        name="ada",
    )(cond, w_ada, b_ada.reshape(1, n_out))


def _swap16(x):
    n = x.shape[-1]
    lane = lax.broadcasted_iota(I32, x.shape, x.ndim - 1)
    return jnp.where((lane & 16) == 0, pltpu.roll(x, n - 16, x.ndim - 1), pltpu.roll(x, 16, x.ndim - 1))


def _inproj_kernel(x_ref, mod_ref, n1_ref, w_ref, wg_ref, wgt_ref, bg_ref, bgt_ref, qg_ref, kg_ref,
                   bd_ref, cos_ref, sin_ref,
                   mq_ref, mk_ref, mv_ref, mo_ref, gc_ref, gr_ref, aq_ref, ak_ref, av_ref):
    x = x_ref[...]
    h = x * lax.rsqrt(jnp.mean(x * x, axis=-1, keepdims=True) + EPS) * n1_ref[...]
    h = h * (1.0 + mod_ref[0, 1:2, :]) + mod_ref[0, 0:1, :]
    z = jnp.dot(h.astype(BF16), w_ref[...], preferred_element_type=F32)

    mq_ref[...] = (z[:, 0:M_WIDTH] * (M_DH ** -0.5)).astype(BF16)
    mk_ref[...] = z[:, M_WIDTH:2 * M_WIDTH].astype(BF16)
    mv_ref[...] = z[:, 2 * M_WIDTH:3 * M_WIDTH].astype(BF16)
    mo_ref[...] = z[:, 3 * M_WIDTH:4 * M_WIDTH]

    g = jnp.dot(h, wg_ref[...], precision=HI, preferred_element_type=F32) + bg_ref[...]
    kind = lax.broadcasted_iota(I32, g.shape, 1) // M_HEADS
    gc_ref[...] = jnp.where((kind & 1) == 1, _log_sigmoid(g), g)
    gt = _dot_t(wgt_ref[...], h, precision=HI) + bgt_ref[...]
    kind_t = lax.broadcasted_iota(I32, gt.shape, 0) // M_HEADS
    gr_ref[...] = jnp.where((kind_t & 1) == 1, _log_sigmoid(gt), gt)

    o = 4 * M_WIDTH
    aq = z[:, o:o + A_WIDTH]
    ak = z[:, o + A_WIDTH:o + A_WIDTH + KV_WIDTH]
    av_ref[...] = z[:, o + A_WIDTH + KV_WIDTH:o + A_WIDTH + 2 * KV_WIDTH]
    bd = bd_ref[...]
    cos = cos_ref[...]
    sin = sin_ref[...]
    aq = aq * lax.rsqrt(jnp.dot(aq * aq, bd, precision=HI, preferred_element_type=F32) + EPS) * qg_ref[...]
    cos4 = jnp.concatenate([cos] * (A_WIDTH // KV_WIDTH), axis=1)
    sin4 = jnp.concatenate([sin] * (A_WIDTH // KV_WIDTH), axis=1)
    aq = (aq * cos4 + _swap16(aq) * sin4) * (A_DH ** -0.5)
    ak = ak * lax.rsqrt(jnp.dot(ak * ak, bd[0:KV_WIDTH, 0:KV_WIDTH], precision=HI,
                                preferred_element_type=F32) + EPS) * kg_ref[...]
    ak_ref[...] = ak * cos + _swap16(ak) * sin

    lane = lax.broadcasted_iota(I32, (aq.shape[0], KV_WIDTH), 1)
    for hd in range(A_HEADS):
        grp = hd // A_REP
        blk = aq[:, (hd // 2) * KV_WIDTH:(hd // 2 + 1) * KV_WIDTH]
        if hd % 2 != grp:
            blk = pltpu.roll(blk, A_DH, 1)
        keep = (lane >= grp * A_DH) & (lane < (grp + 1) * A_DH)
        aq_ref[hd] = jnp.where(keep, blk, 0.0).astype(BF16)


def _inproj_call(x2d, seq, mod3, mod_row0, n1, w_main, w_g, w_gt, b_g, b_gt, qg_t, kg_t, bd, cos, sin):
    n = x2d.shape[0]
    tm = TOKEN_TILE
    per_seq = seq // tm

    def tok(i):
        return (i, 0)

    def const2(i):
        return (0, 0)

    in_specs = [
        pl.BlockSpec((tm, D_MODEL), tok),
        pl.BlockSpec((1, 6, D_MODEL), lambda i: (mod_row0 + (i // per_seq if mod_row0 else 0), 0, 0)),
        pl.BlockSpec((1, D_MODEL), const2),
        pl.BlockSpec((D_MODEL, MAIN_COLS), const2),
        pl.BlockSpec((D_MODEL, N_GATES), const2),
        pl.BlockSpec((N_GATES, D_MODEL), const2),
        pl.BlockSpec((1, N_GATES), const2),
        pl.BlockSpec((N_GATES, 1), const2),
        pl.BlockSpec((1, A_WIDTH), const2),
        pl.BlockSpec((1, KV_WIDTH), const2),
        pl.BlockSpec((A_WIDTH, A_WIDTH), const2),
        pl.BlockSpec((tm, KV_WIDTH), lambda i: (i % per_seq, 0)),
        pl.BlockSpec((tm, KV_WIDTH), lambda i: (i % per_seq, 0)),
    ]
    out_specs = [
        pl.BlockSpec((tm, M_WIDTH), tok),
        pl.BlockSpec((tm, M_WIDTH), tok),
        pl.BlockSpec((tm, M_WIDTH), tok),
        pl.BlockSpec((tm, M_WIDTH), tok),
        pl.BlockSpec((tm, N_GATES), tok),
        pl.BlockSpec((N_GATES, tm), lambda i: (0, i)),
        pl.BlockSpec((A_HEADS, tm, KV_WIDTH), lambda i: (0, i, 0)),
        pl.BlockSpec((tm, KV_WIDTH), tok),
        pl.BlockSpec((tm, KV_WIDTH), tok),
    ]
    out_shape = [
        jax.ShapeDtypeStruct((n, M_WIDTH), BF16),
        jax.ShapeDtypeStruct((n, M_WIDTH), BF16),
        jax.ShapeDtypeStruct((n, M_WIDTH), BF16),
        jax.ShapeDtypeStruct((n, M_WIDTH), F32),
        jax.ShapeDtypeStruct((n, N_GATES), F32),
        jax.ShapeDtypeStruct((N_GATES, n), F32),
        jax.ShapeDtypeStruct((A_HEADS, n, KV_WIDTH), BF16),
        jax.ShapeDtypeStruct((n, KV_WIDTH), F32),
        jax.ShapeDtypeStruct((n, KV_WIDTH), F32),
    ]
    return pl.pallas_call(
        _inproj_kernel, grid=(n // tm,), in_specs=in_specs, out_specs=out_specs, out_shape=out_shape,
        compiler_params=pltpu.CompilerParams(dimension_semantics=("parallel",)),
        name="inproj",
    )(x2d, mod3, n1, w_main, w_g, w_gt, b_g, b_gt, qg_t, kg_t, bd, cos, sin)


def _mlstm_chain(q, k, v, li_c, lf_c, li_r, lf_r, caug, m, tri, tri_t, mask, reverse):
    L = q.shape[0]
    last = 0 if reverse else L - 1
    b_c = jnp.dot(tri, jnp.broadcast_to(lf_c, (L, L)), precision=HI, preferred_element_type=F32)
    b_r = jnp.dot(jnp.broadcast_to(lf_r, (8, L)), tri_t, precision=HI, preferred_element_type=F32)[0:1, :]
    a_inter = b_c[:, 0:1] + m
    d = jnp.where(mask, b_c - b_r + li_r, -jnp.inf)
    m_t = jnp.maximum(a_inter, jnp.max(d, axis=1, keepdims=True))
    w_inter = jnp.exp(a_inter - m_t)
    s = _dot_t(q, k) * jnp.exp(d - m_t)
    qc = jnp.dot(q, caug.astype(BF16), preferred_element_type=F32)
    num = jnp.dot(s.astype(BF16), v, preferred_element_type=F32) + w_inter * qc[:, 0:M_DH]
    den = jnp.sum(s, axis=1, keepdims=True) + w_inter * qc[:, M_DH:M_DH + 1]
    den = jnp.maximum(jnp.abs(den), jnp.exp(-m_t))
    h = num / den
    m_new = m_t[last:last + 1, :]
    b_last = b_c[last:last + 1, 0:1]
    g_c = jnp.exp(b_last - b_c[:, 0:1] + li_c - m_new)
    decay = jnp.exp(b_last + m - m_new)
    kw = (k.astype(F32) * g_c).astype(BF16)
    vaug = jnp.concatenate([v, jnp.ones_like(v)], axis=1)
    upd = lax.dot_general(kw, vaug, (((0,), (0,)), ((), ())), preferred_element_type=F32)
    return h, decay * caug + upd, m_new


def _mlstm_kernel(qf_ref, kf_ref, vf_ref, gcf_ref, grf_ref, qb_ref, kb_ref, vb_ref, gcb_ref, grb_ref,
                  c0_ref, m0_ref, hf_ref, hb_ref, cfin_ref, mfin_ref, c_scr, m_scr):
    c = pl.program_id(1)
    nc = pl.num_programs(1)
    L = qf_ref.shape[0]

    @pl.when(c == 0)
    def _():
        c_scr[...] = c0_ref[0]
        m_scr[...] = m0_ref[0]

    row = lax.broadcasted_iota(I32, (L, L), 0)
    col = lax.broadcasted_iota(I32, (L, L), 1)
    lower = row >= col
    upper = row <= col
    lower_f = lower.astype(F32)
    upper_f = upper.astype(F32)

    for direction in range(2):
        reverse = direction == 1
        q_ref, k_ref, v_ref, gc_ref, gr_ref, h_ref = (
            (qb_ref, kb_ref, vb_ref, gcb_ref, grb_ref, hb_ref) if reverse
            else (qf_ref, kf_ref, vf_ref, gcf_ref, grf_ref, hf_ref))
        tri, tri_t, mask = (upper_f, lower_f, upper) if reverse else (lower_f, upper_f, lower)
        gc = gc_ref[...]
        gr = gr_ref[...]
        for hd in range(M_HEADS):
            ch = direction * M_HEADS + hd
            sl = slice(hd * M_DH, (hd + 1) * M_DH)
            ci = 2 * direction * M_HEADS + hd
            cf = ci + M_HEADS
            h, caug, m_new = _mlstm_chain(
                q_ref[:, sl], k_ref[:, sl], v_ref[:, sl],
                gc[:, ci:ci + 1], gc[:, cf:cf + 1], gr[ci:ci + 1, :], gr[cf:cf + 1, :],
                c_scr[ch], m_scr[ch][0:1, 0:1], tri, tri_t, mask, reverse)
            h_ref[:, sl] = h
            c_scr[ch] = caug
            m_scr[ch] = jnp.broadcast_to(m_new, m_scr.shape[1:])

    @pl.when(c == nc - 1)
    def _():
        cfin_ref[0] = c_scr[...]
        mfin_ref[0] = m_scr[...]


def _mlstm_call(mq, mk, mv, gcol, grow, c0, m0, batch, seq):
    n = mq.shape[0]
    L = MLSTM_CHUNK
    nc = seq // L
    n_ch = 2 * M_HEADS

    def fwd(b, c):
        return (b * nc + c, 0)

    def bwd(b, c):
        return (b * nc + nc - 1 - c, 0)

    def fwd_t(b, c):
        return (0, b * nc + c)

    def bwd_t(b, c):
        return (0, b * nc + nc - 1 - c)

    tok = pl.BlockSpec((L, M_WIDTH), fwd)
    tok_b = pl.BlockSpec((L, M_WIDTH), bwd)
    in_specs = [tok, tok, tok, pl.BlockSpec((L, N_GATES), fwd), pl.BlockSpec((N_GATES, L), fwd_t),
                tok_b, tok_b, tok_b, pl.BlockSpec((L, N_GATES), bwd), pl.BlockSpec((N_GATES, L), bwd_t),
                pl.BlockSpec((1, n_ch, M_DH, 2 * M_DH), lambda b, c: (b, 0, 0, 0)),
                pl.BlockSpec((1, n_ch, 8, M_DH), lambda b, c: (b, 0, 0, 0))]
    out_specs = [tok, tok_b,
                 pl.BlockSpec((1, n_ch, M_DH, 2 * M_DH), lambda b, c: (b, 0, 0, 0)),
                 pl.BlockSpec((1, n_ch, 8, M_DH), lambda b, c: (b, 0, 0, 0))]
    out_shape = [jax.ShapeDtypeStruct((n, M_WIDTH), F32), jax.ShapeDtypeStruct((n, M_WIDTH), F32),
                 jax.ShapeDtypeStruct((batch, n_ch, M_DH, 2 * M_DH), F32),
                 jax.ShapeDtypeStruct((batch, n_ch, 8, M_DH), F32)]
    return pl.pallas_call(
        _mlstm_kernel, grid=(batch, nc), in_specs=in_specs, out_specs=out_specs, out_shape=out_shape,
        scratch_shapes=[pltpu.VMEM((n_ch, M_DH, 2 * M_DH), F32), pltpu.VMEM((n_ch, 8, M_DH), F32)],
        compiler_params=pltpu.CompilerParams(dimension_semantics=("parallel", "arbitrary")),
        name="mlstm",
    )(mq, mk, mv, gcol, grow, mq, mk, mv, gcol, grow, c0, m0)


def _sink_column(sink_ref, grp, rows_per_head):
    return jnp.concatenate(
        [jnp.full((rows_per_head, 1), sink_ref[grp * A_REP + r], F32) for r in range(A_REP)], axis=0)


def _store_heads(out_ref, o, grp, rows_per_head):
    for r in range(A_REP):
        hd = grp * A_REP + r
        out_ref[:, hd * A_DH:(hd + 1) * A_DH] = o[r * rows_per_head:(r + 1) * rows_per_head,
                                                  grp * A_DH:(grp + 1) * A_DH].astype(out_ref.dtype)


def _attn_ctx_kernel(sink_ref, q_ref, k_ref, v_ref, out_ref):
    s_len = k_ref.shape[0]
    k = k_ref[...].astype(BF16)
    v = v_ref[...].astype(BF16)
    for grp in range(A_KV):
        q = q_ref[grp * A_REP:(grp + 1) * A_REP].reshape(A_REP * s_len, KV_WIDTH)
        s = _dot_t(q, k)
        sk = _sink_column(sink_ref, grp, s_len)
        mx = jnp.maximum(jnp.max(s, axis=1, keepdims=True), sk)
        p = jnp.exp(s - mx)
        den = jnp.sum(p, axis=1, keepdims=True) + jnp.exp(sk - mx)
        o = jnp.dot(p.astype(BF16), v, preferred_element_type=F32) / den
        _store_heads(out_ref, o, grp, s_len)


def _attn_ctx_call(sink, aq, ak, av, batch, seq):
    n = ak.shape[0]
    return pl.pallas_call(
        _attn_ctx_kernel, grid=(batch,),
        in_specs=[pl.BlockSpec(memory_space=pltpu.SMEM),
                  pl.BlockSpec((A_HEADS, seq, KV_WIDTH), lambda b: (0, b, 0)),
                  pl.BlockSpec((seq, KV_WIDTH), lambda b: (b, 0)),
                  pl.BlockSpec((seq, KV_WIDTH), lambda b: (b, 0))],
        out_specs=pl.BlockSpec((seq, A_WIDTH), lambda b: (b, 0)),
        out_shape=jax.ShapeDtypeStruct((n, A_WIDTH), BF16),
        compiler_params=pltpu.CompilerParams(dimension_semantics=("parallel",)),
        name="attn_ctx",
    )(sink, aq, ak, av)


def _attn_lat_kernel(sink_ref, q_ref, kc_ref, vc_ref, kp_ref, kq_ref, kn_ref, vp_ref, vq_ref, vn_ref, out_ref):
    i = pl.program_id(1)
    nb = pl.num_programs(1)
    kc = kc_ref[0].astype(BF16)
    vc = vc_ref[0].astype(BF16)
    kp, kq, kn = kp_ref[...].astype(BF16), kq_ref[...].astype(BF16), kn_ref[...].astype(BF16)
    vp, vq, vn = vp_ref[...].astype(BF16), vq_ref[...].astype(BF16), vn_ref[...].astype(BF16)
    rows = A_REP * BLOCK
    qpos = lax.broadcasted_iota(I32, (rows, BLOCK), 0) % BLOCK
    kpos = lax.broadcasted_iota(I32, (rows, BLOCK), 1)
    mask_p = (kpos >= qpos) & (i > 0)
    mask_n = (kpos <= qpos) & (i < nb - 1)
    for grp in range(A_KV):
        q = q_ref[grp * A_REP:(grp + 1) * A_REP].reshape(rows, KV_WIDTH)
        s_c = _dot_t(q, kc)
        s_p = jnp.where(mask_p, _dot_t(q, kp), NEG)
        s_q = _dot_t(q, kq)
        s_n = jnp.where(mask_n, _dot_t(q, kn), NEG)
        sk = _sink_column(sink_ref, grp, BLOCK)
        mx = jnp.maximum(jnp.maximum(jnp.max(s_c, axis=1, keepdims=True), jnp.max(s_p, axis=1, keepdims=True)),
                         jnp.maximum(jnp.max(s_q, axis=1, keepdims=True), jnp.max(s_n, axis=1, keepdims=True)))
        mx = jnp.maximum(mx, sk)
        p_c, p_p, p_q, p_n = jnp.exp(s_c - mx), jnp.exp(s_p - mx), jnp.exp(s_q - mx), jnp.exp(s_n - mx)
        den = (jnp.sum(p_c, axis=1, keepdims=True) + jnp.sum(p_p, axis=1, keepdims=True)
               + jnp.sum(p_q, axis=1, keepdims=True) + jnp.sum(p_n, axis=1, keepdims=True) + jnp.exp(sk - mx))
        o = (jnp.dot(p_c.astype(BF16), vc, preferred_element_type=F32)
             + jnp.dot(p_p.astype(BF16), vp, preferred_element_type=F32)
             + jnp.dot(p_q.astype(BF16), vq, preferred_element_type=F32)
             + jnp.dot(p_n.astype(BF16), vn, preferred_element_type=F32)) / den
        _store_heads(out_ref, o, grp, BLOCK)


def _attn_lat_call(sink, aq, ak, av, kc, vc, batch, seq):
    n = ak.shape[0]
    nb = seq // BLOCK
    past = kc.shape[1]

    def cur(b, i):
        return (b * nb + i, 0)

    def prev(b, i):
        return (b * nb + jnp.maximum(i - 1, 0), 0)

    def nxt(b, i):
        return (b * nb + jnp.minimum(i + 1, nb - 1), 0)

    blk = functools.partial(pl.BlockSpec, (BLOCK, KV_WIDTH))
    cache = pl.BlockSpec((1, past, KV_WIDTH), lambda b, i: (b, 0, 0))
    return pl.pallas_call(
        _attn_lat_kernel, grid=(batch, nb),
        in_specs=[pl.BlockSpec(memory_space=pltpu.SMEM),
                  pl.BlockSpec((A_HEADS, BLOCK, KV_WIDTH), lambda b, i: (0, b * nb + i, 0)),
                  cache, cache, blk(prev), blk(cur), blk(nxt), blk(prev), blk(cur), blk(nxt)],
        out_specs=pl.BlockSpec((BLOCK, A_WIDTH), cur),
        out_shape=jax.ShapeDtypeStruct((n, A_WIDTH), BF16),
        compiler_params=pltpu.CompilerParams(dimension_semantics=("parallel", "parallel")),
        name="attn_lat",
    )(sink, aq, kc, vc, ak, ak, ak, av, av, av)


def _top16_rows(s, payload=None):
    n_rows = s.shape[0]
    rows = lax.broadcasted_iota(I32, s.shape, 0).astype(F32)
    vals, idxs, pays = [], [], []
    for _ in range(P_TOPK):
        mx = jnp.max(s, axis=0, keepdims=True)
        ix = jnp.min(jnp.where(s == mx, rows, float(n_rows)), axis=0, keepdims=True)
        hit = rows == ix
        vals.append(mx)
        idxs.append(ix)
        if payload is not None:
            pays.append(jnp.sum(jnp.where(hit, payload, 0.0), axis=0, keepdims=True))
        s = jnp.where(hit, -jnp.inf, s)
    out = (jnp.concatenate(vals, axis=0), jnp.concatenate(idxs, axis=0))
    if payload is not None:
        out += (jnp.concatenate(pays, axis=0),)
    return out


def _mix_kernel(x_ref, hf_ref, hb_ref, mo_ref, ao_ref, mod_ref, mhg_ref, n2_ref, wm_ref, wa_ref, wq_ref,
                sa_ref, sb_ref, x1_ref, h2_ref, eidx_ref, gate_ref, qp_scr, e_scr, g_scr):
    tm = x_ref.shape[0]
    hs = hf_ref[...] + hb_ref[...]
    parts = []
    for hd in range(M_HEADS):
        blk = hs[:, hd * M_DH:(hd + 1) * M_DH]
        parts.append(blk * lax.rsqrt(jnp.mean(blk * blk, axis=-1, keepdims=True) + EPS))
    m_out = _sigmoid(mo_ref[...]) * (jnp.concatenate(parts, axis=1) * mhg_ref[...])
    mix = (jnp.dot(m_out.astype(BF16), wm_ref[...], preferred_element_type=F32)
           + jnp.dot(ao_ref[...], wa_ref[...], preferred_element_type=F32))
    x1 = x_ref[...] + mod_ref[0, 2:3, :] * mix
    x1_ref[...] = x1
    h2 = x1 * lax.rsqrt(jnp.mean(x1 * x1, axis=-1, keepdims=True) + EPS) * n2_ref[...]
    h2 = h2 * (1.0 + mod_ref[0, 4:5, :]) + mod_ref[0, 3:4, :]
    h2_ref[...] = _pack_bf16_pairs(h2)
    qp = jnp.dot(h2.astype(BF16), wq_ref[...], preferred_element_type=F32)
    for p in range(P_HEADS):
        qp_scr[p] = qp[:, p * P_DKEY:(p + 1) * P_DKEY].astype(BF16)
    sub_a = sa_ref[...]
    sub_b = sb_ref[...]

    def head_body(p, carry):
        for half in range(tm // N_KEYS):
            cols = slice(half * N_KEYS, (half + 1) * N_KEYS)
            qh = qp_scr[p, pl.ds(half * N_KEYS, N_KEYS), :]
            s_a = _dot_t(sub_a, qh[:, 0:P_HALF])
            s_b = _dot_t(sub_b, qh[:, P_HALF:P_DKEY])
            va, ia = _top16_rows(s_a)
            vb, ib = _top16_rows(s_b)
            keep = [P_TOPK // (i + 1) for i in range(P_TOPK)]
            pad = -sum(keep) % 8
            cand = jnp.concatenate([va[i:i + 1, :] + vb[0:keep[i], :] for i in range(P_TOPK)]
                                   + [jnp.full((pad, N_KEYS), -jnp.inf, F32)], axis=0)
            cidx = jnp.concatenate([ia[i:i + 1, :] * float(N_KEYS) + ib[0:keep[i], :] for i in range(P_TOPK)]
                                   + [jnp.zeros((pad, N_KEYS), F32)], axis=0)
            top, _, eidx = _top16_rows(cand, cidx)
            ex = jnp.exp(top - jnp.max(top, axis=0, keepdims=True))
            gates = ex / jnp.sum(ex, axis=0, keepdims=True)
            r0 = pl.multiple_of(p * P_TOPK, P_TOPK)
            e_scr[pl.ds(r0, P_TOPK), cols] = eidx
            g_scr[pl.ds(r0, P_TOPK), cols] = gates
        return carry

    lax.fori_loop(0, P_HEADS, head_body, 0)
    for half in range(tm // N_KEYS):
        cols = slice(half * N_KEYS, (half + 1) * N_KEYS)
        eidx_ref[cols, :] = e_scr[:, cols].T.astype(I32)
        gate_ref[cols, :] = g_scr[:, cols].T


def _mix_call(x2d, seq, h_f, h_b, mo, a_out, mod3, mod_row0, mhg, n2, w_m, w_a, w_q, sub_a, sub_b):
    n = x2d.shape[0]
    tm = TOKEN_TILE
    per_seq = seq // tm

    def tok(i):
        return (i, 0)

    def const2(i):
        return (0, 0)

    in_specs = [
        pl.BlockSpec((tm, D_MODEL), tok),
        pl.BlockSpec((tm, M_WIDTH), tok), pl.BlockSpec((tm, M_WIDTH), tok), pl.BlockSpec((tm, M_WIDTH), tok),
        pl.BlockSpec((tm, A_WIDTH), tok),
        pl.BlockSpec((1, 6, D_MODEL), lambda i: (mod_row0 + (i // per_seq if mod_row0 else 0), 0, 0)),
        pl.BlockSpec((1, M_WIDTH), const2),
        pl.BlockSpec((1, D_MODEL), const2),
        pl.BlockSpec((M_WIDTH, D_MODEL), const2),
        pl.BlockSpec((A_WIDTH, D_MODEL), const2),
        pl.BlockSpec((D_MODEL, P_HEADS * P_DKEY), const2),
        pl.BlockSpec((N_KEYS, P_HALF), const2),
        pl.BlockSpec((N_KEYS, P_HALF), const2),
    ]
    out_specs = [pl.BlockSpec((tm, D_MODEL), tok), pl.BlockSpec((tm, D_MODEL // 2), tok),
                 pl.BlockSpec((tm, N_SEL), tok), pl.BlockSpec((tm, N_SEL), tok)]
    out_shape = [jax.ShapeDtypeStruct((n, D_MODEL), F32), jax.ShapeDtypeStruct((n, D_MODEL // 2), I32),
                 jax.ShapeDtypeStruct((n, N_SEL), I32), jax.ShapeDtypeStruct((n, N_SEL), F32)]
    return pl.pallas_call(
        _mix_kernel, grid=(n // tm,), in_specs=in_specs, out_specs=out_specs, out_shape=out_shape,
        scratch_shapes=[pltpu.VMEM((P_HEADS, tm, P_DKEY), BF16), pltpu.VMEM((N_SEL, tm), F32),
                        pltpu.VMEM((N_SEL, tm), F32)],
        compiler_params=pltpu.CompilerParams(dimension_semantics=("parallel",)),
        name="mix",
    )(x2d, h_f, h_b, mo, a_out, mod3, mhg, n2, w_m, w_a, w_q, sub_a, sub_b)


SC_LANES = 16
SC_CORES = 2
SC_SUBCORES = 16
SC_WORKERS = SC_CORES * SC_SUBCORES
SC_TOKENS = 16
SC_GROUP = SC_LANES
SC_NGROUPS = N_SEL // SC_GROUP
SC_SLOTS = 4
ROW_WORDS = D_MODEL // 2
SC_DOT_ROWS = 8
SC_DOT_PARTIALS = 2
SC_OWORDS = 16 * SC_LANES
HI_MASK = -65536
PACK_ROWS = 512
SCORE_TOKENS = 512
SCORE_EXPERTS = 2048
DENSE_SCORE_EVERY = 2
GELU_C0 = 0.7978845608028654
GELU_C1 = 0.044715


def _pack_bf16_pairs(x):
    half = x.shape[1] // 2
    bits = lax.bitcast_convert_type(x.astype(BF16).astype(F32), I32)
    return (bits[:, :half] & HI_MASK) | lax.shift_right_logical(bits[:, half:], jnp.int32(16))


def _pack_tables_kernel(u_ref, v_ref, uv_ref, vp_ref, ub_ref):
    v_words = _pack_bf16_pairs(v_ref[...])
    uv_ref[:, 0:ROW_WORDS] = _pack_bf16_pairs(u_ref[...])
    uv_ref[:, ROW_WORDS:2 * ROW_WORDS] = v_words
    vp_ref[...] = v_words
    ub_ref[...] = u_ref[...].astype(BF16)


def _pack_tables_call(u_tab, v_tab):
    n_exp = u_tab.shape[0]
    blk = pl.BlockSpec((PACK_ROWS, D_MODEL), lambda i: (i, 0))
    half = pl.BlockSpec((PACK_ROWS, ROW_WORDS), lambda i: (i, 0))
    return pl.pallas_call(
        _pack_tables_kernel, grid=(n_exp // PACK_ROWS,), in_specs=[blk, blk], out_specs=[blk, half, blk],
        out_shape=[jax.ShapeDtypeStruct((n_exp, 2 * ROW_WORDS), I32), jax.ShapeDtypeStruct((n_exp, ROW_WORDS), I32),
                   jax.ShapeDtypeStruct((n_exp, D_MODEL), BF16)],
        compiler_params=pltpu.CompilerParams(dimension_semantics=("parallel",)),
        name="pack_tables",
    )(u_tab, v_tab)


def _unpack_bf16_pairs(words):
    hi = lax.bitcast_convert_type(words & HI_MASK, F32)
    lo = lax.bitcast_convert_type(lax.shift_left(words, jnp.int32(16)), F32)
    return jnp.concatenate([hi, lo], axis=1)


def _scores_kernel(x_ref, u_ref, o_ref):
    o_ref[...] = _dot_t(_unpack_bf16_pairs(x_ref[...]).astype(BF16), u_ref[...])


def _scores_call(h2p, u_bf16):
    n = h2p.shape[0]
    n_exp = u_bf16.shape[0]
    return pl.pallas_call(
        _scores_kernel, grid=(n // SCORE_TOKENS, n_exp // SCORE_EXPERTS),
        in_specs=[pl.BlockSpec((SCORE_TOKENS, ROW_WORDS), lambda i, j: (i, 0)),
                  pl.BlockSpec((SCORE_EXPERTS, D_MODEL), lambda i, j: (j, 0))],
        out_specs=pl.BlockSpec((SCORE_TOKENS, SCORE_EXPERTS), lambda i, j: (i, j)),
        out_shape=jax.ShapeDtypeStruct((n, n_exp), F32),
        compiler_params=pltpu.CompilerParams(dimension_semantics=("parallel", "parallel")),
        name="expert_scores",
    )(h2p, u_bf16)


def _sc_gelu(a):
    z = GELU_C0 * (a + GELU_C1 * (a * a * a))
    tanh = 1.0 - 2.0 / (jnp.exp(2.0 * z) + 1.0)
    return 0.5 * a * (1.0 + tanh)


def _sc_split(words):
    return (plsc.bitcast(words & HI_MASK, F32), plsc.bitcast(lax.shift_left(words, jnp.int32(16)), F32))


def _sc_mul_bf16(a_words, b_words):
    return plsc.bitcast(a_words, BF16) * plsc.bitcast(b_words, BF16)


def _sc_split_sum(p, q):
    return _sc_split(plsc.bitcast(p + q, I32))


def _peer_sc_kernel(h2_hbm, eidx_hbm, gate_hbm, uv_hbm, out_hbm,
                    xbuf, ibuf, gbuf, obuf, uvbuf, mbuf, wbuf, sems, sems_in, sems_out):
    n = h2_hbm.shape[0]
    per_worker = n // SC_WORKERS
    blk_tokens = xbuf.shape[1]
    n_blocks = per_worker // blk_tokens
    wid = lax.axis_index("c") * SC_SUBCORES + lax.axis_index("s")
    lane = lax.iota(I32, SC_LANES)

    def split_item(item):
        return lax.shift_right_logical(item, SC_NGROUPS.bit_length() - 1), item & (SC_NGROUPS - 1)

    def gather_copies(bset, item, slot):
        t, g = split_item(item)
        idx = ibuf[bset, t, pl.ds(g * SC_GROUP, SC_GROUP)]
        return (pltpu.make_async_copy(uv_hbm.at[idx], uvbuf.at[slot], sems.at[slot]),)

    def dots(bset, t, slot):
        zero = jnp.zeros((SC_LANES,), F32)

        @pl.loop(0, SC_GROUP, step=SC_DOT_ROWS)
        def _(r0):
            accs = [[zero] * SC_DOT_PARTIALS for _ in range(SC_DOT_ROWS)]
            for k in range(0, ROW_WORDS // SC_LANES, 4):
                xs = [xbuf[bset, t, pl.ds((k + q) * SC_LANES, SC_LANES)] for q in range(4)]
                for i in range(SC_DOT_ROWS):
                    m = [_sc_mul_bf16(xs[q], uvbuf[slot, r0 + i, pl.ds((k + q) * SC_LANES, SC_LANES)])
                         for q in range(4)]
                    hi, lo = _sc_split_sum(m[0] + m[1], m[2] + m[3])
                    p = (k // 4) % SC_DOT_PARTIALS
                    accs[i][p] = accs[i][p] + (hi + lo)
            for i in range(SC_DOT_ROWS):
                mbuf[r0 + i, :] = functools.reduce(lambda a, b: a + b, accs[i])

        tot = zero
        for c in range(SC_LANES):
            tot = tot + plsc.load_gather(mbuf, [lane, jnp.full((SC_LANES,), c, I32)])
        return tot

    def accumulate(bset, t, slot):
        nv = SC_OWORDS // SC_LANES
        for oc in range(ROW_WORDS // SC_OWORDS):
            w0 = oc * SC_OWORDS
            accs = (tuple(obuf[bset, t, pl.ds(w0 + j * SC_LANES, SC_LANES)] for j in range(nv))
                    + tuple(obuf[bset, t, pl.ds(ROW_WORDS + w0 + j * SC_LANES, SC_LANES)] for j in range(nv)))

            def row_quad(rq, accs):
                r = 4 * rq
                ws = [plsc.load_gather(wbuf, [jnp.full((SC_LANES,), r + q, I32)]) for q in range(4)]
                his, los = [], []
                for j in range(nv):
                    m = [_sc_mul_bf16(ws[q], uvbuf[slot, r + q, pl.ds(ROW_WORDS + w0 + j * SC_LANES, SC_LANES)])
                         for q in range(4)]
                    hi, lo = _sc_split_sum(m[0] + m[1], m[2] + m[3])
                    his.append(accs[j] + hi)
                    los.append(accs[nv + j] + lo)
                return tuple(his) + tuple(los)

            accs = lax.fori_loop(0, SC_GROUP // 4, row_quad, accs)
            for j in range(nv):
                obuf[bset, t, pl.ds(w0 + j * SC_LANES, SC_LANES)] = accs[j]
                obuf[bset, t, pl.ds(ROW_WORDS + w0 + j * SC_LANES, SC_LANES)] = accs[nv + j]

    def pack_weights(w):
        bits = plsc.bitcast(w, I32)
        rounded = (bits + 0x7FFF + (lax.shift_right_logical(bits, jnp.int32(16)) & 1)) & HI_MASK
        return rounded | lax.shift_right_logical(rounded, jnp.int32(16))

    n_items = blk_tokens * SC_NGROUPS

    def block_rows(blk):
        return pl.ds(pl.multiple_of(wid * per_worker + blk * blk_tokens, blk_tokens), blk_tokens)

    def load_copies(blk, bset):
        rows = block_rows(blk)
        return (pltpu.make_async_copy(h2_hbm.at[rows], xbuf.at[bset], sems_in.at[bset]),
                pltpu.make_async_copy(eidx_hbm.at[rows], ibuf.at[bset], sems_in.at[bset]),
                pltpu.make_async_copy(gate_hbm.at[rows], gbuf.at[bset], sems_in.at[bset]))

    def store_copy(blk, bset):
        return pltpu.make_async_copy(obuf.at[bset], out_hbm.at[block_rows(blk)], sems_out.at[bset])

    for c in load_copies(0, 0):
        c.start()

    @pl.loop(0, n_blocks)
    def _(blk):
        bset = blk & 1
        for c in load_copies(blk, bset):
            c.wait()

        @pl.when(blk + 1 < n_blocks)
        def _():
            for c in load_copies(blk + 1, 1 - bset):
                c.start()

        @pl.when(blk >= 2)
        def _():
            store_copy(blk - 2, bset).wait()

        @pl.loop(0, blk_tokens)
        def _(t):
            zero = jnp.zeros((SC_LANES,), F32)
            for j in range(D_MODEL // SC_LANES):
                obuf[bset, t, pl.ds(j * SC_LANES, SC_LANES)] = zero

        for ahead in range(SC_SLOTS - 1):
            for c in gather_copies(bset, ahead, ahead):
                c.start()

        @pl.loop(0, n_items)
        def _(item):
            t, g = split_item(item)
            slot = item & (SC_SLOTS - 1)
            ahead = item + (SC_SLOTS - 1)

            @pl.when(ahead < n_items)
            def _():
                for c in gather_copies(bset, ahead, ahead & (SC_SLOTS - 1)):
                    c.start()

            for c in gather_copies(bset, item, slot):
                c.wait()
            a = dots(bset, t, slot)
            wbuf[...] = pack_weights(gbuf[bset, t, pl.ds(g * SC_GROUP, SC_GROUP)] * _sc_gelu(a))
            accumulate(bset, t, slot)

        store_copy(blk, bset).start()

    for blk in range(max(n_blocks - 2, 0), n_blocks):
        store_copy(blk, blk & 1).wait()


def _peer_experts(h2p, eidx, gates, uv_pack):
    n = h2p.shape[0]
    assert n % (8 * SC_WORKERS) == 0
    blk_tokens = math.gcd(n // SC_WORKERS, SC_TOKENS)
    mesh = plsc.VectorSubcoreMesh(core_axis_name="c", subcore_axis_name="s")
    fn = pl.kernel(
        _peer_sc_kernel,
        out_type=jax.ShapeDtypeStruct((n, D_MODEL), F32),
        mesh=mesh,
        scratch_types=[
            pltpu.VMEM((2, blk_tokens, ROW_WORDS), I32),
            pltpu.VMEM((2, blk_tokens, N_SEL), I32),
            pltpu.VMEM((2, blk_tokens, N_SEL), F32),
            pltpu.VMEM((2, blk_tokens, D_MODEL), F32),
            pltpu.VMEM((SC_SLOTS, SC_GROUP, 2 * ROW_WORDS), I32),
            pltpu.VMEM((SC_GROUP, SC_LANES), F32),
            pltpu.VMEM((SC_LANES,), I32),
            pltpu.SemaphoreType.DMA((SC_SLOTS,)), pltpu.SemaphoreType.DMA((2,)), pltpu.SemaphoreType.DMA((2,)),
        ],
        compiler_params=pltpu.CompilerParams(needs_layout_passes=False),
        cost_estimate=pl.CostEstimate(
            flops=4 * n * N_SEL * D_MODEL, transcendentals=n * N_SEL,
            bytes_accessed=4 * (2 * n * N_SEL * ROW_WORDS + n * ROW_WORDS + n * D_MODEL + 2 * n * N_SEL)),
        name="peer_experts",
    )
    return fn(h2p, eidx, gates, uv_pack)


def _peer_sc_scored_kernel(a_hbm, eidx_hbm, gate_hbm, v_hbm, out_hbm,
                           abuf, ibuf, gbuf, obuf, vbuf, wbuf, sems, sems_a):
    n = a_hbm.shape[0]
    per_worker = n // SC_WORKERS
    blk_tokens = ibuf.shape[0]
    wid = lax.axis_index("c") * SC_SUBCORES + lax.axis_index("s")

    def split_item(item):
        return lax.shift_right_logical(item, SC_NGROUPS.bit_length() - 1), item & (SC_NGROUPS - 1)

    def gather_copy(item, slot):
        t, g = split_item(item)
        idx = ibuf[t, pl.ds(g * SC_GROUP, SC_GROUP)]
        return pltpu.make_async_copy(v_hbm.at[idx], vbuf.at[slot], sems.at[slot])

    def accumulate(t, slot):
        nv = SC_OWORDS // SC_LANES
        for oc in range(ROW_WORDS // SC_OWORDS):
            w0 = oc * SC_OWORDS
            accs = (tuple(obuf[t, pl.ds(w0 + j * SC_LANES, SC_LANES)] for j in range(nv))
                    + tuple(obuf[t, pl.ds(ROW_WORDS + w0 + j * SC_LANES, SC_LANES)] for j in range(nv)))

            def row_quad(rq, accs):
                r = 4 * rq
                ws = [plsc.load_gather(wbuf, [jnp.full((SC_LANES,), r + q, I32)]) for q in range(4)]
                his, los = [], []
                for j in range(nv):
                    m = [_sc_mul_bf16(ws[q], vbuf[slot, r + q, pl.ds(w0 + j * SC_LANES, SC_LANES)])
                         for q in range(4)]
                    hi, lo = _sc_split_sum(m[0] + m[1], m[2] + m[3])
                    his.append(accs[j] + hi)
                    los.append(accs[nv + j] + lo)
                return tuple(his) + tuple(los)

            accs = lax.fori_loop(0, SC_GROUP // 4, row_quad, accs)
            for j in range(nv):
                obuf[t, pl.ds(w0 + j * SC_LANES, SC_LANES)] = accs[j]
                obuf[t, pl.ds(ROW_WORDS + w0 + j * SC_LANES, SC_LANES)] = accs[nv + j]

    def pack_weights(w):
        bits = plsc.bitcast(w, I32)
        rounded = (bits + 0x7FFF + (lax.shift_right_logical(bits, jnp.int32(16)) & 1)) & HI_MASK
        return rounded | lax.shift_right_logical(rounded, jnp.int32(16))

    n_items = blk_tokens * SC_NGROUPS

    @pl.loop(0, per_worker // blk_tokens)
    def _(blk):
        tok0 = pl.multiple_of(wid * per_worker + blk * blk_tokens, blk_tokens)

        def score_copy(t):
            return pltpu.make_async_copy(a_hbm.at[tok0 + t], abuf.at[t & 1], sems_a.at[t & 1])

        score_copy(0).start()
        pltpu.sync_copy(eidx_hbm.at[pl.ds(tok0, blk_tokens)], ibuf)
        pltpu.sync_copy(gate_hbm.at[pl.ds(tok0, blk_tokens)], gbuf)

        @pl.loop(0, blk_tokens)
        def _(t):
            zero = jnp.zeros((SC_LANES,), F32)
            for j in range(D_MODEL // SC_LANES):
                obuf[t, pl.ds(j * SC_LANES, SC_LANES)] = zero

        for ahead in range(SC_SLOTS - 1):
            gather_copy(ahead, ahead).start()

        @pl.loop(0, n_items)
        def _(item):
            t, g = split_item(item)
            slot = item & (SC_SLOTS - 1)
            ahead = item + (SC_SLOTS - 1)

            @pl.when(ahead < n_items)
            def _():
                gather_copy(ahead, ahead & (SC_SLOTS - 1)).start()

            @pl.when(g == 0)
            def _():
                score_copy(t).wait()

                @pl.when(t + 1 < blk_tokens)
                def _():
                    score_copy(t + 1).start()

            gather_copy(item, slot).wait()
            idx = ibuf[t, pl.ds(g * SC_GROUP, SC_GROUP)]
            a = plsc.load_gather(abuf, [jnp.full((SC_LANES,), t & 1, I32), idx])
            wbuf[...] = pack_weights(gbuf[t, pl.ds(g * SC_GROUP, SC_GROUP)] * _sc_gelu(a))
            accumulate(t, slot)

        pltpu.sync_copy(obuf, out_hbm.at[pl.ds(tok0, blk_tokens)])


def _peer_experts_scored(scores, eidx, gates, v_pack):
    n, n_exp = scores.shape
    assert n % (8 * SC_WORKERS) == 0
    blk_tokens = math.gcd(n // SC_WORKERS, 2 * SC_TOKENS)
    mesh = plsc.VectorSubcoreMesh(core_axis_name="c", subcore_axis_name="s")
    fn = pl.kernel(
        _peer_sc_scored_kernel,
        out_type=jax.ShapeDtypeStruct((n, D_MODEL), F32),
        mesh=mesh,
        scratch_types=[
            pltpu.VMEM((2, n_exp), F32),
            pltpu.VMEM((blk_tokens, N_SEL), I32),
            pltpu.VMEM((blk_tokens, N_SEL), F32),
            pltpu.VMEM((blk_tokens, D_MODEL), F32),
            pltpu.VMEM((SC_SLOTS, SC_GROUP, ROW_WORDS), I32),
            pltpu.VMEM((SC_LANES,), I32),
            pltpu.SemaphoreType.DMA((SC_SLOTS,)), pltpu.SemaphoreType.DMA((2,)),
        ],
        compiler_params=pltpu.CompilerParams(needs_layout_passes=False),
        cost_estimate=pl.CostEstimate(
            flops=2 * n * N_SEL * D_MODEL, transcendentals=n * N_SEL,
            bytes_accessed=4 * (n * N_SEL * ROW_WORDS + n * n_exp + n * D_MODEL + 2 * n * N_SEL)),
        name="peer_experts_scored",
    )
    return fn(scores, eidx, gates, v_pack)


def _resid_kernel(x1_ref, p_ref, mod_ref, o_ref):
    o_ref[...] = x1_ref[...] + mod_ref[0, 5:6, :] * p_ref[...]


def _resid_call(x1, peer_out, seq, mod3, mod_row0):
    n = x1.shape[0]
    tm = TOKEN_TILE
    per_seq = seq // tm
    tok = pl.BlockSpec((tm, D_MODEL), lambda i: (i, 0))
    return pl.pallas_call(
        _resid_kernel, grid=(n // tm,),
        in_specs=[tok, tok,
                  pl.BlockSpec((1, 6, D_MODEL), lambda i: (mod_row0 + (i // per_seq if mod_row0 else 0), 0, 0))],
        out_specs=tok, out_shape=jax.ShapeDtypeStruct((n, D_MODEL), F32),
        compiler_params=pltpu.CompilerParams(dimension_semantics=("parallel",)),
        name="resid",
    )(x1, peer_out, mod3)


def _rope_tables(seq, rotate):
    if not rotate:
        return jnp.ones((seq, KV_WIDTH), F32), jnp.zeros((seq, KV_WIDTH), F32)
    quarter = A_DH // 4
    t = jnp.arange(seq)
    row = (t // GRID_W).astype(F32)
    col = (t % GRID_W).astype(F32)
    inv = ROPE_BASE ** (-jnp.arange(quarter, dtype=F32) / quarter)
    d = jnp.arange(A_DH)
    pos = jnp.where(d[None, :] < A_DH // 2, row[:, None], col[:, None])
    ang = pos * inv[d % quarter][None, :]
    sign = jnp.where((d % (A_DH // 2)) < quarter, -1.0, 1.0).astype(F32)
    cos = jnp.cos(ang)
    sin = jnp.sin(ang) * sign[None, :]
    return jnp.tile(cos, (1, KV_WIDTH // A_DH)), jnp.tile(sin, (1, KV_WIDTH // A_DH))


def _run_chunk(x, mod3, mod_row0, prm, cache, rotate, gate_on, dense_scores):
    (n1, n2, w_main, w_g, w_gt, b_g, b_gt, mhg, qg_t, kg_t, bd, sink, w_m, w_a, w_q, sub_a, sub_b,
     (uv_pack, v_pack, u_bf16)) = prm
    batch, seq, _ = x.shape
    n = batch * seq
    x2d = x.reshape(n, D_MODEL)
    if gate_on is not None:
        x2d, _ = lax.optimization_barrier((x2d, gate_on))
    cos, sin = _rope_tables(seq, rotate)
    mq, mk, mv, mo, gcol, grow, aq, ak, av = _inproj_call(
        x2d, seq, mod3, mod_row0, n1, w_main, w_g, w_gt, b_g, b_gt, qg_t, kg_t, bd, cos, sin)
    kc, vc, c0, m0 = cache
    h_f, h_b, c_fin, m_fin = _mlstm_call(mq, mk, mv, gcol, grow, c0, m0, batch, seq)
    if kc is None:
        a_out = _attn_ctx_call(sink, aq, ak, av, batch, seq)
    else:
        a_out = _attn_lat_call(sink, aq, ak, av, kc, vc, batch, seq)
    x1, h2p, eidx, gates = _mix_call(x2d, seq, h_f, h_b, mo, a_out, mod3, mod_row0, mhg, n2, w_m, w_a, w_q,
                                     sub_a, sub_b)
    if dense_scores:
        peer_out = _peer_experts_scored(_scores_call(h2p, u_bf16), eidx, gates, v_pack)
    else:
        peer_out = _peer_experts(h2p, eidx, gates, uv_pack)
    y = _resid_call(x1, peer_out, seq, mod3, mod_row0).reshape(batch, seq, D_MODEL)
    return y, h2p, peer_out, ak, av, c_fin, m_fin


def _pack_state(C, n_vec, m):
    b = C.shape[0]
    caug = jnp.concatenate([C, jnp.broadcast_to(n_vec[..., None], C.shape)], axis=-1)
    caug = caug.reshape(b, 2 * M_HEADS, M_DH, 2 * M_DH)
    m_rep = jnp.broadcast_to(m.reshape(b, 2 * M_HEADS, 1, 1), (b, 2 * M_HEADS, 8, M_DH))
    return caug.astype(F32), m_rep.astype(F32)


def kernel(x_prompt, x_sample, c, cache_attn_k, cache_attn_v, state_mlstm_C, state_mlstm_n, state_mlstm_m,
           c_ctx, w_ada, b_ada, norm1_g, norm2_g, w_in, b_gates, mh_norm_g, q_norm_g, k_norm_g, sink_logits,
           w_out, peer_w_q, peer_sub_a, peer_sub_b, peer_u, peer_v):
    depth = w_ada.shape[0]
    assert depth == 1
    batch, seq, _ = x_prompt.shape
    dec_batch, dec_seq, _ = x_sample.shape
    assert dec_batch + 1 <= MOD_ROWS and batch % sum(CTX_CHUNK_WEIGHTS) == 0 and dec_batch % LATENT_CHUNKS == 0
    l = 0

    cond = jnp.concatenate([c_ctx[None, :], c, jnp.zeros((MOD_ROWS - 1 - dec_batch, D_MODEL), F32)], axis=0)
    mod3 = _ada_call(cond, w_ada[l], b_ada[l]).reshape(MOD_ROWS, 6, D_MODEL)

    wi = w_in[l]
    g0 = 4 * M_WIDTH
    w_main = jnp.concatenate([wi[:, :g0], wi[:, g0 + N_GATES:]], axis=1).astype(BF16)
    w_g = wi[:, g0:g0 + N_GATES]
    seg = jnp.arange(A_WIDTH) // A_DH
    bd = jnp.where(seg[:, None] == seg[None, :], 1.0 / A_DH, 0.0).astype(F32)
    prm = (norm1_g[l][None, :], norm2_g[l][None, :], w_main, w_g, w_g.T, b_gates[l][None, :], b_gates[l][:, None],
           mh_norm_g[l][None, :], jnp.tile(q_norm_g[l], A_HEADS)[None, :], jnp.tile(k_norm_g[l], A_KV)[None, :], bd,
           sink_logits[l], w_out[l][:M_WIDTH].astype(BF16), w_out[l][M_WIDTH:].astype(BF16),
           peer_w_q[l].astype(BF16), peer_sub_a[l].astype(BF16), peer_sub_b[l].astype(BF16),
           _pack_tables_call(peer_u[l], peer_v[l]))

    zeros_c = jnp.zeros((batch, 2, M_HEADS, M_DH, M_DH), F32)
    c0, m0 = _pack_state(zeros_c, zeros_c[..., 0], jnp.full((batch, 2, M_HEADS), NEG, F32))
    c0s, m0s = _pack_state(state_mlstm_C[:, l], state_mlstm_n[:, l], state_mlstm_m[:, l])
    past = cache_attn_k.shape[2]
    kc = cache_attn_k[:, l].reshape(dec_batch, past, KV_WIDTH)
    vc = cache_attn_v[:, l].reshape(dec_batch, past, KV_WIDTH)

    jobs = []
    ctx_sizes = [batch * f // sum(CTX_CHUNK_WEIGHTS) for f in CTX_CHUNK_WEIGHTS]
    assert sum(ctx_sizes) == batch
    b0 = 0
    for size in ctx_sizes:
        b1 = b0 + size
        jobs.append((x_prompt[b0:b1], 0, (None, None, c0[b0:b1], m0[b0:b1]), False, False))
        b0 = b1
    for ci, b0 in enumerate(range(0, dec_batch, dec_batch // LATENT_CHUNKS)):
        b1 = b0 + dec_batch // LATENT_CHUNKS
        jobs.append((x_sample[b0:b1], 1 + b0, (kc[b0:b1], vc[b0:b1], c0s[b0:b1], m0s[b0:b1]), True,
                     ci % DENSE_SCORE_EVERY == DENSE_SCORE_EVERY - 1))
    outs = []
    for i, (x_c, mod_row0, cache, rotate, dense) in enumerate(jobs):
        gate = tuple(g for g in (outs[i - 1][1] if i >= 1 else None,
                                 outs[i - EXPERT_LAG][2] if i >= EXPERT_LAG else None) if g is not None)
        outs.append(_run_chunk(x_c, mod3, mod_row0, prm, cache, rotate, gate or None, dense))
    ctx, lat = outs[:len(ctx_sizes)], outs[len(ctx_sizes):]
    y_p = jnp.concatenate([o[0] for o in ctx], axis=0)
    y_s = jnp.concatenate([o[0] for o in lat], axis=0)
    k_new = jnp.concatenate([o[3] for o in ctx], axis=0)
    v_new = jnp.concatenate([o[4] for o in ctx], axis=0)
    c_fin = jnp.concatenate([o[5] for o in ctx], axis=0)
    m_fin = jnp.concatenate([o[6] for o in ctx], axis=0)

    c_fin = c_fin.reshape(batch, 2, M_HEADS, M_DH, 2 * M_DH)
    new_c = c_fin[..., :M_DH][:, None]
    new_n = c_fin[..., M_DH][:, None]
    new_m = m_fin[:, :, 0, 0].reshape(batch, 2, M_HEADS)[:, None]
    new_k = k_new.reshape(batch, 1, seq, A_KV, A_DH)
    new_v = v_new.reshape(batch, 1, seq, A_KV, A_DH)
    return y_p, y_s, new_k, new_v, new_c, new_n, new_m
```

```python
import functools
import math

import jax
import jax.numpy as jnp
from jax import lax
from jax.experimental import pallas as pl
from jax.experimental.pallas import tpu as pltpu
from jax.experimental.pallas import tpu_sc as plsc

F32 = jnp.float32
BF16 = jnp.bfloat16
I32 = jnp.int32
HI = lax.Precision.HIGHEST

D_MODEL = 1024
EPS = 1e-6
NEG = -1e30
GRID_W = 64
M_HEADS = 4
M_WIDTH = 512
M_DH = 128
A_HEADS = 8
A_KV = 2
A_REP = 4
A_DH = 64
A_WIDTH = 512
KV_WIDTH = A_KV * A_DH
BLOCK = 128
ROPE_BASE = 10000.0
N_KEYS = 128
P_HEADS = 8
P_DKEY = 256
P_HALF = 128
P_TOPK = 16
N_SEL = P_HEADS * P_TOPK
N_GATES = 4 * M_HEADS
MAIN_COLS = 4 * M_WIDTH + A_WIDTH + 2 * KV_WIDTH
MOD_ROWS = 16

TOKEN_TILE = 256
MLSTM_CHUNK = 128
ADA_COL_TILE = 768
CTX_CHUNK_WEIGHTS = (1, 2, 3, 4, 6)
EXPERT_LAG = 3
LATENT_CHUNKS = 8


def _sigmoid(x):
    return 1.0 / (1.0 + jnp.exp(-x))


def _log_sigmoid(x):
    return jnp.minimum(x, 0.0) - jnp.log1p(jnp.exp(-jnp.abs(x)))


def _dot_t(a, b, precision=None):
    return lax.dot_general(a, b, (((1,), (1,)), ((), ())), precision=precision,
                           preferred_element_type=F32)


def _ada_kernel(c_ref, w_ref, b_ref, o_ref):
    c = c_ref[...]
    s = c * _sigmoid(c)
    o_ref[...] = jnp.dot(s, w_ref[...], precision=HI, preferred_element_type=F32) + b_ref[...]


def _ada_call(cond, w_ada, b_ada):
    n_out = w_ada.shape[1]
    return pl.pallas_call(
        _ada_kernel,
        grid=(n_out // ADA_COL_TILE,),
        in_specs=[pl.BlockSpec((MOD_ROWS, D_MODEL), lambda j: (0, 0)),
                  pl.BlockSpec((D_MODEL, ADA_COL_TILE), lambda j: (0, j)),
                  pl.BlockSpec((1, ADA_COL_TILE), lambda j: (0, j))],
        out_specs=pl.BlockSpec((MOD_ROWS, ADA_COL_TILE), lambda j: (0, j)),
        out_shape=jax.ShapeDtypeStruct((MOD_ROWS, n_out), F32),
        name="ada",
    )(cond, w_ada, b_ada.reshape(1, n_out))


def _swap16(x):
    n = x.shape[-1]
    lane = lax.broadcasted_iota(I32, x.shape, x.ndim - 1)
    return jnp.where((lane & 16) == 0, pltpu.roll(x, n - 16, x.ndim - 1), pltpu.roll(x, 16, x.ndim - 1))


def _inproj_kernel(x_ref, mod_ref, n1_ref, w_ref, wg_ref, wgt_ref, bg_ref, bgt_ref, qg_ref, kg_ref,
                   bd_ref, cos_ref, sin_ref,
                   mq_ref, mk_ref, mv_ref, mo_ref, gc_ref, gr_ref, aq_ref, ak_ref, av_ref):
    x = x_ref[...]
    h = x * lax.rsqrt(jnp.mean(x * x, axis=-1, keepdims=True) + EPS) * n1_ref[...]
    h = h * (1.0 + mod_ref[0, 1:2, :]) + mod_ref[0, 0:1, :]
    z = jnp.dot(h.astype(BF16), w_ref[...], preferred_element_type=F32)

    mq_ref[...] = (z[:, 0:M_WIDTH] * (M_DH ** -0.5)).astype(BF16)
    mk_ref[...] = z[:, M_WIDTH:2 * M_WIDTH].astype(BF16)
    mv_ref[...] = z[:, 2 * M_WIDTH:3 * M_WIDTH].astype(BF16)
    mo_ref[...] = z[:, 3 * M_WIDTH:4 * M_WIDTH]

    g = jnp.dot(h, wg_ref[...], precision=HI, preferred_element_type=F32) + bg_ref[...]
    kind = lax.broadcasted_iota(I32, g.shape, 1) // M_HEADS
    gc_ref[...] = jnp.where((kind & 1) == 1, _log_sigmoid(g), g)
    gt = _dot_t(wgt_ref[...], h, precision=HI) + bgt_ref[...]
    kind_t = lax.broadcasted_iota(I32, gt.shape, 0) // M_HEADS
    gr_ref[...] = jnp.where((kind_t & 1) == 1, _log_sigmoid(gt), gt)

    o = 4 * M_WIDTH
    aq = z[:, o:o + A_WIDTH]
    ak = z[:, o + A_WIDTH:o + A_WIDTH + KV_WIDTH]
    av_ref[...] = z[:, o + A_WIDTH + KV_WIDTH:o + A_WIDTH + 2 * KV_WIDTH]
    bd = bd_ref[...]
    cos = cos_ref[...]
    sin = sin_ref[...]
    aq = aq * lax.rsqrt(jnp.dot(aq * aq, bd, precision=HI, preferred_element_type=F32) + EPS) * qg_ref[...]
    cos4 = jnp.concatenate([cos] * (A_WIDTH // KV_WIDTH), axis=1)
    sin4 = jnp.concatenate([sin] * (A_WIDTH // KV_WIDTH), axis=1)
    aq = (aq * cos4 + _swap16(aq) * sin4) * (A_DH ** -0.5)
    ak = ak * lax.rsqrt(jnp.dot(ak * ak, bd[0:KV_WIDTH, 0:KV_WIDTH], precision=HI,
                                preferred_element_type=F32) + EPS) * kg_ref[...]
    ak_ref[...] = ak * cos + _swap16(ak) * sin

    lane = lax.broadcasted_iota(I32, (aq.shape[0], KV_WIDTH), 1)
    for hd in range(A_HEADS):
        grp = hd // A_REP
        blk = aq[:, (hd // 2) * KV_WIDTH:(hd // 2 + 1) * KV_WIDTH]
        if hd % 2 != grp:
            blk = pltpu.roll(blk, A_DH, 1)
        keep = (lane >= grp * A_DH) & (lane < (grp + 1) * A_DH)
        aq_ref[hd] = jnp.where(keep, blk, 0.0).astype(BF16)


def _inproj_call(x2d, seq, mod3, mod_row0, n1, w_main, w_g, w_gt, b_g, b_gt, qg_t, kg_t, bd, cos, sin):
    n = x2d.shape[0]
    tm = TOKEN_TILE
    per_seq = seq // tm

    def tok(i):
        return (i, 0)

    def const2(i):
        return (0, 0)

    in_specs = [
        pl.BlockSpec((tm, D_MODEL), tok),
        pl.BlockSpec((1, 6, D_MODEL), lambda i: (mod_row0 + (i // per_seq if mod_row0 else 0), 0, 0)),
        pl.BlockSpec((1, D_MODEL), const2),
        pl.BlockSpec((D_MODEL, MAIN_COLS), const2),
        pl.BlockSpec((D_MODEL, N_GATES), const2),
        pl.BlockSpec((N_GATES, D_MODEL), const2),
        pl.BlockSpec((1, N_GATES), const2),
        pl.BlockSpec((N_GATES, 1), const2),
        pl.BlockSpec((1, A_WIDTH), const2),
        pl.BlockSpec((1, KV_WIDTH), const2),
        pl.BlockSpec((A_WIDTH, A_WIDTH), const2),
        pl.BlockSpec((tm, KV_WIDTH), lambda i: (i % per_seq, 0)),
        pl.BlockSpec((tm, KV_WIDTH), lambda i: (i % per_seq, 0)),
    ]
    out_specs = [
        pl.BlockSpec((tm, M_WIDTH), tok),
        pl.BlockSpec((tm, M_WIDTH), tok),
        pl.BlockSpec((tm, M_WIDTH), tok),
        pl.BlockSpec((tm, M_WIDTH), tok),
        pl.BlockSpec((tm, N_GATES), tok),
        pl.BlockSpec((N_GATES, tm), lambda i: (0, i)),
        pl.BlockSpec((A_HEADS, tm, KV_WIDTH), lambda i: (0, i, 0)),
        pl.BlockSpec((tm, KV_WIDTH), tok),
        pl.BlockSpec((tm, KV_WIDTH), tok),
    ]
    out_shape = [
        jax.ShapeDtypeStruct((n, M_WIDTH), BF16),
        jax.ShapeDtypeStruct((n, M_WIDTH), BF16),
        jax.ShapeDtypeStruct((n, M_WIDTH), BF16),
        jax.ShapeDtypeStruct((n, M_WIDTH), F32),
        jax.ShapeDtypeStruct((n, N_GATES), F32),
        jax.ShapeDtypeStruct((N_GATES, n), F32),
        jax.ShapeDtypeStruct((A_HEADS, n, KV_WIDTH), BF16),
        jax.ShapeDtypeStruct((n, KV_WIDTH), F32),
        jax.ShapeDtypeStruct((n, KV_WIDTH), F32),
    ]
    return pl.pallas_call(
        _inproj_kernel, grid=(n // tm,), in_specs=in_specs, out_specs=out_specs, out_shape=out_shape,
        compiler_params=pltpu.CompilerParams(dimension_semantics=("parallel",)),
        name="inproj",
    )(x2d, mod3, n1, w_main, w_g, w_gt, b_g, b_gt, qg_t, kg_t, bd, cos, sin)


def _mlstm_chain(q, k, v, li_c, lf_c, li_r, lf_r, caug, m, tri, tri_t, mask, reverse):
    L = q.shape[0]
    last = 0 if reverse else L - 1
    b_c = jnp.dot(tri, jnp.broadcast_to(lf_c, (L, L)), precision=HI, preferred_element_type=F32)
    b_r = jnp.dot(jnp.broadcast_to(lf_r, (8, L)), tri_t, precision=HI, preferred_element_type=F32)[0:1, :]
    a_inter = b_c[:, 0:1] + m
    d = jnp.where(mask, b_c - b_r + li_r, -jnp.inf)
    m_t = jnp.maximum(a_inter, jnp.max(d, axis=1, keepdims=True))
    w_inter = jnp.exp(a_inter - m_t)
    s = _dot_t(q, k) * jnp.exp(d - m_t)
    qc = jnp.dot(q, caug.astype(BF16), preferred_element_type=F32)
    num = jnp.dot(s.astype(BF16), v, preferred_element_type=F32) + w_inter * qc[:, 0:M_DH]
    den = jnp.sum(s, axis=1, keepdims=True) + w_inter * qc[:, M_DH:M_DH + 1]
    den = jnp.maximum(jnp.abs(den), jnp.exp(-m_t))
    h = num / den
    m_new = m_t[last:last + 1, :]
    b_last = b_c[last:last + 1, 0:1]
    g_c = jnp.exp(b_last - b_c[:, 0:1] + li_c - m_new)
    decay = jnp.exp(b_last + m - m_new)
    kw = (k.astype(F32) * g_c).astype(BF16)
    vaug = jnp.concatenate([v, jnp.ones_like(v)], axis=1)
    upd = lax.dot_general(kw, vaug, (((0,), (0,)), ((), ())), preferred_element_type=F32)
    return h, decay * caug + upd, m_new


def _mlstm_kernel(qf_ref, kf_ref, vf_ref, gcf_ref, grf_ref, qb_ref, kb_ref, vb_ref, gcb_ref, grb_ref,
                  c0_ref, m0_ref, hf_ref, hb_ref, cfin_ref, mfin_ref, c_scr, m_scr):
    c = pl.program_id(1)
    nc = pl.num_programs(1)
    L = qf_ref.shape[0]

    @pl.when(c == 0)
    def _():
        c_scr[...] = c0_ref[0]
        m_scr[...] = m0_ref[0]

    row = lax.broadcasted_iota(I32, (L, L), 0)
    col = lax.broadcasted_iota(I32, (L, L), 1)
    lower = row >= col
    upper = row <= col
    lower_f = lower.astype(F32)
    upper_f = upper.astype(F32)

    for direction in range(2):
        reverse = direction == 1
        q_ref, k_ref, v_ref, gc_ref, gr_ref, h_ref = (
            (qb_ref, kb_ref, vb_ref, gcb_ref, grb_ref, hb_ref) if reverse
            else (qf_ref, kf_ref, vf_ref, gcf_ref, grf_ref, hf_ref))
        tri, tri_t, mask = (upper_f, lower_f, upper) if reverse else (lower_f, upper_f, lower)
        gc = gc_ref[...]
        gr = gr_ref[...]
        for hd in range(M_HEADS):
            ch = direction * M_HEADS + hd
            sl = slice(hd * M_DH, (hd + 1) * M_DH)
            ci = 2 * direction * M_HEADS + hd
            cf = ci + M_HEADS
            h, caug, m_new = _mlstm_chain(
                q_ref[:, sl], k_ref[:, sl], v_ref[:, sl],
                gc[:, ci:ci + 1], gc[:, cf:cf + 1], gr[ci:ci + 1, :], gr[cf:cf + 1, :],
                c_scr[ch], m_scr[ch][0:1, 0:1], tri, tri_t, mask, reverse)
            h_ref[:, sl] = h
            c_scr[ch] = caug
            m_scr[ch] = jnp.broadcast_to(m_new, m_scr.shape[1:])

    @pl.when(c == nc - 1)
    def _():
        cfin_ref[0] = c_scr[...]
        mfin_ref[0] = m_scr[...]


def _mlstm_call(mq, mk, mv, gcol, grow, c0, m0, batch, seq):
    n = mq.shape[0]
    L = MLSTM_CHUNK
    nc = seq // L
    n_ch = 2 * M_HEADS

    def fwd(b, c):
        return (b * nc + c, 0)

    def bwd(b, c):
        return (b * nc + nc - 1 - c, 0)

    def fwd_t(b, c):
        return (0, b * nc + c)

    def bwd_t(b, c):
        return (0, b * nc + nc - 1 - c)

    tok = pl.BlockSpec((L, M_WIDTH), fwd)
    tok_b = pl.BlockSpec((L, M_WIDTH), bwd)
    in_specs = [tok, tok, tok, pl.BlockSpec((L, N_GATES), fwd), pl.BlockSpec((N_GATES, L), fwd_t),
                tok_b, tok_b, tok_b, pl.BlockSpec((L, N_GATES), bwd), pl.BlockSpec((N_GATES, L), bwd_t),
                pl.BlockSpec((1, n_ch, M_DH, 2 * M_DH), lambda b, c: (b, 0, 0, 0)),
                pl.BlockSpec((1, n_ch, 8, M_DH), lambda b, c: (b, 0, 0, 0))]
    out_specs = [tok, tok_b,
                 pl.BlockSpec((1, n_ch, M_DH, 2 * M_DH), lambda b, c: (b, 0, 0, 0)),
                 pl.BlockSpec((1, n_ch, 8, M_DH), lambda b, c: (b, 0, 0, 0))]
    out_shape = [jax.ShapeDtypeStruct((n, M_WIDTH), F32), jax.ShapeDtypeStruct((n, M_WIDTH), F32),
                 jax.ShapeDtypeStruct((batch, n_ch, M_DH, 2 * M_DH), F32),
                 jax.ShapeDtypeStruct((batch, n_ch, 8, M_DH), F32)]
    return pl.pallas_call(
        _mlstm_kernel, grid=(batch, nc), in_specs=in_specs, out_specs=out_specs, out_shape=out_shape,
        scratch_shapes=[pltpu.VMEM((n_ch, M_DH, 2 * M_DH), F32), pltpu.VMEM((n_ch, 8, M_DH), F32)],
        compiler_params=pltpu.CompilerParams(dimension_semantics=("parallel", "arbitrary")),
        name="mlstm",
    )(mq, mk, mv, gcol, grow, mq, mk, mv, gcol, grow, c0, m0)


def _sink_column(sink_ref, grp, rows_per_head):
    return jnp.concatenate(
        [jnp.full((rows_per_head, 1), sink_ref[grp * A_REP + r], F32) for r in range(A_REP)], axis=0)


def _store_heads(out_ref, o, grp, rows_per_head):
    for r in range(A_REP):
        hd = grp * A_REP + r
        out_ref[:, hd * A_DH:(hd + 1) * A_DH] = o[r * rows_per_head:(r + 1) * rows_per_head,
                                                  grp * A_DH:(grp + 1) * A_DH].astype(out_ref.dtype)


def _attn_ctx_kernel(sink_ref, q_ref, k_ref, v_ref, out_ref):
    s_len = k_ref.shape[0]
    k = k_ref[...].astype(BF16)
    v = v_ref[...].astype(BF16)
    for grp in range(A_KV):
        q = q_ref[grp * A_REP:(grp + 1) * A_REP].reshape(A_REP * s_len, KV_WIDTH)
        s = _dot_t(q, k)
        sk = _sink_column(sink_ref, grp, s_len)
        mx = jnp.maximum(jnp.max(s, axis=1, keepdims=True), sk)
        p = jnp.exp(s - mx)
        den = jnp.sum(p, axis=1, keepdims=True) + jnp.exp(sk - mx)
        o = jnp.dot(p.astype(BF16), v, preferred_element_type=F32) / den
        _store_heads(out_ref, o, grp, s_len)


def _attn_ctx_call(sink, aq, ak, av, batch, seq):
    n = ak.shape[0]
    return pl.pallas_call(
        _attn_ctx_kernel, grid=(batch,),
        in_specs=[pl.BlockSpec(memory_space=pltpu.SMEM),
                  pl.BlockSpec((A_HEADS, seq, KV_WIDTH), lambda b: (0, b, 0)),
                  pl.BlockSpec((seq, KV_WIDTH), lambda b: (b, 0)),
                  pl.BlockSpec((seq, KV_WIDTH), lambda b: (b, 0))],
        out_specs=pl.BlockSpec((seq, A_WIDTH), lambda b: (b, 0)),
        out_shape=jax.ShapeDtypeStruct((n, A_WIDTH), BF16),
        compiler_params=pltpu.CompilerParams(dimension_semantics=("parallel",)),
        name="attn_ctx",
    )(sink, aq, ak, av)


def _attn_lat_kernel(sink_ref, q_ref, kc_ref, vc_ref, kp_ref, kq_ref, kn_ref, vp_ref, vq_ref, vn_ref, out_ref):
    i = pl.program_id(1)
    nb = pl.num_programs(1)
    kc = kc_ref[0].astype(BF16)
    vc = vc_ref[0].astype(BF16)
    kp, kq, kn = kp_ref[...].astype(BF16), kq_ref[...].astype(BF16), kn_ref[...].astype(BF16)
    vp, vq, vn = vp_ref[...].astype(BF16), vq_ref[...].astype(BF16), vn_ref[...].astype(BF16)
    rows = A_REP * BLOCK
    qpos = lax.broadcasted_iota(I32, (rows, BLOCK), 0) % BLOCK
    kpos = lax.broadcasted_iota(I32, (rows, BLOCK), 1)
    mask_p = (kpos >= qpos) & (i > 0)
    mask_n = (kpos <= qpos) & (i < nb - 1)
    for grp in range(A_KV):
        q = q_ref[grp * A_REP:(grp + 1) * A_REP].reshape(rows, KV_WIDTH)
        s_c = _dot_t(q, kc)
        s_p = jnp.where(mask_p, _dot_t(q, kp), NEG)
        s_q = _dot_t(q, kq)
        s_n = jnp.where(mask_n, _dot_t(q, kn), NEG)
        sk = _sink_column(sink_ref, grp, BLOCK)
        mx = jnp.maximum(jnp.maximum(jnp.max(s_c, axis=1, keepdims=True), jnp.max(s_p, axis=1, keepdims=True)),
                         jnp.maximum(jnp.max(s_q, axis=1, keepdims=True), jnp.max(s_n, axis=1, keepdims=True)))
        mx = jnp.maximum(mx, sk)
        p_c, p_p, p_q, p_n = jnp.exp(s_c - mx), jnp.exp(s_p - mx), jnp.exp(s_q - mx), jnp.exp(s_n - mx)
        den = (jnp.sum(p_c, axis=1, keepdims=True) + jnp.sum(p_p, axis=1, keepdims=True)
               + jnp.sum(p_q, axis=1, keepdims=True) + jnp.sum(p_n, axis=1, keepdims=True) + jnp.exp(sk - mx))
        o = (jnp.dot(p_c.astype(BF16), vc, preferred_element_type=F32)
             + jnp.dot(p_p.astype(BF16), vp, preferred_element_type=F32)
             + jnp.dot(p_q.astype(BF16), vq, preferred_element_type=F32)
             + jnp.dot(p_n.astype(BF16), vn, preferred_element_type=F32)) / den
        _store_heads(out_ref, o, grp, BLOCK)


def _attn_lat_call(sink, aq, ak, av, kc, vc, batch, seq):
    n = ak.shape[0]
    nb = seq // BLOCK
    past = kc.shape[1]

    def cur(b, i):
        return (b * nb + i, 0)

    def prev(b, i):
        return (b * nb + jnp.maximum(i - 1, 0), 0)

    def nxt(b, i):
        return (b * nb + jnp.minimum(i + 1, nb - 1), 0)

    blk = functools.partial(pl.BlockSpec, (BLOCK, KV_WIDTH))
    cache = pl.BlockSpec((1, past, KV_WIDTH), lambda b, i: (b, 0, 0))
    return pl.pallas_call(
        _attn_lat_kernel, grid=(batch, nb),
        in_specs=[pl.BlockSpec(memory_space=pltpu.SMEM),
                  pl.BlockSpec((A_HEADS, BLOCK, KV_WIDTH), lambda b, i: (0, b * nb + i, 0)),
                  cache, cache, blk(prev), blk(cur), blk(nxt), blk(prev), blk(cur), blk(nxt)],
        out_specs=pl.BlockSpec((BLOCK, A_WIDTH), cur),
        out_shape=jax.ShapeDtypeStruct((n, A_WIDTH), BF16),
        compiler_params=pltpu.CompilerParams(dimension_semantics=("parallel", "parallel")),
        name="attn_lat",
    )(sink, aq, kc, vc, ak, ak, ak, av, av, av)


def _top16_rows(s, payload=None):
    n_rows = s.shape[0]
    rows = lax.broadcasted_iota(I32, s.shape, 0).astype(F32)
    vals, idxs, pays = [], [], []
    for _ in range(P_TOPK):
        mx = jnp.max(s, axis=0, keepdims=True)
        ix = jnp.min(jnp.where(s == mx, rows, float(n_rows)), axis=0, keepdims=True)
        hit = rows == ix
        vals.append(mx)
        idxs.append(ix)
        if payload is not None:
            pays.append(jnp.sum(jnp.where(hit, payload, 0.0), axis=0, keepdims=True))
        s = jnp.where(hit, -jnp.inf, s)
    out = (jnp.concatenate(vals, axis=0), jnp.concatenate(idxs, axis=0))
    if payload is not None:
        out += (jnp.concatenate(pays, axis=0),)
    return out


def _mix_kernel(x_ref, hf_ref, hb_ref, mo_ref, ao_ref, mod_ref, mhg_ref, n2_ref, wm_ref, wa_ref, wq_ref,
                sa_ref, sb_ref, x1_ref, h2_ref, eidx_ref, gate_ref, qp_scr, e_scr, g_scr):
    tm = x_ref.shape[0]
    hs = hf_ref[...] + hb_ref[...]
    parts = []
    for hd in range(M_HEADS):
        blk = hs[:, hd * M_DH:(hd + 1) * M_DH]
        parts.append(blk * lax.rsqrt(jnp.mean(blk * blk, axis=-1, keepdims=True) + EPS))
    m_out = _sigmoid(mo_ref[...]) * (jnp.concatenate(parts, axis=1) * mhg_ref[...])
    mix = (jnp.dot(m_out.astype(BF16), wm_ref[...], preferred_element_type=F32)
           + jnp.dot(ao_ref[...], wa_ref[...], preferred_element_type=F32))
    x1 = x_ref[...] + mod_ref[0, 2:3, :] * mix
    x1_ref[...] = x1
    h2 = x1 * lax.rsqrt(jnp.mean(x1 * x1, axis=-1, keepdims=True) + EPS) * n2_ref[...]
    h2 = h2 * (1.0 + mod_ref[0, 4:5, :]) + mod_ref[0, 3:4, :]
    h2_ref[...] = _pack_bf16_pairs(h2)
    qp = jnp.dot(h2.astype(BF16), wq_ref[...], preferred_element_type=F32)
    for p in range(P_HEADS):
        qp_scr[p] = qp[:, p * P_DKEY:(p + 1) * P_DKEY].astype(BF16)
    sub_a = sa_ref[...]
    sub_b = sb_ref[...]

    def head_body(p, carry):
        for half in range(tm // N_KEYS):
            cols = slice(half * N_KEYS, (half + 1) * N_KEYS)
            qh = qp_scr[p, pl.ds(half * N_KEYS, N_KEYS), :]
            s_a = _dot_t(sub_a, qh[:, 0:P_HALF])
            s_b = _dot_t(sub_b, qh[:, P_HALF:P_DKEY])
            va, ia = _top16_rows(s_a)
            vb, ib = _top16_rows(s_b)
            keep = [P_TOPK // (i + 1) for i in range(P_TOPK)]
            pad = -sum(keep) % 8
            cand = jnp.concatenate([va[i:i + 1, :] + vb[0:keep[i], :] for i in range(P_TOPK)]
                                   + [jnp.full((pad, N_KEYS), -jnp.inf, F32)], axis=0)
            cidx = jnp.concatenate([ia[i:i + 1, :] * float(N_KEYS) + ib[0:keep[i], :] for i in range(P_TOPK)]
                                   + [jnp.zeros((pad, N_KEYS), F32)], axis=0)
            top, _, eidx = _top16_rows(cand, cidx)
            ex = jnp.exp(top - jnp.max(top, axis=0, keepdims=True))
            gates = ex / jnp.sum(ex, axis=0, keepdims=True)
            r0 = pl.multiple_of(p * P_TOPK, P_TOPK)
            e_scr[pl.ds(r0, P_TOPK), cols] = eidx
            g_scr[pl.ds(r0, P_TOPK), cols] = gates
        return carry

    lax.fori_loop(0, P_HEADS, head_body, 0)
    for half in range(tm // N_KEYS):
        cols = slice(half * N_KEYS, (half + 1) * N_KEYS)
        eidx_ref[cols, :] = e_scr[:, cols].T.astype(I32)
        gate_ref[cols, :] = g_scr[:, cols].T


def _mix_call(x2d, seq, h_f, h_b, mo, a_out, mod3, mod_row0, mhg, n2, w_m, w_a, w_q, sub_a, sub_b):
    n = x2d.shape[0]
    tm = TOKEN_TILE
    per_seq = seq // tm

    def tok(i):
        return (i, 0)

    def const2(i):
        return (0, 0)

    in_specs = [
        pl.BlockSpec((tm, D_MODEL), tok),
        pl.BlockSpec((tm, M_WIDTH), tok), pl.BlockSpec((tm, M_WIDTH), tok), pl.BlockSpec((tm, M_WIDTH), tok),
        pl.BlockSpec((tm, A_WIDTH), tok),
        pl.BlockSpec((1, 6, D_MODEL), lambda i: (mod_row0 + (i // per_seq if mod_row0 else 0), 0, 0)),
        pl.BlockSpec((1, M_WIDTH), const2),
        pl.BlockSpec((1, D_MODEL), const2),
        pl.BlockSpec((M_WIDTH, D_MODEL), const2),
        pl.BlockSpec((A_WIDTH, D_MODEL), const2),
        pl.BlockSpec((D_MODEL, P_HEADS * P_DKEY), const2),
        pl.BlockSpec((N_KEYS, P_HALF), const2),
        pl.BlockSpec((N_KEYS, P_HALF), const2),
    ]
    out_specs = [pl.BlockSpec((tm, D_MODEL), tok), pl.BlockSpec((tm, D_MODEL // 2), tok),
                 pl.BlockSpec((tm, N_SEL), tok), pl.BlockSpec((tm, N_SEL), tok)]
    out_shape = [jax.ShapeDtypeStruct((n, D_MODEL), F32), jax.ShapeDtypeStruct((n, D_MODEL // 2), I32),
                 jax.ShapeDtypeStruct((n, N_SEL), I32), jax.ShapeDtypeStruct((n, N_SEL), F32)]
    return pl.pallas_call(
        _mix_kernel, grid=(n // tm,), in_specs=in_specs, out_specs=out_specs, out_shape=out_shape,
        scratch_shapes=[pltpu.VMEM((P_HEADS, tm, P_DKEY), BF16), pltpu.VMEM((N_SEL, tm), F32),
                        pltpu.VMEM((N_SEL, tm), F32)],
        compiler_params=pltpu.CompilerParams(dimension_semantics=("parallel",)),
        name="mix",
    )(x2d, h_f, h_b, mo, a_out, mod3, mhg, n2, w_m, w_a, w_q, sub_a, sub_b)


SC_LANES = 16
SC_CORES = 2
SC_SUBCORES = 16
SC_WORKERS = SC_CORES * SC_SUBCORES
SC_TOKENS = 16
SC_GROUP = SC_LANES
SC_NGROUPS = N_SEL // SC_GROUP
SC_SLOTS = 4
ROW_WORDS = D_MODEL // 2
SC_DOT_ROWS = 8
SC_DOT_PARTIALS = 2
SC_OWORDS = 16 * SC_LANES
HI_MASK = -65536
PACK_ROWS = 512
SCORE_TOKENS = 1024
SCORE_EXPERTS = 2048
DENSE_SCORE_EVERY = 2
GELU_C0 = 0.7978845608028654
GELU_C1 = 0.044715


def _pack_bf16_pairs(x):
    half = x.shape[1] // 2
    bits = lax.bitcast_convert_type(x.astype(BF16).astype(F32), I32)
    return (bits[:, :half] & HI_MASK) | lax.shift_right_logical(bits[:, half:], jnp.int32(16))


def _pack_tables_kernel(u_ref, v_ref, uv_ref, vp_ref, ub_ref):
    v_words = _pack_bf16_pairs(v_ref[...])
    uv_ref[:, 0:ROW_WORDS] = _pack_bf16_pairs(u_ref[...])
    uv_ref[:, ROW_WORDS:2 * ROW_WORDS] = v_words
    vp_ref[...] = v_words
    ub_ref[...] = u_ref[...].astype(BF16)


def _pack_tables_call(u_tab, v_tab):
    n_exp = u_tab.shape[0]
    blk = pl.BlockSpec((PACK_ROWS, D_MODEL), lambda i: (i, 0))
    half = pl.BlockSpec((PACK_ROWS, ROW_WORDS), lambda i: (i, 0))
    return pl.pallas_call(
        _pack_tables_kernel, grid=(n_exp // PACK_ROWS,), in_specs=[blk, blk], out_specs=[blk, half, blk],
        out_shape=[jax.ShapeDtypeStruct((n_exp, 2 * ROW_WORDS), I32), jax.ShapeDtypeStruct((n_exp, ROW_WORDS), I32),
                   jax.ShapeDtypeStruct((n_exp, D_MODEL), BF16)],
        compiler_params=pltpu.CompilerParams(dimension_semantics=("parallel",)),
        name="pack_tables",
    )(u_tab, v_tab)


def _unpack_bf16_pairs(words):
    hi = lax.bitcast_convert_type(words & HI_MASK, F32)
    lo = lax.bitcast_convert_type(lax.shift_left(words, jnp.int32(16)), F32)
    return jnp.concatenate([hi, lo], axis=1)


def _scores_kernel(x_ref, u_ref, o_ref):
    o_ref[...] = _pack_bf16_pairs(_dot_t(_unpack_bf16_pairs(x_ref[...]).astype(BF16), u_ref[...]))


def _scores_call(h2p, u_bf16):
    n = h2p.shape[0]
    n_exp = u_bf16.shape[0]
    tm = math.gcd(n, SCORE_TOKENS)
    return pl.pallas_call(
        _scores_kernel, grid=(n // tm, n_exp // SCORE_EXPERTS),
        in_specs=[pl.BlockSpec((tm, ROW_WORDS), lambda i, j: (i, 0)),
                  pl.BlockSpec((SCORE_EXPERTS, D_MODEL), lambda i, j: (j, 0))],
        out_specs=pl.BlockSpec((tm, SCORE_EXPERTS // 2), lambda i, j: (i, j)),
        out_shape=jax.ShapeDtypeStruct((n, n_exp // 2), I32),
        compiler_params=pltpu.CompilerParams(dimension_semantics=("parallel", "parallel")),
        name="expert_scores",
    )(h2p, u_bf16)


def _sc_gelu(a):
    z = GELU_C0 * (a + GELU_C1 * (a * a * a))
    tanh = 1.0 - 2.0 / (jnp.exp(2.0 * z) + 1.0)
    return 0.5 * a * (1.0 + tanh)


def _sc_split(words):
    return (plsc.bitcast(words & HI_MASK, F32), plsc.bitcast(lax.shift_left(words, jnp.int32(16)), F32))


def _sc_mul_bf16(a_words, b_words):
    return plsc.bitcast(a_words, BF16) * plsc.bitcast(b_words, BF16)


def _sc_split_sum(p, q):
    return _sc_split(plsc.bitcast(p + q, I32))


def _peer_sc_kernel(h2_hbm, eidx_hbm, gate_hbm, uv_hbm, out_hbm,
                    xbuf, ibuf, gbuf, obuf, uvbuf, mbuf, wbuf, sems, sems_in, sems_out):
    n = h2_hbm.shape[0]
    per_worker = n // SC_WORKERS
    blk_tokens = xbuf.shape[1]
    n_blocks = per_worker // blk_tokens
    wid = lax.axis_index("c") * SC_SUBCORES + lax.axis_index("s")
    lane = lax.iota(I32, SC_LANES)

    def split_item(item):
        return lax.shift_right_logical(item, SC_NGROUPS.bit_length() - 1), item & (SC_NGROUPS - 1)

    def gather_copies(bset, item, slot):
        t, g = split_item(item)
        idx = ibuf[bset, t, pl.ds(g * SC_GROUP, SC_GROUP)]
        return (pltpu.make_async_copy(uv_hbm.at[idx], uvbuf.at[slot], sems.at[slot]),)

    def dots(bset, t, slot):
        zero = jnp.zeros((SC_LANES,), F32)

        @pl.loop(0, SC_GROUP, step=SC_DOT_ROWS)
        def _(r0):
            accs = [[zero] * SC_DOT_PARTIALS for _ in range(SC_DOT_ROWS)]
            for k in range(0, ROW_WORDS // SC_LANES, 4):
                xs = [xbuf[bset, t, pl.ds((k + q) * SC_LANES, SC_LANES)] for q in range(4)]
                for i in range(SC_DOT_ROWS):
                    m = [_sc_mul_bf16(xs[q], uvbuf[slot, r0 + i, pl.ds((k + q) * SC_LANES, SC_LANES)])
                         for q in range(4)]
                    hi, lo = _sc_split_sum(m[0] + m[1], m[2] + m[3])
                    p = (k // 4) % SC_DOT_PARTIALS
                    accs[i][p] = accs[i][p] + (hi + lo)
            for i in range(SC_DOT_ROWS):
                mbuf[r0 + i, :] = functools.reduce(lambda a, b: a + b, accs[i])

        tot = zero
        for c in range(SC_LANES):
            tot = tot + plsc.load_gather(mbuf, [lane, jnp.full((SC_LANES,), c, I32)])
        return tot

    def accumulate(bset, t, slot):
        nv = SC_OWORDS // SC_LANES
        for oc in range(ROW_WORDS // SC_OWORDS):
            w0 = oc * SC_OWORDS
            accs = (tuple(obuf[bset, t, pl.ds(w0 + j * SC_LANES, SC_LANES)] for j in range(nv))
                    + tuple(obuf[bset, t, pl.ds(ROW_WORDS + w0 + j * SC_LANES, SC_LANES)] for j in range(nv)))

            def row_quad(rq, accs):
                r = 4 * rq
                ws = [plsc.load_gather(wbuf, [jnp.full((SC_LANES,), r + q, I32)]) for q in range(4)]
                his, los = [], []
                for j in range(nv):
                    m = [_sc_mul_bf16(ws[q], uvbuf[slot, r + q, pl.ds(ROW_WORDS + w0 + j * SC_LANES, SC_LANES)])
                         for q in range(4)]
                    hi, lo = _sc_split_sum(m[0] + m[1], m[2] + m[3])
                    his.append(accs[j] + hi)
                    los.append(accs[nv + j] + lo)
                return tuple(his) + tuple(los)

            accs = lax.fori_loop(0, SC_GROUP // 4, row_quad, accs)
            for j in range(nv):
                obuf[bset, t, pl.ds(w0 + j * SC_LANES, SC_LANES)] = accs[j]
                obuf[bset, t, pl.ds(ROW_WORDS + w0 + j * SC_LANES, SC_LANES)] = accs[nv + j]

    def pack_weights(w):
        bits = plsc.bitcast(w, I32)
        rounded = (bits + 0x7FFF + (lax.shift_right_logical(bits, jnp.int32(16)) & 1)) & HI_MASK
        return rounded | lax.shift_right_logical(rounded, jnp.int32(16))

    n_items = blk_tokens * SC_NGROUPS

    def block_rows(blk):
        return pl.ds(pl.multiple_of(wid * per_worker + blk * blk_tokens, blk_tokens), blk_tokens)

    def load_copies(blk, bset):
        rows = block_rows(blk)
        return (pltpu.make_async_copy(h2_hbm.at[rows], xbuf.at[bset], sems_in.at[bset]),
                pltpu.make_async_copy(eidx_hbm.at[rows], ibuf.at[bset], sems_in.at[bset]),
                pltpu.make_async_copy(gate_hbm.at[rows], gbuf.at[bset], sems_in.at[bset]))

    def store_copy(blk, bset):
        return pltpu.make_async_copy(obuf.at[bset], out_hbm.at[block_rows(blk)], sems_out.at[bset])

    for c in load_copies(0, 0):
        c.start()

    @pl.loop(0, n_blocks)
    def _(blk):
        bset = blk & 1
        for c in load_copies(blk, bset):
            c.wait()

        @pl.when(blk + 1 < n_blocks)
        def _():
            for c in load_copies(blk + 1, 1 - bset):
                c.start()

        @pl.when(blk >= 2)
        def _():
            store_copy(blk - 2, bset).wait()

        @pl.loop(0, blk_tokens)
        def _(t):
            zero = jnp.zeros((SC_LANES,), F32)
            for j in range(D_MODEL // SC_LANES):
                obuf[bset, t, pl.ds(j * SC_LANES, SC_LANES)] = zero

        for ahead in range(SC_SLOTS - 1):
            for c in gather_copies(bset, ahead, ahead):
                c.start()

        @pl.loop(0, n_items)
        def _(item):
            t, g = split_item(item)
            slot = item & (SC_SLOTS - 1)
            ahead = item + (SC_SLOTS - 1)

            @pl.when(ahead < n_items)
            def _():
                for c in gather_copies(bset, ahead, ahead & (SC_SLOTS - 1)):
                    c.start()

            for c in gather_copies(bset, item, slot):
                c.wait()
            a = dots(bset, t, slot)
            wbuf[...] = pack_weights(gbuf[bset, t, pl.ds(g * SC_GROUP, SC_GROUP)] * _sc_gelu(a))
            accumulate(bset, t, slot)

        store_copy(blk, bset).start()

    for blk in range(max(n_blocks - 2, 0), n_blocks):
        store_copy(blk, blk & 1).wait()


def _peer_experts(h2p, eidx, gates, uv_pack):
    n = h2p.shape[0]
    assert n % (8 * SC_WORKERS) == 0
    blk_tokens = math.gcd(n // SC_WORKERS, SC_TOKENS)
    mesh = plsc.VectorSubcoreMesh(core_axis_name="c", subcore_axis_name="s")
    fn = pl.kernel(
        _peer_sc_kernel,
        out_type=jax.ShapeDtypeStruct((n, D_MODEL), F32),
        mesh=mesh,
        scratch_types=[
            pltpu.VMEM((2, blk_tokens, ROW_WORDS), I32),
            pltpu.VMEM((2, blk_tokens, N_SEL), I32),
            pltpu.VMEM((2, blk_tokens, N_SEL), F32),
            pltpu.VMEM((2, blk_tokens, D_MODEL), F32),
            pltpu.VMEM((SC_SLOTS, SC_GROUP, 2 * ROW_WORDS), I32),
            pltpu.VMEM((SC_GROUP, SC_LANES), F32),
            pltpu.VMEM((SC_LANES,), I32),
            pltpu.SemaphoreType.DMA((SC_SLOTS,)), pltpu.SemaphoreType.DMA((2,)), pltpu.SemaphoreType.DMA((2,)),
        ],
        compiler_params=pltpu.CompilerParams(needs_layout_passes=False),
        cost_estimate=pl.CostEstimate(
            flops=4 * n * N_SEL * D_MODEL, transcendentals=n * N_SEL,
            bytes_accessed=4 * (2 * n * N_SEL * ROW_WORDS + n * ROW_WORDS + n * D_MODEL + 2 * n * N_SEL)),
        name="peer_experts",
    )
    return fn(h2p, eidx, gates, uv_pack)


def _peer_sc_scored_kernel(a_hbm, eidx_hbm, gate_hbm, v_hbm, out_hbm,
                           abuf, ibuf, gbuf, obuf, vbuf, wbuf, sems, sems_a):
    n = a_hbm.shape[0]
    per_worker = n // SC_WORKERS
    blk_tokens = ibuf.shape[0]
    wid = lax.axis_index("c") * SC_SUBCORES + lax.axis_index("s")

    def split_item(item):
        return lax.shift_right_logical(item, SC_NGROUPS.bit_length() - 1), item & (SC_NGROUPS - 1)

    def gather_copy(item, slot):
        t, g = split_item(item)
        idx = ibuf[t, pl.ds(g * SC_GROUP, SC_GROUP)]
        return pltpu.make_async_copy(v_hbm.at[idx], vbuf.at[slot], sems.at[slot])

    def accumulate(t, slot):
        nv = SC_OWORDS // SC_LANES
        for oc in range(ROW_WORDS // SC_OWORDS):
            w0 = oc * SC_OWORDS
            accs = (tuple(obuf[t, pl.ds(w0 + j * SC_LANES, SC_LANES)] for j in range(nv))
                    + tuple(obuf[t, pl.ds(ROW_WORDS + w0 + j * SC_LANES, SC_LANES)] for j in range(nv)))

            def row_quad(rq, accs):
                r = 4 * rq
                ws = [plsc.load_gather(wbuf, [jnp.full((SC_LANES,), r + q, I32)]) for q in range(4)]
                his, los = [], []
                for j in range(nv):
                    m = [_sc_mul_bf16(ws[q], vbuf[slot, r + q, pl.ds(w0 + j * SC_LANES, SC_LANES)])
                         for q in range(4)]
                    hi, lo = _sc_split_sum(m[0] + m[1], m[2] + m[3])
                    his.append(accs[j] + hi)
                    los.append(accs[nv + j] + lo)
                return tuple(his) + tuple(los)

            accs = lax.fori_loop(0, SC_GROUP // 4, row_quad, accs)
            for j in range(nv):
                obuf[t, pl.ds(w0 + j * SC_LANES, SC_LANES)] = accs[j]
                obuf[t, pl.ds(ROW_WORDS + w0 + j * SC_LANES, SC_LANES)] = accs[nv + j]

    def pack_weights(w):
        bits = plsc.bitcast(w, I32)
        rounded = (bits + 0x7FFF + (lax.shift_right_logical(bits, jnp.int32(16)) & 1)) & HI_MASK
        return rounded | lax.shift_right_logical(rounded, jnp.int32(16))

    n_items = blk_tokens * SC_NGROUPS

    @pl.loop(0, per_worker // blk_tokens)
    def _(blk):
        tok0 = pl.multiple_of(wid * per_worker + blk * blk_tokens, blk_tokens)

        def score_copy(t):
            return pltpu.make_async_copy(a_hbm.at[tok0 + t], abuf.at[t & 1], sems_a.at[t & 1])

        score_copy(0).start()
        pltpu.sync_copy(eidx_hbm.at[pl.ds(tok0, blk_tokens)], ibuf)
        pltpu.sync_copy(gate_hbm.at[pl.ds(tok0, blk_tokens)], gbuf)

        @pl.loop(0, blk_tokens)
        def _(t):
            zero = jnp.zeros((SC_LANES,), F32)
            for j in range(D_MODEL // SC_LANES):
                obuf[t, pl.ds(j * SC_LANES, SC_LANES)] = zero

        for ahead in range(SC_SLOTS - 1):
            gather_copy(ahead, ahead).start()

        @pl.loop(0, n_items)
        def _(item):
            t, g = split_item(item)
            slot = item & (SC_SLOTS - 1)
            ahead = item + (SC_SLOTS - 1)

            @pl.when(ahead < n_items)
            def _():
                gather_copy(ahead, ahead & (SC_SLOTS - 1)).start()

            @pl.when(g == 0)
            def _():
                score_copy(t).wait()

                @pl.when(t + 1 < blk_tokens)
                def _():
                    score_copy(t + 1).start()

            gather_copy(item, slot).wait()
            idx = ibuf[t, pl.ds(g * SC_GROUP, SC_GROUP)]
            half = SCORE_EXPERTS // 2
            word = plsc.load_gather(
                abuf, [jnp.full((SC_LANES,), t & 1, I32),
                       lax.shift_right_logical(idx, jnp.int32(SCORE_EXPERTS.bit_length() - 1)) * half
                       + (idx & (half - 1))])
            hi, lo = _sc_split(word)
            a = jnp.where((idx & half) == 0, hi, lo)
            wbuf[...] = pack_weights(gbuf[t, pl.ds(g * SC_GROUP, SC_GROUP)] * _sc_gelu(a))
            accumulate(t, slot)

        pltpu.sync_copy(obuf, out_hbm.at[pl.ds(tok0, blk_tokens)])


def _peer_experts_scored(scores, eidx, gates, v_pack):
    n, n_exp = scores.shape
    assert n % (8 * SC_WORKERS) == 0
    blk_tokens = math.gcd(n // SC_WORKERS, 2 * SC_TOKENS)
    mesh = plsc.VectorSubcoreMesh(core_axis_name="c", subcore_axis_name="s")
    fn = pl.kernel(
        _peer_sc_scored_kernel,
        out_type=jax.ShapeDtypeStruct((n, D_MODEL), F32),
        mesh=mesh,
        scratch_types=[
            pltpu.VMEM((2, n_exp), I32),
            pltpu.VMEM((blk_tokens, N_SEL), I32),
            pltpu.VMEM((blk_tokens, N_SEL), F32),
            pltpu.VMEM((blk_tokens, D_MODEL), F32),
            pltpu.VMEM((SC_SLOTS, SC_GROUP, ROW_WORDS), I32),
            pltpu.VMEM((SC_LANES,), I32),
            pltpu.SemaphoreType.DMA((SC_SLOTS,)), pltpu.SemaphoreType.DMA((2,)),
        ],
        compiler_params=pltpu.CompilerParams(needs_layout_passes=False),
        cost_estimate=pl.CostEstimate(
            flops=2 * n * N_SEL * D_MODEL, transcendentals=n * N_SEL,
            bytes_accessed=4 * (n * N_SEL * ROW_WORDS + n * n_exp + n * D_MODEL + 2 * n * N_SEL)),
        name="peer_experts_scored",
    )
    return fn(scores, eidx, gates, v_pack)


def _resid_kernel(x1_ref, p_ref, mod_ref, o_ref):
    o_ref[...] = x1_ref[...] + mod_ref[0, 5:6, :] * p_ref[...]


def _resid_call(x1, peer_out, seq, mod3, mod_row0):
    n = x1.shape[0]
    tm = TOKEN_TILE
    per_seq = seq // tm
    tok = pl.BlockSpec((tm, D_MODEL), lambda i: (i, 0))
    return pl.pallas_call(
        _resid_kernel, grid=(n // tm,),
        in_specs=[tok, tok,
                  pl.BlockSpec((1, 6, D_MODEL), lambda i: (mod_row0 + (i // per_seq if mod_row0 else 0), 0, 0))],
        out_specs=tok, out_shape=jax.ShapeDtypeStruct((n, D_MODEL), F32),
        compiler_params=pltpu.CompilerParams(dimension_semantics=("parallel",)),
        name="resid",
    )(x1, peer_out, mod3)


def _rope_tables(seq, rotate):
    if not rotate:
        return jnp.ones((seq, KV_WIDTH), F32), jnp.zeros((seq, KV_WIDTH), F32)
    quarter = A_DH // 4
    t = jnp.arange(seq)
    row = (t // GRID_W).astype(F32)
    col = (t % GRID_W).astype(F32)
    inv = ROPE_BASE ** (-jnp.arange(quarter, dtype=F32) / quarter)
    d = jnp.arange(A_DH)
    pos = jnp.where(d[None, :] < A_DH // 2, row[:, None], col[:, None])
    ang = pos * inv[d % quarter][None, :]
    sign = jnp.where((d % (A_DH // 2)) < quarter, -1.0, 1.0).astype(F32)
    cos = jnp.cos(ang)
    sin = jnp.sin(ang) * sign[None, :]
    return jnp.tile(cos, (1, KV_WIDTH // A_DH)), jnp.tile(sin, (1, KV_WIDTH // A_DH))


def _run_chunk(x, mod3, mod_row0, prm, cache, rotate, gate_on, dense_scores):
    (n1, n2, w_main, w_g, w_gt, b_g, b_gt, mhg, qg_t, kg_t, bd, sink, w_m, w_a, w_q, sub_a, sub_b,
     (uv_pack, v_pack, u_bf16)) = prm
    batch, seq, _ = x.shape
    n = batch * seq
    x2d = x.reshape(n, D_MODEL)
    if gate_on is not None:
        x2d, _ = lax.optimization_barrier((x2d, gate_on))
    cos, sin = _rope_tables(seq, rotate)
    mq, mk, mv, mo, gcol, grow, aq, ak, av = _inproj_call(
        x2d, seq, mod3, mod_row0, n1, w_main, w_g, w_gt, b_g, b_gt, qg_t, kg_t, bd, cos, sin)
    kc, vc, c0, m0 = cache
    h_f, h_b, c_fin, m_fin = _mlstm_call(mq, mk, mv, gcol, grow, c0, m0, batch, seq)
    if kc is None:
        a_out = _attn_ctx_call(sink, aq, ak, av, batch, seq)
    else:
        a_out = _attn_lat_call(sink, aq, ak, av, kc, vc, batch, seq)
    x1, h2p, eidx, gates = _mix_call(x2d, seq, h_f, h_b, mo, a_out, mod3, mod_row0, mhg, n2, w_m, w_a, w_q,
                                     sub_a, sub_b)
    if dense_scores:
        peer_out = _peer_experts_scored(_scores_call(h2p, u_bf16), eidx, gates, v_pack)
    else:
        peer_out = _peer_experts(h2p, eidx, gates, uv_pack)
    y = _resid_call(x1, peer_out, seq, mod3, mod_row0).reshape(batch, seq, D_MODEL)
    return y, h2p, peer_out, ak, av, c_fin, m_fin


def _pack_state(C, n_vec, m):
    b = C.shape[0]
    caug = jnp.concatenate([C, jnp.broadcast_to(n_vec[..., None], C.shape)], axis=-1)
    caug = caug.reshape(b, 2 * M_HEADS, M_DH, 2 * M_DH)
    m_rep = jnp.broadcast_to(m.reshape(b, 2 * M_HEADS, 1, 1), (b, 2 * M_HEADS, 8, M_DH))
    return caug.astype(F32), m_rep.astype(F32)


def kernel(x_prompt, x_sample, c, cache_attn_k, cache_attn_v, state_mlstm_C, state_mlstm_n, state_mlstm_m,
           c_ctx, w_ada, b_ada, norm1_g, norm2_g, w_in, b_gates, mh_norm_g, q_norm_g, k_norm_g, sink_logits,
           w_out, peer_w_q, peer_sub_a, peer_sub_b, peer_u, peer_v):
    depth = w_ada.shape[0]
    assert depth == 1
    batch, seq, _ = x_prompt.shape
    dec_batch, dec_seq, _ = x_sample.shape
    assert dec_batch + 1 <= MOD_ROWS and batch % sum(CTX_CHUNK_WEIGHTS) == 0 and dec_batch % LATENT_CHUNKS == 0
    l = 0

    cond = jnp.concatenate([c_ctx[None, :], c, jnp.zeros((MOD_ROWS - 1 - dec_batch, D_MODEL), F32)], axis=0)
    mod3 = _ada_call(cond, w_ada[l], b_ada[l]).reshape(MOD_ROWS, 6, D_MODEL)

    wi = w_in[l]
    g0 = 4 * M_WIDTH
    w_main = jnp.concatenate([wi[:, :g0], wi[:, g0 + N_GATES:]], axis=1).astype(BF16)
    w_g = wi[:, g0:g0 + N_GATES]
    seg = jnp.arange(A_WIDTH) // A_DH
    bd = jnp.where(seg[:, None] == seg[None, :], 1.0 / A_DH, 0.0).astype(F32)
    prm = (norm1_g[l][None, :], norm2_g[l][None, :], w_main, w_g, w_g.T, b_gates[l][None, :], b_gates[l][:, None],
           mh_norm_g[l][None, :], jnp.tile(q_norm_g[l], A_HEADS)[None, :], jnp.tile(k_norm_g[l], A_KV)[None, :], bd,
           sink_logits[l], w_out[l][:M_WIDTH].astype(BF16), w_out[l][M_WIDTH:].astype(BF16),
           peer_w_q[l].astype(BF16), peer_sub_a[l].astype(BF16), peer_sub_b[l].astype(BF16),
           _pack_tables_call(peer_u[l], peer_v[l]))

    zeros_c = jnp.zeros((batch, 2, M_HEADS, M_DH, M_DH), F32)
    c0, m0 = _pack_state(zeros_c, zeros_c[..., 0], jnp.full((batch, 2, M_HEADS), NEG, F32))
    c0s, m0s = _pack_state(state_mlstm_C[:, l], state_mlstm_n[:, l], state_mlstm_m[:, l])
    past = cache_attn_k.shape[2]
    kc = cache_attn_k[:, l].reshape(dec_batch, past, KV_WIDTH)
    vc = cache_attn_v[:, l].reshape(dec_batch, past, KV_WIDTH)

    jobs = []
    ctx_sizes = [batch * f // sum(CTX_CHUNK_WEIGHTS) for f in CTX_CHUNK_WEIGHTS]
    assert sum(ctx_sizes) == batch
    b0 = 0
    for size in ctx_sizes:
        b1 = b0 + size
        jobs.append((x_prompt[b0:b1], 0, (None, None, c0[b0:b1], m0[b0:b1]), False, False))
        b0 = b1
    for ci, b0 in enumerate(range(0, dec_batch, dec_batch // LATENT_CHUNKS)):
        b1 = b0 + dec_batch // LATENT_CHUNKS
        jobs.append((x_sample[b0:b1], 1 + b0, (kc[b0:b1], vc[b0:b1], c0s[b0:b1], m0s[b0:b1]), True,
                     ci % DENSE_SCORE_EVERY == DENSE_SCORE_EVERY - 1))
    outs = []
    for i, (x_c, mod_row0, cache, rotate, dense) in enumerate(jobs):
        gate = tuple(g for g in (outs[i - 1][1] if i >= 1 else None,
                                 outs[i - EXPERT_LAG][2] if i >= EXPERT_LAG else None) if g is not None)
        outs.append(_run_chunk(x_c, mod3, mod_row0, prm, cache, rotate, gate or None, dense))
    ctx, lat = outs[:len(ctx_sizes)], outs[len(ctx_sizes):]
    y_p = jnp.concatenate([o[0] for o in ctx], axis=0)
    y_s = jnp.concatenate([o[0] for o in lat], axis=0)
    k_new = jnp.concatenate([o[3] for o in ctx], axis=0)
    v_new = jnp.concatenate([o[4] for o in ctx], axis=0)
    c_fin = jnp.concatenate([o[5] for o in ctx], axis=0)
    m_fin = jnp.concatenate([o[6] for o in ctx], axis=0)

    c_fin = c_fin.reshape(batch, 2, M_HEADS, M_DH, 2 * M_DH)
    new_c = c_fin[..., :M_DH][:, None]
    new_n = c_fin[..., M_DH][:, None]
    new_m = m_fin[:, :, 0, 0].reshape(batch, 2, M_HEADS)[:, None]
    new_k = k_new.reshape(batch, 1, seq, A_KV, A_DH)
    new_v = v_new.reshape(batch, 1, seq, A_KV, A_DH)
    return y_p, y_s, new_k, new_v, new_c, new_n, new_m
```

```python
import functools
import math

import jax
import jax.numpy as jnp
from jax import lax
from jax.experimental import pallas as pl
from jax.experimental.pallas import tpu as pltpu
from jax.experimental.pallas import tpu_sc as plsc

F32 = jnp.float32
BF16 = jnp.bfloat16
I32 = jnp.int32
HI = lax.Precision.HIGHEST

D_MODEL = 1024
EPS = 1e-6
NEG = -1e30
GRID_W = 64
M_HEADS = 4
M_WIDTH = 512
M_DH = 128
A_HEADS = 8
A_KV = 2
A_REP = 4
A_DH = 64
A_WIDTH = 512
KV_WIDTH = A_KV * A_DH
BLOCK = 128
ROPE_BASE = 10000.0
N_KEYS = 128
P_HEADS = 8
P_DKEY = 256
P_HALF = 128
P_TOPK = 16
N_SEL = P_HEADS * P_TOPK
N_GATES = 4 * M_HEADS
MAIN_COLS = 4 * M_WIDTH + A_WIDTH + 2 * KV_WIDTH
MOD_ROWS = 16

TOKEN_TILE = 256
MLSTM_CHUNK = 128
ADA_COL_TILE = 768
CTX_CHUNK_WEIGHTS = (1, 2, 3, 4, 6)
EXPERT_LAG = 3
LATENT_CHUNKS = 8


def _sigmoid(x):
    return 1.0 / (1.0 + jnp.exp(-x))


def _log_sigmoid(x):
    return jnp.minimum(x, 0.0) - jnp.log1p(jnp.exp(-jnp.abs(x)))


def _dot_t(a, b, precision=None):
    return lax.dot_general(a, b, (((1,), (1,)), ((), ())), precision=precision,
                           preferred_element_type=F32)


def _ada_kernel(c_ref, w_ref, b_ref, o_ref):
    c = c_ref[...]
    s = c * _sigmoid(c)
    o_ref[...] = jnp.dot(s, w_ref[...], precision=HI, preferred_element_type=F32) + b_ref[...]


def _ada_call(cond, w_ada, b_ada):
    n_out = w_ada.shape[1]
    return pl.pallas_call(
        _ada_kernel,
        grid=(n_out // ADA_COL_TILE,),
        in_specs=[pl.BlockSpec((MOD_ROWS, D_MODEL), lambda j: (0, 0)),
                  pl.BlockSpec((D_MODEL, ADA_COL_TILE), lambda j: (0, j)),
                  pl.BlockSpec((1, ADA_COL_TILE), lambda j: (0, j))],
        out_specs=pl.BlockSpec((MOD_ROWS, ADA_COL_TILE), lambda j: (0, j)),
        out_shape=jax.ShapeDtypeStruct((MOD_ROWS, n_out), F32),
        name="ada",
    )(cond, w_ada, b_ada.reshape(1, n_out))


def _swap16(x):
    n = x.shape[-1]
    lane = lax.broadcasted_iota(I32, x.shape, x.ndim - 1)
    return jnp.where((lane & 16) == 0, pltpu.roll(x, n - 16, x.ndim - 1), pltpu.roll(x, 16, x.ndim - 1))


def _inproj_kernel(x_ref, mod_ref, n1_ref, w_ref, wg_ref, wgt_ref, bg_ref, bgt_ref, qg_ref, kg_ref,
                   bd_ref, cos_ref, sin_ref,
                   mq_ref, mk_ref, mv_ref, mo_ref, gc_ref, gr_ref, aq_ref, ak_ref, av_ref):
    x = x_ref[...]
    h = x * lax.rsqrt(jnp.mean(x * x, axis=-1, keepdims=True) + EPS) * n1_ref[...]
    h = h * (1.0 + mod_ref[0, 1:2, :]) + mod_ref[0, 0:1, :]
    z = jnp.dot(h.astype(BF16), w_ref[...], preferred_element_type=F32)

    mq_ref[...] = (z[:, 0:M_WIDTH] * (M_DH ** -0.5)).astype(BF16)
    mk_ref[...] = z[:, M_WIDTH:2 * M_WIDTH].astype(BF16)
    mv_ref[...] = z[:, 2 * M_WIDTH:3 * M_WIDTH].astype(BF16)
    mo_ref[...] = z[:, 3 * M_WIDTH:4 * M_WIDTH]

    g = jnp.dot(h, wg_ref[...], precision=HI, preferred_element_type=F32) + bg_ref[...]
    kind = lax.broadcasted_iota(I32, g.shape, 1) // M_HEADS
    gc_ref[...] = jnp.where((kind & 1) == 1, _log_sigmoid(g), g)
    gt = _dot_t(wgt_ref[...], h, precision=HI) + bgt_ref[...]
    kind_t = lax.broadcasted_iota(I32, gt.shape, 0) // M_HEADS
    gr_ref[...] = jnp.where((kind_t & 1) == 1, _log_sigmoid(gt), gt)

    o = 4 * M_WIDTH
    aq = z[:, o:o + A_WIDTH]
    ak = z[:, o + A_WIDTH:o + A_WIDTH + KV_WIDTH]
    av_ref[...] = z[:, o + A_WIDTH + KV_WIDTH:o + A_WIDTH + 2 * KV_WIDTH]
    bd = bd_ref[...]
    cos = cos_ref[...]
    sin = sin_ref[...]
    aq = aq * lax.rsqrt(jnp.dot(aq * aq, bd, precision=HI, preferred_element_type=F32) + EPS) * qg_ref[...]
    cos4 = jnp.concatenate([cos] * (A_WIDTH // KV_WIDTH), axis=1)
    sin4 = jnp.concatenate([sin] * (A_WIDTH // KV_WIDTH), axis=1)
    aq = (aq * cos4 + _swap16(aq) * sin4) * (A_DH ** -0.5)
    ak = ak * lax.rsqrt(jnp.dot(ak * ak, bd[0:KV_WIDTH, 0:KV_WIDTH], precision=HI,
                                preferred_element_type=F32) + EPS) * kg_ref[...]
    ak_ref[...] = ak * cos + _swap16(ak) * sin

    lane = lax.broadcasted_iota(I32, (aq.shape[0], KV_WIDTH), 1)
    for hd in range(A_HEADS):
        grp = hd // A_REP
        blk = aq[:, (hd // 2) * KV_WIDTH:(hd // 2 + 1) * KV_WIDTH]
        if hd % 2 != grp:
            blk = pltpu.roll(blk, A_DH, 1)
        keep = (lane >= grp * A_DH) & (lane < (grp + 1) * A_DH)
        aq_ref[hd] = jnp.where(keep, blk, 0.0).astype(BF16)


def _inproj_call(x2d, seq, mod3, mod_row0, n1, w_main, w_g, w_gt, b_g, b_gt, qg_t, kg_t, bd, cos, sin):
    n = x2d.shape[0]
    tm = TOKEN_TILE
    per_seq = seq // tm

    def tok(i):
        return (i, 0)

    def const2(i):
        return (0, 0)

    in_specs = [
        pl.BlockSpec((tm, D_MODEL), tok),
        pl.BlockSpec((1, 6, D_MODEL), lambda i: (mod_row0 + (i // per_seq if mod_row0 else 0), 0, 0)),
        pl.BlockSpec((1, D_MODEL), const2),
        pl.BlockSpec((D_MODEL, MAIN_COLS), const2),
        pl.BlockSpec((D_MODEL, N_GATES), const2),
        pl.BlockSpec((N_GATES, D_MODEL), const2),
        pl.BlockSpec((1, N_GATES), const2),
        pl.BlockSpec((N_GATES, 1), const2),
        pl.BlockSpec((1, A_WIDTH), const2),
        pl.BlockSpec((1, KV_WIDTH), const2),
        pl.BlockSpec((A_WIDTH, A_WIDTH), const2),
        pl.BlockSpec((tm, KV_WIDTH), lambda i: (i % per_seq, 0)),
        pl.BlockSpec((tm, KV_WIDTH), lambda i: (i % per_seq, 0)),
    ]
    out_specs = [
        pl.BlockSpec((tm, M_WIDTH), tok),
        pl.BlockSpec((tm, M_WIDTH), tok),
        pl.BlockSpec((tm, M_WIDTH), tok),
        pl.BlockSpec((tm, M_WIDTH), tok),
        pl.BlockSpec((tm, N_GATES), tok),
        pl.BlockSpec((N_GATES, tm), lambda i: (0, i)),
        pl.BlockSpec((A_HEADS, tm, KV_WIDTH), lambda i: (0, i, 0)),
        pl.BlockSpec((tm, KV_WIDTH), tok),
        pl.BlockSpec((tm, KV_WIDTH), tok),
    ]
    out_shape = [
        jax.ShapeDtypeStruct((n, M_WIDTH), BF16),
        jax.ShapeDtypeStruct((n, M_WIDTH), BF16),
        jax.ShapeDtypeStruct((n, M_WIDTH), BF16),
        jax.ShapeDtypeStruct((n, M_WIDTH), F32),
        jax.ShapeDtypeStruct((n, N_GATES), F32),
        jax.ShapeDtypeStruct((N_GATES, n), F32),
        jax.ShapeDtypeStruct((A_HEADS, n, KV_WIDTH), BF16),
        jax.ShapeDtypeStruct((n, KV_WIDTH), F32),
        jax.ShapeDtypeStruct((n, KV_WIDTH), F32),
    ]
    return pl.pallas_call(
        _inproj_kernel, grid=(n // tm,), in_specs=in_specs, out_specs=out_specs, out_shape=out_shape,
        compiler_params=pltpu.CompilerParams(dimension_semantics=("parallel",)),
        name="inproj",
    )(x2d, mod3, n1, w_main, w_g, w_gt, b_g, b_gt, qg_t, kg_t, bd, cos, sin)


def _mlstm_chain(q, k, v, li_c, lf_c, li_r, lf_r, caug, m, tri, tri_t, mask, reverse):
    L = q.shape[0]
    last = 0 if reverse else L - 1
    b_c = jnp.dot(tri, jnp.broadcast_to(lf_c, (L, L)), precision=HI, preferred_element_type=F32)
    b_r = jnp.dot(jnp.broadcast_to(lf_r, (8, L)), tri_t, precision=HI, preferred_element_type=F32)[0:1, :]
    a_inter = b_c[:, 0:1] + m
    d = jnp.where(mask, b_c - b_r + li_r, -jnp.inf)
    m_t = jnp.maximum(a_inter, jnp.max(d, axis=1, keepdims=True))
    w_inter = jnp.exp(a_inter - m_t)
    s = _dot_t(q, k) * jnp.exp(d - m_t)
    qc = jnp.dot(q, caug.astype(BF16), preferred_element_type=F32)
    num = jnp.dot(s.astype(BF16), v, preferred_element_type=F32) + w_inter * qc[:, 0:M_DH]
    den = jnp.sum(s, axis=1, keepdims=True) + w_inter * qc[:, M_DH:M_DH + 1]
    den = jnp.maximum(jnp.abs(den), jnp.exp(-m_t))
    h = num / den
    m_new = m_t[last:last + 1, :]
    b_last = b_c[last:last + 1, 0:1]
    g_c = jnp.exp(b_last - b_c[:, 0:1] + li_c - m_new)
    decay = jnp.exp(b_last + m - m_new)
    kw = (k.astype(F32) * g_c).astype(BF16)
    vaug = jnp.concatenate([v, jnp.ones_like(v)], axis=1)
    upd = lax.dot_general(kw, vaug, (((0,), (0,)), ((), ())), preferred_element_type=F32)
    return h, decay * caug + upd, m_new


def _mlstm_kernel(qf_ref, kf_ref, vf_ref, gcf_ref, grf_ref, qb_ref, kb_ref, vb_ref, gcb_ref, grb_ref,
                  c0_ref, m0_ref, hf_ref, hb_ref, cfin_ref, mfin_ref, c_scr, m_scr):
    c = pl.program_id(1)
    nc = pl.num_programs(1)
    L = qf_ref.shape[0]

    @pl.when(c == 0)
    def _():
        c_scr[...] = c0_ref[0]
        m_scr[...] = m0_ref[0]

    row = lax.broadcasted_iota(I32, (L, L), 0)
    col = lax.broadcasted_iota(I32, (L, L), 1)
    lower = row >= col
    upper = row <= col
    lower_f = lower.astype(F32)
    upper_f = upper.astype(F32)

    for direction in range(2):
        reverse = direction == 1
        q_ref, k_ref, v_ref, gc_ref, gr_ref, h_ref = (
            (qb_ref, kb_ref, vb_ref, gcb_ref, grb_ref, hb_ref) if reverse
            else (qf_ref, kf_ref, vf_ref, gcf_ref, grf_ref, hf_ref))
        tri, tri_t, mask = (upper_f, lower_f, upper) if reverse else (lower_f, upper_f, lower)
        gc = gc_ref[...]
        gr = gr_ref[...]
        for hd in range(M_HEADS):
            ch = direction * M_HEADS + hd
            sl = slice(hd * M_DH, (hd + 1) * M_DH)
            ci = 2 * direction * M_HEADS + hd
            cf = ci + M_HEADS
            h, caug, m_new = _mlstm_chain(
                q_ref[:, sl], k_ref[:, sl], v_ref[:, sl],
                gc[:, ci:ci + 1], gc[:, cf:cf + 1], gr[ci:ci + 1, :], gr[cf:cf + 1, :],
                c_scr[ch], m_scr[ch][0:1, 0:1], tri, tri_t, mask, reverse)
            h_ref[:, sl] = h
            c_scr[ch] = caug
            m_scr[ch] = jnp.broadcast_to(m_new, m_scr.shape[1:])

    @pl.when(c == nc - 1)
    def _():
        cfin_ref[0] = c_scr[...]
        mfin_ref[0] = m_scr[...]


def _mlstm_call(mq, mk, mv, gcol, grow, c0, m0, batch, seq):
    n = mq.shape[0]
    L = MLSTM_CHUNK
    nc = seq // L
    n_ch = 2 * M_HEADS

    def fwd(b, c):
        return (b * nc + c, 0)

    def bwd(b, c):
        return (b * nc + nc - 1 - c, 0)

    def fwd_t(b, c):
        return (0, b * nc + c)

    def bwd_t(b, c):
        return (0, b * nc + nc - 1 - c)

    tok = pl.BlockSpec((L, M_WIDTH), fwd)
    tok_b = pl.BlockSpec((L, M_WIDTH), bwd)
    in_specs = [tok, tok, tok, pl.BlockSpec((L, N_GATES), fwd), pl.BlockSpec((N_GATES, L), fwd_t),
                tok_b, tok_b, tok_b, pl.BlockSpec((L, N_GATES), bwd), pl.BlockSpec((N_GATES, L), bwd_t),
                pl.BlockSpec((1, n_ch, M_DH, 2 * M_DH), lambda b, c: (b, 0, 0, 0)),
                pl.BlockSpec((1, n_ch, 8, M_DH), lambda b, c: (b, 0, 0, 0))]
    out_specs = [tok, tok_b,
                 pl.BlockSpec((1, n_ch, M_DH, 2 * M_DH), lambda b, c: (b, 0, 0, 0)),
                 pl.BlockSpec((1, n_ch, 8, M_DH), lambda b, c: (b, 0, 0, 0))]
    out_shape = [jax.ShapeDtypeStruct((n, M_WIDTH), F32), jax.ShapeDtypeStruct((n, M_WIDTH), F32),
                 jax.ShapeDtypeStruct((batch, n_ch, M_DH, 2 * M_DH), F32),
                 jax.ShapeDtypeStruct((batch, n_ch, 8, M_DH), F32)]
    return pl.pallas_call(
        _mlstm_kernel, grid=(batch, nc), in_specs=in_specs, out_specs=out_specs, out_shape=out_shape,
        scratch_shapes=[pltpu.VMEM((n_ch, M_DH, 2 * M_DH), F32), pltpu.VMEM((n_ch, 8, M_DH), F32)],
        compiler_params=pltpu.CompilerParams(dimension_semantics=("parallel", "arbitrary")),
        name="mlstm",
    )(mq, mk, mv, gcol, grow, mq, mk, mv, gcol, grow, c0, m0)


def _sink_column(sink_ref, grp, rows_per_head):
    return jnp.concatenate(
        [jnp.full((rows_per_head, 1), sink_ref[grp * A_REP + r], F32) for r in range(A_REP)], axis=0)


def _store_heads(out_ref, o, grp, rows_per_head):
    for r in range(A_REP):
        hd = grp * A_REP + r
        out_ref[:, hd * A_DH:(hd + 1) * A_DH] = o[r * rows_per_head:(r + 1) * rows_per_head,
                                                  grp * A_DH:(grp + 1) * A_DH].astype(out_ref.dtype)


def _attn_ctx_kernel(sink_ref, q_ref, k_ref, v_ref, out_ref):
    s_len = k_ref.shape[0]
    k = k_ref[...].astype(BF16)
    v = v_ref[...].astype(BF16)
    for grp in range(A_KV):
        q = q_ref[grp * A_REP:(grp + 1) * A_REP].reshape(A_REP * s_len, KV_WIDTH)
        s = _dot_t(q, k)
        sk = _sink_column(sink_ref, grp, s_len)
        mx = jnp.maximum(jnp.max(s, axis=1, keepdims=True), sk)
        p = jnp.exp(s - mx)
        den = jnp.sum(p, axis=1, keepdims=True) + jnp.exp(sk - mx)
        o = jnp.dot(p.astype(BF16), v, preferred_element_type=F32) / den
        _store_heads(out_ref, o, grp, s_len)


def _attn_ctx_call(sink, aq, ak, av, batch, seq):
    n = ak.shape[0]
    return pl.pallas_call(
        _attn_ctx_kernel, grid=(batch,),
        in_specs=[pl.BlockSpec(memory_space=pltpu.SMEM),
                  pl.BlockSpec((A_HEADS, seq, KV_WIDTH), lambda b: (0, b, 0)),
                  pl.BlockSpec((seq, KV_WIDTH), lambda b: (b, 0)),
                  pl.BlockSpec((seq, KV_WIDTH), lambda b: (b, 0))],
        out_specs=pl.BlockSpec((seq, A_WIDTH), lambda b: (b, 0)),
        out_shape=jax.ShapeDtypeStruct((n, A_WIDTH), BF16),
        compiler_params=pltpu.CompilerParams(dimension_semantics=("parallel",)),
        name="attn_ctx",
    )(sink, aq, ak, av)


def _attn_lat_kernel(sink_ref, q_ref, kc_ref, vc_ref, kp_ref, kq_ref, kn_ref, vp_ref, vq_ref, vn_ref, out_ref):
    i = pl.program_id(1)
    nb = pl.num_programs(1)
    kc = kc_ref[0].astype(BF16)
    vc = vc_ref[0].astype(BF16)
    kp, kq, kn = kp_ref[...].astype(BF16), kq_ref[...].astype(BF16), kn_ref[...].astype(BF16)
    vp, vq, vn = vp_ref[...].astype(BF16), vq_ref[...].astype(BF16), vn_ref[...].astype(BF16)
    rows = A_REP * BLOCK
    qpos = lax.broadcasted_iota(I32, (rows, BLOCK), 0) % BLOCK
    kpos = lax.broadcasted_iota(I32, (rows, BLOCK), 1)
    mask_p = (kpos >= qpos) & (i > 0)
    mask_n = (kpos <= qpos) & (i < nb - 1)
    for grp in range(A_KV):
        q = q_ref[grp * A_REP:(grp + 1) * A_REP].reshape(rows, KV_WIDTH)
        s_c = _dot_t(q, kc)
        s_p = jnp.where(mask_p, _dot_t(q, kp), NEG)
        s_q = _dot_t(q, kq)
        s_n = jnp.where(mask_n, _dot_t(q, kn), NEG)
        sk = _sink_column(sink_ref, grp, BLOCK)
        mx = jnp.maximum(jnp.maximum(jnp.max(s_c, axis=1, keepdims=True), jnp.max(s_p, axis=1, keepdims=True)),
                         jnp.maximum(jnp.max(s_q, axis=1, keepdims=True), jnp.max(s_n, axis=1, keepdims=True)))
        mx = jnp.maximum(mx, sk)
        p_c, p_p, p_q, p_n = jnp.exp(s_c - mx), jnp.exp(s_p - mx), jnp.exp(s_q - mx), jnp.exp(s_n - mx)
        den = (jnp.sum(p_c, axis=1, keepdims=True) + jnp.sum(p_p, axis=1, keepdims=True)
               + jnp.sum(p_q, axis=1, keepdims=True) + jnp.sum(p_n, axis=1, keepdims=True) + jnp.exp(sk - mx))
        o = (jnp.dot(p_c.astype(BF16), vc, preferred_element_type=F32)
             + jnp.dot(p_p.astype(BF16), vp, preferred_element_type=F32)
             + jnp.dot(p_q.astype(BF16), vq, preferred_element_type=F32)
             + jnp.dot(p_n.astype(BF16), vn, preferred_element_type=F32)) / den
        _store_heads(out_ref, o, grp, BLOCK)


def _attn_lat_call(sink, aq, ak, av, kc, vc, batch, seq):
    n = ak.shape[0]
    nb = seq // BLOCK
    past = kc.shape[1]

    def cur(b, i):
        return (b * nb + i, 0)

    def prev(b, i):
        return (b * nb + jnp.maximum(i - 1, 0), 0)

    def nxt(b, i):
        return (b * nb + jnp.minimum(i + 1, nb - 1), 0)

    blk = functools.partial(pl.BlockSpec, (BLOCK, KV_WIDTH))
    cache = pl.BlockSpec((1, past, KV_WIDTH), lambda b, i: (b, 0, 0))
    return pl.pallas_call(
        _attn_lat_kernel, grid=(batch, nb),
        in_specs=[pl.BlockSpec(memory_space=pltpu.SMEM),
                  pl.BlockSpec((A_HEADS, BLOCK, KV_WIDTH), lambda b, i: (0, b * nb + i, 0)),
                  cache, cache, blk(prev), blk(cur), blk(nxt), blk(prev), blk(cur), blk(nxt)],
        out_specs=pl.BlockSpec((BLOCK, A_WIDTH), cur),
        out_shape=jax.ShapeDtypeStruct((n, A_WIDTH), BF16),
        compiler_params=pltpu.CompilerParams(dimension_semantics=("parallel", "parallel")),
        name="attn_lat",
    )(sink, aq, kc, vc, ak, ak, ak, av, av, av)


def _top16_rows(s, payload=None):
    n_rows = s.shape[0]
    rows = lax.broadcasted_iota(I32, s.shape, 0).astype(F32)
    vals, idxs, pays = [], [], []
    for _ in range(P_TOPK):
        mx = jnp.max(s, axis=0, keepdims=True)
        ix = jnp.min(jnp.where(s == mx, rows, float(n_rows)), axis=0, keepdims=True)
        hit = rows == ix
        vals.append(mx)
        idxs.append(ix)
        if payload is not None:
            pays.append(jnp.sum(jnp.where(hit, payload, 0.0), axis=0, keepdims=True))
        s = jnp.where(hit, -jnp.inf, s)
    out = (jnp.concatenate(vals, axis=0), jnp.concatenate(idxs, axis=0))
    if payload is not None:
        out += (jnp.concatenate(pays, axis=0),)
    return out


def _mix_kernel(x_ref, hf_ref, hb_ref, mo_ref, ao_ref, mod_ref, mhg_ref, n2_ref, wm_ref, wa_ref, wq_ref,
                sa_ref, sb_ref, x1_ref, h2_ref, eidx_ref, gate_ref, qp_scr, e_scr, g_scr):
    tm = x_ref.shape[0]
    hs = hf_ref[...] + hb_ref[...]
    parts = []
    for hd in range(M_HEADS):
        blk = hs[:, hd * M_DH:(hd + 1) * M_DH]
        parts.append(blk * lax.rsqrt(jnp.mean(blk * blk, axis=-1, keepdims=True) + EPS))
    m_out = _sigmoid(mo_ref[...]) * (jnp.concatenate(parts, axis=1) * mhg_ref[...])
    mix = (jnp.dot(m_out.astype(BF16), wm_ref[...], preferred_element_type=F32)
           + jnp.dot(ao_ref[...], wa_ref[...], preferred_element_type=F32))
    x1 = x_ref[...] + mod_ref[0, 2:3, :] * mix
    x1_ref[...] = x1
    h2 = x1 * lax.rsqrt(jnp.mean(x1 * x1, axis=-1, keepdims=True) + EPS) * n2_ref[...]
    h2 = h2 * (1.0 + mod_ref[0, 4:5, :]) + mod_ref[0, 3:4, :]
    h2_ref[...] = _pack_bf16_pairs(h2)
    qp = jnp.dot(h2.astype(BF16), wq_ref[...], preferred_element_type=F32)
    for p in range(P_HEADS):
        qp_scr[p] = qp[:, p * P_DKEY:(p + 1) * P_DKEY].astype(BF16)
    sub_a = sa_ref[...]
    sub_b = sb_ref[...]

    def head_body(p, carry):
        for half in range(tm // N_KEYS):
            cols = slice(half * N_KEYS, (half + 1) * N_KEYS)
            qh = qp_scr[p, pl.ds(half * N_KEYS, N_KEYS), :]
            s_a = _dot_t(sub_a, qh[:, 0:P_HALF])
            s_b = _dot_t(sub_b, qh[:, P_HALF:P_DKEY])
            va, ia = _top16_rows(s_a)
            vb, ib = _top16_rows(s_b)
            keep = [P_TOPK // (i + 1) for i in range(P_TOPK)]
            pad = -sum(keep) % 8
            cand = jnp.concatenate([va[i:i + 1, :] + vb[0:keep[i], :] for i in range(P_TOPK)]
                                   + [jnp.full((pad, N_KEYS), -jnp.inf, F32)], axis=0)
            cidx = jnp.concatenate([ia[i:i + 1, :] * float(N_KEYS) + ib[0:keep[i], :] for i in range(P_TOPK)]
                                   + [jnp.zeros((pad, N_KEYS), F32)], axis=0)
            top, _, eidx = _top16_rows(cand, cidx)
            ex = jnp.exp(top - jnp.max(top, axis=0, keepdims=True))
            gates = ex / jnp.sum(ex, axis=0, keepdims=True)
            r0 = pl.multiple_of(p * P_TOPK, P_TOPK)
            e_scr[pl.ds(r0, P_TOPK), cols] = eidx
            g_scr[pl.ds(r0, P_TOPK), cols] = gates
        return carry

    lax.fori_loop(0, P_HEADS, head_body, 0)
    for half in range(tm // N_KEYS):
        cols = slice(half * N_KEYS, (half + 1) * N_KEYS)
        eidx_ref[cols, :] = e_scr[:, cols].T.astype(I32)
        gate_ref[cols, :] = g_scr[:, cols].T


def _mix_call(x2d, seq, h_f, h_b, mo, a_out, mod3, mod_row0, mhg, n2, w_m, w_a, w_q, sub_a, sub_b):
    n = x2d.shape[0]
    tm = TOKEN_TILE
    per_seq = seq // tm

    def tok(i):
        return (i, 0)

    def const2(i):
        return (0, 0)

    in_specs = [
        pl.BlockSpec((tm, D_MODEL), tok),
        pl.BlockSpec((tm, M_WIDTH), tok), pl.BlockSpec((tm, M_WIDTH), tok), pl.BlockSpec((tm, M_WIDTH), tok),
        pl.BlockSpec((tm, A_WIDTH), tok),
        pl.BlockSpec((1, 6, D_MODEL), lambda i: (mod_row0 + (i // per_seq if mod_row0 else 0), 0, 0)),
        pl.BlockSpec((1, M_WIDTH), const2),
        pl.BlockSpec((1, D_MODEL), const2),
        pl.BlockSpec((M_WIDTH, D_MODEL), const2),
        pl.BlockSpec((A_WIDTH, D_MODEL), const2),
        pl.BlockSpec((D_MODEL, P_HEADS * P_DKEY), const2),
        pl.BlockSpec((N_KEYS, P_HALF), const2),
        pl.BlockSpec((N_KEYS, P_HALF), const2),
    ]
    out_specs = [pl.BlockSpec((tm, D_MODEL), tok), pl.BlockSpec((tm, D_MODEL // 2), tok),
                 pl.BlockSpec((tm, N_SEL), tok), pl.BlockSpec((tm, N_SEL), tok)]
    out_shape = [jax.ShapeDtypeStruct((n, D_MODEL), F32), jax.ShapeDtypeStruct((n, D_MODEL // 2), I32),
                 jax.ShapeDtypeStruct((n, N_SEL), I32), jax.ShapeDtypeStruct((n, N_SEL), F32)]
    return pl.pallas_call(
        _mix_kernel, grid=(n // tm,), in_specs=in_specs, out_specs=out_specs, out_shape=out_shape,
        scratch_shapes=[pltpu.VMEM((P_HEADS, tm, P_DKEY), BF16), pltpu.VMEM((N_SEL, tm), F32),
                        pltpu.VMEM((N_SEL, tm), F32)],
        compiler_params=pltpu.CompilerParams(dimension_semantics=("parallel",)),
        name="mix",
    )(x2d, h_f, h_b, mo, a_out, mod3, mhg, n2, w_m, w_a, w_q, sub_a, sub_b)


SC_LANES = 16
SC_CORES = 2
SC_SUBCORES = 16
SC_WORKERS = SC_CORES * SC_SUBCORES
SC_TOKENS = 16
SC_GROUP = SC_LANES
SC_NGROUPS = N_SEL // SC_GROUP
SC_SLOTS = 4
ROW_WORDS = D_MODEL // 2
SC_DOT_ROWS = 8
SC_DOT_PARTIALS = 2
SC_OWORDS = 16 * SC_LANES
HI_MASK = -65536
PACK_ROWS = 512
SCORE_TOKENS = 1024
SCORE_EXPERTS = 2048
DENSE_LATENT_CHUNKS = (1, 3, 4, 6, 7)
GELU_C0 = 0.7978845608028654
GELU_C1 = 0.044715


def _pack_bf16_pairs(x):
    half = x.shape[1] // 2
    bits = lax.bitcast_convert_type(x.astype(BF16).astype(F32), I32)
    return (bits[:, :half] & HI_MASK) | lax.shift_right_logical(bits[:, half:], jnp.int32(16))


def _pack_tables_kernel(u_ref, v_ref, uv_ref, vp_ref, ub_ref):
    v_words = _pack_bf16_pairs(v_ref[...])
    uv_ref[:, 0:ROW_WORDS] = _pack_bf16_pairs(u_ref[...])
    uv_ref[:, ROW_WORDS:2 * ROW_WORDS] = v_words
    vp_ref[...] = v_words
    ub_ref[...] = u_ref[...].astype(BF16)


def _pack_tables_call(u_tab, v_tab):
    n_exp = u_tab.shape[0]
    blk = pl.BlockSpec((PACK_ROWS, D_MODEL), lambda i: (i, 0))
    half = pl.BlockSpec((PACK_ROWS, ROW_WORDS), lambda i: (i, 0))
    return pl.pallas_call(
        _pack_tables_kernel, grid=(n_exp // PACK_ROWS,), in_specs=[blk, blk], out_specs=[blk, half, blk],
        out_shape=[jax.ShapeDtypeStruct((n_exp, 2 * ROW_WORDS), I32), jax.ShapeDtypeStruct((n_exp, ROW_WORDS), I32),
                   jax.ShapeDtypeStruct((n_exp, D_MODEL), BF16)],
        compiler_params=pltpu.CompilerParams(dimension_semantics=("parallel",)),
        name="pack_tables",
    )(u_tab, v_tab)


def _unpack_bf16_pairs(words):
    hi = lax.bitcast_convert_type(words & HI_MASK, F32)
    lo = lax.bitcast_convert_type(lax.shift_left(words, jnp.int32(16)), F32)
    return jnp.concatenate([hi, lo], axis=1)


def _scores_kernel(x_ref, u_ref, o_ref):
    o_ref[...] = _pack_bf16_pairs(_dot_t(_unpack_bf16_pairs(x_ref[...]).astype(BF16), u_ref[...]))


def _scores_call(h2p, u_bf16):
    n = h2p.shape[0]
    n_exp = u_bf16.shape[0]
    tm = math.gcd(n, SCORE_TOKENS)
    return pl.pallas_call(
        _scores_kernel, grid=(n // tm, n_exp // SCORE_EXPERTS),
        in_specs=[pl.BlockSpec((tm, ROW_WORDS), lambda i, j: (i, 0)),
                  pl.BlockSpec((SCORE_EXPERTS, D_MODEL), lambda i, j: (j, 0))],
        out_specs=pl.BlockSpec((tm, SCORE_EXPERTS // 2), lambda i, j: (i, j)),
        out_shape=jax.ShapeDtypeStruct((n, n_exp // 2), I32),
        compiler_params=pltpu.CompilerParams(dimension_semantics=("parallel", "parallel")),
        name="expert_scores",
    )(h2p, u_bf16)


def _sc_gelu(a):
    z = GELU_C0 * (a + GELU_C1 * (a * a * a))
    tanh = 1.0 - 2.0 / (jnp.exp(2.0 * z) + 1.0)
    return 0.5 * a * (1.0 + tanh)


def _sc_split(words):
    return (plsc.bitcast(words & HI_MASK, F32), plsc.bitcast(lax.shift_left(words, jnp.int32(16)), F32))


def _sc_mul_bf16(a_words, b_words):
    return plsc.bitcast(a_words, BF16) * plsc.bitcast(b_words, BF16)


def _sc_split_sum(p, q):
    return _sc_split(plsc.bitcast(p + q, I32))


def _peer_sc_kernel(h2_hbm, eidx_hbm, gate_hbm, uv_hbm, out_hbm,
                    xbuf, ibuf, gbuf, obuf, uvbuf, mbuf, wbuf, sems, sems_in, sems_out):
    n = h2_hbm.shape[0]
    per_worker = n // SC_WORKERS
    blk_tokens = xbuf.shape[1]
    n_blocks = per_worker // blk_tokens
    wid = lax.axis_index("c") * SC_SUBCORES + lax.axis_index("s")
    lane = lax.iota(I32, SC_LANES)

    def split_item(item):
        return lax.shift_right_logical(item, SC_NGROUPS.bit_length() - 1), item & (SC_NGROUPS - 1)

    def gather_copies(bset, item, slot):
        t, g = split_item(item)
        idx = ibuf[bset, t, pl.ds(g * SC_GROUP, SC_GROUP)]
        return (pltpu.make_async_copy(uv_hbm.at[idx], uvbuf.at[slot], sems.at[slot]),)

    def dots(bset, t, slot):
        zero = jnp.zeros((SC_LANES,), F32)

        @pl.loop(0, SC_GROUP, step=SC_DOT_ROWS)
        def _(r0):
            accs = [[zero] * SC_DOT_PARTIALS for _ in range(SC_DOT_ROWS)]
            for k in range(0, ROW_WORDS // SC_LANES, 4):
                xs = [xbuf[bset, t, pl.ds((k + q) * SC_LANES, SC_LANES)] for q in range(4)]
                for i in range(SC_DOT_ROWS):
                    m = [_sc_mul_bf16(xs[q], uvbuf[slot, r0 + i, pl.ds((k + q) * SC_LANES, SC_LANES)])
                         for q in range(4)]
                    hi, lo = _sc_split_sum(m[0] + m[1], m[2] + m[3])
                    p = (k // 4) % SC_DOT_PARTIALS
                    accs[i][p] = accs[i][p] + (hi + lo)
            for i in range(SC_DOT_ROWS):
                mbuf[r0 + i, :] = functools.reduce(lambda a, b: a + b, accs[i])

        tot = zero
        for c in range(SC_LANES):
            tot = tot + plsc.load_gather(mbuf, [lane, jnp.full((SC_LANES,), c, I32)])
        return tot

    def accumulate(bset, t, slot):
        nv = SC_OWORDS // SC_LANES
        for oc in range(ROW_WORDS // SC_OWORDS):
            w0 = oc * SC_OWORDS
            accs = (tuple(obuf[bset, t, pl.ds(w0 + j * SC_LANES, SC_LANES)] for j in range(nv))
                    + tuple(obuf[bset, t, pl.ds(ROW_WORDS + w0 + j * SC_LANES, SC_LANES)] for j in range(nv)))

            def row_quad(rq, accs):
                r = 4 * rq
                ws = [plsc.load_gather(wbuf, [jnp.full((SC_LANES,), r + q, I32)]) for q in range(4)]
                his, los = [], []
                for j in range(nv):
                    m = [_sc_mul_bf16(ws[q], uvbuf[slot, r + q, pl.ds(ROW_WORDS + w0 + j * SC_LANES, SC_LANES)])
                         for q in range(4)]
                    hi, lo = _sc_split_sum(m[0] + m[1], m[2] + m[3])
                    his.append(accs[j] + hi)
                    los.append(accs[nv + j] + lo)
                return tuple(his) + tuple(los)

            accs = lax.fori_loop(0, SC_GROUP // 4, row_quad, accs)
            for j in range(nv):
                obuf[bset, t, pl.ds(w0 + j * SC_LANES, SC_LANES)] = accs[j]
                obuf[bset, t, pl.ds(ROW_WORDS + w0 + j * SC_LANES, SC_LANES)] = accs[nv + j]

    def pack_weights(w):
        bits = plsc.bitcast(w, I32)
        rounded = (bits + 0x7FFF + (lax.shift_right_logical(bits, jnp.int32(16)) & 1)) & HI_MASK
        return rounded | lax.shift_right_logical(rounded, jnp.int32(16))

    n_items = blk_tokens * SC_NGROUPS

    def block_rows(blk):
        return pl.ds(pl.multiple_of(wid * per_worker + blk * blk_tokens, blk_tokens), blk_tokens)

    def load_copies(blk, bset):
        rows = block_rows(blk)
        return (pltpu.make_async_copy(h2_hbm.at[rows], xbuf.at[bset], sems_in.at[bset]),
                pltpu.make_async_copy(eidx_hbm.at[rows], ibuf.at[bset], sems_in.at[bset]),
                pltpu.make_async_copy(gate_hbm.at[rows], gbuf.at[bset], sems_in.at[bset]))

    def store_copy(blk, bset):
        return pltpu.make_async_copy(obuf.at[bset], out_hbm.at[block_rows(blk)], sems_out.at[bset])

    for c in load_copies(0, 0):
        c.start()

    @pl.loop(0, n_blocks)
    def _(blk):
        bset = blk & 1
        for c in load_copies(blk, bset):
            c.wait()

        @pl.when(blk + 1 < n_blocks)
        def _():
            for c in load_copies(blk + 1, 1 - bset):
                c.start()

        @pl.when(blk >= 2)
        def _():
            store_copy(blk - 2, bset).wait()

        @pl.loop(0, blk_tokens)
        def _(t):
            zero = jnp.zeros((SC_LANES,), F32)
            for j in range(D_MODEL // SC_LANES):
                obuf[bset, t, pl.ds(j * SC_LANES, SC_LANES)] = zero

        for ahead in range(SC_SLOTS - 1):
            for c in gather_copies(bset, ahead, ahead):
                c.start()

        @pl.loop(0, n_items)
        def _(item):
            t, g = split_item(item)
            slot = item & (SC_SLOTS - 1)
            ahead = item + (SC_SLOTS - 1)

            @pl.when(ahead < n_items)
            def _():
                for c in gather_copies(bset, ahead, ahead & (SC_SLOTS - 1)):
                    c.start()

            for c in gather_copies(bset, item, slot):
                c.wait()
            a = dots(bset, t, slot)
            wbuf[...] = pack_weights(gbuf[bset, t, pl.ds(g * SC_GROUP, SC_GROUP)] * _sc_gelu(a))
            accumulate(bset, t, slot)

        store_copy(blk, bset).start()

    for blk in range(max(n_blocks - 2, 0), n_blocks):
        store_copy(blk, blk & 1).wait()


def _peer_experts(h2p, eidx, gates, uv_pack):
    n = h2p.shape[0]
    assert n % (8 * SC_WORKERS) == 0
    blk_tokens = math.gcd(n // SC_WORKERS, SC_TOKENS)
    mesh = plsc.VectorSubcoreMesh(core_axis_name="c", subcore_axis_name="s")
    fn = pl.kernel(
        _peer_sc_kernel,
        out_type=jax.ShapeDtypeStruct((n, D_MODEL), F32),
        mesh=mesh,
        scratch_types=[
            pltpu.VMEM((2, blk_tokens, ROW_WORDS), I32),
            pltpu.VMEM((2, blk_tokens, N_SEL), I32),
            pltpu.VMEM((2, blk_tokens, N_SEL), F32),
            pltpu.VMEM((2, blk_tokens, D_MODEL), F32),
            pltpu.VMEM((SC_SLOTS, SC_GROUP, 2 * ROW_WORDS), I32),
            pltpu.VMEM((SC_GROUP, SC_LANES), F32),
            pltpu.VMEM((SC_LANES,), I32),
            pltpu.SemaphoreType.DMA((SC_SLOTS,)), pltpu.SemaphoreType.DMA((2,)), pltpu.SemaphoreType.DMA((2,)),
        ],
        compiler_params=pltpu.CompilerParams(needs_layout_passes=False),
        cost_estimate=pl.CostEstimate(
            flops=4 * n * N_SEL * D_MODEL, transcendentals=n * N_SEL,
            bytes_accessed=4 * (2 * n * N_SEL * ROW_WORDS + n * ROW_WORDS + n * D_MODEL + 2 * n * N_SEL)),
        name="peer_experts",
    )
    return fn(h2p, eidx, gates, uv_pack)


def _peer_sc_scored_kernel(a_hbm, eidx_hbm, gate_hbm, v_hbm, out_hbm,
                           abuf, ibuf, gbuf, obuf, vbuf, wbuf, sems, sems_a):
    n = a_hbm.shape[0]
    per_worker = n // SC_WORKERS
    blk_tokens = ibuf.shape[0]
    wid = lax.axis_index("c") * SC_SUBCORES + lax.axis_index("s")

    def split_item(item):
        return lax.shift_right_logical(item, SC_NGROUPS.bit_length() - 1), item & (SC_NGROUPS - 1)

    def gather_copy(item, slot):
        t, g = split_item(item)
        idx = ibuf[t, pl.ds(g * SC_GROUP, SC_GROUP)]
        return pltpu.make_async_copy(v_hbm.at[idx], vbuf.at[slot], sems.at[slot])

    def accumulate(t, slot):
        nv = SC_OWORDS // SC_LANES
        for oc in range(ROW_WORDS // SC_OWORDS):
            w0 = oc * SC_OWORDS
            accs = (tuple(obuf[t, pl.ds(w0 + j * SC_LANES, SC_LANES)] for j in range(nv))
                    + tuple(obuf[t, pl.ds(ROW_WORDS + w0 + j * SC_LANES, SC_LANES)] for j in range(nv)))

            def row_quad(rq, accs):
                r = 4 * rq
                ws = [plsc.load_gather(wbuf, [jnp.full((SC_LANES,), r + q, I32)]) for q in range(4)]
                his, los = [], []
                for j in range(nv):
                    m = [_sc_mul_bf16(ws[q], vbuf[slot, r + q, pl.ds(w0 + j * SC_LANES, SC_LANES)])
                         for q in range(4)]
                    hi, lo = _sc_split_sum(m[0] + m[1], m[2] + m[3])
                    his.append(accs[j] + hi)
                    los.append(accs[nv + j] + lo)
                return tuple(his) + tuple(los)

            accs = lax.fori_loop(0, SC_GROUP // 4, row_quad, accs)
            for j in range(nv):
                obuf[t, pl.ds(w0 + j * SC_LANES, SC_LANES)] = accs[j]
                obuf[t, pl.ds(ROW_WORDS + w0 + j * SC_LANES, SC_LANES)] = accs[nv + j]

    def pack_weights(w):
        bits = plsc.bitcast(w, I32)
        rounded = (bits + 0x7FFF + (lax.shift_right_logical(bits, jnp.int32(16)) & 1)) & HI_MASK
        return rounded | lax.shift_right_logical(rounded, jnp.int32(16))

    n_items = blk_tokens * SC_NGROUPS

    @pl.loop(0, per_worker // blk_tokens)
    def _(blk):
        tok0 = pl.multiple_of(wid * per_worker + blk * blk_tokens, blk_tokens)

        def score_copy(t):
            return pltpu.make_async_copy(a_hbm.at[tok0 + t], abuf.at[t & 1], sems_a.at[t & 1])

        score_copy(0).start()
        pltpu.sync_copy(eidx_hbm.at[pl.ds(tok0, blk_tokens)], ibuf)
        pltpu.sync_copy(gate_hbm.at[pl.ds(tok0, blk_tokens)], gbuf)

        @pl.loop(0, blk_tokens)
        def _(t):
            zero = jnp.zeros((SC_LANES,), F32)
            for j in range(D_MODEL // SC_LANES):
                obuf[t, pl.ds(j * SC_LANES, SC_LANES)] = zero

        for ahead in range(SC_SLOTS - 1):
            gather_copy(ahead, ahead).start()

        @pl.loop(0, n_items)
        def _(item):
            t, g = split_item(item)
            slot = item & (SC_SLOTS - 1)
            ahead = item + (SC_SLOTS - 1)

            @pl.when(ahead < n_items)
            def _():
                gather_copy(ahead, ahead & (SC_SLOTS - 1)).start()

            @pl.when(g == 0)
            def _():
                score_copy(t).wait()

                @pl.when(t + 1 < blk_tokens)
                def _():
                    score_copy(t + 1).start()

            gather_copy(item, slot).wait()
            idx = ibuf[t, pl.ds(g * SC_GROUP, SC_GROUP)]
            half = SCORE_EXPERTS // 2
            word = plsc.load_gather(
                abuf, [jnp.full((SC_LANES,), t & 1, I32),
                       lax.shift_right_logical(idx, jnp.int32(SCORE_EXPERTS.bit_length() - 1)) * half
                       + (idx & (half - 1))])
            hi, lo = _sc_split(word)
            a = jnp.where((idx & half) == 0, hi, lo)
            wbuf[...] = pack_weights(gbuf[t, pl.ds(g * SC_GROUP, SC_GROUP)] * _sc_gelu(a))
            accumulate(t, slot)

        pltpu.sync_copy(obuf, out_hbm.at[pl.ds(tok0, blk_tokens)])


def _peer_experts_scored(scores, eidx, gates, v_pack):
    n, n_exp = scores.shape
    assert n % (8 * SC_WORKERS) == 0
    blk_tokens = math.gcd(n // SC_WORKERS, 2 * SC_TOKENS)
    mesh = plsc.VectorSubcoreMesh(core_axis_name="c", subcore_axis_name="s")
    fn = pl.kernel(
        _peer_sc_scored_kernel,
        out_type=jax.ShapeDtypeStruct((n, D_MODEL), F32),
        mesh=mesh,
        scratch_types=[
            pltpu.VMEM((2, n_exp), I32),
            pltpu.VMEM((blk_tokens, N_SEL), I32),
            pltpu.VMEM((blk_tokens, N_SEL), F32),
            pltpu.VMEM((blk_tokens, D_MODEL), F32),
            pltpu.VMEM((SC_SLOTS, SC_GROUP, ROW_WORDS), I32),
            pltpu.VMEM((SC_LANES,), I32),
            pltpu.SemaphoreType.DMA((SC_SLOTS,)), pltpu.SemaphoreType.DMA((2,)),
        ],
        compiler_params=pltpu.CompilerParams(needs_layout_passes=False),
        cost_estimate=pl.CostEstimate(
            flops=2 * n * N_SEL * D_MODEL, transcendentals=n * N_SEL,
            bytes_accessed=4 * (n * N_SEL * ROW_WORDS + n * n_exp + n * D_MODEL + 2 * n * N_SEL)),
        name="peer_experts_scored",
    )
    return fn(scores, eidx, gates, v_pack)


def _resid_kernel(x1_ref, p_ref, mod_ref, o_ref):
    o_ref[...] = x1_ref[...] + mod_ref[0, 5:6, :] * p_ref[...]


def _resid_call(x1, peer_out, seq, mod3, mod_row0):
    n = x1.shape[0]
    tm = TOKEN_TILE
    per_seq = seq // tm
    tok = pl.BlockSpec((tm, D_MODEL), lambda i: (i, 0))
    return pl.pallas_call(
        _resid_kernel, grid=(n // tm,),
        in_specs=[tok, tok,
                  pl.BlockSpec((1, 6, D_MODEL), lambda i: (mod_row0 + (i // per_seq if mod_row0 else 0), 0, 0))],
        out_specs=tok, out_shape=jax.ShapeDtypeStruct((n, D_MODEL), F32),
        compiler_params=pltpu.CompilerParams(dimension_semantics=("parallel",)),
        name="resid",
    )(x1, peer_out, mod3)


def _rope_tables(seq, rotate):
    if not rotate:
        return jnp.ones((seq, KV_WIDTH), F32), jnp.zeros((seq, KV_WIDTH), F32)
    quarter = A_DH // 4
    t = jnp.arange(seq)
    row = (t // GRID_W).astype(F32)
    col = (t % GRID_W).astype(F32)
    inv = ROPE_BASE ** (-jnp.arange(quarter, dtype=F32) / quarter)
    d = jnp.arange(A_DH)
    pos = jnp.where(d[None, :] < A_DH // 2, row[:, None], col[:, None])
    ang = pos * inv[d % quarter][None, :]
    sign = jnp.where((d % (A_DH // 2)) < quarter, -1.0, 1.0).astype(F32)
    cos = jnp.cos(ang)
    sin = jnp.sin(ang) * sign[None, :]
    return jnp.tile(cos, (1, KV_WIDTH // A_DH)), jnp.tile(sin, (1, KV_WIDTH // A_DH))


def _run_chunk(x, mod3, mod_row0, prm, cache, rotate, gate_on, dense_scores):
    (n1, n2, w_main, w_g, w_gt, b_g, b_gt, mhg, qg_t, kg_t, bd, sink, w_m, w_a, w_q, sub_a, sub_b,
     (uv_pack, v_pack, u_bf16)) = prm
    batch, seq, _ = x.shape
    n = batch * seq
    x2d = x.reshape(n, D_MODEL)
    if gate_on is not None:
        x2d, _ = lax.optimization_barrier((x2d, gate_on))
    cos, sin = _rope_tables(seq, rotate)
    mq, mk, mv, mo, gcol, grow, aq, ak, av = _inproj_call(
        x2d, seq, mod3, mod_row0, n1, w_main, w_g, w_gt, b_g, b_gt, qg_t, kg_t, bd, cos, sin)
    kc, vc, c0, m0 = cache
    h_f, h_b, c_fin, m_fin = _mlstm_call(mq, mk, mv, gcol, grow, c0, m0, batch, seq)
    if kc is None:
        a_out = _attn_ctx_call(sink, aq, ak, av, batch, seq)
    else:
        a_out = _attn_lat_call(sink, aq, ak, av, kc, vc, batch, seq)
    x1, h2p, eidx, gates = _mix_call(x2d, seq, h_f, h_b, mo, a_out, mod3, mod_row0, mhg, n2, w_m, w_a, w_q,
                                     sub_a, sub_b)
    if dense_scores:
        peer_out = _peer_experts_scored(_scores_call(h2p, u_bf16), eidx, gates, v_pack)
    else:
        peer_out = _peer_experts(h2p, eidx, gates, uv_pack)
    y = _resid_call(x1, peer_out, seq, mod3, mod_row0).reshape(batch, seq, D_MODEL)
    return y, h2p, peer_out, ak, av, c_fin, m_fin


def _pack_state(C, n_vec, m):
    b = C.shape[0]
    caug = jnp.concatenate([C, jnp.broadcast_to(n_vec[..., None], C.shape)], axis=-1)
    caug = caug.reshape(b, 2 * M_HEADS, M_DH, 2 * M_DH)
    m_rep = jnp.broadcast_to(m.reshape(b, 2 * M_HEADS, 1, 1), (b, 2 * M_HEADS, 8, M_DH))
    return caug.astype(F32), m_rep.astype(F32)


def kernel(x_prompt, x_sample, c, cache_attn_k, cache_attn_v, state_mlstm_C, state_mlstm_n, state_mlstm_m,
           c_ctx, w_ada, b_ada, norm1_g, norm2_g, w_in, b_gates, mh_norm_g, q_norm_g, k_norm_g, sink_logits,
           w_out, peer_w_q, peer_sub_a, peer_sub_b, peer_u, peer_v):
    depth = w_ada.shape[0]
    assert depth == 1
    batch, seq, _ = x_prompt.shape
    dec_batch, dec_seq, _ = x_sample.shape
    assert dec_batch + 1 <= MOD_ROWS and batch % sum(CTX_CHUNK_WEIGHTS) == 0 and dec_batch % LATENT_CHUNKS == 0
    l = 0

    cond = jnp.concatenate([c_ctx[None, :], c, jnp.zeros((MOD_ROWS - 1 - dec_batch, D_MODEL), F32)], axis=0)
    mod3 = _ada_call(cond, w_ada[l], b_ada[l]).reshape(MOD_ROWS, 6, D_MODEL)

    wi = w_in[l]
    g0 = 4 * M_WIDTH
    w_main = jnp.concatenate([wi[:, :g0], wi[:, g0 + N_GATES:]], axis=1).astype(BF16)
    w_g = wi[:, g0:g0 + N_GATES]
    seg = jnp.arange(A_WIDTH) // A_DH
    bd = jnp.where(seg[:, None] == seg[None, :], 1.0 / A_DH, 0.0).astype(F32)
    prm = (norm1_g[l][None, :], norm2_g[l][None, :], w_main, w_g, w_g.T, b_gates[l][None, :], b_gates[l][:, None],
           mh_norm_g[l][None, :], jnp.tile(q_norm_g[l], A_HEADS)[None, :], jnp.tile(k_norm_g[l], A_KV)[None, :], bd,
           sink_logits[l], w_out[l][:M_WIDTH].astype(BF16), w_out[l][M_WIDTH:].astype(BF16),
           peer_w_q[l].astype(BF16), peer_sub_a[l].astype(BF16), peer_sub_b[l].astype(BF16),
           _pack_tables_call(peer_u[l], peer_v[l]))

    zeros_c = jnp.zeros((batch, 2, M_HEADS, M_DH, M_DH), F32)
    c0, m0 = _pack_state(zeros_c, zeros_c[..., 0], jnp.full((batch, 2, M_HEADS), NEG, F32))
    c0s, m0s = _pack_state(state_mlstm_C[:, l], state_mlstm_n[:, l], state_mlstm_m[:, l])
    past = cache_attn_k.shape[2]
    kc = cache_attn_k[:, l].reshape(dec_batch, past, KV_WIDTH)
    vc = cache_attn_v[:, l].reshape(dec_batch, past, KV_WIDTH)

    jobs = []
    ctx_sizes = [batch * f // sum(CTX_CHUNK_WEIGHTS) for f in CTX_CHUNK_WEIGHTS]
    assert sum(ctx_sizes) == batch
    b0 = 0
    for size in ctx_sizes:
        b1 = b0 + size
        jobs.append((x_prompt[b0:b1], 0, (None, None, c0[b0:b1], m0[b0:b1]), False, False))
        b0 = b1
    for ci, b0 in enumerate(range(0, dec_batch, dec_batch // LATENT_CHUNKS)):
        b1 = b0 + dec_batch // LATENT_CHUNKS
        jobs.append((x_sample[b0:b1], 1 + b0, (kc[b0:b1], vc[b0:b1], c0s[b0:b1], m0s[b0:b1]), True,
                     ci in DENSE_LATENT_CHUNKS))
    outs = []
    for i, (x_c, mod_row0, cache, rotate, dense) in enumerate(jobs):
        gate = tuple(g for g in (outs[i - 1][1] if i >= 1 else None,
                                 outs[i - EXPERT_LAG][2] if i >= EXPERT_LAG else None) if g is not None)
        outs.append(_run_chunk(x_c, mod3, mod_row0, prm, cache, rotate, gate or None, dense))
    ctx, lat = outs[:len(ctx_sizes)], outs[len(ctx_sizes):]
    y_p = jnp.concatenate([o[0] for o in ctx], axis=0)
    y_s = jnp.concatenate([o[0] for o in lat], axis=0)
    k_new = jnp.concatenate([o[3] for o in ctx], axis=0)
    v_new = jnp.concatenate([o[4] for o in ctx], axis=0)
    c_fin = jnp.concatenate([o[5] for o in ctx], axis=0)
    m_fin = jnp.concatenate([o[6] for o in ctx], axis=0)

    c_fin = c_fin.reshape(batch, 2, M_HEADS, M_DH, 2 * M_DH)
    new_c = c_fin[..., :M_DH][:, None]
    new_n = c_fin[..., M_DH][:, None]
    new_m = m_fin[:, :, 0, 0].reshape(batch, 2, M_HEADS)[:, None]
    new_k = k_new.reshape(batch, 1, seq, A_KV, A_DH)
    new_v = v_new.reshape(batch, 1, seq, A_KV, A_DH)
    return y_p, y_s, new_k, new_v, new_c, new_n, new_m
```

```python
import functools
import math

import jax
import jax.numpy as jnp
from jax import lax
from jax.experimental import pallas as pl
from jax.experimental.pallas import tpu as pltpu
from jax.experimental.pallas import tpu_sc as plsc

F32 = jnp.float32
BF16 = jnp.bfloat16
I32 = jnp.int32
HI = lax.Precision.HIGHEST

D_MODEL = 1024
EPS = 1e-6
NEG = -1e30
GRID_W = 64
M_HEADS = 4
M_WIDTH = 512
M_DH = 128
A_HEADS = 8
A_KV = 2
A_REP = 4
A_DH = 64
A_WIDTH = 512
KV_WIDTH = A_KV * A_DH
BLOCK = 128
ROPE_BASE = 10000.0
N_KEYS = 128
P_HEADS = 8
P_DKEY = 256
P_HALF = 128
P_TOPK = 16
N_SEL = P_HEADS * P_TOPK
N_GATES = 4 * M_HEADS
MAIN_COLS = 4 * M_WIDTH + A_WIDTH + 2 * KV_WIDTH
MOD_ROWS = 16

TOKEN_TILE = 256
MLSTM_CHUNK = 128
ADA_COL_TILE = 768
CTX_CHUNK_WEIGHTS = (1, 2, 3, 4, 6)
EXPERT_LAG = 3
LATENT_CHUNKS = 8


def _sigmoid(x):
    return 1.0 / (1.0 + jnp.exp(-x))


def _log_sigmoid(x):
    return jnp.minimum(x, 0.0) - jnp.log1p(jnp.exp(-jnp.abs(x)))


def _dot_t(a, b, precision=None):
    return lax.dot_general(a, b, (((1,), (1,)), ((), ())), precision=precision,
                           preferred_element_type=F32)


def _ada_kernel(c_ref, w_ref, b_ref, o_ref):
    c = c_ref[...]
    s = c * _sigmoid(c)
    o_ref[...] = jnp.dot(s, w_ref[...], precision=HI, preferred_element_type=F32) + b_ref[...]


def _ada_call(cond, w_ada, b_ada):
    n_out = w_ada.shape[1]
    return pl.pallas_call(
        _ada_kernel,
        grid=(n_out // ADA_COL_TILE,),
        in_specs=[pl.BlockSpec((MOD_ROWS, D_MODEL), lambda j: (0, 0)),
                  pl.BlockSpec((D_MODEL, ADA_COL_TILE), lambda j: (0, j)),
                  pl.BlockSpec((1, ADA_COL_TILE), lambda j: (0, j))],
        out_specs=pl.BlockSpec((MOD_ROWS, ADA_COL_TILE), lambda j: (0, j)),
        out_shape=jax.ShapeDtypeStruct((MOD_ROWS, n_out), F32),
        name="ada",
    )(cond, w_ada, b_ada.reshape(1, n_out))


def _swap16(x):
    n = x.shape[-1]
    lane = lax.broadcasted_iota(I32, x.shape, x.ndim - 1)
    return jnp.where((lane & 16) == 0, pltpu.roll(x, n - 16, x.ndim - 1), pltpu.roll(x, 16, x.ndim - 1))


def _inproj_kernel(x_ref, mod_ref, n1_ref, w_ref, wg_ref, wgt_ref, bg_ref, bgt_ref, qg_ref, kg_ref,
                   bd_ref, cos_ref, sin_ref,
                   mq_ref, mk_ref, mv_ref, mo_ref, gc_ref, gr_ref, aq_ref, ak_ref, av_ref):
    x = x_ref[...]
    h = x * lax.rsqrt(jnp.mean(x * x, axis=-1, keepdims=True) + EPS) * n1_ref[...]
    h = h * (1.0 + mod_ref[0, 1:2, :]) + mod_ref[0, 0:1, :]
    z = jnp.dot(h.astype(BF16), w_ref[...], preferred_element_type=F32)

    mq_ref[...] = (z[:, 0:M_WIDTH] * (M_DH ** -0.5)).astype(BF16)
    mk_ref[...] = z[:, M_WIDTH:2 * M_WIDTH].astype(BF16)
    mv_ref[...] = z[:, 2 * M_WIDTH:3 * M_WIDTH].astype(BF16)
    mo_ref[...] = z[:, 3 * M_WIDTH:4 * M_WIDTH]

    g = jnp.dot(h, wg_ref[...], precision=HI, preferred_element_type=F32) + bg_ref[...]
    kind = lax.broadcasted_iota(I32, g.shape, 1) // M_HEADS
    gc_ref[...] = jnp.where((kind & 1) == 1, _log_sigmoid(g), g)
    gt = _dot_t(wgt_ref[...], h, precision=HI) + bgt_ref[...]
    kind_t = lax.broadcasted_iota(I32, gt.shape, 0) // M_HEADS
    gr_ref[...] = jnp.where((kind_t & 1) == 1, _log_sigmoid(gt), gt)

    o = 4 * M_WIDTH
    aq = z[:, o:o + A_WIDTH]
    ak = z[:, o + A_WIDTH:o + A_WIDTH + KV_WIDTH]
    av_ref[...] = z[:, o + A_WIDTH + KV_WIDTH:o + A_WIDTH + 2 * KV_WIDTH]
    bd = bd_ref[...]
    cos = cos_ref[...]
    sin = sin_ref[...]
    aq = aq * lax.rsqrt(jnp.dot(aq * aq, bd, precision=HI, preferred_element_type=F32) + EPS) * qg_ref[...]
    cos4 = jnp.concatenate([cos] * (A_WIDTH // KV_WIDTH), axis=1)
    sin4 = jnp.concatenate([sin] * (A_WIDTH // KV_WIDTH), axis=1)
    aq = (aq * cos4 + _swap16(aq) * sin4) * (A_DH ** -0.5)
    ak = ak * lax.rsqrt(jnp.dot(ak * ak, bd[0:KV_WIDTH, 0:KV_WIDTH], precision=HI,
                                preferred_element_type=F32) + EPS) * kg_ref[...]
    ak_ref[...] = ak * cos + _swap16(ak) * sin

    lane = lax.broadcasted_iota(I32, (aq.shape[0], KV_WIDTH), 1)
    for hd in range(A_HEADS):
        grp = hd // A_REP
        blk = aq[:, (hd // 2) * KV_WIDTH:(hd // 2 + 1) * KV_WIDTH]
        if hd % 2 != grp:
            blk = pltpu.roll(blk, A_DH, 1)
        keep = (lane >= grp * A_DH) & (lane < (grp + 1) * A_DH)
        aq_ref[hd] = jnp.where(keep, blk, 0.0).astype(BF16)


def _inproj_call(x2d, seq, mod3, mod_row0, n1, w_main, w_g, w_gt, b_g, b_gt, qg_t, kg_t, bd, cos, sin):
    n = x2d.shape[0]
    tm = TOKEN_TILE
    per_seq = seq // tm

    def tok(i):
        return (i, 0)

    def const2(i):
        return (0, 0)

    in_specs = [
        pl.BlockSpec((tm, D_MODEL), tok),
        pl.BlockSpec((1, 6, D_MODEL), lambda i: (mod_row0 + (i // per_seq if mod_row0 else 0), 0, 0)),
        pl.BlockSpec((1, D_MODEL), const2),
        pl.BlockSpec((D_MODEL, MAIN_COLS), const2),
        pl.BlockSpec((D_MODEL, N_GATES), const2),
        pl.BlockSpec((N_GATES, D_MODEL), const2),
        pl.BlockSpec((1, N_GATES), const2),
        pl.BlockSpec((N_GATES, 1), const2),
        pl.BlockSpec((1, A_WIDTH), const2),
        pl.BlockSpec((1, KV_WIDTH), const2),
        pl.BlockSpec((A_WIDTH, A_WIDTH), const2),
        pl.BlockSpec((tm, KV_WIDTH), lambda i: (i % per_seq, 0)),
        pl.BlockSpec((tm, KV_WIDTH), lambda i: (i % per_seq, 0)),
    ]
    out_specs = [
        pl.BlockSpec((tm, M_WIDTH), tok),
        pl.BlockSpec((tm, M_WIDTH), tok),
        pl.BlockSpec((tm, M_WIDTH), tok),
        pl.BlockSpec((tm, M_WIDTH), tok),
        pl.BlockSpec((tm, N_GATES), tok),
        pl.BlockSpec((N_GATES, tm), lambda i: (0, i)),
        pl.BlockSpec((A_HEADS, tm, KV_WIDTH), lambda i: (0, i, 0)),
        pl.BlockSpec((tm, KV_WIDTH), tok),
        pl.BlockSpec((tm, KV_WIDTH), tok),
    ]
    out_shape = [
        jax.ShapeDtypeStruct((n, M_WIDTH), BF16),
        jax.ShapeDtypeStruct((n, M_WIDTH), BF16),
        jax.ShapeDtypeStruct((n, M_WIDTH), BF16),
        jax.ShapeDtypeStruct((n, M_WIDTH), F32),
        jax.ShapeDtypeStruct((n, N_GATES), F32),
        jax.ShapeDtypeStruct((N_GATES, n), F32),
        jax.ShapeDtypeStruct((A_HEADS, n, KV_WIDTH), BF16),
        jax.ShapeDtypeStruct((n, KV_WIDTH), F32),
        jax.ShapeDtypeStruct((n, KV_WIDTH), F32),
    ]
    return pl.pallas_call(
        _inproj_kernel, grid=(n // tm,), in_specs=in_specs, out_specs=out_specs, out_shape=out_shape,
        compiler_params=pltpu.CompilerParams(dimension_semantics=("parallel",)),
        name="inproj",
    )(x2d, mod3, n1, w_main, w_g, w_gt, b_g, b_gt, qg_t, kg_t, bd, cos, sin)


def _mlstm_chain(q, k, v, li_c, lf_c, li_r, lf_r, caug, m, tri, tri_t, mask, reverse):
    L = q.shape[0]
    last = 0 if reverse else L - 1
    b_c = jnp.dot(tri, jnp.broadcast_to(lf_c, (L, L)), precision=HI, preferred_element_type=F32)
    b_r = jnp.dot(jnp.broadcast_to(lf_r, (8, L)), tri_t, precision=HI, preferred_element_type=F32)[0:1, :]
    a_inter = b_c[:, 0:1] + m
    d = jnp.where(mask, b_c - b_r + li_r, -jnp.inf)
    m_t = jnp.maximum(a_inter, jnp.max(d, axis=1, keepdims=True))
    w_inter = jnp.exp(a_inter - m_t)
    s = _dot_t(q, k) * jnp.exp(d - m_t)
    qc = jnp.dot(q, caug.astype(BF16), preferred_element_type=F32)
    num = jnp.dot(s.astype(BF16), v, preferred_element_type=F32) + w_inter * qc[:, 0:M_DH]
    den = jnp.sum(s, axis=1, keepdims=True) + w_inter * qc[:, M_DH:M_DH + 1]
    den = jnp.maximum(jnp.abs(den), jnp.exp(-m_t))
    h = num / den
    m_new = m_t[last:last + 1, :]
    b_last = b_c[last:last + 1, 0:1]
    g_c = jnp.exp(b_last - b_c[:, 0:1] + li_c - m_new)
    decay = jnp.exp(b_last + m - m_new)
    kw = (k.astype(F32) * g_c).astype(BF16)
    vaug = jnp.concatenate([v, jnp.ones_like(v)], axis=1)
    upd = lax.dot_general(kw, vaug, (((0,), (0,)), ((), ())), preferred_element_type=F32)
    return h, decay * caug + upd, m_new


def _mlstm_kernel(qf_ref, kf_ref, vf_ref, gcf_ref, grf_ref, qb_ref, kb_ref, vb_ref, gcb_ref, grb_ref,
                  c0_ref, m0_ref, hf_ref, hb_ref, cfin_ref, mfin_ref, c_scr, m_scr):
    c = pl.program_id(1)
    nc = pl.num_programs(1)
    L = qf_ref.shape[0]

    @pl.when(c == 0)
    def _():
        c_scr[...] = c0_ref[0]
        m_scr[...] = m0_ref[0]

    row = lax.broadcasted_iota(I32, (L, L), 0)
    col = lax.broadcasted_iota(I32, (L, L), 1)
    lower = row >= col
    upper = row <= col
    lower_f = lower.astype(F32)
    upper_f = upper.astype(F32)

    for direction in range(2):
        reverse = direction == 1
        q_ref, k_ref, v_ref, gc_ref, gr_ref, h_ref = (
            (qb_ref, kb_ref, vb_ref, gcb_ref, grb_ref, hb_ref) if reverse
            else (qf_ref, kf_ref, vf_ref, gcf_ref, grf_ref, hf_ref))
        tri, tri_t, mask = (upper_f, lower_f, upper) if reverse else (lower_f, upper_f, lower)
        gc = gc_ref[...]
        gr = gr_ref[...]
        for hd in range(M_HEADS):
            ch = direction * M_HEADS + hd
            sl = slice(hd * M_DH, (hd + 1) * M_DH)
            ci = 2 * direction * M_HEADS + hd
            cf = ci + M_HEADS
            h, caug, m_new = _mlstm_chain(
                q_ref[:, sl], k_ref[:, sl], v_ref[:, sl],
                gc[:, ci:ci + 1], gc[:, cf:cf + 1], gr[ci:ci + 1, :], gr[cf:cf + 1, :],
                c_scr[ch], m_scr[ch][0:1, 0:1], tri, tri_t, mask, reverse)
            h_ref[:, sl] = h
            c_scr[ch] = caug
            m_scr[ch] = jnp.broadcast_to(m_new, m_scr.shape[1:])

    @pl.when(c == nc - 1)
    def _():
        cfin_ref[0] = c_scr[...]
        mfin_ref[0] = m_scr[...]


def _mlstm_call(mq, mk, mv, gcol, grow, c0, m0, batch, seq):
    n = mq.shape[0]
    L = MLSTM_CHUNK
    nc = seq // L
    n_ch = 2 * M_HEADS

    def fwd(b, c):
        return (b * nc + c, 0)

    def bwd(b, c):
        return (b * nc + nc - 1 - c, 0)

    def fwd_t(b, c):
        return (0, b * nc + c)

    def bwd_t(b, c):
        return (0, b * nc + nc - 1 - c)

    tok = pl.BlockSpec((L, M_WIDTH), fwd)
    tok_b = pl.BlockSpec((L, M_WIDTH), bwd)
    in_specs = [tok, tok, tok, pl.BlockSpec((L, N_GATES), fwd), pl.BlockSpec((N_GATES, L), fwd_t),
                tok_b, tok_b, tok_b, pl.BlockSpec((L, N_GATES), bwd), pl.BlockSpec((N_GATES, L), bwd_t),
                pl.BlockSpec((1, n_ch, M_DH, 2 * M_DH), lambda b, c: (b, 0, 0, 0)),
                pl.BlockSpec((1, n_ch, 8, M_DH), lambda b, c: (b, 0, 0, 0))]
    out_specs = [tok, tok_b,
                 pl.BlockSpec((1, n_ch, M_DH, 2 * M_DH), lambda b, c: (b, 0, 0, 0)),
                 pl.BlockSpec((1, n_ch, 8, M_DH), lambda b, c: (b, 0, 0, 0))]
    out_shape = [jax.ShapeDtypeStruct((n, M_WIDTH), F32), jax.ShapeDtypeStruct((n, M_WIDTH), F32),
                 jax.ShapeDtypeStruct((batch, n_ch, M_DH, 2 * M_DH), F32),
                 jax.ShapeDtypeStruct((batch, n_ch, 8, M_DH), F32)]
    return pl.pallas_call(
        _mlstm_kernel, grid=(batch, nc), in_specs=in_specs, out_specs=out_specs, out_shape=out_shape,
        scratch_shapes=[pltpu.VMEM((n_ch, M_DH, 2 * M_DH), F32), pltpu.VMEM((n_ch, 8, M_DH), F32)],
        compiler_params=pltpu.CompilerParams(dimension_semantics=("parallel", "arbitrary")),
        name="mlstm",
    )(mq, mk, mv, gcol, grow, mq, mk, mv, gcol, grow, c0, m0)


def _sink_column(sink_ref, grp, rows_per_head):
    return jnp.concatenate(
        [jnp.full((rows_per_head, 1), sink_ref[grp * A_REP + r], F32) for r in range(A_REP)], axis=0)


def _store_heads(out_ref, o, grp, rows_per_head):
    for r in range(A_REP):
        hd = grp * A_REP + r
        out_ref[:, hd * A_DH:(hd + 1) * A_DH] = o[r * rows_per_head:(r + 1) * rows_per_head,
                                                  grp * A_DH:(grp + 1) * A_DH].astype(out_ref.dtype)


def _attn_ctx_kernel(sink_ref, q_ref, k_ref, v_ref, out_ref):
    s_len = k_ref.shape[0]
    k = k_ref[...].astype(BF16)
    v = v_ref[...].astype(BF16)
    for grp in range(A_KV):
        q = q_ref[grp * A_REP:(grp + 1) * A_REP].reshape(A_REP * s_len, KV_WIDTH)
        s = _dot_t(q, k)
        sk = _sink_column(sink_ref, grp, s_len)
        mx = jnp.maximum(jnp.max(s, axis=1, keepdims=True), sk)
        p = jnp.exp(s - mx)
        den = jnp.sum(p, axis=1, keepdims=True) + jnp.exp(sk - mx)
        o = jnp.dot(p.astype(BF16), v, preferred_element_type=F32) / den
        _store_heads(out_ref, o, grp, s_len)


def _attn_ctx_call(sink, aq, ak, av, batch, seq):
    n = ak.shape[0]
    return pl.pallas_call(
        _attn_ctx_kernel, grid=(batch,),
        in_specs=[pl.BlockSpec(memory_space=pltpu.SMEM),
                  pl.BlockSpec((A_HEADS, seq, KV_WIDTH), lambda b: (0, b, 0)),
                  pl.BlockSpec((seq, KV_WIDTH), lambda b: (b, 0)),
                  pl.BlockSpec((seq, KV_WIDTH), lambda b: (b, 0))],
        out_specs=pl.BlockSpec((seq, A_WIDTH), lambda b: (b, 0)),
        out_shape=jax.ShapeDtypeStruct((n, A_WIDTH), BF16),
        compiler_params=pltpu.CompilerParams(dimension_semantics=("parallel",)),
        name="attn_ctx",
    )(sink, aq, ak, av)


def _attn_lat_kernel(sink_ref, q_ref, kc_ref, vc_ref, kp_ref, kq_ref, kn_ref, vp_ref, vq_ref, vn_ref, out_ref):
    i = pl.program_id(1)
    nb = pl.num_programs(1)
    kc = kc_ref[0].astype(BF16)
    vc = vc_ref[0].astype(BF16)
    kp, kq, kn = kp_ref[...].astype(BF16), kq_ref[...].astype(BF16), kn_ref[...].astype(BF16)
    vp, vq, vn = vp_ref[...].astype(BF16), vq_ref[...].astype(BF16), vn_ref[...].astype(BF16)
    rows = A_REP * BLOCK
    qpos = lax.broadcasted_iota(I32, (rows, BLOCK), 0) % BLOCK
    kpos = lax.broadcasted_iota(I32, (rows, BLOCK), 1)
    mask_p = (kpos >= qpos) & (i > 0)
    mask_n = (kpos <= qpos) & (i < nb - 1)
    for grp in range(A_KV):
        q = q_ref[grp * A_REP:(grp + 1) * A_REP].reshape(rows, KV_WIDTH)
        s_c = _dot_t(q, kc)
        s_p = jnp.where(mask_p, _dot_t(q, kp), NEG)
        s_q = _dot_t(q, kq)
        s_n = jnp.where(mask_n, _dot_t(q, kn), NEG)
        sk = _sink_column(sink_ref, grp, BLOCK)
        mx = jnp.maximum(jnp.maximum(jnp.max(s_c, axis=1, keepdims=True), jnp.max(s_p, axis=1, keepdims=True)),
                         jnp.maximum(jnp.max(s_q, axis=1, keepdims=True), jnp.max(s_n, axis=1, keepdims=True)))
        mx = jnp.maximum(mx, sk)
        p_c, p_p, p_q, p_n = jnp.exp(s_c - mx), jnp.exp(s_p - mx), jnp.exp(s_q - mx), jnp.exp(s_n - mx)
        den = (jnp.sum(p_c, axis=1, keepdims=True) + jnp.sum(p_p, axis=1, keepdims=True)
               + jnp.sum(p_q, axis=1, keepdims=True) + jnp.sum(p_n, axis=1, keepdims=True) + jnp.exp(sk - mx))
        o = (jnp.dot(p_c.astype(BF16), vc, preferred_element_type=F32)
             + jnp.dot(p_p.astype(BF16), vp, preferred_element_type=F32)
             + jnp.dot(p_q.astype(BF16), vq, preferred_element_type=F32)
             + jnp.dot(p_n.astype(BF16), vn, preferred_element_type=F32)) / den
        _store_heads(out_ref, o, grp, BLOCK)


def _attn_lat_call(sink, aq, ak, av, kc, vc, batch, seq):
    n = ak.shape[0]
    nb = seq // BLOCK
    past = kc.shape[1]

    def cur(b, i):
        return (b * nb + i, 0)

    def prev(b, i):
        return (b * nb + jnp.maximum(i - 1, 0), 0)

    def nxt(b, i):
        return (b * nb + jnp.minimum(i + 1, nb - 1), 0)

    blk = functools.partial(pl.BlockSpec, (BLOCK, KV_WIDTH))
    cache = pl.BlockSpec((1, past, KV_WIDTH), lambda b, i: (b, 0, 0))
    return pl.pallas_call(
        _attn_lat_kernel, grid=(batch, nb),
        in_specs=[pl.BlockSpec(memory_space=pltpu.SMEM),
                  pl.BlockSpec((A_HEADS, BLOCK, KV_WIDTH), lambda b, i: (0, b * nb + i, 0)),
                  cache, cache, blk(prev), blk(cur), blk(nxt), blk(prev), blk(cur), blk(nxt)],
        out_specs=pl.BlockSpec((BLOCK, A_WIDTH), cur),
        out_shape=jax.ShapeDtypeStruct((n, A_WIDTH), BF16),
        compiler_params=pltpu.CompilerParams(dimension_semantics=("parallel", "parallel")),
        name="attn_lat",
    )(sink, aq, kc, vc, ak, ak, ak, av, av, av)


def _top16_rows(s, payload=None):
    n_rows = s.shape[0]
    rows = lax.broadcasted_iota(I32, s.shape, 0).astype(F32)
    vals, idxs, pays = [], [], []
    for _ in range(P_TOPK):
        mx = jnp.max(s, axis=0, keepdims=True)
        ix = jnp.min(jnp.where(s == mx, rows, float(n_rows)), axis=0, keepdims=True)
        hit = rows == ix
        vals.append(mx)
        idxs.append(ix)
        if payload is not None:
            pays.append(jnp.sum(jnp.where(hit, payload, 0.0), axis=0, keepdims=True))
        s = jnp.where(hit, -jnp.inf, s)
    out = (jnp.concatenate(vals, axis=0), jnp.concatenate(idxs, axis=0))
    if payload is not None:
        out += (jnp.concatenate(pays, axis=0),)
    return out


def _mix_kernel(x_ref, hf_ref, hb_ref, mo_ref, ao_ref, mod_ref, mhg_ref, n2_ref, wm_ref, wa_ref, wq_ref,
                sa_ref, sb_ref, x1_ref, h2_ref, eidx_ref, gate_ref, qp_scr, e_scr, g_scr):
    tm = x_ref.shape[0]
    hs = hf_ref[...] + hb_ref[...]
    parts = []
    for hd in range(M_HEADS):
        blk = hs[:, hd * M_DH:(hd + 1) * M_DH]
        parts.append(blk * lax.rsqrt(jnp.mean(blk * blk, axis=-1, keepdims=True) + EPS))
    m_out = _sigmoid(mo_ref[...]) * (jnp.concatenate(parts, axis=1) * mhg_ref[...])
    mix = (jnp.dot(m_out.astype(BF16), wm_ref[...], preferred_element_type=F32)
           + jnp.dot(ao_ref[...], wa_ref[...], preferred_element_type=F32))
    x1 = x_ref[...] + mod_ref[0, 2:3, :] * mix
    x1_ref[...] = x1
    h2 = x1 * lax.rsqrt(jnp.mean(x1 * x1, axis=-1, keepdims=True) + EPS) * n2_ref[...]
    h2 = h2 * (1.0 + mod_ref[0, 4:5, :]) + mod_ref[0, 3:4, :]
    h2_ref[...] = _pack_bf16_pairs(h2)
    qp = jnp.dot(h2.astype(BF16), wq_ref[...], preferred_element_type=F32)
    for p in range(P_HEADS):
        qp_scr[p] = qp[:, p * P_DKEY:(p + 1) * P_DKEY].astype(BF16)
    sub_a = sa_ref[...]
    sub_b = sb_ref[...]

    def head_body(p, carry):
        for half in range(tm // N_KEYS):
            cols = slice(half * N_KEYS, (half + 1) * N_KEYS)
            qh = qp_scr[p, pl.ds(half * N_KEYS, N_KEYS), :]
            s_a = _dot_t(sub_a, qh[:, 0:P_HALF])
            s_b = _dot_t(sub_b, qh[:, P_HALF:P_DKEY])
            va, ia = _top16_rows(s_a)
            vb, ib = _top16_rows(s_b)
            keep = [P_TOPK // (i + 1) for i in range(P_TOPK)]
            pad = -sum(keep) % 8
            cand = jnp.concatenate([va[i:i + 1, :] + vb[0:keep[i], :] for i in range(P_TOPK)]
                                   + [jnp.full((pad, N_KEYS), -jnp.inf, F32)], axis=0)
            cidx = jnp.concatenate([ia[i:i + 1, :] * float(N_KEYS) + ib[0:keep[i], :] for i in range(P_TOPK)]
                                   + [jnp.zeros((pad, N_KEYS), F32)], axis=0)
            top, _, eidx = _top16_rows(cand, cidx)
            ex = jnp.exp(top - jnp.max(top, axis=0, keepdims=True))
            gates = ex / jnp.sum(ex, axis=0, keepdims=True)
            r0 = pl.multiple_of(p * P_TOPK, P_TOPK)
            e_scr[pl.ds(r0, P_TOPK), cols] = eidx
            g_scr[pl.ds(r0, P_TOPK), cols] = gates
        return carry

    lax.fori_loop(0, P_HEADS, head_body, 0)
    for half in range(tm // N_KEYS):
        cols = slice(half * N_KEYS, (half + 1) * N_KEYS)
        eidx_ref[cols, :] = e_scr[:, cols].T.astype(I32)
        gate_ref[cols, :] = g_scr[:, cols].T


def _mix_call(x2d, seq, h_f, h_b, mo, a_out, mod3, mod_row0, mhg, n2, w_m, w_a, w_q, sub_a, sub_b):
    n = x2d.shape[0]
    tm = TOKEN_TILE
    per_seq = seq // tm

    def tok(i):
        return (i, 0)

    def const2(i):
        return (0, 0)

    in_specs = [
        pl.BlockSpec((tm, D_MODEL), tok),
        pl.BlockSpec((tm, M_WIDTH), tok), pl.BlockSpec((tm, M_WIDTH), tok), pl.BlockSpec((tm, M_WIDTH), tok),
        pl.BlockSpec((tm, A_WIDTH), tok),
        pl.BlockSpec((1, 6, D_MODEL), lambda i: (mod_row0 + (i // per_seq if mod_row0 else 0), 0, 0)),
        pl.BlockSpec((1, M_WIDTH), const2),
        pl.BlockSpec((1, D_MODEL), const2),
        pl.BlockSpec((M_WIDTH, D_MODEL), const2),
        pl.BlockSpec((A_WIDTH, D_MODEL), const2),
        pl.BlockSpec((D_MODEL, P_HEADS * P_DKEY), const2),
        pl.BlockSpec((N_KEYS, P_HALF), const2),
        pl.BlockSpec((N_KEYS, P_HALF), const2),
    ]
    out_specs = [pl.BlockSpec((tm, D_MODEL), tok), pl.BlockSpec((tm, D_MODEL // 2), tok),
                 pl.BlockSpec((tm, N_SEL), tok), pl.BlockSpec((tm, N_SEL), tok)]
    out_shape = [jax.ShapeDtypeStruct((n, D_MODEL), F32), jax.ShapeDtypeStruct((n, D_MODEL // 2), I32),
                 jax.ShapeDtypeStruct((n, N_SEL), I32), jax.ShapeDtypeStruct((n, N_SEL), F32)]
    return pl.pallas_call(
        _mix_kernel, grid=(n // tm,), in_specs=in_specs, out_specs=out_specs, out_shape=out_shape,
        scratch_shapes=[pltpu.VMEM((P_HEADS, tm, P_DKEY), BF16), pltpu.VMEM((N_SEL, tm), F32),
                        pltpu.VMEM((N_SEL, tm), F32)],
        compiler_params=pltpu.CompilerParams(dimension_semantics=("parallel",)),
        name="mix",
    )(x2d, h_f, h_b, mo, a_out, mod3, mhg, n2, w_m, w_a, w_q, sub_a, sub_b)


SC_LANES = 16
SC_CORES = 2
SC_SUBCORES = 16
SC_WORKERS = SC_CORES * SC_SUBCORES
SC_TOKENS = 16
SC_GROUP = SC_LANES
SC_NGROUPS = N_SEL // SC_GROUP
SC_SLOTS = 4
ROW_WORDS = D_MODEL // 2
SC_DOT_ROWS = 8
SC_DOT_PARTIALS = 2
SC_OWORDS = 16 * SC_LANES
HI_MASK = -65536
PACK_ROWS = 512
SCORE_TOKENS = 1024
SCORE_EXPERTS = 2048
DENSE_LATENT_CHUNKS = (1, 3, 4, 6, 7)
DENSE_CTX_CHUNKS = (4,)
GELU_C0 = 0.7978845608028654
GELU_C1 = 0.044715


def _pack_bf16_pairs(x):
    half = x.shape[1] // 2
    bits = lax.bitcast_convert_type(x.astype(BF16).astype(F32), I32)
    return (bits[:, :half] & HI_MASK) | lax.shift_right_logical(bits[:, half:], jnp.int32(16))


def _pack_tables_kernel(u_ref, v_ref, uv_ref, vp_ref, ub_ref):
    v_words = _pack_bf16_pairs(v_ref[...])
    uv_ref[:, 0:ROW_WORDS] = _pack_bf16_pairs(u_ref[...])
    uv_ref[:, ROW_WORDS:2 * ROW_WORDS] = v_words
    vp_ref[...] = v_words
    ub_ref[...] = u_ref[...].astype(BF16)


def _pack_tables_call(u_tab, v_tab):
    n_exp = u_tab.shape[0]
    blk = pl.BlockSpec((PACK_ROWS, D_MODEL), lambda i: (i, 0))
    half = pl.BlockSpec((PACK_ROWS, ROW_WORDS), lambda i: (i, 0))
    return pl.pallas_call(
        _pack_tables_kernel, grid=(n_exp // PACK_ROWS,), in_specs=[blk, blk], out_specs=[blk, half, blk],
        out_shape=[jax.ShapeDtypeStruct((n_exp, 2 * ROW_WORDS), I32), jax.ShapeDtypeStruct((n_exp, ROW_WORDS), I32),
                   jax.ShapeDtypeStruct((n_exp, D_MODEL), BF16)],
        compiler_params=pltpu.CompilerParams(dimension_semantics=("parallel",)),
        name="pack_tables",
    )(u_tab, v_tab)


def _unpack_bf16_pairs(words):
    hi = lax.bitcast_convert_type(words & HI_MASK, F32)
    lo = lax.bitcast_convert_type(lax.shift_left(words, jnp.int32(16)), F32)
    return jnp.concatenate([hi, lo], axis=1)


def _scores_kernel(x_ref, u_ref, o_ref):
    o_ref[...] = _pack_bf16_pairs(_dot_t(_unpack_bf16_pairs(x_ref[...]).astype(BF16), u_ref[...]))


def _scores_call(h2p, u_bf16):
    n = h2p.shape[0]
    n_exp = u_bf16.shape[0]
    tm = math.gcd(n, SCORE_TOKENS)
    return pl.pallas_call(
        _scores_kernel, grid=(n // tm, n_exp // SCORE_EXPERTS),
        in_specs=[pl.BlockSpec((tm, ROW_WORDS), lambda i, j: (i, 0)),
                  pl.BlockSpec((SCORE_EXPERTS, D_MODEL), lambda i, j: (j, 0))],
        out_specs=pl.BlockSpec((tm, SCORE_EXPERTS // 2), lambda i, j: (i, j)),
        out_shape=jax.ShapeDtypeStruct((n, n_exp // 2), I32),
        compiler_params=pltpu.CompilerParams(dimension_semantics=("parallel", "parallel")),
        name="expert_scores",
    )(h2p, u_bf16)


def _sc_gelu(a):
    z = GELU_C0 * (a + GELU_C1 * (a * a * a))
    tanh = 1.0 - 2.0 / (jnp.exp(2.0 * z) + 1.0)
    return 0.5 * a * (1.0 + tanh)


def _sc_split(words):
    return (plsc.bitcast(words & HI_MASK, F32), plsc.bitcast(lax.shift_left(words, jnp.int32(16)), F32))


def _sc_mul_bf16(a_words, b_words):
    return plsc.bitcast(a_words, BF16) * plsc.bitcast(b_words, BF16)


def _sc_split_sum(p, q):
    return _sc_split(plsc.bitcast(p + q, I32))


def _peer_sc_kernel(h2_hbm, eidx_hbm, gate_hbm, uv_hbm, out_hbm,
                    xbuf, ibuf, gbuf, obuf, uvbuf, mbuf, wbuf, sems, sems_in, sems_out):
    n = h2_hbm.shape[0]
    per_worker = n // SC_WORKERS
    blk_tokens = xbuf.shape[1]
    n_blocks = per_worker // blk_tokens
    wid = lax.axis_index("c") * SC_SUBCORES + lax.axis_index("s")
    lane = lax.iota(I32, SC_LANES)

    def split_item(item):
        return lax.shift_right_logical(item, SC_NGROUPS.bit_length() - 1), item & (SC_NGROUPS - 1)

    def gather_copies(bset, item, slot):
        t, g = split_item(item)
        idx = ibuf[bset, t, pl.ds(g * SC_GROUP, SC_GROUP)]
        return (pltpu.make_async_copy(uv_hbm.at[idx], uvbuf.at[slot], sems.at[slot]),)

    def dots(bset, t, slot):
        zero = jnp.zeros((SC_LANES,), F32)

        @pl.loop(0, SC_GROUP, step=SC_DOT_ROWS)
        def _(r0):
            accs = [[zero] * SC_DOT_PARTIALS for _ in range(SC_DOT_ROWS)]
            for k in range(0, ROW_WORDS // SC_LANES, 4):
                xs = [xbuf[bset, t, pl.ds((k + q) * SC_LANES, SC_LANES)] for q in range(4)]
                for i in range(SC_DOT_ROWS):
                    m = [_sc_mul_bf16(xs[q], uvbuf[slot, r0 + i, pl.ds((k + q) * SC_LANES, SC_LANES)])
                         for q in range(4)]
                    hi, lo = _sc_split_sum(m[0] + m[1], m[2] + m[3])
                    p = (k // 4) % SC_DOT_PARTIALS
                    accs[i][p] = accs[i][p] + (hi + lo)
            for i in range(SC_DOT_ROWS):
                mbuf[r0 + i, :] = functools.reduce(lambda a, b: a + b, accs[i])

        tot = zero
        for c in range(SC_LANES):
            tot = tot + plsc.load_gather(mbuf, [lane, jnp.full((SC_LANES,), c, I32)])
        return tot

    def accumulate(bset, t, slot):
        nv = SC_OWORDS // SC_LANES
        for oc in range(ROW_WORDS // SC_OWORDS):
            w0 = oc * SC_OWORDS
            accs = (tuple(obuf[bset, t, pl.ds(w0 + j * SC_LANES, SC_LANES)] for j in range(nv))
                    + tuple(obuf[bset, t, pl.ds(ROW_WORDS + w0 + j * SC_LANES, SC_LANES)] for j in range(nv)))

            def row_quad(rq, accs):
                r = 4 * rq
                ws = [plsc.load_gather(wbuf, [jnp.full((SC_LANES,), r + q, I32)]) for q in range(4)]
                his, los = [], []
                for j in range(nv):
                    m = [_sc_mul_bf16(ws[q], uvbuf[slot, r + q, pl.ds(ROW_WORDS + w0 + j * SC_LANES, SC_LANES)])
                         for q in range(4)]
                    hi, lo = _sc_split_sum(m[0] + m[1], m[2] + m[3])
                    his.append(accs[j] + hi)
                    los.append(accs[nv + j] + lo)
                return tuple(his) + tuple(los)

            accs = lax.fori_loop(0, SC_GROUP // 4, row_quad, accs)
            for j in range(nv):
                obuf[bset, t, pl.ds(w0 + j * SC_LANES, SC_LANES)] = accs[j]
                obuf[bset, t, pl.ds(ROW_WORDS + w0 + j * SC_LANES, SC_LANES)] = accs[nv + j]

    def pack_weights(w):
        bits = plsc.bitcast(w, I32)
        rounded = (bits + 0x7FFF + (lax.shift_right_logical(bits, jnp.int32(16)) & 1)) & HI_MASK
        return rounded | lax.shift_right_logical(rounded, jnp.int32(16))

    n_items = blk_tokens * SC_NGROUPS

    def block_rows(blk):
        return pl.ds(pl.multiple_of(wid * per_worker + blk * blk_tokens, blk_tokens), blk_tokens)

    def load_copies(blk, bset):
        rows = block_rows(blk)
        return (pltpu.make_async_copy(h2_hbm.at[rows], xbuf.at[bset], sems_in.at[bset]),
                pltpu.make_async_copy(eidx_hbm.at[rows], ibuf.at[bset], sems_in.at[bset]),
                pltpu.make_async_copy(gate_hbm.at[rows], gbuf.at[bset], sems_in.at[bset]))

    def store_copy(blk, bset):
        return pltpu.make_async_copy(obuf.at[bset], out_hbm.at[block_rows(blk)], sems_out.at[bset])

    for c in load_copies(0, 0):
        c.start()

    @pl.loop(0, n_blocks)
    def _(blk):
        bset = blk & 1
        for c in load_copies(blk, bset):
            c.wait()

        @pl.when(blk + 1 < n_blocks)
        def _():
            for c in load_copies(blk + 1, 1 - bset):
                c.start()

        @pl.when(blk >= 2)
        def _():
            store_copy(blk - 2, bset).wait()

        @pl.loop(0, blk_tokens)
        def _(t):
            zero = jnp.zeros((SC_LANES,), F32)
            for j in range(D_MODEL // SC_LANES):
                obuf[bset, t, pl.ds(j * SC_LANES, SC_LANES)] = zero

        for ahead in range(SC_SLOTS - 1):
            for c in gather_copies(bset, ahead, ahead):
                c.start()

        @pl.loop(0, n_items)
        def _(item):
            t, g = split_item(item)
            slot = item & (SC_SLOTS - 1)
            ahead = item + (SC_SLOTS - 1)

            @pl.when(ahead < n_items)
            def _():
                for c in gather_copies(bset, ahead, ahead & (SC_SLOTS - 1)):
                    c.start()

            for c in gather_copies(bset, item, slot):
                c.wait()
            a = dots(bset, t, slot)
            wbuf[...] = pack_weights(gbuf[bset, t, pl.ds(g * SC_GROUP, SC_GROUP)] * _sc_gelu(a))
            accumulate(bset, t, slot)

        store_copy(blk, bset).start()

    for blk in range(max(n_blocks - 2, 0), n_blocks):
        store_copy(blk, blk & 1).wait()


def _peer_experts(h2p, eidx, gates, uv_pack):
    n = h2p.shape[0]
    assert n % (8 * SC_WORKERS) == 0
    blk_tokens = math.gcd(n // SC_WORKERS, SC_TOKENS)
    mesh = plsc.VectorSubcoreMesh(core_axis_name="c", subcore_axis_name="s")
    fn = pl.kernel(
        _peer_sc_kernel,
        out_type=jax.ShapeDtypeStruct((n, D_MODEL), F32),
        mesh=mesh,
        scratch_types=[
            pltpu.VMEM((2, blk_tokens, ROW_WORDS), I32),
            pltpu.VMEM((2, blk_tokens, N_SEL), I32),
            pltpu.VMEM((2, blk_tokens, N_SEL), F32),
            pltpu.VMEM((2, blk_tokens, D_MODEL), F32),
            pltpu.VMEM((SC_SLOTS, SC_GROUP, 2 * ROW_WORDS), I32),
            pltpu.VMEM((SC_GROUP, SC_LANES), F32),
            pltpu.VMEM((SC_LANES,), I32),
            pltpu.SemaphoreType.DMA((SC_SLOTS,)), pltpu.SemaphoreType.DMA((2,)), pltpu.SemaphoreType.DMA((2,)),
        ],
        compiler_params=pltpu.CompilerParams(needs_layout_passes=False),
        cost_estimate=pl.CostEstimate(
            flops=4 * n * N_SEL * D_MODEL, transcendentals=n * N_SEL,
            bytes_accessed=4 * (2 * n * N_SEL * ROW_WORDS + n * ROW_WORDS + n * D_MODEL + 2 * n * N_SEL)),
        name="peer_experts",
    )
    return fn(h2p, eidx, gates, uv_pack)


def _peer_sc_scored_kernel(a_hbm, eidx_hbm, gate_hbm, v_hbm, out_hbm,
                           abuf, ibuf, gbuf, obuf, vbuf, wbuf, sems, sems_a):
    n = a_hbm.shape[0]
    per_worker = n // SC_WORKERS
    blk_tokens = ibuf.shape[0]
    wid = lax.axis_index("c") * SC_SUBCORES + lax.axis_index("s")

    def split_item(item):
        return lax.shift_right_logical(item, SC_NGROUPS.bit_length() - 1), item & (SC_NGROUPS - 1)

    def gather_copy(item, slot):
        t, g = split_item(item)
        idx = ibuf[t, pl.ds(g * SC_GROUP, SC_GROUP)]
        return pltpu.make_async_copy(v_hbm.at[idx], vbuf.at[slot], sems.at[slot])

    def accumulate(t, slot):
        nv = SC_OWORDS // SC_LANES
        for oc in range(ROW_WORDS // SC_OWORDS):
            w0 = oc * SC_OWORDS
            accs = (tuple(obuf[t, pl.ds(w0 + j * SC_LANES, SC_LANES)] for j in range(nv))
                    + tuple(obuf[t, pl.ds(ROW_WORDS + w0 + j * SC_LANES, SC_LANES)] for j in range(nv)))

            def row_quad(rq, accs):
                r = 4 * rq
                ws = [plsc.load_gather(wbuf, [jnp.full((SC_LANES,), r + q, I32)]) for q in range(4)]
                his, los = [], []
                for j in range(nv):
                    m = [_sc_mul_bf16(ws[q], vbuf[slot, r + q, pl.ds(w0 + j * SC_LANES, SC_LANES)])
                         for q in range(4)]
                    hi, lo = _sc_split_sum(m[0] + m[1], m[2] + m[3])
                    his.append(accs[j] + hi)
                    los.append(accs[nv + j] + lo)
                return tuple(his) + tuple(los)

            accs = lax.fori_loop(0, SC_GROUP // 4, row_quad, accs)
            for j in range(nv):
                obuf[t, pl.ds(w0 + j * SC_LANES, SC_LANES)] = accs[j]
                obuf[t, pl.ds(ROW_WORDS + w0 + j * SC_LANES, SC_LANES)] = accs[nv + j]

    def pack_weights(w):
        bits = plsc.bitcast(w, I32)
        rounded = (bits + 0x7FFF + (lax.shift_right_logical(bits, jnp.int32(16)) & 1)) & HI_MASK
        return rounded | lax.shift_right_logical(rounded, jnp.int32(16))

    n_items = blk_tokens * SC_NGROUPS

    @pl.loop(0, per_worker // blk_tokens)
    def _(blk):
        tok0 = pl.multiple_of(wid * per_worker + blk * blk_tokens, blk_tokens)

        def score_copy(t):
            return pltpu.make_async_copy(a_hbm.at[tok0 + t], abuf.at[t & 1], sems_a.at[t & 1])

        score_copy(0).start()
        pltpu.sync_copy(eidx_hbm.at[pl.ds(tok0, blk_tokens)], ibuf)
        pltpu.sync_copy(gate_hbm.at[pl.ds(tok0, blk_tokens)], gbuf)

        @pl.loop(0, blk_tokens)
        def _(t):
            zero = jnp.zeros((SC_LANES,), F32)
            for j in range(D_MODEL // SC_LANES):
                obuf[t, pl.ds(j * SC_LANES, SC_LANES)] = zero

        for ahead in range(SC_SLOTS - 1):
            gather_copy(ahead, ahead).start()

        @pl.loop(0, n_items)
        def _(item):
            t, g = split_item(item)
            slot = item & (SC_SLOTS - 1)
            ahead = item + (SC_SLOTS - 1)

            @pl.when(ahead < n_items)
            def _():
                gather_copy(ahead, ahead & (SC_SLOTS - 1)).start()

            @pl.when(g == 0)
            def _():
                score_copy(t).wait()

                @pl.when(t + 1 < blk_tokens)
                def _():
                    score_copy(t + 1).start()

            gather_copy(item, slot).wait()
            idx = ibuf[t, pl.ds(g * SC_GROUP, SC_GROUP)]
            half = SCORE_EXPERTS // 2
            word = plsc.load_gather(
                abuf, [jnp.full((SC_LANES,), t & 1, I32),
                       lax.shift_right_logical(idx, jnp.int32(SCORE_EXPERTS.bit_length() - 1)) * half
                       + (idx & (half - 1))])
            hi, lo = _sc_split(word)
            a = jnp.where((idx & half) == 0, hi, lo)
            wbuf[...] = pack_weights(gbuf[t, pl.ds(g * SC_GROUP, SC_GROUP)] * _sc_gelu(a))
            accumulate(t, slot)

        pltpu.sync_copy(obuf, out_hbm.at[pl.ds(tok0, blk_tokens)])


def _peer_experts_scored(scores, eidx, gates, v_pack):
    n, n_exp = scores.shape
    assert n % (8 * SC_WORKERS) == 0
    blk_tokens = math.gcd(n // SC_WORKERS, 2 * SC_TOKENS)
    mesh = plsc.VectorSubcoreMesh(core_axis_name="c", subcore_axis_name="s")
    fn = pl.kernel(
        _peer_sc_scored_kernel,
        out_type=jax.ShapeDtypeStruct((n, D_MODEL), F32),
        mesh=mesh,
        scratch_types=[
            pltpu.VMEM((2, n_exp), I32),
            pltpu.VMEM((blk_tokens, N_SEL), I32),
            pltpu.VMEM((blk_tokens, N_SEL), F32),
            pltpu.VMEM((blk_tokens, D_MODEL), F32),
            pltpu.VMEM((SC_SLOTS, SC_GROUP, ROW_WORDS), I32),
            pltpu.VMEM((SC_LANES,), I32),
            pltpu.SemaphoreType.DMA((SC_SLOTS,)), pltpu.SemaphoreType.DMA((2,)),
        ],
        compiler_params=pltpu.CompilerParams(needs_layout_passes=False),
        cost_estimate=pl.CostEstimate(
            flops=2 * n * N_SEL * D_MODEL, transcendentals=n * N_SEL,
            bytes_accessed=4 * (n * N_SEL * ROW_WORDS + n * n_exp + n * D_MODEL + 2 * n * N_SEL)),
        name="peer_experts_scored",
    )
    return fn(scores, eidx, gates, v_pack)


def _resid_kernel(x1_ref, p_ref, mod_ref, o_ref):
    o_ref[...] = x1_ref[...] + mod_ref[0, 5:6, :] * p_ref[...]


def _resid_call(x1, peer_out, seq, mod3, mod_row0):
    n = x1.shape[0]
    tm = TOKEN_TILE
    per_seq = seq // tm
    tok = pl.BlockSpec((tm, D_MODEL), lambda i: (i, 0))
    return pl.pallas_call(
        _resid_kernel, grid=(n // tm,),
        in_specs=[tok, tok,
                  pl.BlockSpec((1, 6, D_MODEL), lambda i: (mod_row0 + (i // per_seq if mod_row0 else 0), 0, 0))],
        out_specs=tok, out_shape=jax.ShapeDtypeStruct((n, D_MODEL), F32),
        compiler_params=pltpu.CompilerParams(dimension_semantics=("parallel",)),
        name="resid",
    )(x1, peer_out, mod3)


def _rope_tables(seq, rotate):
    if not rotate:
        return jnp.ones((seq, KV_WIDTH), F32), jnp.zeros((seq, KV_WIDTH), F32)
    quarter = A_DH // 4
    t = jnp.arange(seq)
    row = (t // GRID_W).astype(F32)
    col = (t % GRID_W).astype(F32)
    inv = ROPE_BASE ** (-jnp.arange(quarter, dtype=F32) / quarter)
    d = jnp.arange(A_DH)
    pos = jnp.where(d[None, :] < A_DH // 2, row[:, None], col[:, None])
    ang = pos * inv[d % quarter][None, :]
    sign = jnp.where((d % (A_DH // 2)) < quarter, -1.0, 1.0).astype(F32)
    cos = jnp.cos(ang)
    sin = jnp.sin(ang) * sign[None, :]
    return jnp.tile(cos, (1, KV_WIDTH // A_DH)), jnp.tile(sin, (1, KV_WIDTH // A_DH))


def _run_chunk(x, mod3, mod_row0, prm, cache, rotate, gate_on, dense_scores):
    (n1, n2, w_main, w_g, w_gt, b_g, b_gt, mhg, qg_t, kg_t, bd, sink, w_m, w_a, w_q, sub_a, sub_b,
     (uv_pack, v_pack, u_bf16)) = prm
    batch, seq, _ = x.shape
    n = batch * seq
    x2d = x.reshape(n, D_MODEL)
    if gate_on is not None:
        x2d, _ = lax.optimization_barrier((x2d, gate_on))
    cos, sin = _rope_tables(seq, rotate)
    mq, mk, mv, mo, gcol, grow, aq, ak, av = _inproj_call(
        x2d, seq, mod3, mod_row0, n1, w_main, w_g, w_gt, b_g, b_gt, qg_t, kg_t, bd, cos, sin)
    kc, vc, c0, m0 = cache
    h_f, h_b, c_fin, m_fin = _mlstm_call(mq, mk, mv, gcol, grow, c0, m0, batch, seq)
    if kc is None:
        a_out = _attn_ctx_call(sink, aq, ak, av, batch, seq)
    else:
        a_out = _attn_lat_call(sink, aq, ak, av, kc, vc, batch, seq)
    x1, h2p, eidx, gates = _mix_call(x2d, seq, h_f, h_b, mo, a_out, mod3, mod_row0, mhg, n2, w_m, w_a, w_q,
                                     sub_a, sub_b)
    if dense_scores:
        peer_out = _peer_experts_scored(_scores_call(h2p, u_bf16), eidx, gates, v_pack)
    else:
        peer_out = _peer_experts(h2p, eidx, gates, uv_pack)
    y = _resid_call(x1, peer_out, seq, mod3, mod_row0).reshape(batch, seq, D_MODEL)
    return y, h2p, peer_out, ak, av, c_fin, m_fin


def _pack_state(C, n_vec, m):
    b = C.shape[0]
    caug = jnp.concatenate([C, jnp.broadcast_to(n_vec[..., None], C.shape)], axis=-1)
    caug = caug.reshape(b, 2 * M_HEADS, M_DH, 2 * M_DH)
    m_rep = jnp.broadcast_to(m.reshape(b, 2 * M_HEADS, 1, 1), (b, 2 * M_HEADS, 8, M_DH))
    return caug.astype(F32), m_rep.astype(F32)


def kernel(x_prompt, x_sample, c, cache_attn_k, cache_attn_v, state_mlstm_C, state_mlstm_n, state_mlstm_m,
           c_ctx, w_ada, b_ada, norm1_g, norm2_g, w_in, b_gates, mh_norm_g, q_norm_g, k_norm_g, sink_logits,
           w_out, peer_w_q, peer_sub_a, peer_sub_b, peer_u, peer_v):
    depth = w_ada.shape[0]
    assert depth == 1
    batch, seq, _ = x_prompt.shape
    dec_batch, dec_seq, _ = x_sample.shape
    assert dec_batch + 1 <= MOD_ROWS and batch % sum(CTX_CHUNK_WEIGHTS) == 0 and dec_batch % LATENT_CHUNKS == 0
    l = 0

    cond = jnp.concatenate([c_ctx[None, :], c, jnp.zeros((MOD_ROWS - 1 - dec_batch, D_MODEL), F32)], axis=0)
    mod3 = _ada_call(cond, w_ada[l], b_ada[l]).reshape(MOD_ROWS, 6, D_MODEL)

    wi = w_in[l]
    g0 = 4 * M_WIDTH
    w_main = jnp.concatenate([wi[:, :g0], wi[:, g0 + N_GATES:]], axis=1).astype(BF16)
    w_g = wi[:, g0:g0 + N_GATES]
    seg = jnp.arange(A_WIDTH) // A_DH
    bd = jnp.where(seg[:, None] == seg[None, :], 1.0 / A_DH, 0.0).astype(F32)
    prm = (norm1_g[l][None, :], norm2_g[l][None, :], w_main, w_g, w_g.T, b_gates[l][None, :], b_gates[l][:, None],
           mh_norm_g[l][None, :], jnp.tile(q_norm_g[l], A_HEADS)[None, :], jnp.tile(k_norm_g[l], A_KV)[None, :], bd,
           sink_logits[l], w_out[l][:M_WIDTH].astype(BF16), w_out[l][M_WIDTH:].astype(BF16),
           peer_w_q[l].astype(BF16), peer_sub_a[l].astype(BF16), peer_sub_b[l].astype(BF16),
           _pack_tables_call(peer_u[l], peer_v[l]))

    zeros_c = jnp.zeros((batch, 2, M_HEADS, M_DH, M_DH), F32)
    c0, m0 = _pack_state(zeros_c, zeros_c[..., 0], jnp.full((batch, 2, M_HEADS), NEG, F32))
    c0s, m0s = _pack_state(state_mlstm_C[:, l], state_mlstm_n[:, l], state_mlstm_m[:, l])
    past = cache_attn_k.shape[2]
    kc = cache_attn_k[:, l].reshape(dec_batch, past, KV_WIDTH)
    vc = cache_attn_v[:, l].reshape(dec_batch, past, KV_WIDTH)

    jobs = []
    ctx_sizes = [batch * f // sum(CTX_CHUNK_WEIGHTS) for f in CTX_CHUNK_WEIGHTS]
    assert sum(ctx_sizes) == batch
    b0 = 0
    for ci, size in enumerate(ctx_sizes):
        b1 = b0 + size
        jobs.append((x_prompt[b0:b1], 0, (None, None, c0[b0:b1], m0[b0:b1]), False, ci in DENSE_CTX_CHUNKS))
        b0 = b1
    for ci, b0 in enumerate(range(0, dec_batch, dec_batch // LATENT_CHUNKS)):
        b1 = b0 + dec_batch // LATENT_CHUNKS
        jobs.append((x_sample[b0:b1], 1 + b0, (kc[b0:b1], vc[b0:b1], c0s[b0:b1], m0s[b0:b1]), True,
                     ci in DENSE_LATENT_CHUNKS))
    outs = []
    for i, (x_c, mod_row0, cache, rotate, dense) in enumerate(jobs):
        gate = tuple(g for g in (outs[i - 1][1] if i >= 1 else None,
                                 outs[i - EXPERT_LAG][2] if i >= EXPERT_LAG else None) if g is not None)
        outs.append(_run_chunk(x_c, mod3, mod_row0, prm, cache, rotate, gate or None, dense))
    ctx, lat = outs[:len(ctx_sizes)], outs[len(ctx_sizes):]
    y_p = jnp.concatenate([o[0] for o in ctx], axis=0)
    y_s = jnp.concatenate([o[0] for o in lat], axis=0)
    k_new = jnp.concatenate([o[3] for o in ctx], axis=0)
    v_new = jnp.concatenate([o[4] for o in ctx], axis=0)
    c_fin = jnp.concatenate([o[5] for o in ctx], axis=0)
    m_fin = jnp.concatenate([o[6] for o in ctx], axis=0)

    c_fin = c_fin.reshape(batch, 2, M_HEADS, M_DH, 2 * M_DH)
    new_c = c_fin[..., :M_DH][:, None]
    new_n = c_fin[..., M_DH][:, None]
    new_m = m_fin[:, :, 0, 0].reshape(batch, 2, M_HEADS)[:, None]
    new_k = k_new.reshape(batch, 1, seq, A_KV, A_DH)
    new_v = v_new.reshape(batch, 1, seq, A_KV, A_DH)
    return y_p, y_s, new_k, new_v, new_c, new_n, new_m
```

```python
import functools
import math

import jax
import jax.numpy as jnp
from jax import lax
from jax.experimental import pallas as pl
from jax.experimental.pallas import tpu as pltpu
from jax.experimental.pallas import tpu_sc as plsc

F32 = jnp.float32
BF16 = jnp.bfloat16
I32 = jnp.int32
HI = lax.Precision.HIGHEST

D_MODEL = 1024
EPS = 1e-6
NEG = -1e30
GRID_W = 64
M_HEADS = 4
M_WIDTH = 512
M_DH = 128
A_HEADS = 8
A_KV = 2
A_REP = 4
A_DH = 64
A_WIDTH = 512
KV_WIDTH = A_KV * A_DH
BLOCK = 128
ROPE_BASE = 10000.0
N_KEYS = 128
P_HEADS = 8
P_DKEY = 256
P_HALF = 128
P_TOPK = 16
N_SEL = P_HEADS * P_TOPK
N_GATES = 4 * M_HEADS
MAIN_COLS = 4 * M_WIDTH + A_WIDTH + 2 * KV_WIDTH
MOD_ROWS = 16

TOKEN_TILE = 256
MLSTM_CHUNK = 128
ADA_COL_TILE = 768
CTX_HEAD_WEIGHTS = (1, 2, 3)
CTX_TAIL_WEIGHTS = (4, 3, 2, 1)
EXPERT_LAG = 3
LATENT_CHUNKS = 8


def _sigmoid(x):
    return 1.0 / (1.0 + jnp.exp(-x))


def _log_sigmoid(x):
    return jnp.minimum(x, 0.0) - jnp.log1p(jnp.exp(-jnp.abs(x)))


def _dot_t(a, b, precision=None):
    return lax.dot_general(a, b, (((1,), (1,)), ((), ())), precision=precision,
                           preferred_element_type=F32)


def _ada_kernel(c_ref, w_ref, b_ref, o_ref):
    c = c_ref[...]
    s = c * _sigmoid(c)
    o_ref[...] = jnp.dot(s, w_ref[...], precision=HI, preferred_element_type=F32) + b_ref[...]


def _ada_call(cond, w_ada, b_ada):
    n_out = w_ada.shape[1]
    return pl.pallas_call(
        _ada_kernel,
        grid=(n_out // ADA_COL_TILE,),
        in_specs=[pl.BlockSpec((MOD_ROWS, D_MODEL), lambda j: (0, 0)),
                  pl.BlockSpec((D_MODEL, ADA_COL_TILE), lambda j: (0, j)),
                  pl.BlockSpec((1, ADA_COL_TILE), lambda j: (0, j))],
        out_specs=pl.BlockSpec((MOD_ROWS, ADA_COL_TILE), lambda j: (0, j)),
        out_shape=jax.ShapeDtypeStruct((MOD_ROWS, n_out), F32),
        name="ada",
    )(cond, w_ada, b_ada.reshape(1, n_out))


def _swap16(x):
    n = x.shape[-1]
    lane = lax.broadcasted_iota(I32, x.shape, x.ndim - 1)
    return jnp.where((lane & 16) == 0, pltpu.roll(x, n - 16, x.ndim - 1), pltpu.roll(x, 16, x.ndim - 1))


def _inproj_kernel(x_ref, mod_ref, n1_ref, w_ref, wg_ref, wgt_ref, bg_ref, bgt_ref, qg_ref, kg_ref,
                   bd_ref, cos_ref, sin_ref,
                   mq_ref, mk_ref, mv_ref, mo_ref, gc_ref, gr_ref, aq_ref, ak_ref, av_ref):
    x = x_ref[...]
    h = x * lax.rsqrt(jnp.mean(x * x, axis=-1, keepdims=True) + EPS) * n1_ref[...]
    h = h * (1.0 + mod_ref[0, 1:2, :]) + mod_ref[0, 0:1, :]
    z = jnp.dot(h.astype(BF16), w_ref[...], preferred_element_type=F32)

    mq_ref[...] = (z[:, 0:M_WIDTH] * (M_DH ** -0.5)).astype(BF16)
    mk_ref[...] = z[:, M_WIDTH:2 * M_WIDTH].astype(BF16)
    mv_ref[...] = z[:, 2 * M_WIDTH:3 * M_WIDTH].astype(BF16)
    mo_ref[...] = z[:, 3 * M_WIDTH:4 * M_WIDTH]

    g = jnp.dot(h, wg_ref[...], precision=HI, preferred_element_type=F32) + bg_ref[...]
    kind = lax.broadcasted_iota(I32, g.shape, 1) // M_HEADS
    gc_ref[...] = jnp.where((kind & 1) == 1, _log_sigmoid(g), g)
    gt = _dot_t(wgt_ref[...], h, precision=HI) + bgt_ref[...]
    kind_t = lax.broadcasted_iota(I32, gt.shape, 0) // M_HEADS
    gr_ref[...] = jnp.where((kind_t & 1) == 1, _log_sigmoid(gt), gt)

    o = 4 * M_WIDTH
    aq = z[:, o:o + A_WIDTH]
    ak = z[:, o + A_WIDTH:o + A_WIDTH + KV_WIDTH]
    av_ref[...] = z[:, o + A_WIDTH + KV_WIDTH:o + A_WIDTH + 2 * KV_WIDTH]
    bd = bd_ref[...]
    cos = cos_ref[...]
    sin = sin_ref[...]
    aq = aq * lax.rsqrt(jnp.dot(aq * aq, bd, precision=HI, preferred_element_type=F32) + EPS) * qg_ref[...]
    cos4 = jnp.concatenate([cos] * (A_WIDTH // KV_WIDTH), axis=1)
    sin4 = jnp.concatenate([sin] * (A_WIDTH // KV_WIDTH), axis=1)
    aq = (aq * cos4 + _swap16(aq) * sin4) * (A_DH ** -0.5)
    ak = ak * lax.rsqrt(jnp.dot(ak * ak, bd[0:KV_WIDTH, 0:KV_WIDTH], precision=HI,
                                preferred_element_type=F32) + EPS) * kg_ref[...]
    ak_ref[...] = ak * cos + _swap16(ak) * sin

    lane = lax.broadcasted_iota(I32, (aq.shape[0], KV_WIDTH), 1)
    for hd in range(A_HEADS):
        grp = hd // A_REP
        blk = aq[:, (hd // 2) * KV_WIDTH:(hd // 2 + 1) * KV_WIDTH]
        if hd % 2 != grp:
            blk = pltpu.roll(blk, A_DH, 1)
        keep = (lane >= grp * A_DH) & (lane < (grp + 1) * A_DH)
        aq_ref[hd] = jnp.where(keep, blk, 0.0).astype(BF16)


def _inproj_call(x2d, seq, mod3, mod_row0, n1, w_main, w_g, w_gt, b_g, b_gt, qg_t, kg_t, bd, cos, sin):
    n = x2d.shape[0]
    tm = TOKEN_TILE
    per_seq = seq // tm

    def tok(i):
        return (i, 0)

    def const2(i):
        return (0, 0)

    in_specs = [
        pl.BlockSpec((tm, D_MODEL), tok),
        pl.BlockSpec((1, 6, D_MODEL), lambda i: (mod_row0 + (i // per_seq if mod_row0 else 0), 0, 0)),
        pl.BlockSpec((1, D_MODEL), const2),
        pl.BlockSpec((D_MODEL, MAIN_COLS), const2),
        pl.BlockSpec((D_MODEL, N_GATES), const2),
        pl.BlockSpec((N_GATES, D_MODEL), const2),
        pl.BlockSpec((1, N_GATES), const2),
        pl.BlockSpec((N_GATES, 1), const2),
        pl.BlockSpec((1, A_WIDTH), const2),
        pl.BlockSpec((1, KV_WIDTH), const2),
        pl.BlockSpec((A_WIDTH, A_WIDTH), const2),
        pl.BlockSpec((tm, KV_WIDTH), lambda i: (i % per_seq, 0)),
        pl.BlockSpec((tm, KV_WIDTH), lambda i: (i % per_seq, 0)),
    ]
    out_specs = [
        pl.BlockSpec((tm, M_WIDTH), tok),
        pl.BlockSpec((tm, M_WIDTH), tok),
        pl.BlockSpec((tm, M_WIDTH), tok),
        pl.BlockSpec((tm, M_WIDTH), tok),
        pl.BlockSpec((tm, N_GATES), tok),
        pl.BlockSpec((N_GATES, tm), lambda i: (0, i)),
        pl.BlockSpec((A_HEADS, tm, KV_WIDTH), lambda i: (0, i, 0)),
        pl.BlockSpec((tm, KV_WIDTH), tok),
        pl.BlockSpec((tm, KV_WIDTH), tok),
    ]
    out_shape = [
        jax.ShapeDtypeStruct((n, M_WIDTH), BF16),
        jax.ShapeDtypeStruct((n, M_WIDTH), BF16),
        jax.ShapeDtypeStruct((n, M_WIDTH), BF16),
        jax.ShapeDtypeStruct((n, M_WIDTH), F32),
        jax.ShapeDtypeStruct((n, N_GATES), F32),
        jax.ShapeDtypeStruct((N_GATES, n), F32),
        jax.ShapeDtypeStruct((A_HEADS, n, KV_WIDTH), BF16),
        jax.ShapeDtypeStruct((n, KV_WIDTH), F32),
        jax.ShapeDtypeStruct((n, KV_WIDTH), F32),
    ]
    return pl.pallas_call(
        _inproj_kernel, grid=(n // tm,), in_specs=in_specs, out_specs=out_specs, out_shape=out_shape,
        compiler_params=pltpu.CompilerParams(dimension_semantics=("parallel",)),
        name="inproj",
    )(x2d, mod3, n1, w_main, w_g, w_gt, b_g, b_gt, qg_t, kg_t, bd, cos, sin)


def _mlstm_chain(q, k, v, li_c, lf_c, li_r, lf_r, caug, m, tri, tri_t, mask, reverse):
    L = q.shape[0]
    last = 0 if reverse else L - 1
    b_c = jnp.dot(tri, jnp.broadcast_to(lf_c, (L, L)), precision=HI, preferred_element_type=F32)
    b_r = jnp.dot(jnp.broadcast_to(lf_r, (8, L)), tri_t, precision=HI, preferred_element_type=F32)[0:1, :]
    a_inter = b_c[:, 0:1] + m
    d = jnp.where(mask, b_c - b_r + li_r, -jnp.inf)
    m_t = jnp.maximum(a_inter, jnp.max(d, axis=1, keepdims=True))
    w_inter = jnp.exp(a_inter - m_t)
    s = _dot_t(q, k) * jnp.exp(d - m_t)
    qc = jnp.dot(q, caug.astype(BF16), preferred_element_type=F32)
    num = jnp.dot(s.astype(BF16), v, preferred_element_type=F32) + w_inter * qc[:, 0:M_DH]
    den = jnp.sum(s, axis=1, keepdims=True) + w_inter * qc[:, M_DH:M_DH + 1]
    den = jnp.maximum(jnp.abs(den), jnp.exp(-m_t))
    h = num / den
    m_new = m_t[last:last + 1, :]
    b_last = b_c[last:last + 1, 0:1]
    g_c = jnp.exp(b_last - b_c[:, 0:1] + li_c - m_new)
    decay = jnp.exp(b_last + m - m_new)
    kw = (k.astype(F32) * g_c).astype(BF16)
    vaug = jnp.concatenate([v, jnp.ones_like(v)], axis=1)
    upd = lax.dot_general(kw, vaug, (((0,), (0,)), ((), ())), preferred_element_type=F32)
    return h, decay * caug + upd, m_new


def _mlstm_kernel(qf_ref, kf_ref, vf_ref, gcf_ref, grf_ref, qb_ref, kb_ref, vb_ref, gcb_ref, grb_ref,
                  c0_ref, m0_ref, hf_ref, hb_ref, cfin_ref, mfin_ref, c_scr, m_scr):
    c = pl.program_id(1)
    nc = pl.num_programs(1)
    L = qf_ref.shape[0]

    @pl.when(c == 0)
    def _():
        c_scr[...] = c0_ref[0]
        m_scr[...] = m0_ref[0]

    row = lax.broadcasted_iota(I32, (L, L), 0)
    col = lax.broadcasted_iota(I32, (L, L), 1)
    lower = row >= col
    upper = row <= col
    lower_f = lower.astype(F32)
    upper_f = upper.astype(F32)

    for direction in range(2):
        reverse = direction == 1
        q_ref, k_ref, v_ref, gc_ref, gr_ref, h_ref = (
            (qb_ref, kb_ref, vb_ref, gcb_ref, grb_ref, hb_ref) if reverse
            else (qf_ref, kf_ref, vf_ref, gcf_ref, grf_ref, hf_ref))
        tri, tri_t, mask = (upper_f, lower_f, upper) if reverse else (lower_f, upper_f, lower)
        gc = gc_ref[...]
        gr = gr_ref[...]
        for hd in range(M_HEADS):
            ch = direction * M_HEADS + hd
            sl = slice(hd * M_DH, (hd + 1) * M_DH)
            ci = 2 * direction * M_HEADS + hd
            cf = ci + M_HEADS
            h, caug, m_new = _mlstm_chain(
                q_ref[:, sl], k_ref[:, sl], v_ref[:, sl],
                gc[:, ci:ci + 1], gc[:, cf:cf + 1], gr[ci:ci + 1, :], gr[cf:cf + 1, :],
                c_scr[ch], m_scr[ch][0:1, 0:1], tri, tri_t, mask, reverse)
            h_ref[:, sl] = h
            c_scr[ch] = caug
            m_scr[ch] = jnp.broadcast_to(m_new, m_scr.shape[1:])

    @pl.when(c == nc - 1)
    def _():
        cfin_ref[0] = c_scr[...]
        mfin_ref[0] = m_scr[...]


def _mlstm_call(mq, mk, mv, gcol, grow, c0, m0, batch, seq):
    n = mq.shape[0]
    L = MLSTM_CHUNK
    nc = seq // L
    n_ch = 2 * M_HEADS

    def fwd(b, c):
        return (b * nc + c, 0)

    def bwd(b, c):
        return (b * nc + nc - 1 - c, 0)

    def fwd_t(b, c):
        return (0, b * nc + c)

    def bwd_t(b, c):
        return (0, b * nc + nc - 1 - c)

    tok = pl.BlockSpec((L, M_WIDTH), fwd)
    tok_b = pl.BlockSpec((L, M_WIDTH), bwd)
    in_specs = [tok, tok, tok, pl.BlockSpec((L, N_GATES), fwd), pl.BlockSpec((N_GATES, L), fwd_t),
                tok_b, tok_b, tok_b, pl.BlockSpec((L, N_GATES), bwd), pl.BlockSpec((N_GATES, L), bwd_t),
                pl.BlockSpec((1, n_ch, M_DH, 2 * M_DH), lambda b, c: (b, 0, 0, 0)),
                pl.BlockSpec((1, n_ch, 8, M_DH), lambda b, c: (b, 0, 0, 0))]
    out_specs = [tok, tok_b,
                 pl.BlockSpec((1, n_ch, M_DH, 2 * M_DH), lambda b, c: (b, 0, 0, 0)),
                 pl.BlockSpec((1, n_ch, 8, M_DH), lambda b, c: (b, 0, 0, 0))]
    out_shape = [jax.ShapeDtypeStruct((n, M_WIDTH), F32), jax.ShapeDtypeStruct((n, M_WIDTH), F32),
                 jax.ShapeDtypeStruct((batch, n_ch, M_DH, 2 * M_DH), F32),
                 jax.ShapeDtypeStruct((batch, n_ch, 8, M_DH), F32)]
    return pl.pallas_call(
        _mlstm_kernel, grid=(batch, nc), in_specs=in_specs, out_specs=out_specs, out_shape=out_shape,
        scratch_shapes=[pltpu.VMEM((n_ch, M_DH, 2 * M_DH), F32), pltpu.VMEM((n_ch, 8, M_DH), F32)],
        compiler_params=pltpu.CompilerParams(dimension_semantics=("parallel", "arbitrary")),
        name="mlstm",
    )(mq, mk, mv, gcol, grow, mq, mk, mv, gcol, grow, c0, m0)


def _sink_column(sink_ref, grp, rows_per_head):
    return jnp.concatenate(
        [jnp.full((rows_per_head, 1), sink_ref[grp * A_REP + r], F32) for r in range(A_REP)], axis=0)


def _store_heads(out_ref, o, grp, rows_per_head):
    for r in range(A_REP):
        hd = grp * A_REP + r
        out_ref[:, hd * A_DH:(hd + 1) * A_DH] = o[r * rows_per_head:(r + 1) * rows_per_head,
                                                  grp * A_DH:(grp + 1) * A_DH].astype(out_ref.dtype)


def _attn_ctx_kernel(sink_ref, q_ref, k_ref, v_ref, out_ref):
    s_len = k_ref.shape[0]
    k = k_ref[...].astype(BF16)
    v = v_ref[...].astype(BF16)
    for grp in range(A_KV):
        q = q_ref[grp * A_REP:(grp + 1) * A_REP].reshape(A_REP * s_len, KV_WIDTH)
        s = _dot_t(q, k)
        sk = _sink_column(sink_ref, grp, s_len)
        mx = jnp.maximum(jnp.max(s, axis=1, keepdims=True), sk)
        p = jnp.exp(s - mx)
        den = jnp.sum(p, axis=1, keepdims=True) + jnp.exp(sk - mx)
        o = jnp.dot(p.astype(BF16), v, preferred_element_type=F32) / den
        _store_heads(out_ref, o, grp, s_len)


def _attn_ctx_call(sink, aq, ak, av, batch, seq):
    n = ak.shape[0]
    return pl.pallas_call(
        _attn_ctx_kernel, grid=(batch,),
        in_specs=[pl.BlockSpec(memory_space=pltpu.SMEM),
                  pl.BlockSpec((A_HEADS, seq, KV_WIDTH), lambda b: (0, b, 0)),
                  pl.BlockSpec((seq, KV_WIDTH), lambda b: (b, 0)),
                  pl.BlockSpec((seq, KV_WIDTH), lambda b: (b, 0))],
        out_specs=pl.BlockSpec((seq, A_WIDTH), lambda b: (b, 0)),
        out_shape=jax.ShapeDtypeStruct((n, A_WIDTH), BF16),
        compiler_params=pltpu.CompilerParams(dimension_semantics=("parallel",)),
        name="attn_ctx",
    )(sink, aq, ak, av)


def _attn_lat_kernel(sink_ref, q_ref, kc_ref, vc_ref, kp_ref, kq_ref, kn_ref, vp_ref, vq_ref, vn_ref, out_ref):
    i = pl.program_id(1)
    nb = pl.num_programs(1)
    kc = kc_ref[0].astype(BF16)
    vc = vc_ref[0].astype(BF16)
    kp, kq, kn = kp_ref[...].astype(BF16), kq_ref[...].astype(BF16), kn_ref[...].astype(BF16)
    vp, vq, vn = vp_ref[...].astype(BF16), vq_ref[...].astype(BF16), vn_ref[...].astype(BF16)
    rows = A_REP * BLOCK
    qpos = lax.broadcasted_iota(I32, (rows, BLOCK), 0) % BLOCK
    kpos = lax.broadcasted_iota(I32, (rows, BLOCK), 1)
    mask_p = (kpos >= qpos) & (i > 0)
    mask_n = (kpos <= qpos) & (i < nb - 1)
    for grp in range(A_KV):
        q = q_ref[grp * A_REP:(grp + 1) * A_REP].reshape(rows, KV_WIDTH)
        s_c = _dot_t(q, kc)
        s_p = jnp.where(mask_p, _dot_t(q, kp), NEG)
        s_q = _dot_t(q, kq)
        s_n = jnp.where(mask_n, _dot_t(q, kn), NEG)
        sk = _sink_column(sink_ref, grp, BLOCK)
        mx = jnp.maximum(jnp.maximum(jnp.max(s_c, axis=1, keepdims=True), jnp.max(s_p, axis=1, keepdims=True)),
                         jnp.maximum(jnp.max(s_q, axis=1, keepdims=True), jnp.max(s_n, axis=1, keepdims=True)))
        mx = jnp.maximum(mx, sk)
        p_c, p_p, p_q, p_n = jnp.exp(s_c - mx), jnp.exp(s_p - mx), jnp.exp(s_q - mx), jnp.exp(s_n - mx)
        den = (jnp.sum(p_c, axis=1, keepdims=True) + jnp.sum(p_p, axis=1, keepdims=True)
               + jnp.sum(p_q, axis=1, keepdims=True) + jnp.sum(p_n, axis=1, keepdims=True) + jnp.exp(sk - mx))
        o = (jnp.dot(p_c.astype(BF16), vc, preferred_element_type=F32)
             + jnp.dot(p_p.astype(BF16), vp, preferred_element_type=F32)
             + jnp.dot(p_q.astype(BF16), vq, preferred_element_type=F32)
             + jnp.dot(p_n.astype(BF16), vn, preferred_element_type=F32)) / den
        _store_heads(out_ref, o, grp, BLOCK)


def _attn_lat_call(sink, aq, ak, av, kc, vc, batch, seq):
    n = ak.shape[0]
    nb = seq // BLOCK
    past = kc.shape[1]

    def cur(b, i):
        return (b * nb + i, 0)

    def prev(b, i):
        return (b * nb + jnp.maximum(i - 1, 0), 0)

    def nxt(b, i):
        return (b * nb + jnp.minimum(i + 1, nb - 1), 0)

    blk = functools.partial(pl.BlockSpec, (BLOCK, KV_WIDTH))
    cache = pl.BlockSpec((1, past, KV_WIDTH), lambda b, i: (b, 0, 0))
    return pl.pallas_call(
        _attn_lat_kernel, grid=(batch, nb),
        in_specs=[pl.BlockSpec(memory_space=pltpu.SMEM),
                  pl.BlockSpec((A_HEADS, BLOCK, KV_WIDTH), lambda b, i: (0, b * nb + i, 0)),
                  cache, cache, blk(prev), blk(cur), blk(nxt), blk(prev), blk(cur), blk(nxt)],
        out_specs=pl.BlockSpec((BLOCK, A_WIDTH), cur),
        out_shape=jax.ShapeDtypeStruct((n, A_WIDTH), BF16),
        compiler_params=pltpu.CompilerParams(dimension_semantics=("parallel", "parallel")),
        name="attn_lat",
    )(sink, aq, kc, vc, ak, ak, ak, av, av, av)


def _top16_rows(s, payload=None):
    n_rows = s.shape[0]
    rows = lax.broadcasted_iota(I32, s.shape, 0).astype(F32)
    vals, idxs, pays = [], [], []
    for _ in range(P_TOPK):
        mx = jnp.max(s, axis=0, keepdims=True)
        ix = jnp.min(jnp.where(s == mx, rows, float(n_rows)), axis=0, keepdims=True)
        hit = rows == ix
        vals.append(mx)
        idxs.append(ix)
        if payload is not None:
            pays.append(jnp.sum(jnp.where(hit, payload, 0.0), axis=0, keepdims=True))
        s = jnp.where(hit, -jnp.inf, s)
    out = (jnp.concatenate(vals, axis=0), jnp.concatenate(idxs, axis=0))
    if payload is not None:
        out += (jnp.concatenate(pays, axis=0),)
    return out


def _mix_kernel(x_ref, hf_ref, hb_ref, mo_ref, ao_ref, mod_ref, mhg_ref, n2_ref, wm_ref, wa_ref, wq_ref,
                sa_ref, sb_ref, x1_ref, h2_ref, eidx_ref, gate_ref, qp_scr, e_scr, g_scr):
    tm = x_ref.shape[0]
    hs = hf_ref[...] + hb_ref[...]
    parts = []
    for hd in range(M_HEADS):
        blk = hs[:, hd * M_DH:(hd + 1) * M_DH]
        parts.append(blk * lax.rsqrt(jnp.mean(blk * blk, axis=-1, keepdims=True) + EPS))
    m_out = _sigmoid(mo_ref[...]) * (jnp.concatenate(parts, axis=1) * mhg_ref[...])
    mix = (jnp.dot(m_out.astype(BF16), wm_ref[...], preferred_element_type=F32)
           + jnp.dot(ao_ref[...], wa_ref[...], preferred_element_type=F32))
    x1 = x_ref[...] + mod_ref[0, 2:3, :] * mix
    x1_ref[...] = x1
    h2 = x1 * lax.rsqrt(jnp.mean(x1 * x1, axis=-1, keepdims=True) + EPS) * n2_ref[...]
    h2 = h2 * (1.0 + mod_ref[0, 4:5, :]) + mod_ref[0, 3:4, :]
    h2_ref[...] = _pack_bf16_pairs(h2)
    qp = jnp.dot(h2.astype(BF16), wq_ref[...], preferred_element_type=F32)
    for p in range(P_HEADS):
        qp_scr[p] = qp[:, p * P_DKEY:(p + 1) * P_DKEY].astype(BF16)
    sub_a = sa_ref[...]
    sub_b = sb_ref[...]

    def head_body(p, carry):
        for half in range(tm // N_KEYS):
            cols = slice(half * N_KEYS, (half + 1) * N_KEYS)
            qh = qp_scr[p, pl.ds(half * N_KEYS, N_KEYS), :]
            s_a = _dot_t(sub_a, qh[:, 0:P_HALF])
            s_b = _dot_t(sub_b, qh[:, P_HALF:P_DKEY])
            va, ia = _top16_rows(s_a)
            vb, ib = _top16_rows(s_b)
            keep = [P_TOPK // (i + 1) for i in range(P_TOPK)]
            pad = -sum(keep) % 8
            cand = jnp.concatenate([va[i:i + 1, :] + vb[0:keep[i], :] for i in range(P_TOPK)]
                                   + [jnp.full((pad, N_KEYS), -jnp.inf, F32)], axis=0)
            cidx = jnp.concatenate([ia[i:i + 1, :] * float(N_KEYS) + ib[0:keep[i], :] for i in range(P_TOPK)]
                                   + [jnp.zeros((pad, N_KEYS), F32)], axis=0)
            top, _, eidx = _top16_rows(cand, cidx)
            ex = jnp.exp(top - jnp.max(top, axis=0, keepdims=True))
            gates = ex / jnp.sum(ex, axis=0, keepdims=True)
            r0 = pl.multiple_of(p * P_TOPK, P_TOPK)
            e_scr[pl.ds(r0, P_TOPK), cols] = eidx
            g_scr[pl.ds(r0, P_TOPK), cols] = gates
        return carry

    lax.fori_loop(0, P_HEADS, head_body, 0)
    for half in range(tm // N_KEYS):
        cols = slice(half * N_KEYS, (half + 1) * N_KEYS)
        eidx_ref[cols, :] = e_scr[:, cols].T.astype(I32)
        gate_ref[cols, :] = g_scr[:, cols].T


def _mix_call(x2d, seq, h_f, h_b, mo, a_out, mod3, mod_row0, mhg, n2, w_m, w_a, w_q, sub_a, sub_b):
    n = x2d.shape[0]
    tm = TOKEN_TILE
    per_seq = seq // tm

    def tok(i):
        return (i, 0)

    def const2(i):
        return (0, 0)

    in_specs = [
        pl.BlockSpec((tm, D_MODEL), tok),
        pl.BlockSpec((tm, M_WIDTH), tok), pl.BlockSpec((tm, M_WIDTH), tok), pl.BlockSpec((tm, M_WIDTH), tok),
        pl.BlockSpec((tm, A_WIDTH), tok),
        pl.BlockSpec((1, 6, D_MODEL), lambda i: (mod_row0 + (i // per_seq if mod_row0 else 0), 0, 0)),
        pl.BlockSpec((1, M_WIDTH), const2),
        pl.BlockSpec((1, D_MODEL), const2),
        pl.BlockSpec((M_WIDTH, D_MODEL), const2),
        pl.BlockSpec((A_WIDTH, D_MODEL), const2),
        pl.BlockSpec((D_MODEL, P_HEADS * P_DKEY), const2),
        pl.BlockSpec((N_KEYS, P_HALF), const2),
        pl.BlockSpec((N_KEYS, P_HALF), const2),
    ]
    out_specs = [pl.BlockSpec((tm, D_MODEL), tok), pl.BlockSpec((tm, D_MODEL // 2), tok),
                 pl.BlockSpec((tm, N_SEL), tok), pl.BlockSpec((tm, N_SEL), tok)]
    out_shape = [jax.ShapeDtypeStruct((n, D_MODEL), F32), jax.ShapeDtypeStruct((n, D_MODEL // 2), I32),
                 jax.ShapeDtypeStruct((n, N_SEL), I32), jax.ShapeDtypeStruct((n, N_SEL), F32)]
    return pl.pallas_call(
        _mix_kernel, grid=(n // tm,), in_specs=in_specs, out_specs=out_specs, out_shape=out_shape,
        scratch_shapes=[pltpu.VMEM((P_HEADS, tm, P_DKEY), BF16), pltpu.VMEM((N_SEL, tm), F32),
                        pltpu.VMEM((N_SEL, tm), F32)],
        compiler_params=pltpu.CompilerParams(dimension_semantics=("parallel",)),
        name="mix",
    )(x2d, h_f, h_b, mo, a_out, mod3, mhg, n2, w_m, w_a, w_q, sub_a, sub_b)


SC_LANES = 16
SC_CORES = 2
SC_SUBCORES = 16
SC_WORKERS = SC_CORES * SC_SUBCORES
SC_TOKENS = 16
SC_GROUP = SC_LANES
SC_NGROUPS = N_SEL // SC_GROUP
SC_SLOTS = 4
ROW_WORDS = D_MODEL // 2
SC_DOT_ROWS = 8
SC_DOT_PARTIALS = 2
SC_OWORDS = 16 * SC_LANES
HI_MASK = -65536
PACK_ROWS = 512
SCORE_TOKENS = 1024
SCORE_EXPERTS = 2048
DENSE_LATENT_CHUNKS = (1, 3, 4, 6, 7)
GELU_C0 = 0.7978845608028654
GELU_C1 = 0.044715


def _pack_bf16_pairs(x):
    half = x.shape[1] // 2
    bits = lax.bitcast_convert_type(x.astype(BF16).astype(F32), I32)
    return (bits[:, :half] & HI_MASK) | lax.shift_right_logical(bits[:, half:], jnp.int32(16))


def _pack_tables_kernel(u_ref, v_ref, uv_ref, vp_ref, ub_ref):
    v_words = _pack_bf16_pairs(v_ref[...])
    uv_ref[:, 0:ROW_WORDS] = _pack_bf16_pairs(u_ref[...])
    uv_ref[:, ROW_WORDS:2 * ROW_WORDS] = v_words
    vp_ref[...] = v_words
    ub_ref[...] = u_ref[...].astype(BF16)


def _pack_tables_call(u_tab, v_tab):
    n_exp = u_tab.shape[0]
    blk = pl.BlockSpec((PACK_ROWS, D_MODEL), lambda i: (i, 0))
    half = pl.BlockSpec((PACK_ROWS, ROW_WORDS), lambda i: (i, 0))
    return pl.pallas_call(
        _pack_tables_kernel, grid=(n_exp // PACK_ROWS,), in_specs=[blk, blk], out_specs=[blk, half, blk],
        out_shape=[jax.ShapeDtypeStruct((n_exp, 2 * ROW_WORDS), I32), jax.ShapeDtypeStruct((n_exp, ROW_WORDS), I32),
                   jax.ShapeDtypeStruct((n_exp, D_MODEL), BF16)],
        compiler_params=pltpu.CompilerParams(dimension_semantics=("parallel",)),
        name="pack_tables",
    )(u_tab, v_tab)


def _unpack_bf16_pairs(words):
    hi = lax.bitcast_convert_type(words & HI_MASK, F32)
    lo = lax.bitcast_convert_type(lax.shift_left(words, jnp.int32(16)), F32)
    return jnp.concatenate([hi, lo], axis=1)


def _scores_kernel(x_ref, u_ref, o_ref):
    o_ref[...] = _pack_bf16_pairs(_dot_t(_unpack_bf16_pairs(x_ref[...]).astype(BF16), u_ref[...]))


def _scores_call(h2p, u_bf16):
    n = h2p.shape[0]
    n_exp = u_bf16.shape[0]
    tm = math.gcd(n, SCORE_TOKENS)
    return pl.pallas_call(
        _scores_kernel, grid=(n // tm, n_exp // SCORE_EXPERTS),
        in_specs=[pl.BlockSpec((tm, ROW_WORDS), lambda i, j: (i, 0)),
                  pl.BlockSpec((SCORE_EXPERTS, D_MODEL), lambda i, j: (j, 0))],
        out_specs=pl.BlockSpec((tm, SCORE_EXPERTS // 2), lambda i, j: (i, j)),
        out_shape=jax.ShapeDtypeStruct((n, n_exp // 2), I32),
        compiler_params=pltpu.CompilerParams(dimension_semantics=("parallel", "parallel")),
        name="expert_scores",
    )(h2p, u_bf16)


def _sc_gelu(a):
    z = GELU_C0 * (a + GELU_C1 * (a * a * a))
    tanh = 1.0 - 2.0 / (jnp.exp(2.0 * z) + 1.0)
    return 0.5 * a * (1.0 + tanh)


def _sc_split(words):
    return (plsc.bitcast(words & HI_MASK, F32), plsc.bitcast(lax.shift_left(words, jnp.int32(16)), F32))


def _sc_mul_bf16(a_words, b_words):
    return plsc.bitcast(a_words, BF16) * plsc.bitcast(b_words, BF16)


def _sc_split_sum(p, q):
    return _sc_split(plsc.bitcast(p + q, I32))


def _peer_sc_kernel(h2_hbm, eidx_hbm, gate_hbm, uv_hbm, out_hbm,
                    xbuf, ibuf, gbuf, obuf, uvbuf, mbuf, wbuf, sems, sems_in, sems_out):
    n = h2_hbm.shape[0]
    per_worker = n // SC_WORKERS
    blk_tokens = xbuf.shape[1]
    n_blocks = per_worker // blk_tokens
    wid = lax.axis_index("c") * SC_SUBCORES + lax.axis_index("s")
    lane = lax.iota(I32, SC_LANES)

    def split_item(item):
        return lax.shift_right_logical(item, SC_NGROUPS.bit_length() - 1), item & (SC_NGROUPS - 1)

    def gather_copies(bset, item, slot):
        t, g = split_item(item)
        idx = ibuf[bset, t, pl.ds(g * SC_GROUP, SC_GROUP)]
        return (pltpu.make_async_copy(uv_hbm.at[idx], uvbuf.at[slot], sems.at[slot]),)

    def dots(bset, t, slot):
        zero = jnp.zeros((SC_LANES,), F32)

        @pl.loop(0, SC_GROUP, step=SC_DOT_ROWS)
        def _(r0):
            accs = [[zero] * SC_DOT_PARTIALS for _ in range(SC_DOT_ROWS)]
            for k in range(0, ROW_WORDS // SC_LANES, 4):
                xs = [xbuf[bset, t, pl.ds((k + q) * SC_LANES, SC_LANES)] for q in range(4)]
                for i in range(SC_DOT_ROWS):
                    m = [_sc_mul_bf16(xs[q], uvbuf[slot, r0 + i, pl.ds((k + q) * SC_LANES, SC_LANES)])
                         for q in range(4)]
                    hi, lo = _sc_split_sum(m[0] + m[1], m[2] + m[3])
                    p = (k // 4) % SC_DOT_PARTIALS
                    accs[i][p] = accs[i][p] + (hi + lo)
            for i in range(SC_DOT_ROWS):
                mbuf[r0 + i, :] = functools.reduce(lambda a, b: a + b, accs[i])

        tot = zero
        for c in range(SC_LANES):
            tot = tot + plsc.load_gather(mbuf, [lane, jnp.full((SC_LANES,), c, I32)])
        return tot

    def accumulate(bset, t, slot):
        nv = SC_OWORDS // SC_LANES
        for oc in range(ROW_WORDS // SC_OWORDS):
            w0 = oc * SC_OWORDS
            accs = (tuple(obuf[bset, t, pl.ds(w0 + j * SC_LANES, SC_LANES)] for j in range(nv))
                    + tuple(obuf[bset, t, pl.ds(ROW_WORDS + w0 + j * SC_LANES, SC_LANES)] for j in range(nv)))

            def row_quad(rq, accs):
                r = 4 * rq
                ws = [plsc.load_gather(wbuf, [jnp.full((SC_LANES,), r + q, I32)]) for q in range(4)]
                his, los = [], []
                for j in range(nv):
                    m = [_sc_mul_bf16(ws[q], uvbuf[slot, r + q, pl.ds(ROW_WORDS + w0 + j * SC_LANES, SC_LANES)])
                         for q in range(4)]
                    hi, lo = _sc_split_sum(m[0] + m[1], m[2] + m[3])
                    his.append(accs[j] + hi)
                    los.append(accs[nv + j] + lo)
                return tuple(his) + tuple(los)

            accs = lax.fori_loop(0, SC_GROUP // 4, row_quad, accs)
            for j in range(nv):
                obuf[bset, t, pl.ds(w0 + j * SC_LANES, SC_LANES)] = accs[j]
                obuf[bset, t, pl.ds(ROW_WORDS + w0 + j * SC_LANES, SC_LANES)] = accs[nv + j]

    def pack_weights(w):
        bits = plsc.bitcast(w, I32)
        rounded = (bits + 0x7FFF + (lax.shift_right_logical(bits, jnp.int32(16)) & 1)) & HI_MASK
        return rounded | lax.shift_right_logical(rounded, jnp.int32(16))

    n_items = blk_tokens * SC_NGROUPS

    def block_rows(blk):
        return pl.ds(pl.multiple_of(wid * per_worker + blk * blk_tokens, blk_tokens), blk_tokens)

    def load_copies(blk, bset):
        rows = block_rows(blk)
        return (pltpu.make_async_copy(h2_hbm.at[rows], xbuf.at[bset], sems_in.at[bset]),
                pltpu.make_async_copy(eidx_hbm.at[rows], ibuf.at[bset], sems_in.at[bset]),
                pltpu.make_async_copy(gate_hbm.at[rows], gbuf.at[bset], sems_in.at[bset]))

    def store_copy(blk, bset):
        return pltpu.make_async_copy(obuf.at[bset], out_hbm.at[block_rows(blk)], sems_out.at[bset])

    for c in load_copies(0, 0):
        c.start()

    @pl.loop(0, n_blocks)
    def _(blk):
        bset = blk & 1
        for c in load_copies(blk, bset):
            c.wait()

        @pl.when(blk + 1 < n_blocks)
        def _():
            for c in load_copies(blk + 1, 1 - bset):
                c.start()

        @pl.when(blk >= 2)
        def _():
            store_copy(blk - 2, bset).wait()

        @pl.loop(0, blk_tokens)
        def _(t):
            zero = jnp.zeros((SC_LANES,), F32)
            for j in range(D_MODEL // SC_LANES):
                obuf[bset, t, pl.ds(j * SC_LANES, SC_LANES)] = zero

        for ahead in range(SC_SLOTS - 1):
            for c in gather_copies(bset, ahead, ahead):
                c.start()

        @pl.loop(0, n_items)
        def _(item):
            t, g = split_item(item)
            slot = item & (SC_SLOTS - 1)
            ahead = item + (SC_SLOTS - 1)

            @pl.when(ahead < n_items)
            def _():
                for c in gather_copies(bset, ahead, ahead & (SC_SLOTS - 1)):
                    c.start()

            for c in gather_copies(bset, item, slot):
                c.wait()
            a = dots(bset, t, slot)
            wbuf[...] = pack_weights(gbuf[bset, t, pl.ds(g * SC_GROUP, SC_GROUP)] * _sc_gelu(a))
            accumulate(bset, t, slot)

        store_copy(blk, bset).start()

    for blk in range(max(n_blocks - 2, 0), n_blocks):
        store_copy(blk, blk & 1).wait()


def _peer_experts(h2p, eidx, gates, uv_pack):
    n = h2p.shape[0]
    assert n % (8 * SC_WORKERS) == 0
    blk_tokens = math.gcd(n // SC_WORKERS, SC_TOKENS)
    mesh = plsc.VectorSubcoreMesh(core_axis_name="c", subcore_axis_name="s")
    fn = pl.kernel(
        _peer_sc_kernel,
        out_type=jax.ShapeDtypeStruct((n, D_MODEL), F32),
        mesh=mesh,
        scratch_types=[
            pltpu.VMEM((2, blk_tokens, ROW_WORDS), I32),
            pltpu.VMEM((2, blk_tokens, N_SEL), I32),
            pltpu.VMEM((2, blk_tokens, N_SEL), F32),
            pltpu.VMEM((2, blk_tokens, D_MODEL), F32),
            pltpu.VMEM((SC_SLOTS, SC_GROUP, 2 * ROW_WORDS), I32),
            pltpu.VMEM((SC_GROUP, SC_LANES), F32),
            pltpu.VMEM((SC_LANES,), I32),
            pltpu.SemaphoreType.DMA((SC_SLOTS,)), pltpu.SemaphoreType.DMA((2,)), pltpu.SemaphoreType.DMA((2,)),
        ],
        compiler_params=pltpu.CompilerParams(needs_layout_passes=False),
        cost_estimate=pl.CostEstimate(
            flops=4 * n * N_SEL * D_MODEL, transcendentals=n * N_SEL,
            bytes_accessed=4 * (2 * n * N_SEL * ROW_WORDS + n * ROW_WORDS + n * D_MODEL + 2 * n * N_SEL)),
        name="peer_experts",
    )
    return fn(h2p, eidx, gates, uv_pack)


def _peer_sc_scored_kernel(a_hbm, eidx_hbm, gate_hbm, v_hbm, out_hbm,
                           abuf, ibuf, gbuf, obuf, vbuf, wbuf, sems, sems_a):
    n = a_hbm.shape[0]
    per_worker = n // SC_WORKERS
    blk_tokens = ibuf.shape[0]
    wid = lax.axis_index("c") * SC_SUBCORES + lax.axis_index("s")

    def split_item(item):
        return lax.shift_right_logical(item, SC_NGROUPS.bit_length() - 1), item & (SC_NGROUPS - 1)

    def gather_copy(item, slot):
        t, g = split_item(item)
        idx = ibuf[t, pl.ds(g * SC_GROUP, SC_GROUP)]
        return pltpu.make_async_copy(v_hbm.at[idx], vbuf.at[slot], sems.at[slot])

    def accumulate(t, slot):
        nv = SC_OWORDS // SC_LANES
        for oc in range(ROW_WORDS // SC_OWORDS):
            w0 = oc * SC_OWORDS
            accs = (tuple(obuf[t, pl.ds(w0 + j * SC_LANES, SC_LANES)] for j in range(nv))
                    + tuple(obuf[t, pl.ds(ROW_WORDS + w0 + j * SC_LANES, SC_LANES)] for j in range(nv)))

            def row_quad(rq, accs):
                r = 4 * rq
                ws = [plsc.load_gather(wbuf, [jnp.full((SC_LANES,), r + q, I32)]) for q in range(4)]
                his, los = [], []
                for j in range(nv):
                    m = [_sc_mul_bf16(ws[q], vbuf[slot, r + q, pl.ds(w0 + j * SC_LANES, SC_LANES)])
                         for q in range(4)]
                    hi, lo = _sc_split_sum(m[0] + m[1], m[2] + m[3])
                    his.append(accs[j] + hi)
                    los.append(accs[nv + j] + lo)
                return tuple(his) + tuple(los)

            accs = lax.fori_loop(0, SC_GROUP // 4, row_quad, accs)
            for j in range(nv):
                obuf[t, pl.ds(w0 + j * SC_LANES, SC_LANES)] = accs[j]
                obuf[t, pl.ds(ROW_WORDS + w0 + j * SC_LANES, SC_LANES)] = accs[nv + j]

    def pack_weights(w):
        bits = plsc.bitcast(w, I32)
        rounded = (bits + 0x7FFF + (lax.shift_right_logical(bits, jnp.int32(16)) & 1)) & HI_MASK
        return rounded | lax.shift_right_logical(rounded, jnp.int32(16))

    n_items = blk_tokens * SC_NGROUPS

    @pl.loop(0, per_worker // blk_tokens)
    def _(blk):
        tok0 = pl.multiple_of(wid * per_worker + blk * blk_tokens, blk_tokens)

        def score_copy(t):
            return pltpu.make_async_copy(a_hbm.at[tok0 + t], abuf.at[t & 1], sems_a.at[t & 1])

        score_copy(0).start()
        pltpu.sync_copy(eidx_hbm.at[pl.ds(tok0, blk_tokens)], ibuf)
        pltpu.sync_copy(gate_hbm.at[pl.ds(tok0, blk_tokens)], gbuf)

        @pl.loop(0, blk_tokens)
        def _(t):
            zero = jnp.zeros((SC_LANES,), F32)
            for j in range(D_MODEL // SC_LANES):
                obuf[t, pl.ds(j * SC_LANES, SC_LANES)] = zero

        for ahead in range(SC_SLOTS - 1):
            gather_copy(ahead, ahead).start()

        @pl.loop(0, n_items)
        def _(item):
            t, g = split_item(item)
            slot = item & (SC_SLOTS - 1)
            ahead = item + (SC_SLOTS - 1)

            @pl.when(ahead < n_items)
            def _():
                gather_copy(ahead, ahead & (SC_SLOTS - 1)).start()

            @pl.when(g == 0)
            def _():
                score_copy(t).wait()

                @pl.when(t + 1 < blk_tokens)
                def _():
                    score_copy(t + 1).start()

            gather_copy(item, slot).wait()
            idx = ibuf[t, pl.ds(g * SC_GROUP, SC_GROUP)]
            half = SCORE_EXPERTS // 2
            word = plsc.load_gather(
                abuf, [jnp.full((SC_LANES,), t & 1, I32),
                       lax.shift_right_logical(idx, jnp.int32(SCORE_EXPERTS.bit_length() - 1)) * half
                       + (idx & (half - 1))])
            hi, lo = _sc_split(word)
            a = jnp.where((idx & half) == 0, hi, lo)
            wbuf[...] = pack_weights(gbuf[t, pl.ds(g * SC_GROUP, SC_GROUP)] * _sc_gelu(a))
            accumulate(t, slot)

        pltpu.sync_copy(obuf, out_hbm.at[pl.ds(tok0, blk_tokens)])


def _peer_experts_scored(scores, eidx, gates, v_pack):
    n, n_exp = scores.shape
    assert n % (8 * SC_WORKERS) == 0
    blk_tokens = math.gcd(n // SC_WORKERS, 2 * SC_TOKENS)
    mesh = plsc.VectorSubcoreMesh(core_axis_name="c", subcore_axis_name="s")
    fn = pl.kernel(
        _peer_sc_scored_kernel,
        out_type=jax.ShapeDtypeStruct((n, D_MODEL), F32),
        mesh=mesh,
        scratch_types=[
            pltpu.VMEM((2, n_exp), I32),
            pltpu.VMEM((blk_tokens, N_SEL), I32),
            pltpu.VMEM((blk_tokens, N_SEL), F32),
            pltpu.VMEM((blk_tokens, D_MODEL), F32),
            pltpu.VMEM((SC_SLOTS, SC_GROUP, ROW_WORDS), I32),
            pltpu.VMEM((SC_LANES,), I32),
            pltpu.SemaphoreType.DMA((SC_SLOTS,)), pltpu.SemaphoreType.DMA((2,)),
        ],
        compiler_params=pltpu.CompilerParams(needs_layout_passes=False),
        cost_estimate=pl.CostEstimate(
            flops=2 * n * N_SEL * D_MODEL, transcendentals=n * N_SEL,
            bytes_accessed=4 * (n * N_SEL * ROW_WORDS + n * n_exp + n * D_MODEL + 2 * n * N_SEL)),
        name="peer_experts_scored",
    )
    return fn(scores, eidx, gates, v_pack)


def _resid_kernel(x1_ref, p_ref, mod_ref, o_ref):
    o_ref[...] = x1_ref[...] + mod_ref[0, 5:6, :] * p_ref[...]


def _resid_call(x1, peer_out, seq, mod3, mod_row0):
    n = x1.shape[0]
    tm = TOKEN_TILE
    per_seq = seq // tm
    tok = pl.BlockSpec((tm, D_MODEL), lambda i: (i, 0))
    return pl.pallas_call(
        _resid_kernel, grid=(n // tm,),
        in_specs=[tok, tok,
                  pl.BlockSpec((1, 6, D_MODEL), lambda i: (mod_row0 + (i // per_seq if mod_row0 else 0), 0, 0))],
        out_specs=tok, out_shape=jax.ShapeDtypeStruct((n, D_MODEL), F32),
        compiler_params=pltpu.CompilerParams(dimension_semantics=("parallel",)),
        name="resid",
    )(x1, peer_out, mod3)


def _rope_tables(seq, rotate):
    if not rotate:
        return jnp.ones((seq, KV_WIDTH), F32), jnp.zeros((seq, KV_WIDTH), F32)
    quarter = A_DH // 4
    t = jnp.arange(seq)
    row = (t // GRID_W).astype(F32)
    col = (t % GRID_W).astype(F32)
    inv = ROPE_BASE ** (-jnp.arange(quarter, dtype=F32) / quarter)
    d = jnp.arange(A_DH)
    pos = jnp.where(d[None, :] < A_DH // 2, row[:, None], col[:, None])
    ang = pos * inv[d % quarter][None, :]
    sign = jnp.where((d % (A_DH // 2)) < quarter, -1.0, 1.0).astype(F32)
    cos = jnp.cos(ang)
    sin = jnp.sin(ang) * sign[None, :]
    return jnp.tile(cos, (1, KV_WIDTH // A_DH)), jnp.tile(sin, (1, KV_WIDTH // A_DH))


def _run_chunk(x, mod3, mod_row0, prm, cache, rotate, gate_on, dense_scores):
    (n1, n2, w_main, w_g, w_gt, b_g, b_gt, mhg, qg_t, kg_t, bd, sink, w_m, w_a, w_q, sub_a, sub_b,
     (uv_pack, v_pack, u_bf16)) = prm
    batch, seq, _ = x.shape
    n = batch * seq
    x2d = x.reshape(n, D_MODEL)
    if gate_on is not None:
        x2d, _ = lax.optimization_barrier((x2d, gate_on))
    cos, sin = _rope_tables(seq, rotate)
    mq, mk, mv, mo, gcol, grow, aq, ak, av = _inproj_call(
        x2d, seq, mod3, mod_row0, n1, w_main, w_g, w_gt, b_g, b_gt, qg_t, kg_t, bd, cos, sin)
    kc, vc, c0, m0 = cache
    h_f, h_b, c_fin, m_fin = _mlstm_call(mq, mk, mv, gcol, grow, c0, m0, batch, seq)
    if kc is None:
        a_out = _attn_ctx_call(sink, aq, ak, av, batch, seq)
    else:
        a_out = _attn_lat_call(sink, aq, ak, av, kc, vc, batch, seq)
    x1, h2p, eidx, gates = _mix_call(x2d, seq, h_f, h_b, mo, a_out, mod3, mod_row0, mhg, n2, w_m, w_a, w_q,
                                     sub_a, sub_b)
    if dense_scores:
        peer_out = _peer_experts_scored(_scores_call(h2p, u_bf16), eidx, gates, v_pack)
    else:
        peer_out = _peer_experts(h2p, eidx, gates, uv_pack)
    y = _resid_call(x1, peer_out, seq, mod3, mod_row0).reshape(batch, seq, D_MODEL)
    return y, h2p, peer_out, ak, av, c_fin, m_fin


def _pack_state(C, n_vec, m):
    b = C.shape[0]
    caug = jnp.concatenate([C, jnp.broadcast_to(n_vec[..., None], C.shape)], axis=-1)
    caug = caug.reshape(b, 2 * M_HEADS, M_DH, 2 * M_DH)
    m_rep = jnp.broadcast_to(m.reshape(b, 2 * M_HEADS, 1, 1), (b, 2 * M_HEADS, 8, M_DH))
    return caug.astype(F32), m_rep.astype(F32)


def kernel(x_prompt, x_sample, c, cache_attn_k, cache_attn_v, state_mlstm_C, state_mlstm_n, state_mlstm_m,
           c_ctx, w_ada, b_ada, norm1_g, norm2_g, w_in, b_gates, mh_norm_g, q_norm_g, k_norm_g, sink_logits,
           w_out, peer_w_q, peer_sub_a, peer_sub_b, peer_u, peer_v):
    depth = w_ada.shape[0]
    assert depth == 1
    batch, seq, _ = x_prompt.shape
    dec_batch, dec_seq, _ = x_sample.shape
    assert dec_batch + 1 <= MOD_ROWS and batch % sum(CTX_HEAD_WEIGHTS + CTX_TAIL_WEIGHTS) == 0 and dec_batch % LATENT_CHUNKS == 0
    l = 0

    cond = jnp.concatenate([c_ctx[None, :], c, jnp.zeros((MOD_ROWS - 1 - dec_batch, D_MODEL), F32)], axis=0)
    mod3 = _ada_call(cond, w_ada[l], b_ada[l]).reshape(MOD_ROWS, 6, D_MODEL)

    wi = w_in[l]
    g0 = 4 * M_WIDTH
    w_main = jnp.concatenate([wi[:, :g0], wi[:, g0 + N_GATES:]], axis=1).astype(BF16)
    w_g = wi[:, g0:g0 + N_GATES]
    seg = jnp.arange(A_WIDTH) // A_DH
    bd = jnp.where(seg[:, None] == seg[None, :], 1.0 / A_DH, 0.0).astype(F32)
    prm = (norm1_g[l][None, :], norm2_g[l][None, :], w_main, w_g, w_g.T, b_gates[l][None, :], b_gates[l][:, None],
           mh_norm_g[l][None, :], jnp.tile(q_norm_g[l], A_HEADS)[None, :], jnp.tile(k_norm_g[l], A_KV)[None, :], bd,
           sink_logits[l], w_out[l][:M_WIDTH].astype(BF16), w_out[l][M_WIDTH:].astype(BF16),
           peer_w_q[l].astype(BF16), peer_sub_a[l].astype(BF16), peer_sub_b[l].astype(BF16),
           _pack_tables_call(peer_u[l], peer_v[l]))

    zeros_c = jnp.zeros((batch, 2, M_HEADS, M_DH, M_DH), F32)
    c0, m0 = _pack_state(zeros_c, zeros_c[..., 0], jnp.full((batch, 2, M_HEADS), NEG, F32))
    c0s, m0s = _pack_state(state_mlstm_C[:, l], state_mlstm_n[:, l], state_mlstm_m[:, l])
    past = cache_attn_k.shape[2]
    kc = cache_attn_k[:, l].reshape(dec_batch, past, KV_WIDTH)
    vc = cache_attn_v[:, l].reshape(dec_batch, past, KV_WIDTH)

    ctx_jobs = []
    weights = CTX_HEAD_WEIGHTS + CTX_TAIL_WEIGHTS
    b0 = 0
    for f in weights:
        b1 = b0 + batch * f // sum(weights)
        ctx_jobs.append((x_prompt[b0:b1], 0, (None, None, c0[b0:b1], m0[b0:b1]), False, False))
        b0 = b1
    assert b0 == batch
    lat_jobs = []
    for ci, b0 in enumerate(range(0, dec_batch, dec_batch // LATENT_CHUNKS)):
        b1 = b0 + dec_batch // LATENT_CHUNKS
        lat_jobs.append((x_sample[b0:b1], 1 + b0, (kc[b0:b1], vc[b0:b1], c0s[b0:b1], m0s[b0:b1]), True,
                         ci in DENSE_LATENT_CHUNKS))
    n_head = len(CTX_HEAD_WEIGHTS)
    jobs = ctx_jobs[:n_head] + lat_jobs + ctx_jobs[n_head:]
    outs = []
    for i, (x_c, mod_row0, cache, rotate, dense) in enumerate(jobs):
        gate = tuple(g for g in (outs[i - 1][1] if i >= 1 else None,
                                 outs[i - EXPERT_LAG][2] if i >= EXPERT_LAG else None) if g is not None)
        outs.append(_run_chunk(x_c, mod3, mod_row0, prm, cache, rotate, gate or None, dense))
    ctx = outs[:n_head] + outs[n_head + len(lat_jobs):]
    lat = outs[n_head:n_head + len(lat_jobs)]
    y_p = jnp.concatenate([o[0] for o in ctx], axis=0)
    y_s = jnp.concatenate([o[0] for o in lat], axis=0)
    k_new = jnp.concatenate([o[3] for o in ctx], axis=0)
    v_new = jnp.concatenate([o[4] for o in ctx], axis=0)
    c_fin = jnp.concatenate([o[5] for o in ctx], axis=0)
    m_fin = jnp.concatenate([o[6] for o in ctx], axis=0)

    c_fin = c_fin.reshape(batch, 2, M_HEADS, M_DH, 2 * M_DH)
    new_c = c_fin[..., :M_DH][:, None]
    new_n = c_fin[..., M_DH][:, None]
    new_m = m_fin[:, :, 0, 0].reshape(batch, 2, M_HEADS)[:, None]
    new_k = k_new.reshape(batch, 1, seq, A_KV, A_DH)
    new_v = v_new.reshape(batch, 1, seq, A_KV, A_DH)
    return y_p, y_s, new_k, new_v, new_c, new_n, new_m
```

```python
import functools
import math

import jax
import jax.numpy as jnp
from jax import lax
from jax.experimental import pallas as pl
from jax.experimental.pallas import tpu as pltpu
from jax.experimental.pallas import tpu_sc as plsc

F32 = jnp.float32
BF16 = jnp.bfloat16
I32 = jnp.int32
HI = lax.Precision.HIGHEST

D_MODEL = 1024
EPS = 1e-6
NEG = -1e30
GRID_W = 64
M_HEADS = 4
M_WIDTH = 512
M_DH = 128
A_HEADS = 8
A_KV = 2
A_REP = 4
A_DH = 64
A_WIDTH = 512
KV_WIDTH = A_KV * A_DH
BLOCK = 128
ROPE_BASE = 10000.0
N_KEYS = 128
P_HEADS = 8
P_DKEY = 256
P_HALF = 128
P_TOPK = 16
N_SEL = P_HEADS * P_TOPK
N_GATES = 4 * M_HEADS
MAIN_COLS = 4 * M_WIDTH + A_WIDTH + 2 * KV_WIDTH
MOD_ROWS = 16

TOKEN_TILE = 256
MLSTM_CHUNK = 128
ADA_COL_TILE = 768
CTX_CHUNK_WEIGHTS = (1, 2, 3, 4, 6)
EXPERT_LAG = 3
LATENT_CHUNKS = 8


def _sigmoid(x):
    return 1.0 / (1.0 + jnp.exp(-x))


def _log_sigmoid(x):
    return jnp.minimum(x, 0.0) - jnp.log1p(jnp.exp(-jnp.abs(x)))


def _dot_t(a, b, precision=None):
    return lax.dot_general(a, b, (((1,), (1,)), ((), ())), precision=precision,
                           preferred_element_type=F32)


def _ada_kernel(c_ref, w_ref, b_ref, o_ref):
    c = c_ref[...]
    s = c * _sigmoid(c)
    o_ref[...] = jnp.dot(s, w_ref[...], precision=HI, preferred_element_type=F32) + b_ref[...]


def _ada_call(cond, w_ada, b_ada):
    n_out = w_ada.shape[1]
    return pl.pallas_call(
        _ada_kernel,
        grid=(n_out // ADA_COL_TILE,),
        in_specs=[pl.BlockSpec((MOD_ROWS, D_MODEL), lambda j: (0, 0)),
                  pl.BlockSpec((D_MODEL, ADA_COL_TILE), lambda j: (0, j)),
                  pl.BlockSpec((1, ADA_COL_TILE), lambda j: (0, j))],
        out_specs=pl.BlockSpec((MOD_ROWS, ADA_COL_TILE), lambda j: (0, j)),
        out_shape=jax.ShapeDtypeStruct((MOD_ROWS, n_out), F32),
        name="ada",
    )(cond, w_ada, b_ada.reshape(1, n_out))


def _swap16(x):
    n = x.shape[-1]
    lane = lax.broadcasted_iota(I32, x.shape, x.ndim - 1)
    return jnp.where((lane & 16) == 0, pltpu.roll(x, n - 16, x.ndim - 1), pltpu.roll(x, 16, x.ndim - 1))


def _inproj_kernel(x_ref, mod_ref, n1_ref, w_ref, wg_ref, wgt_ref, bg_ref, bgt_ref, qg_ref, kg_ref,
                   bd_ref, cos_ref, sin_ref,
                   mq_ref, mk_ref, mv_ref, mo_ref, gc_ref, gr_ref, aq_ref, ak_ref, av_ref):
    x = x_ref[...]
    h = x * lax.rsqrt(jnp.mean(x * x, axis=-1, keepdims=True) + EPS) * n1_ref[...]
    h = h * (1.0 + mod_ref[0, 1:2, :]) + mod_ref[0, 0:1, :]
    z = jnp.dot(h.astype(BF16), w_ref[...], preferred_element_type=F32)

    mq_ref[...] = (z[:, 0:M_WIDTH] * (M_DH ** -0.5)).astype(BF16)
    mk_ref[...] = z[:, M_WIDTH:2 * M_WIDTH].astype(BF16)
    mv_ref[...] = z[:, 2 * M_WIDTH:3 * M_WIDTH].astype(BF16)
    mo_ref[...] = z[:, 3 * M_WIDTH:4 * M_WIDTH]

    g = jnp.dot(h, wg_ref[...], precision=HI, preferred_element_type=F32) + bg_ref[...]
    kind = lax.broadcasted_iota(I32, g.shape, 1) // M_HEADS
    gc_ref[...] = jnp.where((kind & 1) == 1, _log_sigmoid(g), g)
    gt = _dot_t(wgt_ref[...], h, precision=HI) + bgt_ref[...]
    kind_t = lax.broadcasted_iota(I32, gt.shape, 0) // M_HEADS
    gr_ref[...] = jnp.where((kind_t & 1) == 1, _log_sigmoid(gt), gt)

    o = 4 * M_WIDTH
    aq = z[:, o:o + A_WIDTH]
    ak = z[:, o + A_WIDTH:o + A_WIDTH + KV_WIDTH]
    av_ref[...] = z[:, o + A_WIDTH + KV_WIDTH:o + A_WIDTH + 2 * KV_WIDTH]
    bd = bd_ref[...]
    cos = cos_ref[...]
    sin = sin_ref[...]
    aq = aq * lax.rsqrt(jnp.dot(aq * aq, bd, precision=HI, preferred_element_type=F32) + EPS) * qg_ref[...]
    cos4 = jnp.concatenate([cos] * (A_WIDTH // KV_WIDTH), axis=1)
    sin4 = jnp.concatenate([sin] * (A_WIDTH // KV_WIDTH), axis=1)
    aq = (aq * cos4 + _swap16(aq) * sin4) * (A_DH ** -0.5)
    ak = ak * lax.rsqrt(jnp.dot(ak * ak, bd[0:KV_WIDTH, 0:KV_WIDTH], precision=HI,
                                preferred_element_type=F32) + EPS) * kg_ref[...]
    ak_ref[...] = ak * cos + _swap16(ak) * sin

    lane = lax.broadcasted_iota(I32, (aq.shape[0], KV_WIDTH), 1)
    for hd in range(A_HEADS):
        grp = hd // A_REP
        blk = aq[:, (hd // 2) * KV_WIDTH:(hd // 2 + 1) * KV_WIDTH]
        if hd % 2 != grp:
            blk = pltpu.roll(blk, A_DH, 1)
        keep = (lane >= grp * A_DH) & (lane < (grp + 1) * A_DH)
        aq_ref[hd] = jnp.where(keep, blk, 0.0).astype(BF16)


def _inproj_call(x2d, tile0, n, seq, mod3, mod_row0, n1, w_main, w_g, w_gt, b_g, b_gt, qg_t, kg_t, bd, cos, sin):
    tm = TOKEN_TILE
    per_seq = seq // tm

    def tok(i):
        return (i, 0)

    def const2(i):
        return (0, 0)

    in_specs = [
        pl.BlockSpec((tm, D_MODEL), lambda i: (tile0 + i, 0)),
        pl.BlockSpec((1, 6, D_MODEL), lambda i: (mod_row0 + (i // per_seq if mod_row0 else 0), 0, 0)),
        pl.BlockSpec((1, D_MODEL), const2),
        pl.BlockSpec((D_MODEL, MAIN_COLS), const2),
        pl.BlockSpec((D_MODEL, N_GATES), const2),
        pl.BlockSpec((N_GATES, D_MODEL), const2),
        pl.BlockSpec((1, N_GATES), const2),
        pl.BlockSpec((N_GATES, 1), const2),
        pl.BlockSpec((1, A_WIDTH), const2),
        pl.BlockSpec((1, KV_WIDTH), const2),
        pl.BlockSpec((A_WIDTH, A_WIDTH), const2),
        pl.BlockSpec((tm, KV_WIDTH), lambda i: (i % per_seq, 0)),
        pl.BlockSpec((tm, KV_WIDTH), lambda i: (i % per_seq, 0)),
    ]
    out_specs = [
        pl.BlockSpec((tm, M_WIDTH), tok),
        pl.BlockSpec((tm, M_WIDTH), tok),
        pl.BlockSpec((tm, M_WIDTH), tok),
        pl.BlockSpec((tm, M_WIDTH), tok),
        pl.BlockSpec((tm, N_GATES), tok),
        pl.BlockSpec((N_GATES, tm), lambda i: (0, i)),
        pl.BlockSpec((A_HEADS, tm, KV_WIDTH), lambda i: (0, i, 0)),
        pl.BlockSpec((tm, KV_WIDTH), tok),
        pl.BlockSpec((tm, KV_WIDTH), tok),
    ]
    out_shape = [
        jax.ShapeDtypeStruct((n, M_WIDTH), BF16),
        jax.ShapeDtypeStruct((n, M_WIDTH), BF16),
        jax.ShapeDtypeStruct((n, M_WIDTH), BF16),
        jax.ShapeDtypeStruct((n, M_WIDTH), F32),
        jax.ShapeDtypeStruct((n, N_GATES), F32),
        jax.ShapeDtypeStruct((N_GATES, n), F32),
        jax.ShapeDtypeStruct((A_HEADS, n, KV_WIDTH), BF16),
        jax.ShapeDtypeStruct((n, KV_WIDTH), F32),
        jax.ShapeDtypeStruct((n, KV_WIDTH), F32),
    ]
    return pl.pallas_call(
        _inproj_kernel, grid=(n // tm,), in_specs=in_specs, out_specs=out_specs, out_shape=out_shape,
        compiler_params=pltpu.CompilerParams(dimension_semantics=("parallel",)),
        name="inproj",
    )(x2d, mod3, n1, w_main, w_g, w_gt, b_g, b_gt, qg_t, kg_t, bd, cos, sin)


def _mlstm_chain(q, k, v, li_c, lf_c, li_r, lf_r, caug, m, tri, tri_t, mask, reverse):
    L = q.shape[0]
    last = 0 if reverse else L - 1
    b_c = jnp.dot(tri, jnp.broadcast_to(lf_c, (L, L)), precision=HI, preferred_element_type=F32)
    b_r = jnp.dot(jnp.broadcast_to(lf_r, (8, L)), tri_t, precision=HI, preferred_element_type=F32)[0:1, :]
    a_inter = b_c[:, 0:1] + m
    d = jnp.where(mask, b_c - b_r + li_r, -jnp.inf)
    m_t = jnp.maximum(a_inter, jnp.max(d, axis=1, keepdims=True))
    w_inter = jnp.exp(a_inter - m_t)
    s = _dot_t(q, k) * jnp.exp(d - m_t)
    qc = jnp.dot(q, caug.astype(BF16), preferred_element_type=F32)
    num = jnp.dot(s.astype(BF16), v, preferred_element_type=F32) + w_inter * qc[:, 0:M_DH]
    den = jnp.sum(s, axis=1, keepdims=True) + w_inter * qc[:, M_DH:M_DH + 1]
    den = jnp.maximum(jnp.abs(den), jnp.exp(-m_t))
    h = num / den
    m_new = m_t[last:last + 1, :]
    b_last = b_c[last:last + 1, 0:1]
    g_c = jnp.exp(b_last - b_c[:, 0:1] + li_c - m_new)
    decay = jnp.exp(b_last + m - m_new)
    kw = (k.astype(F32) * g_c).astype(BF16)
    vaug = jnp.concatenate([v, jnp.ones_like(v)], axis=1)
    upd = lax.dot_general(kw, vaug, (((0,), (0,)), ((), ())), preferred_element_type=F32)
    return h, decay * caug + upd, m_new


def _mlstm_kernel(qf_ref, kf_ref, vf_ref, gcf_ref, grf_ref, qb_ref, kb_ref, vb_ref, gcb_ref, grb_ref,
                  c0_ref, m0_ref, hf_ref, hb_ref, cfin_ref, mfin_ref, c_scr, m_scr):
    c = pl.program_id(1)
    nc = pl.num_programs(1)
    L = qf_ref.shape[0]

    @pl.when(c == 0)
    def _():
        c_scr[...] = c0_ref[0]
        m_scr[...] = m0_ref[0]

    row = lax.broadcasted_iota(I32, (L, L), 0)
    col = lax.broadcasted_iota(I32, (L, L), 1)
    lower = row >= col
    upper = row <= col
    lower_f = lower.astype(F32)
    upper_f = upper.astype(F32)

    for direction in range(2):
        reverse = direction == 1
        q_ref, k_ref, v_ref, gc_ref, gr_ref, h_ref = (
            (qb_ref, kb_ref, vb_ref, gcb_ref, grb_ref, hb_ref) if reverse
            else (qf_ref, kf_ref, vf_ref, gcf_ref, grf_ref, hf_ref))
        tri, tri_t, mask = (upper_f, lower_f, upper) if reverse else (lower_f, upper_f, lower)
        gc = gc_ref[...]
        gr = gr_ref[...]
        for hd in range(M_HEADS):
            ch = direction * M_HEADS + hd
            sl = slice(hd * M_DH, (hd + 1) * M_DH)
            ci = 2 * direction * M_HEADS + hd
            cf = ci + M_HEADS
            h, caug, m_new = _mlstm_chain(
                q_ref[:, sl], k_ref[:, sl], v_ref[:, sl],
                gc[:, ci:ci + 1], gc[:, cf:cf + 1], gr[ci:ci + 1, :], gr[cf:cf + 1, :],
                c_scr[ch], m_scr[ch][0:1, 0:1], tri, tri_t, mask, reverse)
            h_ref[:, sl] = h
            c_scr[ch] = caug
            m_scr[ch] = jnp.broadcast_to(m_new, m_scr.shape[1:])

    @pl.when(c == nc - 1)
    def _():
        cfin_ref[0] = c_scr[...]
        mfin_ref[0] = m_scr[...]


def _mlstm_call(mq, mk, mv, gcol, grow, c0, m0, batch, seq):
    n = mq.shape[0]
    L = MLSTM_CHUNK
    nc = seq // L
    n_ch = 2 * M_HEADS

    def fwd(b, c):
        return (b * nc + c, 0)

    def bwd(b, c):
        return (b * nc + nc - 1 - c, 0)

    def fwd_t(b, c):
        return (0, b * nc + c)

    def bwd_t(b, c):
        return (0, b * nc + nc - 1 - c)

    tok = pl.BlockSpec((L, M_WIDTH), fwd)
    tok_b = pl.BlockSpec((L, M_WIDTH), bwd)
    in_specs = [tok, tok, tok, pl.BlockSpec((L, N_GATES), fwd), pl.BlockSpec((N_GATES, L), fwd_t),
                tok_b, tok_b, tok_b, pl.BlockSpec((L, N_GATES), bwd), pl.BlockSpec((N_GATES, L), bwd_t),
                pl.BlockSpec((1, n_ch, M_DH, 2 * M_DH), lambda b, c: (b, 0, 0, 0)),
                pl.BlockSpec((1, n_ch, 8, M_DH), lambda b, c: (b, 0, 0, 0))]
    out_specs = [tok, tok_b,
                 pl.BlockSpec((1, n_ch, M_DH, 2 * M_DH), lambda b, c: (b, 0, 0, 0)),
                 pl.BlockSpec((1, n_ch, 8, M_DH), lambda b, c: (b, 0, 0, 0))]
    out_shape = [jax.ShapeDtypeStruct((n, M_WIDTH), F32), jax.ShapeDtypeStruct((n, M_WIDTH), F32),
                 jax.ShapeDtypeStruct((batch, n_ch, M_DH, 2 * M_DH), F32),
                 jax.ShapeDtypeStruct((batch, n_ch, 8, M_DH), F32)]
    return pl.pallas_call(
        _mlstm_kernel, grid=(batch, nc), in_specs=in_specs, out_specs=out_specs, out_shape=out_shape,
        scratch_shapes=[pltpu.VMEM((n_ch, M_DH, 2 * M_DH), F32), pltpu.VMEM((n_ch, 8, M_DH), F32)],
        compiler_params=pltpu.CompilerParams(dimension_semantics=("parallel", "arbitrary")),
        name="mlstm",
    )(mq, mk, mv, gcol, grow, mq, mk, mv, gcol, grow, c0, m0)


def _sink_column(sink_ref, grp, rows_per_head):
    return jnp.concatenate(
        [jnp.full((rows_per_head, 1), sink_ref[grp * A_REP + r], F32) for r in range(A_REP)], axis=0)


def _store_heads(out_ref, o, grp, rows_per_head):
    for r in range(A_REP):
        hd = grp * A_REP + r
        out_ref[:, hd * A_DH:(hd + 1) * A_DH] = o[r * rows_per_head:(r + 1) * rows_per_head,
                                                  grp * A_DH:(grp + 1) * A_DH].astype(out_ref.dtype)


def _attn_ctx_kernel(sink_ref, q_ref, k_ref, v_ref, out_ref):
    s_len = k_ref.shape[0]
    k = k_ref[...].astype(BF16)
    v = v_ref[...].astype(BF16)
    for grp in range(A_KV):
        q = q_ref[grp * A_REP:(grp + 1) * A_REP].reshape(A_REP * s_len, KV_WIDTH)
        s = _dot_t(q, k)
        sk = _sink_column(sink_ref, grp, s_len)
        mx = jnp.maximum(jnp.max(s, axis=1, keepdims=True), sk)
        p = jnp.exp(s - mx)
        den = jnp.sum(p, axis=1, keepdims=True) + jnp.exp(sk - mx)
        o = jnp.dot(p.astype(BF16), v, preferred_element_type=F32) / den
        _store_heads(out_ref, o, grp, s_len)


def _attn_ctx_call(sink, aq, ak, av, batch, seq):
    n = ak.shape[0]
    return pl.pallas_call(
        _attn_ctx_kernel, grid=(batch,),
        in_specs=[pl.BlockSpec(memory_space=pltpu.SMEM),
                  pl.BlockSpec((A_HEADS, seq, KV_WIDTH), lambda b: (0, b, 0)),
                  pl.BlockSpec((seq, KV_WIDTH), lambda b: (b, 0)),
                  pl.BlockSpec((seq, KV_WIDTH), lambda b: (b, 0))],
        out_specs=pl.BlockSpec((seq, A_WIDTH), lambda b: (b, 0)),
        out_shape=jax.ShapeDtypeStruct((n, A_WIDTH), BF16),
        compiler_params=pltpu.CompilerParams(dimension_semantics=("parallel",)),
        name="attn_ctx",
    )(sink, aq, ak, av)


def _attn_lat_kernel(sink_ref, q_ref, kc_ref, vc_ref, kp_ref, kq_ref, kn_ref, vp_ref, vq_ref, vn_ref, out_ref):
    i = pl.program_id(1)
    nb = pl.num_programs(1)
    kc = kc_ref[0].astype(BF16)
    vc = vc_ref[0].astype(BF16)
    kp, kq, kn = kp_ref[...].astype(BF16), kq_ref[...].astype(BF16), kn_ref[...].astype(BF16)
    vp, vq, vn = vp_ref[...].astype(BF16), vq_ref[...].astype(BF16), vn_ref[...].astype(BF16)
    rows = A_REP * BLOCK
    qpos = lax.broadcasted_iota(I32, (rows, BLOCK), 0) % BLOCK
    kpos = lax.broadcasted_iota(I32, (rows, BLOCK), 1)
    mask_p = (kpos >= qpos) & (i > 0)
    mask_n = (kpos <= qpos) & (i < nb - 1)
    for grp in range(A_KV):
        q = q_ref[grp * A_REP:(grp + 1) * A_REP].reshape(rows, KV_WIDTH)
        s_c = _dot_t(q, kc)
        s_p = jnp.where(mask_p, _dot_t(q, kp), NEG)
        s_q = _dot_t(q, kq)
        s_n = jnp.where(mask_n, _dot_t(q, kn), NEG)
        sk = _sink_column(sink_ref, grp, BLOCK)
        mx = jnp.maximum(jnp.maximum(jnp.max(s_c, axis=1, keepdims=True), jnp.max(s_p, axis=1, keepdims=True)),
                         jnp.maximum(jnp.max(s_q, axis=1, keepdims=True), jnp.max(s_n, axis=1, keepdims=True)))
        mx = jnp.maximum(mx, sk)
        p_c, p_p, p_q, p_n = jnp.exp(s_c - mx), jnp.exp(s_p - mx), jnp.exp(s_q - mx), jnp.exp(s_n - mx)
        den = (jnp.sum(p_c, axis=1, keepdims=True) + jnp.sum(p_p, axis=1, keepdims=True)
               + jnp.sum(p_q, axis=1, keepdims=True) + jnp.sum(p_n, axis=1, keepdims=True) + jnp.exp(sk - mx))
        o = (jnp.dot(p_c.astype(BF16), vc, preferred_element_type=F32)
             + jnp.dot(p_p.astype(BF16), vp, preferred_element_type=F32)
             + jnp.dot(p_q.astype(BF16), vq, preferred_element_type=F32)
             + jnp.dot(p_n.astype(BF16), vn, preferred_element_type=F32)) / den
        _store_heads(out_ref, o, grp, BLOCK)


def _attn_lat_call(sink, aq, ak, av, kc, vc, batch, seq):
    n = ak.shape[0]
    nb = seq // BLOCK
    past = kc.shape[1]

    def cur(b, i):
        return (b * nb + i, 0)

    def prev(b, i):
        return (b * nb + jnp.maximum(i - 1, 0), 0)

    def nxt(b, i):
        return (b * nb + jnp.minimum(i + 1, nb - 1), 0)

    blk = functools.partial(pl.BlockSpec, (BLOCK, KV_WIDTH))
    cache = pl.BlockSpec((1, past, KV_WIDTH), lambda b, i: (b, 0, 0))
    return pl.pallas_call(
        _attn_lat_kernel, grid=(batch, nb),
        in_specs=[pl.BlockSpec(memory_space=pltpu.SMEM),
                  pl.BlockSpec((A_HEADS, BLOCK, KV_WIDTH), lambda b, i: (0, b * nb + i, 0)),
                  cache, cache, blk(prev), blk(cur), blk(nxt), blk(prev), blk(cur), blk(nxt)],
        out_specs=pl.BlockSpec((BLOCK, A_WIDTH), cur),
        out_shape=jax.ShapeDtypeStruct((n, A_WIDTH), BF16),
        compiler_params=pltpu.CompilerParams(dimension_semantics=("parallel", "parallel")),
        name="attn_lat",
    )(sink, aq, kc, vc, ak, ak, ak, av, av, av)


def _top16_rows(s, payload=None):
    n_rows = s.shape[0]
    rows = lax.broadcasted_iota(I32, s.shape, 0).astype(F32)
    vals, idxs, pays = [], [], []
    for _ in range(P_TOPK):
        mx = jnp.max(s, axis=0, keepdims=True)
        ix = jnp.min(jnp.where(s == mx, rows, float(n_rows)), axis=0, keepdims=True)
        hit = rows == ix
        vals.append(mx)
        idxs.append(ix)
        if payload is not None:
            pays.append(jnp.sum(jnp.where(hit, payload, 0.0), axis=0, keepdims=True))
        s = jnp.where(hit, -jnp.inf, s)
    out = (jnp.concatenate(vals, axis=0), jnp.concatenate(idxs, axis=0))
    if payload is not None:
        out += (jnp.concatenate(pays, axis=0),)
    return out


def _mix_kernel(x_ref, hf_ref, hb_ref, mo_ref, ao_ref, mod_ref, mhg_ref, n2_ref, wm_ref, wa_ref, wq_ref,
                sa_ref, sb_ref, x1_ref, h2_ref, eidx_ref, gate_ref, qp_scr, e_scr, g_scr):
    tm = x_ref.shape[0]
    hs = hf_ref[...] + hb_ref[...]
    parts = []
    for hd in range(M_HEADS):
        blk = hs[:, hd * M_DH:(hd + 1) * M_DH]
        parts.append(blk * lax.rsqrt(jnp.mean(blk * blk, axis=-1, keepdims=True) + EPS))
    m_out = _sigmoid(mo_ref[...]) * (jnp.concatenate(parts, axis=1) * mhg_ref[...])
    mix = (jnp.dot(m_out.astype(BF16), wm_ref[...], preferred_element_type=F32)
           + jnp.dot(ao_ref[...], wa_ref[...], preferred_element_type=F32))
    x1 = x_ref[...] + mod_ref[0, 2:3, :] * mix
    x1_ref[...] = x1
    h2 = x1 * lax.rsqrt(jnp.mean(x1 * x1, axis=-1, keepdims=True) + EPS) * n2_ref[...]
    h2 = h2 * (1.0 + mod_ref[0, 4:5, :]) + mod_ref[0, 3:4, :]
    h2_ref[...] = _pack_bf16_pairs(h2)
    qp = jnp.dot(h2.astype(BF16), wq_ref[...], preferred_element_type=F32)
    for p in range(P_HEADS):
        qp_scr[p] = qp[:, p * P_DKEY:(p + 1) * P_DKEY].astype(BF16)
    sub_a = sa_ref[...]
    sub_b = sb_ref[...]

    def head_body(p, carry):
        for half in range(tm // N_KEYS):
            cols = slice(half * N_KEYS, (half + 1) * N_KEYS)
            qh = qp_scr[p, pl.ds(half * N_KEYS, N_KEYS), :]
            s_a = _dot_t(sub_a, qh[:, 0:P_HALF])
            s_b = _dot_t(sub_b, qh[:, P_HALF:P_DKEY])
            va, ia = _top16_rows(s_a)
            vb, ib = _top16_rows(s_b)
            keep = [P_TOPK // (i + 1) for i in range(P_TOPK)]
            pad = -sum(keep) % 8
            cand = jnp.concatenate([va[i:i + 1, :] + vb[0:keep[i], :] for i in range(P_TOPK)]
                                   + [jnp.full((pad, N_KEYS), -jnp.inf, F32)], axis=0)
            cidx = jnp.concatenate([ia[i:i + 1, :] * float(N_KEYS) + ib[0:keep[i], :] for i in range(P_TOPK)]
                                   + [jnp.zeros((pad, N_KEYS), F32)], axis=0)
            top, _, eidx = _top16_rows(cand, cidx)
            ex = jnp.exp(top - jnp.max(top, axis=0, keepdims=True))
            gates = ex / jnp.sum(ex, axis=0, keepdims=True)
            r0 = pl.multiple_of(p * P_TOPK, P_TOPK)
            e_scr[pl.ds(r0, P_TOPK), cols] = eidx
            g_scr[pl.ds(r0, P_TOPK), cols] = gates
        return carry

    lax.fori_loop(0, P_HEADS, head_body, 0)
    for half in range(tm // N_KEYS):
        cols = slice(half * N_KEYS, (half + 1) * N_KEYS)
        eidx_ref[cols, :] = e_scr[:, cols].T.astype(I32)
        gate_ref[cols, :] = g_scr[:, cols].T


def _mix_call(x2d, tile0, seq, h_f, h_b, mo, a_out, mod3, mod_row0, mhg, n2, w_m, w_a, w_q, sub_a, sub_b):
    n = h_f.shape[0]
    tm = TOKEN_TILE
    per_seq = seq // tm

    def tok(i):
        return (i, 0)

    def const2(i):
        return (0, 0)

    in_specs = [
        pl.BlockSpec((tm, D_MODEL), lambda i: (tile0 + i, 0)),
        pl.BlockSpec((tm, M_WIDTH), tok), pl.BlockSpec((tm, M_WIDTH), tok), pl.BlockSpec((tm, M_WIDTH), tok),
        pl.BlockSpec((tm, A_WIDTH), tok),
        pl.BlockSpec((1, 6, D_MODEL), lambda i: (mod_row0 + (i // per_seq if mod_row0 else 0), 0, 0)),
        pl.BlockSpec((1, M_WIDTH), const2),
        pl.BlockSpec((1, D_MODEL), const2),
        pl.BlockSpec((M_WIDTH, D_MODEL), const2),
        pl.BlockSpec((A_WIDTH, D_MODEL), const2),
        pl.BlockSpec((D_MODEL, P_HEADS * P_DKEY), const2),
        pl.BlockSpec((N_KEYS, P_HALF), const2),
        pl.BlockSpec((N_KEYS, P_HALF), const2),
    ]
    out_specs = [pl.BlockSpec((tm, D_MODEL), tok), pl.BlockSpec((tm, D_MODEL // 2), tok),
                 pl.BlockSpec((tm, N_SEL), tok), pl.BlockSpec((tm, N_SEL), tok)]
    out_shape = [jax.ShapeDtypeStruct((n, D_MODEL), F32), jax.ShapeDtypeStruct((n, D_MODEL // 2), I32),
                 jax.ShapeDtypeStruct((n, N_SEL), I32), jax.ShapeDtypeStruct((n, N_SEL), F32)]
    return pl.pallas_call(
        _mix_kernel, grid=(n // tm,), in_specs=in_specs, out_specs=out_specs, out_shape=out_shape,
        scratch_shapes=[pltpu.VMEM((P_HEADS, tm, P_DKEY), BF16), pltpu.VMEM((N_SEL, tm), F32),
                        pltpu.VMEM((N_SEL, tm), F32)],
        compiler_params=pltpu.CompilerParams(dimension_semantics=("parallel",)),
        name="mix",
    )(x2d, h_f, h_b, mo, a_out, mod3, mhg, n2, w_m, w_a, w_q, sub_a, sub_b)


SC_LANES = 16
SC_CORES = 2
SC_SUBCORES = 16
SC_WORKERS = SC_CORES * SC_SUBCORES
SC_TOKENS = 16
SC_GROUP = SC_LANES
SC_NGROUPS = N_SEL // SC_GROUP
SC_SLOTS = 4
ROW_WORDS = D_MODEL // 2
SC_DOT_ROWS = 8
SC_DOT_PARTIALS = 2
SC_OWORDS = 16 * SC_LANES
HI_MASK = -65536
PACK_ROWS = 512
SCORE_TOKENS = 1024
SCORE_EXPERTS = 2048
DENSE_LATENT_CHUNKS = (1, 3, 4, 6, 7)
GELU_C0 = 0.7978845608028654
GELU_C1 = 0.044715


def _pack_bf16_pairs(x):
    half = x.shape[1] // 2
    bits = lax.bitcast_convert_type(x.astype(BF16).astype(F32), I32)
    return (bits[:, :half] & HI_MASK) | lax.shift_right_logical(bits[:, half:], jnp.int32(16))


def _pack_tables_kernel(u_ref, v_ref, uv_ref, vp_ref, ub_ref):
    v_words = _pack_bf16_pairs(v_ref[...])
    uv_ref[:, 0:ROW_WORDS] = _pack_bf16_pairs(u_ref[...])
    uv_ref[:, ROW_WORDS:2 * ROW_WORDS] = v_words
    vp_ref[...] = v_words
    ub_ref[...] = u_ref[...].astype(BF16)


def _pack_tables_call(u_tab, v_tab):
    n_exp = u_tab.shape[0]
    blk = pl.BlockSpec((PACK_ROWS, D_MODEL), lambda i: (i, 0))
    half = pl.BlockSpec((PACK_ROWS, ROW_WORDS), lambda i: (i, 0))
    return pl.pallas_call(
        _pack_tables_kernel, grid=(n_exp // PACK_ROWS,), in_specs=[blk, blk], out_specs=[blk, half, blk],
        out_shape=[jax.ShapeDtypeStruct((n_exp, 2 * ROW_WORDS), I32), jax.ShapeDtypeStruct((n_exp, ROW_WORDS), I32),
                   jax.ShapeDtypeStruct((n_exp, D_MODEL), BF16)],
        compiler_params=pltpu.CompilerParams(dimension_semantics=("parallel",)),
        name="pack_tables",
    )(u_tab, v_tab)


def _unpack_bf16_pairs(words):
    hi = lax.bitcast_convert_type(words & HI_MASK, F32)
    lo = lax.bitcast_convert_type(lax.shift_left(words, jnp.int32(16)), F32)
    return jnp.concatenate([hi, lo], axis=1)


def _scores_kernel(x_ref, u_ref, o_ref):
    o_ref[...] = _pack_bf16_pairs(_dot_t(_unpack_bf16_pairs(x_ref[...]).astype(BF16), u_ref[...]))


def _scores_call(h2p, u_bf16):
    n = h2p.shape[0]
    n_exp = u_bf16.shape[0]
    tm = math.gcd(n, SCORE_TOKENS)
    return pl.pallas_call(
        _scores_kernel, grid=(n // tm, n_exp // SCORE_EXPERTS),
        in_specs=[pl.BlockSpec((tm, ROW_WORDS), lambda i, j: (i, 0)),
                  pl.BlockSpec((SCORE_EXPERTS, D_MODEL), lambda i, j: (j, 0))],
        out_specs=pl.BlockSpec((tm, SCORE_EXPERTS // 2), lambda i, j: (i, j)),
        out_shape=jax.ShapeDtypeStruct((n, n_exp // 2), I32),
        compiler_params=pltpu.CompilerParams(dimension_semantics=("parallel", "parallel")),
        name="expert_scores",
    )(h2p, u_bf16)


def _sc_gelu(a):
    z = GELU_C0 * (a + GELU_C1 * (a * a * a))
    tanh = 1.0 - 2.0 / (jnp.exp(2.0 * z) + 1.0)
    return 0.5 * a * (1.0 + tanh)


def _sc_split(words):
    return (plsc.bitcast(words & HI_MASK, F32), plsc.bitcast(lax.shift_left(words, jnp.int32(16)), F32))


def _sc_mul_bf16(a_words, b_words):
    return plsc.bitcast(a_words, BF16) * plsc.bitcast(b_words, BF16)


def _sc_split_sum(p, q):
    return _sc_split(plsc.bitcast(p + q, I32))


def _peer_sc_kernel(h2_hbm, eidx_hbm, gate_hbm, uv_hbm, out_hbm,
                    xbuf, ibuf, gbuf, obuf, uvbuf, mbuf, wbuf, sems, sems_in, sems_out):
    n = h2_hbm.shape[0]
    per_worker = n // SC_WORKERS
    blk_tokens = xbuf.shape[1]
    n_blocks = per_worker // blk_tokens
    wid = lax.axis_index("c") * SC_SUBCORES + lax.axis_index("s")
    lane = lax.iota(I32, SC_LANES)

    def split_item(item):
        return lax.shift_right_logical(item, SC_NGROUPS.bit_length() - 1), item & (SC_NGROUPS - 1)

    def gather_copies(bset, item, slot):
        t, g = split_item(item)
        idx = ibuf[bset, t, pl.ds(g * SC_GROUP, SC_GROUP)]
        return (pltpu.make_async_copy(uv_hbm.at[idx], uvbuf.at[slot], sems.at[slot]),)

    def dots(bset, t, slot):
        zero = jnp.zeros((SC_LANES,), F32)

        @pl.loop(0, SC_GROUP, step=SC_DOT_ROWS)
        def _(r0):
            accs = [[zero] * SC_DOT_PARTIALS for _ in range(SC_DOT_ROWS)]
            for k in range(0, ROW_WORDS // SC_LANES, 4):
                xs = [xbuf[bset, t, pl.ds((k + q) * SC_LANES, SC_LANES)] for q in range(4)]
                for i in range(SC_DOT_ROWS):
                    m = [_sc_mul_bf16(xs[q], uvbuf[slot, r0 + i, pl.ds((k + q) * SC_LANES, SC_LANES)])
                         for q in range(4)]
                    hi, lo = _sc_split_sum(m[0] + m[1], m[2] + m[3])
                    p = (k // 4) % SC_DOT_PARTIALS
                    accs[i][p] = accs[i][p] + (hi + lo)
            for i in range(SC_DOT_ROWS):
                mbuf[r0 + i, :] = functools.reduce(lambda a, b: a + b, accs[i])

        tot = zero
        for c in range(SC_LANES):
            tot = tot + plsc.load_gather(mbuf, [lane, jnp.full((SC_LANES,), c, I32)])
        return tot

    def accumulate(bset, t, slot):
        nv = SC_OWORDS // SC_LANES
        for oc in range(ROW_WORDS // SC_OWORDS):
            w0 = oc * SC_OWORDS
            accs = (tuple(obuf[bset, t, pl.ds(w0 + j * SC_LANES, SC_LANES)] for j in range(nv))
                    + tuple(obuf[bset, t, pl.ds(ROW_WORDS + w0 + j * SC_LANES, SC_LANES)] for j in range(nv)))

            def row_quad(rq, accs):
                r = 4 * rq
                ws = [plsc.load_gather(wbuf, [jnp.full((SC_LANES,), r + q, I32)]) for q in range(4)]
                his, los = [], []
                for j in range(nv):
                    m = [_sc_mul_bf16(ws[q], uvbuf[slot, r + q, pl.ds(ROW_WORDS + w0 + j * SC_LANES, SC_LANES)])
                         for q in range(4)]
                    hi, lo = _sc_split_sum(m[0] + m[1], m[2] + m[3])
                    his.append(accs[j] + hi)
                    los.append(accs[nv + j] + lo)
                return tuple(his) + tuple(los)

            accs = lax.fori_loop(0, SC_GROUP // 4, row_quad, accs)
            for j in range(nv):
                obuf[bset, t, pl.ds(w0 + j * SC_LANES, SC_LANES)] = accs[j]
                obuf[bset, t, pl.ds(ROW_WORDS + w0 + j * SC_LANES, SC_LANES)] = accs[nv + j]

    def pack_weights(w):
        bits = plsc.bitcast(w, I32)
        rounded = (bits + 0x7FFF + (lax.shift_right_logical(bits, jnp.int32(16)) & 1)) & HI_MASK
        return rounded | lax.shift_right_logical(rounded, jnp.int32(16))

    n_items = blk_tokens * SC_NGROUPS

    def block_rows(blk):
        return pl.ds(pl.multiple_of(wid * per_worker + blk * blk_tokens, blk_tokens), blk_tokens)

    def load_copies(blk, bset):
        rows = block_rows(blk)
        return (pltpu.make_async_copy(h2_hbm.at[rows], xbuf.at[bset], sems_in.at[bset]),
                pltpu.make_async_copy(eidx_hbm.at[rows], ibuf.at[bset], sems_in.at[bset]),
                pltpu.make_async_copy(gate_hbm.at[rows], gbuf.at[bset], sems_in.at[bset]))

    def store_copy(blk, bset):
        return pltpu.make_async_copy(obuf.at[bset], out_hbm.at[block_rows(blk)], sems_out.at[bset])

    for c in load_copies(0, 0):
        c.start()

    @pl.loop(0, n_blocks)
    def _(blk):
        bset = blk & 1
        for c in load_copies(blk, bset):
            c.wait()

        @pl.when(blk + 1 < n_blocks)
        def _():
            for c in load_copies(blk + 1, 1 - bset):
                c.start()

        @pl.when(blk >= 2)
        def _():
            store_copy(blk - 2, bset).wait()

        @pl.loop(0, blk_tokens)
        def _(t):
            zero = jnp.zeros((SC_LANES,), F32)
            for j in range(D_MODEL // SC_LANES):
                obuf[bset, t, pl.ds(j * SC_LANES, SC_LANES)] = zero

        for ahead in range(SC_SLOTS - 1):
            for c in gather_copies(bset, ahead, ahead):
                c.start()

        @pl.loop(0, n_items)
        def _(item):
            t, g = split_item(item)
            slot = item & (SC_SLOTS - 1)
            ahead = item + (SC_SLOTS - 1)

            @pl.when(ahead < n_items)
            def _():
                for c in gather_copies(bset, ahead, ahead & (SC_SLOTS - 1)):
                    c.start()

            for c in gather_copies(bset, item, slot):
                c.wait()
            a = dots(bset, t, slot)
            wbuf[...] = pack_weights(gbuf[bset, t, pl.ds(g * SC_GROUP, SC_GROUP)] * _sc_gelu(a))
            accumulate(bset, t, slot)

        store_copy(blk, bset).start()

    for blk in range(max(n_blocks - 2, 0), n_blocks):
        store_copy(blk, blk & 1).wait()


def _peer_experts(h2p, eidx, gates, uv_pack):
    n = h2p.shape[0]
    assert n % (8 * SC_WORKERS) == 0
    blk_tokens = math.gcd(n // SC_WORKERS, SC_TOKENS)
    mesh = plsc.VectorSubcoreMesh(core_axis_name="c", subcore_axis_name="s")
    fn = pl.kernel(
        _peer_sc_kernel,
        out_type=jax.ShapeDtypeStruct((n, D_MODEL), F32),
        mesh=mesh,
        scratch_types=[
            pltpu.VMEM((2, blk_tokens, ROW_WORDS), I32),
            pltpu.VMEM((2, blk_tokens, N_SEL), I32),
            pltpu.VMEM((2, blk_tokens, N_SEL), F32),
            pltpu.VMEM((2, blk_tokens, D_MODEL), F32),
            pltpu.VMEM((SC_SLOTS, SC_GROUP, 2 * ROW_WORDS), I32),
            pltpu.VMEM((SC_GROUP, SC_LANES), F32),
            pltpu.VMEM((SC_LANES,), I32),
            pltpu.SemaphoreType.DMA((SC_SLOTS,)), pltpu.SemaphoreType.DMA((2,)), pltpu.SemaphoreType.DMA((2,)),
        ],
        compiler_params=pltpu.CompilerParams(needs_layout_passes=False),
        cost_estimate=pl.CostEstimate(
            flops=4 * n * N_SEL * D_MODEL, transcendentals=n * N_SEL,
            bytes_accessed=4 * (2 * n * N_SEL * ROW_WORDS + n * ROW_WORDS + n * D_MODEL + 2 * n * N_SEL)),
        name="peer_experts",
    )
    return fn(h2p, eidx, gates, uv_pack)


def _peer_sc_scored_kernel(a_hbm, eidx_hbm, gate_hbm, v_hbm, out_hbm,
                           abuf, ibuf, gbuf, obuf, vbuf, wbuf, sems, sems_a):
    n = a_hbm.shape[0]
    per_worker = n // SC_WORKERS
    blk_tokens = ibuf.shape[0]
    wid = lax.axis_index("c") * SC_SUBCORES + lax.axis_index("s")

    def split_item(item):
        return lax.shift_right_logical(item, SC_NGROUPS.bit_length() - 1), item & (SC_NGROUPS - 1)

    def gather_copy(item, slot):
        t, g = split_item(item)
        idx = ibuf[t, pl.ds(g * SC_GROUP, SC_GROUP)]
        return pltpu.make_async_copy(v_hbm.at[idx], vbuf.at[slot], sems.at[slot])

    def accumulate(t, slot):
        nv = SC_OWORDS // SC_LANES
        for oc in range(ROW_WORDS // SC_OWORDS):
            w0 = oc * SC_OWORDS
            accs = (tuple(obuf[t, pl.ds(w0 + j * SC_LANES, SC_LANES)] for j in range(nv))
                    + tuple(obuf[t, pl.ds(ROW_WORDS + w0 + j * SC_LANES, SC_LANES)] for j in range(nv)))

            def row_quad(rq, accs):
                r = 4 * rq
                ws = [plsc.load_gather(wbuf, [jnp.full((SC_LANES,), r + q, I32)]) for q in range(4)]
                his, los = [], []
                for j in range(nv):
                    m = [_sc_mul_bf16(ws[q], vbuf[slot, r + q, pl.ds(w0 + j * SC_LANES, SC_LANES)])
                         for q in range(4)]
                    hi, lo = _sc_split_sum(m[0] + m[1], m[2] + m[3])
                    his.append(accs[j] + hi)
                    los.append(accs[nv + j] + lo)
                return tuple(his) + tuple(los)

            accs = lax.fori_loop(0, SC_GROUP // 4, row_quad, accs)
            for j in range(nv):
                obuf[t, pl.ds(w0 + j * SC_LANES, SC_LANES)] = accs[j]
                obuf[t, pl.ds(ROW_WORDS + w0 + j * SC_LANES, SC_LANES)] = accs[nv + j]

    def pack_weights(w):
        bits = plsc.bitcast(w, I32)
        rounded = (bits + 0x7FFF + (lax.shift_right_logical(bits, jnp.int32(16)) & 1)) & HI_MASK
        return rounded | lax.shift_right_logical(rounded, jnp.int32(16))

    n_items = blk_tokens * SC_NGROUPS

    @pl.loop(0, per_worker // blk_tokens)
    def _(blk):
        tok0 = pl.multiple_of(wid * per_worker + blk * blk_tokens, blk_tokens)

        def score_copy(t):
            return pltpu.make_async_copy(a_hbm.at[tok0 + t], abuf.at[t & 1], sems_a.at[t & 1])

        score_copy(0).start()
        pltpu.sync_copy(eidx_hbm.at[pl.ds(tok0, blk_tokens)], ibuf)
        pltpu.sync_copy(gate_hbm.at[pl.ds(tok0, blk_tokens)], gbuf)

        @pl.loop(0, blk_tokens)
        def _(t):
            zero = jnp.zeros((SC_LANES,), F32)
            for j in range(D_MODEL // SC_LANES):
                obuf[t, pl.ds(j * SC_LANES, SC_LANES)] = zero

        for ahead in range(SC_SLOTS - 1):
            gather_copy(ahead, ahead).start()

        @pl.loop(0, n_items)
        def _(item):
            t, g = split_item(item)
            slot = item & (SC_SLOTS - 1)
            ahead = item + (SC_SLOTS - 1)

            @pl.when(ahead < n_items)
            def _():
                gather_copy(ahead, ahead & (SC_SLOTS - 1)).start()

            @pl.when(g == 0)
            def _():
                score_copy(t).wait()

                @pl.when(t + 1 < blk_tokens)
                def _():
                    score_copy(t + 1).start()

            gather_copy(item, slot).wait()
            idx = ibuf[t, pl.ds(g * SC_GROUP, SC_GROUP)]
            half = SCORE_EXPERTS // 2
            word = plsc.load_gather(
                abuf, [jnp.full((SC_LANES,), t & 1, I32),
                       lax.shift_right_logical(idx, jnp.int32(SCORE_EXPERTS.bit_length() - 1)) * half
                       + (idx & (half - 1))])
            hi, lo = _sc_split(word)
            a = jnp.where((idx & half) == 0, hi, lo)
            wbuf[...] = pack_weights(gbuf[t, pl.ds(g * SC_GROUP, SC_GROUP)] * _sc_gelu(a))
            accumulate(t, slot)

        pltpu.sync_copy(obuf, out_hbm.at[pl.ds(tok0, blk_tokens)])


def _peer_experts_scored(scores, eidx, gates, v_pack):
    n, n_exp = scores.shape
    assert n % (8 * SC_WORKERS) == 0
    blk_tokens = math.gcd(n // SC_WORKERS, 2 * SC_TOKENS)
    mesh = plsc.VectorSubcoreMesh(core_axis_name="c", subcore_axis_name="s")
    fn = pl.kernel(
        _peer_sc_scored_kernel,
        out_type=jax.ShapeDtypeStruct((n, D_MODEL), F32),
        mesh=mesh,
        scratch_types=[
            pltpu.VMEM((2, n_exp), I32),
            pltpu.VMEM((blk_tokens, N_SEL), I32),
            pltpu.VMEM((blk_tokens, N_SEL), F32),
            pltpu.VMEM((blk_tokens, D_MODEL), F32),
            pltpu.VMEM((SC_SLOTS, SC_GROUP, ROW_WORDS), I32),
            pltpu.VMEM((SC_LANES,), I32),
            pltpu.SemaphoreType.DMA((SC_SLOTS,)), pltpu.SemaphoreType.DMA((2,)),
        ],
        compiler_params=pltpu.CompilerParams(needs_layout_passes=False),
        cost_estimate=pl.CostEstimate(
            flops=2 * n * N_SEL * D_MODEL, transcendentals=n * N_SEL,
            bytes_accessed=4 * (n * N_SEL * ROW_WORDS + n * n_exp + n * D_MODEL + 2 * n * N_SEL)),
        name="peer_experts_scored",
    )
    return fn(scores, eidx, gates, v_pack)


def _resid_kernel(x1_ref, p_ref, mod_ref, o_ref):
    o_ref[...] = x1_ref[...] + mod_ref[0, 5:6, :] * p_ref[...]


def _resid_call(x1, peer_out, seq, mod3, mod_row0):
    n = x1.shape[0]
    tm = TOKEN_TILE
    per_seq = seq // tm
    tok = pl.BlockSpec((tm, D_MODEL), lambda i: (i, 0))
    return pl.pallas_call(
        _resid_kernel, grid=(n // tm,),
        in_specs=[tok, tok,
                  pl.BlockSpec((1, 6, D_MODEL), lambda i: (mod_row0 + (i // per_seq if mod_row0 else 0), 0, 0))],
        out_specs=tok, out_shape=jax.ShapeDtypeStruct((n, D_MODEL), F32),
        compiler_params=pltpu.CompilerParams(dimension_semantics=("parallel",)),
        name="resid",
    )(x1, peer_out, mod3)


def _rope_tables(seq, rotate):
    if not rotate:
        return jnp.ones((seq, KV_WIDTH), F32), jnp.zeros((seq, KV_WIDTH), F32)
    quarter = A_DH // 4
    t = jnp.arange(seq)
    row = (t // GRID_W).astype(F32)
    col = (t % GRID_W).astype(F32)
    inv = ROPE_BASE ** (-jnp.arange(quarter, dtype=F32) / quarter)
    d = jnp.arange(A_DH)
    pos = jnp.where(d[None, :] < A_DH // 2, row[:, None], col[:, None])
    ang = pos * inv[d % quarter][None, :]
    sign = jnp.where((d % (A_DH // 2)) < quarter, -1.0, 1.0).astype(F32)
    cos = jnp.cos(ang)
    sin = jnp.sin(ang) * sign[None, :]
    return jnp.tile(cos, (1, KV_WIDTH // A_DH)), jnp.tile(sin, (1, KV_WIDTH // A_DH))


def _run_chunk(x, b0, batch, mod3, mod_row0, prm, cache, rotate, gate_on, dense_scores):
    (n1, n2, w_main, w_g, w_gt, b_g, b_gt, mhg, qg_t, kg_t, bd, sink, w_m, w_a, w_q, sub_a, sub_b,
     (uv_pack, v_pack, u_bf16)) = prm
    seq = x.shape[1]
    n = batch * seq
    x2d = x.reshape(x.shape[0] * seq, D_MODEL)
    tile0 = b0 * seq // TOKEN_TILE
    cos, sin = _rope_tables(seq, rotate)
    if gate_on is not None:
        (cos, sin), _ = lax.optimization_barrier(((cos, sin), gate_on))
    mq, mk, mv, mo, gcol, grow, aq, ak, av = _inproj_call(
        x2d, tile0, n, seq, mod3, mod_row0, n1, w_main, w_g, w_gt, b_g, b_gt, qg_t, kg_t, bd, cos, sin)
    kc, vc, c0, m0 = cache
    h_f, h_b, c_fin, m_fin = _mlstm_call(mq, mk, mv, gcol, grow, c0, m0, batch, seq)
    if kc is None:
        a_out = _attn_ctx_call(sink, aq, ak, av, batch, seq)
    else:
        a_out = _attn_lat_call(sink, aq, ak, av, kc, vc, batch, seq)
    x1, h2p, eidx, gates = _mix_call(x2d, tile0, seq, h_f, h_b, mo, a_out, mod3, mod_row0, mhg, n2, w_m, w_a,
                                     w_q, sub_a, sub_b)
    if dense_scores:
        peer_out = _peer_experts_scored(_scores_call(h2p, u_bf16), eidx, gates, v_pack)
    else:
        peer_out = _peer_experts(h2p, eidx, gates, uv_pack)
    y = _resid_call(x1, peer_out, seq, mod3, mod_row0).reshape(batch, seq, D_MODEL)
    return y, h2p, peer_out, ak, av, c_fin, m_fin


def _pack_state(C, n_vec, m):
    b = C.shape[0]
    caug = jnp.concatenate([C, jnp.broadcast_to(n_vec[..., None], C.shape)], axis=-1)
    caug = caug.reshape(b, 2 * M_HEADS, M_DH, 2 * M_DH)
    m_rep = jnp.broadcast_to(m.reshape(b, 2 * M_HEADS, 1, 1), (b, 2 * M_HEADS, 8, M_DH))
    return caug.astype(F32), m_rep.astype(F32)


def kernel(x_prompt, x_sample, c, cache_attn_k, cache_attn_v, state_mlstm_C, state_mlstm_n, state_mlstm_m,
           c_ctx, w_ada, b_ada, norm1_g, norm2_g, w_in, b_gates, mh_norm_g, q_norm_g, k_norm_g, sink_logits,
           w_out, peer_w_q, peer_sub_a, peer_sub_b, peer_u, peer_v):
    depth = w_ada.shape[0]
    assert depth == 1
    batch, seq, _ = x_prompt.shape
    dec_batch, dec_seq, _ = x_sample.shape
    assert dec_batch + 1 <= MOD_ROWS and batch % sum(CTX_CHUNK_WEIGHTS) == 0 and dec_batch % LATENT_CHUNKS == 0
    l = 0

    cond = jnp.concatenate([c_ctx[None, :], c, jnp.zeros((MOD_ROWS - 1 - dec_batch, D_MODEL), F32)], axis=0)
    mod3 = _ada_call(cond, w_ada[l], b_ada[l]).reshape(MOD_ROWS, 6, D_MODEL)

    wi = w_in[l]
    g0 = 4 * M_WIDTH
    w_main = jnp.concatenate([wi[:, :g0], wi[:, g0 + N_GATES:]], axis=1).astype(BF16)
    w_g = wi[:, g0:g0 + N_GATES]
    seg = jnp.arange(A_WIDTH) // A_DH
    bd = jnp.where(seg[:, None] == seg[None, :], 1.0 / A_DH, 0.0).astype(F32)
    prm = (norm1_g[l][None, :], norm2_g[l][None, :], w_main, w_g, w_g.T, b_gates[l][None, :], b_gates[l][:, None],
           mh_norm_g[l][None, :], jnp.tile(q_norm_g[l], A_HEADS)[None, :], jnp.tile(k_norm_g[l], A_KV)[None, :], bd,
           sink_logits[l], w_out[l][:M_WIDTH].astype(BF16), w_out[l][M_WIDTH:].astype(BF16),
           peer_w_q[l].astype(BF16), peer_sub_a[l].astype(BF16), peer_sub_b[l].astype(BF16),
           _pack_tables_call(peer_u[l], peer_v[l]))

    zeros_c = jnp.zeros((batch, 2, M_HEADS, M_DH, M_DH), F32)
    c0, m0 = _pack_state(zeros_c, zeros_c[..., 0], jnp.full((batch, 2, M_HEADS), NEG, F32))
    c0s, m0s = _pack_state(state_mlstm_C[:, l], state_mlstm_n[:, l], state_mlstm_m[:, l])
    past = cache_attn_k.shape[2]
    kc = cache_attn_k[:, l].reshape(dec_batch, past, KV_WIDTH)
    vc = cache_attn_v[:, l].reshape(dec_batch, past, KV_WIDTH)

    jobs = []
    ctx_sizes = [batch * f // sum(CTX_CHUNK_WEIGHTS) for f in CTX_CHUNK_WEIGHTS]
    assert sum(ctx_sizes) == batch
    b0 = 0
    for size in ctx_sizes:
        b1 = b0 + size
        jobs.append((x_prompt, b0, size, 0, (None, None, c0[b0:b1], m0[b0:b1]), False, False))
        b0 = b1
    for ci, b0 in enumerate(range(0, dec_batch, dec_batch // LATENT_CHUNKS)):
        b1 = b0 + dec_batch // LATENT_CHUNKS
        jobs.append((x_sample, b0, b1 - b0, 1 + b0, (kc[b0:b1], vc[b0:b1], c0s[b0:b1], m0s[b0:b1]), True,
                     ci in DENSE_LATENT_CHUNKS))
    outs = []
    for i, (x_all, b0, nb, mod_row0, cache, rotate, dense) in enumerate(jobs):
        gate = tuple(g for g in (outs[i - 1][1] if i >= 1 else None,
                                 outs[i - EXPERT_LAG][2] if i >= EXPERT_LAG else None) if g is not None)
        outs.append(_run_chunk(x_all, b0, nb, mod3, mod_row0, prm, cache, rotate, gate or None, dense))
    ctx, lat = outs[:len(ctx_sizes)], outs[len(ctx_sizes):]
    y_p = jnp.concatenate([o[0] for o in ctx], axis=0)
    y_s = jnp.concatenate([o[0] for o in lat], axis=0)
    k_new = jnp.concatenate([o[3] for o in ctx], axis=0)
    v_new = jnp.concatenate([o[4] for o in ctx], axis=0)
    c_fin = jnp.concatenate([o[5] for o in ctx], axis=0)
    m_fin = jnp.concatenate([o[6] for o in ctx], axis=0)

    c_fin = c_fin.reshape(batch, 2, M_HEADS, M_DH, 2 * M_DH)
    new_c = c_fin[..., :M_DH][:, None]
    new_n = c_fin[..., M_DH][:, None]
    new_m = m_fin[:, :, 0, 0].reshape(batch, 2, M_HEADS)[:, None]
    new_k = k_new.reshape(batch, 1, seq, A_KV, A_DH)
    new_v = v_new.reshape(batch, 1, seq, A_KV, A_DH)
    return y_p, y_s, new_k, new_v, new_c, new_n, new_m
```

```python
import functools
import math

import jax
import jax.numpy as jnp
from jax import lax
from jax.experimental import pallas as pl
from jax.experimental.pallas import tpu as pltpu
from jax.experimental.pallas import tpu_sc as plsc

F32 = jnp.float32
BF16 = jnp.bfloat16
I32 = jnp.int32
HI = lax.Precision.HIGHEST

D_MODEL = 1024
EPS = 1e-6
NEG = -1e30
GRID_W = 64
M_HEADS = 4
M_WIDTH = 512
M_DH = 128
A_HEADS = 8
A_KV = 2
A_REP = 4
A_DH = 64
A_WIDTH = 512
KV_WIDTH = A_KV * A_DH
BLOCK = 128
ROPE_BASE = 10000.0
N_KEYS = 128
P_HEADS = 8
P_DKEY = 256
P_HALF = 128
P_TOPK = 16
N_SEL = P_HEADS * P_TOPK
N_GATES = 4 * M_HEADS
MAIN_COLS = 4 * M_WIDTH + A_WIDTH + 2 * KV_WIDTH
MOD_ROWS = 16

TOKEN_TILE = 256
MLSTM_CHUNK = 128
ADA_COL_TILE = 768
CTX_CHUNK_WEIGHTS = (1, 2, 3, 4, 6)
EXPERT_LAG = 3
LATENT_CHUNKS = 8


def _sigmoid(x):
    return 1.0 / (1.0 + jnp.exp(-x))


def _log_sigmoid(x):
    return jnp.minimum(x, 0.0) - jnp.log1p(jnp.exp(-jnp.abs(x)))


def _dot_t(a, b, precision=None):
    return lax.dot_general(a, b, (((1,), (1,)), ((), ())), precision=precision,
                           preferred_element_type=F32)


def _ada_kernel(c_ref, w_ref, b_ref, o_ref):
    c = c_ref[...]
    s = c * _sigmoid(c)
    o_ref[...] = jnp.dot(s, w_ref[...], precision=HI, preferred_element_type=F32) + b_ref[...]


def _ada_call(cond, w_ada, b_ada):
    n_out = w_ada.shape[1]
    return pl.pallas_call(
        _ada_kernel,
        grid=(n_out // ADA_COL_TILE,),
        in_specs=[pl.BlockSpec((MOD_ROWS, D_MODEL), lambda j: (0, 0)),
                  pl.BlockSpec((D_MODEL, ADA_COL_TILE), lambda j: (0, j)),
                  pl.BlockSpec((1, ADA_COL_TILE), lambda j: (0, j))],
        out_specs=pl.BlockSpec((MOD_ROWS, ADA_COL_TILE), lambda j: (0, j)),
        out_shape=jax.ShapeDtypeStruct((MOD_ROWS, n_out), F32),
        name="ada",
    )(cond, w_ada, b_ada.reshape(1, n_out))


def _swap16(x):
    n = x.shape[-1]
    lane = lax.broadcasted_iota(I32, x.shape, x.ndim - 1)
    return jnp.where((lane & 16) == 0, pltpu.roll(x, n - 16, x.ndim - 1), pltpu.roll(x, 16, x.ndim - 1))


def _inproj_kernel(x_ref, mod_ref, n1_ref, w_ref, wg_ref, wgt_ref, bg_ref, bgt_ref, qg_ref, kg_ref,
                   bd_ref, cos_ref, sin_ref,
                   mq_ref, mk_ref, mv_ref, mo_ref, gc_ref, gr_ref, aq_ref, ak_ref, av_ref):
    x = x_ref[...]
    h = x * lax.rsqrt(jnp.mean(x * x, axis=-1, keepdims=True) + EPS) * n1_ref[...]
    h = h * (1.0 + mod_ref[0, 1:2, :]) + mod_ref[0, 0:1, :]
    z = jnp.dot(h.astype(BF16), w_ref[...], preferred_element_type=F32)

    mq_ref[...] = (z[:, 0:M_WIDTH] * (M_DH ** -0.5)).astype(BF16)
    mk_ref[...] = z[:, M_WIDTH:2 * M_WIDTH].astype(BF16)
    mv_ref[...] = z[:, 2 * M_WIDTH:3 * M_WIDTH].astype(BF16)
    mo_ref[...] = z[:, 3 * M_WIDTH:4 * M_WIDTH]

    g = jnp.dot(h, wg_ref[...], precision=HI, preferred_element_type=F32) + bg_ref[...]
    kind = lax.broadcasted_iota(I32, g.shape, 1) // M_HEADS
    gc_ref[...] = jnp.where((kind & 1) == 1, _log_sigmoid(g), g)
    gt = _dot_t(wgt_ref[...], h, precision=HI) + bgt_ref[...]
    kind_t = lax.broadcasted_iota(I32, gt.shape, 0) // M_HEADS
    gr_ref[...] = jnp.where((kind_t & 1) == 1, _log_sigmoid(gt), gt)

    o = 4 * M_WIDTH
    aq = z[:, o:o + A_WIDTH]
    ak = z[:, o + A_WIDTH:o + A_WIDTH + KV_WIDTH]
    av_ref[...] = z[:, o + A_WIDTH + KV_WIDTH:o + A_WIDTH + 2 * KV_WIDTH]
    bd = bd_ref[...]
    cos = cos_ref[...]
    sin = sin_ref[...]
    aq = aq * lax.rsqrt(jnp.dot(aq * aq, bd, precision=HI, preferred_element_type=F32) + EPS) * qg_ref[...]
    cos4 = jnp.concatenate([cos] * (A_WIDTH // KV_WIDTH), axis=1)
    sin4 = jnp.concatenate([sin] * (A_WIDTH // KV_WIDTH), axis=1)
    aq = (aq * cos4 + _swap16(aq) * sin4) * (A_DH ** -0.5)
    ak = ak * lax.rsqrt(jnp.dot(ak * ak, bd[0:KV_WIDTH, 0:KV_WIDTH], precision=HI,
                                preferred_element_type=F32) + EPS) * kg_ref[...]
    ak_ref[...] = ak * cos + _swap16(ak) * sin

    lane = lax.broadcasted_iota(I32, (aq.shape[0], KV_WIDTH), 1)
    for hd in range(A_HEADS):
        grp = hd // A_REP
        blk = aq[:, (hd // 2) * KV_WIDTH:(hd // 2 + 1) * KV_WIDTH]
        if hd % 2 != grp:
            blk = pltpu.roll(blk, A_DH, 1)
        keep = (lane >= grp * A_DH) & (lane < (grp + 1) * A_DH)
        aq_ref[hd] = jnp.where(keep, blk, 0.0).astype(BF16)


def _inproj_call(x2d, tile0, n, seq, mod3, mod_row0, n1, w_main, w_g, w_gt, b_g, b_gt, qg_t, kg_t, bd, cos, sin):
    tm = TOKEN_TILE
    per_seq = seq // tm

    def tok(i):
        return (i, 0)

    def const2(i):
        return (0, 0)

    in_specs = [
        pl.BlockSpec((tm, D_MODEL), lambda i: (tile0 + i, 0)),
        pl.BlockSpec((1, 6, D_MODEL), lambda i: (mod_row0 + (i // per_seq if mod_row0 else 0), 0, 0)),
        pl.BlockSpec((1, D_MODEL), const2),
        pl.BlockSpec((D_MODEL, MAIN_COLS), const2),
        pl.BlockSpec((D_MODEL, N_GATES), const2),
        pl.BlockSpec((N_GATES, D_MODEL), const2),
        pl.BlockSpec((1, N_GATES), const2),
        pl.BlockSpec((N_GATES, 1), const2),
        pl.BlockSpec((1, A_WIDTH), const2),
        pl.BlockSpec((1, KV_WIDTH), const2),
        pl.BlockSpec((A_WIDTH, A_WIDTH), const2),
        pl.BlockSpec((tm, KV_WIDTH), lambda i: (i % per_seq, 0)),
        pl.BlockSpec((tm, KV_WIDTH), lambda i: (i % per_seq, 0)),
    ]
    out_specs = [
        pl.BlockSpec((tm, M_WIDTH), tok),
        pl.BlockSpec((tm, M_WIDTH), tok),
        pl.BlockSpec((tm, M_WIDTH), tok),
        pl.BlockSpec((tm, M_WIDTH), tok),
        pl.BlockSpec((tm, N_GATES), tok),
        pl.BlockSpec((N_GATES, tm), lambda i: (0, i)),
        pl.BlockSpec((A_HEADS, tm, KV_WIDTH), lambda i: (0, i, 0)),
        pl.BlockSpec((tm, KV_WIDTH), tok),
        pl.BlockSpec((tm, KV_WIDTH), tok),
    ]
    out_shape = [
        jax.ShapeDtypeStruct((n, M_WIDTH), BF16),
        jax.ShapeDtypeStruct((n, M_WIDTH), BF16),
        jax.ShapeDtypeStruct((n, M_WIDTH), BF16),
        jax.ShapeDtypeStruct((n, M_WIDTH), F32),
        jax.ShapeDtypeStruct((n, N_GATES), F32),
        jax.ShapeDtypeStruct((N_GATES, n), F32),
        jax.ShapeDtypeStruct((A_HEADS, n, KV_WIDTH), BF16),
        jax.ShapeDtypeStruct((n, KV_WIDTH), F32),
        jax.ShapeDtypeStruct((n, KV_WIDTH), F32),
    ]
    return pl.pallas_call(
        _inproj_kernel, grid=(n // tm,), in_specs=in_specs, out_specs=out_specs, out_shape=out_shape,
        compiler_params=pltpu.CompilerParams(dimension_semantics=("parallel",)),
        name="inproj",
    )(x2d, mod3, n1, w_main, w_g, w_gt, b_g, b_gt, qg_t, kg_t, bd, cos, sin)


def _mlstm_chain(q, k, v, li_c, lf_c, li_r, lf_r, caug, m, tri, tri_t, mask, reverse):
    L = q.shape[0]
    last = 0 if reverse else L - 1
    b_c = jnp.dot(tri, jnp.broadcast_to(lf_c, (L, L)), precision=HI, preferred_element_type=F32)
    b_r = jnp.dot(jnp.broadcast_to(lf_r, (8, L)), tri_t, precision=HI, preferred_element_type=F32)[0:1, :]
    a_inter = b_c[:, 0:1] + m
    d = jnp.where(mask, b_c - b_r + li_r, -jnp.inf)
    m_t = jnp.maximum(a_inter, jnp.max(d, axis=1, keepdims=True))
    w_inter = jnp.exp(a_inter - m_t)
    s = _dot_t(q, k) * jnp.exp(d - m_t)
    qc = jnp.dot(q, caug.astype(BF16), preferred_element_type=F32)
    num = jnp.dot(s.astype(BF16), v, preferred_element_type=F32) + w_inter * qc[:, 0:M_DH]
    den = jnp.sum(s, axis=1, keepdims=True) + w_inter * qc[:, M_DH:M_DH + 1]
    den = jnp.maximum(jnp.abs(den), jnp.exp(-m_t))
    h = num / den
    m_new = m_t[last:last + 1, :]
    b_last = b_c[last:last + 1, 0:1]
    g_c = jnp.exp(b_last - b_c[:, 0:1] + li_c - m_new)
    decay = jnp.exp(b_last + m - m_new)
    kw = (k.astype(F32) * g_c).astype(BF16)
    vaug = jnp.concatenate([v, jnp.ones_like(v)], axis=1)
    upd = lax.dot_general(kw, vaug, (((0,), (0,)), ((), ())), preferred_element_type=F32)
    return h, decay * caug + upd, m_new


def _mlstm_kernel(qf_ref, kf_ref, vf_ref, gcf_ref, grf_ref, qb_ref, kb_ref, vb_ref, gcb_ref, grb_ref,
                  c0_ref, m0_ref, hf_ref, hb_ref, cfin_ref, mfin_ref, c_scr, m_scr):
    c = pl.program_id(1)
    nc = pl.num_programs(1)
    L = qf_ref.shape[0]

    @pl.when(c == 0)
    def _():
        c_scr[...] = c0_ref[0]
        m_scr[...] = m0_ref[0]

    row = lax.broadcasted_iota(I32, (L, L), 0)
    col = lax.broadcasted_iota(I32, (L, L), 1)
    lower = row >= col
    upper = row <= col
    lower_f = lower.astype(F32)
    upper_f = upper.astype(F32)

    for direction in range(2):
        reverse = direction == 1
        q_ref, k_ref, v_ref, gc_ref, gr_ref, h_ref = (
            (qb_ref, kb_ref, vb_ref, gcb_ref, grb_ref, hb_ref) if reverse
            else (qf_ref, kf_ref, vf_ref, gcf_ref, grf_ref, hf_ref))
        tri, tri_t, mask = (upper_f, lower_f, upper) if reverse else (lower_f, upper_f, lower)
        gc = gc_ref[...]
        gr = gr_ref[...]
        for hd in range(M_HEADS):
            ch = direction * M_HEADS + hd
            sl = slice(hd * M_DH, (hd + 1) * M_DH)
            ci = 2 * direction * M_HEADS + hd
            cf = ci + M_HEADS
            h, caug, m_new = _mlstm_chain(
                q_ref[:, sl], k_ref[:, sl], v_ref[:, sl],
                gc[:, ci:ci + 1], gc[:, cf:cf + 1], gr[ci:ci + 1, :], gr[cf:cf + 1, :],
                c_scr[ch], m_scr[ch][0:1, 0:1], tri, tri_t, mask, reverse)
            h_ref[:, sl] = h
            c_scr[ch] = caug
            m_scr[ch] = jnp.broadcast_to(m_new, m_scr.shape[1:])

    @pl.when(c == nc - 1)
    def _():
        cfin_ref[0] = c_scr[...]
        mfin_ref[0] = m_scr[...]


def _mlstm_call(mq, mk, mv, gcol, grow, c0, m0, batch, seq):
    n = mq.shape[0]
    L = MLSTM_CHUNK
    nc = seq // L
    n_ch = 2 * M_HEADS

    def fwd(b, c):
        return (b * nc + c, 0)

    def bwd(b, c):
        return (b * nc + nc - 1 - c, 0)

    def fwd_t(b, c):
        return (0, b * nc + c)

    def bwd_t(b, c):
        return (0, b * nc + nc - 1 - c)

    tok = pl.BlockSpec((L, M_WIDTH), fwd)
    tok_b = pl.BlockSpec((L, M_WIDTH), bwd)
    in_specs = [tok, tok, tok, pl.BlockSpec((L, N_GATES), fwd), pl.BlockSpec((N_GATES, L), fwd_t),
                tok_b, tok_b, tok_b, pl.BlockSpec((L, N_GATES), bwd), pl.BlockSpec((N_GATES, L), bwd_t),
                pl.BlockSpec((1, n_ch, M_DH, 2 * M_DH), lambda b, c: (b, 0, 0, 0)),
                pl.BlockSpec((1, n_ch, 8, M_DH), lambda b, c: (b, 0, 0, 0))]
    out_specs = [tok, tok_b,
                 pl.BlockSpec((1, n_ch, M_DH, 2 * M_DH), lambda b, c: (b, 0, 0, 0)),
                 pl.BlockSpec((1, n_ch, 8, M_DH), lambda b, c: (b, 0, 0, 0))]
    out_shape = [jax.ShapeDtypeStruct((n, M_WIDTH), F32), jax.ShapeDtypeStruct((n, M_WIDTH), F32),
                 jax.ShapeDtypeStruct((batch, n_ch, M_DH, 2 * M_DH), F32),
                 jax.ShapeDtypeStruct((batch, n_ch, 8, M_DH), F32)]
    return pl.pallas_call(
        _mlstm_kernel, grid=(batch, nc), in_specs=in_specs, out_specs=out_specs, out_shape=out_shape,
        scratch_shapes=[pltpu.VMEM((n_ch, M_DH, 2 * M_DH), F32), pltpu.VMEM((n_ch, 8, M_DH), F32)],
        compiler_params=pltpu.CompilerParams(dimension_semantics=("parallel", "arbitrary")),
        name="mlstm",
    )(mq, mk, mv, gcol, grow, mq, mk, mv, gcol, grow, c0, m0)


def _sink_column(sink_ref, grp, rows_per_head):
    return jnp.concatenate(
        [jnp.full((rows_per_head, 1), sink_ref[grp * A_REP + r], F32) for r in range(A_REP)], axis=0)


def _store_heads(out_ref, o, grp, rows_per_head):
    for r in range(A_REP):
        hd = grp * A_REP + r
        out_ref[:, hd * A_DH:(hd + 1) * A_DH] = o[r * rows_per_head:(r + 1) * rows_per_head,
                                                  grp * A_DH:(grp + 1) * A_DH].astype(out_ref.dtype)


def _attn_ctx_kernel(sink_ref, q_ref, k_ref, v_ref, out_ref):
    s_len = k_ref.shape[0]
    k = k_ref[...].astype(BF16)
    v = v_ref[...].astype(BF16)
    for grp in range(A_KV):
        q = q_ref[grp * A_REP:(grp + 1) * A_REP].reshape(A_REP * s_len, KV_WIDTH)
        s = _dot_t(q, k)
        sk = _sink_column(sink_ref, grp, s_len)
        mx = jnp.maximum(jnp.max(s, axis=1, keepdims=True), sk)
        p = jnp.exp(s - mx)
        den = jnp.sum(p, axis=1, keepdims=True) + jnp.exp(sk - mx)
        o = jnp.dot(p.astype(BF16), v, preferred_element_type=F32) / den
        _store_heads(out_ref, o, grp, s_len)


def _attn_ctx_call(sink, aq, ak, av, batch, seq):
    n = ak.shape[0]
    return pl.pallas_call(
        _attn_ctx_kernel, grid=(batch,),
        in_specs=[pl.BlockSpec(memory_space=pltpu.SMEM),
                  pl.BlockSpec((A_HEADS, seq, KV_WIDTH), lambda b: (0, b, 0)),
                  pl.BlockSpec((seq, KV_WIDTH), lambda b: (b, 0)),
                  pl.BlockSpec((seq, KV_WIDTH), lambda b: (b, 0))],
        out_specs=pl.BlockSpec((seq, A_WIDTH), lambda b: (b, 0)),
        out_shape=jax.ShapeDtypeStruct((n, A_WIDTH), BF16),
        compiler_params=pltpu.CompilerParams(dimension_semantics=("parallel",)),
        name="attn_ctx",
    )(sink, aq, ak, av)


def _attn_lat_kernel(sink_ref, q_ref, kc_ref, vc_ref, kp_ref, kq_ref, kn_ref, vp_ref, vq_ref, vn_ref, out_ref):
    i = pl.program_id(1)
    nb = pl.num_programs(1)
    kc = kc_ref[0].astype(BF16)
    vc = vc_ref[0].astype(BF16)
    kp, kq, kn = kp_ref[...].astype(BF16), kq_ref[...].astype(BF16), kn_ref[...].astype(BF16)
    vp, vq, vn = vp_ref[...].astype(BF16), vq_ref[...].astype(BF16), vn_ref[...].astype(BF16)
    rows = A_REP * BLOCK
    qpos = lax.broadcasted_iota(I32, (rows, BLOCK), 0) % BLOCK
    kpos = lax.broadcasted_iota(I32, (rows, BLOCK), 1)
    mask_p = (kpos >= qpos) & (i > 0)
    mask_n = (kpos <= qpos) & (i < nb - 1)
    for grp in range(A_KV):
        q = q_ref[grp * A_REP:(grp + 1) * A_REP].reshape(rows, KV_WIDTH)
        s_c = _dot_t(q, kc)
        s_p = jnp.where(mask_p, _dot_t(q, kp), NEG)
        s_q = _dot_t(q, kq)
        s_n = jnp.where(mask_n, _dot_t(q, kn), NEG)
        sk = _sink_column(sink_ref, grp, BLOCK)
        mx = jnp.maximum(jnp.maximum(jnp.max(s_c, axis=1, keepdims=True), jnp.max(s_p, axis=1, keepdims=True)),
                         jnp.maximum(jnp.max(s_q, axis=1, keepdims=True), jnp.max(s_n, axis=1, keepdims=True)))
        mx = jnp.maximum(mx, sk)
        p_c, p_p, p_q, p_n = jnp.exp(s_c - mx), jnp.exp(s_p - mx), jnp.exp(s_q - mx), jnp.exp(s_n - mx)
        den = (jnp.sum(p_c, axis=1, keepdims=True) + jnp.sum(p_p, axis=1, keepdims=True)
               + jnp.sum(p_q, axis=1, keepdims=True) + jnp.sum(p_n, axis=1, keepdims=True) + jnp.exp(sk - mx))
        o = (jnp.dot(p_c.astype(BF16), vc, preferred_element_type=F32)
             + jnp.dot(p_p.astype(BF16), vp, preferred_element_type=F32)
             + jnp.dot(p_q.astype(BF16), vq, preferred_element_type=F32)
             + jnp.dot(p_n.astype(BF16), vn, preferred_element_type=F32)) / den
        _store_heads(out_ref, o, grp, BLOCK)


def _attn_lat_call(sink, aq, ak, av, kc, vc, batch, seq):
    n = ak.shape[0]
    nb = seq // BLOCK
    past = kc.shape[1]

    def cur(b, i):
        return (b * nb + i, 0)

    def prev(b, i):
        return (b * nb + jnp.maximum(i - 1, 0), 0)

    def nxt(b, i):
        return (b * nb + jnp.minimum(i + 1, nb - 1), 0)

    blk = functools.partial(pl.BlockSpec, (BLOCK, KV_WIDTH))
    cache = pl.BlockSpec((1, past, KV_WIDTH), lambda b, i: (b, 0, 0))
    return pl.pallas_call(
        _attn_lat_kernel, grid=(batch, nb),
        in_specs=[pl.BlockSpec(memory_space=pltpu.SMEM),
                  pl.BlockSpec((A_HEADS, BLOCK, KV_WIDTH), lambda b, i: (0, b * nb + i, 0)),
                  cache, cache, blk(prev), blk(cur), blk(nxt), blk(prev), blk(cur), blk(nxt)],
        out_specs=pl.BlockSpec((BLOCK, A_WIDTH), cur),
        out_shape=jax.ShapeDtypeStruct((n, A_WIDTH), BF16),
        compiler_params=pltpu.CompilerParams(dimension_semantics=("parallel", "parallel")),
        name="attn_lat",
    )(sink, aq, kc, vc, ak, ak, ak, av, av, av)


def _top16_rows(s, payload=None):
    n_rows = s.shape[0]
    rows = lax.broadcasted_iota(I32, s.shape, 0).astype(F32)
    vals, idxs, pays = [], [], []
    for _ in range(P_TOPK):
        mx = jnp.max(s, axis=0, keepdims=True)
        ix = jnp.min(jnp.where(s == mx, rows, float(n_rows)), axis=0, keepdims=True)
        hit = rows == ix
        vals.append(mx)
        idxs.append(ix)
        if payload is not None:
            pays.append(jnp.sum(jnp.where(hit, payload, 0.0), axis=0, keepdims=True))
        s = jnp.where(hit, -jnp.inf, s)
    out = (jnp.concatenate(vals, axis=0), jnp.concatenate(idxs, axis=0))
    if payload is not None:
        out += (jnp.concatenate(pays, axis=0),)
    return out


def _mix_kernel(x_ref, hf_ref, hb_ref, mo_ref, ao_ref, mod_ref, mhg_ref, n2_ref, wm_ref, wa_ref, wq_ref,
                sa_ref, sb_ref, x1_ref, h2_ref, eidx_ref, gate_ref, qp_scr, e_scr, g_scr):
    tm = x_ref.shape[0]
    hs = hf_ref[...] + hb_ref[...]
    parts = []
    for hd in range(M_HEADS):
        blk = hs[:, hd * M_DH:(hd + 1) * M_DH]
        parts.append(blk * lax.rsqrt(jnp.mean(blk * blk, axis=-1, keepdims=True) + EPS))
    m_out = _sigmoid(mo_ref[...]) * (jnp.concatenate(parts, axis=1) * mhg_ref[...])
    mix = (jnp.dot(m_out.astype(BF16), wm_ref[...], preferred_element_type=F32)
           + jnp.dot(ao_ref[...], wa_ref[...], preferred_element_type=F32))
    x1 = x_ref[...] + mod_ref[0, 2:3, :] * mix
    x1_ref[...] = x1
    h2 = x1 * lax.rsqrt(jnp.mean(x1 * x1, axis=-1, keepdims=True) + EPS) * n2_ref[...]
    h2 = h2 * (1.0 + mod_ref[0, 4:5, :]) + mod_ref[0, 3:4, :]
    h2_ref[...] = _pack_bf16_pairs(h2)
    qp = jnp.dot(h2.astype(BF16), wq_ref[...], preferred_element_type=F32)
    for p in range(P_HEADS):
        qp_scr[p] = qp[:, p * P_DKEY:(p + 1) * P_DKEY].astype(BF16)
    sub_a = sa_ref[...]
    sub_b = sb_ref[...]

    def head_body(p, carry):
        for half in range(tm // N_KEYS):
            cols = slice(half * N_KEYS, (half + 1) * N_KEYS)
            qh = qp_scr[p, pl.ds(half * N_KEYS, N_KEYS), :]
            s_a = _dot_t(sub_a, qh[:, 0:P_HALF])
            s_b = _dot_t(sub_b, qh[:, P_HALF:P_DKEY])
            va, ia = _top16_rows(s_a)
            vb, ib = _top16_rows(s_b)
            keep = [P_TOPK // (i + 1) for i in range(P_TOPK)]
            pad = -sum(keep) % 8
            cand = jnp.concatenate([va[i:i + 1, :] + vb[0:keep[i], :] for i in range(P_TOPK)]
                                   + [jnp.full((pad, N_KEYS), -jnp.inf, F32)], axis=0)
            cidx = jnp.concatenate([ia[i:i + 1, :] * float(N_KEYS) + ib[0:keep[i], :] for i in range(P_TOPK)]
                                   + [jnp.zeros((pad, N_KEYS), F32)], axis=0)
            top, _, eidx = _top16_rows(cand, cidx)
            ex = jnp.exp(top - jnp.max(top, axis=0, keepdims=True))
            gates = ex / jnp.sum(ex, axis=0, keepdims=True)
            r0 = pl.multiple_of(p * P_TOPK, P_TOPK)
            e_scr[pl.ds(r0, P_TOPK), cols] = eidx
            g_scr[pl.ds(r0, P_TOPK), cols] = gates
        return carry

    lax.fori_loop(0, P_HEADS, head_body, 0)
    for half in range(tm // N_KEYS):
        cols = slice(half * N_KEYS, (half + 1) * N_KEYS)
        eidx_ref[cols, :] = e_scr[:, cols].T.astype(I32)
        gate_ref[cols, :] = g_scr[:, cols].T


def _mix_call(x2d, tile0, seq, h_f, h_b, mo, a_out, mod3, mod_row0, mhg, n2, w_m, w_a, w_q, sub_a, sub_b):
    n = h_f.shape[0]
    tm = TOKEN_TILE
    per_seq = seq // tm

    def tok(i):
        return (i, 0)

    def const2(i):
        return (0, 0)

    in_specs = [
        pl.BlockSpec((tm, D_MODEL), lambda i: (tile0 + i, 0)),
        pl.BlockSpec((tm, M_WIDTH), tok), pl.BlockSpec((tm, M_WIDTH), tok), pl.BlockSpec((tm, M_WIDTH), tok),
        pl.BlockSpec((tm, A_WIDTH), tok),
        pl.BlockSpec((1, 6, D_MODEL), lambda i: (mod_row0 + (i // per_seq if mod_row0 else 0), 0, 0)),
        pl.BlockSpec((1, M_WIDTH), const2),
        pl.BlockSpec((1, D_MODEL), const2),
        pl.BlockSpec((M_WIDTH, D_MODEL), const2),
        pl.BlockSpec((A_WIDTH, D_MODEL), const2),
        pl.BlockSpec((D_MODEL, P_HEADS * P_DKEY), const2),
        pl.BlockSpec((N_KEYS, P_HALF), const2),
        pl.BlockSpec((N_KEYS, P_HALF), const2),
    ]
    out_specs = [pl.BlockSpec((tm, D_MODEL), tok), pl.BlockSpec((tm, D_MODEL // 2), tok),
                 pl.BlockSpec((tm, N_SEL), tok), pl.BlockSpec((tm, N_SEL), tok)]
    out_shape = [jax.ShapeDtypeStruct((n, D_MODEL), F32), jax.ShapeDtypeStruct((n, D_MODEL // 2), I32),
                 jax.ShapeDtypeStruct((n, N_SEL), I32), jax.ShapeDtypeStruct((n, N_SEL), F32)]
    return pl.pallas_call(
        _mix_kernel, grid=(n // tm,), in_specs=in_specs, out_specs=out_specs, out_shape=out_shape,
        scratch_shapes=[pltpu.VMEM((P_HEADS, tm, P_DKEY), BF16), pltpu.VMEM((N_SEL, tm), F32),
                        pltpu.VMEM((N_SEL, tm), F32)],
        compiler_params=pltpu.CompilerParams(dimension_semantics=("parallel",)),
        name="mix",
    )(x2d, h_f, h_b, mo, a_out, mod3, mhg, n2, w_m, w_a, w_q, sub_a, sub_b)


SC_LANES = 16
SC_CORES = 2
SC_SUBCORES = 16
SC_WORKERS = SC_CORES * SC_SUBCORES
SC_TOKENS = 16
SC_GROUP = SC_LANES
SC_NGROUPS = N_SEL // SC_GROUP
SC_SLOTS = 4
ROW_WORDS = D_MODEL // 2
SC_DOT_ROWS = 8
SC_DOT_PARTIALS = 2
SC_OWORDS = 16 * SC_LANES
HI_MASK = -65536
PACK_ROWS = 512
SCORE_TOKENS = 1024
SCORE_EXPERTS = 2048
DENSE_LATENT_CHUNKS = (1, 2, 4, 5, 6, 7)
GELU_C0 = 0.7978845608028654
GELU_C1 = 0.044715


def _pack_bf16_pairs(x):
    half = x.shape[1] // 2
    bits = lax.bitcast_convert_type(x.astype(BF16).astype(F32), I32)
    return (bits[:, :half] & HI_MASK) | lax.shift_right_logical(bits[:, half:], jnp.int32(16))


def _pack_tables_kernel(u_ref, v_ref, uv_ref, vp_ref, ub_ref):
    v_words = _pack_bf16_pairs(v_ref[...])
    uv_ref[:, 0:ROW_WORDS] = _pack_bf16_pairs(u_ref[...])
    uv_ref[:, ROW_WORDS:2 * ROW_WORDS] = v_words
    vp_ref[...] = v_words
    ub_ref[...] = u_ref[...].astype(BF16)


def _pack_tables_call(u_tab, v_tab):
    n_exp = u_tab.shape[0]
    blk = pl.BlockSpec((PACK_ROWS, D_MODEL), lambda i: (i, 0))
    half = pl.BlockSpec((PACK_ROWS, ROW_WORDS), lambda i: (i, 0))
    return pl.pallas_call(
        _pack_tables_kernel, grid=(n_exp // PACK_ROWS,), in_specs=[blk, blk], out_specs=[blk, half, blk],
        out_shape=[jax.ShapeDtypeStruct((n_exp, 2 * ROW_WORDS), I32), jax.ShapeDtypeStruct((n_exp, ROW_WORDS), I32),
                   jax.ShapeDtypeStruct((n_exp, D_MODEL), BF16)],
        compiler_params=pltpu.CompilerParams(dimension_semantics=("parallel",)),
        name="pack_tables",
    )(u_tab, v_tab)


def _unpack_bf16_pairs(words):
    hi = lax.bitcast_convert_type(words & HI_MASK, F32)
    lo = lax.bitcast_convert_type(lax.shift_left(words, jnp.int32(16)), F32)
    return jnp.concatenate([hi, lo], axis=1)


def _scores_kernel(x_ref, u_ref, o_ref):
    o_ref[...] = _pack_bf16_pairs(_dot_t(_unpack_bf16_pairs(x_ref[...]).astype(BF16), u_ref[...]))


def _scores_call(h2p, u_bf16):
    n = h2p.shape[0]
    n_exp = u_bf16.shape[0]
    tm = math.gcd(n, SCORE_TOKENS)
    return pl.pallas_call(
        _scores_kernel, grid=(n // tm, n_exp // SCORE_EXPERTS),
        in_specs=[pl.BlockSpec((tm, ROW_WORDS), lambda i, j: (i, 0)),
                  pl.BlockSpec((SCORE_EXPERTS, D_MODEL), lambda i, j: (j, 0))],
        out_specs=pl.BlockSpec((tm, SCORE_EXPERTS // 2), lambda i, j: (i, j)),
        out_shape=jax.ShapeDtypeStruct((n, n_exp // 2), I32),
        compiler_params=pltpu.CompilerParams(dimension_semantics=("parallel", "parallel")),
        name="expert_scores",
    )(h2p, u_bf16)


def _sc_gelu(a):
    z = GELU_C0 * (a + GELU_C1 * (a * a * a))
    tanh = 1.0 - 2.0 / (jnp.exp(2.0 * z) + 1.0)
    return 0.5 * a * (1.0 + tanh)


def _sc_split(words):
    return (plsc.bitcast(words & HI_MASK, F32), plsc.bitcast(lax.shift_left(words, jnp.int32(16)), F32))


def _sc_mul_bf16(a_words, b_words):
    return plsc.bitcast(a_words, BF16) * plsc.bitcast(b_words, BF16)


def _sc_split_sum(p, q):
    return _sc_split(plsc.bitcast(p + q, I32))


def _peer_sc_kernel(h2_hbm, eidx_hbm, gate_hbm, uv_hbm, out_hbm,
                    xbuf, ibuf, gbuf, obuf, uvbuf, mbuf, wbuf, sems, sems_in, sems_out):
    n = h2_hbm.shape[0]
    per_worker = n // SC_WORKERS
    blk_tokens = xbuf.shape[1]
    n_blocks = per_worker // blk_tokens
    wid = lax.axis_index("c") * SC_SUBCORES + lax.axis_index("s")
    lane = lax.iota(I32, SC_LANES)

    def split_item(item):
        return lax.shift_right_logical(item, SC_NGROUPS.bit_length() - 1), item & (SC_NGROUPS - 1)

    def gather_copies(bset, item, slot):
        t, g = split_item(item)
        idx = ibuf[bset, t, pl.ds(g * SC_GROUP, SC_GROUP)]
        return (pltpu.make_async_copy(uv_hbm.at[idx], uvbuf.at[slot], sems.at[slot]),)

    def dots(bset, t, slot):
        zero = jnp.zeros((SC_LANES,), F32)

        @pl.loop(0, SC_GROUP, step=SC_DOT_ROWS)
        def _(r0):
            accs = [[zero] * SC_DOT_PARTIALS for _ in range(SC_DOT_ROWS)]
            for k in range(0, ROW_WORDS // SC_LANES, 4):
                xs = [xbuf[bset, t, pl.ds((k + q) * SC_LANES, SC_LANES)] for q in range(4)]
                for i in range(SC_DOT_ROWS):
                    m = [_sc_mul_bf16(xs[q], uvbuf[slot, r0 + i, pl.ds((k + q) * SC_LANES, SC_LANES)])
                         for q in range(4)]
                    hi, lo = _sc_split_sum(m[0] + m[1], m[2] + m[3])
                    p = (k // 4) % SC_DOT_PARTIALS
                    accs[i][p] = accs[i][p] + (hi + lo)
            for i in range(SC_DOT_ROWS):
                mbuf[r0 + i, :] = functools.reduce(lambda a, b: a + b, accs[i])

        tot = zero
        for c in range(SC_LANES):
            tot = tot + plsc.load_gather(mbuf, [lane, jnp.full((SC_LANES,), c, I32)])
        return tot

    def accumulate(bset, t, slot):
        nv = SC_OWORDS // SC_LANES
        for oc in range(ROW_WORDS // SC_OWORDS):
            w0 = oc * SC_OWORDS
            accs = (tuple(obuf[bset, t, pl.ds(w0 + j * SC_LANES, SC_LANES)] for j in range(nv))
                    + tuple(obuf[bset, t, pl.ds(ROW_WORDS + w0 + j * SC_LANES, SC_LANES)] for j in range(nv)))

            def row_quad(rq, accs):
                r = 4 * rq
                ws = [plsc.load_gather(wbuf, [jnp.full((SC_LANES,), r + q, I32)]) for q in range(4)]
                his, los = [], []
                for j in range(nv):
                    m = [_sc_mul_bf16(ws[q], uvbuf[slot, r + q, pl.ds(ROW_WORDS + w0 + j * SC_LANES, SC_LANES)])
                         for q in range(4)]
                    hi, lo = _sc_split_sum(m[0] + m[1], m[2] + m[3])
                    his.append(accs[j] + hi)
                    los.append(accs[nv + j] + lo)
                return tuple(his) + tuple(los)

            accs = lax.fori_loop(0, SC_GROUP // 4, row_quad, accs)
            for j in range(nv):
                obuf[bset, t, pl.ds(w0 + j * SC_LANES, SC_LANES)] = accs[j]
                obuf[bset, t, pl.ds(ROW_WORDS + w0 + j * SC_LANES, SC_LANES)] = accs[nv + j]

    def pack_weights(w):
        bits = plsc.bitcast(w, I32)
        rounded = (bits + 0x7FFF + (lax.shift_right_logical(bits, jnp.int32(16)) & 1)) & HI_MASK
        return rounded | lax.shift_right_logical(rounded, jnp.int32(16))

    n_items = blk_tokens * SC_NGROUPS

    def block_rows(blk):
        return pl.ds(pl.multiple_of(wid * per_worker + blk * blk_tokens, blk_tokens), blk_tokens)

    def load_copies(blk, bset):
        rows = block_rows(blk)
        return (pltpu.make_async_copy(h2_hbm.at[rows], xbuf.at[bset], sems_in.at[bset]),
                pltpu.make_async_copy(eidx_hbm.at[rows], ibuf.at[bset], sems_in.at[bset]),
                pltpu.make_async_copy(gate_hbm.at[rows], gbuf.at[bset], sems_in.at[bset]))

    def store_copy(blk, bset):
        return pltpu.make_async_copy(obuf.at[bset], out_hbm.at[block_rows(blk)], sems_out.at[bset])

    for c in load_copies(0, 0):
        c.start()

    @pl.loop(0, n_blocks)
    def _(blk):
        bset = blk & 1
        for c in load_copies(blk, bset):
            c.wait()

        @pl.when(blk + 1 < n_blocks)
        def _():
            for c in load_copies(blk + 1, 1 - bset):
                c.start()

        @pl.when(blk >= 2)
        def _():
            store_copy(blk - 2, bset).wait()

        @pl.loop(0, blk_tokens)
        def _(t):
            zero = jnp.zeros((SC_LANES,), F32)
            for j in range(D_MODEL // SC_LANES):
                obuf[bset, t, pl.ds(j * SC_LANES, SC_LANES)] = zero

        for ahead in range(SC_SLOTS - 1):
            for c in gather_copies(bset, ahead, ahead):
                c.start()

        @pl.loop(0, n_items)
        def _(item):
            t, g = split_item(item)
            slot = item & (SC_SLOTS - 1)
            ahead = item + (SC_SLOTS - 1)

            @pl.when(ahead < n_items)
            def _():
                for c in gather_copies(bset, ahead, ahead & (SC_SLOTS - 1)):
                    c.start()

            for c in gather_copies(bset, item, slot):
                c.wait()
            a = dots(bset, t, slot)
            wbuf[...] = pack_weights(gbuf[bset, t, pl.ds(g * SC_GROUP, SC_GROUP)] * _sc_gelu(a))
            accumulate(bset, t, slot)

        store_copy(blk, bset).start()

    for blk in range(max(n_blocks - 2, 0), n_blocks):
        store_copy(blk, blk & 1).wait()


def _peer_experts(h2p, eidx, gates, uv_pack):
    n = h2p.shape[0]
    assert n % (8 * SC_WORKERS) == 0
    blk_tokens = math.gcd(n // SC_WORKERS, SC_TOKENS)
    mesh = plsc.VectorSubcoreMesh(core_axis_name="c", subcore_axis_name="s")
    fn = pl.kernel(
        _peer_sc_kernel,
        out_type=jax.ShapeDtypeStruct((n, D_MODEL), F32),
        mesh=mesh,
        scratch_types=[
            pltpu.VMEM((2, blk_tokens, ROW_WORDS), I32),
            pltpu.VMEM((2, blk_tokens, N_SEL), I32),
            pltpu.VMEM((2, blk_tokens, N_SEL), F32),
            pltpu.VMEM((2, blk_tokens, D_MODEL), F32),
            pltpu.VMEM((SC_SLOTS, SC_GROUP, 2 * ROW_WORDS), I32),
            pltpu.VMEM((SC_GROUP, SC_LANES), F32),
            pltpu.VMEM((SC_LANES,), I32),
            pltpu.SemaphoreType.DMA((SC_SLOTS,)), pltpu.SemaphoreType.DMA((2,)), pltpu.SemaphoreType.DMA((2,)),
        ],
        compiler_params=pltpu.CompilerParams(needs_layout_passes=False),
        cost_estimate=pl.CostEstimate(
            flops=4 * n * N_SEL * D_MODEL, transcendentals=n * N_SEL,
            bytes_accessed=4 * (2 * n * N_SEL * ROW_WORDS + n * ROW_WORDS + n * D_MODEL + 2 * n * N_SEL)),
        name="peer_experts",
    )
    return fn(h2p, eidx, gates, uv_pack)


def _peer_sc_scored_kernel(a_hbm, eidx_hbm, gate_hbm, v_hbm, out_hbm,
                           abuf, ibuf, gbuf, obuf, vbuf, wbuf, sems, sems_a):
    n = a_hbm.shape[0]
    per_worker = n // SC_WORKERS
    blk_tokens = ibuf.shape[0]
    wid = lax.axis_index("c") * SC_SUBCORES + lax.axis_index("s")

    def split_item(item):
        return lax.shift_right_logical(item, SC_NGROUPS.bit_length() - 1), item & (SC_NGROUPS - 1)

    def gather_copy(item, slot):
        t, g = split_item(item)
        idx = ibuf[t, pl.ds(g * SC_GROUP, SC_GROUP)]
        return pltpu.make_async_copy(v_hbm.at[idx], vbuf.at[slot], sems.at[slot])

    def accumulate(t, slot):
        nv = SC_OWORDS // SC_LANES
        for oc in range(ROW_WORDS // SC_OWORDS):
            w0 = oc * SC_OWORDS
            accs = (tuple(obuf[t, pl.ds(w0 + j * SC_LANES, SC_LANES)] for j in range(nv))
                    + tuple(obuf[t, pl.ds(ROW_WORDS + w0 + j * SC_LANES, SC_LANES)] for j in range(nv)))

            def row_quad(rq, accs):
                r = 4 * rq
                ws = [plsc.load_gather(wbuf, [jnp.full((SC_LANES,), r + q, I32)]) for q in range(4)]
                his, los = [], []
                for j in range(nv):
                    m = [_sc_mul_bf16(ws[q], vbuf[slot, r + q, pl.ds(w0 + j * SC_LANES, SC_LANES)])
                         for q in range(4)]
                    hi, lo = _sc_split_sum(m[0] + m[1], m[2] + m[3])
                    his.append(accs[j] + hi)
                    los.append(accs[nv + j] + lo)
                return tuple(his) + tuple(los)

            accs = lax.fori_loop(0, SC_GROUP // 4, row_quad, accs)
            for j in range(nv):
                obuf[t, pl.ds(w0 + j * SC_LANES, SC_LANES)] = accs[j]
                obuf[t, pl.ds(ROW_WORDS + w0 + j * SC_LANES, SC_LANES)] = accs[nv + j]

    def pack_weights(w):
        bits = plsc.bitcast(w, I32)
        rounded = (bits + 0x7FFF + (lax.shift_right_logical(bits, jnp.int32(16)) & 1)) & HI_MASK
        return rounded | lax.shift_right_logical(rounded, jnp.int32(16))

    n_items = blk_tokens * SC_NGROUPS

    @pl.loop(0, per_worker // blk_tokens)
    def _(blk):
        tok0 = pl.multiple_of(wid * per_worker + blk * blk_tokens, blk_tokens)

        def score_copy(t):
            return pltpu.make_async_copy(a_hbm.at[tok0 + t], abuf.at[t & 1], sems_a.at[t & 1])

        score_copy(0).start()
        pltpu.sync_copy(eidx_hbm.at[pl.ds(tok0, blk_tokens)], ibuf)
        pltpu.sync_copy(gate_hbm.at[pl.ds(tok0, blk_tokens)], gbuf)

        @pl.loop(0, blk_tokens)
        def _(t):
            zero = jnp.zeros((SC_LANES,), F32)
            for j in range(D_MODEL // SC_LANES):
                obuf[t, pl.ds(j * SC_LANES, SC_LANES)] = zero

        for ahead in range(SC_SLOTS - 1):
            gather_copy(ahead, ahead).start()

        @pl.loop(0, n_items)
        def _(item):
            t, g = split_item(item)
            slot = item & (SC_SLOTS - 1)
            ahead = item + (SC_SLOTS - 1)

            @pl.when(ahead < n_items)
            def _():
                gather_copy(ahead, ahead & (SC_SLOTS - 1)).start()

            @pl.when(g == 0)
            def _():
                score_copy(t).wait()

                @pl.when(t + 1 < blk_tokens)
                def _():
                    score_copy(t + 1).start()

            gather_copy(item, slot).wait()
            idx = ibuf[t, pl.ds(g * SC_GROUP, SC_GROUP)]
            half = SCORE_EXPERTS // 2
            word = plsc.load_gather(
                abuf, [jnp.full((SC_LANES,), t & 1, I32),
                       lax.shift_right_logical(idx, jnp.int32(SCORE_EXPERTS.bit_length() - 1)) * half
                       + (idx & (half - 1))])
            hi, lo = _sc_split(word)
            a = jnp.where((idx & half) == 0, hi, lo)
            wbuf[...] = pack_weights(gbuf[t, pl.ds(g * SC_GROUP, SC_GROUP)] * _sc_gelu(a))
            accumulate(t, slot)

        pltpu.sync_copy(obuf, out_hbm.at[pl.ds(tok0, blk_tokens)])


def _peer_experts_scored(scores, eidx, gates, v_pack):
    n, n_exp = scores.shape
    assert n % (8 * SC_WORKERS) == 0
    blk_tokens = math.gcd(n // SC_WORKERS, 2 * SC_TOKENS)
    mesh = plsc.VectorSubcoreMesh(core_axis_name="c", subcore_axis_name="s")
    fn = pl.kernel(
        _peer_sc_scored_kernel,
        out_type=jax.ShapeDtypeStruct((n, D_MODEL), F32),
        mesh=mesh,
        scratch_types=[
            pltpu.VMEM((2, n_exp), I32),
            pltpu.VMEM((blk_tokens, N_SEL), I32),
            pltpu.VMEM((blk_tokens, N_SEL), F32),
            pltpu.VMEM((blk_tokens, D_MODEL), F32),
            pltpu.VMEM((SC_SLOTS, SC_GROUP, ROW_WORDS), I32),
            pltpu.VMEM((SC_LANES,), I32),
            pltpu.SemaphoreType.DMA((SC_SLOTS,)), pltpu.SemaphoreType.DMA((2,)),
        ],
        compiler_params=pltpu.CompilerParams(needs_layout_passes=False),
        cost_estimate=pl.CostEstimate(
            flops=2 * n * N_SEL * D_MODEL, transcendentals=n * N_SEL,
            bytes_accessed=4 * (n * N_SEL * ROW_WORDS + n * n_exp + n * D_MODEL + 2 * n * N_SEL)),
        name="peer_experts_scored",
    )
    return fn(scores, eidx, gates, v_pack)


def _resid_kernel(x1_ref, p_ref, mod_ref, o_ref):
    o_ref[...] = x1_ref[...] + mod_ref[0, 5:6, :] * p_ref[...]


def _resid_call(x1, peer_out, seq, mod3, mod_row0):
    n = x1.shape[0]
    tm = TOKEN_TILE
    per_seq = seq // tm
    tok = pl.BlockSpec((tm, D_MODEL), lambda i: (i, 0))
    return pl.pallas_call(
        _resid_kernel, grid=(n // tm,),
        in_specs=[tok, tok,
                  pl.BlockSpec((1, 6, D_MODEL), lambda i: (mod_row0 + (i // per_seq if mod_row0 else 0), 0, 0))],
        out_specs=tok, out_shape=jax.ShapeDtypeStruct((n, D_MODEL), F32),
        compiler_params=pltpu.CompilerParams(dimension_semantics=("parallel",)),
        name="resid",
    )(x1, peer_out, mod3)


def _rope_tables(seq, rotate):
    if not rotate:
        return jnp.ones((seq, KV_WIDTH), F32), jnp.zeros((seq, KV_WIDTH), F32)
    quarter = A_DH // 4
    t = jnp.arange(seq)
    row = (t // GRID_W).astype(F32)
    col = (t % GRID_W).astype(F32)
    inv = ROPE_BASE ** (-jnp.arange(quarter, dtype=F32) / quarter)
    d = jnp.arange(A_DH)
    pos = jnp.where(d[None, :] < A_DH // 2, row[:, None], col[:, None])
    ang = pos * inv[d % quarter][None, :]
    sign = jnp.where((d % (A_DH // 2)) < quarter, -1.0, 1.0).astype(F32)
    cos = jnp.cos(ang)
    sin = jnp.sin(ang) * sign[None, :]
    return jnp.tile(cos, (1, KV_WIDTH // A_DH)), jnp.tile(sin, (1, KV_WIDTH // A_DH))


def _run_chunk(x, b0, batch, mod3, mod_row0, prm, cache, rotate, gate_on, dense_scores):
    (n1, n2, w_main, w_g, w_gt, b_g, b_gt, mhg, qg_t, kg_t, bd, sink, w_m, w_a, w_q, sub_a, sub_b,
     (uv_pack, v_pack, u_bf16)) = prm
    seq = x.shape[1]
    n = batch * seq
    x2d = x.reshape(x.shape[0] * seq, D_MODEL)
    tile0 = b0 * seq // TOKEN_TILE
    cos, sin = _rope_tables(seq, rotate)
    if gate_on is not None:
        (cos, sin), _ = lax.optimization_barrier(((cos, sin), gate_on))
    mq, mk, mv, mo, gcol, grow, aq, ak, av = _inproj_call(
        x2d, tile0, n, seq, mod3, mod_row0, n1, w_main, w_g, w_gt, b_g, b_gt, qg_t, kg_t, bd, cos, sin)
    kc, vc, c0, m0 = cache
    h_f, h_b, c_fin, m_fin = _mlstm_call(mq, mk, mv, gcol, grow, c0, m0, batch, seq)
    if kc is None:
        a_out = _attn_ctx_call(sink, aq, ak, av, batch, seq)
    else:
        a_out = _attn_lat_call(sink, aq, ak, av, kc, vc, batch, seq)
    x1, h2p, eidx, gates = _mix_call(x2d, tile0, seq, h_f, h_b, mo, a_out, mod3, mod_row0, mhg, n2, w_m, w_a,
                                     w_q, sub_a, sub_b)
    if dense_scores:
        peer_out = _peer_experts_scored(_scores_call(h2p, u_bf16), eidx, gates, v_pack)
    else:
        peer_out = _peer_experts(h2p, eidx, gates, uv_pack)
    y = _resid_call(x1, peer_out, seq, mod3, mod_row0).reshape(batch, seq, D_MODEL)
    return y, h2p, peer_out, ak, av, c_fin, m_fin


def _pack_state(C, n_vec, m):
    b = C.shape[0]
    caug = jnp.concatenate([C, jnp.broadcast_to(n_vec[..., None], C.shape)], axis=-1)
    caug = caug.reshape(b, 2 * M_HEADS, M_DH, 2 * M_DH)
    m_rep = jnp.broadcast_to(m.reshape(b, 2 * M_HEADS, 1, 1), (b, 2 * M_HEADS, 8, M_DH))
    return caug.astype(F32), m_rep.astype(F32)


def kernel(x_prompt, x_sample, c, cache_attn_k, cache_attn_v, state_mlstm_C, state_mlstm_n, state_mlstm_m,
           c_ctx, w_ada, b_ada, norm1_g, norm2_g, w_in, b_gates, mh_norm_g, q_norm_g, k_norm_g, sink_logits,
           w_out, peer_w_q, peer_sub_a, peer_sub_b, peer_u, peer_v):
    depth = w_ada.shape[0]
    assert depth == 1
    batch, seq, _ = x_prompt.shape
    dec_batch, dec_seq, _ = x_sample.shape
    assert dec_batch + 1 <= MOD_ROWS and batch % sum(CTX_CHUNK_WEIGHTS) == 0 and dec_batch % LATENT_CHUNKS == 0
    l = 0

    cond = jnp.concatenate([c_ctx[None, :], c, jnp.zeros((MOD_ROWS - 1 - dec_batch, D_MODEL), F32)], axis=0)
    mod3 = _ada_call(cond, w_ada[l], b_ada[l]).reshape(MOD_ROWS, 6, D_MODEL)

    wi = w_in[l]
    g0 = 4 * M_WIDTH
    w_main = jnp.concatenate([wi[:, :g0], wi[:, g0 + N_GATES:]], axis=1).astype(BF16)
    w_g = wi[:, g0:g0 + N_GATES]
    seg = jnp.arange(A_WIDTH) // A_DH
    bd = jnp.where(seg[:, None] == seg[None, :], 1.0 / A_DH, 0.0).astype(F32)
    prm = (norm1_g[l][None, :], norm2_g[l][None, :], w_main, w_g, w_g.T, b_gates[l][None, :], b_gates[l][:, None],
           mh_norm_g[l][None, :], jnp.tile(q_norm_g[l], A_HEADS)[None, :], jnp.tile(k_norm_g[l], A_KV)[None, :], bd,
           sink_logits[l], w_out[l][:M_WIDTH].astype(BF16), w_out[l][M_WIDTH:].astype(BF16),
           peer_w_q[l].astype(BF16), peer_sub_a[l].astype(BF16), peer_sub_b[l].astype(BF16),
           _pack_tables_call(peer_u[l], peer_v[l]))

    zeros_c = jnp.zeros((batch, 2, M_HEADS, M_DH, M_DH), F32)
    c0, m0 = _pack_state(zeros_c, zeros_c[..., 0], jnp.full((batch, 2, M_HEADS), NEG, F32))
    c0s, m0s = _pack_state(state_mlstm_C[:, l], state_mlstm_n[:, l], state_mlstm_m[:, l])
    past = cache_attn_k.shape[2]
    kc = cache_attn_k[:, l].reshape(dec_batch, past, KV_WIDTH)
    vc = cache_attn_v[:, l].reshape(dec_batch, past, KV_WIDTH)

    jobs = []
    ctx_sizes = [batch * f // sum(CTX_CHUNK_WEIGHTS) for f in CTX_CHUNK_WEIGHTS]
    assert sum(ctx_sizes) == batch
    b0 = 0
    for size in ctx_sizes:
        b1 = b0 + size
        jobs.append((x_prompt, b0, size, 0, (None, None, c0[b0:b1], m0[b0:b1]), False, False))
        b0 = b1
    for ci, b0 in enumerate(range(0, dec_batch, dec_batch // LATENT_CHUNKS)):
        b1 = b0 + dec_batch // LATENT_CHUNKS
        jobs.append((x_sample, b0, b1 - b0, 1 + b0, (kc[b0:b1], vc[b0:b1], c0s[b0:b1], m0s[b0:b1]), True,
                     ci in DENSE_LATENT_CHUNKS))
    outs = []
    for i, (x_all, b0, nb, mod_row0, cache, rotate, dense) in enumerate(jobs):
        gate = tuple(g for g in (outs[i - 1][1] if i >= 1 else None,
                                 outs[i - EXPERT_LAG][2] if i >= EXPERT_LAG else None) if g is not None)
        outs.append(_run_chunk(x_all, b0, nb, mod3, mod_row0, prm, cache, rotate, gate or None, dense))
    ctx, lat = outs[:len(ctx_sizes)], outs[len(ctx_sizes):]
    y_p = jnp.concatenate([o[0] for o in ctx], axis=0)
    y_s = jnp.concatenate([o[0] for o in lat], axis=0)
    k_new = jnp.concatenate([o[3] for o in ctx], axis=0)
    v_new = jnp.concatenate([o[4] for o in ctx], axis=0)
    c_fin = jnp.concatenate([o[5] for o in ctx], axis=0)
    m_fin = jnp.concatenate([o[6] for o in ctx], axis=0)

    c_fin = c_fin.reshape(batch, 2, M_HEADS, M_DH, 2 * M_DH)
    new_c = c_fin[..., :M_DH][:, None]
    new_n = c_fin[..., M_DH][:, None]
    new_m = m_fin[:, :, 0, 0].reshape(batch, 2, M_HEADS)[:, None]
    new_k = k_new.reshape(batch, 1, seq, A_KV, A_DH)
    new_v = v_new.reshape(batch, 1, seq, A_KV, A_DH)
    return y_p, y_s, new_k, new_v, new_c, new_n, new_m
```

```python
import functools
import math

import jax
import jax.numpy as jnp
from jax import lax
from jax.experimental import pallas as pl
from jax.experimental.pallas import tpu as pltpu
from jax.experimental.pallas import tpu_sc as plsc

F32 = jnp.float32
BF16 = jnp.bfloat16
I32 = jnp.int32
HI = lax.Precision.HIGHEST

D_MODEL = 1024
EPS = 1e-6
NEG = -1e30
GRID_W = 64
M_HEADS = 4
M_WIDTH = 512
M_DH = 128
A_HEADS = 8
A_KV = 2
A_REP = 4
A_DH = 64
A_WIDTH = 512
KV_WIDTH = A_KV * A_DH
BLOCK = 128
ROPE_BASE = 10000.0
N_KEYS = 128
P_HEADS = 8
P_DKEY = 256
P_HALF = 128
P_TOPK = 16
N_SEL = P_HEADS * P_TOPK
N_GATES = 4 * M_HEADS
MAIN_COLS = 4 * M_WIDTH + A_WIDTH + 2 * KV_WIDTH
MOD_ROWS = 16

TOKEN_TILE = 256
MLSTM_CHUNK = 128
ADA_COL_TILE = 768
CTX_CHUNK_WEIGHTS = (1, 2, 3, 4, 6)
EXPERT_LAG = 3
LATENT_CHUNKS = 8


def _sigmoid(x):
    return 1.0 / (1.0 + jnp.exp(-x))


def _log_sigmoid(x):
    return jnp.minimum(x, 0.0) - jnp.log1p(jnp.exp(-jnp.abs(x)))


def _dot_t(a, b, precision=None):
    return lax.dot_general(a, b, (((1,), (1,)), ((), ())), precision=precision,
                           preferred_element_type=F32)


def _ada_kernel(c_ref, w_ref, b_ref, o_ref):
    c = c_ref[...]
    s = c * _sigmoid(c)
    o_ref[...] = jnp.dot(s, w_ref[...], precision=HI, preferred_element_type=F32) + b_ref[...]


def _ada_call(cond, w_ada, b_ada):
    n_out = w_ada.shape[1]
    return pl.pallas_call(
        _ada_kernel,
        grid=(n_out // ADA_COL_TILE,),
        in_specs=[pl.BlockSpec((MOD_ROWS, D_MODEL), lambda j: (0, 0)),
                  pl.BlockSpec((D_MODEL, ADA_COL_TILE), lambda j: (0, j)),
                  pl.BlockSpec((1, ADA_COL_TILE), lambda j: (0, j))],
        out_specs=pl.BlockSpec((MOD_ROWS, ADA_COL_TILE), lambda j: (0, j)),
        out_shape=jax.ShapeDtypeStruct((MOD_ROWS, n_out), F32),
        name="ada",
    )(cond, w_ada, b_ada.reshape(1, n_out))


def _swap16(x):
    n = x.shape[-1]
    lane = lax.broadcasted_iota(I32, x.shape, x.ndim - 1)
    return jnp.where((lane & 16) == 0, pltpu.roll(x, n - 16, x.ndim - 1), pltpu.roll(x, 16, x.ndim - 1))


def _inproj_kernel(x_ref, mod_ref, n1_ref, w_ref, wg_ref, wgt_ref, bg_ref, bgt_ref, qg_ref, kg_ref,
                   bd_ref, cos_ref, sin_ref,
                   mq_ref, mk_ref, mv_ref, mo_ref, gc_ref, gr_ref, aq_ref, ak_ref, av_ref):
    x = x_ref[...]
    h = x * lax.rsqrt(jnp.mean(x * x, axis=-1, keepdims=True) + EPS) * n1_ref[...]
    h = h * (1.0 + mod_ref[0, 1:2, :]) + mod_ref[0, 0:1, :]
    z = jnp.dot(h.astype(BF16), w_ref[...], preferred_element_type=F32)

    mq_ref[...] = (z[:, 0:M_WIDTH] * (M_DH ** -0.5)).astype(BF16)
    mk_ref[...] = z[:, M_WIDTH:2 * M_WIDTH].astype(BF16)
    mv_ref[...] = z[:, 2 * M_WIDTH:3 * M_WIDTH].astype(BF16)
    mo_ref[...] = z[:, 3 * M_WIDTH:4 * M_WIDTH]

    g = jnp.dot(h, wg_ref[...], precision=HI, preferred_element_type=F32) + bg_ref[...]
    kind = lax.broadcasted_iota(I32, g.shape, 1) // M_HEADS
    gc_ref[...] = jnp.where((kind & 1) == 1, _log_sigmoid(g), g)
    gt = _dot_t(wgt_ref[...], h, precision=HI) + bgt_ref[...]
    kind_t = lax.broadcasted_iota(I32, gt.shape, 0) // M_HEADS
    gr_ref[...] = jnp.where((kind_t & 1) == 1, _log_sigmoid(gt), gt)

    o = 4 * M_WIDTH
    aq = z[:, o:o + A_WIDTH]
    ak = z[:, o + A_WIDTH:o + A_WIDTH + KV_WIDTH]
    av_ref[...] = z[:, o + A_WIDTH + KV_WIDTH:o + A_WIDTH + 2 * KV_WIDTH]
    bd = bd_ref[...]
    cos = cos_ref[...]
    sin = sin_ref[...]
    aq = aq * lax.rsqrt(jnp.dot(aq * aq, bd, precision=HI, preferred_element_type=F32) + EPS) * qg_ref[...]
    cos4 = jnp.concatenate([cos] * (A_WIDTH // KV_WIDTH), axis=1)
    sin4 = jnp.concatenate([sin] * (A_WIDTH // KV_WIDTH), axis=1)
    aq = (aq * cos4 + _swap16(aq) * sin4) * (A_DH ** -0.5)
    ak = ak * lax.rsqrt(jnp.dot(ak * ak, bd[0:KV_WIDTH, 0:KV_WIDTH], precision=HI,
                                preferred_element_type=F32) + EPS) * kg_ref[...]
    ak_ref[...] = ak * cos + _swap16(ak) * sin

    lane = lax.broadcasted_iota(I32, (aq.shape[0], KV_WIDTH), 1)
    for hd in range(A_HEADS):
        grp = hd // A_REP
        blk = aq[:, (hd // 2) * KV_WIDTH:(hd // 2 + 1) * KV_WIDTH]
        if hd % 2 != grp:
            blk = pltpu.roll(blk, A_DH, 1)
        keep = (lane >= grp * A_DH) & (lane < (grp + 1) * A_DH)
        aq_ref[hd] = jnp.where(keep, blk, 0.0).astype(BF16)


def _inproj_call(x2d, tile0, n, seq, mod3, mod_row0, n1, w_main, w_g, w_gt, b_g, b_gt, qg_t, kg_t, bd, cos, sin):
    tm = TOKEN_TILE
    per_seq = seq // tm

    def tok(i):
        return (i, 0)

    def const2(i):
        return (0, 0)

    in_specs = [
        pl.BlockSpec((tm, D_MODEL), lambda i: (tile0 + i, 0)),
        pl.BlockSpec((1, 6, D_MODEL), lambda i: (mod_row0 + (i // per_seq if mod_row0 else 0), 0, 0)),
        pl.BlockSpec((1, D_MODEL), const2),
        pl.BlockSpec((D_MODEL, MAIN_COLS), const2),
        pl.BlockSpec((D_MODEL, N_GATES), const2),
        pl.BlockSpec((N_GATES, D_MODEL), const2),
        pl.BlockSpec((1, N_GATES), const2),
        pl.BlockSpec((N_GATES, 1), const2),
        pl.BlockSpec((1, A_WIDTH), const2),
        pl.BlockSpec((1, KV_WIDTH), const2),
        pl.BlockSpec((A_WIDTH, A_WIDTH), const2),
        pl.BlockSpec((tm, KV_WIDTH), lambda i: (i % per_seq, 0)),
        pl.BlockSpec((tm, KV_WIDTH), lambda i: (i % per_seq, 0)),
    ]
    out_specs = [
        pl.BlockSpec((tm, M_WIDTH), tok),
        pl.BlockSpec((tm, M_WIDTH), tok),
        pl.BlockSpec((tm, M_WIDTH), tok),
        pl.BlockSpec((tm, M_WIDTH), tok),
        pl.BlockSpec((tm, N_GATES), tok),
        pl.BlockSpec((N_GATES, tm), lambda i: (0, i)),
        pl.BlockSpec((A_HEADS, tm, KV_WIDTH), lambda i: (0, i, 0)),
        pl.BlockSpec((tm, KV_WIDTH), tok),
        pl.BlockSpec((tm, KV_WIDTH), tok),
    ]
    out_shape = [
        jax.ShapeDtypeStruct((n, M_WIDTH), BF16),
        jax.ShapeDtypeStruct((n, M_WIDTH), BF16),
        jax.ShapeDtypeStruct((n, M_WIDTH), BF16),
        jax.ShapeDtypeStruct((n, M_WIDTH), F32),
        jax.ShapeDtypeStruct((n, N_GATES), F32),
        jax.ShapeDtypeStruct((N_GATES, n), F32),
        jax.ShapeDtypeStruct((A_HEADS, n, KV_WIDTH), BF16),
        jax.ShapeDtypeStruct((n, KV_WIDTH), F32),
        jax.ShapeDtypeStruct((n, KV_WIDTH), F32),
    ]
    return pl.pallas_call(
        _inproj_kernel, grid=(n // tm,), in_specs=in_specs, out_specs=out_specs, out_shape=out_shape,
        compiler_params=pltpu.CompilerParams(dimension_semantics=("parallel",)),
        name="inproj",
    )(x2d, mod3, n1, w_main, w_g, w_gt, b_g, b_gt, qg_t, kg_t, bd, cos, sin)


def _mlstm_chain(q, k, v, li_c, b_c, li_r, b_r, caug, m, mask, reverse):
    L = q.shape[0]
    last = 0 if reverse else L - 1
    a_inter = b_c + m
    d = jnp.where(mask, b_c - b_r + li_r, -jnp.inf)
    m_t = jnp.maximum(a_inter, jnp.max(d, axis=1, keepdims=True))
    w_inter = jnp.exp(a_inter - m_t)
    s = _dot_t(q, k) * jnp.exp(d - m_t)
    qc = jnp.dot(q, caug.astype(BF16), preferred_element_type=F32)
    num = jnp.dot(s.astype(BF16), v, preferred_element_type=F32) + w_inter * qc[:, 0:M_DH]
    den = jnp.sum(s, axis=1, keepdims=True) + w_inter * qc[:, M_DH:M_DH + 1]
    den = jnp.maximum(jnp.abs(den), jnp.exp(-m_t))
    h = num / den
    m_new = m_t[last:last + 1, :]
    b_last = b_c[last:last + 1, 0:1]
    g_c = jnp.exp(b_last - b_c[:, 0:1] + li_c - m_new)
    decay = jnp.exp(b_last + m - m_new)
    kw = (k.astype(F32) * g_c).astype(BF16)
    vaug = jnp.concatenate([v, jnp.ones_like(v)], axis=1)
    upd = lax.dot_general(kw, vaug, (((0,), (0,)), ((), ())), preferred_element_type=F32)
    return h, decay * caug + upd, m_new


def _mlstm_kernel(qf_ref, kf_ref, vf_ref, gcf_ref, grf_ref, qb_ref, kb_ref, vb_ref, gcb_ref, grb_ref,
                  c0_ref, m0_ref, hf_ref, hb_ref, cfin_ref, mfin_ref, c_scr, m_scr):
    c = pl.program_id(1)
    nc = pl.num_programs(1)
    L = qf_ref.shape[0]

    @pl.when(c == 0)
    def _():
        c_scr[...] = c0_ref[0]
        m_scr[...] = m0_ref[0]

    row = lax.broadcasted_iota(I32, (L, L), 0)
    col = lax.broadcasted_iota(I32, (L, L), 1)
    lower = row >= col
    upper = row <= col
    lower_f = lower.astype(F32)
    upper_f = upper.astype(F32)

    for direction in range(2):
        reverse = direction == 1
        q_ref, k_ref, v_ref, gc_ref, gr_ref, h_ref = (
            (qb_ref, kb_ref, vb_ref, gcb_ref, grb_ref, hb_ref) if reverse
            else (qf_ref, kf_ref, vf_ref, gcf_ref, grf_ref, hf_ref))
        tri, tri_t, mask = (upper_f, lower_f, upper) if reverse else (lower_f, upper_f, lower)
        gc = gc_ref[...]
        gr = gr_ref[...]
        bc = jnp.dot(tri, gc, precision=HI, preferred_element_type=F32)
        br = jnp.dot(gr, tri_t, precision=HI, preferred_element_type=F32)
        for hd in range(M_HEADS):
            ch = direction * M_HEADS + hd
            sl = slice(hd * M_DH, (hd + 1) * M_DH)
            ci = 2 * direction * M_HEADS + hd
            cf = ci + M_HEADS
            h, caug, m_new = _mlstm_chain(
                q_ref[:, sl], k_ref[:, sl], v_ref[:, sl],
                gc[:, ci:ci + 1], bc[:, cf:cf + 1], gr[ci:ci + 1, :], br[cf:cf + 1, :],
                c_scr[ch], m_scr[ch][0:1, 0:1], mask, reverse)
            h_ref[:, sl] = h
            c_scr[ch] = caug
            m_scr[ch] = jnp.broadcast_to(m_new, m_scr.shape[1:])

    @pl.when(c == nc - 1)
    def _():
        cfin_ref[0] = c_scr[...]
        mfin_ref[0] = m_scr[...]


def _mlstm_call(mq, mk, mv, gcol, grow, c0, m0, batch, seq):
    n = mq.shape[0]
    L = MLSTM_CHUNK
    nc = seq // L
    n_ch = 2 * M_HEADS

    def fwd(b, c):
        return (b * nc + c, 0)

    def bwd(b, c):
        return (b * nc + nc - 1 - c, 0)

    def fwd_t(b, c):
        return (0, b * nc + c)

    def bwd_t(b, c):
        return (0, b * nc + nc - 1 - c)

    tok = pl.BlockSpec((L, M_WIDTH), fwd)
    tok_b = pl.BlockSpec((L, M_WIDTH), bwd)
    in_specs = [tok, tok, tok, pl.BlockSpec((L, N_GATES), fwd), pl.BlockSpec((N_GATES, L), fwd_t),
                tok_b, tok_b, tok_b, pl.BlockSpec((L, N_GATES), bwd), pl.BlockSpec((N_GATES, L), bwd_t),
                pl.BlockSpec((1, n_ch, M_DH, 2 * M_DH), lambda b, c: (b, 0, 0, 0)),
                pl.BlockSpec((1, n_ch, 8, M_DH), lambda b, c: (b, 0, 0, 0))]
    out_specs = [tok, tok_b,
                 pl.BlockSpec((1, n_ch, M_DH, 2 * M_DH), lambda b, c: (b, 0, 0, 0)),
                 pl.BlockSpec((1, n_ch, 8, M_DH), lambda b, c: (b, 0, 0, 0))]
    out_shape = [jax.ShapeDtypeStruct((n, M_WIDTH), F32), jax.ShapeDtypeStruct((n, M_WIDTH), F32),
                 jax.ShapeDtypeStruct((batch, n_ch, M_DH, 2 * M_DH), F32),
                 jax.ShapeDtypeStruct((batch, n_ch, 8, M_DH), F32)]
    return pl.pallas_call(
        _mlstm_kernel, grid=(batch, nc), in_specs=in_specs, out_specs=out_specs, out_shape=out_shape,
        scratch_shapes=[pltpu.VMEM((n_ch, M_DH, 2 * M_DH), F32), pltpu.VMEM((n_ch, 8, M_DH), F32)],
        compiler_params=pltpu.CompilerParams(dimension_semantics=("parallel", "arbitrary")),
        name="mlstm",
    )(mq, mk, mv, gcol, grow, mq, mk, mv, gcol, grow, c0, m0)


def _sink_column(sink_ref, grp, rows_per_head):
    return jnp.concatenate(
        [jnp.full((rows_per_head, 1), sink_ref[grp * A_REP + r], F32) for r in range(A_REP)], axis=0)


def _store_heads(out_ref, o, grp, rows_per_head):
    for r in range(A_REP):
        hd = grp * A_REP + r
        out_ref[:, hd * A_DH:(hd + 1) * A_DH] = o[r * rows_per_head:(r + 1) * rows_per_head,
                                                  grp * A_DH:(grp + 1) * A_DH].astype(out_ref.dtype)


def _attn_ctx_kernel(sink_ref, q_ref, k_ref, v_ref, out_ref):
    s_len = k_ref.shape[0]
    k = k_ref[...].astype(BF16)
    v = v_ref[...].astype(BF16)
    for grp in range(A_KV):
        q = q_ref[grp * A_REP:(grp + 1) * A_REP].reshape(A_REP * s_len, KV_WIDTH)
        s = _dot_t(q, k)
        sk = _sink_column(sink_ref, grp, s_len)
        mx = jnp.maximum(jnp.max(s, axis=1, keepdims=True), sk)
        p = jnp.exp(s - mx)
        den = jnp.sum(p, axis=1, keepdims=True) + jnp.exp(sk - mx)
        o = jnp.dot(p.astype(BF16), v, preferred_element_type=F32) / den
        _store_heads(out_ref, o, grp, s_len)


def _attn_ctx_call(sink, aq, ak, av, batch, seq):
    n = ak.shape[0]
    return pl.pallas_call(
        _attn_ctx_kernel, grid=(batch,),
        in_specs=[pl.BlockSpec(memory_space=pltpu.SMEM),
                  pl.BlockSpec((A_HEADS, seq, KV_WIDTH), lambda b: (0, b, 0)),
                  pl.BlockSpec((seq, KV_WIDTH), lambda b: (b, 0)),
                  pl.BlockSpec((seq, KV_WIDTH), lambda b: (b, 0))],
        out_specs=pl.BlockSpec((seq, A_WIDTH), lambda b: (b, 0)),
        out_shape=jax.ShapeDtypeStruct((n, A_WIDTH), BF16),
        compiler_params=pltpu.CompilerParams(dimension_semantics=("parallel",)),
        name="attn_ctx",
    )(sink, aq, ak, av)


def _attn_lat_kernel(sink_ref, q_ref, kc_ref, vc_ref, kp_ref, kq_ref, kn_ref, vp_ref, vq_ref, vn_ref, out_ref):
    i = pl.program_id(1)
    nb = pl.num_programs(1)
    kc = kc_ref[0].astype(BF16)
    vc = vc_ref[0].astype(BF16)
    kp, kq, kn = kp_ref[...].astype(BF16), kq_ref[...].astype(BF16), kn_ref[...].astype(BF16)
    vp, vq, vn = vp_ref[...].astype(BF16), vq_ref[...].astype(BF16), vn_ref[...].astype(BF16)
    rows = A_REP * BLOCK
    qpos = lax.broadcasted_iota(I32, (rows, BLOCK), 0) % BLOCK
    kpos = lax.broadcasted_iota(I32, (rows, BLOCK), 1)
    mask_p = (kpos >= qpos) & (i > 0)
    mask_n = (kpos <= qpos) & (i < nb - 1)
    for grp in range(A_KV):
        q = q_ref[grp * A_REP:(grp + 1) * A_REP].reshape(rows, KV_WIDTH)
        s_c = _dot_t(q, kc)
        s_p = jnp.where(mask_p, _dot_t(q, kp), NEG)
        s_q = _dot_t(q, kq)
        s_n = jnp.where(mask_n, _dot_t(q, kn), NEG)
        sk = _sink_column(sink_ref, grp, BLOCK)
        mx = jnp.maximum(jnp.maximum(jnp.max(s_c, axis=1, keepdims=True), jnp.max(s_p, axis=1, keepdims=True)),
                         jnp.maximum(jnp.max(s_q, axis=1, keepdims=True), jnp.max(s_n, axis=1, keepdims=True)))
        mx = jnp.maximum(mx, sk)
        p_c, p_p, p_q, p_n = jnp.exp(s_c - mx), jnp.exp(s_p - mx), jnp.exp(s_q - mx), jnp.exp(s_n - mx)
        den = (jnp.sum(p_c, axis=1, keepdims=True) + jnp.sum(p_p, axis=1, keepdims=True)
               + jnp.sum(p_q, axis=1, keepdims=True) + jnp.sum(p_n, axis=1, keepdims=True) + jnp.exp(sk - mx))
        o = (jnp.dot(p_c.astype(BF16), vc, preferred_element_type=F32)
             + jnp.dot(p_p.astype(BF16), vp, preferred_element_type=F32)
             + jnp.dot(p_q.astype(BF16), vq, preferred_element_type=F32)
             + jnp.dot(p_n.astype(BF16), vn, preferred_element_type=F32)) / den
        _store_heads(out_ref, o, grp, BLOCK)


def _attn_lat_call(sink, aq, ak, av, kc, vc, batch, seq):
    n = ak.shape[0]
    nb = seq // BLOCK
    past = kc.shape[1]

    def cur(b, i):
        return (b * nb + i, 0)

    def prev(b, i):
        return (b * nb + jnp.maximum(i - 1, 0), 0)

    def nxt(b, i):
        return (b * nb + jnp.minimum(i + 1, nb - 1), 0)

    blk = functools.partial(pl.BlockSpec, (BLOCK, KV_WIDTH))
    cache = pl.BlockSpec((1, past, KV_WIDTH), lambda b, i: (b, 0, 0))
    return pl.pallas_call(
        _attn_lat_kernel, grid=(batch, nb),
        in_specs=[pl.BlockSpec(memory_space=pltpu.SMEM),
                  pl.BlockSpec((A_HEADS, BLOCK, KV_WIDTH), lambda b, i: (0, b * nb + i, 0)),
                  cache, cache, blk(prev), blk(cur), blk(nxt), blk(prev), blk(cur), blk(nxt)],
        out_specs=pl.BlockSpec((BLOCK, A_WIDTH), cur),
        out_shape=jax.ShapeDtypeStruct((n, A_WIDTH), BF16),
        compiler_params=pltpu.CompilerParams(dimension_semantics=("parallel", "parallel")),
        name="attn_lat",
    )(sink, aq, kc, vc, ak, ak, ak, av, av, av)


def _top16_rows(s, payload=None):
    n_rows = s.shape[0]
    rows = lax.broadcasted_iota(I32, s.shape, 0).astype(F32)
    vals, idxs, pays = [], [], []
    for _ in range(P_TOPK):
        mx = jnp.max(s, axis=0, keepdims=True)
        ix = jnp.min(jnp.where(s == mx, rows, float(n_rows)), axis=0, keepdims=True)
        hit = rows == ix
        vals.append(mx)
        idxs.append(ix)
        if payload is not None:
            pays.append(jnp.sum(jnp.where(hit, payload, 0.0), axis=0, keepdims=True))
        s = jnp.where(hit, -jnp.inf, s)
    out = (jnp.concatenate(vals, axis=0), jnp.concatenate(idxs, axis=0))
    if payload is not None:
        out += (jnp.concatenate(pays, axis=0),)
    return out


def _mix_kernel(x_ref, hf_ref, hb_ref, mo_ref, ao_ref, mod_ref, mhg_ref, n2_ref, wm_ref, wa_ref, wq_ref,
                sa_ref, sb_ref, x1_ref, h2_ref, eidx_ref, gate_ref, qp_scr, e_scr, g_scr):
    tm = x_ref.shape[0]
    hs = hf_ref[...] + hb_ref[...]
    parts = []
    for hd in range(M_HEADS):
        blk = hs[:, hd * M_DH:(hd + 1) * M_DH]
        parts.append(blk * lax.rsqrt(jnp.mean(blk * blk, axis=-1, keepdims=True) + EPS))
    m_out = _sigmoid(mo_ref[...]) * (jnp.concatenate(parts, axis=1) * mhg_ref[...])
    mix = (jnp.dot(m_out.astype(BF16), wm_ref[...], preferred_element_type=F32)
           + jnp.dot(ao_ref[...], wa_ref[...], preferred_element_type=F32))
    x1 = x_ref[...] + mod_ref[0, 2:3, :] * mix
    x1_ref[...] = x1
    h2 = x1 * lax.rsqrt(jnp.mean(x1 * x1, axis=-1, keepdims=True) + EPS) * n2_ref[...]
    h2 = h2 * (1.0 + mod_ref[0, 4:5, :]) + mod_ref[0, 3:4, :]
    h2_ref[...] = _pack_bf16_pairs(h2)
    qp = jnp.dot(h2.astype(BF16), wq_ref[...], preferred_element_type=F32)
    for p in range(P_HEADS):
        qp_scr[p] = qp[:, p * P_DKEY:(p + 1) * P_DKEY].astype(BF16)
    sub_a = sa_ref[...]
    sub_b = sb_ref[...]

    def head_body(p, carry):
        for half in range(tm // N_KEYS):
            cols = slice(half * N_KEYS, (half + 1) * N_KEYS)
            qh = qp_scr[p, pl.ds(half * N_KEYS, N_KEYS), :]
            s_a = _dot_t(sub_a, qh[:, 0:P_HALF])
            s_b = _dot_t(sub_b, qh[:, P_HALF:P_DKEY])
            va, ia = _top16_rows(s_a)
            vb, ib = _top16_rows(s_b)
            keep = [P_TOPK // (i + 1) for i in range(P_TOPK)]
            pad = -sum(keep) % 8
            cand = jnp.concatenate([va[i:i + 1, :] + vb[0:keep[i], :] for i in range(P_TOPK)]
                                   + [jnp.full((pad, N_KEYS), -jnp.inf, F32)], axis=0)
            cidx = jnp.concatenate([ia[i:i + 1, :] * float(N_KEYS) + ib[0:keep[i], :] for i in range(P_TOPK)]
                                   + [jnp.zeros((pad, N_KEYS), F32)], axis=0)
            top, _, eidx = _top16_rows(cand, cidx)
            ex = jnp.exp(top - jnp.max(top, axis=0, keepdims=True))
            gates = ex / jnp.sum(ex, axis=0, keepdims=True)
            r0 = pl.multiple_of(p * P_TOPK, P_TOPK)
            e_scr[pl.ds(r0, P_TOPK), cols] = eidx
            g_scr[pl.ds(r0, P_TOPK), cols] = gates
        return carry

    lax.fori_loop(0, P_HEADS, head_body, 0)
    for half in range(tm // N_KEYS):
        cols = slice(half * N_KEYS, (half + 1) * N_KEYS)
        eidx_ref[cols, :] = e_scr[:, cols].T.astype(I32)
        gate_ref[cols, :] = g_scr[:, cols].T


def _mix_call(x2d, tile0, seq, h_f, h_b, mo, a_out, mod3, mod_row0, mhg, n2, w_m, w_a, w_q, sub_a, sub_b):
    n = h_f.shape[0]
    tm = TOKEN_TILE
    per_seq = seq // tm

    def tok(i):
        return (i, 0)

    def const2(i):
        return (0, 0)

    in_specs = [
        pl.BlockSpec((tm, D_MODEL), lambda i: (tile0 + i, 0)),
        pl.BlockSpec((tm, M_WIDTH), tok), pl.BlockSpec((tm, M_WIDTH), tok), pl.BlockSpec((tm, M_WIDTH), tok),
        pl.BlockSpec((tm, A_WIDTH), tok),
        pl.BlockSpec((1, 6, D_MODEL), lambda i: (mod_row0 + (i // per_seq if mod_row0 else 0), 0, 0)),
        pl.BlockSpec((1, M_WIDTH), const2),
        pl.BlockSpec((1, D_MODEL), const2),
        pl.BlockSpec((M_WIDTH, D_MODEL), const2),
        pl.BlockSpec((A_WIDTH, D_MODEL), const2),
        pl.BlockSpec((D_MODEL, P_HEADS * P_DKEY), const2),
        pl.BlockSpec((N_KEYS, P_HALF), const2),
        pl.BlockSpec((N_KEYS, P_HALF), const2),
    ]
    out_specs = [pl.BlockSpec((tm, D_MODEL), tok), pl.BlockSpec((tm, D_MODEL // 2), tok),
                 pl.BlockSpec((tm, N_SEL), tok), pl.BlockSpec((tm, N_SEL), tok)]
    out_shape = [jax.ShapeDtypeStruct((n, D_MODEL), F32), jax.ShapeDtypeStruct((n, D_MODEL // 2), I32),
                 jax.ShapeDtypeStruct((n, N_SEL), I32), jax.ShapeDtypeStruct((n, N_SEL), F32)]
    return pl.pallas_call(
        _mix_kernel, grid=(n // tm,), in_specs=in_specs, out_specs=out_specs, out_shape=out_shape,
        scratch_shapes=[pltpu.VMEM((P_HEADS, tm, P_DKEY), BF16), pltpu.VMEM((N_SEL, tm), F32),
                        pltpu.VMEM((N_SEL, tm), F32)],
        compiler_params=pltpu.CompilerParams(dimension_semantics=("parallel",)),
        name="mix",
    )(x2d, h_f, h_b, mo, a_out, mod3, mhg, n2, w_m, w_a, w_q, sub_a, sub_b)


SC_LANES = 16
SC_CORES = 2
SC_SUBCORES = 16
SC_WORKERS = SC_CORES * SC_SUBCORES
SC_TOKENS = 16
SC_GROUP = SC_LANES
SC_NGROUPS = N_SEL // SC_GROUP
SC_SLOTS = 4
ROW_WORDS = D_MODEL // 2
SC_DOT_ROWS = 8
SC_DOT_PARTIALS = 2
SC_OWORDS = 16 * SC_LANES
HI_MASK = -65536
PACK_ROWS = 512
SCORE_TOKENS = 1024
SCORE_EXPERTS = 2048
DENSE_LATENT_CHUNKS = (1, 2, 4, 5, 6, 7)
GELU_C0 = 0.7978845608028654
GELU_C1 = 0.044715


def _pack_bf16_pairs(x):
    half = x.shape[1] // 2
    bits = lax.bitcast_convert_type(x.astype(BF16).astype(F32), I32)
    return (bits[:, :half] & HI_MASK) | lax.shift_right_logical(bits[:, half:], jnp.int32(16))


def _pack_tables_kernel(u_ref, v_ref, uv_ref, vp_ref, ub_ref):
    v_words = _pack_bf16_pairs(v_ref[...])
    uv_ref[:, 0:ROW_WORDS] = _pack_bf16_pairs(u_ref[...])
    uv_ref[:, ROW_WORDS:2 * ROW_WORDS] = v_words
    vp_ref[...] = v_words
    ub_ref[...] = u_ref[...].astype(BF16)


def _pack_tables_call(u_tab, v_tab):
    n_exp = u_tab.shape[0]
    blk = pl.BlockSpec((PACK_ROWS, D_MODEL), lambda i: (i, 0))
    half = pl.BlockSpec((PACK_ROWS, ROW_WORDS), lambda i: (i, 0))
    return pl.pallas_call(
        _pack_tables_kernel, grid=(n_exp // PACK_ROWS,), in_specs=[blk, blk], out_specs=[blk, half, blk],
        out_shape=[jax.ShapeDtypeStruct((n_exp, 2 * ROW_WORDS), I32), jax.ShapeDtypeStruct((n_exp, ROW_WORDS), I32),
                   jax.ShapeDtypeStruct((n_exp, D_MODEL), BF16)],
        compiler_params=pltpu.CompilerParams(dimension_semantics=("parallel",)),
        name="pack_tables",
    )(u_tab, v_tab)


def _unpack_bf16_pairs(words):
    hi = lax.bitcast_convert_type(words & HI_MASK, F32)
    lo = lax.bitcast_convert_type(lax.shift_left(words, jnp.int32(16)), F32)
    return jnp.concatenate([hi, lo], axis=1)


def _scores_kernel(x_ref, u_ref, o_ref):
    o_ref[...] = _pack_bf16_pairs(_dot_t(_unpack_bf16_pairs(x_ref[...]).astype(BF16), u_ref[...]))


def _scores_call(h2p, u_bf16):
    n = h2p.shape[0]
    n_exp = u_bf16.shape[0]
    tm = math.gcd(n, SCORE_TOKENS)
    return pl.pallas_call(
        _scores_kernel, grid=(n // tm, n_exp // SCORE_EXPERTS),
        in_specs=[pl.BlockSpec((tm, ROW_WORDS), lambda i, j: (i, 0)),
                  pl.BlockSpec((SCORE_EXPERTS, D_MODEL), lambda i, j: (j, 0))],
        out_specs=pl.BlockSpec((tm, SCORE_EXPERTS // 2), lambda i, j: (i, j)),
        out_shape=jax.ShapeDtypeStruct((n, n_exp // 2), I32),
        compiler_params=pltpu.CompilerParams(dimension_semantics=("parallel", "parallel")),
        name="expert_scores",
    )(h2p, u_bf16)


def _sc_gelu(a):
    z = GELU_C0 * (a + GELU_C1 * (a * a * a))
    tanh = 1.0 - 2.0 / (jnp.exp(2.0 * z) + 1.0)
    return 0.5 * a * (1.0 + tanh)


def _sc_split(words):
    return (plsc.bitcast(words & HI_MASK, F32), plsc.bitcast(lax.shift_left(words, jnp.int32(16)), F32))


def _sc_mul_bf16(a_words, b_words):
    return plsc.bitcast(a_words, BF16) * plsc.bitcast(b_words, BF16)


def _sc_split_sum(p, q):
    return _sc_split(plsc.bitcast(p + q, I32))


def _peer_sc_kernel(h2_hbm, eidx_hbm, gate_hbm, uv_hbm, out_hbm,
                    xbuf, ibuf, gbuf, obuf, uvbuf, mbuf, wbuf, sems, sems_in, sems_out):
    n = h2_hbm.shape[0]
    per_worker = n // SC_WORKERS
    blk_tokens = xbuf.shape[1]
    n_blocks = per_worker // blk_tokens
    wid = lax.axis_index("c") * SC_SUBCORES + lax.axis_index("s")
    lane = lax.iota(I32, SC_LANES)

    def split_item(item):
        return lax.shift_right_logical(item, SC_NGROUPS.bit_length() - 1), item & (SC_NGROUPS - 1)

    def gather_copies(bset, item, slot):
        t, g = split_item(item)
        idx = ibuf[bset, t, pl.ds(g * SC_GROUP, SC_GROUP)]
        return (pltpu.make_async_copy(uv_hbm.at[idx], uvbuf.at[slot], sems.at[slot]),)

    def dots(bset, t, slot):
        zero = jnp.zeros((SC_LANES,), F32)

        @pl.loop(0, SC_GROUP, step=SC_DOT_ROWS)
        def _(r0):
            accs = [[zero] * SC_DOT_PARTIALS for _ in range(SC_DOT_ROWS)]
            for k in range(0, ROW_WORDS // SC_LANES, 4):
                xs = [xbuf[bset, t, pl.ds((k + q) * SC_LANES, SC_LANES)] for q in range(4)]
                for i in range(SC_DOT_ROWS):
                    m = [_sc_mul_bf16(xs[q], uvbuf[slot, r0 + i, pl.ds((k + q) * SC_LANES, SC_LANES)])
                         for q in range(4)]
                    hi, lo = _sc_split_sum(m[0] + m[1], m[2] + m[3])
                    p = (k // 4) % SC_DOT_PARTIALS
                    accs[i][p] = accs[i][p] + (hi + lo)
            for i in range(SC_DOT_ROWS):
                mbuf[r0 + i, :] = functools.reduce(lambda a, b: a + b, accs[i])

        tot = zero
        for c in range(SC_LANES):
            tot = tot + plsc.load_gather(mbuf, [lane, jnp.full((SC_LANES,), c, I32)])
        return tot

    def accumulate(bset, t, slot):
        nv = SC_OWORDS // SC_LANES
        for oc in range(ROW_WORDS // SC_OWORDS):
            w0 = oc * SC_OWORDS
            accs = (tuple(obuf[bset, t, pl.ds(w0 + j * SC_LANES, SC_LANES)] for j in range(nv))
                    + tuple(obuf[bset, t, pl.ds(ROW_WORDS + w0 + j * SC_LANES, SC_LANES)] for j in range(nv)))

            def row_quad(rq, accs):
                r = 4 * rq
                ws = [plsc.load_gather(wbuf, [jnp.full((SC_LANES,), r + q, I32)]) for q in range(4)]
                his, los = [], []
                for j in range(nv):
                    m = [_sc_mul_bf16(ws[q], uvbuf[slot, r + q, pl.ds(ROW_WORDS + w0 + j * SC_LANES, SC_LANES)])
                         for q in range(4)]
                    hi, lo = _sc_split_sum(m[0] + m[1], m[2] + m[3])
                    his.append(accs[j] + hi)
                    los.append(accs[nv + j] + lo)
                return tuple(his) + tuple(los)

            accs = lax.fori_loop(0, SC_GROUP // 4, row_quad, accs)
            for j in range(nv):
                obuf[bset, t, pl.ds(w0 + j * SC_LANES, SC_LANES)] = accs[j]
                obuf[bset, t, pl.ds(ROW_WORDS + w0 + j * SC_LANES, SC_LANES)] = accs[nv + j]

    def pack_weights(w):
        bits = plsc.bitcast(w, I32)
        rounded = (bits + 0x7FFF + (lax.shift_right_logical(bits, jnp.int32(16)) & 1)) & HI_MASK
        return rounded | lax.shift_right_logical(rounded, jnp.int32(16))

    n_items = blk_tokens * SC_NGROUPS

    def block_rows(blk):
        return pl.ds(pl.multiple_of(wid * per_worker + blk * blk_tokens, blk_tokens), blk_tokens)

    def load_copies(blk, bset):
        rows = block_rows(blk)
        return (pltpu.make_async_copy(h2_hbm.at[rows], xbuf.at[bset], sems_in.at[bset]),
                pltpu.make_async_copy(eidx_hbm.at[rows], ibuf.at[bset], sems_in.at[bset]),
                pltpu.make_async_copy(gate_hbm.at[rows], gbuf.at[bset], sems_in.at[bset]))

    def store_copy(blk, bset):
        return pltpu.make_async_copy(obuf.at[bset], out_hbm.at[block_rows(blk)], sems_out.at[bset])

    for c in load_copies(0, 0):
        c.start()

    @pl.loop(0, n_blocks)
    def _(blk):
        bset = blk & 1
        for c in load_copies(blk, bset):
            c.wait()

        @pl.when(blk + 1 < n_blocks)
        def _():
            for c in load_copies(blk + 1, 1 - bset):
                c.start()

        @pl.when(blk >= 2)
        def _():
            store_copy(blk - 2, bset).wait()

        @pl.loop(0, blk_tokens)
        def _(t):
            zero = jnp.zeros((SC_LANES,), F32)
            for j in range(D_MODEL // SC_LANES):
                obuf[bset, t, pl.ds(j * SC_LANES, SC_LANES)] = zero

        for ahead in range(SC_SLOTS - 1):
            for c in gather_copies(bset, ahead, ahead):
                c.start()

        @pl.loop(0, n_items)
        def _(item):
            t, g = split_item(item)
            slot = item & (SC_SLOTS - 1)
            ahead = item + (SC_SLOTS - 1)

            @pl.when(ahead < n_items)
            def _():
                for c in gather_copies(bset, ahead, ahead & (SC_SLOTS - 1)):
                    c.start()

            for c in gather_copies(bset, item, slot):
                c.wait()
            a = dots(bset, t, slot)
            wbuf[...] = pack_weights(gbuf[bset, t, pl.ds(g * SC_GROUP, SC_GROUP)] * _sc_gelu(a))
            accumulate(bset, t, slot)

        store_copy(blk, bset).start()

    for blk in range(max(n_blocks - 2, 0), n_blocks):
        store_copy(blk, blk & 1).wait()


def _peer_experts(h2p, eidx, gates, uv_pack):
    n = h2p.shape[0]
    assert n % (8 * SC_WORKERS) == 0
    blk_tokens = math.gcd(n // SC_WORKERS, SC_TOKENS)
    mesh = plsc.VectorSubcoreMesh(core_axis_name="c", subcore_axis_name="s")
    fn = pl.kernel(
        _peer_sc_kernel,
        out_type=jax.ShapeDtypeStruct((n, D_MODEL), F32),
        mesh=mesh,
        scratch_types=[
            pltpu.VMEM((2, blk_tokens, ROW_WORDS), I32),
            pltpu.VMEM((2, blk_tokens, N_SEL), I32),
            pltpu.VMEM((2, blk_tokens, N_SEL), F32),
            pltpu.VMEM((2, blk_tokens, D_MODEL), F32),
            pltpu.VMEM((SC_SLOTS, SC_GROUP, 2 * ROW_WORDS), I32),
            pltpu.VMEM((SC_GROUP, SC_LANES), F32),
            pltpu.VMEM((SC_LANES,), I32),
            pltpu.SemaphoreType.DMA((SC_SLOTS,)), pltpu.SemaphoreType.DMA((2,)), pltpu.SemaphoreType.DMA((2,)),
        ],
        compiler_params=pltpu.CompilerParams(needs_layout_passes=False),
        cost_estimate=pl.CostEstimate(
            flops=4 * n * N_SEL * D_MODEL, transcendentals=n * N_SEL,
            bytes_accessed=4 * (2 * n * N_SEL * ROW_WORDS + n * ROW_WORDS + n * D_MODEL + 2 * n * N_SEL)),
        name="peer_experts",
    )
    return fn(h2p, eidx, gates, uv_pack)


def _peer_sc_scored_kernel(a_hbm, eidx_hbm, gate_hbm, v_hbm, out_hbm,
                           abuf, ibuf, gbuf, obuf, vbuf, wbuf, sems, sems_a):
    n = a_hbm.shape[0]
    per_worker = n // SC_WORKERS
    blk_tokens = ibuf.shape[0]
    wid = lax.axis_index("c") * SC_SUBCORES + lax.axis_index("s")

    def split_item(item):
        return lax.shift_right_logical(item, SC_NGROUPS.bit_length() - 1), item & (SC_NGROUPS - 1)

    def gather_copy(item, slot):
        t, g = split_item(item)
        idx = ibuf[t, pl.ds(g * SC_GROUP, SC_GROUP)]
        return pltpu.make_async_copy(v_hbm.at[idx], vbuf.at[slot], sems.at[slot])

    def accumulate(t, slot):
        nv = SC_OWORDS // SC_LANES
        for oc in range(ROW_WORDS // SC_OWORDS):
            w0 = oc * SC_OWORDS
            accs = (tuple(obuf[t, pl.ds(w0 + j * SC_LANES, SC_LANES)] for j in range(nv))
                    + tuple(obuf[t, pl.ds(ROW_WORDS + w0 + j * SC_LANES, SC_LANES)] for j in range(nv)))

            def row_quad(rq, accs):
                r = 4 * rq
                ws = [plsc.load_gather(wbuf, [jnp.full((SC_LANES,), r + q, I32)]) for q in range(4)]
                his, los = [], []
                for j in range(nv):
                    m = [_sc_mul_bf16(ws[q], vbuf[slot, r + q, pl.ds(w0 + j * SC_LANES, SC_LANES)])
                         for q in range(4)]
                    hi, lo = _sc_split_sum(m[0] + m[1], m[2] + m[3])
                    his.append(accs[j] + hi)
                    los.append(accs[nv + j] + lo)
                return tuple(his) + tuple(los)

            accs = lax.fori_loop(0, SC_GROUP // 4, row_quad, accs)
            for j in range(nv):
                obuf[t, pl.ds(w0 + j * SC_LANES, SC_LANES)] = accs[j]
                obuf[t, pl.ds(ROW_WORDS + w0 + j * SC_LANES, SC_LANES)] = accs[nv + j]

    def pack_weights(w):
        bits = plsc.bitcast(w, I32)
        rounded = (bits + 0x7FFF + (lax.shift_right_logical(bits, jnp.int32(16)) & 1)) & HI_MASK
        return rounded | lax.shift_right_logical(rounded, jnp.int32(16))

    n_items = blk_tokens * SC_NGROUPS

    @pl.loop(0, per_worker // blk_tokens)
    def _(blk):
        tok0 = pl.multiple_of(wid * per_worker + blk * blk_tokens, blk_tokens)

        def score_copy(t):
            return pltpu.make_async_copy(a_hbm.at[tok0 + t], abuf.at[t & 1], sems_a.at[t & 1])

        score_copy(0).start()
        pltpu.sync_copy(eidx_hbm.at[pl.ds(tok0, blk_tokens)], ibuf)
        pltpu.sync_copy(gate_hbm.at[pl.ds(tok0, blk_tokens)], gbuf)

        @pl.loop(0, blk_tokens)
        def _(t):
            zero = jnp.zeros((SC_LANES,), F32)
            for j in range(D_MODEL // SC_LANES):
                obuf[t, pl.ds(j * SC_LANES, SC_LANES)] = zero

        for ahead in range(SC_SLOTS - 1):
            gather_copy(ahead, ahead).start()

        @pl.loop(0, n_items)
        def _(item):
            t, g = split_item(item)
            slot = item & (SC_SLOTS - 1)
            ahead = item + (SC_SLOTS - 1)

            @pl.when(ahead < n_items)
            def _():
                gather_copy(ahead, ahead & (SC_SLOTS - 1)).start()

            @pl.when(g == 0)
            def _():
                score_copy(t).wait()

                @pl.when(t + 1 < blk_tokens)
                def _():
                    score_copy(t + 1).start()

            gather_copy(item, slot).wait()
            idx = ibuf[t, pl.ds(g * SC_GROUP, SC_GROUP)]
            half = SCORE_EXPERTS // 2
            word = plsc.load_gather(
                abuf, [jnp.full((SC_LANES,), t & 1, I32),
                       lax.shift_right_logical(idx, jnp.int32(SCORE_EXPERTS.bit_length() - 1)) * half
                       + (idx & (half - 1))])
            hi, lo = _sc_split(word)
            a = jnp.where((idx & half) == 0, hi, lo)
            wbuf[...] = pack_weights(gbuf[t, pl.ds(g * SC_GROUP, SC_GROUP)] * _sc_gelu(a))
            accumulate(t, slot)

        pltpu.sync_copy(obuf, out_hbm.at[pl.ds(tok0, blk_tokens)])


def _peer_experts_scored(scores, eidx, gates, v_pack):
    n, n_exp = scores.shape
    assert n % (8 * SC_WORKERS) == 0
    blk_tokens = math.gcd(n // SC_WORKERS, 2 * SC_TOKENS)
    mesh = plsc.VectorSubcoreMesh(core_axis_name="c", subcore_axis_name="s")
    fn = pl.kernel(
        _peer_sc_scored_kernel,
        out_type=jax.ShapeDtypeStruct((n, D_MODEL), F32),
        mesh=mesh,
        scratch_types=[
            pltpu.VMEM((2, n_exp), I32),
            pltpu.VMEM((blk_tokens, N_SEL), I32),
            pltpu.VMEM((blk_tokens, N_SEL), F32),
            pltpu.VMEM((blk_tokens, D_MODEL), F32),
            pltpu.VMEM((SC_SLOTS, SC_GROUP, ROW_WORDS), I32),
            pltpu.VMEM((SC_LANES,), I32),
            pltpu.SemaphoreType.DMA((SC_SLOTS,)), pltpu.SemaphoreType.DMA((2,)),
        ],
        compiler_params=pltpu.CompilerParams(needs_layout_passes=False),
        cost_estimate=pl.CostEstimate(
            flops=2 * n * N_SEL * D_MODEL, transcendentals=n * N_SEL,
            bytes_accessed=4 * (n * N_SEL * ROW_WORDS + n * n_exp + n * D_MODEL + 2 * n * N_SEL)),
        name="peer_experts_scored",
    )
    return fn(scores, eidx, gates, v_pack)


def _resid_kernel(x1_ref, p_ref, mod_ref, o_ref):
    o_ref[...] = x1_ref[...] + mod_ref[0, 5:6, :] * p_ref[...]


def _resid_call(x1, peer_out, seq, mod3, mod_row0):
    n = x1.shape[0]
    tm = TOKEN_TILE
    per_seq = seq // tm
    tok = pl.BlockSpec((tm, D_MODEL), lambda i: (i, 0))
    return pl.pallas_call(
        _resid_kernel, grid=(n // tm,),
        in_specs=[tok, tok,
                  pl.BlockSpec((1, 6, D_MODEL), lambda i: (mod_row0 + (i // per_seq if mod_row0 else 0), 0, 0))],
        out_specs=tok, out_shape=jax.ShapeDtypeStruct((n, D_MODEL), F32),
        compiler_params=pltpu.CompilerParams(dimension_semantics=("parallel",)),
        name="resid",
    )(x1, peer_out, mod3)


def _rope_tables(seq, rotate):
    if not rotate:
        return jnp.ones((seq, KV_WIDTH), F32), jnp.zeros((seq, KV_WIDTH), F32)
    quarter = A_DH // 4
    t = jnp.arange(seq)
    row = (t // GRID_W).astype(F32)
    col = (t % GRID_W).astype(F32)
    inv = ROPE_BASE ** (-jnp.arange(quarter, dtype=F32) / quarter)
    d = jnp.arange(A_DH)
    pos = jnp.where(d[None, :] < A_DH // 2, row[:, None], col[:, None])
    ang = pos * inv[d % quarter][None, :]
    sign = jnp.where((d % (A_DH // 2)) < quarter, -1.0, 1.0).astype(F32)
    cos = jnp.cos(ang)
    sin = jnp.sin(ang) * sign[None, :]
    return jnp.tile(cos, (1, KV_WIDTH // A_DH)), jnp.tile(sin, (1, KV_WIDTH // A_DH))


def _run_chunk(x, b0, batch, mod3, mod_row0, prm, cache, rotate, gate_on, dense_scores):
    (n1, n2, w_main, w_g, w_gt, b_g, b_gt, mhg, qg_t, kg_t, bd, sink, w_m, w_a, w_q, sub_a, sub_b,
     (uv_pack, v_pack, u_bf16)) = prm
    seq = x.shape[1]
    n = batch * seq
    x2d = x.reshape(x.shape[0] * seq, D_MODEL)
    tile0 = b0 * seq // TOKEN_TILE
    cos, sin = _rope_tables(seq, rotate)
    if gate_on is not None:
        (cos, sin), _ = lax.optimization_barrier(((cos, sin), gate_on))
    mq, mk, mv, mo, gcol, grow, aq, ak, av = _inproj_call(
        x2d, tile0, n, seq, mod3, mod_row0, n1, w_main, w_g, w_gt, b_g, b_gt, qg_t, kg_t, bd, cos, sin)
    kc, vc, c0, m0 = cache
    h_f, h_b, c_fin, m_fin = _mlstm_call(mq, mk, mv, gcol, grow, c0, m0, batch, seq)
    if kc is None:
        a_out = _attn_ctx_call(sink, aq, ak, av, batch, seq)
    else:
        a_out = _attn_lat_call(sink, aq, ak, av, kc, vc, batch, seq)
    x1, h2p, eidx, gates = _mix_call(x2d, tile0, seq, h_f, h_b, mo, a_out, mod3, mod_row0, mhg, n2, w_m, w_a,
                                     w_q, sub_a, sub_b)
    if dense_scores:
        peer_out = _peer_experts_scored(_scores_call(h2p, u_bf16), eidx, gates, v_pack)
    else:
        peer_out = _peer_experts(h2p, eidx, gates, uv_pack)
    y = _resid_call(x1, peer_out, seq, mod3, mod_row0).reshape(batch, seq, D_MODEL)
    return y, h2p, peer_out, ak, av, c_fin, m_fin


def _pack_state(C, n_vec, m):
    b = C.shape[0]
    caug = jnp.concatenate([C, jnp.broadcast_to(n_vec[..., None], C.shape)], axis=-1)
    caug = caug.reshape(b, 2 * M_HEADS, M_DH, 2 * M_DH)
    m_rep = jnp.broadcast_to(m.reshape(b, 2 * M_HEADS, 1, 1), (b, 2 * M_HEADS, 8, M_DH))
    return caug.astype(F32), m_rep.astype(F32)


def kernel(x_prompt, x_sample, c, cache_attn_k, cache_attn_v, state_mlstm_C, state_mlstm_n, state_mlstm_m,
           c_ctx, w_ada, b_ada, norm1_g, norm2_g, w_in, b_gates, mh_norm_g, q_norm_g, k_norm_g, sink_logits,
           w_out, peer_w_q, peer_sub_a, peer_sub_b, peer_u, peer_v):
    depth = w_ada.shape[0]
    assert depth == 1
    batch, seq, _ = x_prompt.shape
    dec_batch, dec_seq, _ = x_sample.shape
    assert dec_batch + 1 <= MOD_ROWS and batch % sum(CTX_CHUNK_WEIGHTS) == 0 and dec_batch % LATENT_CHUNKS == 0
    l = 0

    cond = jnp.concatenate([c_ctx[None, :], c, jnp.zeros((MOD_ROWS - 1 - dec_batch, D_MODEL), F32)], axis=0)
    mod3 = _ada_call(cond, w_ada[l], b_ada[l]).reshape(MOD_ROWS, 6, D_MODEL)

    wi = w_in[l]
    g0 = 4 * M_WIDTH
    w_main = jnp.concatenate([wi[:, :g0], wi[:, g0 + N_GATES:]], axis=1).astype(BF16)
    w_g = wi[:, g0:g0 + N_GATES]
    seg = jnp.arange(A_WIDTH) // A_DH
    bd = jnp.where(seg[:, None] == seg[None, :], 1.0 / A_DH, 0.0).astype(F32)
    prm = (norm1_g[l][None, :], norm2_g[l][None, :], w_main, w_g, w_g.T, b_gates[l][None, :], b_gates[l][:, None],
           mh_norm_g[l][None, :], jnp.tile(q_norm_g[l], A_HEADS)[None, :], jnp.tile(k_norm_g[l], A_KV)[None, :], bd,
           sink_logits[l], w_out[l][:M_WIDTH].astype(BF16), w_out[l][M_WIDTH:].astype(BF16),
           peer_w_q[l].astype(BF16), peer_sub_a[l].astype(BF16), peer_sub_b[l].astype(BF16),
           _pack_tables_call(peer_u[l], peer_v[l]))

    zeros_c = jnp.zeros((batch, 2, M_HEADS, M_DH, M_DH), F32)
    c0, m0 = _pack_state(zeros_c, zeros_c[..., 0], jnp.full((batch, 2, M_HEADS), NEG, F32))
    c0s, m0s = _pack_state(state_mlstm_C[:, l], state_mlstm_n[:, l], state_mlstm_m[:, l])
    past = cache_attn_k.shape[2]
    kc = cache_attn_k[:, l].reshape(dec_batch, past, KV_WIDTH)
    vc = cache_attn_v[:, l].reshape(dec_batch, past, KV_WIDTH)

    jobs = []
    ctx_sizes = [batch * f // sum(CTX_CHUNK_WEIGHTS) for f in CTX_CHUNK_WEIGHTS]
    assert sum(ctx_sizes) == batch
    b0 = 0
    for size in ctx_sizes:
        b1 = b0 + size
        jobs.append((x_prompt, b0, size, 0, (None, None, c0[b0:b1], m0[b0:b1]), False, False))
        b0 = b1
    for ci, b0 in enumerate(range(0, dec_batch, dec_batch // LATENT_CHUNKS)):
        b1 = b0 + dec_batch // LATENT_CHUNKS
        jobs.append((x_sample, b0, b1 - b0, 1 + b0, (kc[b0:b1], vc[b0:b1], c0s[b0:b1], m0s[b0:b1]), True,
                     ci in DENSE_LATENT_CHUNKS))
    outs = []
    for i, (x_all, b0, nb, mod_row0, cache, rotate, dense) in enumerate(jobs):
        gate = tuple(g for g in (outs[i - 1][1] if i >= 1 else None,
                                 outs[i - EXPERT_LAG][2] if i >= EXPERT_LAG else None) if g is not None)
        outs.append(_run_chunk(x_all, b0, nb, mod3, mod_row0, prm, cache, rotate, gate or None, dense))
    ctx, lat = outs[:len(ctx_sizes)], outs[len(ctx_sizes):]
    y_p = jnp.concatenate([o[0] for o in ctx], axis=0)
    y_s = jnp.concatenate([o[0] for o in lat], axis=0)
    k_new = jnp.concatenate([o[3] for o in ctx], axis=0)
    v_new = jnp.concatenate([o[4] for o in ctx], axis=0)
    c_fin = jnp.concatenate([o[5] for o in ctx], axis=0)
    m_fin = jnp.concatenate([o[6] for o in ctx], axis=0)

    c_fin = c_fin.reshape(batch, 2, M_HEADS, M_DH, 2 * M_DH)
    new_c = c_fin[..., :M_DH][:, None]
    new_n = c_fin[..., M_DH][:, None]
    new_m = m_fin[:, :, 0, 0].reshape(batch, 2, M_HEADS)[:, None]
    new_k = k_new.reshape(batch, 1, seq, A_KV, A_DH)
    new_v = v_new.reshape(batch, 1, seq, A_KV, A_DH)
    return y_p, y_s, new_k, new_v, new_c, new_n, new_m
```

```python
import functools
import math

import jax
import jax.numpy as jnp
from jax import lax
from jax.experimental import pallas as pl
from jax.experimental.pallas import tpu as pltpu
from jax.experimental.pallas import tpu_sc as plsc

F32 = jnp.float32
BF16 = jnp.bfloat16
I32 = jnp.int32
HI = lax.Precision.HIGHEST

D_MODEL = 1024
EPS = 1e-6
NEG = -1e30
GRID_W = 64
M_HEADS = 4
M_WIDTH = 512
M_DH = 128
A_HEADS = 8
A_KV = 2
A_REP = 4
A_DH = 64
A_WIDTH = 512
KV_WIDTH = A_KV * A_DH
BLOCK = 128
ROPE_BASE = 10000.0
N_KEYS = 128
P_HEADS = 8
P_DKEY = 256
P_HALF = 128
P_TOPK = 16
N_SEL = P_HEADS * P_TOPK
N_GATES = 4 * M_HEADS
MAIN_COLS = 4 * M_WIDTH + A_WIDTH + 2 * KV_WIDTH
MOD_ROWS = 16

TOKEN_TILE = 256
MLSTM_CHUNK = 128
ADA_COL_TILE = 768
CTX_CHUNK_WEIGHTS = (1, 2, 3, 4, 6)
EXPERT_LAG = 3
LATENT_CHUNKS = 8


def _sigmoid(x):
    return 1.0 / (1.0 + jnp.exp(-x))


def _log_sigmoid(x):
    return jnp.minimum(x, 0.0) - jnp.log1p(jnp.exp(-jnp.abs(x)))


def _dot_t(a, b, precision=None):
    return lax.dot_general(a, b, (((1,), (1,)), ((), ())), precision=precision,
                           preferred_element_type=F32)


def _ada_kernel(c_ref, w_ref, b_ref, o_ref):
    c = c_ref[...]
    s = c * _sigmoid(c)
    o_ref[...] = jnp.dot(s, w_ref[...], precision=HI, preferred_element_type=F32) + b_ref[...]


def _ada_call(cond, w_ada, b_ada):
    n_out = w_ada.shape[1]
    return pl.pallas_call(
        _ada_kernel,
        grid=(n_out // ADA_COL_TILE,),
        in_specs=[pl.BlockSpec((MOD_ROWS, D_MODEL), lambda j: (0, 0)),
                  pl.BlockSpec((D_MODEL, ADA_COL_TILE), lambda j: (0, j)),
                  pl.BlockSpec((1, ADA_COL_TILE), lambda j: (0, j))],
        out_specs=pl.BlockSpec((MOD_ROWS, ADA_COL_TILE), lambda j: (0, j)),
        out_shape=jax.ShapeDtypeStruct((MOD_ROWS, n_out), F32),
        name="ada",
    )(cond, w_ada, b_ada.reshape(1, n_out))


def _swap16(x):
    n = x.shape[-1]
    lane = lax.broadcasted_iota(I32, x.shape, x.ndim - 1)
    return jnp.where((lane & 16) == 0, pltpu.roll(x, n - 16, x.ndim - 1), pltpu.roll(x, 16, x.ndim - 1))


def _inproj_kernel(x_ref, mod_ref, n1_ref, w_ref, wg_ref, wgt_ref, bg_ref, bgt_ref, qg_ref, kg_ref,
                   bd_ref, cos_ref, sin_ref,
                   mq_ref, mk_ref, mv_ref, mo_ref, gc_ref, gr_ref, aq_ref, ak_ref, av_ref):
    x = x_ref[...]
    h = x * lax.rsqrt(jnp.mean(x * x, axis=-1, keepdims=True) + EPS) * n1_ref[...]
    h = h * (1.0 + mod_ref[0, 1:2, :]) + mod_ref[0, 0:1, :]
    z = jnp.dot(h.astype(BF16), w_ref[...], preferred_element_type=F32)

    mq_ref[...] = (z[:, 0:M_WIDTH] * (M_DH ** -0.5)).astype(BF16)
    mk_ref[...] = z[:, M_WIDTH:2 * M_WIDTH].astype(BF16)
    mv_ref[...] = z[:, 2 * M_WIDTH:3 * M_WIDTH].astype(BF16)
    mo_ref[...] = z[:, 3 * M_WIDTH:4 * M_WIDTH]

    g = jnp.dot(h, wg_ref[...], precision=HI, preferred_element_type=F32) + bg_ref[...]
    kind = lax.broadcasted_iota(I32, g.shape, 1) // M_HEADS
    gc_ref[...] = jnp.where((kind & 1) == 1, _log_sigmoid(g), g)
    gt = _dot_t(wgt_ref[...], h, precision=HI) + bgt_ref[...]
    kind_t = lax.broadcasted_iota(I32, gt.shape, 0) // M_HEADS
    gr_ref[...] = jnp.where((kind_t & 1) == 1, _log_sigmoid(gt), gt)

    o = 4 * M_WIDTH
    aq = z[:, o:o + A_WIDTH]
    ak = z[:, o + A_WIDTH:o + A_WIDTH + KV_WIDTH]
    av_ref[...] = z[:, o + A_WIDTH + KV_WIDTH:o + A_WIDTH + 2 * KV_WIDTH]
    bd = bd_ref[...]
    cos = cos_ref[...]
    sin = sin_ref[...]
    aq = aq * lax.rsqrt(jnp.dot(aq * aq, bd, precision=HI, preferred_element_type=F32) + EPS) * qg_ref[...]
    cos4 = jnp.concatenate([cos] * (A_WIDTH // KV_WIDTH), axis=1)
    sin4 = jnp.concatenate([sin] * (A_WIDTH // KV_WIDTH), axis=1)
    aq = (aq * cos4 + _swap16(aq) * sin4) * (A_DH ** -0.5)
    ak = ak * lax.rsqrt(jnp.dot(ak * ak, bd[0:KV_WIDTH, 0:KV_WIDTH], precision=HI,
                                preferred_element_type=F32) + EPS) * kg_ref[...]
    ak_ref[...] = ak * cos + _swap16(ak) * sin

    lane = lax.broadcasted_iota(I32, (aq.shape[0], KV_WIDTH), 1)
    for hd in range(A_HEADS):
        grp = hd // A_REP
        blk = aq[:, (hd // 2) * KV_WIDTH:(hd // 2 + 1) * KV_WIDTH]
        if hd % 2 != grp:
            blk = pltpu.roll(blk, A_DH, 1)
        keep = (lane >= grp * A_DH) & (lane < (grp + 1) * A_DH)
        aq_ref[hd] = jnp.where(keep, blk, 0.0).astype(BF16)


def _inproj_call(x2d, tile0, n, seq, mod3, mod_row0, n1, w_main, w_g, w_gt, b_g, b_gt, qg_t, kg_t, bd, cos, sin):
    tm = TOKEN_TILE
    per_seq = seq // tm

    def tok(i):
        return (i, 0)

    def const2(i):
        return (0, 0)

    in_specs = [
        pl.BlockSpec((tm, D_MODEL), lambda i: (tile0 + i, 0)),
        pl.BlockSpec((1, 6, D_MODEL), lambda i: (mod_row0 + (i // per_seq if mod_row0 else 0), 0, 0)),
        pl.BlockSpec((1, D_MODEL), const2),
        pl.BlockSpec((D_MODEL, MAIN_COLS), const2),
        pl.BlockSpec((D_MODEL, N_GATES), const2),
        pl.BlockSpec((N_GATES, D_MODEL), const2),
        pl.BlockSpec((1, N_GATES), const2),
        pl.BlockSpec((N_GATES, 1), const2),
        pl.BlockSpec((1, A_WIDTH), const2),
        pl.BlockSpec((1, KV_WIDTH), const2),
        pl.BlockSpec((A_WIDTH, A_WIDTH), const2),
        pl.BlockSpec((tm, KV_WIDTH), lambda i: (i % per_seq, 0)),
        pl.BlockSpec((tm, KV_WIDTH), lambda i: (i % per_seq, 0)),
    ]
    out_specs = [
        pl.BlockSpec((tm, M_WIDTH), tok),
        pl.BlockSpec((tm, M_WIDTH), tok),
        pl.BlockSpec((tm, M_WIDTH), tok),
        pl.BlockSpec((tm, M_WIDTH), tok),
        pl.BlockSpec((tm, N_GATES), tok),
        pl.BlockSpec((N_GATES, tm), lambda i: (0, i)),
        pl.BlockSpec((A_HEADS, tm, KV_WIDTH), lambda i: (0, i, 0)),
        pl.BlockSpec((tm, KV_WIDTH), tok),
        pl.BlockSpec((tm, KV_WIDTH), tok),
    ]
    out_shape = [
        jax.ShapeDtypeStruct((n, M_WIDTH), BF16),
        jax.ShapeDtypeStruct((n, M_WIDTH), BF16),
        jax.ShapeDtypeStruct((n, M_WIDTH), BF16),
        jax.ShapeDtypeStruct((n, M_WIDTH), F32),
        jax.ShapeDtypeStruct((n, N_GATES), F32),
        jax.ShapeDtypeStruct((N_GATES, n), F32),
        jax.ShapeDtypeStruct((A_HEADS, n, KV_WIDTH), BF16),
        jax.ShapeDtypeStruct((n, KV_WIDTH), F32),
        jax.ShapeDtypeStruct((n, KV_WIDTH), F32),
    ]
    return pl.pallas_call(
        _inproj_kernel, grid=(n // tm,), in_specs=in_specs, out_specs=out_specs, out_shape=out_shape,
        compiler_params=pltpu.CompilerParams(dimension_semantics=("parallel",)),
        name="inproj",
    )(x2d, mod3, n1, w_main, w_g, w_gt, b_g, b_gt, qg_t, kg_t, bd, cos, sin)


def _mlstm_chain(q, k, v, li_c, b_c, li_r, b_r, caug, m, mask, reverse):
    L = q.shape[0]
    last = 0 if reverse else L - 1
    a_inter = b_c + m
    d = jnp.where(mask, b_c - b_r + li_r, -jnp.inf)
    m_t = jnp.maximum(a_inter, jnp.max(d, axis=1, keepdims=True))
    w_inter = jnp.exp(a_inter - m_t)
    s = _dot_t(q, k) * jnp.exp(d - m_t)
    qc = jnp.dot(q, caug.astype(BF16), preferred_element_type=F32)
    num = jnp.dot(s.astype(BF16), v, preferred_element_type=F32) + w_inter * qc[:, 0:M_DH]
    den = jnp.sum(s, axis=1, keepdims=True) + w_inter * qc[:, M_DH:M_DH + 1]
    den = jnp.maximum(jnp.abs(den), jnp.exp(-m_t))
    h = num / den
    m_new = m_t[last:last + 1, :]
    b_last = b_c[last:last + 1, 0:1]
    g_c = jnp.exp(b_last - b_c[:, 0:1] + li_c - m_new)
    decay = jnp.exp(b_last + m - m_new)
    kw = (k.astype(F32) * g_c).astype(BF16)
    vaug = jnp.concatenate([v, jnp.ones_like(v)], axis=1)
    upd = lax.dot_general(kw, vaug, (((0,), (0,)), ((), ())), preferred_element_type=F32)
    return h, decay * caug + upd, m_new


def _mlstm_kernel(qf_ref, kf_ref, vf_ref, gcf_ref, grf_ref, qb_ref, kb_ref, vb_ref, gcb_ref, grb_ref,
                  c0_ref, m0_ref, hf_ref, hb_ref, cfin_ref, mfin_ref, c_scr, m_scr):
    c = pl.program_id(1)
    nc = pl.num_programs(1)
    L = qf_ref.shape[0]

    @pl.when(c == 0)
    def _():
        c_scr[...] = c0_ref[0]
        m_scr[...] = m0_ref[0]

    row = lax.broadcasted_iota(I32, (L, L), 0)
    col = lax.broadcasted_iota(I32, (L, L), 1)
    lower = row >= col
    upper = row <= col
    lower_f = lower.astype(F32)
    upper_f = upper.astype(F32)

    for direction in range(2):
        reverse = direction == 1
        q_ref, k_ref, v_ref, gc_ref, gr_ref, h_ref = (
            (qb_ref, kb_ref, vb_ref, gcb_ref, grb_ref, hb_ref) if reverse
            else (qf_ref, kf_ref, vf_ref, gcf_ref, grf_ref, hf_ref))
        tri, tri_t, mask = (upper_f, lower_f, upper) if reverse else (lower_f, upper_f, lower)
        gc = gc_ref[...]
        gr = gr_ref[...]
        bc = jnp.dot(tri, gc, precision=HI, preferred_element_type=F32)
        br = jnp.dot(gr, tri_t, precision=HI, preferred_element_type=F32)
        for hd in range(M_HEADS):
            ch = direction * M_HEADS + hd
            sl = slice(hd * M_DH, (hd + 1) * M_DH)
            ci = 2 * direction * M_HEADS + hd
            cf = ci + M_HEADS
            h, caug, m_new = _mlstm_chain(
                q_ref[:, sl], k_ref[:, sl], v_ref[:, sl],
                gc[:, ci:ci + 1], bc[:, cf:cf + 1], gr[ci:ci + 1, :], br[cf:cf + 1, :],
                c_scr[ch], m_scr[ch][0:1, 0:1], mask, reverse)
            h_ref[:, sl] = h
            c_scr[ch] = caug
            m_scr[ch] = jnp.broadcast_to(m_new, m_scr.shape[1:])

    @pl.when(c == nc - 1)
    def _():
        cfin_ref[0] = c_scr[...]
        mfin_ref[0] = m_scr[...]


def _mlstm_call(mq, mk, mv, gcol, grow, c0, m0, batch, seq):
    n = mq.shape[0]
    L = MLSTM_CHUNK
    nc = seq // L
    n_ch = 2 * M_HEADS

    def fwd(b, c):
        return (b * nc + c, 0)

    def bwd(b, c):
        return (b * nc + nc - 1 - c, 0)

    def fwd_t(b, c):
        return (0, b * nc + c)

    def bwd_t(b, c):
        return (0, b * nc + nc - 1 - c)

    tok = pl.BlockSpec((L, M_WIDTH), fwd)
    tok_b = pl.BlockSpec((L, M_WIDTH), bwd)
    in_specs = [tok, tok, tok, pl.BlockSpec((L, N_GATES), fwd), pl.BlockSpec((N_GATES, L), fwd_t),
                tok_b, tok_b, tok_b, pl.BlockSpec((L, N_GATES), bwd), pl.BlockSpec((N_GATES, L), bwd_t),
                pl.BlockSpec((1, n_ch, M_DH, 2 * M_DH), lambda b, c: (b, 0, 0, 0)),
                pl.BlockSpec((1, n_ch, 8, M_DH), lambda b, c: (b, 0, 0, 0))]
    out_specs = [tok, tok_b,
                 pl.BlockSpec((1, n_ch, M_DH, 2 * M_DH), lambda b, c: (b, 0, 0, 0)),
                 pl.BlockSpec((1, n_ch, 8, M_DH), lambda b, c: (b, 0, 0, 0))]
    out_shape = [jax.ShapeDtypeStruct((n, M_WIDTH), F32), jax.ShapeDtypeStruct((n, M_WIDTH), F32),
                 jax.ShapeDtypeStruct((batch, n_ch, M_DH, 2 * M_DH), F32),
                 jax.ShapeDtypeStruct((batch, n_ch, 8, M_DH), F32)]
    return pl.pallas_call(
        _mlstm_kernel, grid=(batch, nc), in_specs=in_specs, out_specs=out_specs, out_shape=out_shape,
        scratch_shapes=[pltpu.VMEM((n_ch, M_DH, 2 * M_DH), F32), pltpu.VMEM((n_ch, 8, M_DH), F32)],
        compiler_params=pltpu.CompilerParams(dimension_semantics=("parallel", "arbitrary")),
        name="mlstm",
    )(mq, mk, mv, gcol, grow, mq, mk, mv, gcol, grow, c0, m0)


def _sink_column(sink_ref, grp, rows_per_head):
    return jnp.concatenate(
        [jnp.full((rows_per_head, 1), sink_ref[grp * A_REP + r], F32) for r in range(A_REP)], axis=0)


def _store_heads(out_ref, o, grp, rows_per_head):
    for r in range(A_REP):
        hd = grp * A_REP + r
        out_ref[:, hd * A_DH:(hd + 1) * A_DH] = o[r * rows_per_head:(r + 1) * rows_per_head,
                                                  grp * A_DH:(grp + 1) * A_DH].astype(out_ref.dtype)


def _attn_ctx_kernel(sink_ref, q_ref, k_ref, v_ref, out_ref):
    s_len = k_ref.shape[0]
    k = k_ref[...].astype(BF16)
    v = v_ref[...].astype(BF16)
    for grp in range(A_KV):
        q = q_ref[grp * A_REP:(grp + 1) * A_REP].reshape(A_REP * s_len, KV_WIDTH)
        s = _dot_t(q, k)
        sk = _sink_column(sink_ref, grp, s_len)
        mx = jnp.maximum(jnp.max(s, axis=1, keepdims=True), sk)
        p = jnp.exp(s - mx)
        den = jnp.sum(p, axis=1, keepdims=True) + jnp.exp(sk - mx)
        o = jnp.dot(p.astype(BF16), v, preferred_element_type=F32) / den
        _store_heads(out_ref, o, grp, s_len)


def _attn_ctx_call(sink, aq, ak, av, batch, seq):
    n = ak.shape[0]
    return pl.pallas_call(
        _attn_ctx_kernel, grid=(batch,),
        in_specs=[pl.BlockSpec(memory_space=pltpu.SMEM),
                  pl.BlockSpec((A_HEADS, seq, KV_WIDTH), lambda b: (0, b, 0)),
                  pl.BlockSpec((seq, KV_WIDTH), lambda b: (b, 0)),
                  pl.BlockSpec((seq, KV_WIDTH), lambda b: (b, 0))],
        out_specs=pl.BlockSpec((seq, A_WIDTH), lambda b: (b, 0)),
        out_shape=jax.ShapeDtypeStruct((n, A_WIDTH), BF16),
        compiler_params=pltpu.CompilerParams(dimension_semantics=("parallel",)),
        name="attn_ctx",
    )(sink, aq, ak, av)


def _attn_lat_kernel(sink_ref, q_ref, kc_ref, vc_ref, kp_ref, kq_ref, kn_ref, vp_ref, vq_ref, vn_ref, out_ref):
    i = pl.program_id(1)
    nb = pl.num_programs(1)
    kc = kc_ref[0].astype(BF16)
    vc = vc_ref[0].astype(BF16)
    kp, kq, kn = kp_ref[...].astype(BF16), kq_ref[...].astype(BF16), kn_ref[...].astype(BF16)
    vp, vq, vn = vp_ref[...].astype(BF16), vq_ref[...].astype(BF16), vn_ref[...].astype(BF16)
    rows = A_REP * BLOCK
    qpos = lax.broadcasted_iota(I32, (rows, BLOCK), 0) % BLOCK
    kpos = lax.broadcasted_iota(I32, (rows, BLOCK), 1)
    mask_p = (kpos >= qpos) & (i > 0)
    mask_n = (kpos <= qpos) & (i < nb - 1)
    for grp in range(A_KV):
        q = q_ref[grp * A_REP:(grp + 1) * A_REP].reshape(rows, KV_WIDTH)
        s_c = _dot_t(q, kc)
        s_p = jnp.where(mask_p, _dot_t(q, kp), NEG)
        s_q = _dot_t(q, kq)
        s_n = jnp.where(mask_n, _dot_t(q, kn), NEG)
        sk = _sink_column(sink_ref, grp, BLOCK)
        mx = jnp.maximum(jnp.maximum(jnp.max(s_c, axis=1, keepdims=True), jnp.max(s_p, axis=1, keepdims=True)),
                         jnp.maximum(jnp.max(s_q, axis=1, keepdims=True), jnp.max(s_n, axis=1, keepdims=True)))
        mx = jnp.maximum(mx, sk)
        p_c, p_p, p_q, p_n = jnp.exp(s_c - mx), jnp.exp(s_p - mx), jnp.exp(s_q - mx), jnp.exp(s_n - mx)
        den = (jnp.sum(p_c, axis=1, keepdims=True) + jnp.sum(p_p, axis=1, keepdims=True)
               + jnp.sum(p_q, axis=1, keepdims=True) + jnp.sum(p_n, axis=1, keepdims=True) + jnp.exp(sk - mx))
        o = (jnp.dot(p_c.astype(BF16), vc, preferred_element_type=F32)
             + jnp.dot(p_p.astype(BF16), vp, preferred_element_type=F32)
             + jnp.dot(p_q.astype(BF16), vq, preferred_element_type=F32)
             + jnp.dot(p_n.astype(BF16), vn, preferred_element_type=F32)) / den
        _store_heads(out_ref, o, grp, BLOCK)


def _attn_lat_call(sink, aq, ak, av, kc, vc, batch, seq):
    n = ak.shape[0]
    nb = seq // BLOCK
    past = kc.shape[1]

    def cur(b, i):
        return (b * nb + i, 0)

    def prev(b, i):
        return (b * nb + jnp.maximum(i - 1, 0), 0)

    def nxt(b, i):
        return (b * nb + jnp.minimum(i + 1, nb - 1), 0)

    blk = functools.partial(pl.BlockSpec, (BLOCK, KV_WIDTH))
    cache = pl.BlockSpec((1, past, KV_WIDTH), lambda b, i: (b, 0, 0))
    return pl.pallas_call(
        _attn_lat_kernel, grid=(batch, nb),
        in_specs=[pl.BlockSpec(memory_space=pltpu.SMEM),
                  pl.BlockSpec((A_HEADS, BLOCK, KV_WIDTH), lambda b, i: (0, b * nb + i, 0)),
                  cache, cache, blk(prev), blk(cur), blk(nxt), blk(prev), blk(cur), blk(nxt)],
        out_specs=pl.BlockSpec((BLOCK, A_WIDTH), cur),
        out_shape=jax.ShapeDtypeStruct((n, A_WIDTH), BF16),
        compiler_params=pltpu.CompilerParams(dimension_semantics=("parallel", "parallel")),
        name="attn_lat",
    )(sink, aq, kc, vc, ak, ak, ak, av, av, av)


def _top16_rows(s, payload=None):
    n_rows = s.shape[0]
    rows = lax.broadcasted_iota(I32, s.shape, 0).astype(F32)
    vals, idxs, pays = [], [], []
    for _ in range(P_TOPK):
        mx = jnp.max(s, axis=0, keepdims=True)
        ix = jnp.min(jnp.where(s == mx, rows, float(n_rows)), axis=0, keepdims=True)
        hit = rows == ix
        vals.append(mx)
        idxs.append(ix)
        if payload is not None:
            pays.append(jnp.sum(jnp.where(hit, payload, 0.0), axis=0, keepdims=True))
        s = jnp.where(hit, -jnp.inf, s)
    out = (jnp.concatenate(vals, axis=0), jnp.concatenate(idxs, axis=0))
    if payload is not None:
        out += (jnp.concatenate(pays, axis=0),)
    return out


def _mix_kernel(x_ref, hf_ref, hb_ref, mo_ref, ao_ref, mod_ref, mhg_ref, n2_ref, wm_ref, wa_ref, wq_ref,
                sa_ref, sb_ref, x1_ref, h2_ref, eidx_ref, gate_ref, qp_scr, e_scr, g_scr):
    tm = x_ref.shape[0]
    hs = hf_ref[...] + hb_ref[...]
    parts = []
    for hd in range(M_HEADS):
        blk = hs[:, hd * M_DH:(hd + 1) * M_DH]
        parts.append(blk * lax.rsqrt(jnp.mean(blk * blk, axis=-1, keepdims=True) + EPS))
    m_out = _sigmoid(mo_ref[...]) * (jnp.concatenate(parts, axis=1) * mhg_ref[...])
    mix = (jnp.dot(m_out.astype(BF16), wm_ref[...], preferred_element_type=F32)
           + jnp.dot(ao_ref[...], wa_ref[...], preferred_element_type=F32))
    x1 = x_ref[...] + mod_ref[0, 2:3, :] * mix
    x1_ref[...] = x1
    h2 = x1 * lax.rsqrt(jnp.mean(x1 * x1, axis=-1, keepdims=True) + EPS) * n2_ref[...]
    h2 = h2 * (1.0 + mod_ref[0, 4:5, :]) + mod_ref[0, 3:4, :]
    h2_ref[...] = _pack_bf16_pairs(h2)
    qp = jnp.dot(h2.astype(BF16), wq_ref[...], preferred_element_type=F32)
    for p in range(P_HEADS):
        qp_scr[p] = qp[:, p * P_DKEY:(p + 1) * P_DKEY].astype(BF16)
    sub_a = sa_ref[...]
    sub_b = sb_ref[...]

    def head_body(p, carry):
        for half in range(tm // N_KEYS):
            cols = slice(half * N_KEYS, (half + 1) * N_KEYS)
            qh = qp_scr[p, pl.ds(half * N_KEYS, N_KEYS), :]
            s_a = _dot_t(sub_a, qh[:, 0:P_HALF])
            s_b = _dot_t(sub_b, qh[:, P_HALF:P_DKEY])
            va, ia = _top16_rows(s_a)
            vb, ib = _top16_rows(s_b)
            keep = [P_TOPK // (i + 1) for i in range(P_TOPK)]
            pad = -sum(keep) % 8
            cand = jnp.concatenate([va[i:i + 1, :] + vb[0:keep[i], :] for i in range(P_TOPK)]
                                   + [jnp.full((pad, N_KEYS), -jnp.inf, F32)], axis=0)
            cidx = jnp.concatenate([ia[i:i + 1, :] * float(N_KEYS) + ib[0:keep[i], :] for i in range(P_TOPK)]
                                   + [jnp.zeros((pad, N_KEYS), F32)], axis=0)
            top, _, eidx = _top16_rows(cand, cidx)
            ex = jnp.exp(top - jnp.max(top, axis=0, keepdims=True))
            gates = ex / jnp.sum(ex, axis=0, keepdims=True)
            r0 = pl.multiple_of(p * P_TOPK, P_TOPK)
            e_scr[pl.ds(r0, P_TOPK), cols] = eidx
            g_scr[pl.ds(r0, P_TOPK), cols] = gates
        return carry

    lax.fori_loop(0, P_HEADS, head_body, 0)
    for half in range(tm // N_KEYS):
        cols = slice(half * N_KEYS, (half + 1) * N_KEYS)
        eidx_ref[cols, :] = e_scr[:, cols].T.astype(I32)
        gate_ref[cols, :] = g_scr[:, cols].T


def _mix_call(x2d, tile0, seq, h_f, h_b, mo, a_out, mod3, mod_row0, mhg, n2, w_m, w_a, w_q, sub_a, sub_b):
    n = h_f.shape[0]
    tm = TOKEN_TILE
    per_seq = seq // tm

    def tok(i):
        return (i, 0)

    def const2(i):
        return (0, 0)

    in_specs = [
        pl.BlockSpec((tm, D_MODEL), lambda i: (tile0 + i, 0)),
        pl.BlockSpec((tm, M_WIDTH), tok), pl.BlockSpec((tm, M_WIDTH), tok), pl.BlockSpec((tm, M_WIDTH), tok),
        pl.BlockSpec((tm, A_WIDTH), tok),
        pl.BlockSpec((1, 6, D_MODEL), lambda i: (mod_row0 + (i // per_seq if mod_row0 else 0), 0, 0)),
        pl.BlockSpec((1, M_WIDTH), const2),
        pl.BlockSpec((1, D_MODEL), const2),
        pl.BlockSpec((M_WIDTH, D_MODEL), const2),
        pl.BlockSpec((A_WIDTH, D_MODEL), const2),
        pl.BlockSpec((D_MODEL, P_HEADS * P_DKEY), const2),
        pl.BlockSpec((N_KEYS, P_HALF), const2),
        pl.BlockSpec((N_KEYS, P_HALF), const2),
    ]
    out_specs = [pl.BlockSpec((tm, D_MODEL), tok), pl.BlockSpec((tm, D_MODEL // 2), tok),
                 pl.BlockSpec((tm, N_SEL), tok), pl.BlockSpec((tm, N_SEL), tok)]
    out_shape = [jax.ShapeDtypeStruct((n, D_MODEL), F32), jax.ShapeDtypeStruct((n, D_MODEL // 2), I32),
                 jax.ShapeDtypeStruct((n, N_SEL), I32), jax.ShapeDtypeStruct((n, N_SEL), F32)]
    return pl.pallas_call(
        _mix_kernel, grid=(n // tm,), in_specs=in_specs, out_specs=out_specs, out_shape=out_shape,
        scratch_shapes=[pltpu.VMEM((P_HEADS, tm, P_DKEY), BF16), pltpu.VMEM((N_SEL, tm), F32),
                        pltpu.VMEM((N_SEL, tm), F32)],
        compiler_params=pltpu.CompilerParams(dimension_semantics=("parallel",)),
        name="mix",
    )(x2d, h_f, h_b, mo, a_out, mod3, mhg, n2, w_m, w_a, w_q, sub_a, sub_b)


SC_LANES = 16
SC_CORES = 2
SC_SUBCORES = 16
SC_WORKERS = SC_CORES * SC_SUBCORES
SC_TOKENS = 16
SC_GROUP = SC_LANES
SC_NGROUPS = N_SEL // SC_GROUP
SC_SLOTS = 4
ROW_WORDS = D_MODEL // 2
SC_DOT_ROWS = 8
SC_DOT_PARTIALS = 2
SC_OWORDS = 16 * SC_LANES
HI_MASK = -65536
PACK_ROWS = 512
SCORE_TOKENS = 1024
SCORE_EXPERTS = 2048
DENSE_LATENT_CHUNKS = (1, 2, 3, 4, 5, 6, 7)
GELU_C0 = 0.7978845608028654
GELU_C1 = 0.044715


def _pack_bf16_pairs(x):
    half = x.shape[1] // 2
    bits = lax.bitcast_convert_type(x.astype(BF16).astype(F32), I32)
    return (bits[:, :half] & HI_MASK) | lax.shift_right_logical(bits[:, half:], jnp.int32(16))


def _pack_tables_kernel(u_ref, v_ref, uv_ref, vp_ref, ub_ref):
    v_words = _pack_bf16_pairs(v_ref[...])
    uv_ref[:, 0:ROW_WORDS] = _pack_bf16_pairs(u_ref[...])
    uv_ref[:, ROW_WORDS:2 * ROW_WORDS] = v_words
    vp_ref[...] = v_words
    ub_ref[...] = u_ref[...].astype(BF16)


def _pack_tables_call(u_tab, v_tab):
    n_exp = u_tab.shape[0]
    blk = pl.BlockSpec((PACK_ROWS, D_MODEL), lambda i: (i, 0))
    half = pl.BlockSpec((PACK_ROWS, ROW_WORDS), lambda i: (i, 0))
    return pl.pallas_call(
        _pack_tables_kernel, grid=(n_exp // PACK_ROWS,), in_specs=[blk, blk], out_specs=[blk, half, blk],
        out_shape=[jax.ShapeDtypeStruct((n_exp, 2 * ROW_WORDS), I32), jax.ShapeDtypeStruct((n_exp, ROW_WORDS), I32),
                   jax.ShapeDtypeStruct((n_exp, D_MODEL), BF16)],
        compiler_params=pltpu.CompilerParams(dimension_semantics=("parallel",)),
        name="pack_tables",
    )(u_tab, v_tab)


def _unpack_bf16_pairs(words):
    hi = lax.bitcast_convert_type(words & HI_MASK, F32)
    lo = lax.bitcast_convert_type(lax.shift_left(words, jnp.int32(16)), F32)
    return jnp.concatenate([hi, lo], axis=1)


def _scores_kernel(x_ref, u_ref, o_ref):
    o_ref[...] = _pack_bf16_pairs(_dot_t(_unpack_bf16_pairs(x_ref[...]).astype(BF16), u_ref[...]))


def _scores_call(h2p, u_bf16):
    n = h2p.shape[0]
    n_exp = u_bf16.shape[0]
    tm = math.gcd(n, SCORE_TOKENS)
    return pl.pallas_call(
        _scores_kernel, grid=(n // tm, n_exp // SCORE_EXPERTS),
        in_specs=[pl.BlockSpec((tm, ROW_WORDS), lambda i, j: (i, 0)),
                  pl.BlockSpec((SCORE_EXPERTS, D_MODEL), lambda i, j: (j, 0))],
        out_specs=pl.BlockSpec((tm, SCORE_EXPERTS // 2), lambda i, j: (i, j)),
        out_shape=jax.ShapeDtypeStruct((n, n_exp // 2), I32),
        compiler_params=pltpu.CompilerParams(dimension_semantics=("parallel", "parallel")),
        name="expert_scores",
    )(h2p, u_bf16)


def _sc_gelu(a):
    z = GELU_C0 * (a + GELU_C1 * (a * a * a))
    tanh = 1.0 - 2.0 / (jnp.exp(2.0 * z) + 1.0)
    return 0.5 * a * (1.0 + tanh)


def _sc_split(words):
    return (plsc.bitcast(words & HI_MASK, F32), plsc.bitcast(lax.shift_left(words, jnp.int32(16)), F32))


def _sc_mul_bf16(a_words, b_words):
    return plsc.bitcast(a_words, BF16) * plsc.bitcast(b_words, BF16)


def _sc_split_sum(p, q):
    return _sc_split(plsc.bitcast(p + q, I32))


def _peer_sc_kernel(h2_hbm, eidx_hbm, gate_hbm, uv_hbm, out_hbm,
                    xbuf, ibuf, gbuf, obuf, uvbuf, mbuf, wbuf, sems, sems_in, sems_out):
    n = h2_hbm.shape[0]
    per_worker = n // SC_WORKERS
    blk_tokens = xbuf.shape[1]
    n_blocks = per_worker // blk_tokens
    wid = lax.axis_index("c") * SC_SUBCORES + lax.axis_index("s")
    lane = lax.iota(I32, SC_LANES)

    def split_item(item):
        return lax.shift_right_logical(item, SC_NGROUPS.bit_length() - 1), item & (SC_NGROUPS - 1)

    def gather_copies(bset, item, slot):
        t, g = split_item(item)
        idx = ibuf[bset, t, pl.ds(g * SC_GROUP, SC_GROUP)]
        return (pltpu.make_async_copy(uv_hbm.at[idx], uvbuf.at[slot], sems.at[slot]),)

    def dots(bset, t, slot):
        zero = jnp.zeros((SC_LANES,), F32)

        @pl.loop(0, SC_GROUP, step=SC_DOT_ROWS)
        def _(r0):
            accs = [[zero] * SC_DOT_PARTIALS for _ in range(SC_DOT_ROWS)]
            for k in range(0, ROW_WORDS // SC_LANES, 4):
                xs = [xbuf[bset, t, pl.ds((k + q) * SC_LANES, SC_LANES)] for q in range(4)]
                for i in range(SC_DOT_ROWS):
                    m = [_sc_mul_bf16(xs[q], uvbuf[slot, r0 + i, pl.ds((k + q) * SC_LANES, SC_LANES)])
                         for q in range(4)]
                    hi, lo = _sc_split_sum(m[0] + m[1], m[2] + m[3])
                    p = (k // 4) % SC_DOT_PARTIALS
                    accs[i][p] = accs[i][p] + (hi + lo)
            for i in range(SC_DOT_ROWS):
                mbuf[r0 + i, :] = functools.reduce(lambda a, b: a + b, accs[i])

        tot = zero
        for c in range(SC_LANES):
            tot = tot + plsc.load_gather(mbuf, [lane, jnp.full((SC_LANES,), c, I32)])
        return tot

    def accumulate(bset, t, slot):
        nv = SC_OWORDS // SC_LANES
        for oc in range(ROW_WORDS // SC_OWORDS):
            w0 = oc * SC_OWORDS
            accs = (tuple(obuf[bset, t, pl.ds(w0 + j * SC_LANES, SC_LANES)] for j in range(nv))
                    + tuple(obuf[bset, t, pl.ds(ROW_WORDS + w0 + j * SC_LANES, SC_LANES)] for j in range(nv)))

            def row_quad(rq, accs):
                r = 4 * rq
                ws = [plsc.load_gather(wbuf, [jnp.full((SC_LANES,), r + q, I32)]) for q in range(4)]
                his, los = [], []
                for j in range(nv):
                    m = [_sc_mul_bf16(ws[q], uvbuf[slot, r + q, pl.ds(ROW_WORDS + w0 + j * SC_LANES, SC_LANES)])
                         for q in range(4)]
                    hi, lo = _sc_split_sum(m[0] + m[1], m[2] + m[3])
                    his.append(accs[j] + hi)
                    los.append(accs[nv + j] + lo)
                return tuple(his) + tuple(los)

            accs = lax.fori_loop(0, SC_GROUP // 4, row_quad, accs)
            for j in range(nv):
                obuf[bset, t, pl.ds(w0 + j * SC_LANES, SC_LANES)] = accs[j]
                obuf[bset, t, pl.ds(ROW_WORDS + w0 + j * SC_LANES, SC_LANES)] = accs[nv + j]

    def pack_weights(w):
        bits = plsc.bitcast(w, I32)
        rounded = (bits + 0x7FFF + (lax.shift_right_logical(bits, jnp.int32(16)) & 1)) & HI_MASK
        return rounded | lax.shift_right_logical(rounded, jnp.int32(16))

    n_items = blk_tokens * SC_NGROUPS

    def block_rows(blk):
        return pl.ds(pl.multiple_of(wid * per_worker + blk * blk_tokens, blk_tokens), blk_tokens)

    def load_copies(blk, bset):
        rows = block_rows(blk)
        return (pltpu.make_async_copy(h2_hbm.at[rows], xbuf.at[bset], sems_in.at[bset]),
                pltpu.make_async_copy(eidx_hbm.at[rows], ibuf.at[bset], sems_in.at[bset]),
                pltpu.make_async_copy(gate_hbm.at[rows], gbuf.at[bset], sems_in.at[bset]))

    def store_copy(blk, bset):
        return pltpu.make_async_copy(obuf.at[bset], out_hbm.at[block_rows(blk)], sems_out.at[bset])

    for c in load_copies(0, 0):
        c.start()

    @pl.loop(0, n_blocks)
    def _(blk):
        bset = blk & 1
        for c in load_copies(blk, bset):
            c.wait()

        @pl.when(blk + 1 < n_blocks)
        def _():
            for c in load_copies(blk + 1, 1 - bset):
                c.start()

        @pl.when(blk >= 2)
        def _():
            store_copy(blk - 2, bset).wait()

        @pl.loop(0, blk_tokens)
        def _(t):
            zero = jnp.zeros((SC_LANES,), F32)
            for j in range(D_MODEL // SC_LANES):
                obuf[bset, t, pl.ds(j * SC_LANES, SC_LANES)] = zero

        for ahead in range(SC_SLOTS - 1):
            for c in gather_copies(bset, ahead, ahead):
                c.start()

        @pl.loop(0, n_items)
        def _(item):
            t, g = split_item(item)
            slot = item & (SC_SLOTS - 1)
            ahead = item + (SC_SLOTS - 1)

            @pl.when(ahead < n_items)
            def _():
                for c in gather_copies(bset, ahead, ahead & (SC_SLOTS - 1)):
                    c.start()

            for c in gather_copies(bset, item, slot):
                c.wait()
            a = dots(bset, t, slot)
            wbuf[...] = pack_weights(gbuf[bset, t, pl.ds(g * SC_GROUP, SC_GROUP)] * _sc_gelu(a))
            accumulate(bset, t, slot)

        store_copy(blk, bset).start()

    for blk in range(max(n_blocks - 2, 0), n_blocks):
        store_copy(blk, blk & 1).wait()


def _peer_experts(h2p, eidx, gates, uv_pack):
    n = h2p.shape[0]
    assert n % (8 * SC_WORKERS) == 0
    blk_tokens = math.gcd(n // SC_WORKERS, SC_TOKENS)
    mesh = plsc.VectorSubcoreMesh(core_axis_name="c", subcore_axis_name="s")
    fn = pl.kernel(
        _peer_sc_kernel,
        out_type=jax.ShapeDtypeStruct((n, D_MODEL), F32),
        mesh=mesh,
        scratch_types=[
            pltpu.VMEM((2, blk_tokens, ROW_WORDS), I32),
            pltpu.VMEM((2, blk_tokens, N_SEL), I32),
            pltpu.VMEM((2, blk_tokens, N_SEL), F32),
            pltpu.VMEM((2, blk_tokens, D_MODEL), F32),
            pltpu.VMEM((SC_SLOTS, SC_GROUP, 2 * ROW_WORDS), I32),
            pltpu.VMEM((SC_GROUP, SC_LANES), F32),
            pltpu.VMEM((SC_LANES,), I32),
            pltpu.SemaphoreType.DMA((SC_SLOTS,)), pltpu.SemaphoreType.DMA((2,)), pltpu.SemaphoreType.DMA((2,)),
        ],
        compiler_params=pltpu.CompilerParams(needs_layout_passes=False),
        cost_estimate=pl.CostEstimate(
            flops=4 * n * N_SEL * D_MODEL, transcendentals=n * N_SEL,
            bytes_accessed=4 * (2 * n * N_SEL * ROW_WORDS + n * ROW_WORDS + n * D_MODEL + 2 * n * N_SEL)),
        name="peer_experts",
    )
    return fn(h2p, eidx, gates, uv_pack)


def _peer_sc_scored_kernel(a_hbm, eidx_hbm, gate_hbm, v_hbm, out_hbm,
                           abuf, ibuf, gbuf, obuf, vbuf, wbuf, sems, sems_a):
    n = a_hbm.shape[0]
    per_worker = n // SC_WORKERS
    blk_tokens = ibuf.shape[0]
    wid = lax.axis_index("c") * SC_SUBCORES + lax.axis_index("s")

    def split_item(item):
        return lax.shift_right_logical(item, SC_NGROUPS.bit_length() - 1), item & (SC_NGROUPS - 1)

    def gather_copy(item, slot):
        t, g = split_item(item)
        idx = ibuf[t, pl.ds(g * SC_GROUP, SC_GROUP)]
        return pltpu.make_async_copy(v_hbm.at[idx], vbuf.at[slot], sems.at[slot])

    def accumulate(t, slot):
        nv = SC_OWORDS // SC_LANES
        for oc in range(ROW_WORDS // SC_OWORDS):
            w0 = oc * SC_OWORDS
            accs = (tuple(obuf[t, pl.ds(w0 + j * SC_LANES, SC_LANES)] for j in range(nv))
                    + tuple(obuf[t, pl.ds(ROW_WORDS + w0 + j * SC_LANES, SC_LANES)] for j in range(nv)))

            def row_quad(rq, accs):
                r = 4 * rq
                ws = [plsc.load_gather(wbuf, [jnp.full((SC_LANES,), r + q, I32)]) for q in range(4)]
                his, los = [], []
                for j in range(nv):
                    m = [_sc_mul_bf16(ws[q], vbuf[slot, r + q, pl.ds(w0 + j * SC_LANES, SC_LANES)])
                         for q in range(4)]
                    hi, lo = _sc_split_sum(m[0] + m[1], m[2] + m[3])
                    his.append(accs[j] + hi)
                    los.append(accs[nv + j] + lo)
                return tuple(his) + tuple(los)

            accs = lax.fori_loop(0, SC_GROUP // 4, row_quad, accs)
            for j in range(nv):
                obuf[t, pl.ds(w0 + j * SC_LANES, SC_LANES)] = accs[j]
                obuf[t, pl.ds(ROW_WORDS + w0 + j * SC_LANES, SC_LANES)] = accs[nv + j]

    def pack_weights(w):
        bits = plsc.bitcast(w, I32)
        rounded = (bits + 0x7FFF + (lax.shift_right_logical(bits, jnp.int32(16)) & 1)) & HI_MASK
        return rounded | lax.shift_right_logical(rounded, jnp.int32(16))

    n_items = blk_tokens * SC_NGROUPS

    @pl.loop(0, per_worker // blk_tokens)
    def _(blk):
        tok0 = pl.multiple_of(wid * per_worker + blk * blk_tokens, blk_tokens)

        def score_copy(t):
            return pltpu.make_async_copy(a_hbm.at[tok0 + t], abuf.at[t & 1], sems_a.at[t & 1])

        score_copy(0).start()
        pltpu.sync_copy(eidx_hbm.at[pl.ds(tok0, blk_tokens)], ibuf)
        pltpu.sync_copy(gate_hbm.at[pl.ds(tok0, blk_tokens)], gbuf)

        @pl.loop(0, blk_tokens)
        def _(t):
            zero = jnp.zeros((SC_LANES,), F32)
            for j in range(D_MODEL // SC_LANES):
                obuf[t, pl.ds(j * SC_LANES, SC_LANES)] = zero

        for ahead in range(SC_SLOTS - 1):
            gather_copy(ahead, ahead).start()

        @pl.loop(0, n_items)
        def _(item):
            t, g = split_item(item)
            slot = item & (SC_SLOTS - 1)
            ahead = item + (SC_SLOTS - 1)

            @pl.when(ahead < n_items)
            def _():
                gather_copy(ahead, ahead & (SC_SLOTS - 1)).start()

            @pl.when(g == 0)
            def _():
                score_copy(t).wait()

                @pl.when(t + 1 < blk_tokens)
                def _():
                    score_copy(t + 1).start()

            gather_copy(item, slot).wait()
            idx = ibuf[t, pl.ds(g * SC_GROUP, SC_GROUP)]
            half = SCORE_EXPERTS // 2
            word = plsc.load_gather(
                abuf, [jnp.full((SC_LANES,), t & 1, I32),
                       lax.shift_right_logical(idx, jnp.int32(SCORE_EXPERTS.bit_length() - 1)) * half
                       + (idx & (half - 1))])
            hi, lo = _sc_split(word)
            a = jnp.where((idx & half) == 0, hi, lo)
            wbuf[...] = pack_weights(gbuf[t, pl.ds(g * SC_GROUP, SC_GROUP)] * _sc_gelu(a))
            accumulate(t, slot)

        pltpu.sync_copy(obuf, out_hbm.at[pl.ds(tok0, blk_tokens)])


def _peer_experts_scored(scores, eidx, gates, v_pack):
    n, n_exp = scores.shape
    assert n % (8 * SC_WORKERS) == 0
    blk_tokens = math.gcd(n // SC_WORKERS, 2 * SC_TOKENS)
    mesh = plsc.VectorSubcoreMesh(core_axis_name="c", subcore_axis_name="s")
    fn = pl.kernel(
        _peer_sc_scored_kernel,
        out_type=jax.ShapeDtypeStruct((n, D_MODEL), F32),
        mesh=mesh,
        scratch_types=[
            pltpu.VMEM((2, n_exp), I32),
            pltpu.VMEM((blk_tokens, N_SEL), I32),
            pltpu.VMEM((blk_tokens, N_SEL), F32),
            pltpu.VMEM((blk_tokens, D_MODEL), F32),
            pltpu.VMEM((SC_SLOTS, SC_GROUP, ROW_WORDS), I32),
            pltpu.VMEM((SC_LANES,), I32),
            pltpu.SemaphoreType.DMA((SC_SLOTS,)), pltpu.SemaphoreType.DMA((2,)),
        ],
        compiler_params=pltpu.CompilerParams(needs_layout_passes=False),
        cost_estimate=pl.CostEstimate(
            flops=2 * n * N_SEL * D_MODEL, transcendentals=n * N_SEL,
            bytes_accessed=4 * (n * N_SEL * ROW_WORDS + n * n_exp + n * D_MODEL + 2 * n * N_SEL)),
        name="peer_experts_scored",
    )
    return fn(scores, eidx, gates, v_pack)


def _resid_kernel(x1_ref, p_ref, mod_ref, o_ref):
    o_ref[...] = x1_ref[...] + mod_ref[0, 5:6, :] * p_ref[...]


def _resid_call(x1, peer_out, seq, mod3, mod_row0):
    n = x1.shape[0]
    tm = TOKEN_TILE
    per_seq = seq // tm
    tok = pl.BlockSpec((tm, D_MODEL), lambda i: (i, 0))
    return pl.pallas_call(
        _resid_kernel, grid=(n // tm,),
        in_specs=[tok, tok,
                  pl.BlockSpec((1, 6, D_MODEL), lambda i: (mod_row0 + (i // per_seq if mod_row0 else 0), 0, 0))],
        out_specs=tok, out_shape=jax.ShapeDtypeStruct((n, D_MODEL), F32),
        compiler_params=pltpu.CompilerParams(dimension_semantics=("parallel",)),
        name="resid",
    )(x1, peer_out, mod3)


def _rope_tables(seq, rotate):
    if not rotate:
        return jnp.ones((seq, KV_WIDTH), F32), jnp.zeros((seq, KV_WIDTH), F32)
    quarter = A_DH // 4
    t = jnp.arange(seq)
    row = (t // GRID_W).astype(F32)
    col = (t % GRID_W).astype(F32)
    inv = ROPE_BASE ** (-jnp.arange(quarter, dtype=F32) / quarter)
    d = jnp.arange(A_DH)
    pos = jnp.where(d[None, :] < A_DH // 2, row[:, None], col[:, None])
    ang = pos * inv[d % quarter][None, :]
    sign = jnp.where((d % (A_DH // 2)) < quarter, -1.0, 1.0).astype(F32)
    cos = jnp.cos(ang)
    sin = jnp.sin(ang) * sign[None, :]
    return jnp.tile(cos, (1, KV_WIDTH // A_DH)), jnp.tile(sin, (1, KV_WIDTH // A_DH))


def _run_chunk(x, b0, batch, mod3, mod_row0, prm, cache, rotate, gate_on, dense_scores):
    (n1, n2, w_main, w_g, w_gt, b_g, b_gt, mhg, qg_t, kg_t, bd, sink, w_m, w_a, w_q, sub_a, sub_b,
     (uv_pack, v_pack, u_bf16)) = prm
    seq = x.shape[1]
    n = batch * seq
    x2d = x.reshape(x.shape[0] * seq, D_MODEL)
    tile0 = b0 * seq // TOKEN_TILE
    cos, sin = _rope_tables(seq, rotate)
    if gate_on is not None:
        (cos, sin), _ = lax.optimization_barrier(((cos, sin), gate_on))
    mq, mk, mv, mo, gcol, grow, aq, ak, av = _inproj_call(
        x2d, tile0, n, seq, mod3, mod_row0, n1, w_main, w_g, w_gt, b_g, b_gt, qg_t, kg_t, bd, cos, sin)
    kc, vc, c0, m0 = cache
    h_f, h_b, c_fin, m_fin = _mlstm_call(mq, mk, mv, gcol, grow, c0, m0, batch, seq)
    if kc is None:
        a_out = _attn_ctx_call(sink, aq, ak, av, batch, seq)
    else:
        a_out = _attn_lat_call(sink, aq, ak, av, kc, vc, batch, seq)
    x1, h2p, eidx, gates = _mix_call(x2d, tile0, seq, h_f, h_b, mo, a_out, mod3, mod_row0, mhg, n2, w_m, w_a,
                                     w_q, sub_a, sub_b)
    if dense_scores:
        peer_out = _peer_experts_scored(_scores_call(h2p, u_bf16), eidx, gates, v_pack)
    else:
        peer_out = _peer_experts(h2p, eidx, gates, uv_pack)
    y = _resid_call(x1, peer_out, seq, mod3, mod_row0).reshape(batch, seq, D_MODEL)
    return y, h2p, peer_out, ak, av, c_fin, m_fin


def _pack_state(C, n_vec, m):
    b = C.shape[0]
    caug = jnp.concatenate([C, jnp.broadcast_to(n_vec[..., None], C.shape)], axis=-1)
    caug = caug.reshape(b, 2 * M_HEADS, M_DH, 2 * M_DH)
    m_rep = jnp.broadcast_to(m.reshape(b, 2 * M_HEADS, 1, 1), (b, 2 * M_HEADS, 8, M_DH))
    return caug.astype(F32), m_rep.astype(F32)


def kernel(x_prompt, x_sample, c, cache_attn_k, cache_attn_v, state_mlstm_C, state_mlstm_n, state_mlstm_m,
           c_ctx, w_ada, b_ada, norm1_g, norm2_g, w_in, b_gates, mh_norm_g, q_norm_g, k_norm_g, sink_logits,
           w_out, peer_w_q, peer_sub_a, peer_sub_b, peer_u, peer_v):
    depth = w_ada.shape[0]
    assert depth == 1
    batch, seq, _ = x_prompt.shape
    dec_batch, dec_seq, _ = x_sample.shape
    assert dec_batch + 1 <= MOD_ROWS and batch % sum(CTX_CHUNK_WEIGHTS) == 0 and dec_batch % LATENT_CHUNKS == 0
    l = 0

    cond = jnp.concatenate([c_ctx[None, :], c, jnp.zeros((MOD_ROWS - 1 - dec_batch, D_MODEL), F32)], axis=0)
    mod3 = _ada_call(cond, w_ada[l], b_ada[l]).reshape(MOD_ROWS, 6, D_MODEL)

    wi = w_in[l]
    g0 = 4 * M_WIDTH
    w_main = jnp.concatenate([wi[:, :g0], wi[:, g0 + N_GATES:]], axis=1).astype(BF16)
    w_g = wi[:, g0:g0 + N_GATES]
    seg = jnp.arange(A_WIDTH) // A_DH
    bd = jnp.where(seg[:, None] == seg[None, :], 1.0 / A_DH, 0.0).astype(F32)
    prm = (norm1_g[l][None, :], norm2_g[l][None, :], w_main, w_g, w_g.T, b_gates[l][None, :], b_gates[l][:, None],
           mh_norm_g[l][None, :], jnp.tile(q_norm_g[l], A_HEADS)[None, :], jnp.tile(k_norm_g[l], A_KV)[None, :], bd,
           sink_logits[l], w_out[l][:M_WIDTH].astype(BF16), w_out[l][M_WIDTH:].astype(BF16),
           peer_w_q[l].astype(BF16), peer_sub_a[l].astype(BF16), peer_sub_b[l].astype(BF16),
           _pack_tables_call(peer_u[l], peer_v[l]))

    zeros_c = jnp.zeros((batch, 2, M_HEADS, M_DH, M_DH), F32)
    c0, m0 = _pack_state(zeros_c, zeros_c[..., 0], jnp.full((batch, 2, M_HEADS), NEG, F32))
    c0s, m0s = _pack_state(state_mlstm_C[:, l], state_mlstm_n[:, l], state_mlstm_m[:, l])
    past = cache_attn_k.shape[2]
    kc = cache_attn_k[:, l].reshape(dec_batch, past, KV_WIDTH)
    vc = cache_attn_v[:, l].reshape(dec_batch, past, KV_WIDTH)

    jobs = []
    ctx_sizes = [batch * f // sum(CTX_CHUNK_WEIGHTS) for f in CTX_CHUNK_WEIGHTS]
    assert sum(ctx_sizes) == batch
    b0 = 0
    for size in ctx_sizes:
        b1 = b0 + size
        jobs.append((x_prompt, b0, size, 0, (None, None, c0[b0:b1], m0[b0:b1]), False, False))
        b0 = b1
    for ci, b0 in enumerate(range(0, dec_batch, dec_batch // LATENT_CHUNKS)):
        b1 = b0 + dec_batch // LATENT_CHUNKS
        jobs.append((x_sample, b0, b1 - b0, 1 + b0, (kc[b0:b1], vc[b0:b1], c0s[b0:b1], m0s[b0:b1]), True,
                     ci in DENSE_LATENT_CHUNKS))
    outs = []
    for i, (x_all, b0, nb, mod_row0, cache, rotate, dense) in enumerate(jobs):
        gate = tuple(g for g in (outs[i - 1][1] if i >= 1 else None,
                                 outs[i - EXPERT_LAG][2] if i >= EXPERT_LAG else None) if g is not None)
        outs.append(_run_chunk(x_all, b0, nb, mod3, mod_row0, prm, cache, rotate, gate or None, dense))
    ctx, lat = outs[:len(ctx_sizes)], outs[len(ctx_sizes):]
    y_p = jnp.concatenate([o[0] for o in ctx], axis=0)
    y_s = jnp.concatenate([o[0] for o in lat], axis=0)
    k_new = jnp.concatenate([o[3] for o in ctx], axis=0)
    v_new = jnp.concatenate([o[4] for o in ctx], axis=0)
    c_fin = jnp.concatenate([o[5] for o in ctx], axis=0)
    m_fin = jnp.concatenate([o[6] for o in ctx], axis=0)

    c_fin = c_fin.reshape(batch, 2, M_HEADS, M_DH, 2 * M_DH)
    new_c = c_fin[..., :M_DH][:, None]
    new_n = c_fin[..., M_DH][:, None]
    new_m = m_fin[:, :, 0, 0].reshape(batch, 2, M_HEADS)[:, None]
    new_k = k_new.reshape(batch, 1, seq, A_KV, A_DH)
    new_v = v_new.reshape(batch, 1, seq, A_KV, A_DH)
    return y_p, y_s, new_k, new_v, new_c, new_n, new_m
```
